```python
import jax, jax.numpy as jnp
from jax import lax
import numpy as np

D_MODEL = 1024
BATCH = 8
SEQ = 8192
DEPTH = 2

A_WIDTH = D_MODEL // 2
A_HEADS = 8
CONV_WIDTH = 3
B_WIDTH = D_MODEL // 2
POOL_WINDOWS = (2, 4, 8, 16)
B_GROUPS = len(POOL_WINDOWS)
B_GROUP_DIM = B_WIDTH // B_GROUPS
EVEN_IN = 4 * A_WIDTH + 2 * B_WIDTH
EVEN_MIX = A_WIDTH + B_WIDTH
C_WIDTH = D_MODEL
C_HEADS = 8
C_HEAD_DIM = C_WIDTH // C_HEADS
CHUNK = 128
ODD_IN = 3 * C_WIDTH
N_EVEN = (DEPTH + 1) // 2
N_ODD = DEPTH // 2
EPS = 1e-6

kernel_name = "hybrid_conv_pool_gmlp_trunk"


def rmsnorm(x, g):
    xf = x.astype(jnp.float32)
    y = xf * lax.rsqrt(jnp.mean(xf * xf, axis=-1, keepdims=True) + EPS)
    return (y * g.astype(jnp.float32)).astype(x.dtype)


def layernorm(x, g, b):
    xf = x.astype(jnp.float32)
    mu = jnp.mean(xf, axis=-1, keepdims=True)
    var = jnp.mean(jnp.square(xf - mu), axis=-1, keepdims=True)
    y = (xf - mu) * lax.rsqrt(var + EPS)
    return (y * g.astype(jnp.float32) + b.astype(jnp.float32)).astype(x.dtype)


def short_gated_conv(xa, gb, gc, conv_w):
    S = xa.shape[1]
    h = gc * xa
    hp = jnp.pad(h, ((0, 0), (CONV_WIDTH - 1, 0), (0, 0)))
    conv = sum(conv_w[k] * hp[:, k:k + S] for k in range(CONV_WIDTH))
    return gb * conv


def multiscale_pool(xp, pool_w, pool_scale):
    Bsz, S, _ = xp.shape
    xf = xp.astype(jnp.float32)
    cs = jnp.cumsum(xf, axis=1)
    pos = jnp.arange(S)
    outs = []
    for g, w in enumerate(POOL_WINDOWS):
        sl = slice(g * B_GROUP_DIM, (g + 1) * B_GROUP_DIM)
        cs_g = cs[..., sl]
        lower = jnp.pad(cs_g, ((0, 0), (w, 0), (0, 0)))[:, :S]
        count = jnp.minimum(pos + 1, w).astype(jnp.float32)[None, :, None]
        outs.append((cs_g - lower) / count - xf[..., sl])
    pooled = jnp.stack(outs, axis=2).astype(xp.dtype)
    mixed = jnp.einsum('bsgc,gcd->bsgd', pooled, pool_w)
    return mixed.reshape(Bsz, S, B_WIDTH) * pool_scale


def even_layer(h, w_in, conv_w, pool_w, pool_scale, w_out):
    proj = h @ w_in
    xa, gb, gc, za, xp, zp = jnp.split(
        proj, np.cumsum([A_WIDTH] * 4 + [B_WIDTH]).tolist(), axis=-1)
    ya = short_gated_conv(xa, gb, gc, conv_w) * jax.nn.silu(za)
    yb = multiscale_pool(xp, pool_w, pool_scale) * jax.nn.silu(zp)
    return jnp.concatenate([ya, yb], axis=-1) @ w_out


def odd_layer(h, w_in, ln_g, ln_b, w_s, b_s, w_out):
    Bsz, S, _ = h.shape
    proj = h @ w_in
    u, v, z = jnp.split(proj, 3, axis=-1)
    v = layernorm(v, ln_g, ln_b)
    vc = v.reshape(Bsz, S // CHUNK, CHUNK, C_HEADS, C_HEAD_DIM)
    ws = jnp.tril(w_s)
    sv = jnp.einsum('hts,bnshc->bnthc', ws, vc) + b_s.T[None, None, :, :, None]
    y = u * sv.reshape(Bsz, S, C_WIDTH) * jax.nn.silu(z)
    return y @ w_out


def _fwd_setup_inputs(seed: int = 0) -> dict:
    key = jax.random.key(seed)
    ks = jax.random.split(key, 16)
    f32 = jnp.float32
    nrm = lambda k, shape, s: jax.random.normal(k, shape, f32) * s
    return {
        "x": nrm(ks[0], (BATCH, SEQ, D_MODEL), 1.0),
        "pre_norm": 1.0 + nrm(ks[1], (DEPTH, D_MODEL), 0.05),
        "post_norm": 1.0 + nrm(ks[2], (DEPTH, D_MODEL), 0.05),
        "even_w_in": nrm(ks[3], (N_EVEN, D_MODEL, EVEN_IN), D_MODEL ** -0.5),
        "even_conv_w": nrm(ks[4], (N_EVEN, CONV_WIDTH, A_WIDTH), CONV_WIDTH ** -0.5),
        "even_pool_w": nrm(ks[5], (N_EVEN, B_GROUPS, B_GROUP_DIM, B_GROUP_DIM), B_GROUP_DIM ** -0.5),
        "even_pool_scale": 1.0 + nrm(ks[6], (N_EVEN, B_WIDTH), 0.1),
        "even_w_out": nrm(ks[7], (N_EVEN, EVEN_MIX, D_MODEL), EVEN_MIX ** -0.5),
        "odd_w_in": nrm(ks[8], (N_ODD, D_MODEL, ODD_IN), D_MODEL ** -0.5),
        "odd_ln_g": 1.0 + nrm(ks[9], (N_ODD, C_WIDTH), 0.05),
        "odd_ln_b": nrm(ks[10], (N_ODD, C_WIDTH), 0.02),
        "odd_w_s": nrm(ks[11], (N_ODD, C_HEADS, CHUNK, CHUNK), CHUNK ** -0.5),
        "odd_b_s": 1.0 + nrm(ks[12], (N_ODD, C_HEADS, CHUNK), 0.1),
        "odd_w_out": nrm(ks[13], (N_ODD, C_WIDTH, D_MODEL), C_WIDTH ** -0.5),
    }


def _fwd_reference(x, pre_norm, post_norm, even_w_in, even_conv_w, even_pool_w,
              even_pool_scale, even_w_out, odd_w_in, odd_ln_g, odd_ln_b,
              odd_w_s, odd_b_s, odd_w_out):
    for i in range(DEPTH):
        h = rmsnorm(x, pre_norm[i])
        j = i // 2
        if i % 2 == 0:
            m = even_layer(h, even_w_in[j], even_conv_w[j], even_pool_w[j],
                           even_pool_scale[j], even_w_out[j])
        else:
            m = odd_layer(h, odd_w_in[j], odd_ln_g[j], odd_ln_b[j],
                          odd_w_s[j], odd_b_s[j], odd_w_out[j])
        x = x + rmsnorm(m, post_norm[i])
    return x


import jax as _jax
import jax.numpy as _jnp

TWIN_FORMAT = 'train_step'
FWD_PARAMS = ['x', 'pre_norm', 'post_norm', 'even_w_in', 'even_conv_w', 'even_pool_w', 'even_pool_scale', 'even_w_out', 'odd_w_in', 'odd_ln_g', 'odd_ln_b', 'odd_w_s', 'odd_b_s', 'odd_w_out']
TWIN_WEIGHTS = ['pre_norm', 'post_norm', 'even_w_in', 'even_conv_w', 'even_pool_w', 'even_pool_scale', 'even_w_out', 'odd_w_in', 'odd_ln_g', 'odd_ln_b', 'odd_w_s', 'odd_b_s', 'odd_w_out']
TWIN_DIFF_INPUT = 'x'
TWIN_INPUTS = ['x', 'pre_norm', 'post_norm', 'even_w_in', 'even_conv_w', 'even_pool_w', 'even_pool_scale', 'even_w_out', 'odd_w_in', 'odd_ln_g', 'odd_ln_b', 'odd_w_s', 'odd_b_s', 'odd_w_out', 'loss_target', 'm_pre_norm', 'm_post_norm', 'm_even_w_in', 'm_even_conv_w', 'm_even_pool_w', 'm_even_pool_scale', 'm_even_w_out', 'm_odd_w_in', 'm_odd_ln_g', 'm_odd_ln_b', 'm_odd_w_s', 'm_odd_b_s', 'm_odd_w_out', 'v_pre_norm', 'v_post_norm', 'v_even_w_in', 'v_even_conv_w', 'v_even_pool_w', 'v_even_pool_scale', 'v_even_w_out', 'v_odd_w_in', 'v_odd_ln_g', 'v_odd_ln_b', 'v_odd_w_s', 'v_odd_b_s', 'v_odd_w_out']
TWIN_OUTPUTS = ['loss', 'grad_x', 'grad_pre_norm', 'grad_post_norm', 'grad_even_w_in', 'grad_even_conv_w', 'grad_even_pool_w', 'grad_even_pool_scale', 'grad_even_w_out', 'grad_odd_w_in', 'grad_odd_ln_g', 'grad_odd_ln_b', 'grad_odd_w_s', 'grad_odd_b_s', 'grad_odd_w_out', 'delta_pre_norm', 'delta_post_norm', 'delta_even_w_in', 'delta_even_conv_w', 'delta_even_pool_w', 'delta_even_pool_scale', 'delta_even_w_out', 'delta_odd_w_in', 'delta_odd_ln_g', 'delta_odd_ln_b', 'delta_odd_w_s', 'delta_odd_b_s', 'delta_odd_w_out', 'new_m_pre_norm', 'new_m_post_norm', 'new_m_even_w_in', 'new_m_even_conv_w', 'new_m_even_pool_w', 'new_m_even_pool_scale', 'new_m_even_w_out', 'new_m_odd_w_in', 'new_m_odd_ln_g', 'new_m_odd_ln_b', 'new_m_odd_w_s', 'new_m_odd_b_s', 'new_m_odd_w_out', 'new_v_pre_norm', 'new_v_post_norm', 'new_v_even_w_in', 'new_v_even_conv_w', 'new_v_even_pool_w', 'new_v_even_pool_scale', 'new_v_even_w_out', 'new_v_odd_w_in', 'new_v_odd_ln_g', 'new_v_odd_ln_b', 'new_v_odd_w_s', 'new_v_odd_b_s', 'new_v_odd_w_out']
TWIN_LEAF_KINDS = {'loss': 'loss', 'grad_x': 'grad_x', 'grad_pre_norm': 'grad_w', 'grad_post_norm': 'grad_w', 'grad_even_w_in': 'grad_w', 'grad_even_conv_w': 'grad_w', 'grad_even_pool_w': 'grad_w', 'grad_even_pool_scale': 'grad_w', 'grad_even_w_out': 'grad_w', 'grad_odd_w_in': 'grad_w', 'grad_odd_ln_g': 'grad_w', 'grad_odd_ln_b': 'grad_w', 'grad_odd_w_s': 'grad_w', 'grad_odd_b_s': 'grad_w', 'grad_odd_w_out': 'grad_w', 'delta_pre_norm': 'delta_w', 'delta_post_norm': 'delta_w', 'delta_even_w_in': 'delta_w', 'delta_even_conv_w': 'delta_w', 'delta_even_pool_w': 'delta_w', 'delta_even_pool_scale': 'delta_w', 'delta_even_w_out': 'delta_w', 'delta_odd_w_in': 'delta_w', 'delta_odd_ln_g': 'delta_w', 'delta_odd_ln_b': 'delta_w', 'delta_odd_w_s': 'delta_w', 'delta_odd_b_s': 'delta_w', 'delta_odd_w_out': 'delta_w', 'new_m_pre_norm': 'new_m', 'new_m_post_norm': 'new_m', 'new_m_even_w_in': 'new_m', 'new_m_even_conv_w': 'new_m', 'new_m_even_pool_w': 'new_m', 'new_m_even_pool_scale': 'new_m', 'new_m_even_w_out': 'new_m', 'new_m_odd_w_in': 'new_m', 'new_m_odd_ln_g': 'new_m', 'new_m_odd_ln_b': 'new_m', 'new_m_odd_w_s': 'new_m', 'new_m_odd_b_s': 'new_m', 'new_m_odd_w_out': 'new_m', 'new_v_pre_norm': 'new_v', 'new_v_post_norm': 'new_v', 'new_v_even_w_in': 'new_v', 'new_v_even_conv_w': 'new_v', 'new_v_even_pool_w': 'new_v', 'new_v_even_pool_scale': 'new_v', 'new_v_even_w_out': 'new_v', 'new_v_odd_w_in': 'new_v', 'new_v_odd_ln_g': 'new_v', 'new_v_odd_ln_b': 'new_v', 'new_v_odd_w_s': 'new_v', 'new_v_odd_b_s': 'new_v', 'new_v_odd_w_out': 'new_v'}


def _forward(args):
    return _fwd_reference(*[args[k] for k in FWD_PARAMS])


def _output_shape():
    def fwd():
        inp = _fwd_setup_inputs(0)
        return _fwd_reference(*[inp[k] for k in FWD_PARAMS])
    out = _jax.eval_shape(fwd)
    return out.shape, out.dtype

N_MICROBATCH = 1
ADAM_LR = 0.001
ADAM_B1 = 0.9
ADAM_B2 = 0.999
ADAM_EPS = 1e-08
ADAM_WD = 0.01
ADAM_STEP = 10
PER_EXAMPLE_BATCH_AXIS = {'x': 0, 'loss_target': 0}
SHARED_INPUTS = []
_WEIGHT_DTYPES = {'pre_norm': _jnp.float32, 'post_norm': _jnp.float32, 'even_w_in': _jnp.float32, 'even_conv_w': _jnp.float32, 'even_pool_w': _jnp.float32, 'even_pool_scale': _jnp.float32, 'even_w_out': _jnp.float32, 'odd_w_in': _jnp.float32, 'odd_ln_g': _jnp.float32, 'odd_ln_b': _jnp.float32, 'odd_w_s': _jnp.float32, 'odd_b_s': _jnp.float32, 'odd_w_out': _jnp.float32}
MOMENT_SCALE = {'pre_norm': 1.061208e+00, 'post_norm': 6.412272e+01, 'even_w_in': 6.802736e-01, 'even_conv_w': 7.019645e-01, 'even_pool_w': 7.122543e-01, 'even_pool_scale': 7.789452e-01, 'even_w_out': 7.931397e-01, 'odd_w_in': 4.949623e-01, 'odd_ln_g': 2.456419e-01, 'odd_ln_b': 2.267630e-01, 'odd_w_s': 2.399945e-01, 'odd_b_s': 3.486155e-01, 'odd_w_out': 7.151990e-01}


def _to_microbatches(a, axis):
    t = _jnp.moveaxis(a, axis, 0)
    t = t.reshape((N_MICROBATCH, t.shape[0] // N_MICROBATCH) + t.shape[1:])
    return _jnp.moveaxis(t, 1, axis + 1)


def setup_inputs(seed: int = 0) -> dict:
    inp = _fwd_setup_inputs(seed)
    key = _jax.random.fold_in(_jax.random.key(seed), 7919)
    shape, _ = _output_shape()
    out = dict(inp)
    out["loss_target"] = _jax.random.normal(_jax.random.fold_in(key, 0), shape, _jnp.float32)
    for i, name in enumerate(TWIN_WEIGHTS):
        w = inp[name].astype(_jnp.float32)
        if MOMENT_SCALE is None:
            s = _jnp.sqrt(_jnp.mean(_jnp.square(w)) + 1e-30)
        else:
            s = MOMENT_SCALE[name]
        km, kv = _jax.random.split(_jax.random.fold_in(key, i + 1))
        out[name] = w
        out["m_" + name] = s * _jax.random.normal(km, w.shape, _jnp.float32)
        out["v_" + name] = (s * s) * _jax.random.uniform(kv, w.shape, _jnp.float32, 0.5, 1.5)
    if N_MICROBATCH > 1:
        for name, axis in PER_EXAMPLE_BATCH_AXIS.items():
            out[name] = _to_microbatches(out[name], axis)
    return {'x': out['x'], 'pre_norm': out['pre_norm'], 'post_norm': out['post_norm'], 'even_w_in': out['even_w_in'], 'even_conv_w': out['even_conv_w'], 'even_pool_w': out['even_pool_w'], 'even_pool_scale': out['even_pool_scale'], 'even_w_out': out['even_w_out'], 'odd_w_in': out['odd_w_in'], 'odd_ln_g': out['odd_ln_g'], 'odd_ln_b': out['odd_ln_b'], 'odd_w_s': out['odd_w_s'], 'odd_b_s': out['odd_b_s'], 'odd_w_out': out['odd_w_out'], 'loss_target': out['loss_target'], 'm_pre_norm': out['m_pre_norm'], 'm_post_norm': out['m_post_norm'], 'm_even_w_in': out['m_even_w_in'], 'm_even_conv_w': out['m_even_conv_w'], 'm_even_pool_w': out['m_even_pool_w'], 'm_even_pool_scale': out['m_even_pool_scale'], 'm_even_w_out': out['m_even_w_out'], 'm_odd_w_in': out['m_odd_w_in'], 'm_odd_ln_g': out['m_odd_ln_g'], 'm_odd_ln_b': out['m_odd_ln_b'], 'm_odd_w_s': out['m_odd_w_s'], 'm_odd_b_s': out['m_odd_b_s'], 'm_odd_w_out': out['m_odd_w_out'], 'v_pre_norm': out['v_pre_norm'], 'v_post_norm': out['v_post_norm'], 'v_even_w_in': out['v_even_w_in'], 'v_even_conv_w': out['v_even_conv_w'], 'v_even_pool_w': out['v_even_pool_w'], 'v_even_pool_scale': out['v_even_pool_scale'], 'v_even_w_out': out['v_even_w_out'], 'v_odd_w_in': out['v_odd_w_in'], 'v_odd_ln_g': out['v_odd_ln_g'], 'v_odd_ln_b': out['v_odd_ln_b'], 'v_odd_w_s': out['v_odd_w_s'], 'v_odd_b_s': out['v_odd_b_s'], 'v_odd_w_out': out['v_odd_w_out']}


def _loss(weights, diff, rest, loss_target):
    with _jax.named_scope("forward"):
        args = {**rest, TWIN_DIFF_INPUT: diff, **{k: w.astype(_WEIGHT_DTYPES[k]) for k, w in weights.items()}}
        y = _forward(args)
    with _jax.named_scope("loss_head"):
        err = _jnp.square(y.astype(_jnp.float32) - loss_target)
        return 0.5 * _jnp.sum(_jnp.mean(err, axis=-1)) if err.ndim else 0.5 * err


def _adamw(w, g, m, v):
    m = ADAM_B1 * m + (1.0 - ADAM_B1) * g
    v = ADAM_B2 * v + (1.0 - ADAM_B2) * _jnp.square(g)
    m_hat = m / (1.0 - ADAM_B1 ** ADAM_STEP)
    v_hat = v / (1.0 - ADAM_B2 ** ADAM_STEP)
    delta = -ADAM_LR * (m_hat / (_jnp.sqrt(v_hat) + ADAM_EPS) + ADAM_WD * w)
    return delta, m, v


def reference(x, pre_norm, post_norm, even_w_in, even_conv_w, even_pool_w, even_pool_scale, even_w_out, odd_w_in, odd_ln_g, odd_ln_b, odd_w_s, odd_b_s, odd_w_out, loss_target, m_pre_norm, m_post_norm, m_even_w_in, m_even_conv_w, m_even_pool_w, m_even_pool_scale, m_even_w_out, m_odd_w_in, m_odd_ln_g, m_odd_ln_b, m_odd_w_s, m_odd_b_s, m_odd_w_out, v_pre_norm, v_post_norm, v_even_w_in, v_even_conv_w, v_even_pool_w, v_even_pool_scale, v_even_w_out, v_odd_w_in, v_odd_ln_g, v_odd_ln_b, v_odd_w_s, v_odd_b_s, v_odd_w_out):
    given = dict(x=x, pre_norm=pre_norm, post_norm=post_norm, even_w_in=even_w_in, even_conv_w=even_conv_w, even_pool_w=even_pool_w, even_pool_scale=even_pool_scale, even_w_out=even_w_out, odd_w_in=odd_w_in, odd_ln_g=odd_ln_g, odd_ln_b=odd_ln_b, odd_w_s=odd_w_s, odd_b_s=odd_b_s, odd_w_out=odd_w_out, loss_target=loss_target, m_pre_norm=m_pre_norm, m_post_norm=m_post_norm, m_even_w_in=m_even_w_in, m_even_conv_w=m_even_conv_w, m_even_pool_w=m_even_pool_w, m_even_pool_scale=m_even_pool_scale, m_even_w_out=m_even_w_out, m_odd_w_in=m_odd_w_in, m_odd_ln_g=m_odd_ln_g, m_odd_ln_b=m_odd_ln_b, m_odd_w_s=m_odd_w_s, m_odd_b_s=m_odd_b_s, m_odd_w_out=m_odd_w_out, v_pre_norm=v_pre_norm, v_post_norm=v_post_norm, v_even_w_in=v_even_w_in, v_even_conv_w=v_even_conv_w, v_even_pool_w=v_even_pool_w, v_even_pool_scale=v_even_pool_scale, v_even_w_out=v_even_w_out, v_odd_w_in=v_odd_w_in, v_odd_ln_g=v_odd_ln_g, v_odd_ln_b=v_odd_ln_b, v_odd_w_s=v_odd_w_s, v_odd_b_s=v_odd_b_s, v_odd_w_out=v_odd_w_out)
    weights = {n: given[n] for n in TWIN_WEIGHTS}
    shared = {n: given[n] for n in SHARED_INPUTS}
    per_example = {n: given[n] for n in ['x']}
    grad_fn = _jax.value_and_grad(_loss, argnums=(0, 1))

    def one_microbatch(ex, loss_target):
        ex = dict(ex)
        diff = ex.pop(TWIN_DIFF_INPUT)
        return grad_fn(weights, diff, {**shared, **ex}, loss_target)

    if N_MICROBATCH == 1:
        loss, (grad_w, grad_x) = one_microbatch(per_example, given["loss_target"])
    else:
        def body(carry, xs):
            loss_sum, grad_sum = carry
            l_k, (gw_k, gx_k) = one_microbatch(xs[0], xs[1])
            with _jax.named_scope("update"):
                return (loss_sum + l_k, _jax.tree.map(_jnp.add, grad_sum, gw_k)), gx_k

        init = (_jnp.zeros((), _jnp.float32), _jax.tree.map(_jnp.zeros_like, weights))
        (loss, grad_w), grad_x = _jax.lax.scan(body, init, (per_example, given["loss_target"]))
    with _jax.named_scope("update"):
        delta_w, new_m, new_v = {}, {}, {}
        for n in TWIN_WEIGHTS:
            delta_w[n], new_m[n], new_v[n] = _adamw(weights[n], grad_w[n], given["m_" + n], given["v_" + n])
    return (loss, grad_x, *[grad_w[n] for n in TWIN_WEIGHTS], *[delta_w[n] for n in TWIN_WEIGHTS],
            *[new_m[n] for n in TWIN_WEIGHTS], *[new_v[n] for n in TWIN_WEIGHTS])
```

```python
import functools

import jax
import jax.numpy as jnp
from jax import lax
from jax.experimental import pallas as pl
from jax.experimental.pallas import tpu as pltpu

F32 = jnp.float32
BF16 = jnp.bfloat16
MESH = pl.DeviceIdType.MESH

D = 1024
EPS = 1e-6
A_W = 512
B_W = 512
POOL_WINDOWS = (2, 4, 8, 16)
GROUP = 128
N_GROUPS = 4
N_HEADS = 8
HALO = 16
PROJ = 3072
N_DEV = 8
SHARD_IN = PROJ // N_DEV
SHARD_OUT = D // N_DEV

ADAM_LR = 0.001
ADAM_B1 = 0.9
ADAM_B2 = 0.999
ADAM_EPS = 1e-08
ADAM_WD = 0.01
ADAM_STEP = 10

VMEM_LIMIT = 56 * 1024 * 1024

REP_ROWS = 1600
REP_SLICE = REP_ROWS // N_DEV
SMALL_SHARD_ROWS = 16


def _params(n_grid=1, vmem=VMEM_LIMIT):
    return pltpu.CompilerParams(dimension_semantics=("arbitrary",) * n_grid, vmem_limit_bytes=vmem)


def _full(shape):
    return pl.BlockSpec(shape, lambda *_: (0,) * len(shape))


def _sigmoid(z):
    return jax.nn.sigmoid(z)


def _dsilu(z, s):
    return s * (1.0 + z * (1.0 - s))


def _dot(a, b):
    return jnp.dot(a, b, preferred_element_type=F32)


def _dot_t0(a, b):
    return lax.dot_general(a, b, (((0,), (0,)), ((), ())), preferred_element_type=F32)


def _dot_t1(a, b):
    return lax.dot_general(a, b, (((1,), (1,)), ((), ())), preferred_element_type=F32)


def _rms(x):
    return lax.rsqrt(jnp.mean(x * x, axis=-1, keepdims=True) + EPS)


def _shift_down(a, k):
    return pltpu.roll(a, k, 0)


def _shift_up(a, k):
    return pltpu.roll(a, a.shape[0] - k, 0)


def _norm_matmul(x, g, w, ts, name):
    S = x.shape[0]
    N = w.shape[1]

    def body(x_ref, g_ref, w_ref, p_ref, h_ref):
        xv = x_ref[...]
        h = ((xv * _rms(xv)) * g_ref[...]).astype(BF16)
        h_ref[...] = h
        p_ref[...] = _dot(h, w_ref[...]).astype(BF16)

    return pl.pallas_call(
        body, name=name, grid=(S // ts,),
        in_specs=[pl.BlockSpec((ts, D), lambda i: (i, 0)), _full((1, D)), _full((D, N))],
        out_specs=[pl.BlockSpec((ts, N), lambda i: (i, 0)), pl.BlockSpec((ts, D), lambda i: (i, 0))],
        out_shape=[jax.ShapeDtypeStruct((S, N), BF16), jax.ShapeDtypeStruct((S, D), BF16)],
        compiler_params=_params(),
    )(x, g, w)


def _even_forward_parts(p_ref, halo_ref, is_first, row0, cw_ref, pw_ref, ts):
    f = lambda a: a.astype(F32)
    xa = f(p_ref[:, 0:512])
    gb = f(p_ref[:, 512:1024])
    gc = f(p_ref[:, 1024:1536])
    za = f(p_ref[:, 1536:2048])
    xp = f(p_ref[:, 2048:2560])
    zp = f(p_ref[:, 2560:3072])
    keep = jnp.where(is_first, 0.0, 1.0).astype(F32)
    ha = gc * xa
    ha_halo = f(halo_ref[:, 1024:1536]) * f(halo_ref[:, 0:512]) * keep
    xp_halo = f(halo_ref[:, 2048:2560]) * keep
    ha_ext = jnp.concatenate([ha_halo, ha], axis=0)
    ha_m1 = _shift_down(ha_ext, 1)[HALO:]
    ha_m2 = _shift_down(ha_ext, 2)[HALO:]
    conv = cw_ref[2:3, :] * ha + cw_ref[1:2, :] * ha_m1 + cw_ref[0:1, :] * ha_m2
    sig_a = _sigmoid(za)
    silu_a = za * sig_a

    xp_ext = jnp.concatenate([xp_halo, xp], axis=0)
    pos = row0 + lax.broadcasted_iota(jnp.int32, (ts, 1), 0)
    pooled, inv_cnt, mixed = [], [], []
    for g, w in enumerate(POOL_WINDOWS):
        cols = slice(g * GROUP, (g + 1) * GROUP)
        s = xp_ext[:, cols]
        k = 1
        while k < w:
            s = s + _shift_down(s, k)
            k *= 2
        inv = 1.0 / jnp.minimum(pos + 1, w).astype(F32)
        pg = s[HALO:] * inv - xp[:, cols]
        pooled.append(pg)
        inv_cnt.append(inv)
        mixed.append(_dot(pg.astype(BF16), pw_ref[g]))
    mixed = jnp.concatenate(mixed, axis=1)
    sig_b = _sigmoid(zp)
    silu_b = zp * sig_b
    return dict(xa=xa, gb=gb, gc=gc, za=za, xp=xp, zp=zp, ha=ha, ha_m1=ha_m1, ha_m2=ha_m2, conv=conv,
                sig_a=sig_a, silu_a=silu_a, pooled=pooled, inv_cnt=inv_cnt, mixed=mixed, sig_b=sig_b,
                silu_b=silu_b)


def _halo_index(ts):
    blocks_per_tile = ts // HALO
    return lambda i: (jnp.maximum(i * blocks_per_tile - 1, 0), 0)


def _even_forward(p, x, cw, pw_b, ps, wout_b, gpost, ts):
    S = x.shape[0]

    def body(p_ref, halo_ref, x_ref, cw_ref, pw_ref, ps_ref, wo_ref, gp_ref, m_ref, x1_ref, mix_ref):
        i = pl.program_id(0)
        fw = _even_forward_parts(p_ref, halo_ref, i == 0, i * ts, cw_ref, pw_ref, ts)
        mix_ref[:, 0:A_W] = (fw["gb"] * fw["conv"] * fw["silu_a"]).astype(BF16)
        mix_ref[:, A_W:D] = (fw["mixed"] * ps_ref[...] * fw["silu_b"]).astype(BF16)
        m = _dot(mix_ref[...], wo_ref[...])
        m_ref[...] = m
        x1_ref[...] = x_ref[...] + (m * _rms(m)) * gp_ref[...]

    tile = pl.BlockSpec((ts, D), lambda i: (i, 0))
    return pl.pallas_call(
        body, name="even_forward", grid=(S // ts,),
        in_specs=[pl.BlockSpec((ts, PROJ), lambda i: (i, 0)), pl.BlockSpec((HALO, PROJ), _halo_index(ts)), tile,
                  _full((3, A_W)), _full((N_GROUPS, GROUP, GROUP)), _full((1, B_W)), _full((D, D)), _full((1, D))],
        out_specs=[tile, tile],
        out_shape=[jax.ShapeDtypeStruct((S, D), F32), jax.ShapeDtypeStruct((S, D), F32)],
        scratch_shapes=[pltpu.VMEM((ts, D), BF16)],
        compiler_params=_params(),
    )(p, p, x, cw, pw_b, ps, wout_b, gpost)


def _even_backward(g1, m0, p, cw, pw_b, pwt_b, ps, woutt_b, gpost, ts):
    S = g1.shape[0]
    nt = S // ts

    def body(g_ref, m_ref, p_ref, halo_ref, cw_ref, pw_ref, pwt_ref, ps_ref, wot_ref, gp_ref,
             dp_ref, dwo_ref, dcw_ref, dpw_ref, dps_ref, dgp_ref, carry_ref, mix_ref, dm_ref):
        step = pl.program_id(0)
        i = nt - 1 - step

        @pl.when(step == 0)
        def _():
            carry_ref[...] = jnp.zeros_like(carry_ref)
            dwo_ref[...] = jnp.zeros_like(dwo_ref)
            dcw_ref[...] = jnp.zeros_like(dcw_ref)
            dpw_ref[...] = jnp.zeros_like(dpw_ref)
            dps_ref[...] = jnp.zeros_like(dps_ref)
            dgp_ref[...] = jnp.zeros_like(dgp_ref)

        g = g_ref[...]
        m = m_ref[...]
        q = _rms(m)
        n = m * q
        dgp_ref[...] += jnp.sum(g * n, axis=0, keepdims=True)
        dn = g * gp_ref[...]
        dm = q * (dn - n * jnp.mean(dn * n, axis=-1, keepdims=True))
        dm_ref[...] = dm.astype(BF16)
        dmix = _dot(dm_ref[...], wot_ref[...])
        dya = dmix[:, 0:A_W]
        dyb = dmix[:, A_W:D]

        fw = _even_forward_parts(p_ref, halo_ref, i == 0, i * ts, cw_ref, pw_ref, ts)
        ps_v = ps_ref[...]
        mix_ref[:, 0:A_W] = (fw["gb"] * fw["conv"] * fw["silu_a"]).astype(BF16)
        mix_ref[:, A_W:D] = (fw["mixed"] * ps_v * fw["silu_b"]).astype(BF16)
        dwo_ref[...] += _dot_t0(mix_ref[...], dm_ref[...])

        t = dya * fw["gb"]
        dconv = t * fw["silu_a"]
        dgb = dya * fw["conv"] * fw["silu_a"]
        dza = t * fw["conv"] * _dsilu(fw["za"], fw["sig_a"])
        dcw_ref[2:3, :] += jnp.sum(dconv * fw["ha"], axis=0, keepdims=True)
        dcw_ref[1:2, :] += jnp.sum(dconv * fw["ha_m1"], axis=0, keepdims=True)
        dcw_ref[0:1, :] += jnp.sum(dconv * fw["ha_m2"], axis=0, keepdims=True)
        dconv_ext = jnp.concatenate([dconv, carry_ref[:, 0:A_W]], axis=0)
        dha = (cw_ref[2:3, :] * dconv + cw_ref[1:2, :] * _shift_up(dconv_ext, 1)[:ts]
               + cw_ref[0:1, :] * _shift_up(dconv_ext, 2)[:ts])
        dgc = dha * fw["xa"]
        dxa = dha * fw["gc"]

        u = dyb * fw["mixed"]
        dps_ref[...] += jnp.sum(u * fw["silu_b"], axis=0, keepdims=True)
        dzp = u * ps_v * _dsilu(fw["zp"], fw["sig_b"])
        dmixed = (dyb * ps_v * fw["silu_b"]).astype(BF16)
        dxp, e_first = [], []
        for gi, w in enumerate(POOL_WINDOWS):
            cols = slice(gi * GROUP, (gi + 1) * GROUP)
            dmg = dmixed[:, cols]
            dpooled = _dot(dmg, pwt_ref[gi])
            dpw_ref[gi] += _dot_t0(fw["pooled"][gi].astype(BF16), dmg)
            e = dpooled * fw["inv_cnt"][gi]
            e_first.append(e[0:HALO])
            s = jnp.concatenate([e, carry_ref[:, A_W + gi * GROUP:A_W + (gi + 1) * GROUP]], axis=0)
            k = 1
            while k < w:
                s = s + _shift_up(s, k)
                k *= 2
            dxp.append(s[:ts] - dpooled)

        carry_ref[:, 0:A_W] = dconv[0:HALO]
        carry_ref[:, A_W:D] = jnp.concatenate(e_first, axis=1)

        dp_ref[:, 0:512] = dxa.astype(BF16)
        dp_ref[:, 512:1024] = dgb.astype(BF16)
        dp_ref[:, 1024:1536] = dgc.astype(BF16)
        dp_ref[:, 1536:2048] = dza.astype(BF16)
        dp_ref[:, 2048:2560] = jnp.concatenate(dxp, axis=1).astype(BF16)
        dp_ref[:, 2560:3072] = dzp.astype(BF16)

    rev = lambda s: (nt - 1 - s, 0)
    tile = pl.BlockSpec((ts, D), rev)
    bpt = ts // HALO
    halo_map = lambda s: (jnp.maximum((nt - 1 - s) * bpt - 1, 0), 0)
    return pl.pallas_call(
        body, name="even_backward", grid=(nt,),
        in_specs=[tile, tile, pl.BlockSpec((ts, PROJ), rev), pl.BlockSpec((HALO, PROJ), halo_map),
                  _full((3, A_W)), _full((N_GROUPS, GROUP, GROUP)), _full((N_GROUPS, GROUP, GROUP)),
                  _full((1, B_W)), _full((D, D)), _full((1, D))],
        out_specs=[pl.BlockSpec((ts, PROJ), rev), _full((D, D)), _full((8, A_W)),
                   _full((N_GROUPS, GROUP, GROUP)), _full((1, B_W)), _full((1, D))],
        out_shape=[jax.ShapeDtypeStruct((S, PROJ), BF16), jax.ShapeDtypeStruct((D, D), F32),
                   jax.ShapeDtypeStruct((8, A_W), F32), jax.ShapeDtypeStruct((N_GROUPS, GROUP, GROUP), F32),
                   jax.ShapeDtypeStruct((1, B_W), F32), jax.ShapeDtypeStruct((1, D), F32)],
        scratch_shapes=[pltpu.VMEM((HALO, D), F32), pltpu.VMEM((ts, D), BF16), pltpu.VMEM((ts, D), BF16)],
        compiler_params=_params(),
    )(g1, m0, p, p, cw, pw_b, pwt_b, ps, woutt_b, gpost)


def _odd_forward_parts(p_ref, lg_ref, lb_ref, wt_ref, bfull_ref, vln_ref, sv_ref, ts):
    f = lambda a: a.astype(F32)
    u = f(p_ref[:, 0:D])
    v = f(p_ref[:, D:2 * D])
    z = f(p_ref[:, 2 * D:3 * D])
    mu = jnp.mean(v, axis=-1, keepdims=True)
    vc = v - mu
    rstd = lax.rsqrt(jnp.mean(vc * vc, axis=-1, keepdims=True) + EPS)
    vh = vc * rstd
    vln_ref[...] = (vh * lg_ref[...] + lb_ref[...]).astype(BF16)
    for c in range(ts // GROUP):
        rows = slice(c * GROUP, (c + 1) * GROUP)
        for h in range(N_HEADS):
            cols = slice(h * GROUP, (h + 1) * GROUP)
            sv_ref[rows, cols] = _dot(wt_ref[h], vln_ref[rows, cols]) + bfull_ref[h]
    return u, z, vh, rstd


def _odd_forward(p, x1, tgt, lg, lb, wt_b, bfull, wout_b, gpost, ts):
    S = x1.shape[0]

    def body(p_ref, x_ref, t_ref, lg_ref, lb_ref, wt_ref, bfull_ref, wo_ref, gp_ref,
             m_ref, g_ref, loss_ref, vln_ref, sv_ref, y_ref):
        @pl.when(pl.program_id(0) == 0)
        def _():
            loss_ref[...] = jnp.zeros_like(loss_ref)

        u, z, _, _ = _odd_forward_parts(p_ref, lg_ref, lb_ref, wt_ref, bfull_ref, vln_ref, sv_ref, ts)
        y_ref[...] = (u * sv_ref[...] * (z * _sigmoid(z))).astype(BF16)
        m = _dot(y_ref[...], wo_ref[...])
        m_ref[...] = m
        x2 = x_ref[...] + (m * _rms(m)) * gp_ref[...]
        err = x2 - t_ref[...]
        g_ref[...] = err * (1.0 / D)
        loss_ref[...] += jnp.sum(err * err, axis=0, keepdims=True)

    tile = pl.BlockSpec((ts, D), lambda i: (i, 0))
    small = _full((N_HEADS, GROUP, GROUP))
    return pl.pallas_call(
        body, name="odd_forward", grid=(S // ts,),
        in_specs=[pl.BlockSpec((ts, PROJ), lambda i: (i, 0)), tile, tile, _full((1, D)), _full((1, D)),
                  small, small, _full((D, D)), _full((1, D))],
        out_specs=[tile, tile, _full((1, D))],
        out_shape=[jax.ShapeDtypeStruct((S, D), F32), jax.ShapeDtypeStruct((S, D), F32),
                   jax.ShapeDtypeStruct((1, D), F32)],
        scratch_shapes=[pltpu.VMEM((ts, D), BF16), pltpu.VMEM((ts, D), F32), pltpu.VMEM((ts, D), BF16)],
        compiler_params=_params(),
    )(p, x1, tgt, lg, lb, wt_b, bfull, wout_b, gpost)


def _odd_backward(g2, m1, p, lg, lb, wt_b, wtt_b, bfull, woutt_b, gpost, ts):
    S = g2.shape[0]
    nt = S // ts

    def body(g_ref, m_ref, p_ref, lg_ref, lb_ref, wt_ref, wtt_ref, bfull_ref, wot_ref, gp_ref,
             dp_ref, dwo_ref, dws_ref, dbs_ref, dlg_ref, dlb_ref, dgp_ref,
             vln_ref, sv_ref, y_ref, dm_ref, dsv_ref, dsvb_ref, dvln_ref, dsum_ref):
        step = pl.program_id(0)

        @pl.when(step == 0)
        def _():
            dwo_ref[...] = jnp.zeros_like(dwo_ref)
            dws_ref[...] = jnp.zeros_like(dws_ref)
            dsum_ref[...] = jnp.zeros_like(dsum_ref)
            dlg_ref[...] = jnp.zeros_like(dlg_ref)
            dlb_ref[...] = jnp.zeros_like(dlb_ref)
            dgp_ref[...] = jnp.zeros_like(dgp_ref)

        g = g_ref[...]
        m = m_ref[...]
        q = _rms(m)
        n = m * q
        dgp_ref[...] += jnp.sum(g * n, axis=0, keepdims=True)
        dn = g * gp_ref[...]
        dm = q * (dn - n * jnp.mean(dn * n, axis=-1, keepdims=True))
        dm_ref[...] = dm.astype(BF16)
        dy = _dot(dm_ref[...], wot_ref[...])

        u, z, vh, rstd = _odd_forward_parts(p_ref, lg_ref, lb_ref, wt_ref, bfull_ref, vln_ref, sv_ref, ts)
        sv = sv_ref[...]
        sig = _sigmoid(z)
        sz = z * sig
        y_ref[...] = (u * sv * sz).astype(BF16)
        dwo_ref[...] += _dot_t0(y_ref[...], dm_ref[...])

        t = dy * sz
        du = t * sv
        dsv = t * u
        dz = dy * u * sv * _dsilu(z, sig)
        dsv_ref[...] = dsv
        dsvb_ref[...] = dsv.astype(BF16)

        tril = (lax.broadcasted_iota(jnp.int32, (GROUP, GROUP), 0)
                >= lax.broadcasted_iota(jnp.int32, (GROUP, GROUP), 1))
        for c in range(ts // GROUP):
            rows = slice(c * GROUP, (c + 1) * GROUP)
            dsum_ref[...] += dsv_ref[rows, :]
            for h in range(N_HEADS):
                cols = slice(h * GROUP, (h + 1) * GROUP)
                dvln_ref[rows, cols] = _dot(wtt_ref[h], dsvb_ref[rows, cols])
                dws_ref[h] += jnp.where(tril, _dot_t1(dsvb_ref[rows, cols], vln_ref[rows, cols]), 0.0)

        dvln = dvln_ref[...]
        dlg_ref[...] += jnp.sum(dvln * vh, axis=0, keepdims=True)
        dlb_ref[...] += jnp.sum(dvln, axis=0, keepdims=True)
        dvh = dvln * lg_ref[...]
        dv = rstd * (dvh - jnp.mean(dvh, axis=-1, keepdims=True)
                     - vh * jnp.mean(dvh * vh, axis=-1, keepdims=True))
        dp_ref[:, 0:D] = du.astype(BF16)
        dp_ref[:, D:2 * D] = dv.astype(BF16)
        dp_ref[:, 2 * D:3 * D] = dz.astype(BF16)

        @pl.when(step == nt - 1)
        def _():
            ones = jnp.ones((8, GROUP), F32)
            for h in range(N_HEADS):
                cols = slice(h * GROUP, (h + 1) * GROUP)
                sums = lax.dot_general(ones, dsum_ref[:, cols], (((1,), (1,)), ((), ())),
                                       precision=lax.Precision.HIGHEST, preferred_element_type=F32)
                dbs_ref[h:h + 1, :] = sums[0:1, :]

    tile = pl.BlockSpec((ts, D), lambda i: (i, 0))
    small = _full((N_HEADS, GROUP, GROUP))
    vec = _full((1, D))
    return pl.pallas_call(
        body, name="odd_backward", grid=(nt,),
        in_specs=[tile, tile, pl.BlockSpec((ts, PROJ), lambda i: (i, 0)), vec, vec, small, small, small,
                  _full((D, D)), vec],
        out_specs=[pl.BlockSpec((ts, PROJ), lambda i: (i, 0)), _full((D, D)), small, _full((N_HEADS, GROUP)),
                   vec, vec, vec],
        out_shape=[jax.ShapeDtypeStruct((S, PROJ), BF16), jax.ShapeDtypeStruct((D, D), F32),
                   jax.ShapeDtypeStruct((N_HEADS, GROUP, GROUP), F32), jax.ShapeDtypeStruct((N_HEADS, GROUP), F32),
                   jax.ShapeDtypeStruct((1, D), F32), jax.ShapeDtypeStruct((1, D), F32),
                   jax.ShapeDtypeStruct((1, D), F32)],
        scratch_shapes=[pltpu.VMEM((ts, D), BF16), pltpu.VMEM((ts, D), F32), pltpu.VMEM((ts, D), BF16),
                        pltpu.VMEM((ts, D), BF16), pltpu.VMEM((ts, D), F32), pltpu.VMEM((ts, D), BF16),
                        pltpu.VMEM((ts, D), F32), pltpu.VMEM((GROUP, D), F32)],
        compiler_params=_params(),
    )(g2, m1, p, lg, lb, wt_b, wtt_b, bfull, woutt_b, gpost)


def _input_backward(dp, wt_b, x, g_res, gpre, ts, name):
    S = x.shape[0]

    def body(dp_ref, wt_ref, x_ref, g_ref, gp_ref, dx_ref, dgp_ref):
        @pl.when(pl.program_id(0) == 0)
        def _():
            dgp_ref[...] = jnp.zeros_like(dgp_ref)

        dh = _dot(dp_ref[...], wt_ref[...])
        xv = x_ref[...]
        r = _rms(xv)
        xn = xv * r
        dgp_ref[...] += jnp.sum(dh * xn, axis=0, keepdims=True)
        dxn = dh * gp_ref[...]
        dx_ref[...] = g_ref[...] + r * (dxn - xn * jnp.mean(dxn * xn, axis=-1, keepdims=True))

    tile = pl.BlockSpec((ts, D), lambda i: (i, 0))
    return pl.pallas_call(
        body, name=name, grid=(S // ts,),
        in_specs=[pl.BlockSpec((ts, PROJ), lambda i: (i, 0)), _full((PROJ, D)), tile, tile, _full((1, D))],
        out_specs=[tile, _full((1, D))],
        out_shape=[jax.ShapeDtypeStruct((S, D), F32), jax.ShapeDtypeStruct((1, D), F32)],
        compiler_params=_params(),
    )(dp, wt_b, x, g_res, gpre)


def _weight_grad(h, dp, ts, name):
    S = h.shape[0]
    nt = S // ts
    panel = 2 * SHARD_IN

    def body(h_ref, dp_ref, out_ref, acc_ref):
        t = pl.program_id(1)

        @pl.when(t == 0)
        def _():
            acc_ref[...] = jnp.zeros_like(acc_ref)

        acc_ref[...] += _dot_t0(h_ref[...], dp_ref[...])

        @pl.when(t == nt - 1)
        def _():
            out_ref[0] = acc_ref[:, 0:SHARD_IN].astype(BF16)
            out_ref[1] = acc_ref[:, SHARD_IN:panel].astype(BF16)

    return pl.pallas_call(
        body, name=name, grid=(PROJ // panel, nt),
        in_specs=[pl.BlockSpec((ts, D), lambda j, t: (t, 0)), pl.BlockSpec((ts, panel), lambda j, t: (t, j))],
        out_specs=pl.BlockSpec((2, D, SHARD_IN), lambda j, t: (j, 0, 0)),
        out_shape=jax.ShapeDtypeStruct((N_DEV, D, SHARD_IN), BF16),
        scratch_shapes=[pltpu.VMEM((D, panel), F32)],
        compiler_params=_params(2),
    )(h, dp)


def _mesh_position():
    return lax.axis_index("x"), lax.axis_index("y"), lax.axis_index("c")


def _peer(pos, d):
    x, y, c = pos
    px = 1 - x if d & 4 else x
    py = 1 - y if d & 2 else y
    pc = 1 - c if d & 1 else c
    return (px, py, pc), 4 * px + 2 * py + pc


def _gather_weights(wie, woe, wio, woo, small):
    shards = [wie, woe, wio, woo, small]
    out_dtypes = [BF16, BF16, BF16, BF16, F32]
    n = len(shards)

    def body(*refs):
        ins = refs[0:n]
        outs = refs[n:2 * n]
        casts = refs[2 * n:2 * n + 4]
        send_sems, recv_sems = refs[2 * n + 4:]
        pos = _mesh_position()
        me = 4 * pos[0] + 2 * pos[1] + pos[2]
        srcs = []
        for t in range(n):
            if t < 4:
                casts[t][...] = ins[t][...].astype(BF16)
                srcs.append(casts[t])
            else:
                srcs.append(ins[t])
            outs[t][me] = srcs[t][...]
        sends = []
        for d in range(1, N_DEV):
            coords, _ = _peer(pos, d)
            for t in range(n):
                k = (d - 1) * n + t
                cp = pltpu.make_async_remote_copy(src_ref=srcs[t], dst_ref=outs[t].at[me], send_sem=send_sems.at[k],
                                                  recv_sem=recv_sems.at[k], device_id=coords, device_id_type=MESH)
                cp.start()
                sends.append(cp)
        for d in range(1, N_DEV):
            coords, idx = _peer(pos, d)
            for t in range(n):
                k = (d - 1) * n + t
                pltpu.make_async_remote_copy(src_ref=srcs[t], dst_ref=outs[t].at[idx], send_sem=send_sems.at[k],
                                             recv_sem=recv_sems.at[k], device_id=coords,
                                             device_id_type=MESH).wait_recv()
        for cp in sends:
            cp.wait_send()

    vm = pl.BlockSpec(memory_space=pltpu.VMEM)
    return pl.pallas_call(
        body, name="gather_weights",
        in_specs=[vm] * n, out_specs=[vm] * n,
        out_shape=[jax.ShapeDtypeStruct((N_DEV,) + s.shape, dt) for s, dt in zip(shards, out_dtypes)],
        scratch_shapes=[pltpu.VMEM(s.shape, BF16) for s in shards[:4]]
        + [pltpu.SemaphoreType.DMA((n * (N_DEV - 1),)), pltpu.SemaphoreType.DMA((n * (N_DEV - 1),))],
        compiler_params=pltpu.CompilerParams(vmem_limit_bytes=VMEM_LIMIT),
    )(*shards)


def _adamw(w, g, m, v):
    m = ADAM_B1 * m + (1.0 - ADAM_B1) * g
    v = ADAM_B2 * v + (1.0 - ADAM_B2) * (g * g)
    m_hat = m / (1.0 - ADAM_B1 ** ADAM_STEP)
    v_hat = v / (1.0 - ADAM_B2 ** ADAM_STEP)
    delta = -ADAM_LR * (m_hat / (jnp.sqrt(v_hat) + ADAM_EPS) + ADAM_WD * w)
    return delta, m, v


def _reduce_update(gpart, w, m, v, name):
    shard = w.shape
    needs_cast = gpart.dtype != BF16

    def body(gp_ref, w_ref, m_ref, v_ref, g_out, d_out, m_out, v_out, *scratch):
        if needs_cast:
            send_ref, recv_ref, send_sems, recv_sems = scratch
            send_ref[...] = gp_ref[...].astype(BF16)
        else:
            recv_ref, send_sems, recv_sems = scratch
            send_ref = gp_ref
        pos = _mesh_position()
        me = 4 * pos[0] + 2 * pos[1] + pos[2]
        sends = []
        for d in range(1, N_DEV):
            coords, idx = _peer(pos, d)
            cp = pltpu.make_async_remote_copy(src_ref=send_ref.at[idx], dst_ref=recv_ref.at[d - 1],
                                              send_sem=send_sems.at[d - 1], recv_sem=recv_sems.at[d - 1],
                                              device_id=coords, device_id_type=MESH)
            cp.start()
            sends.append(cp)
        acc = gp_ref[me].astype(F32)
        for d in range(1, N_DEV):
            sends[d - 1].wait_recv()
            acc = acc + recv_ref[d - 1].astype(F32)
        g_out[...] = acc
        delta, m_new, v_new = _adamw(w_ref[...], acc, m_ref[...], v_ref[...])
        d_out[...] = delta
        m_out[...] = m_new
        v_out[...] = v_new
        for cp in sends:
            cp.wait_send()

    vm = pl.BlockSpec(memory_space=pltpu.VMEM)
    scratch = [pltpu.VMEM((N_DEV - 1,) + shard, BF16), pltpu.SemaphoreType.DMA((N_DEV - 1,)),
               pltpu.SemaphoreType.DMA((N_DEV - 1,))]
    if needs_cast:
        scratch = [pltpu.VMEM((N_DEV,) + shard, BF16)] + scratch
    return pl.pallas_call(
        body, name=name, in_specs=[vm] * 4, out_specs=[vm] * 4,
        out_shape=[jax.ShapeDtypeStruct(shard, F32)] * 4, scratch_shapes=scratch,
        compiler_params=pltpu.CompilerParams(vmem_limit_bytes=VMEM_LIMIT),
    )(gpart, w, m, v)


def _small_update(gpack, w_rep, m_rep, v_rep, w_sh, m_sh, v_sh):
    rows = REP_SLICE + SMALL_SHARD_ROWS

    def body(gp_ref, wr_ref, mr_ref, vr_ref, ws_ref, ms_ref, vs_ref,
             gr_out, dr_out, mr_out, vr_out, gs_out, ds_out, ms_out, vs_out,
             recv_ref, red_ref, send1, recv1, send2, recv2):
        pos = _mesh_position()
        me = 4 * pos[0] + 2 * pos[1] + pos[2]
        first = []
        for d in range(1, N_DEV):
            coords, idx = _peer(pos, d)
            cp = pltpu.make_async_remote_copy(src_ref=gp_ref.at[idx], dst_ref=recv_ref.at[d - 1],
                                              send_sem=send1.at[d - 1], recv_sem=recv1.at[d - 1],
                                              device_id=coords, device_id_type=MESH)
            cp.start()
            first.append(cp)
        acc = gp_ref[me]
        for d in range(1, N_DEV):
            first[d - 1].wait_recv()
            acc = acc + recv_ref[d - 1]
        red_ref[...] = acc[0:REP_SLICE]
        gr_out[me] = acc[0:REP_SLICE]
        second = []
        for d in range(1, N_DEV):
            coords, idx = _peer(pos, d)
            cp = pltpu.make_async_remote_copy(src_ref=red_ref, dst_ref=gr_out.at[me],
                                              send_sem=send2.at[d - 1], recv_sem=recv2.at[d - 1],
                                              device_id=coords, device_id_type=MESH)
            cp.start()
            second.append(cp)
        gs = acc[REP_SLICE:rows]
        gs_out[...] = gs
        delta, m_new, v_new = _adamw(ws_ref[...], gs, ms_ref[...], vs_ref[...])
        ds_out[...] = delta
        ms_out[...] = m_new
        vs_out[...] = v_new
        for d in range(1, N_DEV):
            coords, idx = _peer(pos, d)
            pltpu.make_async_remote_copy(src_ref=red_ref, dst_ref=gr_out.at[idx], send_sem=send2.at[d - 1],
                                         recv_sem=recv2.at[d - 1], device_id=coords,
                                         device_id_type=MESH).wait_recv()
        for k in range(N_DEV):
            delta, m_new, v_new = _adamw(wr_ref[k], gr_out[k], mr_ref[k], vr_ref[k])
            dr_out[k] = delta
            mr_out[k] = m_new
            vr_out[k] = v_new
        for cp in first + second:
            cp.wait_send()

    vm = pl.BlockSpec(memory_space=pltpu.VMEM)
    rep = jax.ShapeDtypeStruct((N_DEV, REP_SLICE, 128), F32)
    sh = jax.ShapeDtypeStruct((SMALL_SHARD_ROWS, 128), F32)
    sem = pltpu.SemaphoreType.DMA((N_DEV - 1,))
    return pl.pallas_call(
        body, name="small_update", in_specs=[vm] * 7, out_specs=[vm] * 8,
        out_shape=[rep] * 4 + [sh] * 4,
        scratch_shapes=[pltpu.VMEM((N_DEV - 1, rows, 128), F32), pltpu.VMEM((REP_SLICE, 128), F32),
                        sem, sem, sem, sem],
        compiler_params=pltpu.CompilerParams(vmem_limit_bytes=VMEM_LIMIT),
    )(gpack, w_rep, m_rep, v_rep, w_sh, m_sh, v_sh)


REP_PARTS = [("pre_norm", (2, D), 16), ("post_norm", (2, D), 16), ("even_pool_w", (1, 4, 128, 128), 512),
             ("even_pool_scale", (1, B_W), 8), ("odd_w_s", (1, 8, 128, 128), 1024), ("odd_b_s", (1, 8, 128), 8)]


def _rows(a, n_rows):
    a = a.reshape(-1, 128)
    return jnp.pad(a, ((0, n_rows - a.shape[0]), (0, 0)))


def _two_rows(a, b):
    return jnp.pad(a, ((0, 1), (0, 0))) + jnp.pad(b, ((1, 0), (0, 0)))


def _pack_rep(parts):
    blocks = [_rows(a, n) for a, (_, _, n) in zip(parts, REP_PARTS)]
    used = sum(n for _, _, n in REP_PARTS)
    blocks.append(jnp.zeros((REP_ROWS - used, 128), F32))
    return jnp.concatenate(blocks, axis=0).reshape(N_DEV, REP_SLICE, 128)


def _unpack_rep(pack):
    flat = pack.reshape(REP_ROWS, 128)
    out, r = [], 0
    for _, shape, n in REP_PARTS:
        size = 1
        for s in shape:
            size *= s
        out.append(flat[r:r + size // 128].reshape(shape))
        r += n
    return out


def _pack_small_shard(cw, lg, lb):
    return (jnp.pad(cw, ((0, 13), (0, 64))) + jnp.pad(lg, ((8, 7), (0, 0))) + jnp.pad(lb, ((9, 6), (0, 0))))


def _unpack_small_shard(pack):
    return pack[0:3, 0:64].reshape(1, 3, 64), pack[8:9, :], pack[9:10, :]


TS_FWD = 512
TS_BWD = 256


def kernel(x, pre_norm, post_norm, even_w_in, even_conv_w, even_pool_w, even_pool_scale, even_w_out, odd_w_in, odd_ln_g, odd_ln_b, odd_w_s, odd_b_s, odd_w_out, loss_target, m_pre_norm, m_post_norm, m_even_w_in, m_even_conv_w, m_even_pool_w, m_even_pool_scale, m_even_w_out, m_odd_w_in, m_odd_ln_g, m_odd_ln_b, m_odd_w_s, m_odd_b_s, m_odd_w_out, v_pre_norm, v_post_norm, v_even_w_in, v_even_conv_w, v_even_pool_w, v_even_pool_scale, v_even_w_out, v_odd_w_in, v_odd_ln_g, v_odd_ln_b, v_odd_w_s, v_odd_b_s, v_odd_w_out):
    S = x.shape[1]
    xs = x.reshape(S, D)
    tgt = loss_target.reshape(S, D)

    small_shard = _pack_small_shard(even_conv_w[0], odd_ln_g, odd_ln_b)
    wie_g, woe_g, wio_g, woo_g, small_g = _gather_weights(even_w_in[0], even_w_out[0], odd_w_in[0], odd_w_out[0],
                                                          small_shard)
    wie = jnp.transpose(wie_g, (1, 0, 2)).reshape(D, PROJ)
    wie_t = jnp.transpose(wie_g, (0, 2, 1)).reshape(PROJ, D)
    wio = jnp.transpose(wio_g, (1, 0, 2)).reshape(D, PROJ)
    wio_t = jnp.transpose(wio_g, (0, 2, 1)).reshape(PROJ, D)
    woe = woe_g.reshape(D, D)
    woe_t = woe.T
    woo = woo_g.reshape(D, D)
    woo_t = woo.T
    conv_w = jnp.transpose(small_g[:, 0:3, 0:64], (1, 0, 2)).reshape(3, A_W)
    ln_g = small_g[:, 8, :].reshape(1, D)
    ln_b = small_g[:, 9, :].reshape(1, D)

    pool_w_b = even_pool_w[0].astype(BF16)
    pool_wt_b = jnp.swapaxes(even_pool_w[0], 1, 2).astype(BF16)
    ws_tril = jnp.tril(odd_w_s[0])
    ws_b = ws_tril.astype(BF16)
    wst_b = jnp.swapaxes(ws_tril, 1, 2).astype(BF16)
    b_full = jnp.broadcast_to(odd_b_s[0][:, :, None], (N_HEADS, GROUP, GROUP))
    gpre0, gpre1 = pre_norm[0:1], pre_norm[1:2]
    gpost0, gpost1 = post_norm[0:1], post_norm[1:2]

    p0, h0 = _norm_matmul(xs, gpre0, wie, TS_FWD, "even_proj")
    m0, x1 = _even_forward(p0, xs, conv_w, pool_w_b, even_pool_scale, woe, gpost0, TS_FWD)
    p1, h1 = _norm_matmul(x1, gpre1, wio, TS_FWD, "odd_proj")
    m1, g2, loss_vec = _odd_forward(p1, x1, tgt, ln_g, ln_b, ws_b, b_full, woo, gpost1, TS_FWD)
    loss = lax.psum(0.5 * jnp.sum(loss_vec) / D, ("x", "y", "c"))

    dp1, dwoo, dws, dbs, dlg, dlb, dgpost1 = _odd_backward(g2, m1, p1, ln_g, ln_b, ws_b, wst_b, b_full, woo_t,
                                                           gpost1, TS_BWD)
    g1, dgpre1 = _input_backward(dp1, wio_t, x1, g2, gpre1, TS_FWD, "odd_input_backward")
    dwio = _weight_grad(h1, dp1, TS_FWD, "odd_weight_grad")
    dp0, dwoe, dcw, dpw, dps, dgpost0 = _even_backward(g1, m0, p0, conv_w, pool_w_b, pool_wt_b, even_pool_scale,
                                                       woe_t, gpost0, TS_BWD)
    gx, dgpre0 = _input_backward(dp0, wie_t, xs, g1, gpre0, TS_FWD, "even_input_backward")
    dwie = _weight_grad(h0, dp0, TS_FWD, "even_weight_grad")

    g_wie, d_wie, nm_wie, nv_wie = _reduce_update(dwie, even_w_in[0], m_even_w_in[0], v_even_w_in[0], "update_even_w_in")
    g_wio, d_wio, nm_wio, nv_wio = _reduce_update(dwio, odd_w_in[0], m_odd_w_in[0], v_odd_w_in[0], "update_odd_w_in")
    g_woe, d_woe, nm_woe, nv_woe = _reduce_update(dwoe.reshape(N_DEV, SHARD_OUT, D), even_w_out[0], m_even_w_out[0],
                                                  v_even_w_out[0], "update_even_w_out")
    g_woo, d_woo, nm_woo, nv_woo = _reduce_update(dwoo.reshape(N_DEV, SHARD_OUT, D), odd_w_out[0], m_odd_w_out[0],
                                                  v_odd_w_out[0], "update_odd_w_out")

    rep_grads = _pack_rep([_two_rows(dgpre0, dgpre1), _two_rows(dgpost0, dgpost1), dpw, dps, dws, dbs])
    cw_by_owner = jnp.pad(jnp.transpose(dcw[0:3].reshape(3, N_DEV, 64), (1, 0, 2)), ((0, 0), (0, 5), (0, 64)))
    ln_by_owner = (jnp.pad(dlg.reshape(N_DEV, 1, 128), ((0, 0), (0, 7), (0, 0)))
                   + jnp.pad(dlb.reshape(N_DEV, 1, 128), ((0, 0), (1, 6), (0, 0))))
    gpack = jnp.concatenate([rep_grads, cw_by_owner, ln_by_owner], axis=1)
    w_rep = _pack_rep([pre_norm, post_norm, even_pool_w, even_pool_scale, odd_w_s, odd_b_s])
    m_rep = _pack_rep([m_pre_norm, m_post_norm, m_even_pool_w, m_even_pool_scale, m_odd_w_s, m_odd_b_s])
    v_rep = _pack_rep([v_pre_norm, v_post_norm, v_even_pool_w, v_even_pool_scale, v_odd_w_s, v_odd_b_s])
    m_sh = _pack_small_shard(m_even_conv_w[0], m_odd_ln_g, m_odd_ln_b)
    v_sh = _pack_small_shard(v_even_conv_w[0], v_odd_ln_g, v_odd_ln_b)
    small_out = _small_update(gpack, w_rep, m_rep, v_rep, small_shard, m_sh, v_sh)
    rep_out = [_unpack_rep(a) for a in small_out[0:4]]
    sh_out = [_unpack_small_shard(a) for a in small_out[4:8]]

    def leaves(k, big):
        pre, post, pw, psc, ws, bs = rep_out[k]
        cw, lg, lb = sh_out[k]
        wie_k, woe_k, wio_k, woo_k = big
        return [pre, post, wie_k[None], cw, pw, psc, woe_k[None], wio_k[None], lg, lb, ws, bs, woo_k[None]]

    outs = [loss, gx.reshape(1, S, D)]
    outs += leaves(0, (g_wie, g_woe, g_wio, g_woo))
    outs += leaves(1, (d_wie, d_woe, d_wio, d_woo))
    outs += leaves(2, (nm_wie, nm_woe, nm_wio, nm_woo))
    outs += leaves(3, (nv_wie, nv_woe, nv_wio, nv_woo))
    return tuple(outs)
```

```python
import functools

import jax
import jax.numpy as jnp
from jax import lax
from jax.experimental import pallas as pl
from jax.experimental.pallas import tpu as pltpu

F32 = jnp.float32
BF16 = jnp.bfloat16
MESH = pl.DeviceIdType.MESH

D = 1024
EPS = 1e-6
A_W = 512
B_W = 512
POOL_WINDOWS = (2, 4, 8, 16)
GROUP = 128
N_GROUPS = 4
N_HEADS = 8
HALO = 16
PROJ = 3072
N_DEV = 8
SHARD_IN = PROJ // N_DEV
SHARD_OUT = D // N_DEV

ADAM_LR = 0.001
ADAM_B1 = 0.9
ADAM_B2 = 0.999
ADAM_EPS = 1e-08
ADAM_WD = 0.01
ADAM_STEP = 10

VMEM_LIMIT = 56 * 1024 * 1024

REP_ROWS = 1600
REP_SLICE = REP_ROWS // N_DEV
SMALL_SHARD_ROWS = 16


def _params(n_grid=1, vmem=VMEM_LIMIT):
    return pltpu.CompilerParams(dimension_semantics=("arbitrary",) * n_grid, vmem_limit_bytes=vmem)


def _full(shape):
    return pl.BlockSpec(shape, lambda *_: (0,) * len(shape))


def _sigmoid(z):
    return jax.nn.sigmoid(z)


def _dsilu(z, s):
    return s * (1.0 + z * (1.0 - s))


def _dot(a, b):
    return jnp.dot(a, b, preferred_element_type=F32)


def _dot_t0(a, b):
    return lax.dot_general(a, b, (((0,), (0,)), ((), ())), preferred_element_type=F32)


def _dot_t1(a, b):
    return lax.dot_general(a, b, (((1,), (1,)), ((), ())), preferred_element_type=F32)


def _rms(x):
    return lax.rsqrt(jnp.mean(x * x, axis=-1, keepdims=True) + EPS)


def _shift_down(a, k):
    return pltpu.roll(a, k, 0)


def _shift_up(a, k):
    return pltpu.roll(a, a.shape[0] - k, 0)


def _mesh_position():
    return lax.axis_index("x"), lax.axis_index("y"), lax.axis_index("c")


def _index(pos):
    return 4 * pos[0] + 2 * pos[1] + pos[2]


def _peer(pos, d):
    x, y, c = pos
    return (1 - x if d & 4 else x, 1 - y if d & 2 else y, 1 - c if d & 1 else c)


def _remote(src, dst, send_sems, recv_sems, k, to):
    return pltpu.make_async_remote_copy(src_ref=src, dst_ref=dst, send_sem=send_sems.at[k], recv_sem=recv_sems.at[k],
                                        device_id=to, device_id_type=MESH)


class _Gather:
    def __init__(self, srcs, dsts, send_sems, recv_sems):
        x, y, c = _mesh_position()
        n = len(srcs)
        me, sibling = (x, y, c), (x, y, 1 - c)
        chips = [(1 - x, y), (x, 1 - y), (1 - x, 1 - y)]

        def copy(k, t, block, to, src=None):
            slot = dsts[t].at[_index(block)]
            return _remote(slot if src is None else src, slot, send_sems, recv_sems, k * n + t, to)

        tensors = range(n)
        self.local = [pltpu.make_async_copy(srcs[t], dsts[t].at[_index(me)], recv_sems.at[7 * n + t]) for t in tensors]
        self.first = [copy(1 + j, t, me, chip + (c,), srcs[t]) for t in tensors for j, chip in enumerate(chips)]
        self.first += [copy(0, t, me, sibling, srcs[t]) for t in tensors]
        self.ici_in = [copy(1 + j, t, chip + (c,), me) for t in tensors for j, chip in enumerate(chips)]
        self.passed = [copy(4 + j, t, chip + (c,), sibling) for t in tensors for j, chip in enumerate(chips)]
        self.d2d_in = [copy(0, t, sibling, me) for t in tensors]
        self.d2d_in += [copy(4 + j, t, chip + (1 - c,), me) for t in tensors for j, chip in enumerate(chips)]

    def start(self):
        for cp in self.local + self.first:
            cp.start()

    def middle(self):
        for landed, onward in zip(self.ici_in, self.passed):
            landed.wait_recv()
            onward.start()

    def finish(self):
        for cp in self.d2d_in:
            cp.wait_recv()
        for cp in self.first + self.passed:
            cp.wait_send()
        for cp in self.local:
            cp.wait()


class _Scatter:
    def __init__(self, srcs, dsts, send_sems, recv_sems):
        pos = _mesh_position()
        n = len(srcs)
        self.local = [pltpu.make_async_copy(srcs[t].at[_index(pos)], dsts[t].at[0], recv_sems.at[t]) for t in range(n)]
        self.remote = []
        for d in range(1, N_DEV):
            to = _peer(pos, d)
            self.remote += [_remote(srcs[t].at[_index(to)], dsts[t].at[d], send_sems, recv_sems, d * n + t, to)
                            for t in range(n)]

    def start(self):
        for cp in self.local + self.remote:
            cp.start()

    def middle(self):
        pass

    def finish(self):
        for cp in self.local:
            cp.wait()
        for cp in self.remote:
            cp.wait_recv()
        for cp in self.remote:
            cp.wait_send()


HBM_SPEC = pl.BlockSpec(memory_space=pltpu.HBM)
MIDDLE_STEPS_BEFORE_END = 4


def _call(body, *, name, grid, in_specs, out_specs, out_shape, scratch_shapes, args, side=None):
    params = _params(len(grid))
    if side is None:
        return pl.pallas_call(body, name=name, grid=grid, in_specs=in_specs, out_specs=out_specs, out_shape=out_shape,
                              scratch_shapes=scratch_shapes, compiler_params=params)(*args)
    exchange, srcs = side
    ns, n_in, n_out = len(srcs), len(in_specs), len(out_specs)
    total = 1
    for g in grid:
        total *= g
    if exchange is _Gather:
        side_shapes = [jax.ShapeDtypeStruct((N_DEV,) + s.shape, s.dtype) for s in srcs]
    else:
        side_shapes = [jax.ShapeDtypeStruct(s.shape, s.dtype) for s in srcs]

    def wrapped(*refs):
        ins, side_in = refs[:n_in], refs[n_in:n_in + ns]
        outs = refs[n_in + ns:n_in + ns + n_out]
        side_out = refs[n_in + ns + n_out:n_in + 2 * ns + n_out]
        rest = refs[n_in + 2 * ns + n_out:]
        scratch, send_sems, recv_sems = rest[:-2], rest[-2], rest[-1]
        step = pl.program_id(0)
        for axis in range(1, len(grid)):
            step = step * grid[axis] + pl.program_id(axis)

        @pl.when(step == 0)
        def _():
            exchange(side_in, side_out, send_sems, recv_sems).start()

        @pl.when(step == max(total - MIDDLE_STEPS_BEFORE_END, 0))
        def _():
            exchange(side_in, side_out, send_sems, recv_sems).middle()

        body(*ins, *outs, *scratch)

        @pl.when(step == total - 1)
        def _():
            exchange(side_in, side_out, send_sems, recv_sems).finish()

    sems = pltpu.SemaphoreType.DMA((N_DEV * ns,))
    return pl.pallas_call(
        wrapped, name=name, grid=grid, in_specs=list(in_specs) + [HBM_SPEC] * ns,
        out_specs=list(out_specs) + [HBM_SPEC] * ns, out_shape=list(out_shape) + side_shapes,
        scratch_shapes=list(scratch_shapes) + [sems, sems], compiler_params=params)(*args, *srcs)


def _norm_matmul(x, g, w, ts, name, side=None):
    S = x.shape[0]
    N = w.shape[1]

    def body(x_ref, g_ref, w_ref, p_ref, h_ref):
        xv = x_ref[...]
        h = ((xv * _rms(xv)) * g_ref[...]).astype(BF16)
        h_ref[...] = h
        p_ref[...] = _dot(h, w_ref[...]).astype(BF16)

    return _call(
        body, name=name, grid=(S // ts,),
        in_specs=[pl.BlockSpec((ts, D), lambda i: (i, 0)), _full((1, D)), _full((D, N))],
        out_specs=[pl.BlockSpec((ts, N), lambda i: (i, 0)), pl.BlockSpec((ts, D), lambda i: (i, 0))],
        out_shape=[jax.ShapeDtypeStruct((S, N), BF16), jax.ShapeDtypeStruct((S, D), BF16)],
        scratch_shapes=[], args=(x, g, w), side=side)


def _even_forward_parts(p_ref, halo_ref, is_first, row0, cw_ref, pw_ref, ts):
    f = lambda a: a.astype(F32)
    xa = f(p_ref[:, 0:512])
    gb = f(p_ref[:, 512:1024])
    gc = f(p_ref[:, 1024:1536])
    za = f(p_ref[:, 1536:2048])
    xp = f(p_ref[:, 2048:2560])
    zp = f(p_ref[:, 2560:3072])
    keep = jnp.where(is_first, 0.0, 1.0).astype(F32)
    ha = gc * xa
    ha_halo = f(halo_ref[:, 1024:1536]) * f(halo_ref[:, 0:512]) * keep
    xp_halo = f(halo_ref[:, 2048:2560]) * keep
    ha_ext = jnp.concatenate([ha_halo, ha], axis=0)
    ha_m1 = _shift_down(ha_ext, 1)[HALO:]
    ha_m2 = _shift_down(ha_ext, 2)[HALO:]
    conv = cw_ref[2:3, :] * ha + cw_ref[1:2, :] * ha_m1 + cw_ref[0:1, :] * ha_m2
    sig_a = _sigmoid(za)
    silu_a = za * sig_a

    xp_ext = jnp.concatenate([xp_halo, xp], axis=0)
    pos = row0 + lax.broadcasted_iota(jnp.int32, (ts, 1), 0)
    pooled, inv_cnt, mixed = [], [], []
    for g, w in enumerate(POOL_WINDOWS):
        cols = slice(g * GROUP, (g + 1) * GROUP)
        s = xp_ext[:, cols]
        k = 1
        while k < w:
            s = s + _shift_down(s, k)
            k *= 2
        inv = 1.0 / jnp.minimum(pos + 1, w).astype(F32)
        pg = s[HALO:] * inv - xp[:, cols]
        pooled.append(pg)
        inv_cnt.append(inv)
        mixed.append(_dot(pg.astype(BF16), pw_ref[g]))
    mixed = jnp.concatenate(mixed, axis=1)
    sig_b = _sigmoid(zp)
    silu_b = zp * sig_b
    return dict(xa=xa, gb=gb, gc=gc, za=za, xp=xp, zp=zp, ha=ha, ha_m1=ha_m1, ha_m2=ha_m2, conv=conv,
                sig_a=sig_a, silu_a=silu_a, pooled=pooled, inv_cnt=inv_cnt, mixed=mixed, sig_b=sig_b,
                silu_b=silu_b)


def _halo_index(ts):
    blocks_per_tile = ts // HALO
    return lambda i: (jnp.maximum(i * blocks_per_tile - 1, 0), 0)


def _even_forward(p, x, cw, pw_b, ps, wout_b, gpost, ts):
    S = x.shape[0]

    def body(p_ref, halo_ref, x_ref, cw_ref, pw_ref, ps_ref, wo_ref, gp_ref, m_ref, x1_ref, mix_ref):
        i = pl.program_id(0)
        fw = _even_forward_parts(p_ref, halo_ref, i == 0, i * ts, cw_ref, pw_ref, ts)
        mix_ref[:, 0:A_W] = (fw["gb"] * fw["conv"] * fw["silu_a"]).astype(BF16)
        mix_ref[:, A_W:D] = (fw["mixed"] * ps_ref[...] * fw["silu_b"]).astype(BF16)
        m = _dot(mix_ref[...], wo_ref[...])
        m_ref[...] = m
        x1_ref[...] = x_ref[...] + (m * _rms(m)) * gp_ref[...]

    tile = pl.BlockSpec((ts, D), lambda i: (i, 0))
    return pl.pallas_call(
        body, name="even_forward", grid=(S // ts,),
        in_specs=[pl.BlockSpec((ts, PROJ), lambda i: (i, 0)), pl.BlockSpec((HALO, PROJ), _halo_index(ts)), tile,
                  _full((3, A_W)), _full((N_GROUPS, GROUP, GROUP)), _full((1, B_W)), _full((D, D)), _full((1, D))],
        out_specs=[tile, tile],
        out_shape=[jax.ShapeDtypeStruct((S, D), F32), jax.ShapeDtypeStruct((S, D), F32)],
        scratch_shapes=[pltpu.VMEM((ts, D), BF16)],
        compiler_params=_params(),
    )(p, p, x, cw, pw_b, ps, wout_b, gpost)


def _even_backward(g1, m0, p, cw, pw_b, pwt_b, ps, woutt_b, gpost, ts, side=None):
    S = g1.shape[0]
    nt = S // ts

    def body(g_ref, m_ref, p_ref, halo_ref, cw_ref, pw_ref, pwt_ref, ps_ref, wot_ref, gp_ref,
             dp_ref, dwob_ref, dcw_ref, dpw_ref, dps_ref, dgp_ref, carry_ref, mix_ref, dm_ref, dwo_ref):
        step = pl.program_id(0)
        i = nt - 1 - step

        @pl.when(step == 0)
        def _():
            carry_ref[...] = jnp.zeros_like(carry_ref)
            dwo_ref[...] = jnp.zeros_like(dwo_ref)
            dcw_ref[...] = jnp.zeros_like(dcw_ref)
            dpw_ref[...] = jnp.zeros_like(dpw_ref)
            dps_ref[...] = jnp.zeros_like(dps_ref)
            dgp_ref[...] = jnp.zeros_like(dgp_ref)

        g = g_ref[...]
        m = m_ref[...]
        q = _rms(m)
        n = m * q
        dgp_ref[...] += jnp.sum(g * n, axis=0, keepdims=True)
        dn = g * gp_ref[...]
        dm = q * (dn - n * jnp.mean(dn * n, axis=-1, keepdims=True))
        dm_ref[...] = dm.astype(BF16)
        dmix = _dot(dm_ref[...], wot_ref[...])
        dya = dmix[:, 0:A_W]
        dyb = dmix[:, A_W:D]

        fw = _even_forward_parts(p_ref, halo_ref, i == 0, i * ts, cw_ref, pw_ref, ts)
        ps_v = ps_ref[...]
        mix_ref[:, 0:A_W] = (fw["gb"] * fw["conv"] * fw["silu_a"]).astype(BF16)
        mix_ref[:, A_W:D] = (fw["mixed"] * ps_v * fw["silu_b"]).astype(BF16)
        dwo_ref[...] += _dot_t0(mix_ref[...], dm_ref[...])

        t = dya * fw["gb"]
        dconv = t * fw["silu_a"]
        dgb = dya * fw["conv"] * fw["silu_a"]
        dza = t * fw["conv"] * _dsilu(fw["za"], fw["sig_a"])
        dcw_ref[2:3, :] += jnp.sum(dconv * fw["ha"], axis=0, keepdims=True)
        dcw_ref[1:2, :] += jnp.sum(dconv * fw["ha_m1"], axis=0, keepdims=True)
        dcw_ref[0:1, :] += jnp.sum(dconv * fw["ha_m2"], axis=0, keepdims=True)
        dconv_ext = jnp.concatenate([dconv, carry_ref[:, 0:A_W]], axis=0)
        dha = (cw_ref[2:3, :] * dconv + cw_ref[1:2, :] * _shift_up(dconv_ext, 1)[:ts]
               + cw_ref[0:1, :] * _shift_up(dconv_ext, 2)[:ts])
        dgc = dha * fw["xa"]
        dxa = dha * fw["gc"]

        u = dyb * fw["mixed"]
        dps_ref[...] += jnp.sum(u * fw["silu_b"], axis=0, keepdims=True)
        dzp = u * ps_v * _dsilu(fw["zp"], fw["sig_b"])
        dmixed = (dyb * ps_v * fw["silu_b"]).astype(BF16)
        dxp, e_first = [], []
        for gi, w in enumerate(POOL_WINDOWS):
            cols = slice(gi * GROUP, (gi + 1) * GROUP)
            dmg = dmixed[:, cols]
            dpooled = _dot(dmg, pwt_ref[gi])
            dpw_ref[gi] += _dot_t0(fw["pooled"][gi].astype(BF16), dmg)
            e = dpooled * fw["inv_cnt"][gi]
            e_first.append(e[0:HALO])
            s = jnp.concatenate([e, carry_ref[:, A_W + gi * GROUP:A_W + (gi + 1) * GROUP]], axis=0)
            k = 1
            while k < w:
                s = s + _shift_up(s, k)
                k *= 2
            dxp.append(s[:ts] - dpooled)

        carry_ref[:, 0:A_W] = dconv[0:HALO]
        carry_ref[:, A_W:D] = jnp.concatenate(e_first, axis=1)

        dp_ref[:, 0:512] = dxa.astype(BF16)
        dp_ref[:, 512:1024] = dgb.astype(BF16)
        dp_ref[:, 1024:1536] = dgc.astype(BF16)
        dp_ref[:, 1536:2048] = dza.astype(BF16)
        dp_ref[:, 2048:2560] = jnp.concatenate(dxp, axis=1).astype(BF16)
        dp_ref[:, 2560:3072] = dzp.astype(BF16)

        @pl.when(step == nt - 1)
        def _():
            dwob_ref[...] = dwo_ref[...].astype(BF16)

    rev = lambda s: (nt - 1 - s, 0)
    tile = pl.BlockSpec((ts, D), rev)
    bpt = ts // HALO
    halo_map = lambda s: (jnp.maximum((nt - 1 - s) * bpt - 1, 0), 0)
    return _call(
        body, name="even_backward", grid=(nt,),
        in_specs=[tile, tile, pl.BlockSpec((ts, PROJ), rev), pl.BlockSpec((HALO, PROJ), halo_map),
                  _full((3, A_W)), _full((N_GROUPS, GROUP, GROUP)), _full((N_GROUPS, GROUP, GROUP)),
                  _full((1, B_W)), _full((D, D)), _full((1, D))],
        out_specs=[pl.BlockSpec((ts, PROJ), rev), _full((D, D)), _full((8, A_W)),
                   _full((N_GROUPS, GROUP, GROUP)), _full((1, B_W)), _full((1, D))],
        out_shape=[jax.ShapeDtypeStruct((S, PROJ), BF16), jax.ShapeDtypeStruct((D, D), BF16),
                   jax.ShapeDtypeStruct((8, A_W), F32), jax.ShapeDtypeStruct((N_GROUPS, GROUP, GROUP), F32),
                   jax.ShapeDtypeStruct((1, B_W), F32), jax.ShapeDtypeStruct((1, D), F32)],
        scratch_shapes=[pltpu.VMEM((HALO, D), F32), pltpu.VMEM((ts, D), BF16), pltpu.VMEM((ts, D), BF16),
                        pltpu.VMEM((D, D), F32)],
        args=(g1, m0, p, p, cw, pw_b, pwt_b, ps, woutt_b, gpost), side=side)


def _odd_forward_parts(p_ref, lg_ref, lb_ref, wt_ref, bfull_ref, vln_ref, sv_ref, ts):
    f = lambda a: a.astype(F32)
    u = f(p_ref[:, 0:D])
    v = f(p_ref[:, D:2 * D])
    z = f(p_ref[:, 2 * D:3 * D])
    mu = jnp.mean(v, axis=-1, keepdims=True)
    vc = v - mu
    rstd = lax.rsqrt(jnp.mean(vc * vc, axis=-1, keepdims=True) + EPS)
    vh = vc * rstd
    vln_ref[...] = (vh * lg_ref[...] + lb_ref[...]).astype(BF16)
    for c in range(ts // GROUP):
        rows = slice(c * GROUP, (c + 1) * GROUP)
        for h in range(N_HEADS):
            cols = slice(h * GROUP, (h + 1) * GROUP)
            sv_ref[rows, cols] = _dot(wt_ref[h], vln_ref[rows, cols]) + bfull_ref[h]
    return u, z, vh, rstd


def _odd_forward(p, x1, tgt, lg, lb, wt_b, bfull, wout_b, gpost, ts):
    S = x1.shape[0]

    def body(p_ref, x_ref, t_ref, lg_ref, lb_ref, wt_ref, bfull_ref, wo_ref, gp_ref,
             m_ref, g_ref, loss_ref, vln_ref, sv_ref, y_ref):
        @pl.when(pl.program_id(0) == 0)
        def _():
            loss_ref[...] = jnp.zeros_like(loss_ref)

        u, z, _, _ = _odd_forward_parts(p_ref, lg_ref, lb_ref, wt_ref, bfull_ref, vln_ref, sv_ref, ts)
        y_ref[...] = (u * sv_ref[...] * (z * _sigmoid(z))).astype(BF16)
        m = _dot(y_ref[...], wo_ref[...])
        m_ref[...] = m
        x2 = x_ref[...] + (m * _rms(m)) * gp_ref[...]
        err = x2 - t_ref[...]
        g_ref[...] = err * (1.0 / D)
        loss_ref[...] += jnp.sum(err * err, axis=0, keepdims=True)

    tile = pl.BlockSpec((ts, D), lambda i: (i, 0))
    small = _full((N_HEADS, GROUP, GROUP))
    return pl.pallas_call(
        body, name="odd_forward", grid=(S // ts,),
        in_specs=[pl.BlockSpec((ts, PROJ), lambda i: (i, 0)), tile, tile, _full((1, D)), _full((1, D)),
                  small, small, _full((D, D)), _full((1, D))],
        out_specs=[tile, tile, _full((1, D))],
        out_shape=[jax.ShapeDtypeStruct((S, D), F32), jax.ShapeDtypeStruct((S, D), F32),
                   jax.ShapeDtypeStruct((1, D), F32)],
        scratch_shapes=[pltpu.VMEM((ts, D), BF16), pltpu.VMEM((ts, D), F32), pltpu.VMEM((ts, D), BF16)],
        compiler_params=_params(),
    )(p, x1, tgt, lg, lb, wt_b, bfull, wout_b, gpost)


def _odd_backward(g2, m1, p, lg, lb, wt_b, wtt_b, bfull, woutt_b, gpost, ts):
    S = g2.shape[0]
    nt = S // ts

    def body(g_ref, m_ref, p_ref, lg_ref, lb_ref, wt_ref, wtt_ref, bfull_ref, wot_ref, gp_ref,
             dp_ref, dwob_ref, dws_ref, dbs_ref, dlg_ref, dlb_ref, dgp_ref,
             vln_ref, sv_ref, y_ref, dm_ref, dsv_ref, dsvb_ref, dvln_ref, dsum_ref, dwo_ref):
        step = pl.program_id(0)

        @pl.when(step == 0)
        def _():
            dwo_ref[...] = jnp.zeros_like(dwo_ref)
            dws_ref[...] = jnp.zeros_like(dws_ref)
            dsum_ref[...] = jnp.zeros_like(dsum_ref)
            dlg_ref[...] = jnp.zeros_like(dlg_ref)
            dlb_ref[...] = jnp.zeros_like(dlb_ref)
            dgp_ref[...] = jnp.zeros_like(dgp_ref)

        g = g_ref[...]
        m = m_ref[...]
        q = _rms(m)
        n = m * q
        dgp_ref[...] += jnp.sum(g * n, axis=0, keepdims=True)
        dn = g * gp_ref[...]
        dm = q * (dn - n * jnp.mean(dn * n, axis=-1, keepdims=True))
        dm_ref[...] = dm.astype(BF16)
        dy = _dot(dm_ref[...], wot_ref[...])

        u, z, vh, rstd = _odd_forward_parts(p_ref, lg_ref, lb_ref, wt_ref, bfull_ref, vln_ref, sv_ref, ts)
        sv = sv_ref[...]
        sig = _sigmoid(z)
        sz = z * sig
        y_ref[...] = (u * sv * sz).astype(BF16)
        dwo_ref[...] += _dot_t0(y_ref[...], dm_ref[...])

        t = dy * sz
        du = t * sv
        dsv = t * u
        dz = dy * u * sv * _dsilu(z, sig)
        dsv_ref[...] = dsv
        dsvb_ref[...] = dsv.astype(BF16)

        tril = (lax.broadcasted_iota(jnp.int32, (GROUP, GROUP), 0)
                >= lax.broadcasted_iota(jnp.int32, (GROUP, GROUP), 1))
        for c in range(ts // GROUP):
            rows = slice(c * GROUP, (c + 1) * GROUP)
            dsum_ref[...] += dsv_ref[rows, :]
            for h in range(N_HEADS):
                cols = slice(h * GROUP, (h + 1) * GROUP)
                dvln_ref[rows, cols] = _dot(wtt_ref[h], dsvb_ref[rows, cols])
                dws_ref[h] += jnp.where(tril, _dot_t1(dsvb_ref[rows, cols], vln_ref[rows, cols]), 0.0)

        dvln = dvln_ref[...]
        dlg_ref[...] += jnp.sum(dvln * vh, axis=0, keepdims=True)
        dlb_ref[...] += jnp.sum(dvln, axis=0, keepdims=True)
        dvh = dvln * lg_ref[...]
        dv = rstd * (dvh - jnp.mean(dvh, axis=-1, keepdims=True)
                     - vh * jnp.mean(dvh * vh, axis=-1, keepdims=True))
        dp_ref[:, 0:D] = du.astype(BF16)
        dp_ref[:, D:2 * D] = dv.astype(BF16)
        dp_ref[:, 2 * D:3 * D] = dz.astype(BF16)

        @pl.when(step == nt - 1)
        def _():
            dwob_ref[...] = dwo_ref[...].astype(BF16)
            ones = jnp.ones((8, GROUP), F32)
            for h in range(N_HEADS):
                cols = slice(h * GROUP, (h + 1) * GROUP)
                sums = lax.dot_general(ones, dsum_ref[:, cols], (((1,), (1,)), ((), ())),
                                       precision=lax.Precision.HIGHEST, preferred_element_type=F32)
                dbs_ref[h:h + 1, :] = sums[0:1, :]

    tile = pl.BlockSpec((ts, D), lambda i: (i, 0))
    small = _full((N_HEADS, GROUP, GROUP))
    vec = _full((1, D))
    return pl.pallas_call(
        body, name="odd_backward", grid=(nt,),
        in_specs=[tile, tile, pl.BlockSpec((ts, PROJ), lambda i: (i, 0)), vec, vec, small, small, small,
                  _full((D, D)), vec],
        out_specs=[pl.BlockSpec((ts, PROJ), lambda i: (i, 0)), _full((D, D)), small, _full((N_HEADS, GROUP)),
                   vec, vec, vec],
        out_shape=[jax.ShapeDtypeStruct((S, PROJ), BF16), jax.ShapeDtypeStruct((D, D), BF16),
                   jax.ShapeDtypeStruct((N_HEADS, GROUP, GROUP), F32), jax.ShapeDtypeStruct((N_HEADS, GROUP), F32),
                   jax.ShapeDtypeStruct((1, D), F32), jax.ShapeDtypeStruct((1, D), F32),
                   jax.ShapeDtypeStruct((1, D), F32)],
        scratch_shapes=[pltpu.VMEM((ts, D), BF16), pltpu.VMEM((ts, D), F32), pltpu.VMEM((ts, D), BF16),
                        pltpu.VMEM((ts, D), BF16), pltpu.VMEM((ts, D), F32), pltpu.VMEM((ts, D), BF16),
                        pltpu.VMEM((ts, D), F32), pltpu.VMEM((GROUP, D), F32), pltpu.VMEM((D, D), F32)],
        compiler_params=_params(),
    )(g2, m1, p, lg, lb, wt_b, wtt_b, bfull, woutt_b, gpost)


def _input_backward(dp, wt_b, x, g_res, gpre, ts, name, side=None):
    S = x.shape[0]

    def body(dp_ref, wt_ref, x_ref, g_ref, gp_ref, dx_ref, dgp_ref):
        @pl.when(pl.program_id(0) == 0)
        def _():
            dgp_ref[...] = jnp.zeros_like(dgp_ref)

        dh = _dot(dp_ref[...], wt_ref[...])
        xv = x_ref[...]
        r = _rms(xv)
        xn = xv * r
        dgp_ref[...] += jnp.sum(dh * xn, axis=0, keepdims=True)
        dxn = dh * gp_ref[...]
        dx_ref[...] = g_ref[...] + r * (dxn - xn * jnp.mean(dxn * xn, axis=-1, keepdims=True))

    tile = pl.BlockSpec((ts, D), lambda i: (i, 0))
    return _call(
        body, name=name, grid=(S // ts,),
        in_specs=[pl.BlockSpec((ts, PROJ), lambda i: (i, 0)), _full((PROJ, D)), tile, tile, _full((1, D))],
        out_specs=[tile, _full((1, D))],
        out_shape=[jax.ShapeDtypeStruct((S, D), F32), jax.ShapeDtypeStruct((1, D), F32)],
        scratch_shapes=[], args=(dp, wt_b, x, g_res, gpre), side=side)


def _weight_grad(h, dp, ts, name, side=None):
    S = h.shape[0]
    nt = S // ts
    panel = 2 * SHARD_IN

    def body(h_ref, dp_ref, out_ref, acc_ref):
        t = pl.program_id(1)

        @pl.when(t == 0)
        def _():
            acc_ref[...] = jnp.zeros_like(acc_ref)

        acc_ref[...] += _dot_t0(h_ref[...], dp_ref[...])

        @pl.when(t == nt - 1)
        def _():
            out_ref[0] = acc_ref[:, 0:SHARD_IN].astype(BF16)
            out_ref[1] = acc_ref[:, SHARD_IN:panel].astype(BF16)

    return _call(
        body, name=name, grid=(PROJ // panel, nt),
        in_specs=[pl.BlockSpec((ts, D), lambda j, t: (t, 0)), pl.BlockSpec((ts, panel), lambda j, t: (t, j))],
        out_specs=[pl.BlockSpec((2, D, SHARD_IN), lambda j, t: (j, 0, 0))],
        out_shape=[jax.ShapeDtypeStruct((N_DEV, D, SHARD_IN), BF16)],
        scratch_shapes=[pltpu.VMEM((D, panel), F32)], args=(h, dp), side=side)


def _gather_first(wie, small, woe, wio, woo):
    def body(wie_ref, sm_ref, woe_ref, wio_ref, woo_ref, wie_out, sm_out, woe_b, wio_b, woo_b,
             cast_ref, send_sems, recv_sems):
        cast_ref[...] = wie_ref[...].astype(BF16)
        gather = _Gather([cast_ref, sm_ref], [wie_out, sm_out], send_sems, recv_sems)
        gather.start()
        woe_b[...] = woe_ref[...].astype(BF16)
        wio_b[...] = wio_ref[...].astype(BF16)
        woo_b[...] = woo_ref[...].astype(BF16)
        gather.middle()
        gather.finish()

    vm = pl.BlockSpec(memory_space=pltpu.VMEM)
    sems = pltpu.SemaphoreType.DMA((N_DEV * 2,))
    return pl.pallas_call(
        body, name="gather_first", in_specs=[vm] * 5, out_specs=[vm] * 5,
        out_shape=[jax.ShapeDtypeStruct((N_DEV,) + wie.shape, BF16), jax.ShapeDtypeStruct((N_DEV,) + small.shape, F32),
                   jax.ShapeDtypeStruct(woe.shape, BF16), jax.ShapeDtypeStruct(wio.shape, BF16),
                   jax.ShapeDtypeStruct(woo.shape, BF16)],
        scratch_shapes=[pltpu.VMEM(wie.shape, BF16), sems, sems],
        compiler_params=pltpu.CompilerParams(vmem_limit_bytes=VMEM_LIMIT),
    )(wie, small, woe, wio, woo)


def _adamw(w, g, m, v):
    m = ADAM_B1 * m + (1.0 - ADAM_B1) * g
    v = ADAM_B2 * v + (1.0 - ADAM_B2) * (g * g)
    m_hat = m / (1.0 - ADAM_B1 ** ADAM_STEP)
    v_hat = v / (1.0 - ADAM_B2 ** ADAM_STEP)
    delta = -ADAM_LR * (m_hat / (jnp.sqrt(v_hat) + ADAM_EPS) + ADAM_WD * w)
    return delta, m, v


def _sum_update(parts, w, m, v, name):
    R, C = w.shape
    rb = min(R, 256)

    def body(p_ref, w_ref, m_ref, v_ref, g_out, d_out, m_out, v_out):
        acc = p_ref[0].astype(F32)
        for d in range(1, N_DEV):
            acc = acc + p_ref[d].astype(F32)
        g_out[...] = acc
        delta, m_new, v_new = _adamw(w_ref[...], acc, m_ref[...], v_ref[...])
        d_out[...] = delta
        m_out[...] = m_new
        v_out[...] = v_new

    blk = pl.BlockSpec((rb, C), lambda i: (i, 0))
    return pl.pallas_call(
        body, name=name, grid=(R // rb,),
        in_specs=[pl.BlockSpec((N_DEV, rb, C), lambda i: (0, i, 0)), blk, blk, blk], out_specs=[blk] * 4,
        out_shape=[jax.ShapeDtypeStruct((R, C), F32)] * 4, compiler_params=_params(),
    )(parts, w, m, v)


def _small_update(gpack, w_rep, m_rep, v_rep, w_sh, m_sh, v_sh):
    rows = REP_SLICE + SMALL_SHARD_ROWS

    def body(gp_ref, wr_ref, mr_ref, vr_ref, ws_ref, ms_ref, vs_ref,
             gr_out, dr_out, mr_out, vr_out, gs_out, ds_out, ms_out, vs_out,
             recv_ref, red_ref, send1, recv1, send2, recv2):
        pos = _mesh_position()
        me = _index(pos)
        first = []
        for d in range(1, N_DEV):
            coords = _peer(pos, d)
            idx = _index(coords)
            cp = pltpu.make_async_remote_copy(src_ref=gp_ref.at[idx], dst_ref=recv_ref.at[d - 1],
                                              send_sem=send1.at[d - 1], recv_sem=recv1.at[d - 1],
                                              device_id=coords, device_id_type=MESH)
            cp.start()
            first.append(cp)
        acc = gp_ref[me]
        for d in range(1, N_DEV):
            first[d - 1].wait_recv()
            acc = acc + recv_ref[d - 1]
        red_ref[...] = acc[0:REP_SLICE]
        gr_out[me] = acc[0:REP_SLICE]
        second = []
        for d in range(1, N_DEV):
            coords = _peer(pos, d)
            idx = _index(coords)
            cp = pltpu.make_async_remote_copy(src_ref=red_ref, dst_ref=gr_out.at[me],
                                              send_sem=send2.at[d - 1], recv_sem=recv2.at[d - 1],
                                              device_id=coords, device_id_type=MESH)
            cp.start()
            second.append(cp)
        gs = acc[REP_SLICE:rows]
        gs_out[...] = gs
        delta, m_new, v_new = _adamw(ws_ref[...], gs, ms_ref[...], vs_ref[...])
        ds_out[...] = delta
        ms_out[...] = m_new
        vs_out[...] = v_new
        for d in range(1, N_DEV):
            coords = _peer(pos, d)
            idx = _index(coords)
            pltpu.make_async_remote_copy(src_ref=red_ref, dst_ref=gr_out.at[idx], send_sem=send2.at[d - 1],
                                         recv_sem=recv2.at[d - 1], device_id=coords,
                                         device_id_type=MESH).wait_recv()
        for k in range(N_DEV):
            delta, m_new, v_new = _adamw(wr_ref[k], gr_out[k], mr_ref[k], vr_ref[k])
            dr_out[k] = delta
            mr_out[k] = m_new
            vr_out[k] = v_new
        for cp in first + second:
            cp.wait_send()

    vm = pl.BlockSpec(memory_space=pltpu.VMEM)
    rep = jax.ShapeDtypeStruct((N_DEV, REP_SLICE, 128), F32)
    sh = jax.ShapeDtypeStruct((SMALL_SHARD_ROWS, 128), F32)
    sem = pltpu.SemaphoreType.DMA((N_DEV - 1,))
    return pl.pallas_call(
        body, name="small_update", in_specs=[vm] * 7, out_specs=[vm] * 8,
        out_shape=[rep] * 4 + [sh] * 4,
        scratch_shapes=[pltpu.VMEM((N_DEV - 1, rows, 128), F32), pltpu.VMEM((REP_SLICE, 128), F32),
                        sem, sem, sem, sem],
        compiler_params=pltpu.CompilerParams(vmem_limit_bytes=VMEM_LIMIT),
    )(gpack, w_rep, m_rep, v_rep, w_sh, m_sh, v_sh)


REP_PARTS = [("pre_norm", (2, D), 16), ("post_norm", (2, D), 16), ("even_pool_w", (1, 4, 128, 128), 512),
             ("even_pool_scale", (1, B_W), 8), ("odd_w_s", (1, 8, 128, 128), 1024), ("odd_b_s", (1, 8, 128), 8)]


def _rows(a, n_rows):
    a = a.reshape(-1, 128)
    return jnp.pad(a, ((0, n_rows - a.shape[0]), (0, 0)))


def _two_rows(a, b):
    return jnp.pad(a, ((0, 1), (0, 0))) + jnp.pad(b, ((1, 0), (0, 0)))


def _pack_rep(parts):
    blocks = [_rows(a, n) for a, (_, _, n) in zip(parts, REP_PARTS)]
    used = sum(n for _, _, n in REP_PARTS)
    blocks.append(jnp.zeros((REP_ROWS - used, 128), F32))
    return jnp.concatenate(blocks, axis=0).reshape(N_DEV, REP_SLICE, 128)


def _unpack_rep(pack):
    flat = pack.reshape(REP_ROWS, 128)
    out, r = [], 0
    for _, shape, n in REP_PARTS:
        size = 1
        for s in shape:
            size *= s
        out.append(flat[r:r + size // 128].reshape(shape))
        r += n
    return out


def _pack_small_shard(cw, lg, lb):
    return (jnp.pad(cw, ((0, 13), (0, 64))) + jnp.pad(lg, ((8, 7), (0, 0))) + jnp.pad(lb, ((9, 6), (0, 0))))


def _unpack_small_shard(pack):
    return pack[0:3, 0:64].reshape(1, 3, 64), pack[8:9, :], pack[9:10, :]


TS_FWD = 512
TS_BWD = 256


def kernel(x, pre_norm, post_norm, even_w_in, even_conv_w, even_pool_w, even_pool_scale, even_w_out, odd_w_in, odd_ln_g, odd_ln_b, odd_w_s, odd_b_s, odd_w_out, loss_target, m_pre_norm, m_post_norm, m_even_w_in, m_even_conv_w, m_even_pool_w, m_even_pool_scale, m_even_w_out, m_odd_w_in, m_odd_ln_g, m_odd_ln_b, m_odd_w_s, m_odd_b_s, m_odd_w_out, v_pre_norm, v_post_norm, v_even_w_in, v_even_conv_w, v_even_pool_w, v_even_pool_scale, v_even_w_out, v_odd_w_in, v_odd_ln_g, v_odd_ln_b, v_odd_w_s, v_odd_b_s, v_odd_w_out):
    S = x.shape[1]
    xs = x.reshape(S, D)
    tgt = loss_target.reshape(S, D)

    small_shard = _pack_small_shard(even_conv_w[0], odd_ln_g, odd_ln_b)
    wie_g, small_g, woe_b, wio_b, woo_b = _gather_first(even_w_in[0], small_shard, even_w_out[0], odd_w_in[0],
                                                        odd_w_out[0])
    wie = jnp.transpose(wie_g, (1, 0, 2)).reshape(D, PROJ)
    wie_t = jnp.transpose(wie_g, (0, 2, 1)).reshape(PROJ, D)
    conv_w = jnp.transpose(small_g[:, 0:3, 0:64], (1, 0, 2)).reshape(3, A_W)
    ln_g = small_g[:, 8, :].reshape(1, D)
    ln_b = small_g[:, 9, :].reshape(1, D)

    pool_w_b = even_pool_w[0].astype(BF16)
    pool_wt_b = jnp.swapaxes(even_pool_w[0], 1, 2).astype(BF16)
    ws_tril = jnp.tril(odd_w_s[0])
    ws_b = ws_tril.astype(BF16)
    wst_b = jnp.swapaxes(ws_tril, 1, 2).astype(BF16)
    b_full = jnp.broadcast_to(odd_b_s[0][:, :, None], (N_HEADS, GROUP, GROUP))
    gpre0, gpre1 = pre_norm[0:1], pre_norm[1:2]
    gpost0, gpost1 = post_norm[0:1], post_norm[1:2]

    p0, h0, woe_g, wio_g, woo_g = _norm_matmul(xs, gpre0, wie, TS_FWD, "even_proj",
                                               side=(_Gather, [woe_b, wio_b, woo_b]))
    wio = jnp.transpose(wio_g, (1, 0, 2)).reshape(D, PROJ)
    wio_t = jnp.transpose(wio_g, (0, 2, 1)).reshape(PROJ, D)
    woe = woe_g.reshape(D, D)
    woe_t = woe.T
    woo = woo_g.reshape(D, D)
    woo_t = woo.T
    m0, x1 = _even_forward(p0, xs, conv_w, pool_w_b, even_pool_scale, woe, gpost0, TS_FWD)
    p1, h1 = _norm_matmul(x1, gpre1, wio, TS_FWD, "odd_proj")
    m1, g2, loss_vec = _odd_forward(p1, x1, tgt, ln_g, ln_b, ws_b, b_full, woo, gpost1, TS_FWD)
    loss = lax.psum(0.5 * jnp.sum(loss_vec) / D, ("x", "y", "c"))

    dp1, dwoo, dws, dbs, dlg, dlb, dgpost1 = _odd_backward(g2, m1, p1, ln_g, ln_b, ws_b, wst_b, b_full, woo_t,
                                                           gpost1, TS_BWD)
    by_owner = lambda dwo: dwo.reshape(N_DEV, SHARD_OUT, D)
    g1, dgpre1, dwoo_parts = _input_backward(dp1, wio_t, x1, g2, gpre1, TS_FWD, "odd_input_backward",
                                             side=(_Scatter, [by_owner(dwoo)]))
    dwio, = _weight_grad(h1, dp1, TS_FWD, "odd_weight_grad")
    dp0, dwoe, dcw, dpw, dps, dgpost0, dwio_parts = _even_backward(
        g1, m0, p0, conv_w, pool_w_b, pool_wt_b, even_pool_scale, woe_t, gpost0, TS_BWD, side=(_Scatter, [dwio]))
    dwie, dwoe_parts = _weight_grad(h0, dp0, TS_FWD, "even_weight_grad", side=(_Scatter, [by_owner(dwoe)]))
    gx, dgpre0, dwie_parts = _input_backward(dp0, wie_t, xs, g1, gpre0, TS_FWD, "even_input_backward",
                                             side=(_Scatter, [dwie]))

    g_wie, d_wie, nm_wie, nv_wie = _sum_update(dwie_parts, even_w_in[0], m_even_w_in[0], v_even_w_in[0],
                                               "update_even_w_in")
    g_wio, d_wio, nm_wio, nv_wio = _sum_update(dwio_parts, odd_w_in[0], m_odd_w_in[0], v_odd_w_in[0],
                                               "update_odd_w_in")
    g_woe, d_woe, nm_woe, nv_woe = _sum_update(dwoe_parts, even_w_out[0], m_even_w_out[0], v_even_w_out[0],
                                               "update_even_w_out")
    g_woo, d_woo, nm_woo, nv_woo = _sum_update(dwoo_parts, odd_w_out[0], m_odd_w_out[0], v_odd_w_out[0],
                                               "update_odd_w_out")

    rep_grads = _pack_rep([_two_rows(dgpre0, dgpre1), _two_rows(dgpost0, dgpost1), dpw, dps, dws, dbs])
    cw_by_owner = jnp.pad(jnp.transpose(dcw[0:3].reshape(3, N_DEV, 64), (1, 0, 2)), ((0, 0), (0, 5), (0, 64)))
    ln_by_owner = (jnp.pad(dlg.reshape(N_DEV, 1, 128), ((0, 0), (0, 7), (0, 0)))
                   + jnp.pad(dlb.reshape(N_DEV, 1, 128), ((0, 0), (1, 6), (0, 0))))
    gpack = jnp.concatenate([rep_grads, cw_by_owner, ln_by_owner], axis=1)
    w_rep = _pack_rep([pre_norm, post_norm, even_pool_w, even_pool_scale, odd_w_s, odd_b_s])
    m_rep = _pack_rep([m_pre_norm, m_post_norm, m_even_pool_w, m_even_pool_scale, m_odd_w_s, m_odd_b_s])
    v_rep = _pack_rep([v_pre_norm, v_post_norm, v_even_pool_w, v_even_pool_scale, v_odd_w_s, v_odd_b_s])
    m_sh = _pack_small_shard(m_even_conv_w[0], m_odd_ln_g, m_odd_ln_b)
    v_sh = _pack_small_shard(v_even_conv_w[0], v_odd_ln_g, v_odd_ln_b)
    small_out = _small_update(gpack, w_rep, m_rep, v_rep, small_shard, m_sh, v_sh)
    rep_out = [_unpack_rep(a) for a in small_out[0:4]]
    sh_out = [_unpack_small_shard(a) for a in small_out[4:8]]

    def leaves(k, big):
        pre, post, pw, psc, ws, bs = rep_out[k]
        cw, lg, lb = sh_out[k]
        wie_k, woe_k, wio_k, woo_k = big
        return [pre, post, wie_k[None], cw, pw, psc, woe_k[None], wio_k[None], lg, lb, ws, bs, woo_k[None]]

    outs = [loss, gx.reshape(1, S, D)]
    outs += leaves(0, (g_wie, g_woe, g_wio, g_woo))
    outs += leaves(1, (d_wie, d_woe, d_wio, d_woo))
    outs += leaves(2, (nm_wie, nm_woe, nm_wio, nm_woo))
    outs += leaves(3, (nv_wie, nv_woe, nv_wio, nv_woo))
    return tuple(outs)
```

```python
import functools

import jax
import jax.numpy as jnp
from jax import lax
from jax.experimental import pallas as pl
from jax.experimental.pallas import tpu as pltpu

F32 = jnp.float32
BF16 = jnp.bfloat16
MESH = pl.DeviceIdType.MESH

D = 1024
EPS = 1e-6
A_W = 512
B_W = 512
POOL_WINDOWS = (2, 4, 8, 16)
GROUP = 128
N_GROUPS = 4
N_HEADS = 8
HALO = 16
PROJ = 3072
N_DEV = 8
SHARD_IN = PROJ // N_DEV
SHARD_OUT = D // N_DEV

ADAM_LR = 0.001
ADAM_B1 = 0.9
ADAM_B2 = 0.999
ADAM_EPS = 1e-08
ADAM_WD = 0.01
ADAM_STEP = 10

VMEM_LIMIT = 56 * 1024 * 1024


def _params(n_grid=1, vmem=VMEM_LIMIT):
    return pltpu.CompilerParams(dimension_semantics=("arbitrary",) * n_grid, vmem_limit_bytes=vmem)


def _full(shape):
    return pl.BlockSpec(shape, lambda *_: (0,) * len(shape))


def _sigmoid(z):
    return jax.nn.sigmoid(z)


def _dsilu(z, s):
    return s * (1.0 + z * (1.0 - s))


def _dot(a, b):
    return jnp.dot(a, b, preferred_element_type=F32)


def _dot_t0(a, b):
    return lax.dot_general(a, b, (((0,), (0,)), ((), ())), preferred_element_type=F32)


def _dot_t1(a, b):
    return lax.dot_general(a, b, (((1,), (1,)), ((), ())), preferred_element_type=F32)


def _rms(x):
    return lax.rsqrt(jnp.mean(x * x, axis=-1, keepdims=True) + EPS)


def _shift_down(a, k):
    return pltpu.roll(a, k, 0)


def _shift_up(a, k):
    return pltpu.roll(a, a.shape[0] - k, 0)


def _mesh_position():
    return lax.axis_index("x"), lax.axis_index("y"), lax.axis_index("c")


def _index(pos):
    return 4 * pos[0] + 2 * pos[1] + pos[2]


def _peer(pos, d):
    x, y, c = pos
    return (1 - x if d & 4 else x, 1 - y if d & 2 else y, 1 - c if d & 1 else c)


def _remote(src, dst, send_sems, recv_sems, k, to):
    return pltpu.make_async_remote_copy(src_ref=src, dst_ref=dst, send_sem=send_sems.at[k], recv_sem=recv_sems.at[k],
                                        device_id=to, device_id_type=MESH)


class _Gather:
    def __init__(self, srcs, dsts, send_sems, recv_sems):
        x, y, c = _mesh_position()
        n = len(srcs)
        me, sibling = (x, y, c), (x, y, 1 - c)
        chips = [(1 - x, y), (x, 1 - y), (1 - x, 1 - y)]

        def copy(k, t, block, to, src=None):
            slot = dsts[t].at[_index(block)]
            return _remote(slot if src is None else src, slot, send_sems, recv_sems, k * n + t, to)

        tensors = range(n)
        self.local = [pltpu.make_async_copy(srcs[t], dsts[t].at[_index(me)], recv_sems.at[7 * n + t]) for t in tensors]
        self.first = [copy(1 + j, t, me, chip + (c,), srcs[t]) for t in tensors for j, chip in enumerate(chips)]
        self.first += [copy(0, t, me, sibling, srcs[t]) for t in tensors]
        self.ici_in = [copy(1 + j, t, chip + (c,), me) for t in tensors for j, chip in enumerate(chips)]
        self.passed = [copy(4 + j, t, chip + (c,), sibling) for t in tensors for j, chip in enumerate(chips)]
        self.d2d_in = [copy(0, t, sibling, me) for t in tensors]
        self.d2d_in += [copy(4 + j, t, chip + (1 - c,), me) for t in tensors for j, chip in enumerate(chips)]

    def start(self):
        for cp in self.local + self.first:
            cp.start()

    def middle(self):
        for landed, onward in zip(self.ici_in, self.passed):
            landed.wait_recv()
            onward.start()

    def finish(self):
        for cp in self.d2d_in:
            cp.wait_recv()
        for cp in self.first + self.passed:
            cp.wait_send()
        for cp in self.local:
            cp.wait()


class _Scatter:
    def __init__(self, srcs, dsts, send_sems, recv_sems):
        pos = _mesh_position()
        n = len(srcs)
        self.local = [pltpu.make_async_copy(srcs[t].at[_index(pos)], dsts[t].at[0], recv_sems.at[t]) for t in range(n)]
        self.remote = []
        for d in range(1, N_DEV):
            to = _peer(pos, d)
            self.remote += [_remote(srcs[t].at[_index(to)], dsts[t].at[d], send_sems, recv_sems, d * n + t, to)
                            for t in range(n)]

    def start(self):
        for cp in self.local + self.remote:
            cp.start()

    def middle(self):
        pass

    def finish(self):
        for cp in self.local:
            cp.wait()
        for cp in self.remote:
            cp.wait_recv()
        for cp in self.remote:
            cp.wait_send()


HBM_SPEC = pl.BlockSpec(memory_space=pltpu.HBM)
MIDDLE_STEPS_BEFORE_END = 4


def _call(body, *, name, grid, in_specs, out_specs, out_shape, scratch_shapes, args, sides=()):
    params = _params(len(grid))
    if not sides:
        return pl.pallas_call(body, name=name, grid=grid, in_specs=in_specs, out_specs=out_specs, out_shape=out_shape,
                              scratch_shapes=scratch_shapes, compiler_params=params)(*args)
    n_in, n_out, n_scratch = len(in_specs), len(out_specs), len(scratch_shapes)
    counts = [len(srcs) for _, srcs in sides]
    ns = sum(counts)
    total = 1
    for g in grid:
        total *= g
    side_args, side_shapes, side_sems = [], [], []
    for exchange, srcs in sides:
        side_args += list(srcs)
        lead = (N_DEV,) if exchange is _Gather else ()
        side_shapes += [jax.ShapeDtypeStruct(lead + s.shape, s.dtype) for s in srcs]
        side_sems += [pltpu.SemaphoreType.DMA((N_DEV * len(srcs),))] * 2

    def wrapped(*refs):
        ins, side_in = refs[:n_in], refs[n_in:n_in + ns]
        outs = refs[n_in + ns:n_in + ns + n_out]
        side_out = refs[n_in + ns + n_out:n_in + 2 * ns + n_out]
        rest = refs[n_in + 2 * ns + n_out:]
        scratch, sems = rest[:n_scratch], rest[n_scratch:]
        step = pl.program_id(0)
        for axis in range(1, len(grid)):
            step = step * grid[axis] + pl.program_id(axis)

        def exchanges():
            built, at = [], 0
            for k, (exchange, _) in enumerate(sides):
                built.append(exchange(side_in[at:at + counts[k]], side_out[at:at + counts[k]],
                                      sems[2 * k], sems[2 * k + 1]))
                at += counts[k]
            return built

        @pl.when(step == 0)
        def _():
            for ex in exchanges():
                ex.start()

        @pl.when(step == max(total - MIDDLE_STEPS_BEFORE_END, 0))
        def _():
            for ex in exchanges():
                ex.middle()

        body(*ins, *outs, *scratch)

        @pl.when(step == total - 1)
        def _():
            for ex in exchanges():
                ex.finish()

    return pl.pallas_call(
        wrapped, name=name, grid=grid, in_specs=list(in_specs) + [HBM_SPEC] * ns,
        out_specs=list(out_specs) + [HBM_SPEC] * ns, out_shape=list(out_shape) + side_shapes,
        scratch_shapes=list(scratch_shapes) + side_sems, compiler_params=params)(*args, *side_args)


def _norm_matmul(x, g, w, ts, name, sides=()):
    S = x.shape[0]
    N = w.shape[1]

    def body(x_ref, g_ref, w_ref, p_ref, h_ref):
        xv = x_ref[...]
        h = ((xv * _rms(xv)) * g_ref[...]).astype(BF16)
        h_ref[...] = h
        p_ref[...] = _dot(h, w_ref[...]).astype(BF16)

    return _call(
        body, name=name, grid=(S // ts,),
        in_specs=[pl.BlockSpec((ts, D), lambda i: (i, 0)), _full((1, D)), _full((D, N))],
        out_specs=[pl.BlockSpec((ts, N), lambda i: (i, 0)), pl.BlockSpec((ts, D), lambda i: (i, 0))],
        out_shape=[jax.ShapeDtypeStruct((S, N), BF16), jax.ShapeDtypeStruct((S, D), BF16)],
        scratch_shapes=[], args=(x, g, w), sides=sides)


def _even_forward_parts(p_ref, halo_ref, is_first, row0, cw_ref, pw_ref, ts):
    f = lambda a: a.astype(F32)
    xa = f(p_ref[:, 0:512])
    gb = f(p_ref[:, 512:1024])
    gc = f(p_ref[:, 1024:1536])
    za = f(p_ref[:, 1536:2048])
    xp = f(p_ref[:, 2048:2560])
    zp = f(p_ref[:, 2560:3072])
    keep = jnp.where(is_first, 0.0, 1.0).astype(F32)
    ha = gc * xa
    ha_halo = f(halo_ref[:, 1024:1536]) * f(halo_ref[:, 0:512]) * keep
    xp_halo = f(halo_ref[:, 2048:2560]) * keep
    ha_ext = jnp.concatenate([ha_halo, ha], axis=0)
    ha_m1 = _shift_down(ha_ext, 1)[HALO:]
    ha_m2 = _shift_down(ha_ext, 2)[HALO:]
    conv = cw_ref[2:3, :] * ha + cw_ref[1:2, :] * ha_m1 + cw_ref[0:1, :] * ha_m2
    sig_a = _sigmoid(za)
    silu_a = za * sig_a

    xp_ext = jnp.concatenate([xp_halo, xp], axis=0)
    pos = row0 + lax.broadcasted_iota(jnp.int32, (ts, 1), 0)
    pooled, inv_cnt, mixed = [], [], []
    for g, w in enumerate(POOL_WINDOWS):
        cols = slice(g * GROUP, (g + 1) * GROUP)
        s = xp_ext[:, cols]
        k = 1
        while k < w:
            s = s + _shift_down(s, k)
            k *= 2
        inv = 1.0 / jnp.minimum(pos + 1, w).astype(F32)
        pg = s[HALO:] * inv - xp[:, cols]
        pooled.append(pg)
        inv_cnt.append(inv)
        mixed.append(_dot(pg.astype(BF16), pw_ref[g]))
    mixed = jnp.concatenate(mixed, axis=1)
    sig_b = _sigmoid(zp)
    silu_b = zp * sig_b
    return dict(xa=xa, gb=gb, gc=gc, za=za, xp=xp, zp=zp, ha=ha, ha_m1=ha_m1, ha_m2=ha_m2, conv=conv,
                sig_a=sig_a, silu_a=silu_a, pooled=pooled, inv_cnt=inv_cnt, mixed=mixed, sig_b=sig_b,
                silu_b=silu_b)


def _halo_index(ts):
    blocks_per_tile = ts // HALO
    return lambda i: (jnp.maximum(i * blocks_per_tile - 1, 0), 0)


def _even_forward(p, x, cw, pw_b, ps, wout_b, gpost, ts):
    S = x.shape[0]

    def body(p_ref, halo_ref, x_ref, cw_ref, pw_ref, ps_ref, wo_ref, gp_ref, m_ref, x1_ref, mix_ref):
        i = pl.program_id(0)
        fw = _even_forward_parts(p_ref, halo_ref, i == 0, i * ts, cw_ref, pw_ref, ts)
        mix_ref[:, 0:A_W] = (fw["gb"] * fw["conv"] * fw["silu_a"]).astype(BF16)
        mix_ref[:, A_W:D] = (fw["mixed"] * ps_ref[...] * fw["silu_b"]).astype(BF16)
        m = _dot(mix_ref[...], wo_ref[...])
        m_ref[...] = m
        x1_ref[...] = x_ref[...] + (m * _rms(m)) * gp_ref[...]

    tile = pl.BlockSpec((ts, D), lambda i: (i, 0))
    return pl.pallas_call(
        body, name="even_forward", grid=(S // ts,),
        in_specs=[pl.BlockSpec((ts, PROJ), lambda i: (i, 0)), pl.BlockSpec((HALO, PROJ), _halo_index(ts)), tile,
                  _full((3, A_W)), _full((N_GROUPS, GROUP, GROUP)), _full((1, B_W)), _full((D, D)), _full((1, D))],
        out_specs=[tile, tile],
        out_shape=[jax.ShapeDtypeStruct((S, D), F32), jax.ShapeDtypeStruct((S, D), F32)],
        scratch_shapes=[pltpu.VMEM((ts, D), BF16)],
        compiler_params=_params(),
    )(p, p, x, cw, pw_b, ps, wout_b, gpost)


def _even_backward(g1, m0, p, cw, pw_b, pwt_b, ps, woutt_b, gpost, ts, sides=()):
    S = g1.shape[0]
    nt = S // ts

    def body(g_ref, m_ref, p_ref, halo_ref, cw_ref, pw_ref, pwt_ref, ps_ref, wot_ref, gp_ref,
             dp_ref, dwob_ref, dcw_ref, dpw_ref, dps_ref, dgp_ref, carry_ref, mix_ref, dm_ref, dwo_ref):
        step = pl.program_id(0)
        i = nt - 1 - step

        @pl.when(step == 0)
        def _():
            carry_ref[...] = jnp.zeros_like(carry_ref)
            dwo_ref[...] = jnp.zeros_like(dwo_ref)
            dcw_ref[...] = jnp.zeros_like(dcw_ref)
            dpw_ref[...] = jnp.zeros_like(dpw_ref)
            dps_ref[...] = jnp.zeros_like(dps_ref)
            dgp_ref[...] = jnp.zeros_like(dgp_ref)

        g = g_ref[...]
        m = m_ref[...]
        q = _rms(m)
        n = m * q
        dgp_ref[...] += jnp.sum(g * n, axis=0, keepdims=True)
        dn = g * gp_ref[...]
        dm = q * (dn - n * jnp.mean(dn * n, axis=-1, keepdims=True))
        dm_ref[...] = dm.astype(BF16)
        dmix = _dot(dm_ref[...], wot_ref[...])
        dya = dmix[:, 0:A_W]
        dyb = dmix[:, A_W:D]

        fw = _even_forward_parts(p_ref, halo_ref, i == 0, i * ts, cw_ref, pw_ref, ts)
        ps_v = ps_ref[...]
        mix_ref[:, 0:A_W] = (fw["gb"] * fw["conv"] * fw["silu_a"]).astype(BF16)
        mix_ref[:, A_W:D] = (fw["mixed"] * ps_v * fw["silu_b"]).astype(BF16)
        dwo_ref[...] += _dot_t0(mix_ref[...], dm_ref[...])

        t = dya * fw["gb"]
        dconv = t * fw["silu_a"]
        dgb = dya * fw["conv"] * fw["silu_a"]
        dza = t * fw["conv"] * _dsilu(fw["za"], fw["sig_a"])
        dcw_ref[2:3, :] += jnp.sum(dconv * fw["ha"], axis=0, keepdims=True)
        dcw_ref[1:2, :] += jnp.sum(dconv * fw["ha_m1"], axis=0, keepdims=True)
        dcw_ref[0:1, :] += jnp.sum(dconv * fw["ha_m2"], axis=0, keepdims=True)
        dconv_ext = jnp.concatenate([dconv, carry_ref[:, 0:A_W]], axis=0)
        dha = (cw_ref[2:3, :] * dconv + cw_ref[1:2, :] * _shift_up(dconv_ext, 1)[:ts]
               + cw_ref[0:1, :] * _shift_up(dconv_ext, 2)[:ts])
        dgc = dha * fw["xa"]
        dxa = dha * fw["gc"]

        u = dyb * fw["mixed"]
        dps_ref[...] += jnp.sum(u * fw["silu_b"], axis=0, keepdims=True)
        dzp = u * ps_v * _dsilu(fw["zp"], fw["sig_b"])
        dmixed = (dyb * ps_v * fw["silu_b"]).astype(BF16)
        dxp, e_first = [], []
        for gi, w in enumerate(POOL_WINDOWS):
            cols = slice(gi * GROUP, (gi + 1) * GROUP)
            dmg = dmixed[:, cols]
            dpooled = _dot(dmg, pwt_ref[gi])
            dpw_ref[gi] += _dot_t0(fw["pooled"][gi].astype(BF16), dmg)
            e = dpooled * fw["inv_cnt"][gi]
            e_first.append(e[0:HALO])
            s = jnp.concatenate([e, carry_ref[:, A_W + gi * GROUP:A_W + (gi + 1) * GROUP]], axis=0)
            k = 1
            while k < w:
                s = s + _shift_up(s, k)
                k *= 2
            dxp.append(s[:ts] - dpooled)

        carry_ref[:, 0:A_W] = dconv[0:HALO]
        carry_ref[:, A_W:D] = jnp.concatenate(e_first, axis=1)

        dp_ref[:, 0:512] = dxa.astype(BF16)
        dp_ref[:, 512:1024] = dgb.astype(BF16)
        dp_ref[:, 1024:1536] = dgc.astype(BF16)
        dp_ref[:, 1536:2048] = dza.astype(BF16)
        dp_ref[:, 2048:2560] = jnp.concatenate(dxp, axis=1).astype(BF16)
        dp_ref[:, 2560:3072] = dzp.astype(BF16)

        @pl.when(step == nt - 1)
        def _():
            dwob_ref[...] = dwo_ref[...].astype(BF16)

    rev = lambda s: (nt - 1 - s, 0)
    tile = pl.BlockSpec((ts, D), rev)
    bpt = ts // HALO
    halo_map = lambda s: (jnp.maximum((nt - 1 - s) * bpt - 1, 0), 0)
    return _call(
        body, name="even_backward", grid=(nt,),
        in_specs=[tile, tile, pl.BlockSpec((ts, PROJ), rev), pl.BlockSpec((HALO, PROJ), halo_map),
                  _full((3, A_W)), _full((N_GROUPS, GROUP, GROUP)), _full((N_GROUPS, GROUP, GROUP)),
                  _full((1, B_W)), _full((D, D)), _full((1, D))],
        out_specs=[pl.BlockSpec((ts, PROJ), rev), _full((D, D)), _full((8, A_W)),
                   _full((N_GROUPS, GROUP, GROUP)), _full((1, B_W)), _full((1, D))],
        out_shape=[jax.ShapeDtypeStruct((S, PROJ), BF16), jax.ShapeDtypeStruct((D, D), BF16),
                   jax.ShapeDtypeStruct((8, A_W), F32), jax.ShapeDtypeStruct((N_GROUPS, GROUP, GROUP), F32),
                   jax.ShapeDtypeStruct((1, B_W), F32), jax.ShapeDtypeStruct((1, D), F32)],
        scratch_shapes=[pltpu.VMEM((HALO, D), F32), pltpu.VMEM((ts, D), BF16), pltpu.VMEM((ts, D), BF16),
                        pltpu.VMEM((D, D), F32)],
        args=(g1, m0, p, p, cw, pw_b, pwt_b, ps, woutt_b, gpost), sides=sides)


def _odd_forward_parts(p_ref, lg_ref, lb_ref, wt_ref, bfull_ref, vln_ref, sv_ref, r0, nr):
    f = lambda a: a.astype(F32)
    rows = slice(r0, r0 + nr)
    u = f(p_ref[rows, 0:D])
    v = f(p_ref[rows, D:2 * D])
    z = f(p_ref[rows, 2 * D:3 * D])
    mu = jnp.mean(v, axis=-1, keepdims=True)
    vc = v - mu
    rstd = lax.rsqrt(jnp.mean(vc * vc, axis=-1, keepdims=True) + EPS)
    vh = vc * rstd
    vln_ref[rows, :] = (vh * lg_ref[...] + lb_ref[...]).astype(BF16)
    for c in range(nr // GROUP):
        chunk = slice(r0 + c * GROUP, r0 + (c + 1) * GROUP)
        for h in range(N_HEADS):
            cols = slice(h * GROUP, (h + 1) * GROUP)
            sv_ref[chunk, cols] = _dot(wt_ref[h], vln_ref[chunk, cols]) + bfull_ref[h]
    return u, z, vh, rstd


SUB_ROWS = 256


def _odd_forward(p, x1, tgt, lg, lb, wt_b, bfull, wout_b, gpost, ts):
    S = x1.shape[0]

    def body(p_ref, x_ref, t_ref, lg_ref, lb_ref, wt_ref, bfull_ref, wo_ref, gp_ref,
             m_ref, g_ref, loss_ref, vln_ref, sv_ref, y_ref):
        @pl.when(pl.program_id(0) == 0)
        def _():
            loss_ref[...] = jnp.zeros_like(loss_ref)

        loss = None
        for r0 in range(0, ts, SUB_ROWS):
            rows = slice(r0, r0 + SUB_ROWS)
            u, z, _, _ = _odd_forward_parts(p_ref, lg_ref, lb_ref, wt_ref, bfull_ref, vln_ref, sv_ref, r0, SUB_ROWS)
            y_ref[rows, :] = (u * sv_ref[rows, :] * (z * _sigmoid(z))).astype(BF16)
            m = _dot(y_ref[rows, :], wo_ref[...])
            m_ref[rows, :] = m
            x2 = x_ref[rows, :] + (m * _rms(m)) * gp_ref[...]
            err = x2 - t_ref[rows, :]
            g_ref[rows, :] = err * (1.0 / D)
            part = jnp.sum(err * err, axis=0, keepdims=True)
            loss = part if loss is None else loss + part
        loss_ref[...] += loss

    tile = pl.BlockSpec((ts, D), lambda i: (i, 0))
    small = _full((N_HEADS, GROUP, GROUP))
    return pl.pallas_call(
        body, name="odd_forward", grid=(S // ts,),
        in_specs=[pl.BlockSpec((ts, PROJ), lambda i: (i, 0)), tile, tile, _full((1, D)), _full((1, D)),
                  small, small, _full((D, D)), _full((1, D))],
        out_specs=[tile, tile, _full((1, D))],
        out_shape=[jax.ShapeDtypeStruct((S, D), F32), jax.ShapeDtypeStruct((S, D), F32),
                   jax.ShapeDtypeStruct((1, D), F32)],
        scratch_shapes=[pltpu.VMEM((ts, D), BF16), pltpu.VMEM((ts, D), F32), pltpu.VMEM((ts, D), BF16)],
        compiler_params=_params(),
    )(p, x1, tgt, lg, lb, wt_b, bfull, wout_b, gpost)


def _odd_backward(g2, m1, p, lg, lb, wt_b, wtt_b, bfull, woutt_b, gpost, ts):
    S = g2.shape[0]
    nt = S // ts

    def body(g_ref, m_ref, p_ref, lg_ref, lb_ref, wt_ref, wtt_ref, bfull_ref, wot_ref, gp_ref,
             dp_ref, dwob_ref, dws_ref, dbs_ref, dlg_ref, dlb_ref, dgp_ref,
             vln_ref, sv_ref, y_ref, dm_ref, dsv_ref, dsvb_ref, dvln_ref, dsum_ref, dwo_ref):
        step = pl.program_id(0)

        @pl.when(step == 0)
        def _():
            dwo_ref[...] = jnp.zeros_like(dwo_ref)
            dws_ref[...] = jnp.zeros_like(dws_ref)
            dsum_ref[...] = jnp.zeros_like(dsum_ref)
            dlg_ref[...] = jnp.zeros_like(dlg_ref)
            dlb_ref[...] = jnp.zeros_like(dlb_ref)
            dgp_ref[...] = jnp.zeros_like(dgp_ref)

        g = g_ref[...]
        m = m_ref[...]
        q = _rms(m)
        n = m * q
        dgp_ref[...] += jnp.sum(g * n, axis=0, keepdims=True)
        dn = g * gp_ref[...]
        dm = q * (dn - n * jnp.mean(dn * n, axis=-1, keepdims=True))
        dm_ref[...] = dm.astype(BF16)
        dy = _dot(dm_ref[...], wot_ref[...])

        u, z, vh, rstd = _odd_forward_parts(p_ref, lg_ref, lb_ref, wt_ref, bfull_ref, vln_ref, sv_ref, 0, ts)
        sv = sv_ref[...]
        sig = _sigmoid(z)
        sz = z * sig
        y_ref[...] = (u * sv * sz).astype(BF16)
        dwo_ref[...] += _dot_t0(y_ref[...], dm_ref[...])

        t = dy * sz
        du = t * sv
        dsv = t * u
        dz = dy * u * sv * _dsilu(z, sig)
        dsv_ref[...] = dsv
        dsvb_ref[...] = dsv.astype(BF16)

        tril = (lax.broadcasted_iota(jnp.int32, (GROUP, GROUP), 0)
                >= lax.broadcasted_iota(jnp.int32, (GROUP, GROUP), 1))
        for c in range(ts // GROUP):
            rows = slice(c * GROUP, (c + 1) * GROUP)
            dsum_ref[...] += dsv_ref[rows, :]
            for h in range(N_HEADS):
                cols = slice(h * GROUP, (h + 1) * GROUP)
                dvln_ref[rows, cols] = _dot(wtt_ref[h], dsvb_ref[rows, cols])
                dws_ref[h] += jnp.where(tril, _dot_t1(dsvb_ref[rows, cols], vln_ref[rows, cols]), 0.0)

        dvln = dvln_ref[...]
        dlg_ref[...] += jnp.sum(dvln * vh, axis=0, keepdims=True)
        dlb_ref[...] += jnp.sum(dvln, axis=0, keepdims=True)
        dvh = dvln * lg_ref[...]
        dv = rstd * (dvh - jnp.mean(dvh, axis=-1, keepdims=True)
                     - vh * jnp.mean(dvh * vh, axis=-1, keepdims=True))
        dp_ref[:, 0:D] = du.astype(BF16)
        dp_ref[:, D:2 * D] = dv.astype(BF16)
        dp_ref[:, 2 * D:3 * D] = dz.astype(BF16)

        @pl.when(step == nt - 1)
        def _():
            dwob_ref[...] = dwo_ref[...].astype(BF16)
            ones = jnp.ones((8, GROUP), F32)
            for h in range(N_HEADS):
                cols = slice(h * GROUP, (h + 1) * GROUP)
                sums = lax.dot_general(ones, dsum_ref[:, cols], (((1,), (1,)), ((), ())),
                                       precision=lax.Precision.HIGHEST, preferred_element_type=F32)
                dbs_ref[h:h + 1, :] = sums[0:1, :]

    tile = pl.BlockSpec((ts, D), lambda i: (i, 0))
    small = _full((N_HEADS, GROUP, GROUP))
    vec = _full((1, D))
    return pl.pallas_call(
        body, name="odd_backward", grid=(nt,),
        in_specs=[tile, tile, pl.BlockSpec((ts, PROJ), lambda i: (i, 0)), vec, vec, small, small, small,
                  _full((D, D)), vec],
        out_specs=[pl.BlockSpec((ts, PROJ), lambda i: (i, 0)), _full((D, D)), small, _full((N_HEADS, GROUP)),
                   vec, vec, vec],
        out_shape=[jax.ShapeDtypeStruct((S, PROJ), BF16), jax.ShapeDtypeStruct((D, D), BF16),
                   jax.ShapeDtypeStruct((N_HEADS, GROUP, GROUP), F32), jax.ShapeDtypeStruct((N_HEADS, GROUP), F32),
                   jax.ShapeDtypeStruct((1, D), F32), jax.ShapeDtypeStruct((1, D), F32),
                   jax.ShapeDtypeStruct((1, D), F32)],
        scratch_shapes=[pltpu.VMEM((ts, D), BF16), pltpu.VMEM((ts, D), F32), pltpu.VMEM((ts, D), BF16),
                        pltpu.VMEM((ts, D), BF16), pltpu.VMEM((ts, D), F32), pltpu.VMEM((ts, D), BF16),
                        pltpu.VMEM((ts, D), F32), pltpu.VMEM((GROUP, D), F32), pltpu.VMEM((D, D), F32)],
        compiler_params=_params(),
    )(g2, m1, p, lg, lb, wt_b, wtt_b, bfull, woutt_b, gpost)


def _input_backward(dp, wt_b, x, g_res, gpre, ts, name, sides=()):
    S = x.shape[0]

    def body(dp_ref, wt_ref, x_ref, g_ref, gp_ref, dx_ref, dgp_ref):
        @pl.when(pl.program_id(0) == 0)
        def _():
            dgp_ref[...] = jnp.zeros_like(dgp_ref)

        dh = _dot(dp_ref[...], wt_ref[...])
        xv = x_ref[...]
        r = _rms(xv)
        xn = xv * r
        dgp_ref[...] += jnp.sum(dh * xn, axis=0, keepdims=True)
        dxn = dh * gp_ref[...]
        dx_ref[...] = g_ref[...] + r * (dxn - xn * jnp.mean(dxn * xn, axis=-1, keepdims=True))

    tile = pl.BlockSpec((ts, D), lambda i: (i, 0))
    return _call(
        body, name=name, grid=(S // ts,),
        in_specs=[pl.BlockSpec((ts, PROJ), lambda i: (i, 0)), _full((PROJ, D)), tile, tile, _full((1, D))],
        out_specs=[tile, _full((1, D))],
        out_shape=[jax.ShapeDtypeStruct((S, D), F32), jax.ShapeDtypeStruct((1, D), F32)],
        scratch_shapes=[], args=(dp, wt_b, x, g_res, gpre), sides=sides)


def _weight_grad(h, dp, ts, name, sides=()):
    S = h.shape[0]
    ts = min(ts, S)
    nt = S // ts
    panel = 2 * SHARD_IN

    def body(h_ref, dp_ref, out_ref, acc_ref):
        t = pl.program_id(1)

        @pl.when(t == 0)
        def _():
            acc_ref[...] = jnp.zeros_like(acc_ref)

        acc_ref[...] += _dot_t0(h_ref[...], dp_ref[...])

        @pl.when(t == nt - 1)
        def _():
            out_ref[0] = acc_ref[:, 0:SHARD_IN].astype(BF16)
            out_ref[1] = acc_ref[:, SHARD_IN:panel].astype(BF16)

    return _call(
        body, name=name, grid=(PROJ // panel, nt),
        in_specs=[pl.BlockSpec((ts, D), lambda j, t: (t, 0)), pl.BlockSpec((ts, panel), lambda j, t: (t, j))],
        out_specs=[pl.BlockSpec((2, D, SHARD_IN), lambda j, t: (j, 0, 0))],
        out_shape=[jax.ShapeDtypeStruct((N_DEV, D, SHARD_IN), BF16)],
        scratch_shapes=[pltpu.VMEM((D, panel), F32)], args=(h, dp), sides=sides)


N_CHIPS = 4


def _weight_grad_scatter(h, dp, ts, name, sides=()):
    S = h.shape[0]
    ts = min(ts, S)
    nt = S // ts
    panel = 2 * SHARD_IN

    def body(h_ref, dp_ref, parts_ref, acc_ref, send_ref, send_sems, recv_sems):
        s, t = pl.program_id(0), pl.program_id(1)
        x, y, c = _mesh_position()
        flip = N_CHIPS - 1 - s

        @pl.when(t == 0)
        def _():
            acc_ref[...] = jnp.zeros_like(acc_ref)

        acc_ref[...] += _dot_t0(h_ref[...], dp_ref[...])

        @pl.when(t == nt - 1)
        def _():
            for cc in range(2):
                send_ref[s, cc] = acc_ref[:, cc * SHARD_IN:(cc + 1) * SHARD_IN].astype(BF16)
                to = (x ^ (flip >> 1), y ^ (flip & 1), cc)
                d = 2 * flip + (c ^ cc)

                @pl.when(d == 0)
                def _():
                    pltpu.make_async_copy(send_ref.at[s, cc], parts_ref.at[0], recv_sems.at[0]).start()

                @pl.when(d != 0)
                def _():
                    _remote(send_ref.at[s, cc], parts_ref.at[d], send_sems, recv_sems, d, to).start()

        @pl.when((s == N_CHIPS - 1) & (t == nt - 1))
        def _():
            block = send_ref.at[0, 0]
            pltpu.make_async_copy(block, parts_ref.at[0], recv_sems.at[0]).wait()
            arrivals = [_remote(block, parts_ref.at[d], send_sems, recv_sems, d, _peer((x, y, c), d))
                        for d in range(1, N_DEV)]
            for cp in arrivals:
                cp.wait_recv()
            for cp in arrivals:
                cp.wait_send()

    chip_panel = lambda s, t: (t, (2 * lax.axis_index("x") + lax.axis_index("y")) ^ (N_CHIPS - 1 - s))
    sems = pltpu.SemaphoreType.DMA((N_DEV,))
    return _call(
        body, name=name, grid=(N_CHIPS, nt),
        in_specs=[pl.BlockSpec((ts, D), lambda s, t: (t, 0)), pl.BlockSpec((ts, panel), chip_panel)],
        out_specs=[HBM_SPEC], out_shape=[jax.ShapeDtypeStruct((N_DEV, D, SHARD_IN), BF16)],
        scratch_shapes=[pltpu.VMEM((D, panel), F32), pltpu.VMEM((N_CHIPS, 2, D, SHARD_IN), BF16), sems, sems],
        args=(h, dp), sides=sides)


def _gather_first(wie, small, woe, wio, woo):
    def body(wie_ref, sm_ref, woe_ref, wio_ref, woo_ref, wie_out, sm_out, woe_b, wio_b, woo_b,
             cast_ref, send_sems, recv_sems):
        cast_ref[...] = wie_ref[...].astype(BF16)
        gather = _Gather([cast_ref, sm_ref], [wie_out, sm_out], send_sems, recv_sems)
        gather.start()
        woe_b[...] = woe_ref[...].astype(BF16)
        wio_b[...] = wio_ref[...].astype(BF16)
        woo_b[...] = woo_ref[...].astype(BF16)
        gather.middle()
        gather.finish()

    vm = pl.BlockSpec(memory_space=pltpu.VMEM)
    sems = pltpu.SemaphoreType.DMA((N_DEV * 2,))
    return pl.pallas_call(
        body, name="gather_first", in_specs=[vm] * 5, out_specs=[vm] * 5,
        out_shape=[jax.ShapeDtypeStruct((N_DEV,) + wie.shape, BF16), jax.ShapeDtypeStruct((N_DEV,) + small.shape, F32),
                   jax.ShapeDtypeStruct(woe.shape, BF16), jax.ShapeDtypeStruct(wio.shape, BF16),
                   jax.ShapeDtypeStruct(woo.shape, BF16)],
        scratch_shapes=[pltpu.VMEM(wie.shape, BF16), sems, sems],
        compiler_params=pltpu.CompilerParams(vmem_limit_bytes=VMEM_LIMIT),
    )(wie, small, woe, wio, woo)


def _adamw(w, g, m, v):
    m = ADAM_B1 * m + (1.0 - ADAM_B1) * g
    v = ADAM_B2 * v + (1.0 - ADAM_B2) * (g * g)
    m_hat = m / (1.0 - ADAM_B1 ** ADAM_STEP)
    v_hat = v / (1.0 - ADAM_B2 ** ADAM_STEP)
    delta = -ADAM_LR * (m_hat / (jnp.sqrt(v_hat) + ADAM_EPS) + ADAM_WD * w)
    return delta, m, v


def _sum_update(parts, w, m, v, name):
    R, C = w.shape
    rb = min(R, 256)

    def body(p_ref, w_ref, m_ref, v_ref, g_out, d_out, m_out, v_out):
        acc = p_ref[0].astype(F32)
        for d in range(1, N_DEV):
            acc = acc + p_ref[d].astype(F32)
        g_out[...] = acc
        delta, m_new, v_new = _adamw(w_ref[...], acc, m_ref[...], v_ref[...])
        d_out[...] = delta
        m_out[...] = m_new
        v_out[...] = v_new

    blk = pl.BlockSpec((rb, C), lambda i: (i, 0))
    return pl.pallas_call(
        body, name=name, grid=(R // rb,),
        in_specs=[pl.BlockSpec((N_DEV, rb, C), lambda i: (0, i, 0)), blk, blk, blk], out_specs=[blk] * 4,
        out_shape=[jax.ShapeDtypeStruct((R, C), F32)] * 4, compiler_params=_params(),
    )(parts, w, m, v)


def _small_sum_update(partials, row_of, weights, loss_parts):
    n_p, n_w = len(partials), len(weights)

    def body(*refs):
        p_refs = refs[:n_p]
        loss_ref = refs[n_p]
        w_refs = refs[n_p + 1:n_p + 1 + 3 * n_w]
        loss_out = refs[n_p + 1 + 3 * n_w]
        outs = refs[n_p + 2 + 3 * n_w:]

        def total(ref):
            acc = ref[0]
            for d in range(1, N_DEV):
                acc = acc + ref[d]
            return acc

        lsum = jnp.sum(total(loss_ref), axis=-1, keepdims=True) * (0.5 / D)
        loss_out[...] = jnp.broadcast_to(lsum, loss_out.shape)
        for k in range(n_p):
            i, r0 = row_of[k]
            g = total(p_refs[k])
            w_ref, m_ref, v_ref = w_refs[3 * i:3 * i + 3]
            g_out, d_out, m_out, v_out = outs[4 * i:4 * i + 4]
            if g.ndim == 2:
                at = (slice(r0, r0 + g.shape[0]), slice(None))
            else:
                at = (slice(None),) * g.ndim
            delta, m_new, v_new = _adamw(w_ref[at], g, m_ref[at], v_ref[at])
            g_out[at] = g
            d_out[at] = delta
            m_out[at] = m_new
            v_out[at] = v_new

    vm = pl.BlockSpec(memory_space=pltpu.VMEM)
    flat_w = [a for wmv in weights for a in wmv]
    out_shape = [jax.ShapeDtypeStruct((1, 128), F32)]
    for w, _, _ in weights:
        out_shape += [jax.ShapeDtypeStruct(w.shape, F32)] * 4
    return pl.pallas_call(
        body, name="small_update", in_specs=[vm] * (n_p + 1 + 3 * n_w), out_specs=[vm] * len(out_shape),
        out_shape=out_shape, compiler_params=pltpu.CompilerParams(vmem_limit_bytes=VMEM_LIMIT),
    )(*partials, loss_parts, *flat_w)


def _pack_small_shard(cw, lg, lb):
    return (jnp.pad(cw, ((0, 13), (0, 64))) + jnp.pad(lg, ((8, 7), (0, 0))) + jnp.pad(lb, ((9, 6), (0, 0))))


TS_FWD = 512
TS_BWD = 256
TS_WGRAD = 2048


def kernel(x, pre_norm, post_norm, even_w_in, even_conv_w, even_pool_w, even_pool_scale, even_w_out, odd_w_in, odd_ln_g, odd_ln_b, odd_w_s, odd_b_s, odd_w_out, loss_target, m_pre_norm, m_post_norm, m_even_w_in, m_even_conv_w, m_even_pool_w, m_even_pool_scale, m_even_w_out, m_odd_w_in, m_odd_ln_g, m_odd_ln_b, m_odd_w_s, m_odd_b_s, m_odd_w_out, v_pre_norm, v_post_norm, v_even_w_in, v_even_conv_w, v_even_pool_w, v_even_pool_scale, v_even_w_out, v_odd_w_in, v_odd_ln_g, v_odd_ln_b, v_odd_w_s, v_odd_b_s, v_odd_w_out):
    S = x.shape[1]
    xs = x.reshape(S, D)
    tgt = loss_target.reshape(S, D)

    small_shard = _pack_small_shard(even_conv_w[0], odd_ln_g, odd_ln_b)
    wie_g, small_g, woe_b, wio_b, woo_b = _gather_first(even_w_in[0], small_shard, even_w_out[0], odd_w_in[0],
                                                        odd_w_out[0])
    wie = jnp.transpose(wie_g, (1, 0, 2)).reshape(D, PROJ)
    wie_t = jnp.transpose(wie_g, (0, 2, 1)).reshape(PROJ, D)
    conv_w = jnp.transpose(small_g[:, 0:3, 0:64], (1, 0, 2)).reshape(3, A_W)
    ln_g = small_g[:, 8, :].reshape(1, D)
    ln_b = small_g[:, 9, :].reshape(1, D)

    pool_w_b = even_pool_w[0].astype(BF16)
    pool_wt_b = jnp.swapaxes(even_pool_w[0], 1, 2).astype(BF16)
    ws_tril = jnp.tril(odd_w_s[0])
    ws_b = ws_tril.astype(BF16)
    wst_b = jnp.swapaxes(ws_tril, 1, 2).astype(BF16)
    b_full = jnp.broadcast_to(odd_b_s[0][:, :, None], (N_HEADS, GROUP, GROUP))
    gpre0, gpre1 = pre_norm[0:1], pre_norm[1:2]
    gpost0, gpost1 = post_norm[0:1], post_norm[1:2]

    p0, h0, woe_g, wio_g, woo_g = _norm_matmul(xs, gpre0, wie, TS_FWD, "even_proj",
                                               sides=[(_Gather, [woe_b, wio_b, woo_b])])
    wio = jnp.transpose(wio_g, (1, 0, 2)).reshape(D, PROJ)
    wio_t = jnp.transpose(wio_g, (0, 2, 1)).reshape(PROJ, D)
    woe = woe_g.reshape(D, D)
    woe_t = woe.T
    woo = woo_g.reshape(D, D)
    woo_t = woo.T
    m0, x1 = _even_forward(p0, xs, conv_w, pool_w_b, even_pool_scale, woe, gpost0, TS_FWD)
    p1, h1 = _norm_matmul(x1, gpre1, wio, TS_FWD, "odd_proj")
    m1, g2, loss_vec = _odd_forward(p1, x1, tgt, ln_g, ln_b, ws_b, b_full, woo, gpost1, TS_FWD)

    dp1, dwoo, dws, dbs, dlg, dlb, dgpost1 = _odd_backward(g2, m1, p1, ln_g, ln_b, ws_b, wst_b, b_full, woo_t,
                                                           gpost1, TS_BWD)
    by_owner = lambda dwo: dwo.reshape(N_DEV, SHARD_OUT, D)
    g1, dgpre1, dwoo_parts = _input_backward(dp1, wio_t, x1, g2, gpre1, TS_FWD, "odd_input_backward",
                                             sides=[(_Scatter, [by_owner(dwoo)])])
    dwio, = _weight_grad(h1, dp1, TS_WGRAD, "odd_weight_grad")
    (dp0, dwoe, dcw, dpw, dps, dgpost0, dwio_parts, dlg_parts, dlb_parts,
     dws_g, dbs_g, gpost1_g, gpre1_g, loss_g) = _even_backward(
        g1, m0, p0, conv_w, pool_w_b, pool_wt_b, even_pool_scale, woe_t, gpost0, TS_BWD,
        sides=[(_Scatter, [dwio, dlg.reshape(N_DEV, 1, 128), dlb.reshape(N_DEV, 1, 128)]),
               (_Gather, [dws, dbs, dgpost1, dgpre1, loss_vec])])
    gx, dgpre0 = _input_backward(dp0, wie_t, xs, g1, gpre0, TS_FWD, "even_input_backward")
    dcw_by_owner = jnp.transpose(dcw[0:3].reshape(3, N_DEV, 64), (1, 0, 2))
    dwie_parts, dwoe_parts, dcw_parts, dpw_g, dps_g, gpost0_g, gpre0_g = _weight_grad_scatter(
        h0, dp0, TS_WGRAD, "even_weight_grad",
        sides=[(_Scatter, [by_owner(dwoe), dcw_by_owner]), (_Gather, [dpw, dps, dgpost0, dgpre0])])

    g_wie, d_wie, nm_wie, nv_wie = _sum_update(dwie_parts, even_w_in[0], m_even_w_in[0], v_even_w_in[0],
                                               "update_even_w_in")
    g_wio, d_wio, nm_wio, nv_wio = _sum_update(dwio_parts, odd_w_in[0], m_odd_w_in[0], v_odd_w_in[0],
                                               "update_odd_w_in")
    g_woe, d_woe, nm_woe, nv_woe = _sum_update(dwoe_parts, even_w_out[0], m_even_w_out[0], v_even_w_out[0],
                                               "update_even_w_out")
    g_woo, d_woo, nm_woo, nv_woo = _sum_update(dwoo_parts, odd_w_out[0], m_odd_w_out[0], v_odd_w_out[0],
                                               "update_odd_w_out")

    partials = [gpre0_g, gpre1_g, gpost0_g, gpost1_g, dpw_g, dps_g, dws_g, dbs_g, dcw_parts, dlg_parts, dlb_parts]
    row_of = [(0, 0), (0, 1), (1, 0), (1, 1), (2, 0), (3, 0), (4, 0), (5, 0), (6, 0), (7, 0), (8, 0)]
    weights = [(pre_norm, m_pre_norm, v_pre_norm), (post_norm, m_post_norm, v_post_norm),
               (even_pool_w[0], m_even_pool_w[0], v_even_pool_w[0]),
               (even_pool_scale, m_even_pool_scale, v_even_pool_scale),
               (odd_w_s[0], m_odd_w_s[0], v_odd_w_s[0]), (odd_b_s[0], m_odd_b_s[0], v_odd_b_s[0]),
               (even_conv_w[0], m_even_conv_w[0], v_even_conv_w[0]),
               (odd_ln_g, m_odd_ln_g, v_odd_ln_g), (odd_ln_b, m_odd_ln_b, v_odd_ln_b)]
    small = _small_sum_update(partials, row_of, weights, loss_g)
    loss = small[0][0, 0]

    def leaves(k, big):
        pre, post, pw, psc, ws, bs, cw, lg, lb = [small[1 + 4 * i + k] for i in range(len(weights))]
        wie_k, woe_k, wio_k, woo_k = big
        return [pre, post, wie_k[None], cw[None], pw[None], psc, woe_k[None], wio_k[None], lg, lb, ws[None], bs[None],
                woo_k[None]]

    outs = [loss, gx.reshape(1, S, D)]
    outs += leaves(0, (g_wie, g_woe, g_wio, g_woo))
    outs += leaves(1, (d_wie, d_woe, d_wio, d_woo))
    outs += leaves(2, (nm_wie, nm_woe, nm_wio, nm_woo))
    outs += leaves(3, (nv_wie, nv_woe, nv_wio, nv_woo))
    return tuple(outs)
```

```python
import functools

import jax
import jax.numpy as jnp
from jax import lax
from jax.experimental import pallas as pl
from jax.experimental.pallas import tpu as pltpu

F32 = jnp.float32
BF16 = jnp.bfloat16
MESH = pl.DeviceIdType.MESH

D = 1024
EPS = 1e-6
A_W = 512
B_W = 512
POOL_WINDOWS = (2, 4, 8, 16)
GROUP = 128
N_GROUPS = 4
N_HEADS = 8
HALO = 16
PROJ = 3072
N_DEV = 8
SHARD_IN = PROJ // N_DEV
SHARD_OUT = D // N_DEV

ADAM_LR = 0.001
ADAM_B1 = 0.9
ADAM_B2 = 0.999
ADAM_EPS = 1e-08
ADAM_WD = 0.01
ADAM_STEP = 10

VMEM_LIMIT = 56 * 1024 * 1024


def _params(n_grid=1, vmem=VMEM_LIMIT):
    return pltpu.CompilerParams(dimension_semantics=("arbitrary",) * n_grid, vmem_limit_bytes=vmem)


def _full(shape):
    return pl.BlockSpec(shape, lambda *_: (0,) * len(shape))


def _sigmoid(z):
    return jax.nn.sigmoid(z)


def _dsilu(z, s):
    return s * (1.0 + z * (1.0 - s))


def _dot(a, b):
    return jnp.dot(a, b, preferred_element_type=F32)


def _dot_t0(a, b):
    return lax.dot_general(a, b, (((0,), (0,)), ((), ())), preferred_element_type=F32)


def _dot_t1(a, b):
    return lax.dot_general(a, b, (((1,), (1,)), ((), ())), preferred_element_type=F32)


def _rms(x):
    return lax.rsqrt(jnp.mean(x * x, axis=-1, keepdims=True) + EPS)


def _shift_down(a, k):
    return pltpu.roll(a, k, 0)


def _shift_up(a, k):
    return pltpu.roll(a, a.shape[0] - k, 0)


def _mesh_position():
    return lax.axis_index("x"), lax.axis_index("y"), lax.axis_index("c")


def _index(pos):
    return 4 * pos[0] + 2 * pos[1] + pos[2]


def _peer(pos, d):
    x, y, c = pos
    return (1 - x if d & 4 else x, 1 - y if d & 2 else y, 1 - c if d & 1 else c)


def _remote(src, dst, send_sems, recv_sems, k, to):
    return pltpu.make_async_remote_copy(src_ref=src, dst_ref=dst, send_sem=send_sems.at[k], recv_sem=recv_sems.at[k],
                                        device_id=to, device_id_type=MESH)


class _Gather:
    def __init__(self, srcs, dsts, send_sems, recv_sems):
        x, y, c = _mesh_position()
        n = len(srcs)
        me, sibling = (x, y, c), (x, y, 1 - c)
        chips = [(1 - x, y), (x, 1 - y), (1 - x, 1 - y)]

        def copy(k, t, block, to, src=None):
            slot = dsts[t].at[_index(block)]
            return _remote(slot if src is None else src, slot, send_sems, recv_sems, k * n + t, to)

        tensors = range(n)
        self.local = [pltpu.make_async_copy(srcs[t], dsts[t].at[_index(me)], recv_sems.at[7 * n + t]) for t in tensors]
        self.first = [copy(1 + j, t, me, chip + (c,), srcs[t]) for t in tensors for j, chip in enumerate(chips)]
        self.first += [copy(0, t, me, sibling, srcs[t]) for t in tensors]
        self.ici_in = [copy(1 + j, t, chip + (c,), me) for t in tensors for j, chip in enumerate(chips)]
        self.passed = [copy(4 + j, t, chip + (c,), sibling) for t in tensors for j, chip in enumerate(chips)]
        self.d2d_in = [copy(0, t, sibling, me) for t in tensors]
        self.d2d_in += [copy(4 + j, t, chip + (1 - c,), me) for t in tensors for j, chip in enumerate(chips)]

    def start(self):
        for cp in self.local + self.first:
            cp.start()

    def middle(self):
        for landed, onward in zip(self.ici_in, self.passed):
            landed.wait_recv()
            onward.start()

    def finish(self):
        for cp in self.d2d_in:
            cp.wait_recv()
        for cp in self.first + self.passed:
            cp.wait_send()
        for cp in self.local:
            cp.wait()


class _Scatter:
    def __init__(self, srcs, dsts, send_sems, recv_sems):
        pos = _mesh_position()
        n = len(srcs)
        self.local = [pltpu.make_async_copy(srcs[t].at[_index(pos)], dsts[t].at[0], recv_sems.at[t]) for t in range(n)]
        self.remote = []
        for d in range(1, N_DEV):
            to = _peer(pos, d)
            self.remote += [_remote(srcs[t].at[_index(to)], dsts[t].at[d], send_sems, recv_sems, d * n + t, to)
                            for t in range(n)]

    def start(self):
        for cp in self.local + self.remote:
            cp.start()

    def middle(self):
        pass

    def finish(self):
        for cp in self.local:
            cp.wait()
        for cp in self.remote:
            cp.wait_recv()
        for cp in self.remote:
            cp.wait_send()


HBM_SPEC = pl.BlockSpec(memory_space=pltpu.HBM)
MIDDLE_STEPS_BEFORE_END = 4


def _call(body, *, name, grid, in_specs, out_specs, out_shape, scratch_shapes, args, sides=()):
    params = _params(len(grid))
    if not sides:
        return pl.pallas_call(body, name=name, grid=grid, in_specs=in_specs, out_specs=out_specs, out_shape=out_shape,
                              scratch_shapes=scratch_shapes, compiler_params=params)(*args)
    n_in, n_out, n_scratch = len(in_specs), len(out_specs), len(scratch_shapes)
    counts = [len(srcs) for _, srcs in sides]
    ns = sum(counts)
    total = 1
    for g in grid:
        total *= g
    side_args, side_shapes, side_sems = [], [], []
    for exchange, srcs in sides:
        side_args += list(srcs)
        lead = (N_DEV,) if exchange is _Gather else ()
        side_shapes += [jax.ShapeDtypeStruct(lead + s.shape, s.dtype) for s in srcs]
        side_sems += [pltpu.SemaphoreType.DMA((N_DEV * len(srcs),))] * 2

    def wrapped(*refs):
        ins, side_in = refs[:n_in], refs[n_in:n_in + ns]
        outs = refs[n_in + ns:n_in + ns + n_out]
        side_out = refs[n_in + ns + n_out:n_in + 2 * ns + n_out]
        rest = refs[n_in + 2 * ns + n_out:]
        scratch, sems = rest[:n_scratch], rest[n_scratch:]
        step = pl.program_id(0)
        for axis in range(1, len(grid)):
            step = step * grid[axis] + pl.program_id(axis)

        def exchanges():
            built, at = [], 0
            for k, (exchange, _) in enumerate(sides):
                built.append(exchange(side_in[at:at + counts[k]], side_out[at:at + counts[k]],
                                      sems[2 * k], sems[2 * k + 1]))
                at += counts[k]
            return built

        @pl.when(step == 0)
        def _():
            for ex in exchanges():
                ex.start()

        @pl.when(step == max(total - MIDDLE_STEPS_BEFORE_END, 0))
        def _():
            for ex in exchanges():
                ex.middle()

        body(*ins, *outs, *scratch)

        @pl.when(step == total - 1)
        def _():
            for ex in exchanges():
                ex.finish()

    return pl.pallas_call(
        wrapped, name=name, grid=grid, in_specs=list(in_specs) + [HBM_SPEC] * ns,
        out_specs=list(out_specs) + [HBM_SPEC] * ns, out_shape=list(out_shape) + side_shapes,
        scratch_shapes=list(scratch_shapes) + side_sems, compiler_params=params)(*args, *side_args)


def _norm_matmul(x, g, w, ts, name, sides=()):
    S = x.shape[0]
    N = w.shape[1]

    def body(x_ref, g_ref, w_ref, p_ref, h_ref):
        xv = x_ref[...]
        h = ((xv * _rms(xv)) * g_ref[...]).astype(BF16)
        h_ref[...] = h
        p_ref[...] = _dot(h, w_ref[...]).astype(BF16)

    return _call(
        body, name=name, grid=(S // ts,),
        in_specs=[pl.BlockSpec((ts, D), lambda i: (i, 0)), _full((1, D)), _full((D, N))],
        out_specs=[pl.BlockSpec((ts, N), lambda i: (i, 0)), pl.BlockSpec((ts, D), lambda i: (i, 0))],
        out_shape=[jax.ShapeDtypeStruct((S, N), BF16), jax.ShapeDtypeStruct((S, D), BF16)],
        scratch_shapes=[], args=(x, g, w), sides=sides)


def _even_forward_parts(p_ref, halo_ref, is_first, row0, cw_ref, pw_ref, ts):
    f = lambda a: a.astype(F32)
    xa = f(p_ref[:, 0:512])
    gb = f(p_ref[:, 512:1024])
    gc = f(p_ref[:, 1024:1536])
    za = f(p_ref[:, 1536:2048])
    xp = f(p_ref[:, 2048:2560])
    zp = f(p_ref[:, 2560:3072])
    keep = jnp.where(is_first, 0.0, 1.0).astype(F32)
    ha = gc * xa
    ha_halo = f(halo_ref[:, 1024:1536]) * f(halo_ref[:, 0:512]) * keep
    xp_halo = f(halo_ref[:, 2048:2560]) * keep
    ha_ext = jnp.concatenate([ha_halo, ha], axis=0)
    ha_m1 = _shift_down(ha_ext, 1)[HALO:]
    ha_m2 = _shift_down(ha_ext, 2)[HALO:]
    conv = cw_ref[2:3, :] * ha + cw_ref[1:2, :] * ha_m1 + cw_ref[0:1, :] * ha_m2
    sig_a = _sigmoid(za)
    silu_a = za * sig_a

    xp_ext = jnp.concatenate([xp_halo, xp], axis=0)
    pos = row0 + lax.broadcasted_iota(jnp.int32, (ts, 1), 0)
    pooled, inv_cnt, mixed = [], [], []
    for g, w in enumerate(POOL_WINDOWS):
        cols = slice(g * GROUP, (g + 1) * GROUP)
        s = xp_ext[:, cols]
        k = 1
        while k < w:
            s = s + _shift_down(s, k)
            k *= 2
        inv = 1.0 / jnp.minimum(pos + 1, w).astype(F32)
        pg = s[HALO:] * inv - xp[:, cols]
        pooled.append(pg)
        inv_cnt.append(inv)
        mixed.append(_dot(pg.astype(BF16), pw_ref[g]))
    mixed = jnp.concatenate(mixed, axis=1)
    sig_b = _sigmoid(zp)
    silu_b = zp * sig_b
    return dict(xa=xa, gb=gb, gc=gc, za=za, xp=xp, zp=zp, ha=ha, ha_m1=ha_m1, ha_m2=ha_m2, conv=conv,
                sig_a=sig_a, silu_a=silu_a, pooled=pooled, inv_cnt=inv_cnt, mixed=mixed, sig_b=sig_b,
                silu_b=silu_b)


def _halo_index(ts):
    blocks_per_tile = ts // HALO
    return lambda i: (jnp.maximum(i * blocks_per_tile - 1, 0), 0)


def _even_forward(p, x, cw, pw_b, ps, wout_b, gpost, ts):
    S = x.shape[0]

    def body(p_ref, halo_ref, x_ref, cw_ref, pw_ref, ps_ref, wo_ref, gp_ref, m_ref, x1_ref, mix_ref):
        i = pl.program_id(0)
        fw = _even_forward_parts(p_ref, halo_ref, i == 0, i * ts, cw_ref, pw_ref, ts)
        mix_ref[:, 0:A_W] = (fw["gb"] * fw["conv"] * fw["silu_a"]).astype(BF16)
        mix_ref[:, A_W:D] = (fw["mixed"] * ps_ref[...] * fw["silu_b"]).astype(BF16)
        m = _dot(mix_ref[...], wo_ref[...])
        m_ref[...] = m
        x1_ref[...] = x_ref[...] + (m * _rms(m)) * gp_ref[...]

    tile = pl.BlockSpec((ts, D), lambda i: (i, 0))
    return pl.pallas_call(
        body, name="even_forward", grid=(S // ts,),
        in_specs=[pl.BlockSpec((ts, PROJ), lambda i: (i, 0)), pl.BlockSpec((HALO, PROJ), _halo_index(ts)), tile,
                  _full((3, A_W)), _full((N_GROUPS, GROUP, GROUP)), _full((1, B_W)), _full((D, D)), _full((1, D))],
        out_specs=[tile, tile],
        out_shape=[jax.ShapeDtypeStruct((S, D), F32), jax.ShapeDtypeStruct((S, D), F32)],
        scratch_shapes=[pltpu.VMEM((ts, D), BF16)],
        compiler_params=_params(),
    )(p, p, x, cw, pw_b, ps, wout_b, gpost)


def _even_backward(g1, m0, p, cw, pw_b, pwt_b, ps, woutt_b, gpost, ts, sides=()):
    S = g1.shape[0]
    nt = S // ts

    def body(g_ref, m_ref, p_ref, halo_ref, cw_ref, pw_ref, pwt_ref, ps_ref, wot_ref, gp_ref,
             dp_ref, dwob_ref, dcw_ref, dpw_ref, dps_ref, dgp_ref, carry_ref, mix_ref, dm_ref, dwo_ref):
        step = pl.program_id(0)
        i = nt - 1 - step

        @pl.when(step == 0)
        def _():
            carry_ref[...] = jnp.zeros_like(carry_ref)
            dwo_ref[...] = jnp.zeros_like(dwo_ref)
            dcw_ref[...] = jnp.zeros_like(dcw_ref)
            dpw_ref[...] = jnp.zeros_like(dpw_ref)
            dps_ref[...] = jnp.zeros_like(dps_ref)
            dgp_ref[...] = jnp.zeros_like(dgp_ref)

        g = g_ref[...]
        m = m_ref[...]
        q = _rms(m)
        n = m * q
        dgp_ref[...] += jnp.sum(g * n, axis=0, keepdims=True)
        dn = g * gp_ref[...]
        dm = q * (dn - n * jnp.mean(dn * n, axis=-1, keepdims=True))
        dm_ref[...] = dm.astype(BF16)
        dmix = _dot(dm_ref[...], wot_ref[...])
        dya = dmix[:, 0:A_W]
        dyb = dmix[:, A_W:D]

        fw = _even_forward_parts(p_ref, halo_ref, i == 0, i * ts, cw_ref, pw_ref, ts)
        ps_v = ps_ref[...]
        mix_ref[:, 0:A_W] = (fw["gb"] * fw["conv"] * fw["silu_a"]).astype(BF16)
        mix_ref[:, A_W:D] = (fw["mixed"] * ps_v * fw["silu_b"]).astype(BF16)
        dwo_ref[...] += _dot_t0(mix_ref[...], dm_ref[...])

        t = dya * fw["gb"]
        dconv = t * fw["silu_a"]
        dgb = dya * fw["conv"] * fw["silu_a"]
        dza = t * fw["conv"] * _dsilu(fw["za"], fw["sig_a"])
        dcw_ref[2:3, :] += jnp.sum(dconv * fw["ha"], axis=0, keepdims=True)
        dcw_ref[1:2, :] += jnp.sum(dconv * fw["ha_m1"], axis=0, keepdims=True)
        dcw_ref[0:1, :] += jnp.sum(dconv * fw["ha_m2"], axis=0, keepdims=True)
        dconv_ext = jnp.concatenate([dconv, carry_ref[:, 0:A_W]], axis=0)
        dha = (cw_ref[2:3, :] * dconv + cw_ref[1:2, :] * _shift_up(dconv_ext, 1)[:ts]
               + cw_ref[0:1, :] * _shift_up(dconv_ext, 2)[:ts])
        dgc = dha * fw["xa"]
        dxa = dha * fw["gc"]

        u = dyb * fw["mixed"]
        dps_ref[...] += jnp.sum(u * fw["silu_b"], axis=0, keepdims=True)
        dzp = u * ps_v * _dsilu(fw["zp"], fw["sig_b"])
        dmixed = (dyb * ps_v * fw["silu_b"]).astype(BF16)
        dxp, e_first = [], []
        for gi, w in enumerate(POOL_WINDOWS):
            cols = slice(gi * GROUP, (gi + 1) * GROUP)
            dmg = dmixed[:, cols]
            dpooled = _dot(dmg, pwt_ref[gi])
            dpw_ref[gi] += _dot_t0(fw["pooled"][gi].astype(BF16), dmg)
            e = dpooled * fw["inv_cnt"][gi]
            e_first.append(e[0:HALO])
            s = jnp.concatenate([e, carry_ref[:, A_W + gi * GROUP:A_W + (gi + 1) * GROUP]], axis=0)
            k = 1
            while k < w:
                s = s + _shift_up(s, k)
                k *= 2
            dxp.append(s[:ts] - dpooled)

        carry_ref[:, 0:A_W] = dconv[0:HALO]
        carry_ref[:, A_W:D] = jnp.concatenate(e_first, axis=1)

        dp_ref[:, 0:512] = dxa.astype(BF16)
        dp_ref[:, 512:1024] = dgb.astype(BF16)
        dp_ref[:, 1024:1536] = dgc.astype(BF16)
        dp_ref[:, 1536:2048] = dza.astype(BF16)
        dp_ref[:, 2048:2560] = jnp.concatenate(dxp, axis=1).astype(BF16)
        dp_ref[:, 2560:3072] = dzp.astype(BF16)

        @pl.when(step == nt - 1)
        def _():
            dwob_ref[...] = dwo_ref[...].astype(BF16)

    rev = lambda s: (nt - 1 - s, 0)
    tile = pl.BlockSpec((ts, D), rev)
    bpt = ts // HALO
    halo_map = lambda s: (jnp.maximum((nt - 1 - s) * bpt - 1, 0), 0)
    return _call(
        body, name="even_backward", grid=(nt,),
        in_specs=[tile, tile, pl.BlockSpec((ts, PROJ), rev), pl.BlockSpec((HALO, PROJ), halo_map),
                  _full((3, A_W)), _full((N_GROUPS, GROUP, GROUP)), _full((N_GROUPS, GROUP, GROUP)),
                  _full((1, B_W)), _full((D, D)), _full((1, D))],
        out_specs=[pl.BlockSpec((ts, PROJ), rev), _full((D, D)), _full((8, A_W)),
                   _full((N_GROUPS, GROUP, GROUP)), _full((1, B_W)), _full((1, D))],
        out_shape=[jax.ShapeDtypeStruct((S, PROJ), BF16), jax.ShapeDtypeStruct((D, D), BF16),
                   jax.ShapeDtypeStruct((8, A_W), F32), jax.ShapeDtypeStruct((N_GROUPS, GROUP, GROUP), F32),
                   jax.ShapeDtypeStruct((1, B_W), F32), jax.ShapeDtypeStruct((1, D), F32)],
        scratch_shapes=[pltpu.VMEM((HALO, D), F32), pltpu.VMEM((ts, D), BF16), pltpu.VMEM((ts, D), BF16),
                        pltpu.VMEM((D, D), F32)],
        args=(g1, m0, p, p, cw, pw_b, pwt_b, ps, woutt_b, gpost), sides=sides)


def _odd_forward_parts(p_ref, lg_ref, lb_ref, wt_ref, bfull_ref, vln_ref, sv_ref, r0, nr):
    f = lambda a: a.astype(F32)
    rows = slice(r0, r0 + nr)
    u = f(p_ref[rows, 0:D])
    v = f(p_ref[rows, D:2 * D])
    z = f(p_ref[rows, 2 * D:3 * D])
    mu = jnp.mean(v, axis=-1, keepdims=True)
    vc = v - mu
    rstd = lax.rsqrt(jnp.mean(vc * vc, axis=-1, keepdims=True) + EPS)
    vh = vc * rstd
    vln_ref[rows, :] = (vh * lg_ref[...] + lb_ref[...]).astype(BF16)
    for c in range(nr // GROUP):
        chunk = slice(r0 + c * GROUP, r0 + (c + 1) * GROUP)
        for h in range(N_HEADS):
            cols = slice(h * GROUP, (h + 1) * GROUP)
            sv_ref[chunk, cols] = _dot(wt_ref[h], vln_ref[chunk, cols]) + bfull_ref[h]
    return u, z, vh, rstd


SUB_ROWS = 256


def _odd_forward(p, x1, tgt, lg, lb, wt_b, bfull, wout_b, gpost, ts):
    S = x1.shape[0]

    def body(p_ref, x_ref, t_ref, lg_ref, lb_ref, wt_ref, bfull_ref, wo_ref, gp_ref,
             m_ref, g_ref, loss_ref, vln_ref, sv_ref, y_ref):
        @pl.when(pl.program_id(0) == 0)
        def _():
            loss_ref[...] = jnp.zeros_like(loss_ref)

        loss = None
        for r0 in range(0, ts, SUB_ROWS):
            rows = slice(r0, r0 + SUB_ROWS)
            u, z, _, _ = _odd_forward_parts(p_ref, lg_ref, lb_ref, wt_ref, bfull_ref, vln_ref, sv_ref, r0, SUB_ROWS)
            y_ref[rows, :] = (u * sv_ref[rows, :] * (z * _sigmoid(z))).astype(BF16)
            m = _dot(y_ref[rows, :], wo_ref[...])
            m_ref[rows, :] = m
            x2 = x_ref[rows, :] + (m * _rms(m)) * gp_ref[...]
            err = x2 - t_ref[rows, :]
            g_ref[rows, :] = err * (1.0 / D)
            part = jnp.sum(err * err, axis=0, keepdims=True)
            loss = part if loss is None else loss + part
        loss_ref[...] += loss

    tile = pl.BlockSpec((ts, D), lambda i: (i, 0))
    small = _full((N_HEADS, GROUP, GROUP))
    return pl.pallas_call(
        body, name="odd_forward", grid=(S // ts,),
        in_specs=[pl.BlockSpec((ts, PROJ), lambda i: (i, 0)), tile, tile, _full((1, D)), _full((1, D)),
                  small, small, _full((D, D)), _full((1, D))],
        out_specs=[tile, tile, _full((1, D))],
        out_shape=[jax.ShapeDtypeStruct((S, D), F32), jax.ShapeDtypeStruct((S, D), F32),
                   jax.ShapeDtypeStruct((1, D), F32)],
        scratch_shapes=[pltpu.VMEM((ts, D), BF16), pltpu.VMEM((ts, D), F32), pltpu.VMEM((ts, D), BF16)],
        compiler_params=_params(),
    )(p, x1, tgt, lg, lb, wt_b, bfull, wout_b, gpost)


def _odd_backward(g2, m1, p, lg, lb, wt_b, wtt_b, bfull, woutt_b, gpost, ts):
    S = g2.shape[0]
    nt = S // ts

    def body(g_ref, m_ref, p_ref, lg_ref, lb_ref, wt_ref, wtt_ref, bfull_ref, wot_ref, gp_ref,
             dp_ref, dwob_ref, dws_ref, dbs_ref, dlg_ref, dlb_ref, dgp_ref,
             vln_ref, sv_ref, y_ref, dm_ref, dsv_ref, dsvb_ref, dvln_ref, dsum_ref, dwo_ref):
        step = pl.program_id(0)

        @pl.when(step == 0)
        def _():
            dwo_ref[...] = jnp.zeros_like(dwo_ref)
            dws_ref[...] = jnp.zeros_like(dws_ref)
            dsum_ref[...] = jnp.zeros_like(dsum_ref)
            dlg_ref[...] = jnp.zeros_like(dlg_ref)
            dlb_ref[...] = jnp.zeros_like(dlb_ref)
            dgp_ref[...] = jnp.zeros_like(dgp_ref)

        g = g_ref[...]
        m = m_ref[...]
        q = _rms(m)
        n = m * q
        dgp_ref[...] += jnp.sum(g * n, axis=0, keepdims=True)
        dn = g * gp_ref[...]
        dm = q * (dn - n * jnp.mean(dn * n, axis=-1, keepdims=True))
        dm_ref[...] = dm.astype(BF16)
        dy = _dot(dm_ref[...], wot_ref[...])

        u, z, vh, rstd = _odd_forward_parts(p_ref, lg_ref, lb_ref, wt_ref, bfull_ref, vln_ref, sv_ref, 0, ts)
        sv = sv_ref[...]
        sig = _sigmoid(z)
        sz = z * sig
        y_ref[...] = (u * sv * sz).astype(BF16)
        dwo_ref[...] += _dot_t0(y_ref[...], dm_ref[...])

        t = dy * sz
        du = t * sv
        dsv = t * u
        dz = dy * u * sv * _dsilu(z, sig)
        dsv_ref[...] = dsv
        dsvb_ref[...] = dsv.astype(BF16)

        tril = (lax.broadcasted_iota(jnp.int32, (GROUP, GROUP), 0)
                >= lax.broadcasted_iota(jnp.int32, (GROUP, GROUP), 1))
        for c in range(ts // GROUP):
            rows = slice(c * GROUP, (c + 1) * GROUP)
            dsum_ref[...] += dsv_ref[rows, :]
            for h in range(N_HEADS):
                cols = slice(h * GROUP, (h + 1) * GROUP)
                dvln_ref[rows, cols] = _dot(wtt_ref[h], dsvb_ref[rows, cols])
                dws_ref[h] += jnp.where(tril, _dot_t1(dsvb_ref[rows, cols], vln_ref[rows, cols]), 0.0)

        dvln = dvln_ref[...]
        dlg_ref[...] += jnp.sum(dvln * vh, axis=0, keepdims=True)
        dlb_ref[...] += jnp.sum(dvln, axis=0, keepdims=True)
        dvh = dvln * lg_ref[...]
        dv = rstd * (dvh - jnp.mean(dvh, axis=-1, keepdims=True)
                     - vh * jnp.mean(dvh * vh, axis=-1, keepdims=True))
        dp_ref[:, 0:D] = du.astype(BF16)
        dp_ref[:, D:2 * D] = dv.astype(BF16)
        dp_ref[:, 2 * D:3 * D] = dz.astype(BF16)

        @pl.when(step == nt - 1)
        def _():
            dwob_ref[...] = dwo_ref[...].astype(BF16)
            ones = jnp.ones((8, GROUP), F32)
            for h in range(N_HEADS):
                cols = slice(h * GROUP, (h + 1) * GROUP)
                sums = lax.dot_general(ones, dsum_ref[:, cols], (((1,), (1,)), ((), ())),
                                       precision=lax.Precision.HIGHEST, preferred_element_type=F32)
                dbs_ref[h:h + 1, :] = sums[0:1, :]

    tile = pl.BlockSpec((ts, D), lambda i: (i, 0))
    small = _full((N_HEADS, GROUP, GROUP))
    vec = _full((1, D))
    return pl.pallas_call(
        body, name="odd_backward", grid=(nt,),
        in_specs=[tile, tile, pl.BlockSpec((ts, PROJ), lambda i: (i, 0)), vec, vec, small, small, small,
                  _full((D, D)), vec],
        out_specs=[pl.BlockSpec((ts, PROJ), lambda i: (i, 0)), _full((D, D)), small, _full((N_HEADS, GROUP)),
                   vec, vec, vec],
        out_shape=[jax.ShapeDtypeStruct((S, PROJ), BF16), jax.ShapeDtypeStruct((D, D), BF16),
                   jax.ShapeDtypeStruct((N_HEADS, GROUP, GROUP), F32), jax.ShapeDtypeStruct((N_HEADS, GROUP), F32),
                   jax.ShapeDtypeStruct((1, D), F32), jax.ShapeDtypeStruct((1, D), F32),
                   jax.ShapeDtypeStruct((1, D), F32)],
        scratch_shapes=[pltpu.VMEM((ts, D), BF16), pltpu.VMEM((ts, D), F32), pltpu.VMEM((ts, D), BF16),
                        pltpu.VMEM((ts, D), BF16), pltpu.VMEM((ts, D), F32), pltpu.VMEM((ts, D), BF16),
                        pltpu.VMEM((ts, D), F32), pltpu.VMEM((GROUP, D), F32), pltpu.VMEM((D, D), F32)],
        compiler_params=_params(),
    )(g2, m1, p, lg, lb, wt_b, wtt_b, bfull, woutt_b, gpost)


def _input_backward(dp, wt_b, x, g_res, gpre, ts, name, sides=()):
    S = x.shape[0]

    def body(dp_ref, wt_ref, x_ref, g_ref, gp_ref, dx_ref, dgp_ref):
        @pl.when(pl.program_id(0) == 0)
        def _():
            dgp_ref[...] = jnp.zeros_like(dgp_ref)

        dh = _dot(dp_ref[...], wt_ref[...])
        xv = x_ref[...]
        r = _rms(xv)
        xn = xv * r
        dgp_ref[...] += jnp.sum(dh * xn, axis=0, keepdims=True)
        dxn = dh * gp_ref[...]
        dx_ref[...] = g_ref[...] + r * (dxn - xn * jnp.mean(dxn * xn, axis=-1, keepdims=True))

    tile = pl.BlockSpec((ts, D), lambda i: (i, 0))
    return _call(
        body, name=name, grid=(S // ts,),
        in_specs=[pl.BlockSpec((ts, PROJ), lambda i: (i, 0)), _full((PROJ, D)), tile, tile, _full((1, D))],
        out_specs=[tile, _full((1, D))],
        out_shape=[jax.ShapeDtypeStruct((S, D), F32), jax.ShapeDtypeStruct((1, D), F32)],
        scratch_shapes=[], args=(dp, wt_b, x, g_res, gpre), sides=sides)


N_CHIPS = 4


def _weight_grad_scatter(h, dp, ts, name, sides=()):
    S = h.shape[0]
    ts = min(ts, S)
    nt = S // ts
    panel = 2 * SHARD_IN
    last = N_CHIPS - 1

    def body(h_ref, dp_ref, parts_ref, acc_ref, half_ref, swap_ref, sum_ref, send_sems, recv_sems):
        s, t = pl.program_id(0), pl.program_id(1)
        x, y, c = _mesh_position()
        sibling = (x, y, 1 - c)

        def swap(k):
            return _remote(half_ref.at[k, 1], swap_ref.at[k], send_sems, recv_sems, k, sibling)

        def chip_sum(k):
            flip = last - k
            to = (x ^ (flip >> 1), y ^ (flip & 1), c)
            return _remote(sum_ref.at[k], parts_ref.at[1 + flip], send_sems, recv_sems, 3 + flip, to)

        own = pltpu.make_async_copy(half_ref.at[last, 0], parts_ref.at[0], recv_sems.at[7])
        to_sibling = _remote(half_ref.at[last, 1], parts_ref.at[1], send_sems, recv_sems, 3, sibling)

        @pl.when(t == 0)
        def _():
            acc_ref[...] = jnp.zeros_like(acc_ref)

        for k in range(last):
            @pl.when((s == k + 1) & (t == 0))
            def _():
                swap(k).wait_recv()
                sum_ref[k] = (half_ref[k, 0].astype(F32) + swap_ref[k].astype(F32)).astype(BF16)
                chip_sum(k).start()

        acc_ref[...] += _dot_t0(h_ref[...], dp_ref[...])

        @pl.when(t == nt - 1)
        def _():
            for core in range(2):
                @pl.when(c == core)
                def _():
                    half_ref[s, core] = acc_ref[:, 0:SHARD_IN].astype(BF16)
                    half_ref[s, 1 - core] = acc_ref[:, SHARD_IN:panel].astype(BF16)
            for k in range(last):
                @pl.when(s == k)
                def _():
                    swap(k).start()

        @pl.when((s == last) & (t == nt - 1))
        def _():
            own.start()
            to_sibling.start()
            own.wait()
            _remote(half_ref.at[last, 1], parts_ref.at[1], send_sems, recv_sems, 3, sibling).wait_recv()
            for k in range(last):
                chip_sum(k).wait_recv()
            to_sibling.wait_send()
            for k in range(last):
                swap(k).wait_send()
                chip_sum(k).wait_send()

    chip_panel = lambda s, t: (t, (2 * lax.axis_index("x") + lax.axis_index("y")) ^ (last - s))
    sems = pltpu.SemaphoreType.DMA((8,))
    return _call(
        body, name=name, grid=(N_CHIPS, nt),
        in_specs=[pl.BlockSpec((ts, D), lambda s, t: (t, 0)), pl.BlockSpec((ts, panel), chip_panel)],
        out_specs=[HBM_SPEC], out_shape=[jax.ShapeDtypeStruct((N_CHIPS + 1, D, SHARD_IN), BF16)],
        scratch_shapes=[pltpu.VMEM((D, panel), F32), pltpu.VMEM((N_CHIPS, 2, D, SHARD_IN), BF16),
                        pltpu.VMEM((last, D, SHARD_IN), BF16), pltpu.VMEM((last, D, SHARD_IN), BF16), sems, sems],
        args=(h, dp), sides=sides)


def _gather_first(wie, small, woe, wio, woo):
    def body(wie_ref, sm_ref, woe_ref, wio_ref, woo_ref, wie_out, sm_out, woe_b, wio_b, woo_b,
             cast_ref, send_sems, recv_sems):
        cast_ref[...] = wie_ref[...].astype(BF16)
        gather = _Gather([cast_ref, sm_ref], [wie_out, sm_out], send_sems, recv_sems)
        gather.start()
        woe_b[...] = woe_ref[...].astype(BF16)
        wio_b[...] = wio_ref[...].astype(BF16)
        woo_b[...] = woo_ref[...].astype(BF16)
        gather.middle()
        gather.finish()

    vm = pl.BlockSpec(memory_space=pltpu.VMEM)
    sems = pltpu.SemaphoreType.DMA((N_DEV * 2,))
    return pl.pallas_call(
        body, name="gather_first", in_specs=[vm] * 5, out_specs=[vm] * 5,
        out_shape=[jax.ShapeDtypeStruct((N_DEV,) + wie.shape, BF16), jax.ShapeDtypeStruct((N_DEV,) + small.shape, F32),
                   jax.ShapeDtypeStruct(woe.shape, BF16), jax.ShapeDtypeStruct(wio.shape, BF16),
                   jax.ShapeDtypeStruct(woo.shape, BF16)],
        scratch_shapes=[pltpu.VMEM(wie.shape, BF16), sems, sems],
        compiler_params=pltpu.CompilerParams(vmem_limit_bytes=VMEM_LIMIT),
    )(wie, small, woe, wio, woo)


def _adamw(w, g, m, v):
    m = ADAM_B1 * m + (1.0 - ADAM_B1) * g
    v = ADAM_B2 * v + (1.0 - ADAM_B2) * (g * g)
    m_hat = m / (1.0 - ADAM_B1 ** ADAM_STEP)
    v_hat = v / (1.0 - ADAM_B2 ** ADAM_STEP)
    delta = -ADAM_LR * (m_hat / (jnp.sqrt(v_hat) + ADAM_EPS) + ADAM_WD * w)
    return delta, m, v


def _sum_update(parts, w, m, v, name):
    R, C = w.shape
    rb = min(R, 256)
    n_parts = parts.shape[0]

    def body(p_ref, w_ref, m_ref, v_ref, g_out, d_out, m_out, v_out):
        acc = p_ref[0].astype(F32)
        for d in range(1, n_parts):
            acc = acc + p_ref[d].astype(F32)
        g_out[...] = acc
        delta, m_new, v_new = _adamw(w_ref[...], acc, m_ref[...], v_ref[...])
        d_out[...] = delta
        m_out[...] = m_new
        v_out[...] = v_new

    blk = pl.BlockSpec((rb, C), lambda i: (i, 0))
    return pl.pallas_call(
        body, name=name, grid=(R // rb,),
        in_specs=[pl.BlockSpec((n_parts, rb, C), lambda i: (0, i, 0)), blk, blk, blk], out_specs=[blk] * 4,
        out_shape=[jax.ShapeDtypeStruct((R, C), F32)] * 4, compiler_params=_params(),
    )(parts, w, m, v)


def _small_sum_update(partials, row_of, weights, loss_parts):
    n_p, n_w = len(partials), len(weights)

    def body(*refs):
        p_refs = refs[:n_p]
        loss_ref = refs[n_p]
        w_refs = refs[n_p + 1:n_p + 1 + 3 * n_w]
        loss_out = refs[n_p + 1 + 3 * n_w]
        outs = refs[n_p + 2 + 3 * n_w:]

        def total(ref):
            acc = ref[0]
            for d in range(1, N_DEV):
                acc = acc + ref[d]
            return acc

        lsum = jnp.sum(total(loss_ref), axis=-1, keepdims=True) * (0.5 / D)
        loss_out[...] = jnp.broadcast_to(lsum, loss_out.shape)
        for k in range(n_p):
            i, r0 = row_of[k]
            g = total(p_refs[k])
            w_ref, m_ref, v_ref = w_refs[3 * i:3 * i + 3]
            g_out, d_out, m_out, v_out = outs[4 * i:4 * i + 4]
            if g.ndim == 2:
                at = (slice(r0, r0 + g.shape[0]), slice(None))
            else:
                at = (slice(None),) * g.ndim
            delta, m_new, v_new = _adamw(w_ref[at], g, m_ref[at], v_ref[at])
            g_out[at] = g
            d_out[at] = delta
            m_out[at] = m_new
            v_out[at] = v_new

    vm = pl.BlockSpec(memory_space=pltpu.VMEM)
    flat_w = [a for wmv in weights for a in wmv]
    out_shape = [jax.ShapeDtypeStruct((1, 128), F32)]
    for w, _, _ in weights:
        out_shape += [jax.ShapeDtypeStruct(w.shape, F32)] * 4
    return pl.pallas_call(
        body, name="small_update", in_specs=[vm] * (n_p + 1 + 3 * n_w), out_specs=[vm] * len(out_shape),
        out_shape=out_shape, compiler_params=pltpu.CompilerParams(vmem_limit_bytes=VMEM_LIMIT),
    )(*partials, loss_parts, *flat_w)


def _pack_small_shard(cw, lg, lb):
    return (jnp.pad(cw, ((0, 13), (0, 64))) + jnp.pad(lg, ((8, 7), (0, 0))) + jnp.pad(lb, ((9, 6), (0, 0))))


TS_FWD = 512
TS_BWD = 256
TS_WGRAD = 2048


def kernel(x, pre_norm, post_norm, even_w_in, even_conv_w, even_pool_w, even_pool_scale, even_w_out, odd_w_in, odd_ln_g, odd_ln_b, odd_w_s, odd_b_s, odd_w_out, loss_target, m_pre_norm, m_post_norm, m_even_w_in, m_even_conv_w, m_even_pool_w, m_even_pool_scale, m_even_w_out, m_odd_w_in, m_odd_ln_g, m_odd_ln_b, m_odd_w_s, m_odd_b_s, m_odd_w_out, v_pre_norm, v_post_norm, v_even_w_in, v_even_conv_w, v_even_pool_w, v_even_pool_scale, v_even_w_out, v_odd_w_in, v_odd_ln_g, v_odd_ln_b, v_odd_w_s, v_odd_b_s, v_odd_w_out):
    S = x.shape[1]
    xs = x.reshape(S, D)
    tgt = loss_target.reshape(S, D)

    small_shard = _pack_small_shard(even_conv_w[0], odd_ln_g, odd_ln_b)
    wie_g, small_g, woe_b, wio_b, woo_b = _gather_first(even_w_in[0], small_shard, even_w_out[0], odd_w_in[0],
                                                        odd_w_out[0])
    wie = jnp.transpose(wie_g, (1, 0, 2)).reshape(D, PROJ)
    wie_t = jnp.transpose(wie_g, (0, 2, 1)).reshape(PROJ, D)
    conv_w = jnp.transpose(small_g[:, 0:3, 0:64], (1, 0, 2)).reshape(3, A_W)
    ln_g = small_g[:, 8, :].reshape(1, D)
    ln_b = small_g[:, 9, :].reshape(1, D)

    pool_w_b = even_pool_w[0].astype(BF16)
    pool_wt_b = jnp.swapaxes(even_pool_w[0], 1, 2).astype(BF16)
    ws_tril = jnp.tril(odd_w_s[0])
    ws_b = ws_tril.astype(BF16)
    wst_b = jnp.swapaxes(ws_tril, 1, 2).astype(BF16)
    b_full = jnp.broadcast_to(odd_b_s[0][:, :, None], (N_HEADS, GROUP, GROUP))
    gpre0, gpre1 = pre_norm[0:1], pre_norm[1:2]
    gpost0, gpost1 = post_norm[0:1], post_norm[1:2]

    p0, h0, woe_g, wio_g, woo_g = _norm_matmul(xs, gpre0, wie, TS_FWD, "even_proj",
                                               sides=[(_Gather, [woe_b, wio_b, woo_b])])
    wio = jnp.transpose(wio_g, (1, 0, 2)).reshape(D, PROJ)
    wio_t = jnp.transpose(wio_g, (0, 2, 1)).reshape(PROJ, D)
    woe = woe_g.reshape(D, D)
    woe_t = woe.T
    woo = woo_g.reshape(D, D)
    woo_t = woo.T
    m0, x1 = _even_forward(p0, xs, conv_w, pool_w_b, even_pool_scale, woe, gpost0, TS_FWD)
    p1, h1 = _norm_matmul(x1, gpre1, wio, TS_FWD, "odd_proj")
    m1, g2, loss_vec = _odd_forward(p1, x1, tgt, ln_g, ln_b, ws_b, b_full, woo, gpost1, TS_FWD)

    dp1, dwoo, dws, dbs, dlg, dlb, dgpost1 = _odd_backward(g2, m1, p1, ln_g, ln_b, ws_b, wst_b, b_full, woo_t,
                                                           gpost1, TS_BWD)
    by_owner = lambda dwo: dwo.reshape(N_DEV, SHARD_OUT, D)
    g1, dgpre1, dwoo_parts = _input_backward(dp1, wio_t, x1, g2, gpre1, TS_FWD, "odd_input_backward",
                                             sides=[(_Scatter, [by_owner(dwoo)])])
    dwio_parts, = _weight_grad_scatter(h1, dp1, TS_WGRAD, "odd_weight_grad")
    (dp0, dwoe, dcw, dpw, dps, dgpost0, dlg_parts, dlb_parts,
     dws_g, dbs_g, gpost1_g, gpre1_g, loss_g) = _even_backward(
        g1, m0, p0, conv_w, pool_w_b, pool_wt_b, even_pool_scale, woe_t, gpost0, TS_BWD,
        sides=[(_Scatter, [dlg.reshape(N_DEV, 1, 128), dlb.reshape(N_DEV, 1, 128)]),
               (_Gather, [dws, dbs, dgpost1, dgpre1, loss_vec])])
    gx, dgpre0 = _input_backward(dp0, wie_t, xs, g1, gpre0, TS_FWD, "even_input_backward")
    dcw_by_owner = jnp.transpose(dcw[0:3].reshape(3, N_DEV, 64), (1, 0, 2))
    dwie_parts, dwoe_parts, dcw_parts, dpw_g, dps_g, gpost0_g, gpre0_g = _weight_grad_scatter(
        h0, dp0, TS_WGRAD, "even_weight_grad",
        sides=[(_Scatter, [by_owner(dwoe), dcw_by_owner]), (_Gather, [dpw, dps, dgpost0, dgpre0])])

    g_wie, d_wie, nm_wie, nv_wie = _sum_update(dwie_parts, even_w_in[0], m_even_w_in[0], v_even_w_in[0],
                                               "update_even_w_in")
    g_wio, d_wio, nm_wio, nv_wio = _sum_update(dwio_parts, odd_w_in[0], m_odd_w_in[0], v_odd_w_in[0],
                                               "update_odd_w_in")
    g_woe, d_woe, nm_woe, nv_woe = _sum_update(dwoe_parts, even_w_out[0], m_even_w_out[0], v_even_w_out[0],
                                               "update_even_w_out")
    g_woo, d_woo, nm_woo, nv_woo = _sum_update(dwoo_parts, odd_w_out[0], m_odd_w_out[0], v_odd_w_out[0],
                                               "update_odd_w_out")

    partials = [gpre0_g, gpre1_g, gpost0_g, gpost1_g, dpw_g, dps_g, dws_g, dbs_g, dcw_parts, dlg_parts, dlb_parts]
    row_of = [(0, 0), (0, 1), (1, 0), (1, 1), (2, 0), (3, 0), (4, 0), (5, 0), (6, 0), (7, 0), (8, 0)]
    weights = [(pre_norm, m_pre_norm, v_pre_norm), (post_norm, m_post_norm, v_post_norm),
               (even_pool_w[0], m_even_pool_w[0], v_even_pool_w[0]),
               (even_pool_scale, m_even_pool_scale, v_even_pool_scale),
               (odd_w_s[0], m_odd_w_s[0], v_odd_w_s[0]), (odd_b_s[0], m_odd_b_s[0], v_odd_b_s[0]),
               (even_conv_w[0], m_even_conv_w[0], v_even_conv_w[0]),
               (odd_ln_g, m_odd_ln_g, v_odd_ln_g), (odd_ln_b, m_odd_ln_b, v_odd_ln_b)]
    small = _small_sum_update(partials, row_of, weights, loss_g)
    loss = small[0][0, 0]

    def leaves(k, big):
        pre, post, pw, psc, ws, bs, cw, lg, lb = [small[1 + 4 * i + k] for i in range(len(weights))]
        wie_k, woe_k, wio_k, woo_k = big
        return [pre, post, wie_k[None], cw[None], pw[None], psc, woe_k[None], wio_k[None], lg, lb, ws[None], bs[None],
                woo_k[None]]

    outs = [loss, gx.reshape(1, S, D)]
    outs += leaves(0, (g_wie, g_woe, g_wio, g_woo))
    outs += leaves(1, (d_wie, d_woe, d_wio, d_woo))
    outs += leaves(2, (nm_wie, nm_woe, nm_wio, nm_woo))
    outs += leaves(3, (nv_wie, nv_woe, nv_wio, nv_woo))
    return tuple(outs)
```

```python
import functools

import jax
import jax.numpy as jnp
from jax import lax
from jax.experimental import pallas as pl
from jax.experimental.pallas import tpu as pltpu

F32 = jnp.float32
BF16 = jnp.bfloat16
MESH = pl.DeviceIdType.MESH

D = 1024
EPS = 1e-6
A_W = 512
B_W = 512
POOL_WINDOWS = (2, 4, 8, 16)
GROUP = 128
N_GROUPS = 4
N_HEADS = 8
HALO = 16
PROJ = 3072
N_DEV = 8
SHARD_IN = PROJ // N_DEV
SHARD_OUT = D // N_DEV

ADAM_LR = 0.001
ADAM_B1 = 0.9
ADAM_B2 = 0.999
ADAM_EPS = 1e-08
ADAM_WD = 0.01
ADAM_STEP = 10

VMEM_LIMIT = 56 * 1024 * 1024


def _params(n_grid=1, vmem=VMEM_LIMIT):
    return pltpu.CompilerParams(dimension_semantics=("arbitrary",) * n_grid, vmem_limit_bytes=vmem)


def _full(shape):
    return pl.BlockSpec(shape, lambda *_: (0,) * len(shape))


def _sigmoid(z):
    return jax.nn.sigmoid(z)


def _dsilu(z, s):
    return s * (1.0 + z * (1.0 - s))


def _dot(a, b):
    return jnp.dot(a, b, preferred_element_type=F32)


def _dot_t0(a, b):
    return lax.dot_general(a, b, (((0,), (0,)), ((), ())), preferred_element_type=F32)


def _dot_t1(a, b):
    return lax.dot_general(a, b, (((1,), (1,)), ((), ())), preferred_element_type=F32)


def _rms(x):
    return lax.rsqrt(jnp.mean(x * x, axis=-1, keepdims=True) + EPS)


def _shift_down(a, k):
    return pltpu.roll(a, k, 0)


def _shift_up(a, k):
    return pltpu.roll(a, a.shape[0] - k, 0)


def _mesh_position():
    return lax.axis_index("x"), lax.axis_index("y"), lax.axis_index("c")


def _index(pos):
    return 4 * pos[0] + 2 * pos[1] + pos[2]


def _peer(pos, d):
    x, y, c = pos
    return (1 - x if d & 4 else x, 1 - y if d & 2 else y, 1 - c if d & 1 else c)


def _remote(src, dst, send_sems, recv_sems, k, to):
    return pltpu.make_async_remote_copy(src_ref=src, dst_ref=dst, send_sem=send_sems.at[k], recv_sem=recv_sems.at[k],
                                        device_id=to, device_id_type=MESH)


class _Gather:
    def __init__(self, srcs, dsts, send_sems, recv_sems):
        x, y, c = _mesh_position()
        n = len(srcs)
        me, sibling = (x, y, c), (x, y, 1 - c)
        chips = [(1 - x, y), (x, 1 - y), (1 - x, 1 - y)]

        def copy(k, t, block, to, src=None):
            slot = dsts[t].at[_index(block)]
            return _remote(slot if src is None else src, slot, send_sems, recv_sems, k * n + t, to)

        tensors = range(n)
        self.local = [pltpu.make_async_copy(srcs[t], dsts[t].at[_index(me)], recv_sems.at[7 * n + t]) for t in tensors]
        self.first = [copy(1 + j, t, me, chip + (c,), srcs[t]) for t in tensors for j, chip in enumerate(chips)]
        self.first += [copy(0, t, me, sibling, srcs[t]) for t in tensors]
        self.ici_in = [copy(1 + j, t, chip + (c,), me) for t in tensors for j, chip in enumerate(chips)]
        self.passed = [copy(4 + j, t, chip + (c,), sibling) for t in tensors for j, chip in enumerate(chips)]
        self.d2d_in = [copy(0, t, sibling, me) for t in tensors]
        self.d2d_in += [copy(4 + j, t, chip + (1 - c,), me) for t in tensors for j, chip in enumerate(chips)]

    def start(self):
        for cp in self.local + self.first:
            cp.start()

    def middle(self):
        for landed, onward in zip(self.ici_in, self.passed):
            landed.wait_recv()
            onward.start()

    def finish(self):
        for cp in self.d2d_in:
            cp.wait_recv()
        for cp in self.first + self.passed:
            cp.wait_send()
        for cp in self.local:
            cp.wait()


class _Scatter:
    def __init__(self, srcs, dsts, send_sems, recv_sems):
        pos = _mesh_position()
        n = len(srcs)
        self.local = [pltpu.make_async_copy(srcs[t].at[_index(pos)], dsts[t].at[0], recv_sems.at[t]) for t in range(n)]
        self.remote = []
        for d in range(1, N_DEV):
            to = _peer(pos, d)
            self.remote += [_remote(srcs[t].at[_index(to)], dsts[t].at[d], send_sems, recv_sems, d * n + t, to)
                            for t in range(n)]

    def start(self):
        for cp in self.local + self.remote:
            cp.start()

    def middle(self):
        pass

    def finish(self):
        for cp in self.local:
            cp.wait()
        for cp in self.remote:
            cp.wait_recv()
        for cp in self.remote:
            cp.wait_send()


HBM_SPEC = pl.BlockSpec(memory_space=pltpu.HBM)
MIDDLE_STEPS_BEFORE_END = 4


def _call(body, *, name, grid, in_specs, out_specs, out_shape, scratch_shapes, args, sides=()):
    params = _params(len(grid))
    if not sides:
        return pl.pallas_call(body, name=name, grid=grid, in_specs=in_specs, out_specs=out_specs, out_shape=out_shape,
                              scratch_shapes=scratch_shapes, compiler_params=params)(*args)
    n_in, n_out, n_scratch = len(in_specs), len(out_specs), len(scratch_shapes)
    counts = [len(srcs) for _, srcs in sides]
    ns = sum(counts)
    total = 1
    for g in grid:
        total *= g
    side_args, side_shapes, side_sems = [], [], []
    for exchange, srcs in sides:
        side_args += list(srcs)
        lead = (N_DEV,) if exchange is _Gather else ()
        side_shapes += [jax.ShapeDtypeStruct(lead + s.shape, s.dtype) for s in srcs]
        side_sems += [pltpu.SemaphoreType.DMA((N_DEV * len(srcs),))] * 2

    def wrapped(*refs):
        ins, side_in = refs[:n_in], refs[n_in:n_in + ns]
        outs = refs[n_in + ns:n_in + ns + n_out]
        side_out = refs[n_in + ns + n_out:n_in + 2 * ns + n_out]
        rest = refs[n_in + 2 * ns + n_out:]
        scratch, sems = rest[:n_scratch], rest[n_scratch:]
        step = pl.program_id(0)
        for axis in range(1, len(grid)):
            step = step * grid[axis] + pl.program_id(axis)

        def exchanges():
            built, at = [], 0
            for k, (exchange, _) in enumerate(sides):
                built.append(exchange(side_in[at:at + counts[k]], side_out[at:at + counts[k]],
                                      sems[2 * k], sems[2 * k + 1]))
                at += counts[k]
            return built

        @pl.when(step == 0)
        def _():
            for ex in exchanges():
                ex.start()

        @pl.when(step == max(total - MIDDLE_STEPS_BEFORE_END, 0))
        def _():
            for ex in exchanges():
                ex.middle()

        body(*ins, *outs, *scratch)

        @pl.when(step == total - 1)
        def _():
            for ex in exchanges():
                ex.finish()

    return pl.pallas_call(
        wrapped, name=name, grid=grid, in_specs=list(in_specs) + [HBM_SPEC] * ns,
        out_specs=list(out_specs) + [HBM_SPEC] * ns, out_shape=list(out_shape) + side_shapes,
        scratch_shapes=list(scratch_shapes) + side_sems, compiler_params=params)(*args, *side_args)


def _prenorm_gather(x, g, wie, small, woe, wio, woo, ts):
    S = x.shape[0]
    nt = S // ts

    def body(x_ref, g_ref, wie_ref, sm_ref, woe_ref, wio_ref, woo_ref,
             h_ref, wie_out, sm_out, woe_b, wio_b, woo_b, cast_ref, small_ref, send_sems, recv_sems):
        step = pl.program_id(0)
        gather = lambda: _Gather([cast_ref, small_ref], [wie_out, sm_out], send_sems, recv_sems)

        @pl.when(step == 0)
        def _():
            cast_ref[...] = wie_ref[...].astype(BF16)
            small_ref[...] = sm_ref[...]
            gather().start()
            woe_b[...] = woe_ref[...].astype(BF16)
            wio_b[...] = wio_ref[...].astype(BF16)
            woo_b[...] = woo_ref[...].astype(BF16)

        xv = x_ref[...]
        h_ref[...] = ((xv * _rms(xv)) * g_ref[...]).astype(BF16)

        @pl.when(step == nt - 1)
        def _():
            exchange = gather()
            exchange.middle()
            exchange.finish()

    tile = pl.BlockSpec((ts, D), lambda i: (i, 0))
    sems = pltpu.SemaphoreType.DMA((N_DEV * 2,))
    shards = [wie, small, woe, wio, woo]
    return pl.pallas_call(
        body, name="prenorm_gather", grid=(nt,),
        in_specs=[tile, _full((1, D))] + [_full(a.shape) for a in shards],
        out_specs=[tile, HBM_SPEC, HBM_SPEC] + [_full(a.shape) for a in shards[2:]],
        out_shape=[jax.ShapeDtypeStruct((S, D), BF16), jax.ShapeDtypeStruct((N_DEV,) + wie.shape, BF16),
                   jax.ShapeDtypeStruct((N_DEV,) + small.shape, F32)]
        + [jax.ShapeDtypeStruct(a.shape, BF16) for a in shards[2:]],
        scratch_shapes=[pltpu.VMEM(wie.shape, BF16), pltpu.VMEM(small.shape, F32), sems, sems],
        compiler_params=_params(),
    )(x, g, wie, small, woe, wio, woo)


def _matmul(h, w, ts, name, sides=()):
    S = h.shape[0]
    N = w.shape[1]

    def body(h_ref, w_ref, p_ref):
        p_ref[...] = _dot(h_ref[...], w_ref[...]).astype(BF16)

    return _call(
        body, name=name, grid=(S // ts,),
        in_specs=[pl.BlockSpec((ts, D), lambda i: (i, 0)), _full((D, N))],
        out_specs=[pl.BlockSpec((ts, N), lambda i: (i, 0))],
        out_shape=[jax.ShapeDtypeStruct((S, N), BF16)],
        scratch_shapes=[], args=(h, w), sides=sides)


def _even_forward_parts(p_ref, halo_ref, is_first, row0, cw_ref, pw_ref, ts):
    f = lambda a: a.astype(F32)
    xa = f(p_ref[:, 0:512])
    gb = f(p_ref[:, 512:1024])
    gc = f(p_ref[:, 1024:1536])
    za = f(p_ref[:, 1536:2048])
    xp = f(p_ref[:, 2048:2560])
    zp = f(p_ref[:, 2560:3072])
    keep = jnp.where(is_first, 0.0, 1.0).astype(F32)
    ha = gc * xa
    ha_halo = f(halo_ref[:, 1024:1536]) * f(halo_ref[:, 0:512]) * keep
    xp_halo = f(halo_ref[:, 2048:2560]) * keep
    ha_ext = jnp.concatenate([ha_halo, ha], axis=0)
    ha_m1 = _shift_down(ha_ext, 1)[HALO:]
    ha_m2 = _shift_down(ha_ext, 2)[HALO:]
    conv = cw_ref[2:3, :] * ha + cw_ref[1:2, :] * ha_m1 + cw_ref[0:1, :] * ha_m2
    sig_a = _sigmoid(za)
    silu_a = za * sig_a

    xp_ext = jnp.concatenate([xp_halo, xp], axis=0)
    pos = row0 + lax.broadcasted_iota(jnp.int32, (ts, 1), 0)
    pooled, inv_cnt, mixed = [], [], []
    for g, w in enumerate(POOL_WINDOWS):
        cols = slice(g * GROUP, (g + 1) * GROUP)
        s = xp_ext[:, cols]
        k = 1
        while k < w:
            s = s + _shift_down(s, k)
            k *= 2
        inv = 1.0 / jnp.minimum(pos + 1, w).astype(F32)
        pg = s[HALO:] * inv - xp[:, cols]
        pooled.append(pg)
        inv_cnt.append(inv)
        mixed.append(_dot(pg.astype(BF16), pw_ref[g]))
    mixed = jnp.concatenate(mixed, axis=1)
    sig_b = _sigmoid(zp)
    silu_b = zp * sig_b
    return dict(xa=xa, gb=gb, gc=gc, za=za, xp=xp, zp=zp, ha=ha, ha_m1=ha_m1, ha_m2=ha_m2, conv=conv,
                sig_a=sig_a, silu_a=silu_a, pooled=pooled, inv_cnt=inv_cnt, mixed=mixed, sig_b=sig_b,
                silu_b=silu_b)


def _halo_index(ts):
    blocks_per_tile = ts // HALO
    return lambda i: (jnp.maximum(i * blocks_per_tile - 1, 0), 0)


def _even_forward(p, x, cw, pw_b, ps, wout_b, gpost, gpre_next, ts, sides=()):
    S = x.shape[0]

    def body(p_ref, halo_ref, x_ref, cw_ref, pw_ref, ps_ref, wo_ref, gp_ref, gn_ref, m_ref, x1_ref, h1_ref, mix_ref):
        i = pl.program_id(0)
        fw = _even_forward_parts(p_ref, halo_ref, i == 0, i * ts, cw_ref, pw_ref, ts)
        mix_ref[:, 0:A_W] = (fw["gb"] * fw["conv"] * fw["silu_a"]).astype(BF16)
        mix_ref[:, A_W:D] = (fw["mixed"] * ps_ref[...] * fw["silu_b"]).astype(BF16)
        m = _dot(mix_ref[...], wo_ref[...])
        m_ref[...] = m
        x1 = x_ref[...] + (m * _rms(m)) * gp_ref[...]
        x1_ref[...] = x1
        h1_ref[...] = ((x1 * _rms(x1)) * gn_ref[...]).astype(BF16)

    tile = pl.BlockSpec((ts, D), lambda i: (i, 0))
    return _call(
        body, name="even_forward", grid=(S // ts,),
        in_specs=[pl.BlockSpec((ts, PROJ), lambda i: (i, 0)), pl.BlockSpec((HALO, PROJ), _halo_index(ts)), tile,
                  _full((3, A_W)), _full((N_GROUPS, GROUP, GROUP)), _full((1, B_W)), _full((D, D)), _full((1, D)),
                  _full((1, D))],
        out_specs=[tile, tile, tile],
        out_shape=[jax.ShapeDtypeStruct((S, D), F32), jax.ShapeDtypeStruct((S, D), F32),
                   jax.ShapeDtypeStruct((S, D), BF16)],
        scratch_shapes=[pltpu.VMEM((ts, D), BF16)],
        args=(p, p, x, cw, pw_b, ps, wout_b, gpost, gpre_next), sides=sides)


def _even_backward(g1, m0, p, cw, pw_b, pwt_b, ps, woutt_b, gpost, ts, sides=()):
    S = g1.shape[0]
    nt = S // ts

    def body(g_ref, m_ref, p_ref, halo_ref, cw_ref, pw_ref, pwt_ref, ps_ref, wot_ref, gp_ref,
             dp_ref, dwob_ref, dcw_ref, dpw_ref, dps_ref, dgp_ref, carry_ref, mix_ref, dm_ref, dwo_ref):
        step = pl.program_id(0)
        i = nt - 1 - step

        @pl.when(step == 0)
        def _():
            carry_ref[...] = jnp.zeros_like(carry_ref)
            dwo_ref[...] = jnp.zeros_like(dwo_ref)
            dcw_ref[...] = jnp.zeros_like(dcw_ref)
            dpw_ref[...] = jnp.zeros_like(dpw_ref)
            dps_ref[...] = jnp.zeros_like(dps_ref)
            dgp_ref[...] = jnp.zeros_like(dgp_ref)

        g = g_ref[...]
        m = m_ref[...]
        q = _rms(m)
        n = m * q
        dgp_ref[...] += jnp.sum(g * n, axis=0, keepdims=True)
        dn = g * gp_ref[...]
        dm = q * (dn - n * jnp.mean(dn * n, axis=-1, keepdims=True))
        dm_ref[...] = dm.astype(BF16)
        dmix = _dot(dm_ref[...], wot_ref[...])
        dya = dmix[:, 0:A_W]
        dyb = dmix[:, A_W:D]

        fw = _even_forward_parts(p_ref, halo_ref, i == 0, i * ts, cw_ref, pw_ref, ts)
        ps_v = ps_ref[...]
        mix_ref[:, 0:A_W] = (fw["gb"] * fw["conv"] * fw["silu_a"]).astype(BF16)
        mix_ref[:, A_W:D] = (fw["mixed"] * ps_v * fw["silu_b"]).astype(BF16)
        dwo_ref[...] += _dot_t0(mix_ref[...], dm_ref[...])

        t = dya * fw["gb"]
        dconv = t * fw["silu_a"]
        dgb = dya * fw["conv"] * fw["silu_a"]
        dza = t * fw["conv"] * _dsilu(fw["za"], fw["sig_a"])
        dcw_ref[2:3, :] += jnp.sum(dconv * fw["ha"], axis=0, keepdims=True)
        dcw_ref[1:2, :] += jnp.sum(dconv * fw["ha_m1"], axis=0, keepdims=True)
        dcw_ref[0:1, :] += jnp.sum(dconv * fw["ha_m2"], axis=0, keepdims=True)
        dconv_ext = jnp.concatenate([dconv, carry_ref[:, 0:A_W]], axis=0)
        dha = (cw_ref[2:3, :] * dconv + cw_ref[1:2, :] * _shift_up(dconv_ext, 1)[:ts]
               + cw_ref[0:1, :] * _shift_up(dconv_ext, 2)[:ts])
        dgc = dha * fw["xa"]
        dxa = dha * fw["gc"]

        u = dyb * fw["mixed"]
        dps_ref[...] += jnp.sum(u * fw["silu_b"], axis=0, keepdims=True)
        dzp = u * ps_v * _dsilu(fw["zp"], fw["sig_b"])
        dmixed = (dyb * ps_v * fw["silu_b"]).astype(BF16)
        dxp, e_first = [], []
        for gi, w in enumerate(POOL_WINDOWS):
            cols = slice(gi * GROUP, (gi + 1) * GROUP)
            dmg = dmixed[:, cols]
            dpooled = _dot(dmg, pwt_ref[gi])
            dpw_ref[gi] += _dot_t0(fw["pooled"][gi].astype(BF16), dmg)
            e = dpooled * fw["inv_cnt"][gi]
            e_first.append(e[0:HALO])
            s = jnp.concatenate([e, carry_ref[:, A_W + gi * GROUP:A_W + (gi + 1) * GROUP]], axis=0)
            k = 1
            while k < w:
                s = s + _shift_up(s, k)
                k *= 2
            dxp.append(s[:ts] - dpooled)

        carry_ref[:, 0:A_W] = dconv[0:HALO]
        carry_ref[:, A_W:D] = jnp.concatenate(e_first, axis=1)

        dp_ref[:, 0:512] = dxa.astype(BF16)
        dp_ref[:, 512:1024] = dgb.astype(BF16)
        dp_ref[:, 1024:1536] = dgc.astype(BF16)
        dp_ref[:, 1536:2048] = dza.astype(BF16)
        dp_ref[:, 2048:2560] = jnp.concatenate(dxp, axis=1).astype(BF16)
        dp_ref[:, 2560:3072] = dzp.astype(BF16)

        @pl.when(step == nt - 1)
        def _():
            dwob_ref[...] = dwo_ref[...].astype(BF16)

    rev = lambda s: (nt - 1 - s, 0)
    tile = pl.BlockSpec((ts, D), rev)
    bpt = ts // HALO
    halo_map = lambda s: (jnp.maximum((nt - 1 - s) * bpt - 1, 0), 0)
    return _call(
        body, name="even_backward", grid=(nt,),
        in_specs=[tile, tile, pl.BlockSpec((ts, PROJ), rev), pl.BlockSpec((HALO, PROJ), halo_map),
                  _full((3, A_W)), _full((N_GROUPS, GROUP, GROUP)), _full((N_GROUPS, GROUP, GROUP)),
                  _full((1, B_W)), _full((D, D)), _full((1, D))],
        out_specs=[pl.BlockSpec((ts, PROJ), rev), _full((D, D)), _full((8, A_W)),
                   _full((N_GROUPS, GROUP, GROUP)), _full((1, B_W)), _full((1, D))],
        out_shape=[jax.ShapeDtypeStruct((S, PROJ), BF16), jax.ShapeDtypeStruct((D, D), BF16),
                   jax.ShapeDtypeStruct((8, A_W), F32), jax.ShapeDtypeStruct((N_GROUPS, GROUP, GROUP), F32),
                   jax.ShapeDtypeStruct((1, B_W), F32), jax.ShapeDtypeStruct((1, D), F32)],
        scratch_shapes=[pltpu.VMEM((HALO, D), F32), pltpu.VMEM((ts, D), BF16), pltpu.VMEM((ts, D), BF16),
                        pltpu.VMEM((D, D), F32)],
        args=(g1, m0, p, p, cw, pw_b, pwt_b, ps, woutt_b, gpost), sides=sides)


def _odd_forward_parts(p_ref, lg_ref, lb_ref, wt_ref, bfull_ref, vln_ref, sv_ref, r0, nr):
    f = lambda a: a.astype(F32)
    rows = slice(r0, r0 + nr)
    u = f(p_ref[rows, 0:D])
    v = f(p_ref[rows, D:2 * D])
    z = f(p_ref[rows, 2 * D:3 * D])
    mu = jnp.mean(v, axis=-1, keepdims=True)
    vc = v - mu
    rstd = lax.rsqrt(jnp.mean(vc * vc, axis=-1, keepdims=True) + EPS)
    vh = vc * rstd
    vln_ref[rows, :] = (vh * lg_ref[...] + lb_ref[...]).astype(BF16)
    for c in range(nr // GROUP):
        chunk = slice(r0 + c * GROUP, r0 + (c + 1) * GROUP)
        for h in range(N_HEADS):
            cols = slice(h * GROUP, (h + 1) * GROUP)
            sv_ref[chunk, cols] = _dot(wt_ref[h], vln_ref[chunk, cols]) + bfull_ref[h]
    return u, z, vh, rstd


SUB_ROWS = 256


def _odd_forward(p, x1, tgt, lg, lb, wt_b, bfull, wout_b, gpost, ts):
    S = x1.shape[0]

    def body(p_ref, x_ref, t_ref, lg_ref, lb_ref, wt_ref, bfull_ref, wo_ref, gp_ref,
             m_ref, g_ref, loss_ref, vln_ref, sv_ref, y_ref):
        @pl.when(pl.program_id(0) == 0)
        def _():
            loss_ref[...] = jnp.zeros_like(loss_ref)

        loss = None
        for r0 in range(0, ts, SUB_ROWS):
            rows = slice(r0, r0 + SUB_ROWS)
            u, z, _, _ = _odd_forward_parts(p_ref, lg_ref, lb_ref, wt_ref, bfull_ref, vln_ref, sv_ref, r0, SUB_ROWS)
            y_ref[rows, :] = (u * sv_ref[rows, :] * (z * _sigmoid(z))).astype(BF16)
            m = _dot(y_ref[rows, :], wo_ref[...])
            m_ref[rows, :] = m
            x2 = x_ref[rows, :] + (m * _rms(m)) * gp_ref[...]
            err = x2 - t_ref[rows, :]
            g_ref[rows, :] = err * (1.0 / D)
            part = jnp.sum(err * err, axis=0, keepdims=True)
            loss = part if loss is None else loss + part
        loss_ref[...] += loss

    tile = pl.BlockSpec((ts, D), lambda i: (i, 0))
    small = _full((N_HEADS, GROUP, GROUP))
    return pl.pallas_call(
        body, name="odd_forward", grid=(S // ts,),
        in_specs=[pl.BlockSpec((ts, PROJ), lambda i: (i, 0)), tile, tile, _full((1, D)), _full((1, D)),
                  small, small, _full((D, D)), _full((1, D))],
        out_specs=[tile, tile, _full((1, D))],
        out_shape=[jax.ShapeDtypeStruct((S, D), F32), jax.ShapeDtypeStruct((S, D), F32),
                   jax.ShapeDtypeStruct((1, D), F32)],
        scratch_shapes=[pltpu.VMEM((ts, D), BF16), pltpu.VMEM((ts, D), F32), pltpu.VMEM((ts, D), BF16)],
        compiler_params=_params(),
    )(p, x1, tgt, lg, lb, wt_b, bfull, wout_b, gpost)


def _odd_backward(g2, m1, p, lg, lb, wt_b, wtt_b, bfull, woutt_b, gpost, ts):
    S = g2.shape[0]
    nt = S // ts

    def body(g_ref, m_ref, p_ref, lg_ref, lb_ref, wt_ref, wtt_ref, bfull_ref, wot_ref, gp_ref,
             dp_ref, dwob_ref, dws_ref, dbs_ref, dlg_ref, dlb_ref, dgp_ref,
             vln_ref, sv_ref, y_ref, dm_ref, dsv_ref, dsvb_ref, dvln_ref, dsum_ref, dwo_ref):
        step = pl.program_id(0)

        @pl.when(step == 0)
        def _():
            dwo_ref[...] = jnp.zeros_like(dwo_ref)
            dws_ref[...] = jnp.zeros_like(dws_ref)
            dsum_ref[...] = jnp.zeros_like(dsum_ref)
            dlg_ref[...] = jnp.zeros_like(dlg_ref)
            dlb_ref[...] = jnp.zeros_like(dlb_ref)
            dgp_ref[...] = jnp.zeros_like(dgp_ref)

        g = g_ref[...]
        m = m_ref[...]
        q = _rms(m)
        n = m * q
        dgp_ref[...] += jnp.sum(g * n, axis=0, keepdims=True)
        dn = g * gp_ref[...]
        dm = q * (dn - n * jnp.mean(dn * n, axis=-1, keepdims=True))
        dm_ref[...] = dm.astype(BF16)
        dy = _dot(dm_ref[...], wot_ref[...])

        u, z, vh, rstd = _odd_forward_parts(p_ref, lg_ref, lb_ref, wt_ref, bfull_ref, vln_ref, sv_ref, 0, ts)
        sv = sv_ref[...]
        sig = _sigmoid(z)
        sz = z * sig
        y_ref[...] = (u * sv * sz).astype(BF16)
        dwo_ref[...] += _dot_t0(y_ref[...], dm_ref[...])

        t = dy * sz
        du = t * sv
        dsv = t * u
        dz = dy * u * sv * _dsilu(z, sig)
        dsv_ref[...] = dsv
        dsvb_ref[...] = dsv.astype(BF16)

        tril = (lax.broadcasted_iota(jnp.int32, (GROUP, GROUP), 0)
                >= lax.broadcasted_iota(jnp.int32, (GROUP, GROUP), 1))
        for c in range(ts // GROUP):
            rows = slice(c * GROUP, (c + 1) * GROUP)
            dsum_ref[...] += dsv_ref[rows, :]
            for h in range(N_HEADS):
                cols = slice(h * GROUP, (h + 1) * GROUP)
                dvln_ref[rows, cols] = _dot(wtt_ref[h], dsvb_ref[rows, cols])
                dws_ref[h] += jnp.where(tril, _dot_t1(dsvb_ref[rows, cols], vln_ref[rows, cols]), 0.0)

        dvln = dvln_ref[...]
        dlg_ref[...] += jnp.sum(dvln * vh, axis=0, keepdims=True)
        dlb_ref[...] += jnp.sum(dvln, axis=0, keepdims=True)
        dvh = dvln * lg_ref[...]
        dv = rstd * (dvh - jnp.mean(dvh, axis=-1, keepdims=True)
                     - vh * jnp.mean(dvh * vh, axis=-1, keepdims=True))
        dp_ref[:, 0:D] = du.astype(BF16)
        dp_ref[:, D:2 * D] = dv.astype(BF16)
        dp_ref[:, 2 * D:3 * D] = dz.astype(BF16)

        @pl.when(step == nt - 1)
        def _():
            dwob_ref[...] = dwo_ref[...].astype(BF16)
            ones = jnp.ones((8, GROUP), F32)
            for h in range(N_HEADS):
                cols = slice(h * GROUP, (h + 1) * GROUP)
                sums = lax.dot_general(ones, dsum_ref[:, cols], (((1,), (1,)), ((), ())),
                                       precision=lax.Precision.HIGHEST, preferred_element_type=F32)
                dbs_ref[h:h + 1, :] = sums[0:1, :]

    tile = pl.BlockSpec((ts, D), lambda i: (i, 0))
    small = _full((N_HEADS, GROUP, GROUP))
    vec = _full((1, D))
    return pl.pallas_call(
        body, name="odd_backward", grid=(nt,),
        in_specs=[tile, tile, pl.BlockSpec((ts, PROJ), lambda i: (i, 0)), vec, vec, small, small, small,
                  _full((D, D)), vec],
        out_specs=[pl.BlockSpec((ts, PROJ), lambda i: (i, 0)), _full((D, D)), small, _full((N_HEADS, GROUP)),
                   vec, vec, vec],
        out_shape=[jax.ShapeDtypeStruct((S, PROJ), BF16), jax.ShapeDtypeStruct((D, D), BF16),
                   jax.ShapeDtypeStruct((N_HEADS, GROUP, GROUP), F32), jax.ShapeDtypeStruct((N_HEADS, GROUP), F32),
                   jax.ShapeDtypeStruct((1, D), F32), jax.ShapeDtypeStruct((1, D), F32),
                   jax.ShapeDtypeStruct((1, D), F32)],
        scratch_shapes=[pltpu.VMEM((ts, D), BF16), pltpu.VMEM((ts, D), F32), pltpu.VMEM((ts, D), BF16),
                        pltpu.VMEM((ts, D), BF16), pltpu.VMEM((ts, D), F32), pltpu.VMEM((ts, D), BF16),
                        pltpu.VMEM((ts, D), F32), pltpu.VMEM((GROUP, D), F32), pltpu.VMEM((D, D), F32)],
        compiler_params=_params(),
    )(g2, m1, p, lg, lb, wt_b, wtt_b, bfull, woutt_b, gpost)


def _input_backward(dp, wt_b, x, g_res, gpre, ts, name, sides=()):
    S = x.shape[0]

    def body(dp_ref, wt_ref, x_ref, g_ref, gp_ref, dx_ref, dgp_ref):
        @pl.when(pl.program_id(0) == 0)
        def _():
            dgp_ref[...] = jnp.zeros_like(dgp_ref)

        dh = _dot(dp_ref[...], wt_ref[...])
        xv = x_ref[...]
        r = _rms(xv)
        xn = xv * r
        dgp_ref[...] += jnp.sum(dh * xn, axis=0, keepdims=True)
        dxn = dh * gp_ref[...]
        dx_ref[...] = g_ref[...] + r * (dxn - xn * jnp.mean(dxn * xn, axis=-1, keepdims=True))

    tile = pl.BlockSpec((ts, D), lambda i: (i, 0))
    return _call(
        body, name=name, grid=(S // ts,),
        in_specs=[pl.BlockSpec((ts, PROJ), lambda i: (i, 0)), _full((PROJ, D)), tile, tile, _full((1, D))],
        out_specs=[tile, _full((1, D))],
        out_shape=[jax.ShapeDtypeStruct((S, D), F32), jax.ShapeDtypeStruct((1, D), F32)],
        scratch_shapes=[], args=(dp, wt_b, x, g_res, gpre), sides=sides)


N_CHIPS = 4


def _weight_grad_scatter(h, dp, ts, name, sides=()):
    S = h.shape[0]
    ts = min(ts, S)
    nt = S // ts
    panel = 2 * SHARD_IN
    last = N_CHIPS - 1

    def body(h_ref, dp_ref, parts_ref, acc_ref, half_ref, swap_ref, sum_ref, send_sems, recv_sems):
        s, t = pl.program_id(0), pl.program_id(1)
        x, y, c = _mesh_position()
        sibling = (x, y, 1 - c)

        def swap(k):
            return _remote(half_ref.at[k, 1], swap_ref.at[k], send_sems, recv_sems, k, sibling)

        def chip_sum(k):
            flip = last - k
            to = (x ^ (flip >> 1), y ^ (flip & 1), c)
            return _remote(sum_ref.at[k], parts_ref.at[1 + flip], send_sems, recv_sems, 3 + flip, to)

        own = pltpu.make_async_copy(half_ref.at[last, 0], parts_ref.at[0], recv_sems.at[7])
        to_sibling = _remote(half_ref.at[last, 1], parts_ref.at[1], send_sems, recv_sems, 3, sibling)

        @pl.when(t == 0)
        def _():
            acc_ref[...] = jnp.zeros_like(acc_ref)

        for k in range(last):
            @pl.when((s == k + 1) & (t == 0))
            def _():
                swap(k).wait_recv()
                sum_ref[k] = (half_ref[k, 0].astype(F32) + swap_ref[k].astype(F32)).astype(BF16)
                chip_sum(k).start()

        acc_ref[...] += _dot_t0(h_ref[...], dp_ref[...])

        @pl.when(t == nt - 1)
        def _():
            for core in range(2):
                @pl.when(c == core)
                def _():
                    half_ref[s, core] = acc_ref[:, 0:SHARD_IN].astype(BF16)
                    half_ref[s, 1 - core] = acc_ref[:, SHARD_IN:panel].astype(BF16)
            for k in range(last):
                @pl.when(s == k)
                def _():
                    swap(k).start()

        @pl.when((s == last) & (t == nt - 1))
        def _():
            own.start()
            to_sibling.start()
            own.wait()
            _remote(half_ref.at[last, 1], parts_ref.at[1], send_sems, recv_sems, 3, sibling).wait_recv()
            for k in range(last):
                chip_sum(k).wait_recv()
            to_sibling.wait_send()
            for k in range(last):
                swap(k).wait_send()
                chip_sum(k).wait_send()

    chip_panel = lambda s, t: (t, (2 * lax.axis_index("x") + lax.axis_index("y")) ^ (last - s))
    sems = pltpu.SemaphoreType.DMA((8,))
    return _call(
        body, name=name, grid=(N_CHIPS, nt),
        in_specs=[pl.BlockSpec((ts, D), lambda s, t: (t, 0)), pl.BlockSpec((ts, panel), chip_panel)],
        out_specs=[HBM_SPEC], out_shape=[jax.ShapeDtypeStruct((N_CHIPS + 1, D, SHARD_IN), BF16)],
        scratch_shapes=[pltpu.VMEM((D, panel), F32), pltpu.VMEM((N_CHIPS, 2, D, SHARD_IN), BF16),
                        pltpu.VMEM((last, D, SHARD_IN), BF16), pltpu.VMEM((last, D, SHARD_IN), BF16), sems, sems],
        args=(h, dp), sides=sides)


def _adamw(w, g, m, v):
    m = ADAM_B1 * m + (1.0 - ADAM_B1) * g
    v = ADAM_B2 * v + (1.0 - ADAM_B2) * (g * g)
    m_hat = m / (1.0 - ADAM_B1 ** ADAM_STEP)
    v_hat = v / (1.0 - ADAM_B2 ** ADAM_STEP)
    delta = -ADAM_LR * (m_hat / (jnp.sqrt(v_hat) + ADAM_EPS) + ADAM_WD * w)
    return delta, m, v


def _sum_update(parts, w, m, v, name):
    R, C = w.shape
    rb = min(R, 256)
    n_parts = parts.shape[0]

    def body(p_ref, w_ref, m_ref, v_ref, g_out, d_out, m_out, v_out):
        acc = p_ref[0].astype(F32)
        for d in range(1, n_parts):
            acc = acc + p_ref[d].astype(F32)
        g_out[...] = acc
        delta, m_new, v_new = _adamw(w_ref[...], acc, m_ref[...], v_ref[...])
        d_out[...] = delta
        m_out[...] = m_new
        v_out[...] = v_new

    blk = pl.BlockSpec((rb, C), lambda i: (i, 0))
    return pl.pallas_call(
        body, name=name, grid=(R // rb,),
        in_specs=[pl.BlockSpec((n_parts, rb, C), lambda i: (0, i, 0)), blk, blk, blk], out_specs=[blk] * 4,
        out_shape=[jax.ShapeDtypeStruct((R, C), F32)] * 4, compiler_params=_params(),
    )(parts, w, m, v)


def _small_sum_update(partials, row_of, weights, loss_parts):
    n_p, n_w = len(partials), len(weights)

    def body(*refs):
        p_refs = refs[:n_p]
        loss_ref = refs[n_p]
        w_refs = refs[n_p + 1:n_p + 1 + 3 * n_w]
        loss_out = refs[n_p + 1 + 3 * n_w]
        outs = refs[n_p + 2 + 3 * n_w:]

        def total(ref):
            acc = ref[0]
            for d in range(1, N_DEV):
                acc = acc + ref[d]
            return acc

        lsum = jnp.sum(total(loss_ref), axis=-1, keepdims=True) * (0.5 / D)
        loss_out[...] = jnp.broadcast_to(lsum, loss_out.shape)
        for k in range(n_p):
            i, r0 = row_of[k]
            g = total(p_refs[k])
            w_ref, m_ref, v_ref = w_refs[3 * i:3 * i + 3]
            g_out, d_out, m_out, v_out = outs[4 * i:4 * i + 4]
            if g.ndim == 2:
                at = (slice(r0, r0 + g.shape[0]), slice(None))
            else:
                at = (slice(None),) * g.ndim
            delta, m_new, v_new = _adamw(w_ref[at], g, m_ref[at], v_ref[at])
            g_out[at] = g
            d_out[at] = delta
            m_out[at] = m_new
            v_out[at] = v_new

    vm = pl.BlockSpec(memory_space=pltpu.VMEM)
    flat_w = [a for wmv in weights for a in wmv]
    out_shape = [jax.ShapeDtypeStruct((1, 128), F32)]
    for w, _, _ in weights:
        out_shape += [jax.ShapeDtypeStruct(w.shape, F32)] * 4
    return pl.pallas_call(
        body, name="small_update", in_specs=[vm] * (n_p + 1 + 3 * n_w), out_specs=[vm] * len(out_shape),
        out_shape=out_shape, compiler_params=pltpu.CompilerParams(vmem_limit_bytes=VMEM_LIMIT),
    )(*partials, loss_parts, *flat_w)


def _pack_small_shard(cw, lg, lb):
    return (jnp.pad(cw, ((0, 13), (0, 64))) + jnp.pad(lg, ((8, 7), (0, 0))) + jnp.pad(lb, ((9, 6), (0, 0))))


TS_PRENORM = 1024
TS_FWD = 512
TS_BWD = 256
TS_WGRAD = 2048


def kernel(x, pre_norm, post_norm, even_w_in, even_conv_w, even_pool_w, even_pool_scale, even_w_out, odd_w_in, odd_ln_g, odd_ln_b, odd_w_s, odd_b_s, odd_w_out, loss_target, m_pre_norm, m_post_norm, m_even_w_in, m_even_conv_w, m_even_pool_w, m_even_pool_scale, m_even_w_out, m_odd_w_in, m_odd_ln_g, m_odd_ln_b, m_odd_w_s, m_odd_b_s, m_odd_w_out, v_pre_norm, v_post_norm, v_even_w_in, v_even_conv_w, v_even_pool_w, v_even_pool_scale, v_even_w_out, v_odd_w_in, v_odd_ln_g, v_odd_ln_b, v_odd_w_s, v_odd_b_s, v_odd_w_out):
    S = x.shape[1]
    xs = x.reshape(S, D)
    tgt = loss_target.reshape(S, D)

    gpre0, gpre1 = pre_norm[0:1], pre_norm[1:2]
    gpost0, gpost1 = post_norm[0:1], post_norm[1:2]
    small_shard = _pack_small_shard(even_conv_w[0], odd_ln_g, odd_ln_b)
    h0, wie_g, small_g, woe_b, wio_b, woo_b = _prenorm_gather(xs, gpre0, even_w_in[0], small_shard, even_w_out[0],
                                                              odd_w_in[0], odd_w_out[0], TS_PRENORM)
    wie = jnp.transpose(wie_g, (1, 0, 2)).reshape(D, PROJ)
    wie_t = jnp.transpose(wie_g, (0, 2, 1)).reshape(PROJ, D)
    conv_w = jnp.transpose(small_g[:, 0:3, 0:64], (1, 0, 2)).reshape(3, A_W)
    ln_g = small_g[:, 8, :].reshape(1, D)
    ln_b = small_g[:, 9, :].reshape(1, D)

    pool_w_b = even_pool_w[0].astype(BF16)
    pool_wt_b = jnp.swapaxes(even_pool_w[0], 1, 2).astype(BF16)
    ws_tril = jnp.tril(odd_w_s[0])
    ws_b = ws_tril.astype(BF16)
    wst_b = jnp.swapaxes(ws_tril, 1, 2).astype(BF16)
    b_full = jnp.broadcast_to(odd_b_s[0][:, :, None], (N_HEADS, GROUP, GROUP))

    p0, woe_g, wio_g = _matmul(h0, wie, TS_FWD, "even_proj", sides=[(_Gather, [woe_b, wio_b])])
    wio = jnp.transpose(wio_g, (1, 0, 2)).reshape(D, PROJ)
    wio_t = jnp.transpose(wio_g, (0, 2, 1)).reshape(PROJ, D)
    woe = woe_g.reshape(D, D)
    woe_t = woe.T
    m0, x1, h1, woo_g = _even_forward(p0, xs, conv_w, pool_w_b, even_pool_scale, woe, gpost0, gpre1, TS_FWD,
                                      sides=[(_Gather, [woo_b])])
    woo = woo_g.reshape(D, D)
    woo_t = woo.T
    p1, = _matmul(h1, wio, TS_FWD, "odd_proj")
    m1, g2, loss_vec = _odd_forward(p1, x1, tgt, ln_g, ln_b, ws_b, b_full, woo, gpost1, TS_FWD)

    dp1, dwoo, dws, dbs, dlg, dlb, dgpost1 = _odd_backward(g2, m1, p1, ln_g, ln_b, ws_b, wst_b, b_full, woo_t,
                                                           gpost1, TS_BWD)
    by_owner = lambda dwo: dwo.reshape(N_DEV, SHARD_OUT, D)
    g1, dgpre1, dwoo_parts = _input_backward(dp1, wio_t, x1, g2, gpre1, TS_FWD, "odd_input_backward",
                                             sides=[(_Scatter, [by_owner(dwoo)])])
    dwio_parts, = _weight_grad_scatter(h1, dp1, TS_WGRAD, "odd_weight_grad")
    (dp0, dwoe, dcw, dpw, dps, dgpost0, dlg_parts, dlb_parts,
     dws_g, dbs_g, gpost1_g, gpre1_g, loss_g) = _even_backward(
        g1, m0, p0, conv_w, pool_w_b, pool_wt_b, even_pool_scale, woe_t, gpost0, TS_BWD,
        sides=[(_Scatter, [dlg.reshape(N_DEV, 1, 128), dlb.reshape(N_DEV, 1, 128)]),
               (_Gather, [dws, dbs, dgpost1, dgpre1, loss_vec])])
    gx, dgpre0 = _input_backward(dp0, wie_t, xs, g1, gpre0, TS_FWD, "even_input_backward")
    dcw_by_owner = jnp.transpose(dcw[0:3].reshape(3, N_DEV, 64), (1, 0, 2))
    dwie_parts, dwoe_parts, dcw_parts, dpw_g, dps_g, gpost0_g, gpre0_g = _weight_grad_scatter(
        h0, dp0, TS_WGRAD, "even_weight_grad",
        sides=[(_Scatter, [by_owner(dwoe), dcw_by_owner]), (_Gather, [dpw, dps, dgpost0, dgpre0])])

    g_wie, d_wie, nm_wie, nv_wie = _sum_update(dwie_parts, even_w_in[0], m_even_w_in[0], v_even_w_in[0],
                                               "update_even_w_in")
    g_wio, d_wio, nm_wio, nv_wio = _sum_update(dwio_parts, odd_w_in[0], m_odd_w_in[0], v_odd_w_in[0],
                                               "update_odd_w_in")
    g_woe, d_woe, nm_woe, nv_woe = _sum_update(dwoe_parts, even_w_out[0], m_even_w_out[0], v_even_w_out[0],
                                               "update_even_w_out")
    g_woo, d_woo, nm_woo, nv_woo = _sum_update(dwoo_parts, odd_w_out[0], m_odd_w_out[0], v_odd_w_out[0],
                                               "update_odd_w_out")

    partials = [gpre0_g, gpre1_g, gpost0_g, gpost1_g, dpw_g, dps_g, dws_g, dbs_g, dcw_parts, dlg_parts, dlb_parts]
    row_of = [(0, 0), (0, 1), (1, 0), (1, 1), (2, 0), (3, 0), (4, 0), (5, 0), (6, 0), (7, 0), (8, 0)]
    weights = [(pre_norm, m_pre_norm, v_pre_norm), (post_norm, m_post_norm, v_post_norm),
               (even_pool_w[0], m_even_pool_w[0], v_even_pool_w[0]),
               (even_pool_scale, m_even_pool_scale, v_even_pool_scale),
               (odd_w_s[0], m_odd_w_s[0], v_odd_w_s[0]), (odd_b_s[0], m_odd_b_s[0], v_odd_b_s[0]),
               (even_conv_w[0], m_even_conv_w[0], v_even_conv_w[0]),
               (odd_ln_g, m_odd_ln_g, v_odd_ln_g), (odd_ln_b, m_odd_ln_b, v_odd_ln_b)]
    small = _small_sum_update(partials, row_of, weights, loss_g)
    loss = small[0][0, 0]

    def leaves(k, big):
        pre, post, pw, psc, ws, bs, cw, lg, lb = [small[1 + 4 * i + k] for i in range(len(weights))]
        wie_k, woe_k, wio_k, woo_k = big
        return [pre, post, wie_k[None], cw[None], pw[None], psc, woe_k[None], wio_k[None], lg, lb, ws[None], bs[None],
                woo_k[None]]

    outs = [loss, gx.reshape(1, S, D)]
    outs += leaves(0, (g_wie, g_woe, g_wio, g_woo))
    outs += leaves(1, (d_wie, d_woe, d_wio, d_woo))
    outs += leaves(2, (nm_wie, nm_woe, nm_wio, nm_woo))
    outs += leaves(3, (nv_wie, nv_woe, nv_wio, nv_woo))
    return tuple(outs)
```

```python
import functools

import jax
import jax.numpy as jnp
from jax import lax
from jax.experimental import pallas as pl
from jax.experimental.pallas import tpu as pltpu

F32 = jnp.float32
BF16 = jnp.bfloat16
MESH = pl.DeviceIdType.MESH

D = 1024
EPS = 1e-6
A_W = 512
B_W = 512
POOL_WINDOWS = (2, 4, 8, 16)
GROUP = 128
N_GROUPS = 4
N_HEADS = 8
HALO = 16
PROJ = 3072
N_DEV = 8
SHARD_IN = PROJ // N_DEV
SHARD_OUT = D // N_DEV

ADAM_LR = 0.001
ADAM_B1 = 0.9
ADAM_B2 = 0.999
ADAM_EPS = 1e-08
ADAM_WD = 0.01
ADAM_STEP = 10

VMEM_LIMIT = 54 * 1024 * 1024


def _params(n_grid=1, vmem=VMEM_LIMIT):
    return pltpu.CompilerParams(dimension_semantics=("arbitrary",) * n_grid, vmem_limit_bytes=vmem)


def _full(shape):
    return pl.BlockSpec(shape, lambda *_: (0,) * len(shape))


def _sigmoid(z):
    return jax.nn.sigmoid(z)


def _dsilu(z, s):
    return s * (1.0 + z * (1.0 - s))


def _dot(a, b):
    return jnp.dot(a, b, preferred_element_type=F32)


def _dot_t0(a, b):
    return lax.dot_general(a, b, (((0,), (0,)), ((), ())), preferred_element_type=F32)


def _dot_t1(a, b):
    return lax.dot_general(a, b, (((1,), (1,)), ((), ())), preferred_element_type=F32)


def _rms(x):
    return lax.rsqrt(jnp.mean(x * x, axis=-1, keepdims=True) + EPS)


def _shift_down(a, k):
    return pltpu.roll(a, k, 0)


def _shift_up(a, k):
    return pltpu.roll(a, a.shape[0] - k, 0)


def _mesh_position():
    return lax.axis_index("x"), lax.axis_index("y"), lax.axis_index("c")


def _index(pos):
    return 4 * pos[0] + 2 * pos[1] + pos[2]


def _peer(pos, d):
    x, y, c = pos
    return (1 - x if d & 4 else x, 1 - y if d & 2 else y, 1 - c if d & 1 else c)


def _remote(src, dst, send_sems, recv_sems, k, to):
    return pltpu.make_async_remote_copy(src_ref=src, dst_ref=dst, send_sem=send_sems.at[k], recv_sem=recv_sems.at[k],
                                        device_id=to, device_id_type=MESH)


class _Gather:
    def __init__(self, srcs, dsts, send_sems, recv_sems):
        x, y, c = _mesh_position()
        n = len(srcs)
        me, sibling = (x, y, c), (x, y, 1 - c)
        chips = [(1 - x, y), (x, 1 - y), (1 - x, 1 - y)]

        def copy(k, t, block, to, src=None):
            slot = dsts[t].at[_index(block)]
            return _remote(slot if src is None else src, slot, send_sems, recv_sems, k * n + t, to)

        tensors = range(n)
        self.local = [pltpu.make_async_copy(srcs[t], dsts[t].at[_index(me)], recv_sems.at[7 * n + t]) for t in tensors]
        self.first = [copy(1 + j, t, me, chip + (c,), srcs[t]) for t in tensors for j, chip in enumerate(chips)]
        self.first += [copy(0, t, me, sibling, srcs[t]) for t in tensors]
        self.ici_in = [copy(1 + j, t, chip + (c,), me) for t in tensors for j, chip in enumerate(chips)]
        self.passed = [copy(4 + j, t, chip + (c,), sibling) for t in tensors for j, chip in enumerate(chips)]
        self.d2d_in = [copy(0, t, sibling, me) for t in tensors]
        self.d2d_in += [copy(4 + j, t, chip + (1 - c,), me) for t in tensors for j, chip in enumerate(chips)]

    def start(self):
        for cp in self.local + self.first:
            cp.start()

    def middle(self):
        for landed, onward in zip(self.ici_in, self.passed):
            landed.wait_recv()
            onward.start()

    def finish(self):
        for cp in self.d2d_in:
            cp.wait_recv()
        for cp in self.first + self.passed:
            cp.wait_send()
        for cp in self.local:
            cp.wait()


class _Scatter:
    def __init__(self, srcs, dsts, send_sems, recv_sems):
        pos = _mesh_position()
        n = len(srcs)
        self.local = [pltpu.make_async_copy(srcs[t].at[_index(pos)], dsts[t].at[0], recv_sems.at[t]) for t in range(n)]
        self.remote = []
        for d in range(1, N_DEV):
            to = _peer(pos, d)
            self.remote += [_remote(srcs[t].at[_index(to)], dsts[t].at[d], send_sems, recv_sems, d * n + t, to)
                            for t in range(n)]

    def start(self):
        for cp in self.local + self.remote:
            cp.start()

    def middle(self):
        pass

    def finish(self):
        for cp in self.local:
            cp.wait()
        for cp in self.remote:
            cp.wait_recv()
        for cp in self.remote:
            cp.wait_send()


HBM_SPEC = pl.BlockSpec(memory_space=pltpu.HBM)
MIDDLE_STEPS_BEFORE_END = 4


def _call(body, *, name, grid, in_specs, out_specs, out_shape, scratch_shapes, args, sides=()):
    params = _params(len(grid))
    if not sides:
        return pl.pallas_call(body, name=name, grid=grid, in_specs=in_specs, out_specs=out_specs, out_shape=out_shape,
                              scratch_shapes=scratch_shapes, compiler_params=params)(*args)
    n_in, n_out, n_scratch = len(in_specs), len(out_specs), len(scratch_shapes)
    counts = [len(srcs) for _, srcs in sides]
    ns = sum(counts)
    total = 1
    for g in grid:
        total *= g
    side_args, side_shapes, side_sems = [], [], []
    for exchange, srcs in sides:
        side_args += list(srcs)
        lead = (N_DEV,) if exchange is _Gather else ()
        side_shapes += [jax.ShapeDtypeStruct(lead + s.shape, s.dtype) for s in srcs]
        side_sems += [pltpu.SemaphoreType.DMA((N_DEV * len(srcs),))] * 2

    def wrapped(*refs):
        ins, side_in = refs[:n_in], refs[n_in:n_in + ns]
        outs = refs[n_in + ns:n_in + ns + n_out]
        side_out = refs[n_in + ns + n_out:n_in + 2 * ns + n_out]
        rest = refs[n_in + 2 * ns + n_out:]
        scratch, sems = rest[:n_scratch], rest[n_scratch:]
        step = pl.program_id(0)
        for axis in range(1, len(grid)):
            step = step * grid[axis] + pl.program_id(axis)

        def exchanges():
            built, at = [], 0
            for k, (exchange, _) in enumerate(sides):
                built.append(exchange(side_in[at:at + counts[k]], side_out[at:at + counts[k]],
                                      sems[2 * k], sems[2 * k + 1]))
                at += counts[k]
            return built

        @pl.when(step == 0)
        def _():
            for ex in exchanges():
                ex.start()

        @pl.when(step == max(total - MIDDLE_STEPS_BEFORE_END, 0))
        def _():
            for ex in exchanges():
                ex.middle()

        body(*ins, *outs, *scratch)

        @pl.when(step == total - 1)
        def _():
            for ex in exchanges():
                ex.finish()

    return pl.pallas_call(
        wrapped, name=name, grid=grid, in_specs=list(in_specs) + [HBM_SPEC] * ns,
        out_specs=list(out_specs) + [HBM_SPEC] * ns, out_shape=list(out_shape) + side_shapes,
        scratch_shapes=list(scratch_shapes) + side_sems, compiler_params=params)(*args, *side_args)


def _prenorm_gather(x, g, wie, small, woe, wio, woo, ts):
    S = x.shape[0]
    nt = S // ts

    def body(x_ref, g_ref, wie_ref, sm_ref, woe_ref, wio_ref, woo_ref,
             h_ref, wie_out, sm_out, woe_out, wio_b, woo_b, cast_ref, small_ref, cast2_ref, send_sems, recv_sems):
        step = pl.program_id(0)
        gather = lambda: _Gather([cast_ref, small_ref, cast2_ref], [wie_out, sm_out, woe_out], send_sems, recv_sems)

        @pl.when(step == 0)
        def _():
            cast_ref[...] = wie_ref[...].astype(BF16)
            small_ref[...] = sm_ref[...]
            cast2_ref[...] = woe_ref[...].astype(BF16)
            gather().start()
            wio_b[...] = wio_ref[...].astype(BF16)
            woo_b[...] = woo_ref[...].astype(BF16)

        xv = x_ref[...]
        h_ref[...] = ((xv * _rms(xv)) * g_ref[...]).astype(BF16)

        @pl.when(step == nt - 1)
        def _():
            exchange = gather()
            exchange.middle()
            exchange.finish()

    tile = pl.BlockSpec((ts, D), lambda i: (i, 0))
    sems = pltpu.SemaphoreType.DMA((N_DEV * 3,))
    shards = [wie, small, woe, wio, woo]
    return pl.pallas_call(
        body, name="prenorm_gather", grid=(nt,),
        in_specs=[tile, _full((1, D))] + [_full(a.shape) for a in shards],
        out_specs=[tile, HBM_SPEC, HBM_SPEC, HBM_SPEC] + [_full(a.shape) for a in shards[3:]],
        out_shape=[jax.ShapeDtypeStruct((S, D), BF16), jax.ShapeDtypeStruct((N_DEV,) + wie.shape, BF16),
                   jax.ShapeDtypeStruct((N_DEV,) + small.shape, F32), jax.ShapeDtypeStruct((N_DEV,) + woe.shape, BF16)]
        + [jax.ShapeDtypeStruct(a.shape, BF16) for a in shards[3:]],
        scratch_shapes=[pltpu.VMEM(wie.shape, BF16), pltpu.VMEM(small.shape, F32), pltpu.VMEM(woe.shape, BF16),
                        sems, sems],
        compiler_params=_params(),
    )(x, g, wie, small, woe, wio, woo)


def _matmul(h, w, ts, name, sides=()):
    S = h.shape[0]
    N = w.shape[1]

    def body(h_ref, w_ref, p_ref):
        p_ref[...] = _dot(h_ref[...], w_ref[...]).astype(BF16)

    return _call(
        body, name=name, grid=(S // ts,),
        in_specs=[pl.BlockSpec((ts, D), lambda i: (i, 0)), _full((D, N))],
        out_specs=[pl.BlockSpec((ts, N), lambda i: (i, 0))],
        out_shape=[jax.ShapeDtypeStruct((S, N), BF16)],
        scratch_shapes=[], args=(h, w), sides=sides)


def _even_mixer(proj, halo, row0, cw_ref, pw_ref):
    xa, gb, gc, za, xp, zp = proj
    nr = xa.shape[0]
    ha = gc * xa
    ha_ext = jnp.concatenate([halo[0], ha], axis=0)
    ha_m1 = _shift_down(ha_ext, 1)[HALO:]
    ha_m2 = _shift_down(ha_ext, 2)[HALO:]
    conv = cw_ref[2:3, :] * ha + cw_ref[1:2, :] * ha_m1 + cw_ref[0:1, :] * ha_m2
    sig_a = _sigmoid(za)
    silu_a = za * sig_a

    xp_ext = jnp.concatenate([halo[1], xp], axis=0)
    pos = row0 + lax.broadcasted_iota(jnp.int32, (nr, 1), 0)
    pooled, inv_cnt, mixed = [], [], []
    for g, w in enumerate(POOL_WINDOWS):
        cols = slice(g * GROUP, (g + 1) * GROUP)
        s = xp_ext[:, cols]
        k = 1
        while k < w:
            s = s + _shift_down(s, k)
            k *= 2
        inv = 1.0 / jnp.minimum(pos + 1, w).astype(F32)
        pg = s[HALO:] * inv - xp[:, cols]
        pooled.append(pg)
        inv_cnt.append(inv)
        mixed.append(_dot(pg.astype(BF16), pw_ref[g]))
    mixed = jnp.concatenate(mixed, axis=1)
    sig_b = _sigmoid(zp)
    silu_b = zp * sig_b
    return dict(xa=xa, gb=gb, gc=gc, za=za, xp=xp, zp=zp, ha=ha, ha_m1=ha_m1, ha_m2=ha_m2, conv=conv,
                sig_a=sig_a, silu_a=silu_a, pooled=pooled, inv_cnt=inv_cnt, mixed=mixed, sig_b=sig_b,
                silu_b=silu_b)


def _columns(a):
    return [a[:, k * A_W:(k + 1) * A_W].astype(F32) for k in range(6)]


def _even_forward_fused(h, x, win_b, cw, pw_b, ps, wout_b, gpost, gpre_next, ts, sides=()):
    S = x.shape[0]

    def body(h_ref, x_ref, wi_ref, cw_ref, pw_ref, ps_ref, wo_ref, gp_ref, gn_ref,
             p_ref, m_ref, x1_ref, h1_ref, mix_ref, carry_ref):
        i = pl.program_id(0)

        @pl.when(i == 0)
        def _():
            carry_ref[...] = jnp.zeros_like(carry_ref)

        for r0 in range(0, ts, SUB_ROWS):
            rows = slice(r0, r0 + SUB_ROWS)
            proj = _dot(h_ref[rows, :], wi_ref[...])
            p_ref[rows, :] = proj.astype(BF16)
            fw = _even_mixer(_columns(proj), (carry_ref[:, 0:A_W], carry_ref[:, A_W:D]), i * ts + r0, cw_ref, pw_ref)
            carry_ref[:, 0:A_W] = fw["ha"][SUB_ROWS - HALO:]
            carry_ref[:, A_W:D] = fw["xp"][SUB_ROWS - HALO:]
            mix_ref[rows, 0:A_W] = (fw["gb"] * fw["conv"] * fw["silu_a"]).astype(BF16)
            mix_ref[rows, A_W:D] = (fw["mixed"] * ps_ref[...] * fw["silu_b"]).astype(BF16)
            m = _dot(mix_ref[rows, :], wo_ref[...])
            m_ref[rows, :] = m
            x1 = x_ref[rows, :] + (m * _rms(m)) * gp_ref[...]
            x1_ref[rows, :] = x1
            h1_ref[rows, :] = ((x1 * _rms(x1)) * gn_ref[...]).astype(BF16)

    tile = pl.BlockSpec((ts, D), lambda i: (i, 0))
    return _call(
        body, name="even_forward", grid=(S // ts,),
        in_specs=[tile, tile, _resident((D, PROJ)), _full((3, A_W)), _full((N_GROUPS, GROUP, GROUP)), _full((1, B_W)),
                  _resident((D, D)), _full((1, D)), _full((1, D))],
        out_specs=[pl.BlockSpec((ts, PROJ), lambda i: (i, 0)), tile, tile, tile],
        out_shape=[jax.ShapeDtypeStruct((S, PROJ), BF16), jax.ShapeDtypeStruct((S, D), F32),
                   jax.ShapeDtypeStruct((S, D), F32), jax.ShapeDtypeStruct((S, D), BF16)],
        scratch_shapes=[pltpu.VMEM((ts, D), BF16), pltpu.VMEM((HALO, D), F32)],
        args=(h, x, win_b, cw, pw_b, ps, wout_b, gpost, gpre_next), sides=sides)


def _even_backward_fused(g1, m0, p, x, cw, pw_b, pwt_b, ps, woutt_b, wint_b, gpost, gpre, ts):
    S = g1.shape[0]
    nt = S // ts

    def body(g_ref, m_ref, p_ref, halo_ref, x_ref, cw_ref, pw_ref, pwt_ref, ps_ref, wot_ref, wit_ref, gp_ref, gn_ref,
             dp_ref, dx_ref, dwob_ref, dcw_ref, dpw_ref, dps_ref, dgp_ref, dgn_ref,
             carry_ref, mix_ref, dm_ref, dwo_ref):
        step = pl.program_id(0)
        i = nt - 1 - step

        @pl.when(step == 0)
        def _():
            carry_ref[...] = jnp.zeros_like(carry_ref)
            dwo_ref[...] = jnp.zeros_like(dwo_ref)
            dcw_ref[...] = jnp.zeros_like(dcw_ref)
            dpw_ref[...] = jnp.zeros_like(dpw_ref)
            dps_ref[...] = jnp.zeros_like(dps_ref)
            dgp_ref[...] = jnp.zeros_like(dgp_ref)
            dgn_ref[...] = jnp.zeros_like(dgn_ref)

        ps_v = ps_ref[...]
        sums = [None] * 6
        add = lambda k, part: sums.__setitem__(k, part if sums[k] is None else sums[k] + part)
        for r0 in range(ts - SUB_ROWS, -1, -SUB_ROWS):
            rows = slice(r0, r0 + SUB_ROWS)
            g = g_ref[rows, :]
            m = m_ref[rows, :]
            q = _rms(m)
            n = m * q
            add(0, jnp.sum(g * n, axis=0, keepdims=True))
            dn = g * gp_ref[...]
            dm = q * (dn - n * jnp.mean(dn * n, axis=-1, keepdims=True))
            dm_ref[rows, :] = dm.astype(BF16)
            dmix = _dot(dm_ref[rows, :], wot_ref[...])
            dya = dmix[:, 0:A_W]
            dyb = dmix[:, A_W:D]

            if r0 == 0:
                before = _columns(halo_ref[...])
                keep = jnp.where(i == 0, 0.0, 1.0).astype(F32)
                halo = (before[2] * before[0] * keep, before[4] * keep)
            else:
                before = _columns(p_ref[r0 - HALO:r0, :])
                halo = (before[2] * before[0], before[4])
            fw = _even_mixer(_columns(p_ref[rows, :]), halo, i * ts + r0, cw_ref, pw_ref)
            mix_ref[rows, 0:A_W] = (fw["gb"] * fw["conv"] * fw["silu_a"]).astype(BF16)
            mix_ref[rows, A_W:D] = (fw["mixed"] * ps_v * fw["silu_b"]).astype(BF16)

            t = dya * fw["gb"]
            dconv = t * fw["silu_a"]
            dgb = dya * fw["conv"] * fw["silu_a"]
            dza = t * fw["conv"] * _dsilu(fw["za"], fw["sig_a"])
            add(2, jnp.sum(dconv * fw["ha"], axis=0, keepdims=True))
            add(3, jnp.sum(dconv * fw["ha_m1"], axis=0, keepdims=True))
            add(4, jnp.sum(dconv * fw["ha_m2"], axis=0, keepdims=True))
            dconv_ext = jnp.concatenate([dconv, carry_ref[:, 0:A_W]], axis=0)
            dha = (cw_ref[2:3, :] * dconv + cw_ref[1:2, :] * _shift_up(dconv_ext, 1)[:SUB_ROWS]
                   + cw_ref[0:1, :] * _shift_up(dconv_ext, 2)[:SUB_ROWS])
            dgc = dha * fw["xa"]
            dxa = dha * fw["gc"]

            u = dyb * fw["mixed"]
            add(1, jnp.sum(u * fw["silu_b"], axis=0, keepdims=True))
            dzp = u * ps_v * _dsilu(fw["zp"], fw["sig_b"])
            dmixed = (dyb * ps_v * fw["silu_b"]).astype(BF16)
            dxp, e_first = [], []
            for gi, w in enumerate(POOL_WINDOWS):
                cols = slice(gi * GROUP, (gi + 1) * GROUP)
                dmg = dmixed[:, cols]
                dpooled = _dot(dmg, pwt_ref[gi])
                dpw_ref[gi] += _dot_t0(fw["pooled"][gi].astype(BF16), dmg)
                e = dpooled * fw["inv_cnt"][gi]
                e_first.append(e[0:HALO])
                s = jnp.concatenate([e, carry_ref[:, A_W + gi * GROUP:A_W + (gi + 1) * GROUP]], axis=0)
                k = 1
                while k < w:
                    s = s + _shift_up(s, k)
                    k *= 2
                dxp.append(s[:SUB_ROWS] - dpooled)
            carry_ref[:, 0:A_W] = dconv[0:HALO]
            carry_ref[:, A_W:D] = jnp.concatenate(e_first, axis=1)

            dp_ref[rows, 0:512] = dxa.astype(BF16)
            dp_ref[rows, 512:1024] = dgb.astype(BF16)
            dp_ref[rows, 1024:1536] = dgc.astype(BF16)
            dp_ref[rows, 1536:2048] = dza.astype(BF16)
            dp_ref[rows, 2048:2560] = jnp.concatenate(dxp, axis=1).astype(BF16)
            dp_ref[rows, 2560:3072] = dzp.astype(BF16)

            dh = _dot(dp_ref[rows, :], wit_ref[...])
            xv = x_ref[rows, :]
            r = _rms(xv)
            xn = xv * r
            add(5, jnp.sum(dh * xn, axis=0, keepdims=True))
            dxn = dh * gn_ref[...]
            dx_ref[rows, :] = g + r * (dxn - xn * jnp.mean(dxn * xn, axis=-1, keepdims=True))

        dwo_ref[...] += _dot_t0(mix_ref[...], dm_ref[...])
        dgp_ref[...] += sums[0]
        dps_ref[...] += sums[1]
        dcw_ref[2:3, :] += sums[2]
        dcw_ref[1:2, :] += sums[3]
        dcw_ref[0:1, :] += sums[4]
        dgn_ref[...] += sums[5]

        @pl.when(step == nt - 1)
        def _():
            dwob_ref[...] = dwo_ref[...].astype(BF16)

    rev = lambda s: (nt - 1 - s, 0)
    tile = pl.BlockSpec((ts, D), rev)
    wide = pl.BlockSpec((ts, PROJ), rev)
    bpt = ts // HALO
    halo_map = lambda s: (jnp.maximum((nt - 1 - s) * bpt - 1, 0), 0)
    groups = _full((N_GROUPS, GROUP, GROUP))
    vec = _full((1, D))
    return pl.pallas_call(
        body, name="even_backward", grid=(nt,),
        in_specs=[tile, tile, wide, pl.BlockSpec((HALO, PROJ), halo_map), tile, _full((3, A_W)), groups, groups,
                  _full((1, B_W)), _resident((D, D)), _resident((PROJ, D)), vec, vec],
        out_specs=[wide, tile, _full((D, D)), _full((8, A_W)), groups, _full((1, B_W)), vec, vec],
        out_shape=[jax.ShapeDtypeStruct((S, PROJ), BF16), jax.ShapeDtypeStruct((S, D), F32),
                   jax.ShapeDtypeStruct((D, D), BF16), jax.ShapeDtypeStruct((8, A_W), F32),
                   jax.ShapeDtypeStruct((N_GROUPS, GROUP, GROUP), F32), jax.ShapeDtypeStruct((1, B_W), F32),
                   jax.ShapeDtypeStruct((1, D), F32), jax.ShapeDtypeStruct((1, D), F32)],
        scratch_shapes=[pltpu.VMEM((HALO, D), F32), pltpu.VMEM((ts, D), BF16), pltpu.VMEM((ts, D), BF16),
                        pltpu.VMEM((D, D), F32)],
        compiler_params=_params(),
    )(g1, m0, p, p, x, cw, pw_b, pwt_b, ps, woutt_b, wint_b, gpost, gpre)


def _even_forward_parts(p_ref, halo_ref, is_first, row0, cw_ref, pw_ref, ts):
    f = lambda a: a.astype(F32)
    xa = f(p_ref[:, 0:512])
    gb = f(p_ref[:, 512:1024])
    gc = f(p_ref[:, 1024:1536])
    za = f(p_ref[:, 1536:2048])
    xp = f(p_ref[:, 2048:2560])
    zp = f(p_ref[:, 2560:3072])
    keep = jnp.where(is_first, 0.0, 1.0).astype(F32)
    ha = gc * xa
    ha_halo = f(halo_ref[:, 1024:1536]) * f(halo_ref[:, 0:512]) * keep
    xp_halo = f(halo_ref[:, 2048:2560]) * keep
    ha_ext = jnp.concatenate([ha_halo, ha], axis=0)
    ha_m1 = _shift_down(ha_ext, 1)[HALO:]
    ha_m2 = _shift_down(ha_ext, 2)[HALO:]
    conv = cw_ref[2:3, :] * ha + cw_ref[1:2, :] * ha_m1 + cw_ref[0:1, :] * ha_m2
    sig_a = _sigmoid(za)
    silu_a = za * sig_a

    xp_ext = jnp.concatenate([xp_halo, xp], axis=0)
    pos = row0 + lax.broadcasted_iota(jnp.int32, (ts, 1), 0)
    pooled, inv_cnt, mixed = [], [], []
    for g, w in enumerate(POOL_WINDOWS):
        cols = slice(g * GROUP, (g + 1) * GROUP)
        s = xp_ext[:, cols]
        k = 1
        while k < w:
            s = s + _shift_down(s, k)
            k *= 2
        inv = 1.0 / jnp.minimum(pos + 1, w).astype(F32)
        pg = s[HALO:] * inv - xp[:, cols]
        pooled.append(pg)
        inv_cnt.append(inv)
        mixed.append(_dot(pg.astype(BF16), pw_ref[g]))
    mixed = jnp.concatenate(mixed, axis=1)
    sig_b = _sigmoid(zp)
    silu_b = zp * sig_b
    return dict(xa=xa, gb=gb, gc=gc, za=za, xp=xp, zp=zp, ha=ha, ha_m1=ha_m1, ha_m2=ha_m2, conv=conv,
                sig_a=sig_a, silu_a=silu_a, pooled=pooled, inv_cnt=inv_cnt, mixed=mixed, sig_b=sig_b,
                silu_b=silu_b)


def _halo_index(ts):
    blocks_per_tile = ts // HALO
    return lambda i: (jnp.maximum(i * blocks_per_tile - 1, 0), 0)


def _even_forward(p, x, cw, pw_b, ps, wout_b, gpost, gpre_next, ts, sides=()):
    S = x.shape[0]

    def body(p_ref, halo_ref, x_ref, cw_ref, pw_ref, ps_ref, wo_ref, gp_ref, gn_ref, m_ref, x1_ref, h1_ref, mix_ref):
        i = pl.program_id(0)
        fw = _even_forward_parts(p_ref, halo_ref, i == 0, i * ts, cw_ref, pw_ref, ts)
        mix_ref[:, 0:A_W] = (fw["gb"] * fw["conv"] * fw["silu_a"]).astype(BF16)
        mix_ref[:, A_W:D] = (fw["mixed"] * ps_ref[...] * fw["silu_b"]).astype(BF16)
        m = _dot(mix_ref[...], wo_ref[...])
        m_ref[...] = m
        x1 = x_ref[...] + (m * _rms(m)) * gp_ref[...]
        x1_ref[...] = x1
        h1_ref[...] = ((x1 * _rms(x1)) * gn_ref[...]).astype(BF16)

    tile = pl.BlockSpec((ts, D), lambda i: (i, 0))
    return _call(
        body, name="even_forward", grid=(S // ts,),
        in_specs=[pl.BlockSpec((ts, PROJ), lambda i: (i, 0)), pl.BlockSpec((HALO, PROJ), _halo_index(ts)), tile,
                  _full((3, A_W)), _full((N_GROUPS, GROUP, GROUP)), _full((1, B_W)), _full((D, D)), _full((1, D)),
                  _full((1, D))],
        out_specs=[tile, tile, tile],
        out_shape=[jax.ShapeDtypeStruct((S, D), F32), jax.ShapeDtypeStruct((S, D), F32),
                   jax.ShapeDtypeStruct((S, D), BF16)],
        scratch_shapes=[pltpu.VMEM((ts, D), BF16)],
        args=(p, p, x, cw, pw_b, ps, wout_b, gpost, gpre_next), sides=sides)


def _even_backward(g1, m0, p, cw, pw_b, pwt_b, ps, woutt_b, gpost, ts, sides=()):
    S = g1.shape[0]
    nt = S // ts

    def body(g_ref, m_ref, p_ref, halo_ref, cw_ref, pw_ref, pwt_ref, ps_ref, wot_ref, gp_ref,
             dp_ref, dwob_ref, dcw_ref, dpw_ref, dps_ref, dgp_ref, carry_ref, mix_ref, dm_ref, dwo_ref):
        step = pl.program_id(0)
        i = nt - 1 - step

        @pl.when(step == 0)
        def _():
            carry_ref[...] = jnp.zeros_like(carry_ref)
            dwo_ref[...] = jnp.zeros_like(dwo_ref)
            dcw_ref[...] = jnp.zeros_like(dcw_ref)
            dpw_ref[...] = jnp.zeros_like(dpw_ref)
            dps_ref[...] = jnp.zeros_like(dps_ref)
            dgp_ref[...] = jnp.zeros_like(dgp_ref)

        g = g_ref[...]
        m = m_ref[...]
        q = _rms(m)
        n = m * q
        dgp_ref[...] += jnp.sum(g * n, axis=0, keepdims=True)
        dn = g * gp_ref[...]
        dm = q * (dn - n * jnp.mean(dn * n, axis=-1, keepdims=True))
        dm_ref[...] = dm.astype(BF16)
        dmix = _dot(dm_ref[...], wot_ref[...])
        dya = dmix[:, 0:A_W]
        dyb = dmix[:, A_W:D]

        fw = _even_forward_parts(p_ref, halo_ref, i == 0, i * ts, cw_ref, pw_ref, ts)
        ps_v = ps_ref[...]
        mix_ref[:, 0:A_W] = (fw["gb"] * fw["conv"] * fw["silu_a"]).astype(BF16)
        mix_ref[:, A_W:D] = (fw["mixed"] * ps_v * fw["silu_b"]).astype(BF16)
        dwo_ref[...] += _dot_t0(mix_ref[...], dm_ref[...])

        t = dya * fw["gb"]
        dconv = t * fw["silu_a"]
        dgb = dya * fw["conv"] * fw["silu_a"]
        dza = t * fw["conv"] * _dsilu(fw["za"], fw["sig_a"])
        dcw_ref[2:3, :] += jnp.sum(dconv * fw["ha"], axis=0, keepdims=True)
        dcw_ref[1:2, :] += jnp.sum(dconv * fw["ha_m1"], axis=0, keepdims=True)
        dcw_ref[0:1, :] += jnp.sum(dconv * fw["ha_m2"], axis=0, keepdims=True)
        dconv_ext = jnp.concatenate([dconv, carry_ref[:, 0:A_W]], axis=0)
        dha = (cw_ref[2:3, :] * dconv + cw_ref[1:2, :] * _shift_up(dconv_ext, 1)[:ts]
               + cw_ref[0:1, :] * _shift_up(dconv_ext, 2)[:ts])
        dgc = dha * fw["xa"]
        dxa = dha * fw["gc"]

        u = dyb * fw["mixed"]
        dps_ref[...] += jnp.sum(u * fw["silu_b"], axis=0, keepdims=True)
        dzp = u * ps_v * _dsilu(fw["zp"], fw["sig_b"])
        dmixed = (dyb * ps_v * fw["silu_b"]).astype(BF16)
        dxp, e_first = [], []
        for gi, w in enumerate(POOL_WINDOWS):
            cols = slice(gi * GROUP, (gi + 1) * GROUP)
            dmg = dmixed[:, cols]
            dpooled = _dot(dmg, pwt_ref[gi])
            dpw_ref[gi] += _dot_t0(fw["pooled"][gi].astype(BF16), dmg)
            e = dpooled * fw["inv_cnt"][gi]
            e_first.append(e[0:HALO])
            s = jnp.concatenate([e, carry_ref[:, A_W + gi * GROUP:A_W + (gi + 1) * GROUP]], axis=0)
            k = 1
            while k < w:
                s = s + _shift_up(s, k)
                k *= 2
            dxp.append(s[:ts] - dpooled)

        carry_ref[:, 0:A_W] = dconv[0:HALO]
        carry_ref[:, A_W:D] = jnp.concatenate(e_first, axis=1)

        dp_ref[:, 0:512] = dxa.astype(BF16)
        dp_ref[:, 512:1024] = dgb.astype(BF16)
        dp_ref[:, 1024:1536] = dgc.astype(BF16)
        dp_ref[:, 1536:2048] = dza.astype(BF16)
        dp_ref[:, 2048:2560] = jnp.concatenate(dxp, axis=1).astype(BF16)
        dp_ref[:, 2560:3072] = dzp.astype(BF16)

        @pl.when(step == nt - 1)
        def _():
            dwob_ref[...] = dwo_ref[...].astype(BF16)

    rev = lambda s: (nt - 1 - s, 0)
    tile = pl.BlockSpec((ts, D), rev)
    bpt = ts // HALO
    halo_map = lambda s: (jnp.maximum((nt - 1 - s) * bpt - 1, 0), 0)
    return _call(
        body, name="even_backward", grid=(nt,),
        in_specs=[tile, tile, pl.BlockSpec((ts, PROJ), rev), pl.BlockSpec((HALO, PROJ), halo_map),
                  _full((3, A_W)), _full((N_GROUPS, GROUP, GROUP)), _full((N_GROUPS, GROUP, GROUP)),
                  _full((1, B_W)), _full((D, D)), _full((1, D))],
        out_specs=[pl.BlockSpec((ts, PROJ), rev), _full((D, D)), _full((8, A_W)),
                   _full((N_GROUPS, GROUP, GROUP)), _full((1, B_W)), _full((1, D))],
        out_shape=[jax.ShapeDtypeStruct((S, PROJ), BF16), jax.ShapeDtypeStruct((D, D), BF16),
                   jax.ShapeDtypeStruct((8, A_W), F32), jax.ShapeDtypeStruct((N_GROUPS, GROUP, GROUP), F32),
                   jax.ShapeDtypeStruct((1, B_W), F32), jax.ShapeDtypeStruct((1, D), F32)],
        scratch_shapes=[pltpu.VMEM((HALO, D), F32), pltpu.VMEM((ts, D), BF16), pltpu.VMEM((ts, D), BF16),
                        pltpu.VMEM((D, D), F32)],
        args=(g1, m0, p, p, cw, pw_b, pwt_b, ps, woutt_b, gpost), sides=sides)


def _odd_forward_parts(v, lg_ref, lb_ref, wt_ref, bfull_ref, vln_ref, sv_ref, r0, nr):
    rows = slice(r0, r0 + nr)
    mu = jnp.mean(v, axis=-1, keepdims=True)
    vc = v - mu
    rstd = lax.rsqrt(jnp.mean(vc * vc, axis=-1, keepdims=True) + EPS)
    vh = vc * rstd
    vln_ref[rows, :] = (vh * lg_ref[...] + lb_ref[...]).astype(BF16)
    for c in range(nr // GROUP):
        chunk = slice(r0 + c * GROUP, r0 + (c + 1) * GROUP)
        for h in range(N_HEADS):
            cols = slice(h * GROUP, (h + 1) * GROUP)
            sv_ref[chunk, cols] = _dot(wt_ref[h], vln_ref[chunk, cols]) + bfull_ref[h]
    return vh, rstd


SUB_ROWS = 256


def _odd_forward(h, x1, tgt, win_b, lg, lb, wt_b, bfull, wout_b, gpost, ts):
    S = x1.shape[0]

    def body(h_ref, x_ref, t_ref, wi_ref, lg_ref, lb_ref, wt_ref, bfull_ref, wo_ref, gp_ref,
             p_ref, m_ref, g_ref, loss_ref, vln_ref, sv_ref, y_ref):
        @pl.when(pl.program_id(0) == 0)
        def _():
            loss_ref[...] = jnp.zeros_like(loss_ref)

        loss = None
        for r0 in range(0, ts, SUB_ROWS):
            rows = slice(r0, r0 + SUB_ROWS)
            proj = _dot(h_ref[rows, :], wi_ref[...])
            p_ref[rows, :] = proj.astype(BF16)
            u, v, z = proj[:, 0:D], proj[:, D:2 * D], proj[:, 2 * D:3 * D]
            _odd_forward_parts(v, lg_ref, lb_ref, wt_ref, bfull_ref, vln_ref, sv_ref, r0, SUB_ROWS)
            y_ref[rows, :] = (u * sv_ref[rows, :] * (z * _sigmoid(z))).astype(BF16)
            m = _dot(y_ref[rows, :], wo_ref[...])
            m_ref[rows, :] = m
            x2 = x_ref[rows, :] + (m * _rms(m)) * gp_ref[...]
            err = x2 - t_ref[rows, :]
            g_ref[rows, :] = err * (1.0 / D)
            part = jnp.sum(err * err, axis=0, keepdims=True)
            loss = part if loss is None else loss + part
        loss_ref[...] += loss

    tile = pl.BlockSpec((ts, D), lambda i: (i, 0))
    small = _full((N_HEADS, GROUP, GROUP))
    return pl.pallas_call(
        body, name="odd_forward", grid=(S // ts,),
        in_specs=[pl.BlockSpec((ts, D), lambda i: (i, 0)), tile, tile, _full((D, PROJ)), _full((1, D)), _full((1, D)),
                  small, small, _full((D, D)), _full((1, D))],
        out_specs=[pl.BlockSpec((ts, PROJ), lambda i: (i, 0)), tile, tile, _full((1, D))],
        out_shape=[jax.ShapeDtypeStruct((S, PROJ), BF16), jax.ShapeDtypeStruct((S, D), F32),
                   jax.ShapeDtypeStruct((S, D), F32), jax.ShapeDtypeStruct((1, D), F32)],
        scratch_shapes=[pltpu.VMEM((ts, D), BF16), pltpu.VMEM((ts, D), F32), pltpu.VMEM((ts, D), BF16)],
        compiler_params=_params(),
    )(h, x1, tgt, win_b, lg, lb, wt_b, bfull, wout_b, gpost)


def _resident(shape):
    return pl.BlockSpec(shape, lambda *_: (0,) * len(shape), pipeline_mode=pl.Buffered(1))


def _odd_backward(g2, m1, p, x1, lg, lb, wt_b, wtt_b, bfull, woutt_b, wint_b, gpost, gpre, ts):
    S = g2.shape[0]
    nt = S // ts
    sub = ts // 2

    def body(g_ref, m_ref, p_ref, x_ref, lg_ref, lb_ref, wt_ref, wtt_ref, bfull_ref, wot_ref, wit_ref, gp_ref, gn_ref,
             dp_ref, dx_ref, dwob_ref, dws_ref, dbs_ref, dlg_ref, dlb_ref, dgp_ref, dgn_ref,
             vln_ref, sv_ref, y_ref, dm_ref, dsvb_ref, dvln_ref, dsum_ref, dwo_ref):
        step = pl.program_id(0)

        @pl.when(step == 0)
        def _():
            dwo_ref[...] = jnp.zeros_like(dwo_ref)
            dws_ref[...] = jnp.zeros_like(dws_ref)
            dsum_ref[...] = jnp.zeros_like(dsum_ref)
            dlg_ref[...] = jnp.zeros_like(dlg_ref)
            dlb_ref[...] = jnp.zeros_like(dlb_ref)
            dgp_ref[...] = jnp.zeros_like(dgp_ref)
            dgn_ref[...] = jnp.zeros_like(dgn_ref)

        tril = (lax.broadcasted_iota(jnp.int32, (GROUP, GROUP), 0)
                >= lax.broadcasted_iota(jnp.int32, (GROUP, GROUP), 1))
        sums = [None] * 4
        add = lambda k, part: sums.__setitem__(k, part if sums[k] is None else sums[k] + part)
        for r0 in range(0, ts, sub):
            rows = slice(r0, r0 + sub)
            g = g_ref[rows, :]
            m = m_ref[rows, :]
            q = _rms(m)
            n = m * q
            add(0, jnp.sum(g * n, axis=0, keepdims=True))
            dn = g * gp_ref[...]
            dm = q * (dn - n * jnp.mean(dn * n, axis=-1, keepdims=True))
            dm_ref[rows, :] = dm.astype(BF16)
            dy = _dot(dm_ref[rows, :], wot_ref[...])

            u = p_ref[rows, 0:D].astype(F32)
            z = p_ref[rows, 2 * D:3 * D].astype(F32)
            vh, rstd = _odd_forward_parts(p_ref[rows, D:2 * D].astype(F32), lg_ref, lb_ref, wt_ref, bfull_ref,
                                          vln_ref, sv_ref, r0, sub)
            sv = sv_ref[rows, :]
            sig = _sigmoid(z)
            sz = z * sig
            y_ref[rows, :] = (u * sv * sz).astype(BF16)
            t = dy * sz
            du = t * sv
            dsv = t * u
            dz = dy * u * sv * _dsilu(z, sig)
            dsvb_ref[rows, :] = dsv.astype(BF16)
            for c in range(sub // GROUP):
                chunk = slice(r0 + c * GROUP, r0 + (c + 1) * GROUP)
                dsum_ref[...] += dsv[c * GROUP:(c + 1) * GROUP]
                for h in range(N_HEADS):
                    cols = slice(h * GROUP, (h + 1) * GROUP)
                    dvln_ref[chunk, cols] = _dot(wtt_ref[h], dsvb_ref[chunk, cols])
                    dws_ref[h] += jnp.where(tril, _dot_t1(dsvb_ref[chunk, cols], vln_ref[chunk, cols]), 0.0)
            dvln = dvln_ref[rows, :]
            add(1, jnp.sum(dvln * vh, axis=0, keepdims=True))
            add(2, jnp.sum(dvln, axis=0, keepdims=True))
            dvh = dvln * lg_ref[...]
            dv = rstd * (dvh - jnp.mean(dvh, axis=-1, keepdims=True)
                         - vh * jnp.mean(dvh * vh, axis=-1, keepdims=True))
            dp_ref[rows, 0:D] = du.astype(BF16)
            dp_ref[rows, D:2 * D] = dv.astype(BF16)
            dp_ref[rows, 2 * D:3 * D] = dz.astype(BF16)

            dh = _dot(dp_ref[rows, :], wit_ref[...])
            xv = x_ref[rows, :]
            r = _rms(xv)
            xn = xv * r
            add(3, jnp.sum(dh * xn, axis=0, keepdims=True))
            dxn = dh * gn_ref[...]
            dx_ref[rows, :] = g + r * (dxn - xn * jnp.mean(dxn * xn, axis=-1, keepdims=True))

        dwo_ref[...] += _dot_t0(y_ref[...], dm_ref[...])
        dgp_ref[...] += sums[0]
        dlg_ref[...] += sums[1]
        dlb_ref[...] += sums[2]
        dgn_ref[...] += sums[3]

        @pl.when(step == nt - 1)
        def _():
            dwob_ref[...] = dwo_ref[...].astype(BF16)
            ones = jnp.ones((8, GROUP), F32)
            for h in range(N_HEADS):
                cols = slice(h * GROUP, (h + 1) * GROUP)
                total = lax.dot_general(ones, dsum_ref[:, cols], (((1,), (1,)), ((), ())),
                                        precision=lax.Precision.HIGHEST, preferred_element_type=F32)
                dbs_ref[h:h + 1, :] = total[0:1, :]

    tile = pl.BlockSpec((ts, D), lambda i: (i, 0))
    wide = pl.BlockSpec((ts, PROJ), lambda i: (i, 0))
    small = _full((N_HEADS, GROUP, GROUP))
    heads = _resident((N_HEADS, GROUP, GROUP))
    vec = _full((1, D))
    return pl.pallas_call(
        body, name="odd_backward", grid=(nt,),
        in_specs=[tile, tile, wide, tile, vec, vec, heads, heads, heads, _resident((D, D)), _resident((PROJ, D)),
                  vec, vec],
        out_specs=[wide, tile, _full((D, D)), small, _full((N_HEADS, GROUP)), vec, vec, vec, vec],
        out_shape=[jax.ShapeDtypeStruct((S, PROJ), BF16), jax.ShapeDtypeStruct((S, D), F32),
                   jax.ShapeDtypeStruct((D, D), BF16),
                   jax.ShapeDtypeStruct((N_HEADS, GROUP, GROUP), F32), jax.ShapeDtypeStruct((N_HEADS, GROUP), F32),
                   jax.ShapeDtypeStruct((1, D), F32), jax.ShapeDtypeStruct((1, D), F32),
                   jax.ShapeDtypeStruct((1, D), F32), jax.ShapeDtypeStruct((1, D), F32)],
        scratch_shapes=[pltpu.VMEM((ts, D), BF16), pltpu.VMEM((ts, D), F32), pltpu.VMEM((ts, D), BF16),
                        pltpu.VMEM((ts, D), BF16), pltpu.VMEM((ts, D), BF16),
                        pltpu.VMEM((ts, D), F32), pltpu.VMEM((GROUP, D), F32), pltpu.VMEM((D, D), F32)],
        compiler_params=_params(),
    )(g2, m1, p, x1, lg, lb, wt_b, wtt_b, bfull, woutt_b, wint_b, gpost, gpre)


def _odd_backward_unfused(g2, m1, p, lg, lb, wt_b, wtt_b, bfull, woutt_b, gpost, ts):
    S = g2.shape[0]
    nt = S // ts

    def body(g_ref, m_ref, p_ref, lg_ref, lb_ref, wt_ref, wtt_ref, bfull_ref, wot_ref, gp_ref,
             dp_ref, dwob_ref, dws_ref, dbs_ref, dlg_ref, dlb_ref, dgp_ref,
             vln_ref, sv_ref, y_ref, dm_ref, dsv_ref, dsvb_ref, dvln_ref, dsum_ref, dwo_ref):
        step = pl.program_id(0)

        @pl.when(step == 0)
        def _():
            dwo_ref[...] = jnp.zeros_like(dwo_ref)
            dws_ref[...] = jnp.zeros_like(dws_ref)
            dsum_ref[...] = jnp.zeros_like(dsum_ref)
            dlg_ref[...] = jnp.zeros_like(dlg_ref)
            dlb_ref[...] = jnp.zeros_like(dlb_ref)
            dgp_ref[...] = jnp.zeros_like(dgp_ref)

        g = g_ref[...]
        m = m_ref[...]
        q = _rms(m)
        n = m * q
        dgp_ref[...] += jnp.sum(g * n, axis=0, keepdims=True)
        dn = g * gp_ref[...]
        dm = q * (dn - n * jnp.mean(dn * n, axis=-1, keepdims=True))
        dm_ref[...] = dm.astype(BF16)
        dy = _dot(dm_ref[...], wot_ref[...])

        u = p_ref[:, 0:D].astype(F32)
        z = p_ref[:, 2 * D:3 * D].astype(F32)
        vh, rstd = _odd_forward_parts(p_ref[:, D:2 * D].astype(F32), lg_ref, lb_ref, wt_ref, bfull_ref, vln_ref,
                                      sv_ref, 0, ts)
        sv = sv_ref[...]
        sig = _sigmoid(z)
        sz = z * sig
        y_ref[...] = (u * sv * sz).astype(BF16)
        dwo_ref[...] += _dot_t0(y_ref[...], dm_ref[...])

        t = dy * sz
        du = t * sv
        dsv = t * u
        dz = dy * u * sv * _dsilu(z, sig)
        dsv_ref[...] = dsv
        dsvb_ref[...] = dsv.astype(BF16)

        tril = (lax.broadcasted_iota(jnp.int32, (GROUP, GROUP), 0)
                >= lax.broadcasted_iota(jnp.int32, (GROUP, GROUP), 1))
        for c in range(ts // GROUP):
            rows = slice(c * GROUP, (c + 1) * GROUP)
            dsum_ref[...] += dsv_ref[rows, :]
            for h in range(N_HEADS):
                cols = slice(h * GROUP, (h + 1) * GROUP)
                dvln_ref[rows, cols] = _dot(wtt_ref[h], dsvb_ref[rows, cols])
                dws_ref[h] += jnp.where(tril, _dot_t1(dsvb_ref[rows, cols], vln_ref[rows, cols]), 0.0)

        dvln = dvln_ref[...]
        dlg_ref[...] += jnp.sum(dvln * vh, axis=0, keepdims=True)
        dlb_ref[...] += jnp.sum(dvln, axis=0, keepdims=True)
        dvh = dvln * lg_ref[...]
        dv = rstd * (dvh - jnp.mean(dvh, axis=-1, keepdims=True)
                     - vh * jnp.mean(dvh * vh, axis=-1, keepdims=True))
        dp_ref[:, 0:D] = du.astype(BF16)
        dp_ref[:, D:2 * D] = dv.astype(BF16)
        dp_ref[:, 2 * D:3 * D] = dz.astype(BF16)

        @pl.when(step == nt - 1)
        def _():
            dwob_ref[...] = dwo_ref[...].astype(BF16)
            ones = jnp.ones((8, GROUP), F32)
            for h in range(N_HEADS):
                cols = slice(h * GROUP, (h + 1) * GROUP)
                sums = lax.dot_general(ones, dsum_ref[:, cols], (((1,), (1,)), ((), ())),
                                       precision=lax.Precision.HIGHEST, preferred_element_type=F32)
                dbs_ref[h:h + 1, :] = sums[0:1, :]

    tile = pl.BlockSpec((ts, D), lambda i: (i, 0))
    small = _full((N_HEADS, GROUP, GROUP))
    vec = _full((1, D))
    return pl.pallas_call(
        body, name="odd_backward", grid=(nt,),
        in_specs=[tile, tile, pl.BlockSpec((ts, PROJ), lambda i: (i, 0)), vec, vec, small, small, small,
                  _full((D, D)), vec],
        out_specs=[pl.BlockSpec((ts, PROJ), lambda i: (i, 0)), _full((D, D)), small, _full((N_HEADS, GROUP)),
                   vec, vec, vec],
        out_shape=[jax.ShapeDtypeStruct((S, PROJ), BF16), jax.ShapeDtypeStruct((D, D), BF16),
                   jax.ShapeDtypeStruct((N_HEADS, GROUP, GROUP), F32), jax.ShapeDtypeStruct((N_HEADS, GROUP), F32),
                   jax.ShapeDtypeStruct((1, D), F32), jax.ShapeDtypeStruct((1, D), F32),
                   jax.ShapeDtypeStruct((1, D), F32)],
        scratch_shapes=[pltpu.VMEM((ts, D), BF16), pltpu.VMEM((ts, D), F32), pltpu.VMEM((ts, D), BF16),
                        pltpu.VMEM((ts, D), BF16), pltpu.VMEM((ts, D), F32), pltpu.VMEM((ts, D), BF16),
                        pltpu.VMEM((ts, D), F32), pltpu.VMEM((GROUP, D), F32), pltpu.VMEM((D, D), F32)],
        compiler_params=_params(),
    )(g2, m1, p, lg, lb, wt_b, wtt_b, bfull, woutt_b, gpost)


def _input_backward(dp, wt_b, x, g_res, gpre, ts, name, sides=()):
    S = x.shape[0]

    def body(dp_ref, wt_ref, x_ref, g_ref, gp_ref, dx_ref, dgp_ref):
        @pl.when(pl.program_id(0) == 0)
        def _():
            dgp_ref[...] = jnp.zeros_like(dgp_ref)

        dgp = None
        for r0 in range(0, ts, SUB_ROWS):
            rows = slice(r0, r0 + SUB_ROWS)
            dh = _dot(dp_ref[rows, :], wt_ref[...])
            xv = x_ref[rows, :]
            r = _rms(xv)
            xn = xv * r
            part = jnp.sum(dh * xn, axis=0, keepdims=True)
            dgp = part if dgp is None else dgp + part
            dxn = dh * gp_ref[...]
            dx_ref[rows, :] = g_ref[rows, :] + r * (dxn - xn * jnp.mean(dxn * xn, axis=-1, keepdims=True))
        dgp_ref[...] += dgp

    tile = pl.BlockSpec((ts, D), lambda i: (i, 0))
    return _call(
        body, name=name, grid=(S // ts,),
        in_specs=[pl.BlockSpec((ts, PROJ), lambda i: (i, 0)), _full((PROJ, D)), tile, tile, _full((1, D))],
        out_specs=[tile, _full((1, D))],
        out_shape=[jax.ShapeDtypeStruct((S, D), F32), jax.ShapeDtypeStruct((1, D), F32)],
        scratch_shapes=[], args=(dp, wt_b, x, g_res, gpre), sides=sides)


N_CHIPS = 4


def _weight_grad_scatter(h, dp, ts, name, sides=()):
    S = h.shape[0]
    ts = min(ts, S)
    nt = S // ts
    panel = 2 * SHARD_IN
    last = N_CHIPS - 1

    def body(h_ref, dp_ref, parts_ref, acc_ref, half_ref, swap_ref, sum_ref, send_sems, recv_sems):
        s, t = pl.program_id(0), pl.program_id(1)
        x, y, c = _mesh_position()
        sibling = (x, y, 1 - c)

        def swap(k):
            return _remote(half_ref.at[k, 1], swap_ref.at[k], send_sems, recv_sems, k, sibling)

        def chip_sum(k):
            flip = last - k
            to = (x ^ (flip >> 1), y ^ (flip & 1), c)
            return _remote(sum_ref.at[k], parts_ref.at[1 + flip], send_sems, recv_sems, 3 + flip, to)

        own = pltpu.make_async_copy(half_ref.at[last, 0], parts_ref.at[0], recv_sems.at[7])
        to_sibling = _remote(half_ref.at[last, 1], parts_ref.at[1], send_sems, recv_sems, 3, sibling)

        @pl.when(t == 0)
        def _():
            acc_ref[...] = jnp.zeros_like(acc_ref)

        for k in range(last):
            @pl.when((s == k + 1) & (t == 0))
            def _():
                swap(k).wait_recv()
                sum_ref[k] = (half_ref[k, 0].astype(F32) + swap_ref[k].astype(F32)).astype(BF16)
                chip_sum(k).start()

        acc_ref[...] += _dot_t0(h_ref[...], dp_ref[...])

        @pl.when(t == nt - 1)
        def _():
            for core in range(2):
                @pl.when(c == core)
                def _():
                    half_ref[s, core] = acc_ref[:, 0:SHARD_IN].astype(BF16)
                    half_ref[s, 1 - core] = acc_ref[:, SHARD_IN:panel].astype(BF16)
            for k in range(last):
                @pl.when(s == k)
                def _():
                    swap(k).start()

        @pl.when((s == last) & (t == nt - 1))
        def _():
            own.start()
            to_sibling.start()
            own.wait()
            _remote(half_ref.at[last, 1], parts_ref.at[1], send_sems, recv_sems, 3, sibling).wait_recv()
            for k in range(last):
                chip_sum(k).wait_recv()
            to_sibling.wait_send()
            for k in range(last):
                swap(k).wait_send()
                chip_sum(k).wait_send()

    chip_panel = lambda s, t: (t, (2 * lax.axis_index("x") + lax.axis_index("y")) ^ (last - s))
    sems = pltpu.SemaphoreType.DMA((8,))
    return _call(
        body, name=name, grid=(N_CHIPS, nt),
        in_specs=[pl.BlockSpec((ts, D), lambda s, t: (t, 0)), pl.BlockSpec((ts, panel), chip_panel)],
        out_specs=[HBM_SPEC], out_shape=[jax.ShapeDtypeStruct((N_CHIPS + 1, D, SHARD_IN), BF16)],
        scratch_shapes=[pltpu.VMEM((D, panel), F32), pltpu.VMEM((N_CHIPS, 2, D, SHARD_IN), BF16),
                        pltpu.VMEM((last, D, SHARD_IN), BF16), pltpu.VMEM((last, D, SHARD_IN), BF16), sems, sems],
        args=(h, dp), sides=sides)


def _adamw(w, g, m, v):
    m = ADAM_B1 * m + (1.0 - ADAM_B1) * g
    v = ADAM_B2 * v + (1.0 - ADAM_B2) * (g * g)
    m_hat = m / (1.0 - ADAM_B1 ** ADAM_STEP)
    v_hat = v / (1.0 - ADAM_B2 ** ADAM_STEP)
    delta = -ADAM_LR * (m_hat / (jnp.sqrt(v_hat) + ADAM_EPS) + ADAM_WD * w)
    return delta, m, v


def _sum_update(parts, w, m, v, name):
    R, C = w.shape
    rb = min(R, 256)
    n_parts = parts.shape[0]

    def body(p_ref, w_ref, m_ref, v_ref, g_out, d_out, m_out, v_out):
        acc = p_ref[0].astype(F32)
        for d in range(1, n_parts):
            acc = acc + p_ref[d].astype(F32)
        g_out[...] = acc
        delta, m_new, v_new = _adamw(w_ref[...], acc, m_ref[...], v_ref[...])
        d_out[...] = delta
        m_out[...] = m_new
        v_out[...] = v_new

    blk = pl.BlockSpec((rb, C), lambda i: (i, 0))
    return pl.pallas_call(
        body, name=name, grid=(R // rb,),
        in_specs=[pl.BlockSpec((n_parts, rb, C), lambda i: (0, i, 0)), blk, blk, blk], out_specs=[blk] * 4,
        out_shape=[jax.ShapeDtypeStruct((R, C), F32)] * 4, compiler_params=_params(),
    )(parts, w, m, v)


def _small_sum_update(partials, row_of, weights, loss_parts):
    n_p, n_w = len(partials), len(weights)

    def body(*refs):
        p_refs = refs[:n_p]
        loss_ref = refs[n_p]
        w_refs = refs[n_p + 1:n_p + 1 + 3 * n_w]
        loss_out = refs[n_p + 1 + 3 * n_w]
        outs = refs[n_p + 2 + 3 * n_w:]

        def total(ref):
            acc = ref[0]
            for d in range(1, N_DEV):
                acc = acc + ref[d]
            return acc

        lsum = jnp.sum(total(loss_ref), axis=-1, keepdims=True) * (0.5 / D)
        loss_out[...] = jnp.broadcast_to(lsum, loss_out.shape)
        for k in range(n_p):
            i, r0 = row_of[k]
            g = total(p_refs[k])
            w_ref, m_ref, v_ref = w_refs[3 * i:3 * i + 3]
            g_out, d_out, m_out, v_out = outs[4 * i:4 * i + 4]
            if g.ndim == 2:
                at = (slice(r0, r0 + g.shape[0]), slice(None))
            else:
                at = (slice(None),) * g.ndim
            delta, m_new, v_new = _adamw(w_ref[at], g, m_ref[at], v_ref[at])
            g_out[at] = g
            d_out[at] = delta
            m_out[at] = m_new
            v_out[at] = v_new

    vm = pl.BlockSpec(memory_space=pltpu.VMEM)
    flat_w = [a for wmv in weights for a in wmv]
    out_shape = [jax.ShapeDtypeStruct((1, 128), F32)]
    for w, _, _ in weights:
        out_shape += [jax.ShapeDtypeStruct(w.shape, F32)] * 4
    return pl.pallas_call(
        body, name="small_update", in_specs=[vm] * (n_p + 1 + 3 * n_w), out_specs=[vm] * len(out_shape),
        out_shape=out_shape, compiler_params=pltpu.CompilerParams(vmem_limit_bytes=VMEM_LIMIT),
    )(*partials, loss_parts, *flat_w)


def _pack_small_shard(cw, lg, lb):
    return (jnp.pad(cw, ((0, 13), (0, 64))) + jnp.pad(lg, ((8, 7), (0, 0))) + jnp.pad(lb, ((9, 6), (0, 0))))


TS_PRENORM = 1024
TS_FWD = 512
TS_BWD = 256
TS_WGRAD = 2048


def kernel(x, pre_norm, post_norm, even_w_in, even_conv_w, even_pool_w, even_pool_scale, even_w_out, odd_w_in, odd_ln_g, odd_ln_b, odd_w_s, odd_b_s, odd_w_out, loss_target, m_pre_norm, m_post_norm, m_even_w_in, m_even_conv_w, m_even_pool_w, m_even_pool_scale, m_even_w_out, m_odd_w_in, m_odd_ln_g, m_odd_ln_b, m_odd_w_s, m_odd_b_s, m_odd_w_out, v_pre_norm, v_post_norm, v_even_w_in, v_even_conv_w, v_even_pool_w, v_even_pool_scale, v_even_w_out, v_odd_w_in, v_odd_ln_g, v_odd_ln_b, v_odd_w_s, v_odd_b_s, v_odd_w_out):
    S = x.shape[1]
    xs = x.reshape(S, D)
    tgt = loss_target.reshape(S, D)

    gpre0, gpre1 = pre_norm[0:1], pre_norm[1:2]
    gpost0, gpost1 = post_norm[0:1], post_norm[1:2]
    small_shard = _pack_small_shard(even_conv_w[0], odd_ln_g, odd_ln_b)
    h0, wie_g, small_g, woe_g, wio_b, woo_b = _prenorm_gather(xs, gpre0, even_w_in[0], small_shard, even_w_out[0],
                                                              odd_w_in[0], odd_w_out[0], TS_PRENORM)
    wie = jnp.transpose(wie_g, (1, 0, 2)).reshape(D, PROJ)
    wie_t = jnp.transpose(wie_g, (0, 2, 1)).reshape(PROJ, D)
    woe = woe_g.reshape(D, D)
    woe_t = woe.T
    conv_w = jnp.transpose(small_g[:, 0:3, 0:64], (1, 0, 2)).reshape(3, A_W)
    ln_g = small_g[:, 8, :].reshape(1, D)
    ln_b = small_g[:, 9, :].reshape(1, D)

    pool_w_b = even_pool_w[0].astype(BF16)
    pool_wt_b = jnp.swapaxes(even_pool_w[0], 1, 2).astype(BF16)
    ws_tril = jnp.tril(odd_w_s[0])
    ws_b = ws_tril.astype(BF16)
    wst_b = jnp.swapaxes(ws_tril, 1, 2).astype(BF16)
    b_full = jnp.broadcast_to(odd_b_s[0][:, :, None], (N_HEADS, GROUP, GROUP))

    p0, m0, x1, h1, wio_g, woo_g = _even_forward_fused(h0, xs, wie, conv_w, pool_w_b, even_pool_scale, woe, gpost0,
                                                       gpre1, TS_FWD, sides=[(_Gather, [wio_b, woo_b])])
    wio = jnp.transpose(wio_g, (1, 0, 2)).reshape(D, PROJ)
    wio_t = jnp.transpose(wio_g, (0, 2, 1)).reshape(PROJ, D)
    woo = woo_g.reshape(D, D)
    woo_t = woo.T
    p1, m1, g2, loss_vec = _odd_forward(h1, x1, tgt, wio, ln_g, ln_b, ws_b, b_full, woo, gpost1, TS_FWD)

    dp1, g1, dwoo, dws, dbs, dlg, dlb, dgpost1, dgpre1 = _odd_backward(
        g2, m1, p1, x1, ln_g, ln_b, ws_b, wst_b, b_full, woo_t, wio_t, gpost1, gpre1, TS_BWD)
    by_owner = lambda dwo: dwo.reshape(N_DEV, SHARD_OUT, D)
    (dwio_parts, dwoo_parts, dlg_parts, dlb_parts,
     dws_g, dbs_g, gpost1_g, gpre1_g, loss_g) = _weight_grad_scatter(
        h1, dp1, TS_WGRAD, "odd_weight_grad",
        sides=[(_Scatter, [by_owner(dwoo), dlg.reshape(N_DEV, 1, 128), dlb.reshape(N_DEV, 1, 128)]),
               (_Gather, [dws, dbs, dgpost1, dgpre1, loss_vec])])
    dp0, gx, dwoe, dcw, dpw, dps, dgpost0, dgpre0 = _even_backward_fused(
        g1, m0, p0, xs, conv_w, pool_w_b, pool_wt_b, even_pool_scale, woe_t, wie_t, gpost0, gpre0, TS_FWD)
    dcw_by_owner = jnp.transpose(dcw[0:3].reshape(3, N_DEV, 64), (1, 0, 2))
    dwie_parts, dwoe_parts, dcw_parts, dpw_g, dps_g, gpost0_g, gpre0_g = _weight_grad_scatter(
        h0, dp0, TS_WGRAD, "even_weight_grad",
        sides=[(_Scatter, [by_owner(dwoe), dcw_by_owner]), (_Gather, [dpw, dps, dgpost0, dgpre0])])

    g_wie, d_wie, nm_wie, nv_wie = _sum_update(dwie_parts, even_w_in[0], m_even_w_in[0], v_even_w_in[0],
                                               "update_even_w_in")
    g_wio, d_wio, nm_wio, nv_wio = _sum_update(dwio_parts, odd_w_in[0], m_odd_w_in[0], v_odd_w_in[0],
                                               "update_odd_w_in")
    g_woe, d_woe, nm_woe, nv_woe = _sum_update(dwoe_parts, even_w_out[0], m_even_w_out[0], v_even_w_out[0],
                                               "update_even_w_out")
    g_woo, d_woo, nm_woo, nv_woo = _sum_update(dwoo_parts, odd_w_out[0], m_odd_w_out[0], v_odd_w_out[0],
                                               "update_odd_w_out")

    partials = [gpre0_g, gpre1_g, gpost0_g, gpost1_g, dpw_g, dps_g, dws_g, dbs_g, dcw_parts, dlg_parts, dlb_parts]
    row_of = [(0, 0), (0, 1), (1, 0), (1, 1), (2, 0), (3, 0), (4, 0), (5, 0), (6, 0), (7, 0), (8, 0)]
    weights = [(pre_norm, m_pre_norm, v_pre_norm), (post_norm, m_post_norm, v_post_norm),
               (even_pool_w[0], m_even_pool_w[0], v_even_pool_w[0]),
               (even_pool_scale, m_even_pool_scale, v_even_pool_scale),
               (odd_w_s[0], m_odd_w_s[0], v_odd_w_s[0]), (odd_b_s[0], m_odd_b_s[0], v_odd_b_s[0]),
               (even_conv_w[0], m_even_conv_w[0], v_even_conv_w[0]),
               (odd_ln_g, m_odd_ln_g, v_odd_ln_g), (odd_ln_b, m_odd_ln_b, v_odd_ln_b)]
    small = _small_sum_update(partials, row_of, weights, loss_g)
    loss = small[0][0, 0]

    def leaves(k, big):
        pre, post, pw, psc, ws, bs, cw, lg, lb = [small[1 + 4 * i + k] for i in range(len(weights))]
        wie_k, woe_k, wio_k, woo_k = big
        return [pre, post, wie_k[None], cw[None], pw[None], psc, woe_k[None], wio_k[None], lg, lb, ws[None], bs[None],
                woo_k[None]]

    outs = [loss, gx.reshape(1, S, D)]
    outs += leaves(0, (g_wie, g_woe, g_wio, g_woo))
    outs += leaves(1, (d_wie, d_woe, d_wio, d_woo))
    outs += leaves(2, (nm_wie, nm_woe, nm_wio, nm_woo))
    outs += leaves(3, (nv_wie, nv_woe, nv_wio, nv_woo))
    return tuple(outs)
```

```python
import functools

import jax
import jax.numpy as jnp
from jax import lax
from jax.experimental import pallas as pl
from jax.experimental.pallas import tpu as pltpu

F32 = jnp.float32
BF16 = jnp.bfloat16
MESH = pl.DeviceIdType.MESH

D = 1024
EPS = 1e-6
A_W = 512
B_W = 512
POOL_WINDOWS = (2, 4, 8, 16)
GROUP = 128
N_GROUPS = 4
N_HEADS = 8
HALO = 16
PROJ = 3072
N_DEV = 8
SHARD_IN = PROJ // N_DEV
SHARD_OUT = D // N_DEV

ADAM_LR = 0.001
ADAM_B1 = 0.9
ADAM_B2 = 0.999
ADAM_EPS = 1e-08
ADAM_WD = 0.01
ADAM_STEP = 10

VMEM_LIMIT = 54 * 1024 * 1024


def _params(n_grid=1, vmem=VMEM_LIMIT):
    return pltpu.CompilerParams(dimension_semantics=("arbitrary",) * n_grid, vmem_limit_bytes=vmem)


def _full(shape):
    return pl.BlockSpec(shape, lambda *_: (0,) * len(shape))


def _sigmoid(z):
    return jax.nn.sigmoid(z)


def _dsilu(z, s):
    return s * (1.0 + z * (1.0 - s))


def _dot(a, b):
    return jnp.dot(a, b, preferred_element_type=F32)


def _dot_t0(a, b):
    return lax.dot_general(a, b, (((0,), (0,)), ((), ())), preferred_element_type=F32)


def _dot_t1(a, b):
    return lax.dot_general(a, b, (((1,), (1,)), ((), ())), preferred_element_type=F32)


def _rms(x):
    return lax.rsqrt(jnp.mean(x * x, axis=-1, keepdims=True) + EPS)


PIECE = 16


def _fold(a):
    return a[0:8] + a[8:16]


def _shift_down(a, k):
    return pltpu.roll(a, k, 0)


def _shift_up(a, k):
    return pltpu.roll(a, a.shape[0] - k, 0)


def _mesh_position():
    return lax.axis_index("x"), lax.axis_index("y"), lax.axis_index("c")


def _index(pos):
    return 4 * pos[0] + 2 * pos[1] + pos[2]


def _peer(pos, d):
    x, y, c = pos
    return (1 - x if d & 4 else x, 1 - y if d & 2 else y, 1 - c if d & 1 else c)


def _remote(src, dst, send_sems, recv_sems, k, to):
    return pltpu.make_async_remote_copy(src_ref=src, dst_ref=dst, send_sem=send_sems.at[k], recv_sem=recv_sems.at[k],
                                        device_id=to, device_id_type=MESH)


class _Gather:
    def __init__(self, srcs, dsts, send_sems, recv_sems):
        x, y, c = _mesh_position()
        n = len(srcs)
        me, sibling = (x, y, c), (x, y, 1 - c)
        chips = [(1 - x, y), (x, 1 - y), (1 - x, 1 - y)]

        def copy(k, t, block, to, src=None):
            slot = dsts[t].at[_index(block)]
            return _remote(slot if src is None else src, slot, send_sems, recv_sems, k * n + t, to)

        tensors = range(n)
        self.local = [pltpu.make_async_copy(srcs[t], dsts[t].at[_index(me)], recv_sems.at[7 * n + t]) for t in tensors]
        self.first = [copy(1 + j, t, me, chip + (c,), srcs[t]) for t in tensors for j, chip in enumerate(chips)]
        self.first += [copy(0, t, me, sibling, srcs[t]) for t in tensors]
        self.ici_in = [copy(1 + j, t, chip + (c,), me) for t in tensors for j, chip in enumerate(chips)]
        self.passed = [copy(4 + j, t, chip + (c,), sibling) for t in tensors for j, chip in enumerate(chips)]
        self.d2d_in = [copy(0, t, sibling, me) for t in tensors]
        self.d2d_in += [copy(4 + j, t, chip + (1 - c,), me) for t in tensors for j, chip in enumerate(chips)]

    def start(self):
        for cp in self.local + self.first:
            cp.start()

    def middle(self):
        for landed, onward in zip(self.ici_in, self.passed):
            landed.wait_recv()
            onward.start()

    def finish(self):
        for cp in self.d2d_in:
            cp.wait_recv()
        for cp in self.first + self.passed:
            cp.wait_send()
        for cp in self.local:
            cp.wait()


class _Scatter:
    def __init__(self, srcs, dsts, send_sems, recv_sems):
        pos = _mesh_position()
        n = len(srcs)
        self.local = [pltpu.make_async_copy(srcs[t].at[_index(pos)], dsts[t].at[0], recv_sems.at[t]) for t in range(n)]
        self.remote = []
        for d in range(1, N_DEV):
            to = _peer(pos, d)
            self.remote += [_remote(srcs[t].at[_index(to)], dsts[t].at[d], send_sems, recv_sems, d * n + t, to)
                            for t in range(n)]

    def start(self):
        for cp in self.local + self.remote:
            cp.start()

    def middle(self):
        pass

    def finish(self):
        for cp in self.local:
            cp.wait()
        for cp in self.remote:
            cp.wait_recv()
        for cp in self.remote:
            cp.wait_send()


HBM_SPEC = pl.BlockSpec(memory_space=pltpu.HBM)
MIDDLE_STEPS_BEFORE_END = 4


def _call(body, *, name, grid, in_specs, out_specs, out_shape, scratch_shapes, args, sides=()):
    params = _params(len(grid))
    if not sides:
        return pl.pallas_call(body, name=name, grid=grid, in_specs=in_specs, out_specs=out_specs, out_shape=out_shape,
                              scratch_shapes=scratch_shapes, compiler_params=params)(*args)
    n_in, n_out, n_scratch = len(in_specs), len(out_specs), len(scratch_shapes)
    counts = [len(srcs) for _, srcs in sides]
    ns = sum(counts)
    total = 1
    for g in grid:
        total *= g
    side_args, side_shapes, side_sems = [], [], []
    for exchange, srcs in sides:
        side_args += list(srcs)
        lead = (N_DEV,) if exchange is _Gather else ()
        side_shapes += [jax.ShapeDtypeStruct(lead + s.shape, s.dtype) for s in srcs]
        side_sems += [pltpu.SemaphoreType.DMA((N_DEV * len(srcs),))] * 2

    def wrapped(*refs):
        ins, side_in = refs[:n_in], refs[n_in:n_in + ns]
        outs = refs[n_in + ns:n_in + ns + n_out]
        side_out = refs[n_in + ns + n_out:n_in + 2 * ns + n_out]
        rest = refs[n_in + 2 * ns + n_out:]
        scratch, sems = rest[:n_scratch], rest[n_scratch:]
        step = pl.program_id(0)
        for axis in range(1, len(grid)):
            step = step * grid[axis] + pl.program_id(axis)

        def exchanges():
            built, at = [], 0
            for k, (exchange, _) in enumerate(sides):
                built.append(exchange(side_in[at:at + counts[k]], side_out[at:at + counts[k]],
                                      sems[2 * k], sems[2 * k + 1]))
                at += counts[k]
            return built

        @pl.when(step == 0)
        def _():
            for ex in exchanges():
                ex.start()

        @pl.when(step == max(total - MIDDLE_STEPS_BEFORE_END, 0))
        def _():
            for ex in exchanges():
                ex.middle()

        body(*ins, *outs, *scratch)

        @pl.when(step == total - 1)
        def _():
            for ex in exchanges():
                ex.finish()

    return pl.pallas_call(
        wrapped, name=name, grid=grid, in_specs=list(in_specs) + [HBM_SPEC] * ns,
        out_specs=list(out_specs) + [HBM_SPEC] * ns, out_shape=list(out_shape) + side_shapes,
        scratch_shapes=list(scratch_shapes) + side_sems, compiler_params=params)(*args, *side_args)


def _prenorm_gather(x, g, wie, small, woe, wio, woo, ts):
    S = x.shape[0]
    nt = S // ts

    def body(x_ref, g_ref, wie_ref, sm_ref, woe_ref, wio_ref, woo_ref,
             h_ref, wie_out, sm_out, woe_out, wio_b, woo_b, cast_ref, small_ref, cast2_ref, send_sems, recv_sems):
        step = pl.program_id(0)
        gather = lambda: _Gather([cast_ref, small_ref, cast2_ref], [wie_out, sm_out, woe_out], send_sems, recv_sems)

        @pl.when(step == 0)
        def _():
            cast_ref[...] = wie_ref[...].astype(BF16)
            small_ref[...] = sm_ref[...]
            cast2_ref[...] = woe_ref[...].astype(BF16)
            gather().start()
            wio_b[...] = wio_ref[...].astype(BF16)
            woo_b[...] = woo_ref[...].astype(BF16)

        xv = x_ref[...]
        h_ref[...] = ((xv * _rms(xv)) * g_ref[...]).astype(BF16)

        @pl.when(step == nt - 1)
        def _():
            exchange = gather()
            exchange.middle()
            exchange.finish()

    tile = pl.BlockSpec((ts, D), lambda i: (i, 0))
    sems = pltpu.SemaphoreType.DMA((N_DEV * 3,))
    shards = [wie, small, woe, wio, woo]
    return pl.pallas_call(
        body, name="prenorm_gather", grid=(nt,),
        in_specs=[tile, _full((1, D))] + [_full(a.shape) for a in shards],
        out_specs=[tile, HBM_SPEC, HBM_SPEC, HBM_SPEC] + [_full(a.shape) for a in shards[3:]],
        out_shape=[jax.ShapeDtypeStruct((S, D), BF16), jax.ShapeDtypeStruct((N_DEV,) + wie.shape, BF16),
                   jax.ShapeDtypeStruct((N_DEV,) + small.shape, F32), jax.ShapeDtypeStruct((N_DEV,) + woe.shape, BF16)]
        + [jax.ShapeDtypeStruct(a.shape, BF16) for a in shards[3:]],
        scratch_shapes=[pltpu.VMEM(wie.shape, BF16), pltpu.VMEM(small.shape, F32), pltpu.VMEM(woe.shape, BF16),
                        sems, sems],
        compiler_params=_params(),
    )(x, g, wie, small, woe, wio, woo)


def _matmul(h, w, ts, name, sides=()):
    S = h.shape[0]
    N = w.shape[1]

    def body(h_ref, w_ref, p_ref):
        p_ref[...] = _dot(h_ref[...], w_ref[...]).astype(BF16)

    return _call(
        body, name=name, grid=(S // ts,),
        in_specs=[pl.BlockSpec((ts, D), lambda i: (i, 0)), _full((D, N))],
        out_specs=[pl.BlockSpec((ts, N), lambda i: (i, 0))],
        out_shape=[jax.ShapeDtypeStruct((S, N), BF16)],
        scratch_shapes=[], args=(h, w), sides=sides)


def _even_mixer(proj, halo, row0, cw_ref, pw_ref):
    xa, gb, gc, za, xp, zp = proj
    nr = xa.shape[0]
    ha = gc * xa
    ha_ext = jnp.concatenate([halo[0], ha], axis=0)
    ha_m1 = _shift_down(ha_ext, 1)[HALO:]
    ha_m2 = _shift_down(ha_ext, 2)[HALO:]
    conv = cw_ref[2:3, :] * ha + cw_ref[1:2, :] * ha_m1 + cw_ref[0:1, :] * ha_m2
    sig_a = _sigmoid(za)
    silu_a = za * sig_a

    xp_ext = jnp.concatenate([halo[1], xp], axis=0)
    pos = row0 + lax.broadcasted_iota(jnp.int32, (nr, 1), 0)
    pooled, inv_cnt, mixed = [], [], []
    for g, w in enumerate(POOL_WINDOWS):
        cols = slice(g * GROUP, (g + 1) * GROUP)
        s = xp_ext[:, cols]
        k = 1
        while k < w:
            s = s + _shift_down(s, k)
            k *= 2
        inv = 1.0 / jnp.minimum(pos + 1, w).astype(F32)
        pg = s[HALO:] * inv - xp[:, cols]
        pooled.append(pg)
        inv_cnt.append(inv)
        mixed.append(_dot(pg.astype(BF16), pw_ref[g]))
    mixed = jnp.concatenate(mixed, axis=1)
    sig_b = _sigmoid(zp)
    silu_b = zp * sig_b
    return dict(xa=xa, gb=gb, gc=gc, za=za, xp=xp, zp=zp, ha=ha, ha_m1=ha_m1, ha_m2=ha_m2, conv=conv,
                sig_a=sig_a, silu_a=silu_a, pooled=pooled, inv_cnt=inv_cnt, mixed=mixed, sig_b=sig_b,
                silu_b=silu_b)


def _columns(a):
    return [a[:, k * A_W:(k + 1) * A_W].astype(F32) for k in range(6)]


def _even_forward_fused(h, x, win_b, cw, pw_b, ps, wout_b, gpost, gpre_next, ts, sides=()):
    S = x.shape[0]

    def body(h_ref, x_ref, wi_ref, cw_ref, pw_ref, ps_ref, wo_ref, gp_ref, gn_ref,
             p_ref, m_ref, x1_ref, h1_ref, mix_ref, carry_ref):
        i = pl.program_id(0)

        @pl.when(i == 0)
        def _():
            carry_ref[...] = jnp.zeros_like(carry_ref)

        for r0 in range(0, ts, SUB_ROWS):
            rows = slice(r0, r0 + SUB_ROWS)
            proj = _dot(h_ref[rows, :], wi_ref[...])
            p_ref[rows, :] = proj.astype(BF16)
            fw = _even_mixer(_columns(proj), (carry_ref[:, 0:A_W], carry_ref[:, A_W:D]), i * ts + r0, cw_ref, pw_ref)
            carry_ref[:, 0:A_W] = fw["ha"][SUB_ROWS - HALO:]
            carry_ref[:, A_W:D] = fw["xp"][SUB_ROWS - HALO:]
            mix_ref[rows, 0:A_W] = (fw["gb"] * fw["conv"] * fw["silu_a"]).astype(BF16)
            mix_ref[rows, A_W:D] = (fw["mixed"] * ps_ref[...] * fw["silu_b"]).astype(BF16)
            m = _dot(mix_ref[rows, :], wo_ref[...])
            m_ref[rows, :] = m.astype(BF16)
            x1 = x_ref[rows, :] + (m * _rms(m)) * gp_ref[...]
            x1_ref[rows, :] = x1
            h1_ref[rows, :] = ((x1 * _rms(x1)) * gn_ref[...]).astype(BF16)

    tile = pl.BlockSpec((ts, D), lambda i: (i, 0))
    return _call(
        body, name="even_forward", grid=(S // ts,),
        in_specs=[tile, tile, _resident((D, PROJ)), _full((3, A_W)), _full((N_GROUPS, GROUP, GROUP)), _full((1, B_W)),
                  _resident((D, D)), _full((1, D)), _full((1, D))],
        out_specs=[pl.BlockSpec((ts, PROJ), lambda i: (i, 0)), tile, tile, tile],
        out_shape=[jax.ShapeDtypeStruct((S, PROJ), BF16), jax.ShapeDtypeStruct((S, D), BF16),
                   jax.ShapeDtypeStruct((S, D), F32), jax.ShapeDtypeStruct((S, D), BF16)],
        scratch_shapes=[pltpu.VMEM((ts, D), BF16), pltpu.VMEM((HALO, D), F32)],
        args=(h, x, win_b, cw, pw_b, ps, wout_b, gpost, gpre_next), sides=sides)


def _even_backward_fused(g1, m0, p, x, cw, pw_b, pwt_b, ps, woutt_b, wint_b, gpost, gpre, ts):
    S = g1.shape[0]
    nt = S // ts

    def body(g_ref, m_ref, p_ref, halo_ref, x_ref, cw_ref, pw_ref, pwt_ref, ps_ref, wot_ref, wit_ref, gp_ref, gn_ref,
             dp_ref, dx_ref, mix_ref, dm_ref, dcw_ref, dpw_ref, dps_ref, dgp_ref, dgn_ref, carry_ref):
        step = pl.program_id(0)
        i = nt - 1 - step

        @pl.when(step == 0)
        def _():
            carry_ref[...] = jnp.zeros_like(carry_ref)
            dcw_ref[...] = jnp.zeros_like(dcw_ref)
            dpw_ref[...] = jnp.zeros_like(dpw_ref)
            dps_ref[...] = jnp.zeros_like(dps_ref)
            dgp_ref[...] = jnp.zeros_like(dgp_ref)
            dgn_ref[...] = jnp.zeros_like(dgn_ref)

        ps_v = ps_ref[...]
        sums = [None] * 6
        add = lambda k, part: sums.__setitem__(k, part if sums[k] is None else sums[k] + part)
        for r0 in range(ts - SUB_ROWS, -1, -SUB_ROWS):
            rows = slice(r0, r0 + SUB_ROWS)
            g = g_ref[rows, :]
            m = m_ref[rows, :].astype(F32)
            q = _rms(m)
            n = m * q
            add(0, jnp.sum(g * n, axis=0, keepdims=True))
            dn = g * gp_ref[...]
            dm = q * (dn - n * jnp.mean(dn * n, axis=-1, keepdims=True))
            dm_ref[rows, :] = dm.astype(BF16)
            dmix = _dot(dm_ref[rows, :], wot_ref[...])
            dya = dmix[:, 0:A_W]
            dyb = dmix[:, A_W:D]

            if r0 == 0:
                before = _columns(halo_ref[...])
                keep = jnp.where(i == 0, 0.0, 1.0).astype(F32)
                halo = (before[2] * before[0] * keep, before[4] * keep)
            else:
                before = _columns(p_ref[r0 - HALO:r0, :])
                halo = (before[2] * before[0], before[4])
            fw = _even_mixer(_columns(p_ref[rows, :]), halo, i * ts + r0, cw_ref, pw_ref)
            mix_ref[rows, 0:A_W] = (fw["gb"] * fw["conv"] * fw["silu_a"]).astype(BF16)
            mix_ref[rows, A_W:D] = (fw["mixed"] * ps_v * fw["silu_b"]).astype(BF16)

            t = dya * fw["gb"]
            dconv = t * fw["silu_a"]
            dgb = dya * fw["conv"] * fw["silu_a"]
            dza = t * fw["conv"] * _dsilu(fw["za"], fw["sig_a"])
            add(2, jnp.sum(dconv * fw["ha"], axis=0, keepdims=True))
            add(3, jnp.sum(dconv * fw["ha_m1"], axis=0, keepdims=True))
            add(4, jnp.sum(dconv * fw["ha_m2"], axis=0, keepdims=True))
            dconv_ext = jnp.concatenate([dconv, carry_ref[:, 0:A_W]], axis=0)
            dha = (cw_ref[2:3, :] * dconv + cw_ref[1:2, :] * _shift_up(dconv_ext, 1)[:SUB_ROWS]
                   + cw_ref[0:1, :] * _shift_up(dconv_ext, 2)[:SUB_ROWS])
            dgc = dha * fw["xa"]
            dxa = dha * fw["gc"]

            u = dyb * fw["mixed"]
            add(1, jnp.sum(u * fw["silu_b"], axis=0, keepdims=True))
            dzp = u * ps_v * _dsilu(fw["zp"], fw["sig_b"])
            dmixed = (dyb * ps_v * fw["silu_b"]).astype(BF16)
            dxp, e_first = [], []
            for gi, w in enumerate(POOL_WINDOWS):
                cols = slice(gi * GROUP, (gi + 1) * GROUP)
                dmg = dmixed[:, cols]
                dpooled = _dot(dmg, pwt_ref[gi])
                dpw_ref[gi] += _dot_t0(fw["pooled"][gi].astype(BF16), dmg)
                e = dpooled * fw["inv_cnt"][gi]
                e_first.append(e[0:HALO])
                s = jnp.concatenate([e, carry_ref[:, A_W + gi * GROUP:A_W + (gi + 1) * GROUP]], axis=0)
                k = 1
                while k < w:
                    s = s + _shift_up(s, k)
                    k *= 2
                dxp.append(s[:SUB_ROWS] - dpooled)
            carry_ref[:, 0:A_W] = dconv[0:HALO]
            carry_ref[:, A_W:D] = jnp.concatenate(e_first, axis=1)

            dp_ref[rows, 0:512] = dxa.astype(BF16)
            dp_ref[rows, 512:1024] = dgb.astype(BF16)
            dp_ref[rows, 1024:1536] = dgc.astype(BF16)
            dp_ref[rows, 1536:2048] = dza.astype(BF16)
            dp_ref[rows, 2048:2560] = jnp.concatenate(dxp, axis=1).astype(BF16)
            dp_ref[rows, 2560:3072] = dzp.astype(BF16)

            dh = _dot(dp_ref[rows, :], wit_ref[...])
            xv = x_ref[rows, :]
            r = _rms(xv)
            xn = xv * r
            add(5, jnp.sum(dh * xn, axis=0, keepdims=True))
            dxn = dh * gn_ref[...]
            dx_ref[rows, :] = g + r * (dxn - xn * jnp.mean(dxn * xn, axis=-1, keepdims=True))

        dgp_ref[...] += sums[0]
        dps_ref[...] += sums[1]
        dcw_ref[2:3, :] += sums[2]
        dcw_ref[1:2, :] += sums[3]
        dcw_ref[0:1, :] += sums[4]
        dgn_ref[...] += sums[5]

    rev = lambda s: (nt - 1 - s, 0)
    tile = pl.BlockSpec((ts, D), rev)
    wide = pl.BlockSpec((ts, PROJ), rev)
    bpt = ts // HALO
    halo_map = lambda s: (jnp.maximum((nt - 1 - s) * bpt - 1, 0), 0)
    groups = _full((N_GROUPS, GROUP, GROUP))
    vec = _full((1, D))
    return pl.pallas_call(
        body, name="even_backward", grid=(nt,),
        in_specs=[tile, tile, wide, pl.BlockSpec((HALO, PROJ), halo_map), tile, _full((3, A_W)), groups, groups,
                  _full((1, B_W)), _resident((D, D)), _resident((PROJ, D)), vec, vec],
        out_specs=[wide, tile, tile, tile, _full((8, A_W)), groups, _full((1, B_W)), vec, vec],
        out_shape=[jax.ShapeDtypeStruct((S, PROJ), BF16), jax.ShapeDtypeStruct((S, D), F32),
                   jax.ShapeDtypeStruct((S, D), BF16), jax.ShapeDtypeStruct((S, D), BF16),
                   jax.ShapeDtypeStruct((8, A_W), F32),
                   jax.ShapeDtypeStruct((N_GROUPS, GROUP, GROUP), F32), jax.ShapeDtypeStruct((1, B_W), F32),
                   jax.ShapeDtypeStruct((1, D), F32), jax.ShapeDtypeStruct((1, D), F32)],
        scratch_shapes=[pltpu.VMEM((HALO, D), F32)],
        compiler_params=_params(),
    )(g1, m0, p, p, x, cw, pw_b, pwt_b, ps, woutt_b, wint_b, gpost, gpre)


def _even_forward_parts(p_ref, halo_ref, is_first, row0, cw_ref, pw_ref, ts):
    f = lambda a: a.astype(F32)
    xa = f(p_ref[:, 0:512])
    gb = f(p_ref[:, 512:1024])
    gc = f(p_ref[:, 1024:1536])
    za = f(p_ref[:, 1536:2048])
    xp = f(p_ref[:, 2048:2560])
    zp = f(p_ref[:, 2560:3072])
    keep = jnp.where(is_first, 0.0, 1.0).astype(F32)
    ha = gc * xa
    ha_halo = f(halo_ref[:, 1024:1536]) * f(halo_ref[:, 0:512]) * keep
    xp_halo = f(halo_ref[:, 2048:2560]) * keep
    ha_ext = jnp.concatenate([ha_halo, ha], axis=0)
    ha_m1 = _shift_down(ha_ext, 1)[HALO:]
    ha_m2 = _shift_down(ha_ext, 2)[HALO:]
    conv = cw_ref[2:3, :] * ha + cw_ref[1:2, :] * ha_m1 + cw_ref[0:1, :] * ha_m2
    sig_a = _sigmoid(za)
    silu_a = za * sig_a

    xp_ext = jnp.concatenate([xp_halo, xp], axis=0)
    pos = row0 + lax.broadcasted_iota(jnp.int32, (ts, 1), 0)
    pooled, inv_cnt, mixed = [], [], []
    for g, w in enumerate(POOL_WINDOWS):
        cols = slice(g * GROUP, (g + 1) * GROUP)
        s = xp_ext[:, cols]
        k = 1
        while k < w:
            s = s + _shift_down(s, k)
            k *= 2
        inv = 1.0 / jnp.minimum(pos + 1, w).astype(F32)
        pg = s[HALO:] * inv - xp[:, cols]
        pooled.append(pg)
        inv_cnt.append(inv)
        mixed.append(_dot(pg.astype(BF16), pw_ref[g]))
    mixed = jnp.concatenate(mixed, axis=1)
    sig_b = _sigmoid(zp)
    silu_b = zp * sig_b
    return dict(xa=xa, gb=gb, gc=gc, za=za, xp=xp, zp=zp, ha=ha, ha_m1=ha_m1, ha_m2=ha_m2, conv=conv,
                sig_a=sig_a, silu_a=silu_a, pooled=pooled, inv_cnt=inv_cnt, mixed=mixed, sig_b=sig_b,
                silu_b=silu_b)


def _halo_index(ts):
    blocks_per_tile = ts // HALO
    return lambda i: (jnp.maximum(i * blocks_per_tile - 1, 0), 0)


def _even_forward(p, x, cw, pw_b, ps, wout_b, gpost, gpre_next, ts, sides=()):
    S = x.shape[0]

    def body(p_ref, halo_ref, x_ref, cw_ref, pw_ref, ps_ref, wo_ref, gp_ref, gn_ref, m_ref, x1_ref, h1_ref, mix_ref):
        i = pl.program_id(0)
        fw = _even_forward_parts(p_ref, halo_ref, i == 0, i * ts, cw_ref, pw_ref, ts)
        mix_ref[:, 0:A_W] = (fw["gb"] * fw["conv"] * fw["silu_a"]).astype(BF16)
        mix_ref[:, A_W:D] = (fw["mixed"] * ps_ref[...] * fw["silu_b"]).astype(BF16)
        m = _dot(mix_ref[...], wo_ref[...])
        m_ref[...] = m
        x1 = x_ref[...] + (m * _rms(m)) * gp_ref[...]
        x1_ref[...] = x1
        h1_ref[...] = ((x1 * _rms(x1)) * gn_ref[...]).astype(BF16)

    tile = pl.BlockSpec((ts, D), lambda i: (i, 0))
    return _call(
        body, name="even_forward", grid=(S // ts,),
        in_specs=[pl.BlockSpec((ts, PROJ), lambda i: (i, 0)), pl.BlockSpec((HALO, PROJ), _halo_index(ts)), tile,
                  _full((3, A_W)), _full((N_GROUPS, GROUP, GROUP)), _full((1, B_W)), _full((D, D)), _full((1, D)),
                  _full((1, D))],
        out_specs=[tile, tile, tile],
        out_shape=[jax.ShapeDtypeStruct((S, D), F32), jax.ShapeDtypeStruct((S, D), F32),
                   jax.ShapeDtypeStruct((S, D), BF16)],
        scratch_shapes=[pltpu.VMEM((ts, D), BF16)],
        args=(p, p, x, cw, pw_b, ps, wout_b, gpost, gpre_next), sides=sides)


def _even_backward(g1, m0, p, cw, pw_b, pwt_b, ps, woutt_b, gpost, ts, sides=()):
    S = g1.shape[0]
    nt = S // ts

    def body(g_ref, m_ref, p_ref, halo_ref, cw_ref, pw_ref, pwt_ref, ps_ref, wot_ref, gp_ref,
             dp_ref, dwob_ref, dcw_ref, dpw_ref, dps_ref, dgp_ref, carry_ref, mix_ref, dm_ref, dwo_ref):
        step = pl.program_id(0)
        i = nt - 1 - step

        @pl.when(step == 0)
        def _():
            carry_ref[...] = jnp.zeros_like(carry_ref)
            dwo_ref[...] = jnp.zeros_like(dwo_ref)
            dcw_ref[...] = jnp.zeros_like(dcw_ref)
            dpw_ref[...] = jnp.zeros_like(dpw_ref)
            dps_ref[...] = jnp.zeros_like(dps_ref)
            dgp_ref[...] = jnp.zeros_like(dgp_ref)

        g = g_ref[...]
        m = m_ref[...]
        q = _rms(m)
        n = m * q
        dgp_ref[...] += jnp.sum(g * n, axis=0, keepdims=True)
        dn = g * gp_ref[...]
        dm = q * (dn - n * jnp.mean(dn * n, axis=-1, keepdims=True))
        dm_ref[...] = dm.astype(BF16)
        dmix = _dot(dm_ref[...], wot_ref[...])
        dya = dmix[:, 0:A_W]
        dyb = dmix[:, A_W:D]

        fw = _even_forward_parts(p_ref, halo_ref, i == 0, i * ts, cw_ref, pw_ref, ts)
        ps_v = ps_ref[...]
        mix_ref[:, 0:A_W] = (fw["gb"] * fw["conv"] * fw["silu_a"]).astype(BF16)
        mix_ref[:, A_W:D] = (fw["mixed"] * ps_v * fw["silu_b"]).astype(BF16)
        dwo_ref[...] += _dot_t0(mix_ref[...], dm_ref[...])

        t = dya * fw["gb"]
        dconv = t * fw["silu_a"]
        dgb = dya * fw["conv"] * fw["silu_a"]
        dza = t * fw["conv"] * _dsilu(fw["za"], fw["sig_a"])
        dcw_ref[2:3, :] += jnp.sum(dconv * fw["ha"], axis=0, keepdims=True)
        dcw_ref[1:2, :] += jnp.sum(dconv * fw["ha_m1"], axis=0, keepdims=True)
        dcw_ref[0:1, :] += jnp.sum(dconv * fw["ha_m2"], axis=0, keepdims=True)
        dconv_ext = jnp.concatenate([dconv, carry_ref[:, 0:A_W]], axis=0)
        dha = (cw_ref[2:3, :] * dconv + cw_ref[1:2, :] * _shift_up(dconv_ext, 1)[:ts]
               + cw_ref[0:1, :] * _shift_up(dconv_ext, 2)[:ts])
        dgc = dha * fw["xa"]
        dxa = dha * fw["gc"]

        u = dyb * fw["mixed"]
        dps_ref[...] += jnp.sum(u * fw["silu_b"], axis=0, keepdims=True)
        dzp = u * ps_v * _dsilu(fw["zp"], fw["sig_b"])
        dmixed = (dyb * ps_v * fw["silu_b"]).astype(BF16)
        dxp, e_first = [], []
        for gi, w in enumerate(POOL_WINDOWS):
            cols = slice(gi * GROUP, (gi + 1) * GROUP)
            dmg = dmixed[:, cols]
            dpooled = _dot(dmg, pwt_ref[gi])
            dpw_ref[gi] += _dot_t0(fw["pooled"][gi].astype(BF16), dmg)
            e = dpooled * fw["inv_cnt"][gi]
            e_first.append(e[0:HALO])
            s = jnp.concatenate([e, carry_ref[:, A_W + gi * GROUP:A_W + (gi + 1) * GROUP]], axis=0)
            k = 1
            while k < w:
                s = s + _shift_up(s, k)
                k *= 2
            dxp.append(s[:ts] - dpooled)

        carry_ref[:, 0:A_W] = dconv[0:HALO]
        carry_ref[:, A_W:D] = jnp.concatenate(e_first, axis=1)

        dp_ref[:, 0:512] = dxa.astype(BF16)
        dp_ref[:, 512:1024] = dgb.astype(BF16)
        dp_ref[:, 1024:1536] = dgc.astype(BF16)
        dp_ref[:, 1536:2048] = dza.astype(BF16)
        dp_ref[:, 2048:2560] = jnp.concatenate(dxp, axis=1).astype(BF16)
        dp_ref[:, 2560:3072] = dzp.astype(BF16)

        @pl.when(step == nt - 1)
        def _():
            dwob_ref[...] = dwo_ref[...].astype(BF16)

    rev = lambda s: (nt - 1 - s, 0)
    tile = pl.BlockSpec((ts, D), rev)
    bpt = ts // HALO
    halo_map = lambda s: (jnp.maximum((nt - 1 - s) * bpt - 1, 0), 0)
    return _call(
        body, name="even_backward", grid=(nt,),
        in_specs=[tile, tile, pl.BlockSpec((ts, PROJ), rev), pl.BlockSpec((HALO, PROJ), halo_map),
                  _full((3, A_W)), _full((N_GROUPS, GROUP, GROUP)), _full((N_GROUPS, GROUP, GROUP)),
                  _full((1, B_W)), _full((D, D)), _full((1, D))],
        out_specs=[pl.BlockSpec((ts, PROJ), rev), _full((D, D)), _full((8, A_W)),
                   _full((N_GROUPS, GROUP, GROUP)), _full((1, B_W)), _full((1, D))],
        out_shape=[jax.ShapeDtypeStruct((S, PROJ), BF16), jax.ShapeDtypeStruct((D, D), BF16),
                   jax.ShapeDtypeStruct((8, A_W), F32), jax.ShapeDtypeStruct((N_GROUPS, GROUP, GROUP), F32),
                   jax.ShapeDtypeStruct((1, B_W), F32), jax.ShapeDtypeStruct((1, D), F32)],
        scratch_shapes=[pltpu.VMEM((HALO, D), F32), pltpu.VMEM((ts, D), BF16), pltpu.VMEM((ts, D), BF16),
                        pltpu.VMEM((D, D), F32)],
        args=(g1, m0, p, p, cw, pw_b, pwt_b, ps, woutt_b, gpost), sides=sides)


def _odd_forward_parts(v, lg_ref, lb_ref, wt_ref, bfull_ref, vln_ref, sv_ref, r0, nr):
    rows = slice(r0, r0 + nr)
    mu = jnp.mean(v, axis=-1, keepdims=True)
    vc = v - mu
    rstd = lax.rsqrt(jnp.mean(vc * vc, axis=-1, keepdims=True) + EPS)
    vh = vc * rstd
    vln_ref[rows, :] = (vh * lg_ref[...] + lb_ref[...]).astype(BF16)
    for c in range(nr // GROUP):
        chunk = slice(r0 + c * GROUP, r0 + (c + 1) * GROUP)
        for h in range(N_HEADS):
            cols = slice(h * GROUP, (h + 1) * GROUP)
            sv_ref[chunk, cols] = _dot(wt_ref[h], vln_ref[chunk, cols]) + bfull_ref[h]
    return vh, rstd


SUB_ROWS = 256


def _odd_forward(h, x1, tgt, win_b, lg, lb, wt_b, bfull, wout_b, gpost, ts):
    S = x1.shape[0]

    def body(h_ref, x_ref, t_ref, wi_ref, lg_ref, lb_ref, wt_ref, bfull_ref, wo_ref, gp_ref,
             p_ref, m_ref, g_ref, loss_ref, vln_ref, sv_ref, y_ref):
        @pl.when(pl.program_id(0) == 0)
        def _():
            loss_ref[...] = jnp.zeros_like(loss_ref)

        loss = None
        for r0 in range(0, ts, SUB_ROWS):
            rows = slice(r0, r0 + SUB_ROWS)
            proj = _dot(h_ref[rows, :], wi_ref[...])
            p_ref[rows, :] = proj.astype(BF16)
            u, v, z = proj[:, 0:D], proj[:, D:2 * D], proj[:, 2 * D:3 * D]
            _odd_forward_parts(v, lg_ref, lb_ref, wt_ref, bfull_ref, vln_ref, sv_ref, r0, SUB_ROWS)
            y_ref[rows, :] = (u * sv_ref[rows, :] * (z * _sigmoid(z))).astype(BF16)
            m = _dot(y_ref[rows, :], wo_ref[...])
            m_ref[rows, :] = m.astype(BF16)
            x2 = x_ref[rows, :] + (m * _rms(m)) * gp_ref[...]
            err = x2 - t_ref[rows, :]
            g_ref[rows, :] = err * (1.0 / D)
            part = jnp.sum(err * err, axis=0, keepdims=True)
            loss = part if loss is None else loss + part
        loss_ref[...] += loss

    tile = pl.BlockSpec((ts, D), lambda i: (i, 0))
    small = _full((N_HEADS, GROUP, GROUP))
    return pl.pallas_call(
        body, name="odd_forward", grid=(S // ts,),
        in_specs=[pl.BlockSpec((ts, D), lambda i: (i, 0)), tile, tile, _full((D, PROJ)), _full((1, D)), _full((1, D)),
                  small, small, _full((D, D)), _full((1, D))],
        out_specs=[pl.BlockSpec((ts, PROJ), lambda i: (i, 0)), tile, tile, _full((1, D))],
        out_shape=[jax.ShapeDtypeStruct((S, PROJ), BF16), jax.ShapeDtypeStruct((S, D), BF16),
                   jax.ShapeDtypeStruct((S, D), F32), jax.ShapeDtypeStruct((1, D), F32)],
        scratch_shapes=[pltpu.VMEM((ts, D), BF16), pltpu.VMEM((ts, D), F32), pltpu.VMEM((ts, D), BF16)],
        compiler_params=_params(),
    )(h, x1, tgt, win_b, lg, lb, wt_b, bfull, wout_b, gpost)


def _resident(shape):
    return pl.BlockSpec(shape, lambda *_: (0,) * len(shape), pipeline_mode=pl.Buffered(1))


def _odd_backward(g2, m1, p, x1, lg, lb, wt_b, wtt_b, bfull, woutt_b, wint_b, gpost, gpre, ts):
    S = g2.shape[0]
    nt = S // ts
    sub = ts // 2

    def body(g_ref, m_ref, p_ref, x_ref, lg_ref, lb_ref, wt_ref, wtt_ref, bfull_ref, wot_ref, wit_ref, gp_ref, gn_ref,
             dp_ref, dx_ref, y_ref, dm_ref, dws_ref, dbs_ref, dlg_ref, dlb_ref, dgp_ref, dgn_ref,
             vln_ref, sv_ref, dsvb_ref, dvln_ref, dsum_ref):
        step = pl.program_id(0)

        @pl.when(step == 0)
        def _():
            dws_ref[...] = jnp.zeros_like(dws_ref)
            dsum_ref[...] = jnp.zeros_like(dsum_ref)
            dlg_ref[...] = jnp.zeros_like(dlg_ref)
            dlb_ref[...] = jnp.zeros_like(dlb_ref)
            dgp_ref[...] = jnp.zeros_like(dgp_ref)
            dgn_ref[...] = jnp.zeros_like(dgn_ref)

        tril = (lax.broadcasted_iota(jnp.int32, (GROUP, GROUP), 0)
                >= lax.broadcasted_iota(jnp.int32, (GROUP, GROUP), 1))
        sums = [None] * 4
        add = lambda k, part: sums.__setitem__(k, part if sums[k] is None else sums[k] + part)
        def post_norm(c):
            gp = gp_ref[...]
            for rows in c["pieces"]:
                g = g_ref[rows, :]
                m = m_ref[rows, :].astype(F32)
                q = _rms(m)
                n = m * q
                add(0, _fold(g * n))
                dn = g * gp
                dm = q * (dn - n * jnp.mean(dn * n, axis=-1, keepdims=True))
                dm_ref[rows, :] = dm.astype(BF16)

        def out_projection(c):
            c["dy"] = _dot(dm_ref[c["rows"], :], wot_ref[...])

        def layernorm(c):
            v = p_ref[c["rows"], D:2 * D].astype(F32)
            mu = jnp.mean(v, axis=-1, keepdims=True)
            vc = v - mu
            c["rstd"] = lax.rsqrt(jnp.mean(vc * vc, axis=-1, keepdims=True) + EPS)
            c["vh"] = vc * c["rstd"]
            vln_ref[c["rows"], :] = (c["vh"] * lg_ref[...] + lb_ref[...]).astype(BF16)

        def gate_matmuls(c):
            for chunk in c["chunks"]:
                for h in range(N_HEADS):
                    cols = slice(h * GROUP, (h + 1) * GROUP)
                    sv_ref[chunk, cols] = _dot(wt_ref[h], vln_ref[chunk, cols]) + bfull_ref[h]

        def gating(c):
            rows = c["rows"]
            u = p_ref[rows, 0:D].astype(F32)
            z = p_ref[rows, 2 * D:3 * D].astype(F32)
            sv = sv_ref[rows, :]
            sig = _sigmoid(z)
            sz = z * sig
            y_ref[rows, :] = (u * sv * sz).astype(BF16)
            dy = c.pop("dy")
            t = dy * sz
            dsv = t * u
            dsvb_ref[rows, :] = dsv.astype(BF16)
            for k in range(sub // GROUP):
                dsum_ref[...] += dsv[k * GROUP:(k + 1) * GROUP]
            dp_ref[rows, 0:D] = (t * sv).astype(BF16)
            dp_ref[rows, 2 * D:3 * D] = (dy * u * sv * _dsilu(z, sig)).astype(BF16)

        def gate_backward_matmuls(c):
            for chunk in c["chunks"]:
                for h in range(N_HEADS):
                    cols = slice(h * GROUP, (h + 1) * GROUP)
                    dvln_ref[chunk, cols] = _dot(wtt_ref[h], dsvb_ref[chunk, cols])
                    dws_ref[h] += jnp.where(tril, _dot_t1(dsvb_ref[chunk, cols], vln_ref[chunk, cols]), 0.0)

        def layernorm_backward(c):
            vh, rstd = c.pop("vh"), c.pop("rstd")
            dvln = dvln_ref[c["rows"], :]
            add(1, jnp.sum(dvln * vh, axis=0, keepdims=True))
            add(2, jnp.sum(dvln, axis=0, keepdims=True))
            dvh = dvln * lg_ref[...]
            dv = rstd * (dvh - jnp.mean(dvh, axis=-1, keepdims=True)
                         - vh * jnp.mean(dvh * vh, axis=-1, keepdims=True))
            dp_ref[c["rows"], D:2 * D] = dv.astype(BF16)

        def in_projection(c):
            c["dh"] = _dot(dp_ref[c["rows"], :], wit_ref[...])

        def pre_norm(c):
            dh_all = c.pop("dh")
            gn = gn_ref[...]
            for k, rows in enumerate(c["pieces"]):
                dh = dh_all[k * PIECE:(k + 1) * PIECE]
                xv = x_ref[rows, :]
                r = _rms(xv)
                xn = xv * r
                add(3, _fold(dh * xn))
                dxn = dh * gn
                dx_ref[rows, :] = g_ref[rows, :] + r * (dxn - xn * jnp.mean(dxn * xn, axis=-1, keepdims=True))

        phases = [post_norm, out_projection, layernorm, gate_matmuls, gating, gate_backward_matmuls,
                  layernorm_backward, in_projection, pre_norm]
        groups = [dict(rows=slice(r0, r0 + sub),
                       chunks=[slice(r0 + k * GROUP, r0 + (k + 1) * GROUP) for k in range(sub // GROUP)],
                       pieces=[slice(r0 + k, r0 + k + PIECE) for k in range(0, sub, PIECE)])
                  for r0 in range(0, ts, sub)]
        for group in groups:
            for phase in phases:
                phase(group)

        dgp_ref[...] += jnp.sum(sums[0], axis=0, keepdims=True)
        dlg_ref[...] += jnp.sum(sums[1], axis=0, keepdims=True)
        dlb_ref[...] += jnp.sum(sums[2], axis=0, keepdims=True)
        dgn_ref[...] += jnp.sum(sums[3], axis=0, keepdims=True)

        @pl.when(step == nt - 1)
        def _():
            ones = jnp.ones((8, GROUP), F32)
            for h in range(N_HEADS):
                cols = slice(h * GROUP, (h + 1) * GROUP)
                total = lax.dot_general(ones, dsum_ref[:, cols], (((1,), (1,)), ((), ())),
                                        precision=lax.Precision.HIGHEST, preferred_element_type=F32)
                dbs_ref[h:h + 1, :] = total[0:1, :]

    tile = pl.BlockSpec((ts, D), lambda i: (i, 0))
    wide = pl.BlockSpec((ts, PROJ), lambda i: (i, 0))
    small = _full((N_HEADS, GROUP, GROUP))
    heads = _resident((N_HEADS, GROUP, GROUP))
    vec = _full((1, D))
    return pl.pallas_call(
        body, name="odd_backward", grid=(nt,),
        in_specs=[tile, tile, wide, tile, vec, vec, heads, heads, heads, _resident((D, D)), _resident((PROJ, D)),
                  vec, vec],
        out_specs=[wide, tile, tile, tile, small, _full((N_HEADS, GROUP)), vec, vec, vec, vec],
        out_shape=[jax.ShapeDtypeStruct((S, PROJ), BF16), jax.ShapeDtypeStruct((S, D), F32),
                   jax.ShapeDtypeStruct((S, D), BF16), jax.ShapeDtypeStruct((S, D), BF16),
                   jax.ShapeDtypeStruct((N_HEADS, GROUP, GROUP), F32), jax.ShapeDtypeStruct((N_HEADS, GROUP), F32),
                   jax.ShapeDtypeStruct((1, D), F32), jax.ShapeDtypeStruct((1, D), F32),
                   jax.ShapeDtypeStruct((1, D), F32), jax.ShapeDtypeStruct((1, D), F32)],
        scratch_shapes=[pltpu.VMEM((ts, D), BF16), pltpu.VMEM((ts, D), F32), pltpu.VMEM((ts, D), BF16),
                        pltpu.VMEM((ts, D), F32), pltpu.VMEM((GROUP, D), F32)],
        compiler_params=_params(),
    )(g2, m1, p, x1, lg, lb, wt_b, wtt_b, bfull, woutt_b, wint_b, gpost, gpre)


def _odd_backward_unfused(g2, m1, p, lg, lb, wt_b, wtt_b, bfull, woutt_b, gpost, ts):
    S = g2.shape[0]
    nt = S // ts

    def body(g_ref, m_ref, p_ref, lg_ref, lb_ref, wt_ref, wtt_ref, bfull_ref, wot_ref, gp_ref,
             dp_ref, dwob_ref, dws_ref, dbs_ref, dlg_ref, dlb_ref, dgp_ref,
             vln_ref, sv_ref, y_ref, dm_ref, dsv_ref, dsvb_ref, dvln_ref, dsum_ref, dwo_ref):
        step = pl.program_id(0)

        @pl.when(step == 0)
        def _():
            dwo_ref[...] = jnp.zeros_like(dwo_ref)
            dws_ref[...] = jnp.zeros_like(dws_ref)
            dsum_ref[...] = jnp.zeros_like(dsum_ref)
            dlg_ref[...] = jnp.zeros_like(dlg_ref)
            dlb_ref[...] = jnp.zeros_like(dlb_ref)
            dgp_ref[...] = jnp.zeros_like(dgp_ref)

        g = g_ref[...]
        m = m_ref[...]
        q = _rms(m)
        n = m * q
        dgp_ref[...] += jnp.sum(g * n, axis=0, keepdims=True)
        dn = g * gp_ref[...]
        dm = q * (dn - n * jnp.mean(dn * n, axis=-1, keepdims=True))
        dm_ref[...] = dm.astype(BF16)
        dy = _dot(dm_ref[...], wot_ref[...])

        u = p_ref[:, 0:D].astype(F32)
        z = p_ref[:, 2 * D:3 * D].astype(F32)
        vh, rstd = _odd_forward_parts(p_ref[:, D:2 * D].astype(F32), lg_ref, lb_ref, wt_ref, bfull_ref, vln_ref,
                                      sv_ref, 0, ts)
        sv = sv_ref[...]
        sig = _sigmoid(z)
        sz = z * sig
        y_ref[...] = (u * sv * sz).astype(BF16)
        dwo_ref[...] += _dot_t0(y_ref[...], dm_ref[...])

        t = dy * sz
        du = t * sv
        dsv = t * u
        dz = dy * u * sv * _dsilu(z, sig)
        dsv_ref[...] = dsv
        dsvb_ref[...] = dsv.astype(BF16)

        tril = (lax.broadcasted_iota(jnp.int32, (GROUP, GROUP), 0)
                >= lax.broadcasted_iota(jnp.int32, (GROUP, GROUP), 1))
        for c in range(ts // GROUP):
            rows = slice(c * GROUP, (c + 1) * GROUP)
            dsum_ref[...] += dsv_ref[rows, :]
            for h in range(N_HEADS):
                cols = slice(h * GROUP, (h + 1) * GROUP)
                dvln_ref[rows, cols] = _dot(wtt_ref[h], dsvb_ref[rows, cols])
                dws_ref[h] += jnp.where(tril, _dot_t1(dsvb_ref[rows, cols], vln_ref[rows, cols]), 0.0)

        dvln = dvln_ref[...]
        dlg_ref[...] += jnp.sum(dvln * vh, axis=0, keepdims=True)
        dlb_ref[...] += jnp.sum(dvln, axis=0, keepdims=True)
        dvh = dvln * lg_ref[...]
        dv = rstd * (dvh - jnp.mean(dvh, axis=-1, keepdims=True)
                     - vh * jnp.mean(dvh * vh, axis=-1, keepdims=True))
        dp_ref[:, 0:D] = du.astype(BF16)
        dp_ref[:, D:2 * D] = dv.astype(BF16)
        dp_ref[:, 2 * D:3 * D] = dz.astype(BF16)

        @pl.when(step == nt - 1)
        def _():
            dwob_ref[...] = dwo_ref[...].astype(BF16)
            ones = jnp.ones((8, GROUP), F32)
            for h in range(N_HEADS):
                cols = slice(h * GROUP, (h + 1) * GROUP)
                sums = lax.dot_general(ones, dsum_ref[:, cols], (((1,), (1,)), ((), ())),
                                       precision=lax.Precision.HIGHEST, preferred_element_type=F32)
                dbs_ref[h:h + 1, :] = sums[0:1, :]

    tile = pl.BlockSpec((ts, D), lambda i: (i, 0))
    small = _full((N_HEADS, GROUP, GROUP))
    vec = _full((1, D))
    return pl.pallas_call(
        body, name="odd_backward", grid=(nt,),
        in_specs=[tile, tile, pl.BlockSpec((ts, PROJ), lambda i: (i, 0)), vec, vec, small, small, small,
                  _full((D, D)), vec],
        out_specs=[pl.BlockSpec((ts, PROJ), lambda i: (i, 0)), _full((D, D)), small, _full((N_HEADS, GROUP)),
                   vec, vec, vec],
        out_shape=[jax.ShapeDtypeStruct((S, PROJ), BF16), jax.ShapeDtypeStruct((D, D), BF16),
                   jax.ShapeDtypeStruct((N_HEADS, GROUP, GROUP), F32), jax.ShapeDtypeStruct((N_HEADS, GROUP), F32),
                   jax.ShapeDtypeStruct((1, D), F32), jax.ShapeDtypeStruct((1, D), F32),
                   jax.ShapeDtypeStruct((1, D), F32)],
        scratch_shapes=[pltpu.VMEM((ts, D), BF16), pltpu.VMEM((ts, D), F32), pltpu.VMEM((ts, D), BF16),
                        pltpu.VMEM((ts, D), BF16), pltpu.VMEM((ts, D), F32), pltpu.VMEM((ts, D), BF16),
                        pltpu.VMEM((ts, D), F32), pltpu.VMEM((GROUP, D), F32), pltpu.VMEM((D, D), F32)],
        compiler_params=_params(),
    )(g2, m1, p, lg, lb, wt_b, wtt_b, bfull, woutt_b, gpost)


def _input_backward(dp, wt_b, x, g_res, gpre, ts, name, sides=()):
    S = x.shape[0]

    def body(dp_ref, wt_ref, x_ref, g_ref, gp_ref, dx_ref, dgp_ref):
        @pl.when(pl.program_id(0) == 0)
        def _():
            dgp_ref[...] = jnp.zeros_like(dgp_ref)

        dgp = None
        for r0 in range(0, ts, SUB_ROWS):
            rows = slice(r0, r0 + SUB_ROWS)
            dh = _dot(dp_ref[rows, :], wt_ref[...])
            xv = x_ref[rows, :]
            r = _rms(xv)
            xn = xv * r
            part = jnp.sum(dh * xn, axis=0, keepdims=True)
            dgp = part if dgp is None else dgp + part
            dxn = dh * gp_ref[...]
            dx_ref[rows, :] = g_ref[rows, :] + r * (dxn - xn * jnp.mean(dxn * xn, axis=-1, keepdims=True))
        dgp_ref[...] += dgp

    tile = pl.BlockSpec((ts, D), lambda i: (i, 0))
    return _call(
        body, name=name, grid=(S // ts,),
        in_specs=[pl.BlockSpec((ts, PROJ), lambda i: (i, 0)), _full((PROJ, D)), tile, tile, _full((1, D))],
        out_specs=[tile, _full((1, D))],
        out_shape=[jax.ShapeDtypeStruct((S, D), F32), jax.ShapeDtypeStruct((1, D), F32)],
        scratch_shapes=[], args=(dp, wt_b, x, g_res, gpre), sides=sides)


def _out_weight_grad(a, b, ts, name):
    S = a.shape[0]
    ts = min(ts, S)
    nt = S // ts

    def body(a_ref, b_ref, out_ref, acc_ref):
        t = pl.program_id(0)

        @pl.when(t == 0)
        def _():
            acc_ref[...] = jnp.zeros_like(acc_ref)

        acc_ref[...] += _dot_t0(a_ref[...], b_ref[...])

        @pl.when(t == nt - 1)
        def _():
            out_ref[...] = acc_ref[...].astype(BF16)

    tile = pl.BlockSpec((ts, D), lambda t: (t, 0))
    return pl.pallas_call(
        body, name=name, grid=(nt,), in_specs=[tile, tile], out_specs=_full((D, D)),
        out_shape=jax.ShapeDtypeStruct((D, D), BF16), scratch_shapes=[pltpu.VMEM((D, D), F32)],
        compiler_params=_params(),
    )(a, b)


N_CHIPS = 4


def _weight_grad_scatter(h, dp, ts, name, sides=()):
    S = h.shape[0]
    ts = min(ts, S)
    nt = S // ts
    panel = 2 * SHARD_IN
    last = N_CHIPS - 1

    def body(h_ref, dp_ref, parts_ref, acc_ref, half_ref, swap_ref, sum_ref, send_sems, recv_sems):
        s, t = pl.program_id(0), pl.program_id(1)
        x, y, c = _mesh_position()
        sibling = (x, y, 1 - c)

        def swap(k):
            return _remote(half_ref.at[k, 1], swap_ref.at[k], send_sems, recv_sems, k, sibling)

        def chip_sum(k):
            flip = last - k
            to = (x ^ (flip >> 1), y ^ (flip & 1), c)
            return _remote(sum_ref.at[k], parts_ref.at[1 + flip], send_sems, recv_sems, 3 + flip, to)

        own = pltpu.make_async_copy(half_ref.at[last, 0], parts_ref.at[0], recv_sems.at[7])
        to_sibling = _remote(half_ref.at[last, 1], parts_ref.at[1], send_sems, recv_sems, 3, sibling)

        @pl.when(t == 0)
        def _():
            acc_ref[...] = jnp.zeros_like(acc_ref)

        for k in range(last):
            @pl.when((s == k + 1) & (t == 0))
            def _():
                swap(k).wait_recv()
                sum_ref[k] = (half_ref[k, 0].astype(F32) + swap_ref[k].astype(F32)).astype(BF16)
                chip_sum(k).start()

        acc_ref[...] += _dot_t0(h_ref[...], dp_ref[...])

        @pl.when(t == nt - 1)
        def _():
            for core in range(2):
                @pl.when(c == core)
                def _():
                    half_ref[s, core] = acc_ref[:, 0:SHARD_IN].astype(BF16)
                    half_ref[s, 1 - core] = acc_ref[:, SHARD_IN:panel].astype(BF16)
            for k in range(last):
                @pl.when(s == k)
                def _():
                    swap(k).start()

        @pl.when((s == last) & (t == nt - 1))
        def _():
            own.start()
            to_sibling.start()
            own.wait()
            _remote(half_ref.at[last, 1], parts_ref.at[1], send_sems, recv_sems, 3, sibling).wait_recv()
            for k in range(last):
                chip_sum(k).wait_recv()
            to_sibling.wait_send()
            for k in range(last):
                swap(k).wait_send()
                chip_sum(k).wait_send()

    chip_panel = lambda s, t: (t, (2 * lax.axis_index("x") + lax.axis_index("y")) ^ (last - s))
    sems = pltpu.SemaphoreType.DMA((8,))
    return _call(
        body, name=name, grid=(N_CHIPS, nt),
        in_specs=[pl.BlockSpec((ts, D), lambda s, t: (t, 0)), pl.BlockSpec((ts, panel), chip_panel)],
        out_specs=[HBM_SPEC], out_shape=[jax.ShapeDtypeStruct((N_CHIPS + 1, D, SHARD_IN), BF16)],
        scratch_shapes=[pltpu.VMEM((D, panel), F32), pltpu.VMEM((N_CHIPS, 2, D, SHARD_IN), BF16),
                        pltpu.VMEM((last, D, SHARD_IN), BF16), pltpu.VMEM((last, D, SHARD_IN), BF16), sems, sems],
        args=(h, dp), sides=sides)


def _adamw(w, g, m, v):
    m = ADAM_B1 * m + (1.0 - ADAM_B1) * g
    v = ADAM_B2 * v + (1.0 - ADAM_B2) * (g * g)
    m_hat = m / (1.0 - ADAM_B1 ** ADAM_STEP)
    v_hat = v / (1.0 - ADAM_B2 ** ADAM_STEP)
    delta = -ADAM_LR * (m_hat / (jnp.sqrt(v_hat) + ADAM_EPS) + ADAM_WD * w)
    return delta, m, v


def _sum_update(parts, w, m, v, name):
    R, C = w.shape
    rb = min(R, 256)
    n_parts = parts.shape[0]

    def body(p_ref, w_ref, m_ref, v_ref, g_out, d_out, m_out, v_out):
        acc = p_ref[0].astype(F32)
        for d in range(1, n_parts):
            acc = acc + p_ref[d].astype(F32)
        g_out[...] = acc
        delta, m_new, v_new = _adamw(w_ref[...], acc, m_ref[...], v_ref[...])
        d_out[...] = delta
        m_out[...] = m_new
        v_out[...] = v_new

    blk = pl.BlockSpec((rb, C), lambda i: (i, 0))
    return pl.pallas_call(
        body, name=name, grid=(R // rb,),
        in_specs=[pl.BlockSpec((n_parts, rb, C), lambda i: (0, i, 0)), blk, blk, blk], out_specs=[blk] * 4,
        out_shape=[jax.ShapeDtypeStruct((R, C), F32)] * 4, compiler_params=_params(),
    )(parts, w, m, v)


def _small_sum_update(partials, row_of, weights, loss_parts):
    n_p, n_w = len(partials), len(weights)

    def body(*refs):
        p_refs = refs[:n_p]
        loss_ref = refs[n_p]
        w_refs = refs[n_p + 1:n_p + 1 + 3 * n_w]
        loss_out = refs[n_p + 1 + 3 * n_w]
        outs = refs[n_p + 2 + 3 * n_w:]

        def total(ref):
            acc = ref[0]
            for d in range(1, N_DEV):
                acc = acc + ref[d]
            return acc

        lsum = jnp.sum(total(loss_ref), axis=-1, keepdims=True) * (0.5 / D)
        loss_out[...] = jnp.broadcast_to(lsum, loss_out.shape)
        for k in range(n_p):
            i, r0 = row_of[k]
            g = total(p_refs[k])
            w_ref, m_ref, v_ref = w_refs[3 * i:3 * i + 3]
            g_out, d_out, m_out, v_out = outs[4 * i:4 * i + 4]
            if g.ndim == 2:
                at = (slice(r0, r0 + g.shape[0]), slice(None))
            else:
                at = (slice(None),) * g.ndim
            delta, m_new, v_new = _adamw(w_ref[at], g, m_ref[at], v_ref[at])
            g_out[at] = g
            d_out[at] = delta
            m_out[at] = m_new
            v_out[at] = v_new

    vm = pl.BlockSpec(memory_space=pltpu.VMEM)
    flat_w = [a for wmv in weights for a in wmv]
    out_shape = [jax.ShapeDtypeStruct((1, 128), F32)]
    for w, _, _ in weights:
        out_shape += [jax.ShapeDtypeStruct(w.shape, F32)] * 4
    return pl.pallas_call(
        body, name="small_update", in_specs=[vm] * (n_p + 1 + 3 * n_w), out_specs=[vm] * len(out_shape),
        out_shape=out_shape, compiler_params=pltpu.CompilerParams(vmem_limit_bytes=VMEM_LIMIT),
    )(*partials, loss_parts, *flat_w)


def _pack_small_shard(cw, lg, lb):
    return (jnp.pad(cw, ((0, 13), (0, 64))) + jnp.pad(lg, ((8, 7), (0, 0))) + jnp.pad(lb, ((9, 6), (0, 0))))


TS_PRENORM = 1024
TS_FWD = 512
TS_BWD = 256
TS_WGRAD = 2048


def kernel(x, pre_norm, post_norm, even_w_in, even_conv_w, even_pool_w, even_pool_scale, even_w_out, odd_w_in, odd_ln_g, odd_ln_b, odd_w_s, odd_b_s, odd_w_out, loss_target, m_pre_norm, m_post_norm, m_even_w_in, m_even_conv_w, m_even_pool_w, m_even_pool_scale, m_even_w_out, m_odd_w_in, m_odd_ln_g, m_odd_ln_b, m_odd_w_s, m_odd_b_s, m_odd_w_out, v_pre_norm, v_post_norm, v_even_w_in, v_even_conv_w, v_even_pool_w, v_even_pool_scale, v_even_w_out, v_odd_w_in, v_odd_ln_g, v_odd_ln_b, v_odd_w_s, v_odd_b_s, v_odd_w_out):
    S = x.shape[1]
    xs = x.reshape(S, D)
    tgt = loss_target.reshape(S, D)

    gpre0, gpre1 = pre_norm[0:1], pre_norm[1:2]
    gpost0, gpost1 = post_norm[0:1], post_norm[1:2]
    small_shard = _pack_small_shard(even_conv_w[0], odd_ln_g, odd_ln_b)
    h0, wie_g, small_g, woe_g, wio_b, woo_b = _prenorm_gather(xs, gpre0, even_w_in[0], small_shard, even_w_out[0],
                                                              odd_w_in[0], odd_w_out[0], TS_PRENORM)
    wie = jnp.transpose(wie_g, (1, 0, 2)).reshape(D, PROJ)
    wie_t = jnp.transpose(wie_g, (0, 2, 1)).reshape(PROJ, D)
    woe = woe_g.reshape(D, D)
    woe_t = woe.T
    conv_w = jnp.transpose(small_g[:, 0:3, 0:64], (1, 0, 2)).reshape(3, A_W)
    ln_g = small_g[:, 8, :].reshape(1, D)
    ln_b = small_g[:, 9, :].reshape(1, D)

    pool_w_b = even_pool_w[0].astype(BF16)
    pool_wt_b = jnp.swapaxes(even_pool_w[0], 1, 2).astype(BF16)
    ws_tril = jnp.tril(odd_w_s[0])
    ws_b = ws_tril.astype(BF16)
    wst_b = jnp.swapaxes(ws_tril, 1, 2).astype(BF16)
    b_full = jnp.broadcast_to(odd_b_s[0][:, :, None], (N_HEADS, GROUP, GROUP))

    p0, m0, x1, h1, wio_g, woo_g = _even_forward_fused(h0, xs, wie, conv_w, pool_w_b, even_pool_scale, woe, gpost0,
                                                       gpre1, TS_FWD, sides=[(_Gather, [wio_b, woo_b])])
    wio = jnp.transpose(wio_g, (1, 0, 2)).reshape(D, PROJ)
    wio_t = jnp.transpose(wio_g, (0, 2, 1)).reshape(PROJ, D)
    woo = woo_g.reshape(D, D)
    woo_t = woo.T
    p1, m1, g2, loss_vec = _odd_forward(h1, x1, tgt, wio, ln_g, ln_b, ws_b, b_full, woo, gpost1, TS_FWD)

    dp1, g1, y1, dm1, dws, dbs, dlg, dlb, dgpost1, dgpre1 = _odd_backward(
        g2, m1, p1, x1, ln_g, ln_b, ws_b, wst_b, b_full, woo_t, wio_t, gpost1, gpre1, TS_FWD)
    dwoo = _out_weight_grad(y1, dm1, TS_WGRAD, "odd_out_weight_grad")
    by_owner = lambda dwo: dwo.reshape(N_DEV, SHARD_OUT, D)
    (dwio_parts, dwoo_parts, dlg_parts, dlb_parts,
     dws_g, dbs_g, gpost1_g, gpre1_g, loss_g) = _weight_grad_scatter(
        h1, dp1, TS_WGRAD, "odd_weight_grad",
        sides=[(_Scatter, [by_owner(dwoo), dlg.reshape(N_DEV, 1, 128), dlb.reshape(N_DEV, 1, 128)]),
               (_Gather, [dws, dbs, dgpost1, dgpre1, loss_vec])])
    dp0, gx, mix0, dm0, dcw, dpw, dps, dgpost0, dgpre0 = _even_backward_fused(
        g1, m0, p0, xs, conv_w, pool_w_b, pool_wt_b, even_pool_scale, woe_t, wie_t, gpost0, gpre0, TS_FWD)
    dwoe = _out_weight_grad(mix0, dm0, TS_WGRAD, "even_out_weight_grad")
    dcw_by_owner = jnp.transpose(dcw[0:3].reshape(3, N_DEV, 64), (1, 0, 2))
    dwie_parts, dwoe_parts, dcw_parts, dpw_g, dps_g, gpost0_g, gpre0_g = _weight_grad_scatter(
        h0, dp0, TS_WGRAD, "even_weight_grad",
        sides=[(_Scatter, [by_owner(dwoe), dcw_by_owner]), (_Gather, [dpw, dps, dgpost0, dgpre0])])

    g_wie, d_wie, nm_wie, nv_wie = _sum_update(dwie_parts, even_w_in[0], m_even_w_in[0], v_even_w_in[0],
                                               "update_even_w_in")
    g_wio, d_wio, nm_wio, nv_wio = _sum_update(dwio_parts, odd_w_in[0], m_odd_w_in[0], v_odd_w_in[0],
                                               "update_odd_w_in")
    g_woe, d_woe, nm_woe, nv_woe = _sum_update(dwoe_parts, even_w_out[0], m_even_w_out[0], v_even_w_out[0],
                                               "update_even_w_out")
    g_woo, d_woo, nm_woo, nv_woo = _sum_update(dwoo_parts, odd_w_out[0], m_odd_w_out[0], v_odd_w_out[0],
                                               "update_odd_w_out")

    partials = [gpre0_g, gpre1_g, gpost0_g, gpost1_g, dpw_g, dps_g, dws_g, dbs_g, dcw_parts, dlg_parts, dlb_parts]
    row_of = [(0, 0), (0, 1), (1, 0), (1, 1), (2, 0), (3, 0), (4, 0), (5, 0), (6, 0), (7, 0), (8, 0)]
    weights = [(pre_norm, m_pre_norm, v_pre_norm), (post_norm, m_post_norm, v_post_norm),
               (even_pool_w[0], m_even_pool_w[0], v_even_pool_w[0]),
               (even_pool_scale, m_even_pool_scale, v_even_pool_scale),
               (odd_w_s[0], m_odd_w_s[0], v_odd_w_s[0]), (odd_b_s[0], m_odd_b_s[0], v_odd_b_s[0]),
               (even_conv_w[0], m_even_conv_w[0], v_even_conv_w[0]),
               (odd_ln_g, m_odd_ln_g, v_odd_ln_g), (odd_ln_b, m_odd_ln_b, v_odd_ln_b)]
    small = _small_sum_update(partials, row_of, weights, loss_g)
    loss = small[0][0, 0]

    def leaves(k, big):
        pre, post, pw, psc, ws, bs, cw, lg, lb = [small[1 + 4 * i + k] for i in range(len(weights))]
        wie_k, woe_k, wio_k, woo_k = big
        return [pre, post, wie_k[None], cw[None], pw[None], psc, woe_k[None], wio_k[None], lg, lb, ws[None], bs[None],
                woo_k[None]]

    outs = [loss, gx.reshape(1, S, D)]
    outs += leaves(0, (g_wie, g_woe, g_wio, g_woo))
    outs += leaves(1, (d_wie, d_woe, d_wio, d_woo))
    outs += leaves(2, (nm_wie, nm_woe, nm_wio, nm_woo))
    outs += leaves(3, (nv_wie, nv_woe, nv_wio, nv_woo))
    return tuple(outs)
```

```python
import functools

import jax
import jax.numpy as jnp
from jax import lax
from jax.experimental import pallas as pl
from jax.experimental.pallas import tpu as pltpu

F32 = jnp.float32
BF16 = jnp.bfloat16
MESH = pl.DeviceIdType.MESH

D = 1024
EPS = 1e-6
A_W = 512
B_W = 512
POOL_WINDOWS = (2, 4, 8, 16)
GROUP = 128
N_GROUPS = 4
N_HEADS = 8
HALO = 16
PROJ = 3072
N_DEV = 8
SHARD_IN = PROJ // N_DEV
SHARD_OUT = D // N_DEV

ADAM_LR = 0.001
ADAM_B1 = 0.9
ADAM_B2 = 0.999
ADAM_EPS = 1e-08
ADAM_WD = 0.01
ADAM_STEP = 10

VMEM_LIMIT = 54 * 1024 * 1024


def _params(n_grid=1, vmem=VMEM_LIMIT):
    return pltpu.CompilerParams(dimension_semantics=("arbitrary",) * n_grid, vmem_limit_bytes=vmem)


def _full(shape):
    return pl.BlockSpec(shape, lambda *_: (0,) * len(shape))


def _sigmoid(z):
    return jax.nn.sigmoid(z)


def _dsilu(z, s):
    return s * (1.0 + z * (1.0 - s))


def _dot(a, b):
    return jnp.dot(a, b, preferred_element_type=F32)


def _dot_t0(a, b):
    return lax.dot_general(a, b, (((0,), (0,)), ((), ())), preferred_element_type=F32)


def _dot_t1(a, b):
    return lax.dot_general(a, b, (((1,), (1,)), ((), ())), preferred_element_type=F32)


def _rms(x):
    return lax.rsqrt(jnp.mean(x * x, axis=-1, keepdims=True) + EPS)


PIECE = 16


def _fold(a):
    return a[0:8] + a[8:16]


def _shift_down(a, k):
    return pltpu.roll(a, k, 0)


def _shift_up(a, k):
    return pltpu.roll(a, a.shape[0] - k, 0)


def _mesh_position():
    return lax.axis_index("x"), lax.axis_index("y"), lax.axis_index("c")


def _index(pos):
    return 4 * pos[0] + 2 * pos[1] + pos[2]


def _peer(pos, d):
    x, y, c = pos
    return (1 - x if d & 4 else x, 1 - y if d & 2 else y, 1 - c if d & 1 else c)


def _remote(src, dst, send_sems, recv_sems, k, to):
    return pltpu.make_async_remote_copy(src_ref=src, dst_ref=dst, send_sem=send_sems.at[k], recv_sem=recv_sems.at[k],
                                        device_id=to, device_id_type=MESH)


class _Gather:
    def __init__(self, srcs, dsts, send_sems, recv_sems):
        x, y, c = _mesh_position()
        n = len(srcs)
        me, sibling = (x, y, c), (x, y, 1 - c)
        chips = [(1 - x, y), (x, 1 - y), (1 - x, 1 - y)]

        def copy(k, t, block, to, src=None):
            slot = dsts[t].at[_index(block)]
            return _remote(slot if src is None else src, slot, send_sems, recv_sems, k * n + t, to)

        tensors = range(n)
        self.local = [pltpu.make_async_copy(srcs[t], dsts[t].at[_index(me)], recv_sems.at[7 * n + t]) for t in tensors]
        self.first = [copy(1 + j, t, me, chip + (c,), srcs[t]) for t in tensors for j, chip in enumerate(chips)]
        self.first += [copy(0, t, me, sibling, srcs[t]) for t in tensors]
        self.ici_in = [copy(1 + j, t, chip + (c,), me) for t in tensors for j, chip in enumerate(chips)]
        self.passed = [copy(4 + j, t, chip + (c,), sibling) for t in tensors for j, chip in enumerate(chips)]
        self.d2d_in = [copy(0, t, sibling, me) for t in tensors]
        self.d2d_in += [copy(4 + j, t, chip + (1 - c,), me) for t in tensors for j, chip in enumerate(chips)]

    def start(self):
        for cp in self.local + self.first:
            cp.start()

    def middle(self):
        for landed, onward in zip(self.ici_in, self.passed):
            landed.wait_recv()
            onward.start()

    def finish(self):
        for cp in self.d2d_in:
            cp.wait_recv()
        for cp in self.first + self.passed:
            cp.wait_send()
        for cp in self.local:
            cp.wait()


class _Scatter:
    def __init__(self, srcs, dsts, send_sems, recv_sems):
        pos = _mesh_position()
        n = len(srcs)
        self.local = [pltpu.make_async_copy(srcs[t].at[_index(pos)], dsts[t].at[0], recv_sems.at[t]) for t in range(n)]
        self.remote = []
        for d in range(1, N_DEV):
            to = _peer(pos, d)
            self.remote += [_remote(srcs[t].at[_index(to)], dsts[t].at[d], send_sems, recv_sems, d * n + t, to)
                            for t in range(n)]

    def start(self):
        for cp in self.local + self.remote:
            cp.start()

    def middle(self):
        pass

    def finish(self):
        for cp in self.local:
            cp.wait()
        for cp in self.remote:
            cp.wait_recv()
        for cp in self.remote:
            cp.wait_send()


PARTS = 5
LANDING = PARTS + 6


class _ChipSumScatter:
    def __init__(self, srcs, dsts, send_sems, recv_sems):
        x, y, c = _mesh_position()
        self.n = len(srcs)
        self.srcs, self.dsts = srcs, dsts
        me, sibling = (x, y, c), (x, y, 1 - c)
        self.chips = [(x ^ (f >> 1), y ^ (f & 1)) for f in range(1, N_CHIPS)]
        rc = lambda t, k, src, slot, to: _remote(src, dsts[t].at[slot], send_sems, recv_sems, 8 * t + k, to)
        self.local = [pltpu.make_async_copy(srcs[t].at[_index(me)], dsts[t].at[0], recv_sems.at[8 * t])
                      for t in range(self.n)]
        self.to_sibling = [rc(t, 1, srcs[t].at[_index(sibling)], 1, sibling) for t in range(self.n)]
        self.swaps = [[rc(t, 2 + j, srcs[t].at[_index(chip + (1 - c,))], 5 + j, sibling)
                       for j, chip in enumerate(self.chips)] for t in range(self.n)]
        self.sums = [[rc(t, 5 + j, dsts[t].at[8 + j], 2 + j, chip + (c,))
                      for j, chip in enumerate(self.chips)] for t in range(self.n)]
        self.mine = [[srcs[t].at[_index(chip + (c,))] for chip in self.chips] for t in range(self.n)]

    def start(self):
        for t in range(self.n):
            for cp in [self.local[t], self.to_sibling[t]] + self.swaps[t]:
                cp.start()

    def middle(self):
        for t in range(self.n):
            block = self.srcs[t].shape[1:]
            for j in range(N_CHIPS - 1):
                self.swaps[t][j].wait_recv()

                def add(a_ref, b_ref, t=t, j=j):
                    pltpu.sync_copy(self.mine[t][j], a_ref)
                    pltpu.sync_copy(self.dsts[t].at[5 + j], b_ref)
                    a_ref[...] = (a_ref[...].astype(F32) + b_ref[...].astype(F32)).astype(BF16)
                    pltpu.sync_copy(a_ref, self.dsts[t].at[8 + j])

                pl.run_scoped(add, pltpu.VMEM(block, BF16), pltpu.VMEM(block, BF16))
                self.sums[t][j].start()

    def finish(self):
        for t in range(self.n):
            self.local[t].wait()
            self.to_sibling[t].wait_recv()
            for cp in self.sums[t]:
                cp.wait_recv()
            for cp in [self.to_sibling[t]] + self.swaps[t] + self.sums[t]:
                cp.wait_send()


HBM_SPEC = pl.BlockSpec(memory_space=pltpu.HBM)
MIDDLE_STEPS_BEFORE_END = 4


def _middle_step(exchange, total):
    return min(1, total - 1) if exchange is _ChipSumScatter else max(total - MIDDLE_STEPS_BEFORE_END, 0)


def _landing_shape(exchange, src):
    if exchange is _Gather:
        return (N_DEV,) + src.shape
    if exchange is _ChipSumScatter:
        return (LANDING,) + src.shape[1:]
    return src.shape


def _call(body, *, name, grid, in_specs, out_specs, out_shape, scratch_shapes, args, sides=()):
    params = _params(len(grid))
    if not sides:
        return pl.pallas_call(body, name=name, grid=grid, in_specs=in_specs, out_specs=out_specs, out_shape=out_shape,
                              scratch_shapes=scratch_shapes, compiler_params=params)(*args)
    n_in, n_out, n_scratch = len(in_specs), len(out_specs), len(scratch_shapes)
    counts = [len(srcs) for _, srcs in sides]
    ns = sum(counts)
    total = 1
    for g in grid:
        total *= g
    side_args, side_shapes, side_sems = [], [], []
    for exchange, srcs in sides:
        side_args += list(srcs)
        side_shapes += [jax.ShapeDtypeStruct(_landing_shape(exchange, s), s.dtype) for s in srcs]
        side_sems += [pltpu.SemaphoreType.DMA((N_DEV * len(srcs),))] * 2

    def wrapped(*refs):
        ins, side_in = refs[:n_in], refs[n_in:n_in + ns]
        outs = refs[n_in + ns:n_in + ns + n_out]
        side_out = refs[n_in + ns + n_out:n_in + 2 * ns + n_out]
        rest = refs[n_in + 2 * ns + n_out:]
        scratch, sems = rest[:n_scratch], rest[n_scratch:]
        step = pl.program_id(0)
        for axis in range(1, len(grid)):
            step = step * grid[axis] + pl.program_id(axis)

        def exchanges():
            built, at = [], 0
            for k, (exchange, _) in enumerate(sides):
                built.append(exchange(side_in[at:at + counts[k]], side_out[at:at + counts[k]],
                                      sems[2 * k], sems[2 * k + 1]))
                at += counts[k]
            return built

        @pl.when(step == 0)
        def _():
            for ex in exchanges():
                ex.start()

        for k, (exchange, _) in enumerate(sides):
            @pl.when(step == _middle_step(exchange, total))
            def _(k=k):
                exchanges()[k].middle()

        body(*ins, *outs, *scratch)

        @pl.when(step == total - 1)
        def _():
            for ex in exchanges():
                ex.finish()

    return pl.pallas_call(
        wrapped, name=name, grid=grid, in_specs=list(in_specs) + [HBM_SPEC] * ns,
        out_specs=list(out_specs) + [HBM_SPEC] * ns, out_shape=list(out_shape) + side_shapes,
        scratch_shapes=list(scratch_shapes) + side_sems, compiler_params=params)(*args, *side_args)


def _prenorm_gather(x, g, wie, small, woe, wio, woo, ts):
    S = x.shape[0]
    nt = S // ts

    def body(x_ref, g_ref, wie_ref, sm_ref, woe_ref, wio_ref, woo_ref,
             h_ref, wie_out, sm_out, woe_out, wio_b, woo_b, cast_ref, small_ref, cast2_ref, send_sems, recv_sems):
        step = pl.program_id(0)
        gather = lambda: _Gather([cast_ref, small_ref, cast2_ref], [wie_out, sm_out, woe_out], send_sems, recv_sems)

        @pl.when(step == 0)
        def _():
            cast_ref[...] = wie_ref[...].astype(BF16)
            small_ref[...] = sm_ref[...]
            cast2_ref[...] = woe_ref[...].astype(BF16)
            gather().start()
            wio_b[...] = wio_ref[...].astype(BF16)
            woo_b[...] = woo_ref[...].astype(BF16)

        xv = x_ref[...]
        h_ref[...] = ((xv * _rms(xv)) * g_ref[...]).astype(BF16)

        @pl.when(step == nt - 1)
        def _():
            exchange = gather()
            exchange.middle()
            exchange.finish()

    tile = pl.BlockSpec((ts, D), lambda i: (i, 0))
    sems = pltpu.SemaphoreType.DMA((N_DEV * 3,))
    shards = [wie, small, woe, wio, woo]
    return pl.pallas_call(
        body, name="prenorm_gather", grid=(nt,),
        in_specs=[tile, _full((1, D))] + [_full(a.shape) for a in shards],
        out_specs=[tile, HBM_SPEC, HBM_SPEC, HBM_SPEC] + [_full(a.shape) for a in shards[3:]],
        out_shape=[jax.ShapeDtypeStruct((S, D), BF16), jax.ShapeDtypeStruct((N_DEV,) + wie.shape, BF16),
                   jax.ShapeDtypeStruct((N_DEV,) + small.shape, F32), jax.ShapeDtypeStruct((N_DEV,) + woe.shape, BF16)]
        + [jax.ShapeDtypeStruct(a.shape, BF16) for a in shards[3:]],
        scratch_shapes=[pltpu.VMEM(wie.shape, BF16), pltpu.VMEM(small.shape, F32), pltpu.VMEM(woe.shape, BF16),
                        sems, sems],
        compiler_params=_params(),
    )(x, g, wie, small, woe, wio, woo)


def _matmul(h, w, ts, name, sides=()):
    S = h.shape[0]
    N = w.shape[1]

    def body(h_ref, w_ref, p_ref):
        p_ref[...] = _dot(h_ref[...], w_ref[...]).astype(BF16)

    return _call(
        body, name=name, grid=(S // ts,),
        in_specs=[pl.BlockSpec((ts, D), lambda i: (i, 0)), _full((D, N))],
        out_specs=[pl.BlockSpec((ts, N), lambda i: (i, 0))],
        out_shape=[jax.ShapeDtypeStruct((S, N), BF16)],
        scratch_shapes=[], args=(h, w), sides=sides)


def _even_mixer(proj, halo, row0, cw_ref, pw_ref):
    xa, gb, gc, za, xp, zp = proj
    nr = xa.shape[0]
    ha = gc * xa
    ha_ext = jnp.concatenate([halo[0], ha], axis=0)
    ha_m1 = _shift_down(ha_ext, 1)[HALO:]
    ha_m2 = _shift_down(ha_ext, 2)[HALO:]
    conv = cw_ref[2:3, :] * ha + cw_ref[1:2, :] * ha_m1 + cw_ref[0:1, :] * ha_m2
    sig_a = _sigmoid(za)
    silu_a = za * sig_a

    xp_ext = jnp.concatenate([halo[1], xp], axis=0)
    pos = row0 + lax.broadcasted_iota(jnp.int32, (nr, 1), 0)
    pooled, inv_cnt, mixed = [], [], []
    for g, w in enumerate(POOL_WINDOWS):
        cols = slice(g * GROUP, (g + 1) * GROUP)
        s = xp_ext[:, cols]
        k = 1
        while k < w:
            s = s + _shift_down(s, k)
            k *= 2
        inv = 1.0 / jnp.minimum(pos + 1, w).astype(F32)
        pg = s[HALO:] * inv - xp[:, cols]
        pooled.append(pg)
        inv_cnt.append(inv)
        mixed.append(_dot(pg.astype(BF16), pw_ref[g]))
    mixed = jnp.concatenate(mixed, axis=1)
    sig_b = _sigmoid(zp)
    silu_b = zp * sig_b
    return dict(xa=xa, gb=gb, gc=gc, za=za, xp=xp, zp=zp, ha=ha, ha_m1=ha_m1, ha_m2=ha_m2, conv=conv,
                sig_a=sig_a, silu_a=silu_a, pooled=pooled, inv_cnt=inv_cnt, mixed=mixed, sig_b=sig_b,
                silu_b=silu_b)


def _columns(a):
    return [a[:, k * A_W:(k + 1) * A_W].astype(F32) for k in range(6)]


def _even_forward_fused(h, x, win_b, cw, pw_b, ps, wout_b, gpost, gpre_next, ts, sides=()):
    S = x.shape[0]

    def body(h_ref, x_ref, wi_ref, cw_ref, pw_ref, ps_ref, wo_ref, gp_ref, gn_ref,
             p_ref, m_ref, x1_ref, h1_ref, mix_ref, carry_ref):
        i = pl.program_id(0)

        @pl.when(i == 0)
        def _():
            carry_ref[...] = jnp.zeros_like(carry_ref)

        for r0 in range(0, ts, SUB_ROWS):
            rows = slice(r0, r0 + SUB_ROWS)
            proj = _dot(h_ref[rows, :], wi_ref[...])
            p_ref[rows, :] = proj.astype(BF16)
            fw = _even_mixer(_columns(proj), (carry_ref[:, 0:A_W], carry_ref[:, A_W:D]), i * ts + r0, cw_ref, pw_ref)
            carry_ref[:, 0:A_W] = fw["ha"][SUB_ROWS - HALO:]
            carry_ref[:, A_W:D] = fw["xp"][SUB_ROWS - HALO:]
            mix_ref[rows, 0:A_W] = (fw["gb"] * fw["conv"] * fw["silu_a"]).astype(BF16)
            mix_ref[rows, A_W:D] = (fw["mixed"] * ps_ref[...] * fw["silu_b"]).astype(BF16)
            m = _dot(mix_ref[rows, :], wo_ref[...])
            m_ref[rows, :] = m.astype(BF16)
            x1 = x_ref[rows, :] + (m * _rms(m)) * gp_ref[...]
            x1_ref[rows, :] = x1
            h1_ref[rows, :] = ((x1 * _rms(x1)) * gn_ref[...]).astype(BF16)

    tile = pl.BlockSpec((ts, D), lambda i: (i, 0))
    return _call(
        body, name="even_forward", grid=(S // ts,),
        in_specs=[tile, tile, _resident((D, PROJ)), _full((3, A_W)), _full((N_GROUPS, GROUP, GROUP)), _full((1, B_W)),
                  _resident((D, D)), _full((1, D)), _full((1, D))],
        out_specs=[pl.BlockSpec((ts, PROJ), lambda i: (i, 0)), tile, tile, tile],
        out_shape=[jax.ShapeDtypeStruct((S, PROJ), BF16), jax.ShapeDtypeStruct((S, D), BF16),
                   jax.ShapeDtypeStruct((S, D), F32), jax.ShapeDtypeStruct((S, D), BF16)],
        scratch_shapes=[pltpu.VMEM((ts, D), BF16), pltpu.VMEM((HALO, D), F32)],
        args=(h, x, win_b, cw, pw_b, ps, wout_b, gpost, gpre_next), sides=sides)


def _even_backward_fused(g1, m0, p, x, cw, pw_b, pwt_b, ps, woutt_b, wint_b, gpost, gpre, ts):
    S = g1.shape[0]
    nt = S // ts

    def body(g_ref, m_ref, p_ref, halo_ref, x_ref, cw_ref, pw_ref, pwt_ref, ps_ref, wot_ref, wit_ref, gp_ref, gn_ref,
             dp_ref, dx_ref, dwob_ref, dcw_ref, dpw_ref, dps_ref, dgp_ref, dgn_ref,
             carry_ref, mix_ref, dm_ref, dwo_ref):
        step = pl.program_id(0)
        i = nt - 1 - step

        @pl.when(step == 0)
        def _():
            carry_ref[...] = jnp.zeros_like(carry_ref)
            dwo_ref[...] = jnp.zeros_like(dwo_ref)
            dcw_ref[...] = jnp.zeros_like(dcw_ref)
            dpw_ref[...] = jnp.zeros_like(dpw_ref)
            dps_ref[...] = jnp.zeros_like(dps_ref)
            dgp_ref[...] = jnp.zeros_like(dgp_ref)
            dgn_ref[...] = jnp.zeros_like(dgn_ref)

        ps_v = ps_ref[...]
        sums = [None] * 6
        add = lambda k, part: sums.__setitem__(k, part if sums[k] is None else sums[k] + part)
        for r0 in range(ts - SUB_ROWS, -1, -SUB_ROWS):
            rows = slice(r0, r0 + SUB_ROWS)
            g = g_ref[rows, :]
            m = m_ref[rows, :].astype(F32)
            q = _rms(m)
            n = m * q
            add(0, jnp.sum(g * n, axis=0, keepdims=True))
            dn = g * gp_ref[...]
            dm = q * (dn - n * jnp.mean(dn * n, axis=-1, keepdims=True))
            dm_ref[rows, :] = dm.astype(BF16)
            dmix = _dot(dm_ref[rows, :], wot_ref[...])
            dya = dmix[:, 0:A_W]
            dyb = dmix[:, A_W:D]

            if r0 == 0:
                before = _columns(halo_ref[...])
                keep = jnp.where(i == 0, 0.0, 1.0).astype(F32)
                halo = (before[2] * before[0] * keep, before[4] * keep)
            else:
                before = _columns(p_ref[r0 - HALO:r0, :])
                halo = (before[2] * before[0], before[4])
            fw = _even_mixer(_columns(p_ref[rows, :]), halo, i * ts + r0, cw_ref, pw_ref)
            mix_ref[rows, 0:A_W] = (fw["gb"] * fw["conv"] * fw["silu_a"]).astype(BF16)
            mix_ref[rows, A_W:D] = (fw["mixed"] * ps_v * fw["silu_b"]).astype(BF16)

            t = dya * fw["gb"]
            dconv = t * fw["silu_a"]
            dgb = dya * fw["conv"] * fw["silu_a"]
            dza = t * fw["conv"] * _dsilu(fw["za"], fw["sig_a"])
            add(2, jnp.sum(dconv * fw["ha"], axis=0, keepdims=True))
            add(3, jnp.sum(dconv * fw["ha_m1"], axis=0, keepdims=True))
            add(4, jnp.sum(dconv * fw["ha_m2"], axis=0, keepdims=True))
            dconv_ext = jnp.concatenate([dconv, carry_ref[:, 0:A_W]], axis=0)
            dha = (cw_ref[2:3, :] * dconv + cw_ref[1:2, :] * _shift_up(dconv_ext, 1)[:SUB_ROWS]
                   + cw_ref[0:1, :] * _shift_up(dconv_ext, 2)[:SUB_ROWS])
            dgc = dha * fw["xa"]
            dxa = dha * fw["gc"]

            u = dyb * fw["mixed"]
            add(1, jnp.sum(u * fw["silu_b"], axis=0, keepdims=True))
            dzp = u * ps_v * _dsilu(fw["zp"], fw["sig_b"])
            dmixed = (dyb * ps_v * fw["silu_b"]).astype(BF16)
            dxp, e_first = [], []
            for gi, w in enumerate(POOL_WINDOWS):
                cols = slice(gi * GROUP, (gi + 1) * GROUP)
                dmg = dmixed[:, cols]
                dpooled = _dot(dmg, pwt_ref[gi])
                dpw_ref[gi] += _dot_t0(fw["pooled"][gi].astype(BF16), dmg)
                e = dpooled * fw["inv_cnt"][gi]
                e_first.append(e[0:HALO])
                s = jnp.concatenate([e, carry_ref[:, A_W + gi * GROUP:A_W + (gi + 1) * GROUP]], axis=0)
                k = 1
                while k < w:
                    s = s + _shift_up(s, k)
                    k *= 2
                dxp.append(s[:SUB_ROWS] - dpooled)
            carry_ref[:, 0:A_W] = dconv[0:HALO]
            carry_ref[:, A_W:D] = jnp.concatenate(e_first, axis=1)

            dp_ref[rows, 0:512] = dxa.astype(BF16)
            dp_ref[rows, 512:1024] = dgb.astype(BF16)
            dp_ref[rows, 1024:1536] = dgc.astype(BF16)
            dp_ref[rows, 1536:2048] = dza.astype(BF16)
            dp_ref[rows, 2048:2560] = jnp.concatenate(dxp, axis=1).astype(BF16)
            dp_ref[rows, 2560:3072] = dzp.astype(BF16)

            dh = _dot(dp_ref[rows, :], wit_ref[...])
            xv = x_ref[rows, :]
            r = _rms(xv)
            xn = xv * r
            add(5, jnp.sum(dh * xn, axis=0, keepdims=True))
            dxn = dh * gn_ref[...]
            dx_ref[rows, :] = g + r * (dxn - xn * jnp.mean(dxn * xn, axis=-1, keepdims=True))

        dwo_ref[...] += _dot_t0(mix_ref[...], dm_ref[...])
        dgp_ref[...] += sums[0]
        dps_ref[...] += sums[1]
        dcw_ref[2:3, :] += sums[2]
        dcw_ref[1:2, :] += sums[3]
        dcw_ref[0:1, :] += sums[4]
        dgn_ref[...] += sums[5]

        @pl.when(step == nt - 1)
        def _():
            dwob_ref[...] = dwo_ref[...].astype(BF16)

    rev = lambda s: (nt - 1 - s, 0)
    tile = pl.BlockSpec((ts, D), rev)
    wide = pl.BlockSpec((ts, PROJ), rev)
    bpt = ts // HALO
    halo_map = lambda s: (jnp.maximum((nt - 1 - s) * bpt - 1, 0), 0)
    groups = _full((N_GROUPS, GROUP, GROUP))
    vec = _full((1, D))
    return pl.pallas_call(
        body, name="even_backward", grid=(nt,),
        in_specs=[tile, tile, wide, pl.BlockSpec((HALO, PROJ), halo_map), tile, _full((3, A_W)), groups, groups,
                  _full((1, B_W)), _resident((D, D)), _resident((PROJ, D)), vec, vec],
        out_specs=[wide, tile, _full((D, D)), _full((8, A_W)), groups, _full((1, B_W)), vec, vec],
        out_shape=[jax.ShapeDtypeStruct((S, PROJ), BF16), jax.ShapeDtypeStruct((S, D), F32),
                   jax.ShapeDtypeStruct((D, D), BF16), jax.ShapeDtypeStruct((8, A_W), F32),
                   jax.ShapeDtypeStruct((N_GROUPS, GROUP, GROUP), F32), jax.ShapeDtypeStruct((1, B_W), F32),
                   jax.ShapeDtypeStruct((1, D), F32), jax.ShapeDtypeStruct((1, D), F32)],
        scratch_shapes=[pltpu.VMEM((HALO, D), F32), pltpu.VMEM((ts, D), BF16), pltpu.VMEM((ts, D), BF16),
                        pltpu.VMEM((D, D), F32)],
        compiler_params=_params(),
    )(g1, m0, p, p, x, cw, pw_b, pwt_b, ps, woutt_b, wint_b, gpost, gpre)


def _even_forward_parts(p_ref, halo_ref, is_first, row0, cw_ref, pw_ref, ts):
    f = lambda a: a.astype(F32)
    xa = f(p_ref[:, 0:512])
    gb = f(p_ref[:, 512:1024])
    gc = f(p_ref[:, 1024:1536])
    za = f(p_ref[:, 1536:2048])
    xp = f(p_ref[:, 2048:2560])
    zp = f(p_ref[:, 2560:3072])
    keep = jnp.where(is_first, 0.0, 1.0).astype(F32)
    ha = gc * xa
    ha_halo = f(halo_ref[:, 1024:1536]) * f(halo_ref[:, 0:512]) * keep
    xp_halo = f(halo_ref[:, 2048:2560]) * keep
    ha_ext = jnp.concatenate([ha_halo, ha], axis=0)
    ha_m1 = _shift_down(ha_ext, 1)[HALO:]
    ha_m2 = _shift_down(ha_ext, 2)[HALO:]
    conv = cw_ref[2:3, :] * ha + cw_ref[1:2, :] * ha_m1 + cw_ref[0:1, :] * ha_m2
    sig_a = _sigmoid(za)
    silu_a = za * sig_a

    xp_ext = jnp.concatenate([xp_halo, xp], axis=0)
    pos = row0 + lax.broadcasted_iota(jnp.int32, (ts, 1), 0)
    pooled, inv_cnt, mixed = [], [], []
    for g, w in enumerate(POOL_WINDOWS):
        cols = slice(g * GROUP, (g + 1) * GROUP)
        s = xp_ext[:, cols]
        k = 1
        while k < w:
            s = s + _shift_down(s, k)
            k *= 2
        inv = 1.0 / jnp.minimum(pos + 1, w).astype(F32)
        pg = s[HALO:] * inv - xp[:, cols]
        pooled.append(pg)
        inv_cnt.append(inv)
        mixed.append(_dot(pg.astype(BF16), pw_ref[g]))
    mixed = jnp.concatenate(mixed, axis=1)
    sig_b = _sigmoid(zp)
    silu_b = zp * sig_b
    return dict(xa=xa, gb=gb, gc=gc, za=za, xp=xp, zp=zp, ha=ha, ha_m1=ha_m1, ha_m2=ha_m2, conv=conv,
                sig_a=sig_a, silu_a=silu_a, pooled=pooled, inv_cnt=inv_cnt, mixed=mixed, sig_b=sig_b,
                silu_b=silu_b)


def _halo_index(ts):
    blocks_per_tile = ts // HALO
    return lambda i: (jnp.maximum(i * blocks_per_tile - 1, 0), 0)


def _even_forward(p, x, cw, pw_b, ps, wout_b, gpost, gpre_next, ts, sides=()):
    S = x.shape[0]

    def body(p_ref, halo_ref, x_ref, cw_ref, pw_ref, ps_ref, wo_ref, gp_ref, gn_ref, m_ref, x1_ref, h1_ref, mix_ref):
        i = pl.program_id(0)
        fw = _even_forward_parts(p_ref, halo_ref, i == 0, i * ts, cw_ref, pw_ref, ts)
        mix_ref[:, 0:A_W] = (fw["gb"] * fw["conv"] * fw["silu_a"]).astype(BF16)
        mix_ref[:, A_W:D] = (fw["mixed"] * ps_ref[...] * fw["silu_b"]).astype(BF16)
        m = _dot(mix_ref[...], wo_ref[...])
        m_ref[...] = m
        x1 = x_ref[...] + (m * _rms(m)) * gp_ref[...]
        x1_ref[...] = x1
        h1_ref[...] = ((x1 * _rms(x1)) * gn_ref[...]).astype(BF16)

    tile = pl.BlockSpec((ts, D), lambda i: (i, 0))
    return _call(
        body, name="even_forward", grid=(S // ts,),
        in_specs=[pl.BlockSpec((ts, PROJ), lambda i: (i, 0)), pl.BlockSpec((HALO, PROJ), _halo_index(ts)), tile,
                  _full((3, A_W)), _full((N_GROUPS, GROUP, GROUP)), _full((1, B_W)), _full((D, D)), _full((1, D)),
                  _full((1, D))],
        out_specs=[tile, tile, tile],
        out_shape=[jax.ShapeDtypeStruct((S, D), F32), jax.ShapeDtypeStruct((S, D), F32),
                   jax.ShapeDtypeStruct((S, D), BF16)],
        scratch_shapes=[pltpu.VMEM((ts, D), BF16)],
        args=(p, p, x, cw, pw_b, ps, wout_b, gpost, gpre_next), sides=sides)


def _even_backward(g1, m0, p, cw, pw_b, pwt_b, ps, woutt_b, gpost, ts, sides=()):
    S = g1.shape[0]
    nt = S // ts

    def body(g_ref, m_ref, p_ref, halo_ref, cw_ref, pw_ref, pwt_ref, ps_ref, wot_ref, gp_ref,
             dp_ref, dwob_ref, dcw_ref, dpw_ref, dps_ref, dgp_ref, carry_ref, mix_ref, dm_ref, dwo_ref):
        step = pl.program_id(0)
        i = nt - 1 - step

        @pl.when(step == 0)
        def _():
            carry_ref[...] = jnp.zeros_like(carry_ref)
            dwo_ref[...] = jnp.zeros_like(dwo_ref)
            dcw_ref[...] = jnp.zeros_like(dcw_ref)
            dpw_ref[...] = jnp.zeros_like(dpw_ref)
            dps_ref[...] = jnp.zeros_like(dps_ref)
            dgp_ref[...] = jnp.zeros_like(dgp_ref)

        g = g_ref[...]
        m = m_ref[...]
        q = _rms(m)
        n = m * q
        dgp_ref[...] += jnp.sum(g * n, axis=0, keepdims=True)
        dn = g * gp_ref[...]
        dm = q * (dn - n * jnp.mean(dn * n, axis=-1, keepdims=True))
        dm_ref[...] = dm.astype(BF16)
        dmix = _dot(dm_ref[...], wot_ref[...])
        dya = dmix[:, 0:A_W]
        dyb = dmix[:, A_W:D]

        fw = _even_forward_parts(p_ref, halo_ref, i == 0, i * ts, cw_ref, pw_ref, ts)
        ps_v = ps_ref[...]
        mix_ref[:, 0:A_W] = (fw["gb"] * fw["conv"] * fw["silu_a"]).astype(BF16)
        mix_ref[:, A_W:D] = (fw["mixed"] * ps_v * fw["silu_b"]).astype(BF16)
        dwo_ref[...] += _dot_t0(mix_ref[...], dm_ref[...])

        t = dya * fw["gb"]
        dconv = t * fw["silu_a"]
        dgb = dya * fw["conv"] * fw["silu_a"]
        dza = t * fw["conv"] * _dsilu(fw["za"], fw["sig_a"])
        dcw_ref[2:3, :] += jnp.sum(dconv * fw["ha"], axis=0, keepdims=True)
        dcw_ref[1:2, :] += jnp.sum(dconv * fw["ha_m1"], axis=0, keepdims=True)
        dcw_ref[0:1, :] += jnp.sum(dconv * fw["ha_m2"], axis=0, keepdims=True)
        dconv_ext = jnp.concatenate([dconv, carry_ref[:, 0:A_W]], axis=0)
        dha = (cw_ref[2:3, :] * dconv + cw_ref[1:2, :] * _shift_up(dconv_ext, 1)[:ts]
               + cw_ref[0:1, :] * _shift_up(dconv_ext, 2)[:ts])
        dgc = dha * fw["xa"]
        dxa = dha * fw["gc"]

        u = dyb * fw["mixed"]
        dps_ref[...] += jnp.sum(u * fw["silu_b"], axis=0, keepdims=True)
        dzp = u * ps_v * _dsilu(fw["zp"], fw["sig_b"])
        dmixed = (dyb * ps_v * fw["silu_b"]).astype(BF16)
        dxp, e_first = [], []
        for gi, w in enumerate(POOL_WINDOWS):
            cols = slice(gi * GROUP, (gi + 1) * GROUP)
            dmg = dmixed[:, cols]
            dpooled = _dot(dmg, pwt_ref[gi])
            dpw_ref[gi] += _dot_t0(fw["pooled"][gi].astype(BF16), dmg)
            e = dpooled * fw["inv_cnt"][gi]
            e_first.append(e[0:HALO])
            s = jnp.concatenate([e, carry_ref[:, A_W + gi * GROUP:A_W + (gi + 1) * GROUP]], axis=0)
            k = 1
            while k < w:
                s = s + _shift_up(s, k)
                k *= 2
            dxp.append(s[:ts] - dpooled)

        carry_ref[:, 0:A_W] = dconv[0:HALO]
        carry_ref[:, A_W:D] = jnp.concatenate(e_first, axis=1)

        dp_ref[:, 0:512] = dxa.astype(BF16)
        dp_ref[:, 512:1024] = dgb.astype(BF16)
        dp_ref[:, 1024:1536] = dgc.astype(BF16)
        dp_ref[:, 1536:2048] = dza.astype(BF16)
        dp_ref[:, 2048:2560] = jnp.concatenate(dxp, axis=1).astype(BF16)
        dp_ref[:, 2560:3072] = dzp.astype(BF16)

        @pl.when(step == nt - 1)
        def _():
            dwob_ref[...] = dwo_ref[...].astype(BF16)

    rev = lambda s: (nt - 1 - s, 0)
    tile = pl.BlockSpec((ts, D), rev)
    bpt = ts // HALO
    halo_map = lambda s: (jnp.maximum((nt - 1 - s) * bpt - 1, 0), 0)
    return _call(
        body, name="even_backward", grid=(nt,),
        in_specs=[tile, tile, pl.BlockSpec((ts, PROJ), rev), pl.BlockSpec((HALO, PROJ), halo_map),
                  _full((3, A_W)), _full((N_GROUPS, GROUP, GROUP)), _full((N_GROUPS, GROUP, GROUP)),
                  _full((1, B_W)), _full((D, D)), _full((1, D))],
        out_specs=[pl.BlockSpec((ts, PROJ), rev), _full((D, D)), _full((8, A_W)),
                   _full((N_GROUPS, GROUP, GROUP)), _full((1, B_W)), _full((1, D))],
        out_shape=[jax.ShapeDtypeStruct((S, PROJ), BF16), jax.ShapeDtypeStruct((D, D), BF16),
                   jax.ShapeDtypeStruct((8, A_W), F32), jax.ShapeDtypeStruct((N_GROUPS, GROUP, GROUP), F32),
                   jax.ShapeDtypeStruct((1, B_W), F32), jax.ShapeDtypeStruct((1, D), F32)],
        scratch_shapes=[pltpu.VMEM((HALO, D), F32), pltpu.VMEM((ts, D), BF16), pltpu.VMEM((ts, D), BF16),
                        pltpu.VMEM((D, D), F32)],
        args=(g1, m0, p, p, cw, pw_b, pwt_b, ps, woutt_b, gpost), sides=sides)


def _odd_forward_parts(v, lg_ref, lb_ref, wt_ref, bfull_ref, vln_ref, sv_ref, r0, nr):
    rows = slice(r0, r0 + nr)
    mu = jnp.mean(v, axis=-1, keepdims=True)
    vc = v - mu
    rstd = lax.rsqrt(jnp.mean(vc * vc, axis=-1, keepdims=True) + EPS)
    vh = vc * rstd
    vln_ref[rows, :] = (vh * lg_ref[...] + lb_ref[...]).astype(BF16)
    for c in range(nr // GROUP):
        chunk = slice(r0 + c * GROUP, r0 + (c + 1) * GROUP)
        for h in range(N_HEADS):
            cols = slice(h * GROUP, (h + 1) * GROUP)
            sv_ref[chunk, cols] = _dot(wt_ref[h], vln_ref[chunk, cols]) + bfull_ref[h]
    return vh, rstd


SUB_ROWS = 256


def _odd_forward(h, x1, tgt, win_b, lg, lb, wt_b, bfull, wout_b, gpost, ts):
    S = x1.shape[0]

    def body(h_ref, x_ref, t_ref, wi_ref, lg_ref, lb_ref, wt_ref, bfull_ref, wo_ref, gp_ref,
             p_ref, m_ref, g_ref, loss_ref, vln_ref, sv_ref, y_ref):
        @pl.when(pl.program_id(0) == 0)
        def _():
            loss_ref[...] = jnp.zeros_like(loss_ref)

        loss = None
        for r0 in range(0, ts, SUB_ROWS):
            rows = slice(r0, r0 + SUB_ROWS)
            proj = _dot(h_ref[rows, :], wi_ref[...])
            p_ref[rows, :] = proj.astype(BF16)
            u, v, z = proj[:, 0:D], proj[:, D:2 * D], proj[:, 2 * D:3 * D]
            _odd_forward_parts(v, lg_ref, lb_ref, wt_ref, bfull_ref, vln_ref, sv_ref, r0, SUB_ROWS)
            y_ref[rows, :] = (u * sv_ref[rows, :] * (z * _sigmoid(z))).astype(BF16)
            m = _dot(y_ref[rows, :], wo_ref[...])
            m_ref[rows, :] = m.astype(BF16)
            x2 = x_ref[rows, :] + (m * _rms(m)) * gp_ref[...]
            err = x2 - t_ref[rows, :]
            g_ref[rows, :] = err * (1.0 / D)
            part = jnp.sum(err * err, axis=0, keepdims=True)
            loss = part if loss is None else loss + part
        loss_ref[...] += loss

    tile = pl.BlockSpec((ts, D), lambda i: (i, 0))
    small = _full((N_HEADS, GROUP, GROUP))
    return pl.pallas_call(
        body, name="odd_forward", grid=(S // ts,),
        in_specs=[pl.BlockSpec((ts, D), lambda i: (i, 0)), tile, tile, _full((D, PROJ)), _full((1, D)), _full((1, D)),
                  small, small, _full((D, D)), _full((1, D))],
        out_specs=[pl.BlockSpec((ts, PROJ), lambda i: (i, 0)), tile, tile, _full((1, D))],
        out_shape=[jax.ShapeDtypeStruct((S, PROJ), BF16), jax.ShapeDtypeStruct((S, D), BF16),
                   jax.ShapeDtypeStruct((S, D), F32), jax.ShapeDtypeStruct((1, D), F32)],
        scratch_shapes=[pltpu.VMEM((ts, D), BF16), pltpu.VMEM((ts, D), F32), pltpu.VMEM((ts, D), BF16)],
        compiler_params=_params(),
    )(h, x1, tgt, win_b, lg, lb, wt_b, bfull, wout_b, gpost)


def _resident(shape):
    return pl.BlockSpec(shape, lambda *_: (0,) * len(shape), pipeline_mode=pl.Buffered(1))


def _odd_backward(g2, m1, p, x1, lg, lb, wt_b, wtt_b, bfull, woutt_b, wint_b, gpost, gpre, ts):
    S = g2.shape[0]
    nt = S // ts
    sub = ts // 2

    def body(g_ref, m_ref, p_ref, x_ref, lg_ref, lb_ref, wt_ref, wtt_ref, bfull_ref, wot_ref, wit_ref, gp_ref, gn_ref,
             dp_ref, dx_ref, dwob_ref, dws_ref, dbs_ref, dlg_ref, dlb_ref, dgp_ref, dgn_ref,
             vln_ref, sv_ref, dsvb_ref, dvln_ref, dsum_ref, y_ref, dm_ref, dwo_ref):
        step = pl.program_id(0)

        @pl.when(step == 0)
        def _():
            dwo_ref[...] = jnp.zeros_like(dwo_ref)
            dws_ref[...] = jnp.zeros_like(dws_ref)
            dsum_ref[...] = jnp.zeros_like(dsum_ref)
            dlg_ref[...] = jnp.zeros_like(dlg_ref)
            dlb_ref[...] = jnp.zeros_like(dlb_ref)
            dgp_ref[...] = jnp.zeros_like(dgp_ref)
            dgn_ref[...] = jnp.zeros_like(dgn_ref)

        tril = (lax.broadcasted_iota(jnp.int32, (GROUP, GROUP), 0)
                >= lax.broadcasted_iota(jnp.int32, (GROUP, GROUP), 1))
        sums = [None] * 4
        add = lambda k, part: sums.__setitem__(k, part if sums[k] is None else sums[k] + part)
        def post_norm(c):
            gp = gp_ref[...]
            for rows in c["pieces"]:
                g = g_ref[rows, :]
                m = m_ref[rows, :].astype(F32)
                q = _rms(m)
                n = m * q
                add(0, _fold(g * n))
                dn = g * gp
                dm = q * (dn - n * jnp.mean(dn * n, axis=-1, keepdims=True))
                dm_ref[rows, :] = dm.astype(BF16)

        def out_projection(c):
            c["dy"] = _dot(dm_ref[c["rows"], :], wot_ref[...])

        def layernorm(c):
            v = p_ref[c["rows"], D:2 * D].astype(F32)
            mu = jnp.mean(v, axis=-1, keepdims=True)
            vc = v - mu
            c["rstd"] = lax.rsqrt(jnp.mean(vc * vc, axis=-1, keepdims=True) + EPS)
            c["vh"] = vc * c["rstd"]
            vln_ref[c["rows"], :] = (c["vh"] * lg_ref[...] + lb_ref[...]).astype(BF16)

        def gate_matmuls(c):
            for chunk in c["chunks"]:
                for h in range(N_HEADS):
                    cols = slice(h * GROUP, (h + 1) * GROUP)
                    sv_ref[chunk, cols] = _dot(wt_ref[h], vln_ref[chunk, cols]) + bfull_ref[h]

        def gating(c):
            rows = c["rows"]
            u = p_ref[rows, 0:D].astype(F32)
            z = p_ref[rows, 2 * D:3 * D].astype(F32)
            sv = sv_ref[rows, :]
            sig = _sigmoid(z)
            sz = z * sig
            y_ref[rows, :] = (u * sv * sz).astype(BF16)
            dy = c.pop("dy")
            t = dy * sz
            dsv = t * u
            dsvb_ref[rows, :] = dsv.astype(BF16)
            for k in range(sub // GROUP):
                dsum_ref[...] += dsv[k * GROUP:(k + 1) * GROUP]
            dp_ref[rows, 0:D] = (t * sv).astype(BF16)
            dp_ref[rows, 2 * D:3 * D] = (dy * u * sv * _dsilu(z, sig)).astype(BF16)

        def gate_backward_matmuls(c):
            for chunk in c["chunks"]:
                for h in range(N_HEADS):
                    cols = slice(h * GROUP, (h + 1) * GROUP)
                    dvln_ref[chunk, cols] = _dot(wtt_ref[h], dsvb_ref[chunk, cols])
                    dws_ref[h] += jnp.where(tril, _dot_t1(dsvb_ref[chunk, cols], vln_ref[chunk, cols]), 0.0)

        def layernorm_backward(c):
            vh, rstd = c.pop("vh"), c.pop("rstd")
            dvln = dvln_ref[c["rows"], :]
            add(1, jnp.sum(dvln * vh, axis=0, keepdims=True))
            add(2, jnp.sum(dvln, axis=0, keepdims=True))
            dvh = dvln * lg_ref[...]
            dv = rstd * (dvh - jnp.mean(dvh, axis=-1, keepdims=True)
                         - vh * jnp.mean(dvh * vh, axis=-1, keepdims=True))
            dp_ref[c["rows"], D:2 * D] = dv.astype(BF16)

        def in_projection(c):
            c["dh"] = _dot(dp_ref[c["rows"], :], wit_ref[...])

        def pre_norm(c):
            dh_all = c.pop("dh")
            gn = gn_ref[...]
            for k, rows in enumerate(c["pieces"]):
                dh = dh_all[k * PIECE:(k + 1) * PIECE]
                xv = x_ref[rows, :]
                r = _rms(xv)
                xn = xv * r
                add(3, _fold(dh * xn))
                dxn = dh * gn
                dx_ref[rows, :] = g_ref[rows, :] + r * (dxn - xn * jnp.mean(dxn * xn, axis=-1, keepdims=True))

        phases = [post_norm, out_projection, layernorm, gate_matmuls, gating, gate_backward_matmuls,
                  layernorm_backward, in_projection, pre_norm]
        groups = [dict(rows=slice(r0, r0 + sub),
                       chunks=[slice(r0 + k * GROUP, r0 + (k + 1) * GROUP) for k in range(sub // GROUP)],
                       pieces=[slice(r0 + k, r0 + k + PIECE) for k in range(0, sub, PIECE)])
                  for r0 in range(0, ts, sub)]
        for group in groups:
            for phase in phases:
                phase(group)

        dwo_ref[...] += _dot_t0(y_ref[...], dm_ref[...])
        dgp_ref[...] += jnp.sum(sums[0], axis=0, keepdims=True)
        dlg_ref[...] += jnp.sum(sums[1], axis=0, keepdims=True)
        dlb_ref[...] += jnp.sum(sums[2], axis=0, keepdims=True)
        dgn_ref[...] += jnp.sum(sums[3], axis=0, keepdims=True)

        @pl.when(step == nt - 1)
        def _():
            dwob_ref[...] = dwo_ref[...].astype(BF16)
            ones = jnp.ones((8, GROUP), F32)
            for h in range(N_HEADS):
                cols = slice(h * GROUP, (h + 1) * GROUP)
                total = lax.dot_general(ones, dsum_ref[:, cols], (((1,), (1,)), ((), ())),
                                        precision=lax.Precision.HIGHEST, preferred_element_type=F32)
                dbs_ref[h:h + 1, :] = total[0:1, :]

    tile = pl.BlockSpec((ts, D), lambda i: (i, 0))
    wide = pl.BlockSpec((ts, PROJ), lambda i: (i, 0))
    small = _full((N_HEADS, GROUP, GROUP))
    heads = _resident((N_HEADS, GROUP, GROUP))
    vec = _full((1, D))
    return pl.pallas_call(
        body, name="odd_backward", grid=(nt,),
        in_specs=[tile, tile, wide, tile, vec, vec, heads, heads, heads, _resident((D, D)), _resident((PROJ, D)),
                  vec, vec],
        out_specs=[wide, tile, _full((D, D)), small, _full((N_HEADS, GROUP)), vec, vec, vec, vec],
        out_shape=[jax.ShapeDtypeStruct((S, PROJ), BF16), jax.ShapeDtypeStruct((S, D), F32),
                   jax.ShapeDtypeStruct((D, D), BF16),
                   jax.ShapeDtypeStruct((N_HEADS, GROUP, GROUP), F32), jax.ShapeDtypeStruct((N_HEADS, GROUP), F32),
                   jax.ShapeDtypeStruct((1, D), F32), jax.ShapeDtypeStruct((1, D), F32),
                   jax.ShapeDtypeStruct((1, D), F32), jax.ShapeDtypeStruct((1, D), F32)],
        scratch_shapes=[pltpu.VMEM((ts, D), BF16), pltpu.VMEM((ts, D), F32), pltpu.VMEM((ts, D), BF16),
                        pltpu.VMEM((ts, D), F32), pltpu.VMEM((GROUP, D), F32), pltpu.VMEM((ts, D), BF16),
                        pltpu.VMEM((ts, D), BF16), pltpu.VMEM((D, D), F32)],
        compiler_params=_params(),
    )(g2, m1, p, x1, lg, lb, wt_b, wtt_b, bfull, woutt_b, wint_b, gpost, gpre)


def _odd_backward_unfused(g2, m1, p, lg, lb, wt_b, wtt_b, bfull, woutt_b, gpost, ts):
    S = g2.shape[0]
    nt = S // ts

    def body(g_ref, m_ref, p_ref, lg_ref, lb_ref, wt_ref, wtt_ref, bfull_ref, wot_ref, gp_ref,
             dp_ref, dwob_ref, dws_ref, dbs_ref, dlg_ref, dlb_ref, dgp_ref,
             vln_ref, sv_ref, y_ref, dm_ref, dsv_ref, dsvb_ref, dvln_ref, dsum_ref, dwo_ref):
        step = pl.program_id(0)

        @pl.when(step == 0)
        def _():
            dwo_ref[...] = jnp.zeros_like(dwo_ref)
            dws_ref[...] = jnp.zeros_like(dws_ref)
            dsum_ref[...] = jnp.zeros_like(dsum_ref)
            dlg_ref[...] = jnp.zeros_like(dlg_ref)
            dlb_ref[...] = jnp.zeros_like(dlb_ref)
            dgp_ref[...] = jnp.zeros_like(dgp_ref)

        g = g_ref[...]
        m = m_ref[...]
        q = _rms(m)
        n = m * q
        dgp_ref[...] += jnp.sum(g * n, axis=0, keepdims=True)
        dn = g * gp_ref[...]
        dm = q * (dn - n * jnp.mean(dn * n, axis=-1, keepdims=True))
        dm_ref[...] = dm.astype(BF16)
        dy = _dot(dm_ref[...], wot_ref[...])

        u = p_ref[:, 0:D].astype(F32)
        z = p_ref[:, 2 * D:3 * D].astype(F32)
        vh, rstd = _odd_forward_parts(p_ref[:, D:2 * D].astype(F32), lg_ref, lb_ref, wt_ref, bfull_ref, vln_ref,
                                      sv_ref, 0, ts)
        sv = sv_ref[...]
        sig = _sigmoid(z)
        sz = z * sig
        y_ref[...] = (u * sv * sz).astype(BF16)
        dwo_ref[...] += _dot_t0(y_ref[...], dm_ref[...])

        t = dy * sz
        du = t * sv
        dsv = t * u
        dz = dy * u * sv * _dsilu(z, sig)
        dsv_ref[...] = dsv
        dsvb_ref[...] = dsv.astype(BF16)

        tril = (lax.broadcasted_iota(jnp.int32, (GROUP, GROUP), 0)
                >= lax.broadcasted_iota(jnp.int32, (GROUP, GROUP), 1))
        for c in range(ts // GROUP):
            rows = slice(c * GROUP, (c + 1) * GROUP)
            dsum_ref[...] += dsv_ref[rows, :]
            for h in range(N_HEADS):
                cols = slice(h * GROUP, (h + 1) * GROUP)
                dvln_ref[rows, cols] = _dot(wtt_ref[h], dsvb_ref[rows, cols])
                dws_ref[h] += jnp.where(tril, _dot_t1(dsvb_ref[rows, cols], vln_ref[rows, cols]), 0.0)

        dvln = dvln_ref[...]
        dlg_ref[...] += jnp.sum(dvln * vh, axis=0, keepdims=True)
        dlb_ref[...] += jnp.sum(dvln, axis=0, keepdims=True)
        dvh = dvln * lg_ref[...]
        dv = rstd * (dvh - jnp.mean(dvh, axis=-1, keepdims=True)
                     - vh * jnp.mean(dvh * vh, axis=-1, keepdims=True))
        dp_ref[:, 0:D] = du.astype(BF16)
        dp_ref[:, D:2 * D] = dv.astype(BF16)
        dp_ref[:, 2 * D:3 * D] = dz.astype(BF16)

        @pl.when(step == nt - 1)
        def _():
            dwob_ref[...] = dwo_ref[...].astype(BF16)
            ones = jnp.ones((8, GROUP), F32)
            for h in range(N_HEADS):
                cols = slice(h * GROUP, (h + 1) * GROUP)
                sums = lax.dot_general(ones, dsum_ref[:, cols], (((1,), (1,)), ((), ())),
                                       precision=lax.Precision.HIGHEST, preferred_element_type=F32)
                dbs_ref[h:h + 1, :] = sums[0:1, :]

    tile = pl.BlockSpec((ts, D), lambda i: (i, 0))
    small = _full((N_HEADS, GROUP, GROUP))
    vec = _full((1, D))
    return pl.pallas_call(
        body, name="odd_backward", grid=(nt,),
        in_specs=[tile, tile, pl.BlockSpec((ts, PROJ), lambda i: (i, 0)), vec, vec, small, small, small,
                  _full((D, D)), vec],
        out_specs=[pl.BlockSpec((ts, PROJ), lambda i: (i, 0)), _full((D, D)), small, _full((N_HEADS, GROUP)),
                   vec, vec, vec],
        out_shape=[jax.ShapeDtypeStruct((S, PROJ), BF16), jax.ShapeDtypeStruct((D, D), BF16),
                   jax.ShapeDtypeStruct((N_HEADS, GROUP, GROUP), F32), jax.ShapeDtypeStruct((N_HEADS, GROUP), F32),
                   jax.ShapeDtypeStruct((1, D), F32), jax.ShapeDtypeStruct((1, D), F32),
                   jax.ShapeDtypeStruct((1, D), F32)],
        scratch_shapes=[pltpu.VMEM((ts, D), BF16), pltpu.VMEM((ts, D), F32), pltpu.VMEM((ts, D), BF16),
                        pltpu.VMEM((ts, D), BF16), pltpu.VMEM((ts, D), F32), pltpu.VMEM((ts, D), BF16),
                        pltpu.VMEM((ts, D), F32), pltpu.VMEM((GROUP, D), F32), pltpu.VMEM((D, D), F32)],
        compiler_params=_params(),
    )(g2, m1, p, lg, lb, wt_b, wtt_b, bfull, woutt_b, gpost)


def _input_backward(dp, wt_b, x, g_res, gpre, ts, name, sides=()):
    S = x.shape[0]

    def body(dp_ref, wt_ref, x_ref, g_ref, gp_ref, dx_ref, dgp_ref):
        @pl.when(pl.program_id(0) == 0)
        def _():
            dgp_ref[...] = jnp.zeros_like(dgp_ref)

        dgp = None
        for r0 in range(0, ts, SUB_ROWS):
            rows = slice(r0, r0 + SUB_ROWS)
            dh = _dot(dp_ref[rows, :], wt_ref[...])
            xv = x_ref[rows, :]
            r = _rms(xv)
            xn = xv * r
            part = jnp.sum(dh * xn, axis=0, keepdims=True)
            dgp = part if dgp is None else dgp + part
            dxn = dh * gp_ref[...]
            dx_ref[rows, :] = g_ref[rows, :] + r * (dxn - xn * jnp.mean(dxn * xn, axis=-1, keepdims=True))
        dgp_ref[...] += dgp

    tile = pl.BlockSpec((ts, D), lambda i: (i, 0))
    return _call(
        body, name=name, grid=(S // ts,),
        in_specs=[pl.BlockSpec((ts, PROJ), lambda i: (i, 0)), _full((PROJ, D)), tile, tile, _full((1, D))],
        out_specs=[tile, _full((1, D))],
        out_shape=[jax.ShapeDtypeStruct((S, D), F32), jax.ShapeDtypeStruct((1, D), F32)],
        scratch_shapes=[], args=(dp, wt_b, x, g_res, gpre), sides=sides)


N_CHIPS = 4


def _weight_grad_scatter(h, dp, ts, name, sides=()):
    S = h.shape[0]
    ts = min(ts, S)
    nt = S // ts
    panel = 2 * SHARD_IN
    last = N_CHIPS - 1

    def body(h_ref, dp_ref, parts_ref, acc_ref, half_ref, swap_ref, sum_ref, send_sems, recv_sems):
        s, t = pl.program_id(0), pl.program_id(1)
        x, y, c = _mesh_position()
        sibling = (x, y, 1 - c)

        def swap(k):
            return _remote(half_ref.at[k, 1], swap_ref.at[k], send_sems, recv_sems, k, sibling)

        def chip_sum(k):
            flip = last - k
            to = (x ^ (flip >> 1), y ^ (flip & 1), c)
            return _remote(sum_ref.at[k], parts_ref.at[1 + flip], send_sems, recv_sems, 3 + flip, to)

        own = pltpu.make_async_copy(half_ref.at[last, 0], parts_ref.at[0], recv_sems.at[7])
        to_sibling = _remote(half_ref.at[last, 1], parts_ref.at[1], send_sems, recv_sems, 3, sibling)

        @pl.when(t == 0)
        def _():
            acc_ref[...] = jnp.zeros_like(acc_ref)

        for k in range(last):
            @pl.when((s == k + 1) & (t == 0))
            def _():
                swap(k).wait_recv()
                sum_ref[k] = (half_ref[k, 0].astype(F32) + swap_ref[k].astype(F32)).astype(BF16)
                chip_sum(k).start()

        acc_ref[...] += _dot_t0(h_ref[...], dp_ref[...])

        @pl.when(t == nt - 1)
        def _():
            for core in range(2):
                @pl.when(c == core)
                def _():
                    half_ref[s, core] = acc_ref[:, 0:SHARD_IN].astype(BF16)
                    half_ref[s, 1 - core] = acc_ref[:, SHARD_IN:panel].astype(BF16)
            for k in range(last):
                @pl.when(s == k)
                def _():
                    swap(k).start()

        @pl.when((s == last) & (t == nt - 1))
        def _():
            own.start()
            to_sibling.start()
            own.wait()
            _remote(half_ref.at[last, 1], parts_ref.at[1], send_sems, recv_sems, 3, sibling).wait_recv()
            for k in range(last):
                chip_sum(k).wait_recv()
            to_sibling.wait_send()
            for k in range(last):
                swap(k).wait_send()
                chip_sum(k).wait_send()

    chip_panel = lambda s, t: (t, (2 * lax.axis_index("x") + lax.axis_index("y")) ^ (last - s))
    sems = pltpu.SemaphoreType.DMA((8,))
    return _call(
        body, name=name, grid=(N_CHIPS, nt),
        in_specs=[pl.BlockSpec((ts, D), lambda s, t: (t, 0)), pl.BlockSpec((ts, panel), chip_panel)],
        out_specs=[HBM_SPEC], out_shape=[jax.ShapeDtypeStruct((N_CHIPS + 1, D, SHARD_IN), BF16)],
        scratch_shapes=[pltpu.VMEM((D, panel), F32), pltpu.VMEM((N_CHIPS, 2, D, SHARD_IN), BF16),
                        pltpu.VMEM((last, D, SHARD_IN), BF16), pltpu.VMEM((last, D, SHARD_IN), BF16), sems, sems],
        args=(h, dp), sides=sides)


def _adamw(w, g, m, v):
    m = ADAM_B1 * m + (1.0 - ADAM_B1) * g
    v = ADAM_B2 * v + (1.0 - ADAM_B2) * (g * g)
    m_hat = m / (1.0 - ADAM_B1 ** ADAM_STEP)
    v_hat = v / (1.0 - ADAM_B2 ** ADAM_STEP)
    delta = -ADAM_LR * (m_hat / (jnp.sqrt(v_hat) + ADAM_EPS) + ADAM_WD * w)
    return delta, m, v


def _sum_update(parts, w, m, v, name):
    R, C = w.shape
    rb = min(R, 256)
    n_parts = min(parts.shape[0], N_DEV)
    if parts.shape[0] == LANDING:
        n_parts = PARTS

    def body(p_ref, w_ref, m_ref, v_ref, g_out, d_out, m_out, v_out):
        acc = p_ref[0].astype(F32)
        for d in range(1, n_parts):
            acc = acc + p_ref[d].astype(F32)
        g_out[...] = acc
        delta, m_new, v_new = _adamw(w_ref[...], acc, m_ref[...], v_ref[...])
        d_out[...] = delta
        m_out[...] = m_new
        v_out[...] = v_new

    blk = pl.BlockSpec((rb, C), lambda i: (i, 0))
    return pl.pallas_call(
        body, name=name, grid=(R // rb,),
        in_specs=[pl.BlockSpec((n_parts, rb, C), lambda i: (0, i, 0)), blk, blk, blk], out_specs=[blk] * 4,
        out_shape=[jax.ShapeDtypeStruct((R, C), F32)] * 4, compiler_params=_params(),
    )(parts, w, m, v)


def _small_sum_update(partials, row_of, weights, loss_parts):
    n_p, n_w = len(partials), len(weights)

    def body(*refs):
        p_refs = refs[:n_p]
        loss_ref = refs[n_p]
        w_refs = refs[n_p + 1:n_p + 1 + 3 * n_w]
        loss_out = refs[n_p + 1 + 3 * n_w]
        outs = refs[n_p + 2 + 3 * n_w:]

        def total(ref):
            acc = ref[0].astype(F32)
            for d in range(1, N_DEV):
                acc = acc + ref[d].astype(F32)
            return acc

        lsum = jnp.sum(total(loss_ref), axis=-1, keepdims=True) * (0.5 / D)
        loss_out[...] = jnp.broadcast_to(lsum, loss_out.shape)
        for k in range(n_p):
            i, r0 = row_of[k]
            g = total(p_refs[k])
            w_ref, m_ref, v_ref = w_refs[3 * i:3 * i + 3]
            g_out, d_out, m_out, v_out = outs[4 * i:4 * i + 4]
            if g.ndim == 2:
                at = (slice(r0, r0 + g.shape[0]), slice(None))
            else:
                at = (slice(None),) * g.ndim
            delta, m_new, v_new = _adamw(w_ref[at], g, m_ref[at], v_ref[at])
            g_out[at] = g
            d_out[at] = delta
            m_out[at] = m_new
            v_out[at] = v_new

    vm = pl.BlockSpec(memory_space=pltpu.VMEM)
    flat_w = [a for wmv in weights for a in wmv]
    out_shape = [jax.ShapeDtypeStruct((1, 128), F32)]
    for w, _, _ in weights:
        out_shape += [jax.ShapeDtypeStruct(w.shape, F32)] * 4
    return pl.pallas_call(
        body, name="small_update", in_specs=[vm] * (n_p + 1 + 3 * n_w), out_specs=[vm] * len(out_shape),
        out_shape=out_shape, compiler_params=pltpu.CompilerParams(vmem_limit_bytes=VMEM_LIMIT),
    )(*partials, loss_parts, *flat_w)


def _pack_small_shard(cw, lg, lb):
    return (jnp.pad(cw, ((0, 13), (0, 64))) + jnp.pad(lg, ((8, 7), (0, 0))) + jnp.pad(lb, ((9, 6), (0, 0))))


TS_PRENORM = 1024
TS_FWD = 512
TS_BWD = 256
TS_WGRAD = 2048


def kernel(x, pre_norm, post_norm, even_w_in, even_conv_w, even_pool_w, even_pool_scale, even_w_out, odd_w_in, odd_ln_g, odd_ln_b, odd_w_s, odd_b_s, odd_w_out, loss_target, m_pre_norm, m_post_norm, m_even_w_in, m_even_conv_w, m_even_pool_w, m_even_pool_scale, m_even_w_out, m_odd_w_in, m_odd_ln_g, m_odd_ln_b, m_odd_w_s, m_odd_b_s, m_odd_w_out, v_pre_norm, v_post_norm, v_even_w_in, v_even_conv_w, v_even_pool_w, v_even_pool_scale, v_even_w_out, v_odd_w_in, v_odd_ln_g, v_odd_ln_b, v_odd_w_s, v_odd_b_s, v_odd_w_out):
    S = x.shape[1]
    xs = x.reshape(S, D)
    tgt = loss_target.reshape(S, D)

    gpre0, gpre1 = pre_norm[0:1], pre_norm[1:2]
    gpost0, gpost1 = post_norm[0:1], post_norm[1:2]
    small_shard = _pack_small_shard(even_conv_w[0], odd_ln_g, odd_ln_b)
    h0, wie_g, small_g, woe_g, wio_b, woo_b = _prenorm_gather(xs, gpre0, even_w_in[0], small_shard, even_w_out[0],
                                                              odd_w_in[0], odd_w_out[0], TS_PRENORM)
    wie = jnp.transpose(wie_g, (1, 0, 2)).reshape(D, PROJ)
    wie_t = jnp.transpose(wie_g, (0, 2, 1)).reshape(PROJ, D)
    woe = woe_g.reshape(D, D)
    woe_t = woe.T
    conv_w = jnp.transpose(small_g[:, 0:3, 0:64], (1, 0, 2)).reshape(3, A_W)
    ln_g = small_g[:, 8, :].reshape(1, D)
    ln_b = small_g[:, 9, :].reshape(1, D)

    pool_w_b = even_pool_w[0].astype(BF16)
    pool_wt_b = jnp.swapaxes(even_pool_w[0], 1, 2).astype(BF16)
    ws_tril = jnp.tril(odd_w_s[0])
    ws_b = ws_tril.astype(BF16)
    wst_b = jnp.swapaxes(ws_tril, 1, 2).astype(BF16)
    b_full = jnp.broadcast_to(odd_b_s[0][:, :, None], (N_HEADS, GROUP, GROUP))

    p0, m0, x1, h1, wio_g, woo_g = _even_forward_fused(h0, xs, wie, conv_w, pool_w_b, even_pool_scale, woe, gpost0,
                                                       gpre1, TS_FWD, sides=[(_Gather, [wio_b, woo_b])])
    wio = jnp.transpose(wio_g, (1, 0, 2)).reshape(D, PROJ)
    wio_t = jnp.transpose(wio_g, (0, 2, 1)).reshape(PROJ, D)
    woo = woo_g.reshape(D, D)
    woo_t = woo.T
    p1, m1, g2, loss_vec = _odd_forward(h1, x1, tgt, wio, ln_g, ln_b, ws_b, b_full, woo, gpost1, TS_FWD)

    dp1, g1, dwoo, dws, dbs, dlg, dlb, dgpost1, dgpre1 = _odd_backward(
        g2, m1, p1, x1, ln_g, ln_b, ws_b, wst_b, b_full, woo_t, wio_t, gpost1, gpre1, TS_BWD)
    by_owner = lambda dwo: dwo.reshape(N_DEV, SHARD_OUT, D)
    (dwio_parts, dwoo_parts, dlg_parts, dlb_parts,
     dws_g, dbs_g, gpost1_g, gpre1_g, loss_g) = _weight_grad_scatter(
        h1, dp1, TS_WGRAD, "odd_weight_grad",
        sides=[(_ChipSumScatter, [by_owner(dwoo)]),
               (_Scatter, [dlg.reshape(N_DEV, 1, 128), dlb.reshape(N_DEV, 1, 128)]),
               (_Gather, [dws.astype(BF16), dbs, dgpost1, dgpre1, loss_vec])])
    dp0, gx, dwoe, dcw, dpw, dps, dgpost0, dgpre0 = _even_backward_fused(
        g1, m0, p0, xs, conv_w, pool_w_b, pool_wt_b, even_pool_scale, woe_t, wie_t, gpost0, gpre0, TS_FWD)
    dcw_by_owner = jnp.transpose(dcw[0:3].reshape(3, N_DEV, 64), (1, 0, 2))
    dwie_parts, dwoe_parts, dcw_parts, dpw_g, dps_g, gpost0_g, gpre0_g = _weight_grad_scatter(
        h0, dp0, TS_WGRAD, "even_weight_grad",
        sides=[(_ChipSumScatter, [by_owner(dwoe)]), (_Scatter, [dcw_by_owner]),
               (_Gather, [dpw.astype(BF16), dps, dgpost0, dgpre0])])

    g_wie, d_wie, nm_wie, nv_wie = _sum_update(dwie_parts, even_w_in[0], m_even_w_in[0], v_even_w_in[0],
                                               "update_even_w_in")
    g_wio, d_wio, nm_wio, nv_wio = _sum_update(dwio_parts, odd_w_in[0], m_odd_w_in[0], v_odd_w_in[0],
                                               "update_odd_w_in")
    g_woe, d_woe, nm_woe, nv_woe = _sum_update(dwoe_parts, even_w_out[0], m_even_w_out[0], v_even_w_out[0],
                                               "update_even_w_out")
    g_woo, d_woo, nm_woo, nv_woo = _sum_update(dwoo_parts, odd_w_out[0], m_odd_w_out[0], v_odd_w_out[0],
                                               "update_odd_w_out")

    partials = [gpre0_g, gpre1_g, gpost0_g, gpost1_g, dpw_g, dps_g, dws_g, dbs_g, dcw_parts, dlg_parts, dlb_parts]
    row_of = [(0, 0), (0, 1), (1, 0), (1, 1), (2, 0), (3, 0), (4, 0), (5, 0), (6, 0), (7, 0), (8, 0)]
    weights = [(pre_norm, m_pre_norm, v_pre_norm), (post_norm, m_post_norm, v_post_norm),
               (even_pool_w[0], m_even_pool_w[0], v_even_pool_w[0]),
               (even_pool_scale, m_even_pool_scale, v_even_pool_scale),
               (odd_w_s[0], m_odd_w_s[0], v_odd_w_s[0]), (odd_b_s[0], m_odd_b_s[0], v_odd_b_s[0]),
               (even_conv_w[0], m_even_conv_w[0], v_even_conv_w[0]),
               (odd_ln_g, m_odd_ln_g, v_odd_ln_g), (odd_ln_b, m_odd_ln_b, v_odd_ln_b)]
    small = _small_sum_update(partials, row_of, weights, loss_g)
    loss = small[0][0, 0]

    def leaves(k, big):
        pre, post, pw, psc, ws, bs, cw, lg, lb = [small[1 + 4 * i + k] for i in range(len(weights))]
        wie_k, woe_k, wio_k, woo_k = big
        return [pre, post, wie_k[None], cw[None], pw[None], psc, woe_k[None], wio_k[None], lg, lb, ws[None], bs[None],
                woo_k[None]]

    outs = [loss, gx.reshape(1, S, D)]
    outs += leaves(0, (g_wie, g_woe, g_wio, g_woo))
    outs += leaves(1, (d_wie, d_woe, d_wio, d_woo))
    outs += leaves(2, (nm_wie, nm_woe, nm_wio, nm_woo))
    outs += leaves(3, (nv_wie, nv_woe, nv_wio, nv_woo))
    return tuple(outs)
```

```python
import functools

import jax
import jax.numpy as jnp
from jax import lax
from jax.experimental import pallas as pl
from jax.experimental.pallas import tpu as pltpu

F32 = jnp.float32
BF16 = jnp.bfloat16
MESH = pl.DeviceIdType.MESH

D = 1024
EPS = 1e-6
A_W = 512
B_W = 512
POOL_WINDOWS = (2, 4, 8, 16)
GROUP = 128
N_GROUPS = 4
N_HEADS = 8
HALO = 16
PROJ = 3072
N_DEV = 8
SHARD_IN = PROJ // N_DEV
SHARD_OUT = D // N_DEV

ADAM_LR = 0.001
ADAM_B1 = 0.9
ADAM_B2 = 0.999
ADAM_EPS = 1e-08
ADAM_WD = 0.01
ADAM_STEP = 10

VMEM_LIMIT = 54 * 1024 * 1024


def _params(n_grid=1, vmem=VMEM_LIMIT):
    return pltpu.CompilerParams(dimension_semantics=("arbitrary",) * n_grid, vmem_limit_bytes=vmem)


def _full(shape):
    return pl.BlockSpec(shape, lambda *_: (0,) * len(shape))


def _sigmoid(z):
    return jax.nn.sigmoid(z)


def _dsilu(z, s):
    return s * (1.0 + z * (1.0 - s))


def _dot(a, b):
    return jnp.dot(a, b, preferred_element_type=F32)


def _dot_t0(a, b):
    return lax.dot_general(a, b, (((0,), (0,)), ((), ())), preferred_element_type=F32)


def _dot_t1(a, b):
    return lax.dot_general(a, b, (((1,), (1,)), ((), ())), preferred_element_type=F32)


def _rms(x):
    return lax.rsqrt(jnp.mean(x * x, axis=-1, keepdims=True) + EPS)


PIECE = 16


def _fold(a):
    return a[0:8] + a[8:16]


def _shift_down(a, k):
    return pltpu.roll(a, k, 0)


def _shift_up(a, k):
    return pltpu.roll(a, a.shape[0] - k, 0)


def _mesh_position():
    return lax.axis_index("x"), lax.axis_index("y"), lax.axis_index("c")


def _index(pos):
    return 4 * pos[0] + 2 * pos[1] + pos[2]


def _peer(pos, d):
    x, y, c = pos
    return (1 - x if d & 4 else x, 1 - y if d & 2 else y, 1 - c if d & 1 else c)


def _remote(src, dst, send_sems, recv_sems, k, to):
    return pltpu.make_async_remote_copy(src_ref=src, dst_ref=dst, send_sem=send_sems.at[k], recv_sem=recv_sems.at[k],
                                        device_id=to, device_id_type=MESH)


class _Gather:
    def __init__(self, srcs, dsts, send_sems, recv_sems):
        x, y, c = _mesh_position()
        n = len(srcs)
        me, sibling = (x, y, c), (x, y, 1 - c)
        chips = [(1 - x, y), (x, 1 - y), (1 - x, 1 - y)]

        def copy(k, t, block, to, src=None):
            slot = dsts[t].at[_index(block)]
            return _remote(slot if src is None else src, slot, send_sems, recv_sems, k * n + t, to)

        tensors = range(n)
        self.local = [pltpu.make_async_copy(srcs[t], dsts[t].at[_index(me)], recv_sems.at[7 * n + t]) for t in tensors]
        self.first = [copy(1 + j, t, me, chip + (c,), srcs[t]) for t in tensors for j, chip in enumerate(chips)]
        self.first += [copy(0, t, me, sibling, srcs[t]) for t in tensors]
        self.ici_in = [copy(1 + j, t, chip + (c,), me) for t in tensors for j, chip in enumerate(chips)]
        self.passed = [copy(4 + j, t, chip + (c,), sibling) for t in tensors for j, chip in enumerate(chips)]
        self.d2d_in = [copy(0, t, sibling, me) for t in tensors]
        self.d2d_in += [copy(4 + j, t, chip + (1 - c,), me) for t in tensors for j, chip in enumerate(chips)]

    def start(self):
        for cp in self.local + self.first:
            cp.start()

    def middle(self):
        for landed, onward in zip(self.ici_in, self.passed):
            landed.wait_recv()
            onward.start()

    def finish(self):
        for cp in self.d2d_in:
            cp.wait_recv()
        for cp in self.first + self.passed:
            cp.wait_send()
        for cp in self.local:
            cp.wait()


class _Scatter:
    def __init__(self, srcs, dsts, send_sems, recv_sems):
        pos = _mesh_position()
        n = len(srcs)
        self.local = [pltpu.make_async_copy(srcs[t].at[_index(pos)], dsts[t].at[0], recv_sems.at[t]) for t in range(n)]
        self.remote = []
        for d in range(1, N_DEV):
            to = _peer(pos, d)
            self.remote += [_remote(srcs[t].at[_index(to)], dsts[t].at[d], send_sems, recv_sems, d * n + t, to)
                            for t in range(n)]

    def start(self):
        for cp in self.local + self.remote:
            cp.start()

    def middle(self):
        pass

    def finish(self):
        for cp in self.local:
            cp.wait()
        for cp in self.remote:
            cp.wait_recv()
        for cp in self.remote:
            cp.wait_send()


PARTS = 5
LANDING = PARTS + 6


class _ChipSumScatter:
    def __init__(self, srcs, dsts, send_sems, recv_sems):
        x, y, c = _mesh_position()
        self.n = len(srcs)
        self.srcs, self.dsts = srcs, dsts
        me, sibling = (x, y, c), (x, y, 1 - c)
        self.chips = [(x ^ (f >> 1), y ^ (f & 1)) for f in range(1, N_CHIPS)]
        rc = lambda t, k, src, slot, to: _remote(src, dsts[t].at[slot], send_sems, recv_sems, 8 * t + k, to)
        self.local = [pltpu.make_async_copy(srcs[t].at[_index(me)], dsts[t].at[0], recv_sems.at[8 * t])
                      for t in range(self.n)]
        self.to_sibling = [rc(t, 1, srcs[t].at[_index(sibling)], 1, sibling) for t in range(self.n)]
        self.swaps = [[rc(t, 2 + j, srcs[t].at[_index(chip + (1 - c,))], 5 + j, sibling)
                       for j, chip in enumerate(self.chips)] for t in range(self.n)]
        self.sums = [[rc(t, 5 + j, dsts[t].at[8 + j], 2 + j, chip + (c,))
                      for j, chip in enumerate(self.chips)] for t in range(self.n)]
        self.mine = [[srcs[t].at[_index(chip + (c,))] for chip in self.chips] for t in range(self.n)]

    def start(self):
        for t in range(self.n):
            for cp in [self.local[t], self.to_sibling[t]] + self.swaps[t]:
                cp.start()

    def middle(self):
        for t in range(self.n):
            block = self.srcs[t].shape[1:]
            for j in range(N_CHIPS - 1):
                self.swaps[t][j].wait_recv()

                def add(a_ref, b_ref, t=t, j=j):
                    pltpu.sync_copy(self.mine[t][j], a_ref)
                    pltpu.sync_copy(self.dsts[t].at[5 + j], b_ref)
                    a_ref[...] = (a_ref[...].astype(F32) + b_ref[...].astype(F32)).astype(BF16)
                    pltpu.sync_copy(a_ref, self.dsts[t].at[8 + j])

                pl.run_scoped(add, pltpu.VMEM(block, BF16), pltpu.VMEM(block, BF16))
                self.sums[t][j].start()

    def finish(self):
        for t in range(self.n):
            self.local[t].wait()
            self.to_sibling[t].wait_recv()
            for cp in self.sums[t]:
                cp.wait_recv()
            for cp in [self.to_sibling[t]] + self.swaps[t] + self.sums[t]:
                cp.wait_send()


HBM_SPEC = pl.BlockSpec(memory_space=pltpu.HBM)
MIDDLE_STEPS_BEFORE_END = 4


def _middle_step(exchange, total):
    return min(1, total - 1) if exchange is _ChipSumScatter else max(total - MIDDLE_STEPS_BEFORE_END, 0)


def _landing_shape(exchange, src):
    if exchange is _Gather:
        return (N_DEV,) + src.shape
    if exchange is _ChipSumScatter:
        return (LANDING,) + src.shape[1:]
    return src.shape


def _call(body, *, name, grid, in_specs, out_specs, out_shape, scratch_shapes, args, sides=()):
    params = _params(len(grid))
    if not sides:
        return pl.pallas_call(body, name=name, grid=grid, in_specs=in_specs, out_specs=out_specs, out_shape=out_shape,
                              scratch_shapes=scratch_shapes, compiler_params=params)(*args)
    n_in, n_out, n_scratch = len(in_specs), len(out_specs), len(scratch_shapes)
    counts = [len(srcs) for _, srcs in sides]
    ns = sum(counts)
    total = 1
    for g in grid:
        total *= g
    side_args, side_shapes, side_sems = [], [], []
    for exchange, srcs in sides:
        side_args += list(srcs)
        side_shapes += [jax.ShapeDtypeStruct(_landing_shape(exchange, s), s.dtype) for s in srcs]
        side_sems += [pltpu.SemaphoreType.DMA((N_DEV * len(srcs),))] * 2

    def wrapped(*refs):
        ins, side_in = refs[:n_in], refs[n_in:n_in + ns]
        outs = refs[n_in + ns:n_in + ns + n_out]
        side_out = refs[n_in + ns + n_out:n_in + 2 * ns + n_out]
        rest = refs[n_in + 2 * ns + n_out:]
        scratch, sems = rest[:n_scratch], rest[n_scratch:]
        step = pl.program_id(0)
        for axis in range(1, len(grid)):
            step = step * grid[axis] + pl.program_id(axis)

        def exchanges():
            built, at = [], 0
            for k, (exchange, _) in enumerate(sides):
                built.append(exchange(side_in[at:at + counts[k]], side_out[at:at + counts[k]],
                                      sems[2 * k], sems[2 * k + 1]))
                at += counts[k]
            return built

        @pl.when(step == 0)
        def _():
            for ex in exchanges():
                ex.start()

        for k, (exchange, _) in enumerate(sides):
            @pl.when(step == _middle_step(exchange, total))
            def _(k=k):
                exchanges()[k].middle()

        body(*ins, *outs, *scratch)

        @pl.when(step == total - 1)
        def _():
            for ex in exchanges():
                ex.finish()

    return pl.pallas_call(
        wrapped, name=name, grid=grid, in_specs=list(in_specs) + [HBM_SPEC] * ns,
        out_specs=list(out_specs) + [HBM_SPEC] * ns, out_shape=list(out_shape) + side_shapes,
        scratch_shapes=list(scratch_shapes) + side_sems, compiler_params=params)(*args, *side_args)


def _prenorm_gather(x, g, wie, small, woe, wio, woo, ts):
    S = x.shape[0]
    nt = S // ts

    def body(x_ref, g_ref, wie_ref, sm_ref, woe_ref, wio_ref, woo_ref,
             h_ref, wie_out, sm_out, woe_out, wio_b, woo_b, cast_ref, small_ref, cast2_ref, send_sems, recv_sems):
        step = pl.program_id(0)
        gather = lambda: _Gather([cast_ref, small_ref, cast2_ref], [wie_out, sm_out, woe_out], send_sems, recv_sems)

        @pl.when(step == 0)
        def _():
            cast_ref[...] = wie_ref[...].astype(BF16)
            small_ref[...] = sm_ref[...]
            cast2_ref[...] = woe_ref[...].astype(BF16)
            gather().start()
            wio_b[...] = wio_ref[...].astype(BF16)
            woo_b[...] = woo_ref[...].astype(BF16)

        xv = x_ref[...]
        h_ref[...] = ((xv * _rms(xv)) * g_ref[...]).astype(BF16)

        @pl.when(step == nt - 1)
        def _():
            exchange = gather()
            exchange.middle()
            exchange.finish()

    tile = pl.BlockSpec((ts, D), lambda i: (i, 0))
    sems = pltpu.SemaphoreType.DMA((N_DEV * 3,))
    shards = [wie, small, woe, wio, woo]
    return pl.pallas_call(
        body, name="prenorm_gather", grid=(nt,),
        in_specs=[tile, _full((1, D))] + [_full(a.shape) for a in shards],
        out_specs=[tile, HBM_SPEC, HBM_SPEC, HBM_SPEC] + [_full(a.shape) for a in shards[3:]],
        out_shape=[jax.ShapeDtypeStruct((S, D), BF16), jax.ShapeDtypeStruct((N_DEV,) + wie.shape, BF16),
                   jax.ShapeDtypeStruct((N_DEV,) + small.shape, F32), jax.ShapeDtypeStruct((N_DEV,) + woe.shape, BF16)]
        + [jax.ShapeDtypeStruct(a.shape, BF16) for a in shards[3:]],
        scratch_shapes=[pltpu.VMEM(wie.shape, BF16), pltpu.VMEM(small.shape, F32), pltpu.VMEM(woe.shape, BF16),
                        sems, sems],
        compiler_params=_params(),
    )(x, g, wie, small, woe, wio, woo)


def _matmul(h, w, ts, name, sides=()):
    S = h.shape[0]
    N = w.shape[1]

    def body(h_ref, w_ref, p_ref):
        p_ref[...] = _dot(h_ref[...], w_ref[...]).astype(BF16)

    return _call(
        body, name=name, grid=(S // ts,),
        in_specs=[pl.BlockSpec((ts, D), lambda i: (i, 0)), _full((D, N))],
        out_specs=[pl.BlockSpec((ts, N), lambda i: (i, 0))],
        out_shape=[jax.ShapeDtypeStruct((S, N), BF16)],
        scratch_shapes=[], args=(h, w), sides=sides)


def _even_mixer(proj, halo, row0, cw_ref, pw_ref):
    xa, gb, gc, za, xp, zp = proj
    nr = xa.shape[0]
    ha = gc * xa
    ha_ext = jnp.concatenate([halo[0], ha], axis=0)
    ha_m1 = _shift_down(ha_ext, 1)[HALO:]
    ha_m2 = _shift_down(ha_ext, 2)[HALO:]
    conv = cw_ref[2:3, :] * ha + cw_ref[1:2, :] * ha_m1 + cw_ref[0:1, :] * ha_m2
    sig_a = _sigmoid(za)
    silu_a = za * sig_a

    xp_ext = jnp.concatenate([halo[1], xp], axis=0)
    pos = row0 + lax.broadcasted_iota(jnp.int32, (nr, 1), 0)
    pooled, inv_cnt, mixed = [], [], []
    for g, w in enumerate(POOL_WINDOWS):
        cols = slice(g * GROUP, (g + 1) * GROUP)
        s = xp_ext[:, cols]
        k = 1
        while k < w:
            s = s + _shift_down(s, k)
            k *= 2
        inv = 1.0 / jnp.minimum(pos + 1, w).astype(F32)
        pg = s[HALO:] * inv - xp[:, cols]
        pooled.append(pg)
        inv_cnt.append(inv)
        mixed.append(_dot(pg.astype(BF16), pw_ref[g]))
    mixed = jnp.concatenate(mixed, axis=1)
    sig_b = _sigmoid(zp)
    silu_b = zp * sig_b
    return dict(xa=xa, gb=gb, gc=gc, za=za, xp=xp, zp=zp, ha=ha, ha_m1=ha_m1, ha_m2=ha_m2, conv=conv,
                sig_a=sig_a, silu_a=silu_a, pooled=pooled, inv_cnt=inv_cnt, mixed=mixed, sig_b=sig_b,
                silu_b=silu_b)


def _columns(a):
    return [a[:, k * A_W:(k + 1) * A_W].astype(F32) for k in range(6)]


def _even_forward_fused(h, x, win_b, cw, pw_b, ps, wout_b, gpost, gpre_next, ts, sides=()):
    S = x.shape[0]

    def body(h_ref, x_ref, wi_ref, cw_ref, pw_ref, ps_ref, wo_ref, gp_ref, gn_ref,
             p_ref, m_ref, x1_ref, h1_ref, mix_ref, carry_ref):
        i = pl.program_id(0)

        @pl.when(i == 0)
        def _():
            carry_ref[...] = jnp.zeros_like(carry_ref)

        for r0 in range(0, ts, SUB_ROWS):
            rows = slice(r0, r0 + SUB_ROWS)
            proj = _dot(h_ref[rows, :], wi_ref[...])
            p_ref[rows, :] = proj.astype(BF16)
            fw = _even_mixer(_columns(proj), (carry_ref[:, 0:A_W], carry_ref[:, A_W:D]), i * ts + r0, cw_ref, pw_ref)
            carry_ref[:, 0:A_W] = fw["ha"][SUB_ROWS - HALO:]
            carry_ref[:, A_W:D] = fw["xp"][SUB_ROWS - HALO:]
            mix_ref[rows, 0:A_W] = (fw["gb"] * fw["conv"] * fw["silu_a"]).astype(BF16)
            mix_ref[rows, A_W:D] = (fw["mixed"] * ps_ref[...] * fw["silu_b"]).astype(BF16)
            m = _dot(mix_ref[rows, :], wo_ref[...])
            m_ref[rows, :] = m.astype(BF16)
            x1 = x_ref[rows, :] + (m * _rms(m)) * gp_ref[...]
            x1_ref[rows, :] = x1
            h1_ref[rows, :] = ((x1 * _rms(x1)) * gn_ref[...]).astype(BF16)

    tile = pl.BlockSpec((ts, D), lambda i: (i, 0))
    return _call(
        body, name="even_forward", grid=(S // ts,),
        in_specs=[tile, tile, _resident((D, PROJ)), _full((3, A_W)), _full((N_GROUPS, GROUP, GROUP)), _full((1, B_W)),
                  _resident((D, D)), _full((1, D)), _full((1, D))],
        out_specs=[pl.BlockSpec((ts, PROJ), lambda i: (i, 0)), tile, tile, tile],
        out_shape=[jax.ShapeDtypeStruct((S, PROJ), BF16), jax.ShapeDtypeStruct((S, D), BF16),
                   jax.ShapeDtypeStruct((S, D), F32), jax.ShapeDtypeStruct((S, D), BF16)],
        scratch_shapes=[pltpu.VMEM((ts, D), BF16), pltpu.VMEM((HALO, D), F32)],
        args=(h, x, win_b, cw, pw_b, ps, wout_b, gpost, gpre_next), sides=sides)


def _even_backward_fused(g1, m0, p, x, cw, pw_b, pwt_b, ps, woutt_b, wint_b, gpost, gpre, ts):
    S = g1.shape[0]
    nt = S // ts

    def body(g_ref, m_ref, p_ref, halo_ref, x_ref, cw_ref, pw_ref, pwt_ref, ps_ref, wot_ref, wit_ref, gp_ref, gn_ref,
             dp_ref, dx_ref, dwob_ref, dcw_ref, dpw_ref, dps_ref, dgp_ref, dgn_ref,
             carry_ref, mix_ref, dm_ref, dwo_ref):
        step = pl.program_id(0)
        i = nt - 1 - step

        @pl.when(step == 0)
        def _():
            carry_ref[...] = jnp.zeros_like(carry_ref)
            dwo_ref[...] = jnp.zeros_like(dwo_ref)
            dcw_ref[...] = jnp.zeros_like(dcw_ref)
            dpw_ref[...] = jnp.zeros_like(dpw_ref)
            dps_ref[...] = jnp.zeros_like(dps_ref)
            dgp_ref[...] = jnp.zeros_like(dgp_ref)
            dgn_ref[...] = jnp.zeros_like(dgn_ref)

        ps_v = ps_ref[...]
        sums = [None] * 6
        add = lambda k, part: sums.__setitem__(k, part if sums[k] is None else sums[k] + part)
        for r0 in range(ts - SUB_ROWS, -1, -SUB_ROWS):
            rows = slice(r0, r0 + SUB_ROWS)
            g = g_ref[rows, :]
            m = m_ref[rows, :].astype(F32)
            q = _rms(m)
            n = m * q
            add(0, jnp.sum(g * n, axis=0, keepdims=True))
            dn = g * gp_ref[...]
            dm = q * (dn - n * jnp.mean(dn * n, axis=-1, keepdims=True))
            dm_ref[rows, :] = dm.astype(BF16)
            dmix = _dot(dm_ref[rows, :], wot_ref[...])
            dya = dmix[:, 0:A_W]
            dyb = dmix[:, A_W:D]

            if r0 == 0:
                before = _columns(halo_ref[...])
                keep = jnp.where(i == 0, 0.0, 1.0).astype(F32)
                halo = (before[2] * before[0] * keep, before[4] * keep)
            else:
                before = _columns(p_ref[r0 - HALO:r0, :])
                halo = (before[2] * before[0], before[4])
            fw = _even_mixer(_columns(p_ref[rows, :]), halo, i * ts + r0, cw_ref, pw_ref)
            mix_ref[rows, 0:A_W] = (fw["gb"] * fw["conv"] * fw["silu_a"]).astype(BF16)
            mix_ref[rows, A_W:D] = (fw["mixed"] * ps_v * fw["silu_b"]).astype(BF16)

            t = dya * fw["gb"]
            dconv = t * fw["silu_a"]
            dgb = dya * fw["conv"] * fw["silu_a"]
            dza = t * fw["conv"] * _dsilu(fw["za"], fw["sig_a"])
            add(2, jnp.sum(dconv * fw["ha"], axis=0, keepdims=True))
            add(3, jnp.sum(dconv * fw["ha_m1"], axis=0, keepdims=True))
            add(4, jnp.sum(dconv * fw["ha_m2"], axis=0, keepdims=True))
            dconv_ext = jnp.concatenate([dconv, carry_ref[:, 0:A_W]], axis=0)
            dha = (cw_ref[2:3, :] * dconv + cw_ref[1:2, :] * _shift_up(dconv_ext, 1)[:SUB_ROWS]
                   + cw_ref[0:1, :] * _shift_up(dconv_ext, 2)[:SUB_ROWS])
            dgc = dha * fw["xa"]
            dxa = dha * fw["gc"]

            u = dyb * fw["mixed"]
            add(1, jnp.sum(u * fw["silu_b"], axis=0, keepdims=True))
            dzp = u * ps_v * _dsilu(fw["zp"], fw["sig_b"])
            dmixed = (dyb * ps_v * fw["silu_b"]).astype(BF16)
            dxp, e_first = [], []
            for gi, w in enumerate(POOL_WINDOWS):
                cols = slice(gi * GROUP, (gi + 1) * GROUP)
                dmg = dmixed[:, cols]
                dpooled = _dot(dmg, pwt_ref[gi])
                dpw_ref[gi] += _dot_t0(fw["pooled"][gi].astype(BF16), dmg)
                e = dpooled * fw["inv_cnt"][gi]
                e_first.append(e[0:HALO])
                s = jnp.concatenate([e, carry_ref[:, A_W + gi * GROUP:A_W + (gi + 1) * GROUP]], axis=0)
                k = 1
                while k < w:
                    s = s + _shift_up(s, k)
                    k *= 2
                dxp.append(s[:SUB_ROWS] - dpooled)
            carry_ref[:, 0:A_W] = dconv[0:HALO]
            carry_ref[:, A_W:D] = jnp.concatenate(e_first, axis=1)

            dp_ref[rows, 0:512] = dxa.astype(BF16)
            dp_ref[rows, 512:1024] = dgb.astype(BF16)
            dp_ref[rows, 1024:1536] = dgc.astype(BF16)
            dp_ref[rows, 1536:2048] = dza.astype(BF16)
            dp_ref[rows, 2048:2560] = jnp.concatenate(dxp, axis=1).astype(BF16)
            dp_ref[rows, 2560:3072] = dzp.astype(BF16)

            dh = _dot(dp_ref[rows, :], wit_ref[...])
            xv = x_ref[rows, :]
            r = _rms(xv)
            xn = xv * r
            add(5, jnp.sum(dh * xn, axis=0, keepdims=True))
            dxn = dh * gn_ref[...]
            dx_ref[rows, :] = g + r * (dxn - xn * jnp.mean(dxn * xn, axis=-1, keepdims=True))

        dwo_ref[...] += _dot_t0(mix_ref[...], dm_ref[...])
        dgp_ref[...] += sums[0]
        dps_ref[...] += sums[1]
        dcw_ref[2:3, :] += sums[2]
        dcw_ref[1:2, :] += sums[3]
        dcw_ref[0:1, :] += sums[4]
        dgn_ref[...] += sums[5]

        @pl.when(step == nt - 1)
        def _():
            dwob_ref[...] = dwo_ref[...].astype(BF16)

    rev = lambda s: (nt - 1 - s, 0)
    tile = pl.BlockSpec((ts, D), rev)
    wide = pl.BlockSpec((ts, PROJ), rev)
    bpt = ts // HALO
    halo_map = lambda s: (jnp.maximum((nt - 1 - s) * bpt - 1, 0), 0)
    groups = _full((N_GROUPS, GROUP, GROUP))
    vec = _full((1, D))
    return pl.pallas_call(
        body, name="even_backward", grid=(nt,),
        in_specs=[tile, tile, wide, pl.BlockSpec((HALO, PROJ), halo_map), tile, _full((3, A_W)), groups, groups,
                  _full((1, B_W)), _resident((D, D)), _resident((PROJ, D)), vec, vec],
        out_specs=[wide, tile, _full((D, D)), _full((8, A_W)), groups, _full((1, B_W)), vec, vec],
        out_shape=[jax.ShapeDtypeStruct((S, PROJ), BF16), jax.ShapeDtypeStruct((S, D), F32),
                   jax.ShapeDtypeStruct((D, D), BF16), jax.ShapeDtypeStruct((8, A_W), F32),
                   jax.ShapeDtypeStruct((N_GROUPS, GROUP, GROUP), F32), jax.ShapeDtypeStruct((1, B_W), F32),
                   jax.ShapeDtypeStruct((1, D), F32), jax.ShapeDtypeStruct((1, D), F32)],
        scratch_shapes=[pltpu.VMEM((HALO, D), F32), pltpu.VMEM((ts, D), BF16), pltpu.VMEM((ts, D), BF16),
                        pltpu.VMEM((D, D), F32)],
        compiler_params=_params(),
    )(g1, m0, p, p, x, cw, pw_b, pwt_b, ps, woutt_b, wint_b, gpost, gpre)


def _even_forward_parts(p_ref, halo_ref, is_first, row0, cw_ref, pw_ref, ts):
    f = lambda a: a.astype(F32)
    xa = f(p_ref[:, 0:512])
    gb = f(p_ref[:, 512:1024])
    gc = f(p_ref[:, 1024:1536])
    za = f(p_ref[:, 1536:2048])
    xp = f(p_ref[:, 2048:2560])
    zp = f(p_ref[:, 2560:3072])
    keep = jnp.where(is_first, 0.0, 1.0).astype(F32)
    ha = gc * xa
    ha_halo = f(halo_ref[:, 1024:1536]) * f(halo_ref[:, 0:512]) * keep
    xp_halo = f(halo_ref[:, 2048:2560]) * keep
    ha_ext = jnp.concatenate([ha_halo, ha], axis=0)
    ha_m1 = _shift_down(ha_ext, 1)[HALO:]
    ha_m2 = _shift_down(ha_ext, 2)[HALO:]
    conv = cw_ref[2:3, :] * ha + cw_ref[1:2, :] * ha_m1 + cw_ref[0:1, :] * ha_m2
    sig_a = _sigmoid(za)
    silu_a = za * sig_a

    xp_ext = jnp.concatenate([xp_halo, xp], axis=0)
    pos = row0 + lax.broadcasted_iota(jnp.int32, (ts, 1), 0)
    pooled, inv_cnt, mixed = [], [], []
    for g, w in enumerate(POOL_WINDOWS):
        cols = slice(g * GROUP, (g + 1) * GROUP)
        s = xp_ext[:, cols]
        k = 1
        while k < w:
            s = s + _shift_down(s, k)
            k *= 2
        inv = 1.0 / jnp.minimum(pos + 1, w).astype(F32)
        pg = s[HALO:] * inv - xp[:, cols]
        pooled.append(pg)
        inv_cnt.append(inv)
        mixed.append(_dot(pg.astype(BF16), pw_ref[g]))
    mixed = jnp.concatenate(mixed, axis=1)
    sig_b = _sigmoid(zp)
    silu_b = zp * sig_b
    return dict(xa=xa, gb=gb, gc=gc, za=za, xp=xp, zp=zp, ha=ha, ha_m1=ha_m1, ha_m2=ha_m2, conv=conv,
                sig_a=sig_a, silu_a=silu_a, pooled=pooled, inv_cnt=inv_cnt, mixed=mixed, sig_b=sig_b,
                silu_b=silu_b)


def _halo_index(ts):
    blocks_per_tile = ts // HALO
    return lambda i: (jnp.maximum(i * blocks_per_tile - 1, 0), 0)


def _even_forward(p, x, cw, pw_b, ps, wout_b, gpost, gpre_next, ts, sides=()):
    S = x.shape[0]

    def body(p_ref, halo_ref, x_ref, cw_ref, pw_ref, ps_ref, wo_ref, gp_ref, gn_ref, m_ref, x1_ref, h1_ref, mix_ref):
        i = pl.program_id(0)
        fw = _even_forward_parts(p_ref, halo_ref, i == 0, i * ts, cw_ref, pw_ref, ts)
        mix_ref[:, 0:A_W] = (fw["gb"] * fw["conv"] * fw["silu_a"]).astype(BF16)
        mix_ref[:, A_W:D] = (fw["mixed"] * ps_ref[...] * fw["silu_b"]).astype(BF16)
        m = _dot(mix_ref[...], wo_ref[...])
        m_ref[...] = m
        x1 = x_ref[...] + (m * _rms(m)) * gp_ref[...]
        x1_ref[...] = x1
        h1_ref[...] = ((x1 * _rms(x1)) * gn_ref[...]).astype(BF16)

    tile = pl.BlockSpec((ts, D), lambda i: (i, 0))
    return _call(
        body, name="even_forward", grid=(S // ts,),
        in_specs=[pl.BlockSpec((ts, PROJ), lambda i: (i, 0)), pl.BlockSpec((HALO, PROJ), _halo_index(ts)), tile,
                  _full((3, A_W)), _full((N_GROUPS, GROUP, GROUP)), _full((1, B_W)), _full((D, D)), _full((1, D)),
                  _full((1, D))],
        out_specs=[tile, tile, tile],
        out_shape=[jax.ShapeDtypeStruct((S, D), F32), jax.ShapeDtypeStruct((S, D), F32),
                   jax.ShapeDtypeStruct((S, D), BF16)],
        scratch_shapes=[pltpu.VMEM((ts, D), BF16)],
        args=(p, p, x, cw, pw_b, ps, wout_b, gpost, gpre_next), sides=sides)


def _even_backward(g1, m0, p, cw, pw_b, pwt_b, ps, woutt_b, gpost, ts, sides=()):
    S = g1.shape[0]
    nt = S // ts

    def body(g_ref, m_ref, p_ref, halo_ref, cw_ref, pw_ref, pwt_ref, ps_ref, wot_ref, gp_ref,
             dp_ref, dwob_ref, dcw_ref, dpw_ref, dps_ref, dgp_ref, carry_ref, mix_ref, dm_ref, dwo_ref):
        step = pl.program_id(0)
        i = nt - 1 - step

        @pl.when(step == 0)
        def _():
            carry_ref[...] = jnp.zeros_like(carry_ref)
            dwo_ref[...] = jnp.zeros_like(dwo_ref)
            dcw_ref[...] = jnp.zeros_like(dcw_ref)
            dpw_ref[...] = jnp.zeros_like(dpw_ref)
            dps_ref[...] = jnp.zeros_like(dps_ref)
            dgp_ref[...] = jnp.zeros_like(dgp_ref)

        g = g_ref[...]
        m = m_ref[...]
        q = _rms(m)
        n = m * q
        dgp_ref[...] += jnp.sum(g * n, axis=0, keepdims=True)
        dn = g * gp_ref[...]
        dm = q * (dn - n * jnp.mean(dn * n, axis=-1, keepdims=True))
        dm_ref[...] = dm.astype(BF16)
        dmix = _dot(dm_ref[...], wot_ref[...])
        dya = dmix[:, 0:A_W]
        dyb = dmix[:, A_W:D]

        fw = _even_forward_parts(p_ref, halo_ref, i == 0, i * ts, cw_ref, pw_ref, ts)
        ps_v = ps_ref[...]
        mix_ref[:, 0:A_W] = (fw["gb"] * fw["conv"] * fw["silu_a"]).astype(BF16)
        mix_ref[:, A_W:D] = (fw["mixed"] * ps_v * fw["silu_b"]).astype(BF16)
        dwo_ref[...] += _dot_t0(mix_ref[...], dm_ref[...])

        t = dya * fw["gb"]
        dconv = t * fw["silu_a"]
        dgb = dya * fw["conv"] * fw["silu_a"]
        dza = t * fw["conv"] * _dsilu(fw["za"], fw["sig_a"])
        dcw_ref[2:3, :] += jnp.sum(dconv * fw["ha"], axis=0, keepdims=True)
        dcw_ref[1:2, :] += jnp.sum(dconv * fw["ha_m1"], axis=0, keepdims=True)
        dcw_ref[0:1, :] += jnp.sum(dconv * fw["ha_m2"], axis=0, keepdims=True)
        dconv_ext = jnp.concatenate([dconv, carry_ref[:, 0:A_W]], axis=0)
        dha = (cw_ref[2:3, :] * dconv + cw_ref[1:2, :] * _shift_up(dconv_ext, 1)[:ts]
               + cw_ref[0:1, :] * _shift_up(dconv_ext, 2)[:ts])
        dgc = dha * fw["xa"]
        dxa = dha * fw["gc"]

        u = dyb * fw["mixed"]
        dps_ref[...] += jnp.sum(u * fw["silu_b"], axis=0, keepdims=True)
        dzp = u * ps_v * _dsilu(fw["zp"], fw["sig_b"])
        dmixed = (dyb * ps_v * fw["silu_b"]).astype(BF16)
        dxp, e_first = [], []
        for gi, w in enumerate(POOL_WINDOWS):
            cols = slice(gi * GROUP, (gi + 1) * GROUP)
            dmg = dmixed[:, cols]
            dpooled = _dot(dmg, pwt_ref[gi])
            dpw_ref[gi] += _dot_t0(fw["pooled"][gi].astype(BF16), dmg)
            e = dpooled * fw["inv_cnt"][gi]
            e_first.append(e[0:HALO])
            s = jnp.concatenate([e, carry_ref[:, A_W + gi * GROUP:A_W + (gi + 1) * GROUP]], axis=0)
            k = 1
            while k < w:
                s = s + _shift_up(s, k)
                k *= 2
            dxp.append(s[:ts] - dpooled)

        carry_ref[:, 0:A_W] = dconv[0:HALO]
        carry_ref[:, A_W:D] = jnp.concatenate(e_first, axis=1)

        dp_ref[:, 0:512] = dxa.astype(BF16)
        dp_ref[:, 512:1024] = dgb.astype(BF16)
        dp_ref[:, 1024:1536] = dgc.astype(BF16)
        dp_ref[:, 1536:2048] = dza.astype(BF16)
        dp_ref[:, 2048:2560] = jnp.concatenate(dxp, axis=1).astype(BF16)
        dp_ref[:, 2560:3072] = dzp.astype(BF16)

        @pl.when(step == nt - 1)
        def _():
            dwob_ref[...] = dwo_ref[...].astype(BF16)

    rev = lambda s: (nt - 1 - s, 0)
    tile = pl.BlockSpec((ts, D), rev)
    bpt = ts // HALO
    halo_map = lambda s: (jnp.maximum((nt - 1 - s) * bpt - 1, 0), 0)
    return _call(
        body, name="even_backward", grid=(nt,),
        in_specs=[tile, tile, pl.BlockSpec((ts, PROJ), rev), pl.BlockSpec((HALO, PROJ), halo_map),
                  _full((3, A_W)), _full((N_GROUPS, GROUP, GROUP)), _full((N_GROUPS, GROUP, GROUP)),
                  _full((1, B_W)), _full((D, D)), _full((1, D))],
        out_specs=[pl.BlockSpec((ts, PROJ), rev), _full((D, D)), _full((8, A_W)),
                   _full((N_GROUPS, GROUP, GROUP)), _full((1, B_W)), _full((1, D))],
        out_shape=[jax.ShapeDtypeStruct((S, PROJ), BF16), jax.ShapeDtypeStruct((D, D), BF16),
                   jax.ShapeDtypeStruct((8, A_W), F32), jax.ShapeDtypeStruct((N_GROUPS, GROUP, GROUP), F32),
                   jax.ShapeDtypeStruct((1, B_W), F32), jax.ShapeDtypeStruct((1, D), F32)],
        scratch_shapes=[pltpu.VMEM((HALO, D), F32), pltpu.VMEM((ts, D), BF16), pltpu.VMEM((ts, D), BF16),
                        pltpu.VMEM((D, D), F32)],
        args=(g1, m0, p, p, cw, pw_b, pwt_b, ps, woutt_b, gpost), sides=sides)


def _odd_forward_parts(v, lg_ref, lb_ref, wt_ref, bfull_ref, vln_ref, sv_ref, r0, nr):
    rows = slice(r0, r0 + nr)
    mu = jnp.mean(v, axis=-1, keepdims=True)
    vc = v - mu
    rstd = lax.rsqrt(jnp.mean(vc * vc, axis=-1, keepdims=True) + EPS)
    vh = vc * rstd
    vln_ref[rows, :] = (vh * lg_ref[...] + lb_ref[...]).astype(BF16)
    _gate_matmuls(wt_ref, vln_ref, sv_ref, [slice(r0 + k * GROUP, r0 + (k + 1) * GROUP) for k in range(nr // GROUP)],
                  bfull_ref)
    return vh, rstd


def _gate_matmuls(w_ref, in_ref, out_ref, chunks, bias_ref=None):
    for h in range(N_HEADS):
        cols = slice(h * GROUP, (h + 1) * GROUP)
        wide = jnp.concatenate([in_ref[chunk, cols] for chunk in chunks], axis=1)
        res = _dot(w_ref[h], wide)
        for k, chunk in enumerate(chunks):
            part = res[:, k * GROUP:(k + 1) * GROUP]
            out_ref[chunk, cols] = part if bias_ref is None else part + bias_ref[h]


SUB_ROWS = 256


def _odd_forward(h, x1, tgt, win_b, lg, lb, wt_b, bfull, wout_b, gpost, ts):
    S = x1.shape[0]

    def body(h_ref, x_ref, t_ref, wi_ref, lg_ref, lb_ref, wt_ref, bfull_ref, wo_ref, gp_ref,
             p_ref, m_ref, g_ref, loss_ref, vln_ref, sv_ref, y_ref):
        @pl.when(pl.program_id(0) == 0)
        def _():
            loss_ref[...] = jnp.zeros_like(loss_ref)

        loss = None
        for r0 in range(0, ts, SUB_ROWS):
            rows = slice(r0, r0 + SUB_ROWS)
            proj = _dot(h_ref[rows, :], wi_ref[...])
            p_ref[rows, :] = proj.astype(BF16)
            u, v, z = proj[:, 0:D], proj[:, D:2 * D], proj[:, 2 * D:3 * D]
            _odd_forward_parts(v, lg_ref, lb_ref, wt_ref, bfull_ref, vln_ref, sv_ref, r0, SUB_ROWS)
            y_ref[rows, :] = (u * sv_ref[rows, :] * (z * _sigmoid(z))).astype(BF16)
            m = _dot(y_ref[rows, :], wo_ref[...])
            m_ref[rows, :] = m.astype(BF16)
            x2 = x_ref[rows, :] + (m * _rms(m)) * gp_ref[...]
            err = x2 - t_ref[rows, :]
            g_ref[rows, :] = err * (1.0 / D)
            part = jnp.sum(err * err, axis=0, keepdims=True)
            loss = part if loss is None else loss + part
        loss_ref[...] += loss

    tile = pl.BlockSpec((ts, D), lambda i: (i, 0))
    small = _full((N_HEADS, GROUP, GROUP))
    return pl.pallas_call(
        body, name="odd_forward", grid=(S // ts,),
        in_specs=[pl.BlockSpec((ts, D), lambda i: (i, 0)), tile, tile, _full((D, PROJ)), _full((1, D)), _full((1, D)),
                  small, small, _full((D, D)), _full((1, D))],
        out_specs=[pl.BlockSpec((ts, PROJ), lambda i: (i, 0)), tile, tile, _full((1, D))],
        out_shape=[jax.ShapeDtypeStruct((S, PROJ), BF16), jax.ShapeDtypeStruct((S, D), BF16),
                   jax.ShapeDtypeStruct((S, D), F32), jax.ShapeDtypeStruct((1, D), F32)],
        scratch_shapes=[pltpu.VMEM((ts, D), BF16), pltpu.VMEM((ts, D), F32), pltpu.VMEM((ts, D), BF16)],
        compiler_params=_params(),
    )(h, x1, tgt, win_b, lg, lb, wt_b, bfull, wout_b, gpost)


def _resident(shape):
    return pl.BlockSpec(shape, lambda *_: (0,) * len(shape), pipeline_mode=pl.Buffered(1))


def _odd_backward(g2, m1, p, x1, lg, lb, wt_b, wtt_b, bfull, woutt_b, wint_b, gpost, gpre, ts):
    S = g2.shape[0]
    nt = S // ts
    sub = ts // 2

    def body(g_ref, m_ref, p_ref, x_ref, lg_ref, lb_ref, wt_ref, wtt_ref, bfull_ref, wot_ref, wit_ref, gp_ref, gn_ref,
             dp_ref, dx_ref, dwob_ref, dws_ref, dbs_ref, dlg_ref, dlb_ref, dgp_ref, dgn_ref,
             vln_ref, sv_ref, dsvb_ref, dvln_ref, dsum_ref, y_ref, dm_ref, dwo_ref):
        step = pl.program_id(0)

        @pl.when(step == 0)
        def _():
            dwo_ref[...] = jnp.zeros_like(dwo_ref)
            dws_ref[...] = jnp.zeros_like(dws_ref)
            dsum_ref[...] = jnp.zeros_like(dsum_ref)
            dlg_ref[...] = jnp.zeros_like(dlg_ref)
            dlb_ref[...] = jnp.zeros_like(dlb_ref)
            dgp_ref[...] = jnp.zeros_like(dgp_ref)
            dgn_ref[...] = jnp.zeros_like(dgn_ref)

        tril = (lax.broadcasted_iota(jnp.int32, (GROUP, GROUP), 0)
                >= lax.broadcasted_iota(jnp.int32, (GROUP, GROUP), 1))
        sums = [None] * 4
        add = lambda k, part: sums.__setitem__(k, part if sums[k] is None else sums[k] + part)
        def post_norm(c):
            gp = gp_ref[...]
            for rows in c["pieces"]:
                g = g_ref[rows, :]
                m = m_ref[rows, :].astype(F32)
                q = _rms(m)
                n = m * q
                add(0, _fold(g * n))
                dn = g * gp
                dm = q * (dn - n * jnp.mean(dn * n, axis=-1, keepdims=True))
                dm_ref[rows, :] = dm.astype(BF16)

        def out_projection(c):
            c["dy"] = _dot(dm_ref[c["rows"], :], wot_ref[...])

        def layernorm(c):
            v = p_ref[c["rows"], D:2 * D].astype(F32)
            mu = jnp.mean(v, axis=-1, keepdims=True)
            vc = v - mu
            c["rstd"] = lax.rsqrt(jnp.mean(vc * vc, axis=-1, keepdims=True) + EPS)
            c["vh"] = vc * c["rstd"]
            vln_ref[c["rows"], :] = (c["vh"] * lg_ref[...] + lb_ref[...]).astype(BF16)

        def gate_matmuls(c):
            for chunk in c["chunks"]:
                for h in range(N_HEADS):
                    cols = slice(h * GROUP, (h + 1) * GROUP)
                    sv_ref[chunk, cols] = _dot(wt_ref[h], vln_ref[chunk, cols]) + bfull_ref[h]

        def gating(c):
            rows = c["rows"]
            u = p_ref[rows, 0:D].astype(F32)
            z = p_ref[rows, 2 * D:3 * D].astype(F32)
            sv = sv_ref[rows, :]
            sig = _sigmoid(z)
            sz = z * sig
            y_ref[rows, :] = (u * sv * sz).astype(BF16)
            dy = c.pop("dy")
            t = dy * sz
            dsv = t * u
            dsvb_ref[rows, :] = dsv.astype(BF16)
            for k in range(sub // GROUP):
                dsum_ref[...] += dsv[k * GROUP:(k + 1) * GROUP]
            dp_ref[rows, 0:D] = (t * sv).astype(BF16)
            dp_ref[rows, 2 * D:3 * D] = (dy * u * sv * _dsilu(z, sig)).astype(BF16)

        def gate_backward_matmuls(c):
            for chunk in c["chunks"]:
                for h in range(N_HEADS):
                    cols = slice(h * GROUP, (h + 1) * GROUP)
                    dvln_ref[chunk, cols] = _dot(wtt_ref[h], dsvb_ref[chunk, cols])
                    dws_ref[h] += jnp.where(tril, _dot_t1(dsvb_ref[chunk, cols], vln_ref[chunk, cols]), 0.0)

        def layernorm_backward(c):
            vh, rstd = c.pop("vh"), c.pop("rstd")
            dvln = dvln_ref[c["rows"], :]
            add(1, jnp.sum(dvln * vh, axis=0, keepdims=True))
            add(2, jnp.sum(dvln, axis=0, keepdims=True))
            dvh = dvln * lg_ref[...]
            dv = rstd * (dvh - jnp.mean(dvh, axis=-1, keepdims=True)
                         - vh * jnp.mean(dvh * vh, axis=-1, keepdims=True))
            dp_ref[c["rows"], D:2 * D] = dv.astype(BF16)

        def in_projection(c):
            c["dh"] = _dot(dp_ref[c["rows"], :], wit_ref[...])

        def pre_norm(c):
            dh_all = c.pop("dh")
            gn = gn_ref[...]
            for k, rows in enumerate(c["pieces"]):
                dh = dh_all[k * PIECE:(k + 1) * PIECE]
                xv = x_ref[rows, :]
                r = _rms(xv)
                xn = xv * r
                add(3, _fold(dh * xn))
                dxn = dh * gn
                dx_ref[rows, :] = g_ref[rows, :] + r * (dxn - xn * jnp.mean(dxn * xn, axis=-1, keepdims=True))

        phases = [post_norm, out_projection, layernorm, gate_matmuls, gating, gate_backward_matmuls,
                  layernorm_backward, in_projection, pre_norm]
        groups = [dict(rows=slice(r0, r0 + sub),
                       chunks=[slice(r0 + k * GROUP, r0 + (k + 1) * GROUP) for k in range(sub // GROUP)],
                       pieces=[slice(r0 + k, r0 + k + PIECE) for k in range(0, sub, PIECE)])
                  for r0 in range(0, ts, sub)]
        for group in groups:
            for phase in phases:
                phase(group)

        dwo_ref[...] += _dot_t0(y_ref[...], dm_ref[...])
        dgp_ref[...] += jnp.sum(sums[0], axis=0, keepdims=True)
        dlg_ref[...] += jnp.sum(sums[1], axis=0, keepdims=True)
        dlb_ref[...] += jnp.sum(sums[2], axis=0, keepdims=True)
        dgn_ref[...] += jnp.sum(sums[3], axis=0, keepdims=True)

        @pl.when(step == nt - 1)
        def _():
            dwob_ref[...] = dwo_ref[...].astype(BF16)
            ones = jnp.ones((8, GROUP), F32)
            for h in range(N_HEADS):
                cols = slice(h * GROUP, (h + 1) * GROUP)
                total = lax.dot_general(ones, dsum_ref[:, cols], (((1,), (1,)), ((), ())),
                                        precision=lax.Precision.HIGHEST, preferred_element_type=F32)
                dbs_ref[h:h + 1, :] = total[0:1, :]

    tile = pl.BlockSpec((ts, D), lambda i: (i, 0))
    wide = pl.BlockSpec((ts, PROJ), lambda i: (i, 0))
    small = _full((N_HEADS, GROUP, GROUP))
    heads = _resident((N_HEADS, GROUP, GROUP))
    vec = _full((1, D))
    return pl.pallas_call(
        body, name="odd_backward", grid=(nt,),
        in_specs=[tile, tile, wide, tile, vec, vec, heads, heads, heads, _resident((D, D)), _resident((PROJ, D)),
                  vec, vec],
        out_specs=[wide, tile, _full((D, D)), small, _full((N_HEADS, GROUP)), vec, vec, vec, vec],
        out_shape=[jax.ShapeDtypeStruct((S, PROJ), BF16), jax.ShapeDtypeStruct((S, D), F32),
                   jax.ShapeDtypeStruct((D, D), BF16),
                   jax.ShapeDtypeStruct((N_HEADS, GROUP, GROUP), F32), jax.ShapeDtypeStruct((N_HEADS, GROUP), F32),
                   jax.ShapeDtypeStruct((1, D), F32), jax.ShapeDtypeStruct((1, D), F32),
                   jax.ShapeDtypeStruct((1, D), F32), jax.ShapeDtypeStruct((1, D), F32)],
        scratch_shapes=[pltpu.VMEM((ts, D), BF16), pltpu.VMEM((ts, D), F32), pltpu.VMEM((ts, D), BF16),
                        pltpu.VMEM((ts, D), F32), pltpu.VMEM((GROUP, D), F32), pltpu.VMEM((ts, D), BF16),
                        pltpu.VMEM((ts, D), BF16), pltpu.VMEM((D, D), F32)],
        compiler_params=_params(),
    )(g2, m1, p, x1, lg, lb, wt_b, wtt_b, bfull, woutt_b, wint_b, gpost, gpre)


def _odd_backward_unfused(g2, m1, p, lg, lb, wt_b, wtt_b, bfull, woutt_b, gpost, ts):
    S = g2.shape[0]
    nt = S // ts

    def body(g_ref, m_ref, p_ref, lg_ref, lb_ref, wt_ref, wtt_ref, bfull_ref, wot_ref, gp_ref,
             dp_ref, dwob_ref, dws_ref, dbs_ref, dlg_ref, dlb_ref, dgp_ref,
             vln_ref, sv_ref, y_ref, dm_ref, dsv_ref, dsvb_ref, dvln_ref, dsum_ref, dwo_ref):
        step = pl.program_id(0)

        @pl.when(step == 0)
        def _():
            dwo_ref[...] = jnp.zeros_like(dwo_ref)
            dws_ref[...] = jnp.zeros_like(dws_ref)
            dsum_ref[...] = jnp.zeros_like(dsum_ref)
            dlg_ref[...] = jnp.zeros_like(dlg_ref)
            dlb_ref[...] = jnp.zeros_like(dlb_ref)
            dgp_ref[...] = jnp.zeros_like(dgp_ref)

        g = g_ref[...]
        m = m_ref[...]
        q = _rms(m)
        n = m * q
        dgp_ref[...] += jnp.sum(g * n, axis=0, keepdims=True)
        dn = g * gp_ref[...]
        dm = q * (dn - n * jnp.mean(dn * n, axis=-1, keepdims=True))
        dm_ref[...] = dm.astype(BF16)
        dy = _dot(dm_ref[...], wot_ref[...])

        u = p_ref[:, 0:D].astype(F32)
        z = p_ref[:, 2 * D:3 * D].astype(F32)
        vh, rstd = _odd_forward_parts(p_ref[:, D:2 * D].astype(F32), lg_ref, lb_ref, wt_ref, bfull_ref, vln_ref,
                                      sv_ref, 0, ts)
        sv = sv_ref[...]
        sig = _sigmoid(z)
        sz = z * sig
        y_ref[...] = (u * sv * sz).astype(BF16)
        dwo_ref[...] += _dot_t0(y_ref[...], dm_ref[...])

        t = dy * sz
        du = t * sv
        dsv = t * u
        dz = dy * u * sv * _dsilu(z, sig)
        dsv_ref[...] = dsv
        dsvb_ref[...] = dsv.astype(BF16)

        tril = (lax.broadcasted_iota(jnp.int32, (GROUP, GROUP), 0)
                >= lax.broadcasted_iota(jnp.int32, (GROUP, GROUP), 1))
        for c in range(ts // GROUP):
            rows = slice(c * GROUP, (c + 1) * GROUP)
            dsum_ref[...] += dsv_ref[rows, :]
            for h in range(N_HEADS):
                cols = slice(h * GROUP, (h + 1) * GROUP)
                dvln_ref[rows, cols] = _dot(wtt_ref[h], dsvb_ref[rows, cols])
                dws_ref[h] += jnp.where(tril, _dot_t1(dsvb_ref[rows, cols], vln_ref[rows, cols]), 0.0)

        dvln = dvln_ref[...]
        dlg_ref[...] += jnp.sum(dvln * vh, axis=0, keepdims=True)
        dlb_ref[...] += jnp.sum(dvln, axis=0, keepdims=True)
        dvh = dvln * lg_ref[...]
        dv = rstd * (dvh - jnp.mean(dvh, axis=-1, keepdims=True)
                     - vh * jnp.mean(dvh * vh, axis=-1, keepdims=True))
        dp_ref[:, 0:D] = du.astype(BF16)
        dp_ref[:, D:2 * D] = dv.astype(BF16)
        dp_ref[:, 2 * D:3 * D] = dz.astype(BF16)

        @pl.when(step == nt - 1)
        def _():
            dwob_ref[...] = dwo_ref[...].astype(BF16)
            ones = jnp.ones((8, GROUP), F32)
            for h in range(N_HEADS):
                cols = slice(h * GROUP, (h + 1) * GROUP)
                sums = lax.dot_general(ones, dsum_ref[:, cols], (((1,), (1,)), ((), ())),
                                       precision=lax.Precision.HIGHEST, preferred_element_type=F32)
                dbs_ref[h:h + 1, :] = sums[0:1, :]

    tile = pl.BlockSpec((ts, D), lambda i: (i, 0))
    small = _full((N_HEADS, GROUP, GROUP))
    vec = _full((1, D))
    return pl.pallas_call(
        body, name="odd_backward", grid=(nt,),
        in_specs=[tile, tile, pl.BlockSpec((ts, PROJ), lambda i: (i, 0)), vec, vec, small, small, small,
                  _full((D, D)), vec],
        out_specs=[pl.BlockSpec((ts, PROJ), lambda i: (i, 0)), _full((D, D)), small, _full((N_HEADS, GROUP)),
                   vec, vec, vec],
        out_shape=[jax.ShapeDtypeStruct((S, PROJ), BF16), jax.ShapeDtypeStruct((D, D), BF16),
                   jax.ShapeDtypeStruct((N_HEADS, GROUP, GROUP), F32), jax.ShapeDtypeStruct((N_HEADS, GROUP), F32),
                   jax.ShapeDtypeStruct((1, D), F32), jax.ShapeDtypeStruct((1, D), F32),
                   jax.ShapeDtypeStruct((1, D), F32)],
        scratch_shapes=[pltpu.VMEM((ts, D), BF16), pltpu.VMEM((ts, D), F32), pltpu.VMEM((ts, D), BF16),
                        pltpu.VMEM((ts, D), BF16), pltpu.VMEM((ts, D), F32), pltpu.VMEM((ts, D), BF16),
                        pltpu.VMEM((ts, D), F32), pltpu.VMEM((GROUP, D), F32), pltpu.VMEM((D, D), F32)],
        compiler_params=_params(),
    )(g2, m1, p, lg, lb, wt_b, wtt_b, bfull, woutt_b, gpost)


def _input_backward(dp, wt_b, x, g_res, gpre, ts, name, sides=()):
    S = x.shape[0]

    def body(dp_ref, wt_ref, x_ref, g_ref, gp_ref, dx_ref, dgp_ref):
        @pl.when(pl.program_id(0) == 0)
        def _():
            dgp_ref[...] = jnp.zeros_like(dgp_ref)

        dgp = None
        for r0 in range(0, ts, SUB_ROWS):
            rows = slice(r0, r0 + SUB_ROWS)
            dh = _dot(dp_ref[rows, :], wt_ref[...])
            xv = x_ref[rows, :]
            r = _rms(xv)
            xn = xv * r
            part = jnp.sum(dh * xn, axis=0, keepdims=True)
            dgp = part if dgp is None else dgp + part
            dxn = dh * gp_ref[...]
            dx_ref[rows, :] = g_ref[rows, :] + r * (dxn - xn * jnp.mean(dxn * xn, axis=-1, keepdims=True))
        dgp_ref[...] += dgp

    tile = pl.BlockSpec((ts, D), lambda i: (i, 0))
    return _call(
        body, name=name, grid=(S // ts,),
        in_specs=[pl.BlockSpec((ts, PROJ), lambda i: (i, 0)), _full((PROJ, D)), tile, tile, _full((1, D))],
        out_specs=[tile, _full((1, D))],
        out_shape=[jax.ShapeDtypeStruct((S, D), F32), jax.ShapeDtypeStruct((1, D), F32)],
        scratch_shapes=[], args=(dp, wt_b, x, g_res, gpre), sides=sides)


N_CHIPS = 4


def _weight_grad_scatter(h, dp, ts, name, sides=()):
    S = h.shape[0]
    ts = min(ts, S)
    nt = S // ts
    panel = 2 * SHARD_IN
    last = N_CHIPS - 1

    def body(h_ref, dp_ref, parts_ref, acc_ref, half_ref, swap_ref, sum_ref, send_sems, recv_sems):
        s, t = pl.program_id(0), pl.program_id(1)
        x, y, c = _mesh_position()
        sibling = (x, y, 1 - c)

        def swap(k):
            return _remote(half_ref.at[k, 1], swap_ref.at[k], send_sems, recv_sems, k, sibling)

        def chip_sum(k):
            flip = last - k
            to = (x ^ (flip >> 1), y ^ (flip & 1), c)
            return _remote(sum_ref.at[k], parts_ref.at[1 + flip], send_sems, recv_sems, 3 + flip, to)

        own = pltpu.make_async_copy(half_ref.at[last, 0], parts_ref.at[0], recv_sems.at[7])
        to_sibling = _remote(half_ref.at[last, 1], parts_ref.at[1], send_sems, recv_sems, 3, sibling)

        @pl.when(t == 0)
        def _():
            acc_ref[...] = jnp.zeros_like(acc_ref)

        for k in range(last):
            @pl.when((s == k + 1) & (t == 0))
            def _():
                swap(k).wait_recv()
                sum_ref[k] = (half_ref[k, 0].astype(F32) + swap_ref[k].astype(F32)).astype(BF16)
                chip_sum(k).start()

        acc_ref[...] += _dot_t0(h_ref[...], dp_ref[...])

        @pl.when(t == nt - 1)
        def _():
            for core in range(2):
                @pl.when(c == core)
                def _():
                    half_ref[s, core] = acc_ref[:, 0:SHARD_IN].astype(BF16)
                    half_ref[s, 1 - core] = acc_ref[:, SHARD_IN:panel].astype(BF16)
            for k in range(last):
                @pl.when(s == k)
                def _():
                    swap(k).start()

        @pl.when((s == last) & (t == nt - 1))
        def _():
            own.start()
            to_sibling.start()
            own.wait()
            _remote(half_ref.at[last, 1], parts_ref.at[1], send_sems, recv_sems, 3, sibling).wait_recv()
            for k in range(last):
                chip_sum(k).wait_recv()
            to_sibling.wait_send()
            for k in range(last):
                swap(k).wait_send()
                chip_sum(k).wait_send()

    chip_panel = lambda s, t: (t, (2 * lax.axis_index("x") + lax.axis_index("y")) ^ (last - s))
    sems = pltpu.SemaphoreType.DMA((8,))
    return _call(
        body, name=name, grid=(N_CHIPS, nt),
        in_specs=[pl.BlockSpec((ts, D), lambda s, t: (t, 0)), pl.BlockSpec((ts, panel), chip_panel)],
        out_specs=[HBM_SPEC], out_shape=[jax.ShapeDtypeStruct((N_CHIPS + 1, D, SHARD_IN), BF16)],
        scratch_shapes=[pltpu.VMEM((D, panel), F32), pltpu.VMEM((N_CHIPS, 2, D, SHARD_IN), BF16),
                        pltpu.VMEM((last, D, SHARD_IN), BF16), pltpu.VMEM((last, D, SHARD_IN), BF16), sems, sems],
        args=(h, dp), sides=sides)


def _adamw(w, g, m, v):
    m = ADAM_B1 * m + (1.0 - ADAM_B1) * g
    v = ADAM_B2 * v + (1.0 - ADAM_B2) * (g * g)
    m_hat = m / (1.0 - ADAM_B1 ** ADAM_STEP)
    v_hat = v / (1.0 - ADAM_B2 ** ADAM_STEP)
    delta = -ADAM_LR * (m_hat / (jnp.sqrt(v_hat) + ADAM_EPS) + ADAM_WD * w)
    return delta, m, v


def _sum_update(parts, w, m, v, name):
    R, C = w.shape
    rb = min(R, 256)
    n_parts = min(parts.shape[0], N_DEV)
    if parts.shape[0] == LANDING:
        n_parts = PARTS

    def body(p_ref, w_ref, m_ref, v_ref, g_out, d_out, m_out, v_out):
        acc = p_ref[0].astype(F32)
        for d in range(1, n_parts):
            acc = acc + p_ref[d].astype(F32)
        g_out[...] = acc
        delta, m_new, v_new = _adamw(w_ref[...], acc, m_ref[...], v_ref[...])
        d_out[...] = delta
        m_out[...] = m_new
        v_out[...] = v_new

    blk = pl.BlockSpec((rb, C), lambda i: (i, 0))
    return pl.pallas_call(
        body, name=name, grid=(R // rb,),
        in_specs=[pl.BlockSpec((n_parts, rb, C), lambda i: (0, i, 0)), blk, blk, blk], out_specs=[blk] * 4,
        out_shape=[jax.ShapeDtypeStruct((R, C), F32)] * 4, compiler_params=_params(),
    )(parts, w, m, v)


def _small_sum_update(partials, row_of, weights, loss_parts):
    n_p, n_w = len(partials), len(weights)

    def body(*refs):
        p_refs = refs[:n_p]
        loss_ref = refs[n_p]
        w_refs = refs[n_p + 1:n_p + 1 + 3 * n_w]
        loss_out = refs[n_p + 1 + 3 * n_w]
        outs = refs[n_p + 2 + 3 * n_w:]

        def total(ref):
            acc = ref[0].astype(F32)
            for d in range(1, N_DEV):
                acc = acc + ref[d].astype(F32)
            return acc

        lsum = jnp.sum(total(loss_ref), axis=-1, keepdims=True) * (0.5 / D)
        loss_out[...] = jnp.broadcast_to(lsum, loss_out.shape)
        for k in range(n_p):
            i, r0 = row_of[k]
            g = total(p_refs[k])
            w_ref, m_ref, v_ref = w_refs[3 * i:3 * i + 3]
            g_out, d_out, m_out, v_out = outs[4 * i:4 * i + 4]
            if g.ndim == 2:
                at = (slice(r0, r0 + g.shape[0]), slice(None))
            else:
                at = (slice(None),) * g.ndim
            delta, m_new, v_new = _adamw(w_ref[at], g, m_ref[at], v_ref[at])
            g_out[at] = g
            d_out[at] = delta
            m_out[at] = m_new
            v_out[at] = v_new

    vm = pl.BlockSpec(memory_space=pltpu.VMEM)
    flat_w = [a for wmv in weights for a in wmv]
    out_shape = [jax.ShapeDtypeStruct((1, 128), F32)]
    for w, _, _ in weights:
        out_shape += [jax.ShapeDtypeStruct(w.shape, F32)] * 4
    return pl.pallas_call(
        body, name="small_update", in_specs=[vm] * (n_p + 1 + 3 * n_w), out_specs=[vm] * len(out_shape),
        out_shape=out_shape, compiler_params=pltpu.CompilerParams(vmem_limit_bytes=VMEM_LIMIT),
    )(*partials, loss_parts, *flat_w)


def _pack_small_shard(cw, lg, lb):
    return (jnp.pad(cw, ((0, 13), (0, 64))) + jnp.pad(lg, ((8, 7), (0, 0))) + jnp.pad(lb, ((9, 6), (0, 0))))


TS_PRENORM = 1024
TS_FWD = 512
TS_BWD = 256
TS_WGRAD = 2048


def kernel(x, pre_norm, post_norm, even_w_in, even_conv_w, even_pool_w, even_pool_scale, even_w_out, odd_w_in, odd_ln_g, odd_ln_b, odd_w_s, odd_b_s, odd_w_out, loss_target, m_pre_norm, m_post_norm, m_even_w_in, m_even_conv_w, m_even_pool_w, m_even_pool_scale, m_even_w_out, m_odd_w_in, m_odd_ln_g, m_odd_ln_b, m_odd_w_s, m_odd_b_s, m_odd_w_out, v_pre_norm, v_post_norm, v_even_w_in, v_even_conv_w, v_even_pool_w, v_even_pool_scale, v_even_w_out, v_odd_w_in, v_odd_ln_g, v_odd_ln_b, v_odd_w_s, v_odd_b_s, v_odd_w_out):
    S = x.shape[1]
    xs = x.reshape(S, D)
    tgt = loss_target.reshape(S, D)

    gpre0, gpre1 = pre_norm[0:1], pre_norm[1:2]
    gpost0, gpost1 = post_norm[0:1], post_norm[1:2]
    small_shard = _pack_small_shard(even_conv_w[0], odd_ln_g, odd_ln_b)
    h0, wie_g, small_g, woe_g, wio_b, woo_b = _prenorm_gather(xs, gpre0, even_w_in[0], small_shard, even_w_out[0],
                                                              odd_w_in[0], odd_w_out[0], TS_PRENORM)
    wie = jnp.transpose(wie_g, (1, 0, 2)).reshape(D, PROJ)
    wie_t = jnp.transpose(wie_g, (0, 2, 1)).reshape(PROJ, D)
    woe = woe_g.reshape(D, D)
    woe_t = woe.T
    conv_w = jnp.transpose(small_g[:, 0:3, 0:64], (1, 0, 2)).reshape(3, A_W)
    ln_g = small_g[:, 8, :].reshape(1, D)
    ln_b = small_g[:, 9, :].reshape(1, D)

    pool_w_b = even_pool_w[0].astype(BF16)
    pool_wt_b = jnp.swapaxes(even_pool_w[0], 1, 2).astype(BF16)
    ws_tril = jnp.tril(odd_w_s[0])
    ws_b = ws_tril.astype(BF16)
    wst_b = jnp.swapaxes(ws_tril, 1, 2).astype(BF16)
    b_full = jnp.broadcast_to(odd_b_s[0][:, :, None], (N_HEADS, GROUP, GROUP))

    p0, m0, x1, h1, wio_g, woo_g = _even_forward_fused(h0, xs, wie, conv_w, pool_w_b, even_pool_scale, woe, gpost0,
                                                       gpre1, TS_FWD, sides=[(_Gather, [wio_b, woo_b])])
    wio = jnp.transpose(wio_g, (1, 0, 2)).reshape(D, PROJ)
    wio_t = jnp.transpose(wio_g, (0, 2, 1)).reshape(PROJ, D)
    woo = woo_g.reshape(D, D)
    woo_t = woo.T
    p1, m1, g2, loss_vec = _odd_forward(h1, x1, tgt, wio, ln_g, ln_b, ws_b, b_full, woo, gpost1, TS_FWD)

    dp1, g1, dwoo, dws, dbs, dlg, dlb, dgpost1, dgpre1 = _odd_backward(
        g2, m1, p1, x1, ln_g, ln_b, ws_b, wst_b, b_full, woo_t, wio_t, gpost1, gpre1, TS_BWD)
    by_owner = lambda dwo: dwo.reshape(N_DEV, SHARD_OUT, D)
    (dwio_parts, dwoo_parts, dlg_parts, dlb_parts,
     dws_g, dbs_g, gpost1_g, gpre1_g, loss_g) = _weight_grad_scatter(
        h1, dp1, TS_WGRAD, "odd_weight_grad",
        sides=[(_Scatter, [by_owner(dwoo), dlg.reshape(N_DEV, 1, 128), dlb.reshape(N_DEV, 1, 128)]),
               (_Gather, [dws.astype(BF16), dbs, dgpost1, dgpre1, loss_vec])])
    dp0, gx, dwoe, dcw, dpw, dps, dgpost0, dgpre0 = _even_backward_fused(
        g1, m0, p0, xs, conv_w, pool_w_b, pool_wt_b, even_pool_scale, woe_t, wie_t, gpost0, gpre0, TS_FWD)
    dcw_by_owner = jnp.transpose(dcw[0:3].reshape(3, N_DEV, 64), (1, 0, 2))
    dwie_parts, dwoe_parts, dcw_parts, dpw_g, dps_g, gpost0_g, gpre0_g = _weight_grad_scatter(
        h0, dp0, TS_WGRAD, "even_weight_grad",
        sides=[(_Scatter, [by_owner(dwoe), dcw_by_owner]), (_Gather, [dpw.astype(BF16), dps, dgpost0, dgpre0])])

    g_wie, d_wie, nm_wie, nv_wie = _sum_update(dwie_parts, even_w_in[0], m_even_w_in[0], v_even_w_in[0],
                                               "update_even_w_in")
    g_wio, d_wio, nm_wio, nv_wio = _sum_update(dwio_parts, odd_w_in[0], m_odd_w_in[0], v_odd_w_in[0],
                                               "update_odd_w_in")
    g_woe, d_woe, nm_woe, nv_woe = _sum_update(dwoe_parts, even_w_out[0], m_even_w_out[0], v_even_w_out[0],
                                               "update_even_w_out")
    g_woo, d_woo, nm_woo, nv_woo = _sum_update(dwoo_parts, odd_w_out[0], m_odd_w_out[0], v_odd_w_out[0],
                                               "update_odd_w_out")

    partials = [gpre0_g, gpre1_g, gpost0_g, gpost1_g, dpw_g, dps_g, dws_g, dbs_g, dcw_parts, dlg_parts, dlb_parts]
    row_of = [(0, 0), (0, 1), (1, 0), (1, 1), (2, 0), (3, 0), (4, 0), (5, 0), (6, 0), (7, 0), (8, 0)]
    weights = [(pre_norm, m_pre_norm, v_pre_norm), (post_norm, m_post_norm, v_post_norm),
               (even_pool_w[0], m_even_pool_w[0], v_even_pool_w[0]),
               (even_pool_scale, m_even_pool_scale, v_even_pool_scale),
               (odd_w_s[0], m_odd_w_s[0], v_odd_w_s[0]), (odd_b_s[0], m_odd_b_s[0], v_odd_b_s[0]),
               (even_conv_w[0], m_even_conv_w[0], v_even_conv_w[0]),
               (odd_ln_g, m_odd_ln_g, v_odd_ln_g), (odd_ln_b, m_odd_ln_b, v_odd_ln_b)]
    small = _small_sum_update(partials, row_of, weights, loss_g)
    loss = small[0][0, 0]

    def leaves(k, big):
        pre, post, pw, psc, ws, bs, cw, lg, lb = [small[1 + 4 * i + k] for i in range(len(weights))]
        wie_k, woe_k, wio_k, woo_k = big
        return [pre, post, wie_k[None], cw[None], pw[None], psc, woe_k[None], wio_k[None], lg, lb, ws[None], bs[None],
                woo_k[None]]

    outs = [loss, gx.reshape(1, S, D)]
    outs += leaves(0, (g_wie, g_woe, g_wio, g_woo))
    outs += leaves(1, (d_wie, d_woe, d_wio, d_woo))
    outs += leaves(2, (nm_wie, nm_woe, nm_wio, nm_woo))
    outs += leaves(3, (nv_wie, nv_woe, nv_wio, nv_woo))
    return tuple(outs)
```

```python
import functools

import jax
import jax.numpy as jnp
from jax import lax
from jax.experimental import pallas as pl
from jax.experimental.pallas import tpu as pltpu

F32 = jnp.float32
BF16 = jnp.bfloat16
MESH = pl.DeviceIdType.MESH

D = 1024
EPS = 1e-6
A_W = 512
B_W = 512
POOL_WINDOWS = (2, 4, 8, 16)
GROUP = 128
N_GROUPS = 4
N_HEADS = 8
HALO = 16
PROJ = 3072
N_DEV = 8
SHARD_IN = PROJ // N_DEV
SHARD_OUT = D // N_DEV

ADAM_LR = 0.001
ADAM_B1 = 0.9
ADAM_B2 = 0.999
ADAM_EPS = 1e-08
ADAM_WD = 0.01
ADAM_STEP = 10

VMEM_LIMIT = 54 * 1024 * 1024
VMEM_LIMIT_ODD_BACKWARD = 60 * 1024 * 1024


def _params(n_grid=1, vmem=VMEM_LIMIT):
    return pltpu.CompilerParams(dimension_semantics=("arbitrary",) * n_grid, vmem_limit_bytes=vmem)


def _full(shape):
    return pl.BlockSpec(shape, lambda *_: (0,) * len(shape))


def _sigmoid(z):
    return jax.nn.sigmoid(z)


def _dsilu(z, s):
    return s * (1.0 + z * (1.0 - s))


def _dot(a, b):
    return jnp.dot(a, b, preferred_element_type=F32)


def _dot_t0(a, b):
    return lax.dot_general(a, b, (((0,), (0,)), ((), ())), preferred_element_type=F32)


def _dot_t1(a, b):
    return lax.dot_general(a, b, (((1,), (1,)), ((), ())), preferred_element_type=F32)


def _rms(x):
    return lax.rsqrt(jnp.mean(x * x, axis=-1, keepdims=True) + EPS)


PIECE = 16


def _fold(a):
    return a[0:8] + a[8:16]


def _shift_down(a, k):
    return pltpu.roll(a, k, 0)


def _shift_up(a, k):
    return pltpu.roll(a, a.shape[0] - k, 0)


def _mesh_position():
    return lax.axis_index("x"), lax.axis_index("y"), lax.axis_index("c")


def _index(pos):
    return 4 * pos[0] + 2 * pos[1] + pos[2]


def _peer(pos, d):
    x, y, c = pos
    return (1 - x if d & 4 else x, 1 - y if d & 2 else y, 1 - c if d & 1 else c)


def _remote(src, dst, send_sems, recv_sems, k, to):
    return pltpu.make_async_remote_copy(src_ref=src, dst_ref=dst, send_sem=send_sems.at[k], recv_sem=recv_sems.at[k],
                                        device_id=to, device_id_type=MESH)


class _Gather:
    def __init__(self, srcs, dsts, send_sems, recv_sems):
        x, y, c = _mesh_position()
        n = len(srcs)
        me, sibling = (x, y, c), (x, y, 1 - c)
        chips = [(1 - x, y), (x, 1 - y), (1 - x, 1 - y)]

        def copy(k, t, block, to, src=None):
            slot = dsts[t].at[_index(block)]
            return _remote(slot if src is None else src, slot, send_sems, recv_sems, k * n + t, to)

        tensors = range(n)
        self.local = [pltpu.make_async_copy(srcs[t], dsts[t].at[_index(me)], recv_sems.at[7 * n + t]) for t in tensors]
        self.first = [copy(1 + j, t, me, chip + (c,), srcs[t]) for t in tensors for j, chip in enumerate(chips)]
        self.first += [copy(0, t, me, sibling, srcs[t]) for t in tensors]
        self.ici_in = [copy(1 + j, t, chip + (c,), me) for t in tensors for j, chip in enumerate(chips)]
        self.passed = [copy(4 + j, t, chip + (c,), sibling) for t in tensors for j, chip in enumerate(chips)]
        self.d2d_in = [copy(0, t, sibling, me) for t in tensors]
        self.d2d_in += [copy(4 + j, t, chip + (1 - c,), me) for t in tensors for j, chip in enumerate(chips)]

    def start(self):
        for cp in self.local + self.first:
            cp.start()

    def middle(self):
        for landed, onward in zip(self.ici_in, self.passed):
            landed.wait_recv()
            onward.start()

    def finish(self):
        for cp in self.d2d_in:
            cp.wait_recv()
        for cp in self.first + self.passed:
            cp.wait_send()
        for cp in self.local:
            cp.wait()


class _Scatter:
    def __init__(self, srcs, dsts, send_sems, recv_sems):
        pos = _mesh_position()
        n = len(srcs)
        self.local = [pltpu.make_async_copy(srcs[t].at[_index(pos)], dsts[t].at[0], recv_sems.at[t]) for t in range(n)]
        self.remote = []
        for d in range(1, N_DEV):
            to = _peer(pos, d)
            self.remote += [_remote(srcs[t].at[_index(to)], dsts[t].at[d], send_sems, recv_sems, d * n + t, to)
                            for t in range(n)]

    def start(self):
        for cp in self.local + self.remote:
            cp.start()

    def middle(self):
        pass

    def finish(self):
        for cp in self.local:
            cp.wait()
        for cp in self.remote:
            cp.wait_recv()
        for cp in self.remote:
            cp.wait_send()


PARTS = 5
LANDING = PARTS + 6


class _ChipSumScatter:
    def __init__(self, srcs, dsts, send_sems, recv_sems):
        x, y, c = _mesh_position()
        self.n = len(srcs)
        self.srcs, self.dsts = srcs, dsts
        me, sibling = (x, y, c), (x, y, 1 - c)
        self.chips = [(x ^ (f >> 1), y ^ (f & 1)) for f in range(1, N_CHIPS)]
        rc = lambda t, k, src, slot, to: _remote(src, dsts[t].at[slot], send_sems, recv_sems, 8 * t + k, to)
        self.local = [pltpu.make_async_copy(srcs[t].at[_index(me)], dsts[t].at[0], recv_sems.at[8 * t])
                      for t in range(self.n)]
        self.to_sibling = [rc(t, 1, srcs[t].at[_index(sibling)], 1, sibling) for t in range(self.n)]
        self.swaps = [[rc(t, 2 + j, srcs[t].at[_index(chip + (1 - c,))], 5 + j, sibling)
                       for j, chip in enumerate(self.chips)] for t in range(self.n)]
        self.sums = [[rc(t, 5 + j, dsts[t].at[8 + j], 2 + j, chip + (c,))
                      for j, chip in enumerate(self.chips)] for t in range(self.n)]
        self.mine = [[srcs[t].at[_index(chip + (c,))] for chip in self.chips] for t in range(self.n)]

    def start(self):
        for t in range(self.n):
            for cp in [self.local[t], self.to_sibling[t]] + self.swaps[t]:
                cp.start()

    def middle(self):
        for t in range(self.n):
            block = self.srcs[t].shape[1:]
            for j in range(N_CHIPS - 1):
                self.swaps[t][j].wait_recv()

                def add(a_ref, b_ref, t=t, j=j):
                    pltpu.sync_copy(self.mine[t][j], a_ref)
                    pltpu.sync_copy(self.dsts[t].at[5 + j], b_ref)
                    a_ref[...] = (a_ref[...].astype(F32) + b_ref[...].astype(F32)).astype(BF16)
                    pltpu.sync_copy(a_ref, self.dsts[t].at[8 + j])

                pl.run_scoped(add, pltpu.VMEM(block, BF16), pltpu.VMEM(block, BF16))
                self.sums[t][j].start()

    def finish(self):
        for t in range(self.n):
            self.local[t].wait()
            self.to_sibling[t].wait_recv()
            for cp in self.sums[t]:
                cp.wait_recv()
            for cp in [self.to_sibling[t]] + self.swaps[t] + self.sums[t]:
                cp.wait_send()


HBM_SPEC = pl.BlockSpec(memory_space=pltpu.HBM)
MIDDLE_STEPS_BEFORE_END = 4


def _middle_step(exchange, total):
    return min(1, total - 1) if exchange is _ChipSumScatter else max(total - MIDDLE_STEPS_BEFORE_END, 0)


def _landing_shape(exchange, src):
    if exchange is _Gather:
        return (N_DEV,) + src.shape
    if exchange is _ChipSumScatter:
        return (LANDING,) + src.shape[1:]
    return src.shape


def _call(body, *, name, grid, in_specs, out_specs, out_shape, scratch_shapes, args, sides=()):
    params = _params(len(grid))
    if not sides:
        return pl.pallas_call(body, name=name, grid=grid, in_specs=in_specs, out_specs=out_specs, out_shape=out_shape,
                              scratch_shapes=scratch_shapes, compiler_params=params)(*args)
    n_in, n_out, n_scratch = len(in_specs), len(out_specs), len(scratch_shapes)
    counts = [len(srcs) for _, srcs in sides]
    ns = sum(counts)
    total = 1
    for g in grid:
        total *= g
    side_args, side_shapes, side_sems = [], [], []
    for exchange, srcs in sides:
        side_args += list(srcs)
        side_shapes += [jax.ShapeDtypeStruct(_landing_shape(exchange, s), s.dtype) for s in srcs]
        side_sems += [pltpu.SemaphoreType.DMA((N_DEV * len(srcs),))] * 2

    def wrapped(*refs):
        ins, side_in = refs[:n_in], refs[n_in:n_in + ns]
        outs = refs[n_in + ns:n_in + ns + n_out]
        side_out = refs[n_in + ns + n_out:n_in + 2 * ns + n_out]
        rest = refs[n_in + 2 * ns + n_out:]
        scratch, sems = rest[:n_scratch], rest[n_scratch:]
        step = pl.program_id(0)
        for axis in range(1, len(grid)):
            step = step * grid[axis] + pl.program_id(axis)

        def exchanges():
            built, at = [], 0
            for k, (exchange, _) in enumerate(sides):
                built.append(exchange(side_in[at:at + counts[k]], side_out[at:at + counts[k]],
                                      sems[2 * k], sems[2 * k + 1]))
                at += counts[k]
            return built

        @pl.when(step == 0)
        def _():
            for ex in exchanges():
                ex.start()

        for k, (exchange, _) in enumerate(sides):
            @pl.when(step == _middle_step(exchange, total))
            def _(k=k):
                exchanges()[k].middle()

        body(*ins, *outs, *scratch)

        @pl.when(step == total - 1)
        def _():
            for ex in exchanges():
                ex.finish()

    return pl.pallas_call(
        wrapped, name=name, grid=grid, in_specs=list(in_specs) + [HBM_SPEC] * ns,
        out_specs=list(out_specs) + [HBM_SPEC] * ns, out_shape=list(out_shape) + side_shapes,
        scratch_shapes=list(scratch_shapes) + side_sems, compiler_params=params)(*args, *side_args)


def _prenorm_gather(x, g, wie, small, woe, wio, woo, ts):
    S = x.shape[0]
    nt = S // ts

    def body(x_ref, g_ref, wie_ref, sm_ref, woe_ref, wio_ref, woo_ref,
             h_ref, wie_out, sm_out, woe_out, wio_b, woo_b, cast_ref, small_ref, cast2_ref, send_sems, recv_sems):
        step = pl.program_id(0)
        gather = lambda: _Gather([cast_ref, small_ref, cast2_ref], [wie_out, sm_out, woe_out], send_sems, recv_sems)

        @pl.when(step == 0)
        def _():
            cast_ref[...] = wie_ref[...].astype(BF16)
            small_ref[...] = sm_ref[...]
            cast2_ref[...] = woe_ref[...].astype(BF16)
            gather().start()
            wio_b[...] = wio_ref[...].astype(BF16)
            woo_b[...] = woo_ref[...].astype(BF16)

        xv = x_ref[...]
        h_ref[...] = ((xv * _rms(xv)) * g_ref[...]).astype(BF16)

        @pl.when(step == nt - 1)
        def _():
            exchange = gather()
            exchange.middle()
            exchange.finish()

    tile = pl.BlockSpec((ts, D), lambda i: (i, 0))
    sems = pltpu.SemaphoreType.DMA((N_DEV * 3,))
    shards = [wie, small, woe, wio, woo]
    return pl.pallas_call(
        body, name="prenorm_gather", grid=(nt,),
        in_specs=[tile, _full((1, D))] + [_full(a.shape) for a in shards],
        out_specs=[tile, HBM_SPEC, HBM_SPEC, HBM_SPEC] + [_full(a.shape) for a in shards[3:]],
        out_shape=[jax.ShapeDtypeStruct((S, D), BF16), jax.ShapeDtypeStruct((N_DEV,) + wie.shape, BF16),
                   jax.ShapeDtypeStruct((N_DEV,) + small.shape, F32), jax.ShapeDtypeStruct((N_DEV,) + woe.shape, BF16)]
        + [jax.ShapeDtypeStruct(a.shape, BF16) for a in shards[3:]],
        scratch_shapes=[pltpu.VMEM(wie.shape, BF16), pltpu.VMEM(small.shape, F32), pltpu.VMEM(woe.shape, BF16),
                        sems, sems],
        compiler_params=_params(),
    )(x, g, wie, small, woe, wio, woo)


def _matmul(h, w, ts, name, sides=()):
    S = h.shape[0]
    N = w.shape[1]

    def body(h_ref, w_ref, p_ref):
        p_ref[...] = _dot(h_ref[...], w_ref[...]).astype(BF16)

    return _call(
        body, name=name, grid=(S // ts,),
        in_specs=[pl.BlockSpec((ts, D), lambda i: (i, 0)), _full((D, N))],
        out_specs=[pl.BlockSpec((ts, N), lambda i: (i, 0))],
        out_shape=[jax.ShapeDtypeStruct((S, N), BF16)],
        scratch_shapes=[], args=(h, w), sides=sides)


def _even_mixer(proj, halo, row0, cw_ref, pw_ref):
    xa, gb, gc, za, xp, zp = proj
    nr = xa.shape[0]
    ha = gc * xa
    ha_ext = jnp.concatenate([halo[0], ha], axis=0)
    ha_m1 = _shift_down(ha_ext, 1)[HALO:]
    ha_m2 = _shift_down(ha_ext, 2)[HALO:]
    conv = cw_ref[2:3, :] * ha + cw_ref[1:2, :] * ha_m1 + cw_ref[0:1, :] * ha_m2
    sig_a = _sigmoid(za)
    silu_a = za * sig_a

    xp_ext = jnp.concatenate([halo[1], xp], axis=0)
    pos = row0 + lax.broadcasted_iota(jnp.int32, (nr, 1), 0)
    pooled, inv_cnt, mixed = [], [], []
    for g, w in enumerate(POOL_WINDOWS):
        cols = slice(g * GROUP, (g + 1) * GROUP)
        s = xp_ext[:, cols]
        k = 1
        while k < w:
            s = s + _shift_down(s, k)
            k *= 2
        inv = 1.0 / jnp.minimum(pos + 1, w).astype(F32)
        pg = s[HALO:] * inv - xp[:, cols]
        pooled.append(pg)
        inv_cnt.append(inv)
        mixed.append(_dot(pg.astype(BF16), pw_ref[g]))
    mixed = jnp.concatenate(mixed, axis=1)
    sig_b = _sigmoid(zp)
    silu_b = zp * sig_b
    return dict(xa=xa, gb=gb, gc=gc, za=za, xp=xp, zp=zp, ha=ha, ha_m1=ha_m1, ha_m2=ha_m2, conv=conv,
                sig_a=sig_a, silu_a=silu_a, pooled=pooled, inv_cnt=inv_cnt, mixed=mixed, sig_b=sig_b,
                silu_b=silu_b)


def _columns(a):
    return [a[:, k * A_W:(k + 1) * A_W].astype(F32) for k in range(6)]


def _even_forward_fused(h, x, win_b, cw, pw_b, ps, wout_b, gpost, gpre_next, ts, sides=()):
    S = x.shape[0]

    def body(h_ref, x_ref, wi_ref, cw_ref, pw_ref, ps_ref, wo_ref, gp_ref, gn_ref,
             p_ref, m_ref, x1_ref, h1_ref, mix_ref, carry_ref):
        i = pl.program_id(0)

        @pl.when(i == 0)
        def _():
            carry_ref[...] = jnp.zeros_like(carry_ref)

        for r0 in range(0, ts, SUB_ROWS):
            rows = slice(r0, r0 + SUB_ROWS)
            proj = _dot(h_ref[rows, :], wi_ref[...])
            p_ref[rows, :] = proj.astype(BF16)
            fw = _even_mixer(_columns(proj), (carry_ref[:, 0:A_W], carry_ref[:, A_W:D]), i * ts + r0, cw_ref, pw_ref)
            carry_ref[:, 0:A_W] = fw["ha"][SUB_ROWS - HALO:]
            carry_ref[:, A_W:D] = fw["xp"][SUB_ROWS - HALO:]
            mix_ref[rows, 0:A_W] = (fw["gb"] * fw["conv"] * fw["silu_a"]).astype(BF16)
            mix_ref[rows, A_W:D] = (fw["mixed"] * ps_ref[...] * fw["silu_b"]).astype(BF16)
            m = _dot(mix_ref[rows, :], wo_ref[...])
            m_ref[rows, :] = m.astype(BF16)
            x1 = x_ref[rows, :] + (m * _rms(m)) * gp_ref[...]
            x1_ref[rows, :] = x1
            h1_ref[rows, :] = ((x1 * _rms(x1)) * gn_ref[...]).astype(BF16)

    tile = pl.BlockSpec((ts, D), lambda i: (i, 0))
    return _call(
        body, name="even_forward", grid=(S // ts,),
        in_specs=[tile, tile, _resident((D, PROJ)), _full((3, A_W)), _full((N_GROUPS, GROUP, GROUP)), _full((1, B_W)),
                  _resident((D, D)), _full((1, D)), _full((1, D))],
        out_specs=[pl.BlockSpec((ts, PROJ), lambda i: (i, 0)), tile, tile, tile],
        out_shape=[jax.ShapeDtypeStruct((S, PROJ), BF16), jax.ShapeDtypeStruct((S, D), BF16),
                   jax.ShapeDtypeStruct((S, D), F32), jax.ShapeDtypeStruct((S, D), BF16)],
        scratch_shapes=[pltpu.VMEM((ts, D), BF16), pltpu.VMEM((HALO, D), F32)],
        args=(h, x, win_b, cw, pw_b, ps, wout_b, gpost, gpre_next), sides=sides)


def _even_backward_fused(g1, m0, p, x, cw, pw_b, pwt_b, ps, woutt_b, wint_b, gpost, gpre, ts):
    S = g1.shape[0]
    nt = S // ts

    def body(g_ref, m_ref, p_ref, halo_ref, x_ref, cw_ref, pw_ref, pwt_ref, ps_ref, wot_ref, wit_ref, gp_ref, gn_ref,
             dp_ref, dx_ref, dwob_ref, dcw_ref, dpw_ref, dps_ref, dgp_ref, dgn_ref,
             carry_ref, mix_ref, dm_ref, dwo_ref):
        step = pl.program_id(0)
        i = nt - 1 - step

        @pl.when(step == 0)
        def _():
            carry_ref[...] = jnp.zeros_like(carry_ref)
            dwo_ref[...] = jnp.zeros_like(dwo_ref)
            dcw_ref[...] = jnp.zeros_like(dcw_ref)
            dpw_ref[...] = jnp.zeros_like(dpw_ref)
            dps_ref[...] = jnp.zeros_like(dps_ref)
            dgp_ref[...] = jnp.zeros_like(dgp_ref)
            dgn_ref[...] = jnp.zeros_like(dgn_ref)

        ps_v = ps_ref[...]
        sums = [None] * 6
        add = lambda k, part: sums.__setitem__(k, part if sums[k] is None else sums[k] + part)
        for r0 in range(ts - SUB_ROWS, -1, -SUB_ROWS):
            rows = slice(r0, r0 + SUB_ROWS)
            g = g_ref[rows, :]
            m = m_ref[rows, :].astype(F32)
            q = _rms(m)
            n = m * q
            add(0, jnp.sum(g * n, axis=0, keepdims=True))
            dn = g * gp_ref[...]
            dm = q * (dn - n * jnp.mean(dn * n, axis=-1, keepdims=True))
            dm_ref[rows, :] = dm.astype(BF16)
            dmix = _dot(dm_ref[rows, :], wot_ref[...])
            dya = dmix[:, 0:A_W]
            dyb = dmix[:, A_W:D]

            if r0 == 0:
                before = _columns(halo_ref[...])
                keep = jnp.where(i == 0, 0.0, 1.0).astype(F32)
                halo = (before[2] * before[0] * keep, before[4] * keep)
            else:
                before = _columns(p_ref[r0 - HALO:r0, :])
                halo = (before[2] * before[0], before[4])
            fw = _even_mixer(_columns(p_ref[rows, :]), halo, i * ts + r0, cw_ref, pw_ref)
            mix_ref[rows, 0:A_W] = (fw["gb"] * fw["conv"] * fw["silu_a"]).astype(BF16)
            mix_ref[rows, A_W:D] = (fw["mixed"] * ps_v * fw["silu_b"]).astype(BF16)

            t = dya * fw["gb"]
            dconv = t * fw["silu_a"]
            dgb = dya * fw["conv"] * fw["silu_a"]
            dza = t * fw["conv"] * _dsilu(fw["za"], fw["sig_a"])
            add(2, jnp.sum(dconv * fw["ha"], axis=0, keepdims=True))
            add(3, jnp.sum(dconv * fw["ha_m1"], axis=0, keepdims=True))
            add(4, jnp.sum(dconv * fw["ha_m2"], axis=0, keepdims=True))
            dconv_ext = jnp.concatenate([dconv, carry_ref[:, 0:A_W]], axis=0)
            dha = (cw_ref[2:3, :] * dconv + cw_ref[1:2, :] * _shift_up(dconv_ext, 1)[:SUB_ROWS]
                   + cw_ref[0:1, :] * _shift_up(dconv_ext, 2)[:SUB_ROWS])
            dgc = dha * fw["xa"]
            dxa = dha * fw["gc"]

            u = dyb * fw["mixed"]
            add(1, jnp.sum(u * fw["silu_b"], axis=0, keepdims=True))
            dzp = u * ps_v * _dsilu(fw["zp"], fw["sig_b"])
            dmixed = (dyb * ps_v * fw["silu_b"]).astype(BF16)
            dxp, e_first = [], []
            for gi, w in enumerate(POOL_WINDOWS):
                cols = slice(gi * GROUP, (gi + 1) * GROUP)
                dmg = dmixed[:, cols]
                dpooled = _dot(dmg, pwt_ref[gi])
                dpw_ref[gi] += _dot_t0(fw["pooled"][gi].astype(BF16), dmg)
                e = dpooled * fw["inv_cnt"][gi]
                e_first.append(e[0:HALO])
                s = jnp.concatenate([e, carry_ref[:, A_W + gi * GROUP:A_W + (gi + 1) * GROUP]], axis=0)
                k = 1
                while k < w:
                    s = s + _shift_up(s, k)
                    k *= 2
                dxp.append(s[:SUB_ROWS] - dpooled)
            carry_ref[:, 0:A_W] = dconv[0:HALO]
            carry_ref[:, A_W:D] = jnp.concatenate(e_first, axis=1)

            dp_ref[rows, 0:512] = dxa.astype(BF16)
            dp_ref[rows, 512:1024] = dgb.astype(BF16)
            dp_ref[rows, 1024:1536] = dgc.astype(BF16)
            dp_ref[rows, 1536:2048] = dza.astype(BF16)
            dp_ref[rows, 2048:2560] = jnp.concatenate(dxp, axis=1).astype(BF16)
            dp_ref[rows, 2560:3072] = dzp.astype(BF16)

            dh = _dot(dp_ref[rows, :], wit_ref[...])
            xv = x_ref[rows, :]
            r = _rms(xv)
            xn = xv * r
            add(5, jnp.sum(dh * xn, axis=0, keepdims=True))
            dxn = dh * gn_ref[...]
            dx_ref[rows, :] = g + r * (dxn - xn * jnp.mean(dxn * xn, axis=-1, keepdims=True))

        dwo_ref[...] += _dot_t0(mix_ref[...], dm_ref[...])
        dgp_ref[...] += sums[0]
        dps_ref[...] += sums[1]
        dcw_ref[2:3, :] += sums[2]
        dcw_ref[1:2, :] += sums[3]
        dcw_ref[0:1, :] += sums[4]
        dgn_ref[...] += sums[5]

        @pl.when(step == nt - 1)
        def _():
            dwob_ref[...] = dwo_ref[...].astype(BF16)

    rev = lambda s: (nt - 1 - s, 0)
    tile = pl.BlockSpec((ts, D), rev)
    wide = pl.BlockSpec((ts, PROJ), rev)
    bpt = ts // HALO
    halo_map = lambda s: (jnp.maximum((nt - 1 - s) * bpt - 1, 0), 0)
    groups = _full((N_GROUPS, GROUP, GROUP))
    vec = _full((1, D))
    return pl.pallas_call(
        body, name="even_backward", grid=(nt,),
        in_specs=[tile, tile, wide, pl.BlockSpec((HALO, PROJ), halo_map), tile, _full((3, A_W)), groups, groups,
                  _full((1, B_W)), _resident((D, D)), _resident((PROJ, D)), vec, vec],
        out_specs=[wide, tile, _full((D, D)), _full((8, A_W)), groups, _full((1, B_W)), vec, vec],
        out_shape=[jax.ShapeDtypeStruct((S, PROJ), BF16), jax.ShapeDtypeStruct((S, D), F32),
                   jax.ShapeDtypeStruct((D, D), BF16), jax.ShapeDtypeStruct((8, A_W), F32),
                   jax.ShapeDtypeStruct((N_GROUPS, GROUP, GROUP), F32), jax.ShapeDtypeStruct((1, B_W), F32),
                   jax.ShapeDtypeStruct((1, D), F32), jax.ShapeDtypeStruct((1, D), F32)],
        scratch_shapes=[pltpu.VMEM((HALO, D), F32), pltpu.VMEM((ts, D), BF16), pltpu.VMEM((ts, D), BF16),
                        pltpu.VMEM((D, D), F32)],
        compiler_params=_params(),
    )(g1, m0, p, p, x, cw, pw_b, pwt_b, ps, woutt_b, wint_b, gpost, gpre)


def _even_forward_parts(p_ref, halo_ref, is_first, row0, cw_ref, pw_ref, ts):
    f = lambda a: a.astype(F32)
    xa = f(p_ref[:, 0:512])
    gb = f(p_ref[:, 512:1024])
    gc = f(p_ref[:, 1024:1536])
    za = f(p_ref[:, 1536:2048])
    xp = f(p_ref[:, 2048:2560])
    zp = f(p_ref[:, 2560:3072])
    keep = jnp.where(is_first, 0.0, 1.0).astype(F32)
    ha = gc * xa
    ha_halo = f(halo_ref[:, 1024:1536]) * f(halo_ref[:, 0:512]) * keep
    xp_halo = f(halo_ref[:, 2048:2560]) * keep
    ha_ext = jnp.concatenate([ha_halo, ha], axis=0)
    ha_m1 = _shift_down(ha_ext, 1)[HALO:]
    ha_m2 = _shift_down(ha_ext, 2)[HALO:]
    conv = cw_ref[2:3, :] * ha + cw_ref[1:2, :] * ha_m1 + cw_ref[0:1, :] * ha_m2
    sig_a = _sigmoid(za)
    silu_a = za * sig_a

    xp_ext = jnp.concatenate([xp_halo, xp], axis=0)
    pos = row0 + lax.broadcasted_iota(jnp.int32, (ts, 1), 0)
    pooled, inv_cnt, mixed = [], [], []
    for g, w in enumerate(POOL_WINDOWS):
        cols = slice(g * GROUP, (g + 1) * GROUP)
        s = xp_ext[:, cols]
        k = 1
        while k < w:
            s = s + _shift_down(s, k)
            k *= 2
        inv = 1.0 / jnp.minimum(pos + 1, w).astype(F32)
        pg = s[HALO:] * inv - xp[:, cols]
        pooled.append(pg)
        inv_cnt.append(inv)
        mixed.append(_dot(pg.astype(BF16), pw_ref[g]))
    mixed = jnp.concatenate(mixed, axis=1)
    sig_b = _sigmoid(zp)
    silu_b = zp * sig_b
    return dict(xa=xa, gb=gb, gc=gc, za=za, xp=xp, zp=zp, ha=ha, ha_m1=ha_m1, ha_m2=ha_m2, conv=conv,
                sig_a=sig_a, silu_a=silu_a, pooled=pooled, inv_cnt=inv_cnt, mixed=mixed, sig_b=sig_b,
                silu_b=silu_b)


def _halo_index(ts):
    blocks_per_tile = ts // HALO
    return lambda i: (jnp.maximum(i * blocks_per_tile - 1, 0), 0)


def _even_forward(p, x, cw, pw_b, ps, wout_b, gpost, gpre_next, ts, sides=()):
    S = x.shape[0]

    def body(p_ref, halo_ref, x_ref, cw_ref, pw_ref, ps_ref, wo_ref, gp_ref, gn_ref, m_ref, x1_ref, h1_ref, mix_ref):
        i = pl.program_id(0)
        fw = _even_forward_parts(p_ref, halo_ref, i == 0, i * ts, cw_ref, pw_ref, ts)
        mix_ref[:, 0:A_W] = (fw["gb"] * fw["conv"] * fw["silu_a"]).astype(BF16)
        mix_ref[:, A_W:D] = (fw["mixed"] * ps_ref[...] * fw["silu_b"]).astype(BF16)
        m = _dot(mix_ref[...], wo_ref[...])
        m_ref[...] = m
        x1 = x_ref[...] + (m * _rms(m)) * gp_ref[...]
        x1_ref[...] = x1
        h1_ref[...] = ((x1 * _rms(x1)) * gn_ref[...]).astype(BF16)

    tile = pl.BlockSpec((ts, D), lambda i: (i, 0))
    return _call(
        body, name="even_forward", grid=(S // ts,),
        in_specs=[pl.BlockSpec((ts, PROJ), lambda i: (i, 0)), pl.BlockSpec((HALO, PROJ), _halo_index(ts)), tile,
                  _full((3, A_W)), _full((N_GROUPS, GROUP, GROUP)), _full((1, B_W)), _full((D, D)), _full((1, D)),
                  _full((1, D))],
        out_specs=[tile, tile, tile],
        out_shape=[jax.ShapeDtypeStruct((S, D), F32), jax.ShapeDtypeStruct((S, D), F32),
                   jax.ShapeDtypeStruct((S, D), BF16)],
        scratch_shapes=[pltpu.VMEM((ts, D), BF16)],
        args=(p, p, x, cw, pw_b, ps, wout_b, gpost, gpre_next), sides=sides)


def _even_backward(g1, m0, p, cw, pw_b, pwt_b, ps, woutt_b, gpost, ts, sides=()):
    S = g1.shape[0]
    nt = S // ts

    def body(g_ref, m_ref, p_ref, halo_ref, cw_ref, pw_ref, pwt_ref, ps_ref, wot_ref, gp_ref,
             dp_ref, dwob_ref, dcw_ref, dpw_ref, dps_ref, dgp_ref, carry_ref, mix_ref, dm_ref, dwo_ref):
        step = pl.program_id(0)
        i = nt - 1 - step

        @pl.when(step == 0)
        def _():
            carry_ref[...] = jnp.zeros_like(carry_ref)
            dwo_ref[...] = jnp.zeros_like(dwo_ref)
            dcw_ref[...] = jnp.zeros_like(dcw_ref)
            dpw_ref[...] = jnp.zeros_like(dpw_ref)
            dps_ref[...] = jnp.zeros_like(dps_ref)
            dgp_ref[...] = jnp.zeros_like(dgp_ref)

        g = g_ref[...]
        m = m_ref[...]
        q = _rms(m)
        n = m * q
        dgp_ref[...] += jnp.sum(g * n, axis=0, keepdims=True)
        dn = g * gp_ref[...]
        dm = q * (dn - n * jnp.mean(dn * n, axis=-1, keepdims=True))
        dm_ref[...] = dm.astype(BF16)
        dmix = _dot(dm_ref[...], wot_ref[...])
        dya = dmix[:, 0:A_W]
        dyb = dmix[:, A_W:D]

        fw = _even_forward_parts(p_ref, halo_ref, i == 0, i * ts, cw_ref, pw_ref, ts)
        ps_v = ps_ref[...]
        mix_ref[:, 0:A_W] = (fw["gb"] * fw["conv"] * fw["silu_a"]).astype(BF16)
        mix_ref[:, A_W:D] = (fw["mixed"] * ps_v * fw["silu_b"]).astype(BF16)
        dwo_ref[...] += _dot_t0(mix_ref[...], dm_ref[...])

        t = dya * fw["gb"]
        dconv = t * fw["silu_a"]
        dgb = dya * fw["conv"] * fw["silu_a"]
        dza = t * fw["conv"] * _dsilu(fw["za"], fw["sig_a"])
        dcw_ref[2:3, :] += jnp.sum(dconv * fw["ha"], axis=0, keepdims=True)
        dcw_ref[1:2, :] += jnp.sum(dconv * fw["ha_m1"], axis=0, keepdims=True)
        dcw_ref[0:1, :] += jnp.sum(dconv * fw["ha_m2"], axis=0, keepdims=True)
        dconv_ext = jnp.concatenate([dconv, carry_ref[:, 0:A_W]], axis=0)
        dha = (cw_ref[2:3, :] * dconv + cw_ref[1:2, :] * _shift_up(dconv_ext, 1)[:ts]
               + cw_ref[0:1, :] * _shift_up(dconv_ext, 2)[:ts])
        dgc = dha * fw["xa"]
        dxa = dha * fw["gc"]

        u = dyb * fw["mixed"]
        dps_ref[...] += jnp.sum(u * fw["silu_b"], axis=0, keepdims=True)
        dzp = u * ps_v * _dsilu(fw["zp"], fw["sig_b"])
        dmixed = (dyb * ps_v * fw["silu_b"]).astype(BF16)
        dxp, e_first = [], []
        for gi, w in enumerate(POOL_WINDOWS):
            cols = slice(gi * GROUP, (gi + 1) * GROUP)
            dmg = dmixed[:, cols]
            dpooled = _dot(dmg, pwt_ref[gi])
            dpw_ref[gi] += _dot_t0(fw["pooled"][gi].astype(BF16), dmg)
            e = dpooled * fw["inv_cnt"][gi]
            e_first.append(e[0:HALO])
            s = jnp.concatenate([e, carry_ref[:, A_W + gi * GROUP:A_W + (gi + 1) * GROUP]], axis=0)
            k = 1
            while k < w:
                s = s + _shift_up(s, k)
                k *= 2
            dxp.append(s[:ts] - dpooled)

        carry_ref[:, 0:A_W] = dconv[0:HALO]
        carry_ref[:, A_W:D] = jnp.concatenate(e_first, axis=1)

        dp_ref[:, 0:512] = dxa.astype(BF16)
        dp_ref[:, 512:1024] = dgb.astype(BF16)
        dp_ref[:, 1024:1536] = dgc.astype(BF16)
        dp_ref[:, 1536:2048] = dza.astype(BF16)
        dp_ref[:, 2048:2560] = jnp.concatenate(dxp, axis=1).astype(BF16)
        dp_ref[:, 2560:3072] = dzp.astype(BF16)

        @pl.when(step == nt - 1)
        def _():
            dwob_ref[...] = dwo_ref[...].astype(BF16)

    rev = lambda s: (nt - 1 - s, 0)
    tile = pl.BlockSpec((ts, D), rev)
    bpt = ts // HALO
    halo_map = lambda s: (jnp.maximum((nt - 1 - s) * bpt - 1, 0), 0)
    return _call(
        body, name="even_backward", grid=(nt,),
        in_specs=[tile, tile, pl.BlockSpec((ts, PROJ), rev), pl.BlockSpec((HALO, PROJ), halo_map),
                  _full((3, A_W)), _full((N_GROUPS, GROUP, GROUP)), _full((N_GROUPS, GROUP, GROUP)),
                  _full((1, B_W)), _full((D, D)), _full((1, D))],
        out_specs=[pl.BlockSpec((ts, PROJ), rev), _full((D, D)), _full((8, A_W)),
                   _full((N_GROUPS, GROUP, GROUP)), _full((1, B_W)), _full((1, D))],
        out_shape=[jax.ShapeDtypeStruct((S, PROJ), BF16), jax.ShapeDtypeStruct((D, D), BF16),
                   jax.ShapeDtypeStruct((8, A_W), F32), jax.ShapeDtypeStruct((N_GROUPS, GROUP, GROUP), F32),
                   jax.ShapeDtypeStruct((1, B_W), F32), jax.ShapeDtypeStruct((1, D), F32)],
        scratch_shapes=[pltpu.VMEM((HALO, D), F32), pltpu.VMEM((ts, D), BF16), pltpu.VMEM((ts, D), BF16),
                        pltpu.VMEM((D, D), F32)],
        args=(g1, m0, p, p, cw, pw_b, pwt_b, ps, woutt_b, gpost), sides=sides)


def _odd_forward_parts(v, lg_ref, lb_ref, wt_ref, bfull_ref, vln_ref, sv_ref, r0, nr):
    rows = slice(r0, r0 + nr)
    mu = jnp.mean(v, axis=-1, keepdims=True)
    vc = v - mu
    rstd = lax.rsqrt(jnp.mean(vc * vc, axis=-1, keepdims=True) + EPS)
    vh = vc * rstd
    vln_ref[rows, :] = (vh * lg_ref[...] + lb_ref[...]).astype(BF16)
    _gate_matmuls(wt_ref, vln_ref, sv_ref, [slice(r0 + k * GROUP, r0 + (k + 1) * GROUP) for k in range(nr // GROUP)],
                  bfull_ref)
    return vh, rstd


def _gate_matmuls(w_ref, in_ref, out_ref, chunks, bias_ref=None):
    for h in range(N_HEADS):
        cols = slice(h * GROUP, (h + 1) * GROUP)
        wide = jnp.concatenate([in_ref[chunk, cols] for chunk in chunks], axis=1)
        res = _dot(w_ref[h], wide)
        for k, chunk in enumerate(chunks):
            part = res[:, k * GROUP:(k + 1) * GROUP]
            out_ref[chunk, cols] = part if bias_ref is None else part + bias_ref[h]


SUB_ROWS = 256


def _odd_forward(h, x1, tgt, win_b, lg, lb, wt_b, bfull, wout_b, gpost, ts):
    S = x1.shape[0]

    def body(h_ref, x_ref, t_ref, wi_ref, lg_ref, lb_ref, wt_ref, bfull_ref, wo_ref, gp_ref,
             p_ref, m_ref, g_ref, loss_ref, vln_ref, sv_ref, y_ref):
        @pl.when(pl.program_id(0) == 0)
        def _():
            loss_ref[...] = jnp.zeros_like(loss_ref)

        loss = None
        for r0 in range(0, ts, SUB_ROWS):
            rows = slice(r0, r0 + SUB_ROWS)
            proj = _dot(h_ref[rows, :], wi_ref[...])
            p_ref[rows, :] = proj.astype(BF16)
            u, v, z = proj[:, 0:D], proj[:, D:2 * D], proj[:, 2 * D:3 * D]
            _odd_forward_parts(v, lg_ref, lb_ref, wt_ref, bfull_ref, vln_ref, sv_ref, r0, SUB_ROWS)
            y_ref[rows, :] = (u * sv_ref[rows, :] * (z * _sigmoid(z))).astype(BF16)
            m = _dot(y_ref[rows, :], wo_ref[...])
            m_ref[rows, :] = m.astype(BF16)
            x2 = x_ref[rows, :] + (m * _rms(m)) * gp_ref[...]
            err = x2 - t_ref[rows, :]
            g_ref[rows, :] = err * (1.0 / D)
            part = jnp.sum(err * err, axis=0, keepdims=True)
            loss = part if loss is None else loss + part
        loss_ref[...] += loss

    tile = pl.BlockSpec((ts, D), lambda i: (i, 0))
    small = _full((N_HEADS, GROUP, GROUP))
    return pl.pallas_call(
        body, name="odd_forward", grid=(S // ts,),
        in_specs=[pl.BlockSpec((ts, D), lambda i: (i, 0)), tile, tile, _full((D, PROJ)), _full((1, D)), _full((1, D)),
                  small, small, _full((D, D)), _full((1, D))],
        out_specs=[pl.BlockSpec((ts, PROJ), lambda i: (i, 0)), tile, tile, _full((1, D))],
        out_shape=[jax.ShapeDtypeStruct((S, PROJ), BF16), jax.ShapeDtypeStruct((S, D), BF16),
                   jax.ShapeDtypeStruct((S, D), F32), jax.ShapeDtypeStruct((1, D), F32)],
        scratch_shapes=[pltpu.VMEM((ts, D), BF16), pltpu.VMEM((ts, D), F32), pltpu.VMEM((ts, D), BF16)],
        compiler_params=_params(),
    )(h, x1, tgt, win_b, lg, lb, wt_b, bfull, wout_b, gpost)


def _resident(shape):
    return pl.BlockSpec(shape, lambda *_: (0,) * len(shape), pipeline_mode=pl.Buffered(1))


def _odd_backward(g2, m1, p, x1, lg, lb, wt_b, wtt_b, bfull, woutt_b, wint_b, gpost, gpre, ts):
    S = g2.shape[0]
    nt = S // ts
    sub = ts // 2

    def body(g_ref, m_ref, p_ref, x_ref, lg_ref, lb_ref, wt_ref, wtt_ref, bfull_ref, wot_ref, wit_ref, gp_ref, gn_ref,
             dp_ref, dx_ref, dwob_ref, dws_ref, dbs_ref, dlg_ref, dlb_ref, dgp_ref, dgn_ref,
             vln_ref, sv_ref, dsvb_ref, dvln_ref, dsum_ref, y_ref, dm_ref, dwo_ref):
        step = pl.program_id(0)

        @pl.when(step == 0)
        def _():
            dwo_ref[...] = jnp.zeros_like(dwo_ref)
            dws_ref[...] = jnp.zeros_like(dws_ref)
            dsum_ref[...] = jnp.zeros_like(dsum_ref)
            dlg_ref[...] = jnp.zeros_like(dlg_ref)
            dlb_ref[...] = jnp.zeros_like(dlb_ref)
            dgp_ref[...] = jnp.zeros_like(dgp_ref)
            dgn_ref[...] = jnp.zeros_like(dgn_ref)

        tril = (lax.broadcasted_iota(jnp.int32, (GROUP, GROUP), 0)
                >= lax.broadcasted_iota(jnp.int32, (GROUP, GROUP), 1))
        sums = [None] * 4
        add = lambda k, part: sums.__setitem__(k, part if sums[k] is None else sums[k] + part)
        def post_norm(c):
            gp = gp_ref[...]
            for rows in c["pieces"]:
                g = g_ref[rows, :]
                m = m_ref[rows, :].astype(F32)
                q = _rms(m)
                n = m * q
                add(0, _fold(g * n))
                dn = g * gp
                dm = q * (dn - n * jnp.mean(dn * n, axis=-1, keepdims=True))
                dm_ref[rows, :] = dm.astype(BF16)

        def out_projection(c):
            c["dy"] = _dot(dm_ref[c["rows"], :], wot_ref[...])

        def layernorm(c):
            v = p_ref[c["rows"], D:2 * D].astype(F32)
            mu = jnp.mean(v, axis=-1, keepdims=True)
            vc = v - mu
            c["rstd"] = lax.rsqrt(jnp.mean(vc * vc, axis=-1, keepdims=True) + EPS)
            c["vh"] = vc * c["rstd"]
            vln_ref[c["rows"], :] = (c["vh"] * lg_ref[...] + lb_ref[...]).astype(BF16)

        def gate_matmuls(c):
            for chunk in c["chunks"]:
                for h in range(N_HEADS):
                    cols = slice(h * GROUP, (h + 1) * GROUP)
                    sv_ref[chunk, cols] = _dot(wt_ref[h], vln_ref[chunk, cols]) + bfull_ref[h]

        def gating(c):
            rows = c["rows"]
            u = p_ref[rows, 0:D].astype(F32)
            z = p_ref[rows, 2 * D:3 * D].astype(F32)
            sv = sv_ref[rows, :]
            sig = _sigmoid(z)
            sz = z * sig
            y_ref[rows, :] = (u * sv * sz).astype(BF16)
            dy = c.pop("dy")
            t = dy * sz
            dsv = t * u
            dsvb_ref[rows, :] = dsv.astype(BF16)
            for k in range(sub // GROUP):
                dsum_ref[...] += dsv[k * GROUP:(k + 1) * GROUP]
            dp_ref[rows, 0:D] = (t * sv).astype(BF16)
            dp_ref[rows, 2 * D:3 * D] = (dy * u * sv * _dsilu(z, sig)).astype(BF16)

        def gate_backward_matmuls(c):
            for chunk in c["chunks"]:
                for h in range(N_HEADS):
                    cols = slice(h * GROUP, (h + 1) * GROUP)
                    dvln_ref[chunk, cols] = _dot(wtt_ref[h], dsvb_ref[chunk, cols])
                    dws_ref[h] += jnp.where(tril, _dot_t1(dsvb_ref[chunk, cols], vln_ref[chunk, cols]), 0.0)

        def layernorm_backward(c):
            vh, rstd = c.pop("vh"), c.pop("rstd")
            dvln = dvln_ref[c["rows"], :]
            add(1, jnp.sum(dvln * vh, axis=0, keepdims=True))
            add(2, jnp.sum(dvln, axis=0, keepdims=True))
            dvh = dvln * lg_ref[...]
            dv = rstd * (dvh - jnp.mean(dvh, axis=-1, keepdims=True)
                         - vh * jnp.mean(dvh * vh, axis=-1, keepdims=True))
            dp_ref[c["rows"], D:2 * D] = dv.astype(BF16)

        def in_projection(c):
            c["dh"] = _dot(dp_ref[c["rows"], :], wit_ref[...])

        def pre_norm(c):
            dh_all = c.pop("dh")
            gn = gn_ref[...]
            for k, rows in enumerate(c["pieces"]):
                dh = dh_all[k * PIECE:(k + 1) * PIECE]
                xv = x_ref[rows, :]
                r = _rms(xv)
                xn = xv * r
                add(3, _fold(dh * xn))
                dxn = dh * gn
                dx_ref[rows, :] = g_ref[rows, :] + r * (dxn - xn * jnp.mean(dxn * xn, axis=-1, keepdims=True))

        phases = [post_norm, out_projection, layernorm, gate_matmuls, gating, gate_backward_matmuls,
                  layernorm_backward, in_projection, pre_norm]
        groups = [dict(rows=slice(r0, r0 + sub),
                       chunks=[slice(r0 + k * GROUP, r0 + (k + 1) * GROUP) for k in range(sub // GROUP)],
                       pieces=[slice(r0 + k, r0 + k + PIECE) for k in range(0, sub, PIECE)])
                  for r0 in range(0, ts, sub)]
        for group in groups:
            for phase in phases:
                phase(group)

        dwo_ref[...] += _dot_t0(y_ref[...], dm_ref[...])
        dgp_ref[...] += jnp.sum(sums[0], axis=0, keepdims=True)
        dlg_ref[...] += jnp.sum(sums[1], axis=0, keepdims=True)
        dlb_ref[...] += jnp.sum(sums[2], axis=0, keepdims=True)
        dgn_ref[...] += jnp.sum(sums[3], axis=0, keepdims=True)

        @pl.when(step == nt - 1)
        def _():
            dwob_ref[...] = dwo_ref[...].astype(BF16)
            ones = jnp.ones((8, GROUP), F32)
            for h in range(N_HEADS):
                cols = slice(h * GROUP, (h + 1) * GROUP)
                total = lax.dot_general(ones, dsum_ref[:, cols], (((1,), (1,)), ((), ())),
                                        precision=lax.Precision.HIGHEST, preferred_element_type=F32)
                dbs_ref[h:h + 1, :] = total[0:1, :]

    tile = pl.BlockSpec((ts, D), lambda i: (i, 0))
    wide = pl.BlockSpec((ts, PROJ), lambda i: (i, 0))
    small = _full((N_HEADS, GROUP, GROUP))
    heads = _resident((N_HEADS, GROUP, GROUP))
    vec = _full((1, D))
    return pl.pallas_call(
        body, name="odd_backward", grid=(nt,),
        in_specs=[tile, tile, wide, tile, vec, vec, heads, heads, heads, _resident((D, D)), _resident((PROJ, D)),
                  vec, vec],
        out_specs=[wide, tile, _full((D, D)), small, _full((N_HEADS, GROUP)), vec, vec, vec, vec],
        out_shape=[jax.ShapeDtypeStruct((S, PROJ), BF16), jax.ShapeDtypeStruct((S, D), F32),
                   jax.ShapeDtypeStruct((D, D), BF16),
                   jax.ShapeDtypeStruct((N_HEADS, GROUP, GROUP), F32), jax.ShapeDtypeStruct((N_HEADS, GROUP), F32),
                   jax.ShapeDtypeStruct((1, D), F32), jax.ShapeDtypeStruct((1, D), F32),
                   jax.ShapeDtypeStruct((1, D), F32), jax.ShapeDtypeStruct((1, D), F32)],
        scratch_shapes=[pltpu.VMEM((ts, D), BF16), pltpu.VMEM((ts, D), F32), pltpu.VMEM((ts, D), BF16),
                        pltpu.VMEM((ts, D), F32), pltpu.VMEM((GROUP, D), F32), pltpu.VMEM((ts, D), BF16),
                        pltpu.VMEM((ts, D), BF16), pltpu.VMEM((D, D), F32)],
        compiler_params=_params(1, VMEM_LIMIT_ODD_BACKWARD),
    )(g2, m1, p, x1, lg, lb, wt_b, wtt_b, bfull, woutt_b, wint_b, gpost, gpre)


def _odd_backward_unfused(g2, m1, p, lg, lb, wt_b, wtt_b, bfull, woutt_b, gpost, ts):
    S = g2.shape[0]
    nt = S // ts

    def body(g_ref, m_ref, p_ref, lg_ref, lb_ref, wt_ref, wtt_ref, bfull_ref, wot_ref, gp_ref,
             dp_ref, dwob_ref, dws_ref, dbs_ref, dlg_ref, dlb_ref, dgp_ref,
             vln_ref, sv_ref, y_ref, dm_ref, dsv_ref, dsvb_ref, dvln_ref, dsum_ref, dwo_ref):
        step = pl.program_id(0)

        @pl.when(step == 0)
        def _():
            dwo_ref[...] = jnp.zeros_like(dwo_ref)
            dws_ref[...] = jnp.zeros_like(dws_ref)
            dsum_ref[...] = jnp.zeros_like(dsum_ref)
            dlg_ref[...] = jnp.zeros_like(dlg_ref)
            dlb_ref[...] = jnp.zeros_like(dlb_ref)
            dgp_ref[...] = jnp.zeros_like(dgp_ref)

        g = g_ref[...]
        m = m_ref[...]
        q = _rms(m)
        n = m * q
        dgp_ref[...] += jnp.sum(g * n, axis=0, keepdims=True)
        dn = g * gp_ref[...]
        dm = q * (dn - n * jnp.mean(dn * n, axis=-1, keepdims=True))
        dm_ref[...] = dm.astype(BF16)
        dy = _dot(dm_ref[...], wot_ref[...])

        u = p_ref[:, 0:D].astype(F32)
        z = p_ref[:, 2 * D:3 * D].astype(F32)
        vh, rstd = _odd_forward_parts(p_ref[:, D:2 * D].astype(F32), lg_ref, lb_ref, wt_ref, bfull_ref, vln_ref,
                                      sv_ref, 0, ts)
        sv = sv_ref[...]
        sig = _sigmoid(z)
        sz = z * sig
        y_ref[...] = (u * sv * sz).astype(BF16)
        dwo_ref[...] += _dot_t0(y_ref[...], dm_ref[...])

        t = dy * sz
        du = t * sv
        dsv = t * u
        dz = dy * u * sv * _dsilu(z, sig)
        dsv_ref[...] = dsv
        dsvb_ref[...] = dsv.astype(BF16)

        tril = (lax.broadcasted_iota(jnp.int32, (GROUP, GROUP), 0)
                >= lax.broadcasted_iota(jnp.int32, (GROUP, GROUP), 1))
        for c in range(ts // GROUP):
            rows = slice(c * GROUP, (c + 1) * GROUP)
            dsum_ref[...] += dsv_ref[rows, :]
            for h in range(N_HEADS):
                cols = slice(h * GROUP, (h + 1) * GROUP)
                dvln_ref[rows, cols] = _dot(wtt_ref[h], dsvb_ref[rows, cols])
                dws_ref[h] += jnp.where(tril, _dot_t1(dsvb_ref[rows, cols], vln_ref[rows, cols]), 0.0)

        dvln = dvln_ref[...]
        dlg_ref[...] += jnp.sum(dvln * vh, axis=0, keepdims=True)
        dlb_ref[...] += jnp.sum(dvln, axis=0, keepdims=True)
        dvh = dvln * lg_ref[...]
        dv = rstd * (dvh - jnp.mean(dvh, axis=-1, keepdims=True)
                     - vh * jnp.mean(dvh * vh, axis=-1, keepdims=True))
        dp_ref[:, 0:D] = du.astype(BF16)
        dp_ref[:, D:2 * D] = dv.astype(BF16)
        dp_ref[:, 2 * D:3 * D] = dz.astype(BF16)

        @pl.when(step == nt - 1)
        def _():
            dwob_ref[...] = dwo_ref[...].astype(BF16)
            ones = jnp.ones((8, GROUP), F32)
            for h in range(N_HEADS):
                cols = slice(h * GROUP, (h + 1) * GROUP)
                sums = lax.dot_general(ones, dsum_ref[:, cols], (((1,), (1,)), ((), ())),
                                       precision=lax.Precision.HIGHEST, preferred_element_type=F32)
                dbs_ref[h:h + 1, :] = sums[0:1, :]

    tile = pl.BlockSpec((ts, D), lambda i: (i, 0))
    small = _full((N_HEADS, GROUP, GROUP))
    vec = _full((1, D))
    return pl.pallas_call(
        body, name="odd_backward", grid=(nt,),
        in_specs=[tile, tile, pl.BlockSpec((ts, PROJ), lambda i: (i, 0)), vec, vec, small, small, small,
                  _full((D, D)), vec],
        out_specs=[pl.BlockSpec((ts, PROJ), lambda i: (i, 0)), _full((D, D)), small, _full((N_HEADS, GROUP)),
                   vec, vec, vec],
        out_shape=[jax.ShapeDtypeStruct((S, PROJ), BF16), jax.ShapeDtypeStruct((D, D), BF16),
                   jax.ShapeDtypeStruct((N_HEADS, GROUP, GROUP), F32), jax.ShapeDtypeStruct((N_HEADS, GROUP), F32),
                   jax.ShapeDtypeStruct((1, D), F32), jax.ShapeDtypeStruct((1, D), F32),
                   jax.ShapeDtypeStruct((1, D), F32)],
        scratch_shapes=[pltpu.VMEM((ts, D), BF16), pltpu.VMEM((ts, D), F32), pltpu.VMEM((ts, D), BF16),
                        pltpu.VMEM((ts, D), BF16), pltpu.VMEM((ts, D), F32), pltpu.VMEM((ts, D), BF16),
                        pltpu.VMEM((ts, D), F32), pltpu.VMEM((GROUP, D), F32), pltpu.VMEM((D, D), F32)],
        compiler_params=_params(),
    )(g2, m1, p, lg, lb, wt_b, wtt_b, bfull, woutt_b, gpost)


def _input_backward(dp, wt_b, x, g_res, gpre, ts, name, sides=()):
    S = x.shape[0]

    def body(dp_ref, wt_ref, x_ref, g_ref, gp_ref, dx_ref, dgp_ref):
        @pl.when(pl.program_id(0) == 0)
        def _():
            dgp_ref[...] = jnp.zeros_like(dgp_ref)

        dgp = None
        for r0 in range(0, ts, SUB_ROWS):
            rows = slice(r0, r0 + SUB_ROWS)
            dh = _dot(dp_ref[rows, :], wt_ref[...])
            xv = x_ref[rows, :]
            r = _rms(xv)
            xn = xv * r
            part = jnp.sum(dh * xn, axis=0, keepdims=True)
            dgp = part if dgp is None else dgp + part
            dxn = dh * gp_ref[...]
            dx_ref[rows, :] = g_ref[rows, :] + r * (dxn - xn * jnp.mean(dxn * xn, axis=-1, keepdims=True))
        dgp_ref[...] += dgp

    tile = pl.BlockSpec((ts, D), lambda i: (i, 0))
    return _call(
        body, name=name, grid=(S // ts,),
        in_specs=[pl.BlockSpec((ts, PROJ), lambda i: (i, 0)), _full((PROJ, D)), tile, tile, _full((1, D))],
        out_specs=[tile, _full((1, D))],
        out_shape=[jax.ShapeDtypeStruct((S, D), F32), jax.ShapeDtypeStruct((1, D), F32)],
        scratch_shapes=[], args=(dp, wt_b, x, g_res, gpre), sides=sides)


N_CHIPS = 4


def _weight_grad_scatter(h, dp, ts, name, sides=()):
    S = h.shape[0]
    ts = min(ts, S)
    nt = S // ts
    panel = 2 * SHARD_IN
    last = N_CHIPS - 1

    def body(h_ref, dp_ref, parts_ref, acc_ref, half_ref, swap_ref, sum_ref, send_sems, recv_sems):
        s, t = pl.program_id(0), pl.program_id(1)
        x, y, c = _mesh_position()
        sibling = (x, y, 1 - c)

        def swap(k):
            return _remote(half_ref.at[k, 1], swap_ref.at[k], send_sems, recv_sems, k, sibling)

        def chip_sum(k):
            flip = last - k
            to = (x ^ (flip >> 1), y ^ (flip & 1), c)
            return _remote(sum_ref.at[k], parts_ref.at[1 + flip], send_sems, recv_sems, 3 + flip, to)

        own = pltpu.make_async_copy(half_ref.at[last, 0], parts_ref.at[0], recv_sems.at[7])
        to_sibling = _remote(half_ref.at[last, 1], parts_ref.at[1], send_sems, recv_sems, 3, sibling)

        @pl.when(t == 0)
        def _():
            acc_ref[...] = jnp.zeros_like(acc_ref)

        for k in range(last):
            @pl.when((s == k + 1) & (t == 0))
            def _():
                swap(k).wait_recv()
                sum_ref[k] = (half_ref[k, 0].astype(F32) + swap_ref[k].astype(F32)).astype(BF16)
                chip_sum(k).start()

        acc_ref[...] += _dot_t0(h_ref[...], dp_ref[...])

        @pl.when(t == nt - 1)
        def _():
            for core in range(2):
                @pl.when(c == core)
                def _():
                    half_ref[s, core] = acc_ref[:, 0:SHARD_IN].astype(BF16)
                    half_ref[s, 1 - core] = acc_ref[:, SHARD_IN:panel].astype(BF16)
            for k in range(last):
                @pl.when(s == k)
                def _():
                    swap(k).start()

        @pl.when((s == last) & (t == nt - 1))
        def _():
            own.start()
            to_sibling.start()
            own.wait()
            _remote(half_ref.at[last, 1], parts_ref.at[1], send_sems, recv_sems, 3, sibling).wait_recv()
            for k in range(last):
                chip_sum(k).wait_recv()
            to_sibling.wait_send()
            for k in range(last):
                swap(k).wait_send()
                chip_sum(k).wait_send()

    chip_panel = lambda s, t: (t, (2 * lax.axis_index("x") + lax.axis_index("y")) ^ (last - s))
    sems = pltpu.SemaphoreType.DMA((8,))
    return _call(
        body, name=name, grid=(N_CHIPS, nt),
        in_specs=[pl.BlockSpec((ts, D), lambda s, t: (t, 0)), pl.BlockSpec((ts, panel), chip_panel)],
        out_specs=[HBM_SPEC], out_shape=[jax.ShapeDtypeStruct((N_CHIPS + 1, D, SHARD_IN), BF16)],
        scratch_shapes=[pltpu.VMEM((D, panel), F32), pltpu.VMEM((N_CHIPS, 2, D, SHARD_IN), BF16),
                        pltpu.VMEM((last, D, SHARD_IN), BF16), pltpu.VMEM((last, D, SHARD_IN), BF16), sems, sems],
        args=(h, dp), sides=sides)


def _adamw(w, g, m, v):
    m = ADAM_B1 * m + (1.0 - ADAM_B1) * g
    v = ADAM_B2 * v + (1.0 - ADAM_B2) * (g * g)
    m_hat = m / (1.0 - ADAM_B1 ** ADAM_STEP)
    v_hat = v / (1.0 - ADAM_B2 ** ADAM_STEP)
    delta = -ADAM_LR * (m_hat / (jnp.sqrt(v_hat) + ADAM_EPS) + ADAM_WD * w)
    return delta, m, v


def _sum_update(parts, w, m, v, name):
    R, C = w.shape
    rb = min(R, 256)
    n_parts = min(parts.shape[0], N_DEV)
    if parts.shape[0] == LANDING:
        n_parts = PARTS

    def body(p_ref, w_ref, m_ref, v_ref, g_out, d_out, m_out, v_out):
        acc = p_ref[0].astype(F32)
        for d in range(1, n_parts):
            acc = acc + p_ref[d].astype(F32)
        g_out[...] = acc
        delta, m_new, v_new = _adamw(w_ref[...], acc, m_ref[...], v_ref[...])
        d_out[...] = delta
        m_out[...] = m_new
        v_out[...] = v_new

    blk = pl.BlockSpec((rb, C), lambda i: (i, 0))
    return pl.pallas_call(
        body, name=name, grid=(R // rb,),
        in_specs=[pl.BlockSpec((n_parts, rb, C), lambda i: (0, i, 0)), blk, blk, blk], out_specs=[blk] * 4,
        out_shape=[jax.ShapeDtypeStruct((R, C), F32)] * 4, compiler_params=_params(),
    )(parts, w, m, v)


def _small_sum_update(partials, row_of, weights, loss_parts):
    n_p, n_w = len(partials), len(weights)

    def body(*refs):
        p_refs = refs[:n_p]
        loss_ref = refs[n_p]
        w_refs = refs[n_p + 1:n_p + 1 + 3 * n_w]
        loss_out = refs[n_p + 1 + 3 * n_w]
        outs = refs[n_p + 2 + 3 * n_w:]

        def total(ref):
            acc = ref[0].astype(F32)
            for d in range(1, N_DEV):
                acc = acc + ref[d].astype(F32)
            return acc

        lsum = jnp.sum(total(loss_ref), axis=-1, keepdims=True) * (0.5 / D)
        loss_out[...] = jnp.broadcast_to(lsum, loss_out.shape)
        for k in range(n_p):
            i, r0 = row_of[k]
            g = total(p_refs[k])
            w_ref, m_ref, v_ref = w_refs[3 * i:3 * i + 3]
            g_out, d_out, m_out, v_out = outs[4 * i:4 * i + 4]
            if g.ndim == 2:
                at = (slice(r0, r0 + g.shape[0]), slice(None))
            else:
                at = (slice(None),) * g.ndim
            delta, m_new, v_new = _adamw(w_ref[at], g, m_ref[at], v_ref[at])
            g_out[at] = g
            d_out[at] = delta
            m_out[at] = m_new
            v_out[at] = v_new

    vm = pl.BlockSpec(memory_space=pltpu.VMEM)
    flat_w = [a for wmv in weights for a in wmv]
    out_shape = [jax.ShapeDtypeStruct((1, 128), F32)]
    for w, _, _ in weights:
        out_shape += [jax.ShapeDtypeStruct(w.shape, F32)] * 4
    return pl.pallas_call(
        body, name="small_update", in_specs=[vm] * (n_p + 1 + 3 * n_w), out_specs=[vm] * len(out_shape),
        out_shape=out_shape, compiler_params=pltpu.CompilerParams(vmem_limit_bytes=VMEM_LIMIT),
    )(*partials, loss_parts, *flat_w)


def _pack_small_shard(cw, lg, lb):
    return (jnp.pad(cw, ((0, 13), (0, 64))) + jnp.pad(lg, ((8, 7), (0, 0))) + jnp.pad(lb, ((9, 6), (0, 0))))


TS_PRENORM = 1024
TS_FWD = 512
TS_BWD = 256
TS_WGRAD = 2048


def kernel(x, pre_norm, post_norm, even_w_in, even_conv_w, even_pool_w, even_pool_scale, even_w_out, odd_w_in, odd_ln_g, odd_ln_b, odd_w_s, odd_b_s, odd_w_out, loss_target, m_pre_norm, m_post_norm, m_even_w_in, m_even_conv_w, m_even_pool_w, m_even_pool_scale, m_even_w_out, m_odd_w_in, m_odd_ln_g, m_odd_ln_b, m_odd_w_s, m_odd_b_s, m_odd_w_out, v_pre_norm, v_post_norm, v_even_w_in, v_even_conv_w, v_even_pool_w, v_even_pool_scale, v_even_w_out, v_odd_w_in, v_odd_ln_g, v_odd_ln_b, v_odd_w_s, v_odd_b_s, v_odd_w_out):
    S = x.shape[1]
    xs = x.reshape(S, D)
    tgt = loss_target.reshape(S, D)

    gpre0, gpre1 = pre_norm[0:1], pre_norm[1:2]
    gpost0, gpost1 = post_norm[0:1], post_norm[1:2]
    small_shard = _pack_small_shard(even_conv_w[0], odd_ln_g, odd_ln_b)
    h0, wie_g, small_g, woe_g, wio_b, woo_b = _prenorm_gather(xs, gpre0, even_w_in[0], small_shard, even_w_out[0],
                                                              odd_w_in[0], odd_w_out[0], TS_PRENORM)
    wie = jnp.transpose(wie_g, (1, 0, 2)).reshape(D, PROJ)
    wie_t = jnp.transpose(wie_g, (0, 2, 1)).reshape(PROJ, D)
    woe = woe_g.reshape(D, D)
    woe_t = woe.T
    conv_w = jnp.transpose(small_g[:, 0:3, 0:64], (1, 0, 2)).reshape(3, A_W)
    ln_g = small_g[:, 8, :].reshape(1, D)
    ln_b = small_g[:, 9, :].reshape(1, D)

    pool_w_b = even_pool_w[0].astype(BF16)
    pool_wt_b = jnp.swapaxes(even_pool_w[0], 1, 2).astype(BF16)
    ws_tril = jnp.tril(odd_w_s[0])
    ws_b = ws_tril.astype(BF16)
    wst_b = jnp.swapaxes(ws_tril, 1, 2).astype(BF16)
    b_full = jnp.broadcast_to(odd_b_s[0][:, :, None], (N_HEADS, GROUP, GROUP))

    p0, m0, x1, h1, wio_g, woo_g = _even_forward_fused(h0, xs, wie, conv_w, pool_w_b, even_pool_scale, woe, gpost0,
                                                       gpre1, TS_FWD, sides=[(_Gather, [wio_b, woo_b])])
    wio = jnp.transpose(wio_g, (1, 0, 2)).reshape(D, PROJ)
    wio_t = jnp.transpose(wio_g, (0, 2, 1)).reshape(PROJ, D)
    woo = woo_g.reshape(D, D)
    woo_t = woo.T
    p1, m1, g2, loss_vec = _odd_forward(h1, x1, tgt, wio, ln_g, ln_b, ws_b, b_full, woo, gpost1, TS_FWD)

    dp1, g1, dwoo, dws, dbs, dlg, dlb, dgpost1, dgpre1 = _odd_backward(
        g2, m1, p1, x1, ln_g, ln_b, ws_b, wst_b, b_full, woo_t, wio_t, gpost1, gpre1, TS_FWD)
    by_owner = lambda dwo: dwo.reshape(N_DEV, SHARD_OUT, D)
    (dwio_parts, dwoo_parts, dlg_parts, dlb_parts,
     dws_g, dbs_g, gpost1_g, gpre1_g, loss_g) = _weight_grad_scatter(
        h1, dp1, TS_WGRAD, "odd_weight_grad",
        sides=[(_Scatter, [by_owner(dwoo), dlg.reshape(N_DEV, 1, 128), dlb.reshape(N_DEV, 1, 128)]),
               (_Gather, [dws.astype(BF16), dbs, dgpost1, dgpre1, loss_vec])])
    dp0, gx, dwoe, dcw, dpw, dps, dgpost0, dgpre0 = _even_backward_fused(
        g1, m0, p0, xs, conv_w, pool_w_b, pool_wt_b, even_pool_scale, woe_t, wie_t, gpost0, gpre0, TS_FWD)
    dcw_by_owner = jnp.transpose(dcw[0:3].reshape(3, N_DEV, 64), (1, 0, 2))
    dwie_parts, dwoe_parts, dcw_parts, dpw_g, dps_g, gpost0_g, gpre0_g = _weight_grad_scatter(
        h0, dp0, TS_WGRAD, "even_weight_grad",
        sides=[(_Scatter, [by_owner(dwoe), dcw_by_owner]), (_Gather, [dpw.astype(BF16), dps, dgpost0, dgpre0])])

    g_wie, d_wie, nm_wie, nv_wie = _sum_update(dwie_parts, even_w_in[0], m_even_w_in[0], v_even_w_in[0],
                                               "update_even_w_in")
    g_wio, d_wio, nm_wio, nv_wio = _sum_update(dwio_parts, odd_w_in[0], m_odd_w_in[0], v_odd_w_in[0],
                                               "update_odd_w_in")
    g_woe, d_woe, nm_woe, nv_woe = _sum_update(dwoe_parts, even_w_out[0], m_even_w_out[0], v_even_w_out[0],
                                               "update_even_w_out")
    g_woo, d_woo, nm_woo, nv_woo = _sum_update(dwoo_parts, odd_w_out[0], m_odd_w_out[0], v_odd_w_out[0],
                                               "update_odd_w_out")

    partials = [gpre0_g, gpre1_g, gpost0_g, gpost1_g, dpw_g, dps_g, dws_g, dbs_g, dcw_parts, dlg_parts, dlb_parts]
    row_of = [(0, 0), (0, 1), (1, 0), (1, 1), (2, 0), (3, 0), (4, 0), (5, 0), (6, 0), (7, 0), (8, 0)]
    weights = [(pre_norm, m_pre_norm, v_pre_norm), (post_norm, m_post_norm, v_post_norm),
               (even_pool_w[0], m_even_pool_w[0], v_even_pool_w[0]),
               (even_pool_scale, m_even_pool_scale, v_even_pool_scale),
               (odd_w_s[0], m_odd_w_s[0], v_odd_w_s[0]), (odd_b_s[0], m_odd_b_s[0], v_odd_b_s[0]),
               (even_conv_w[0], m_even_conv_w[0], v_even_conv_w[0]),
               (odd_ln_g, m_odd_ln_g, v_odd_ln_g), (odd_ln_b, m_odd_ln_b, v_odd_ln_b)]
    small = _small_sum_update(partials, row_of, weights, loss_g)
    loss = small[0][0, 0]

    def leaves(k, big):
        pre, post, pw, psc, ws, bs, cw, lg, lb = [small[1 + 4 * i + k] for i in range(len(weights))]
        wie_k, woe_k, wio_k, woo_k = big
        return [pre, post, wie_k[None], cw[None], pw[None], psc, woe_k[None], wio_k[None], lg, lb, ws[None], bs[None],
                woo_k[None]]

    outs = [loss, gx.reshape(1, S, D)]
    outs += leaves(0, (g_wie, g_woe, g_wio, g_woo))
    outs += leaves(1, (d_wie, d_woe, d_wio, d_woo))
    outs += leaves(2, (nm_wie, nm_woe, nm_wio, nm_woo))
    outs += leaves(3, (nv_wie, nv_woe, nv_wio, nv_woo))
    return tuple(outs)
```

```python
import jax
import jax.numpy as jnp
from jax import lax
from jax.experimental import pallas as pl
from jax.experimental.pallas import tpu as pltpu

F32 = jnp.float32
BF16 = jnp.bfloat16
MESH = pl.DeviceIdType.MESH

D = 1024
EPS = 1e-6
A_W = 512
B_W = 512
POOL_WINDOWS = (2, 4, 8, 16)
GROUP = 128
N_GROUPS = 4
N_HEADS = 8
HALO = 16
PROJ = 3072
N_DEV = 8
N_CHIPS = 4
SHARD_IN = PROJ // N_DEV
SHARD_OUT = D // N_DEV

ADAM_LR = 0.001
ADAM_B1 = 0.9
ADAM_B2 = 0.999
ADAM_EPS = 1e-08
ADAM_WD = 0.01
ADAM_STEP = 10

VMEM_LIMIT = 54 * 1024 * 1024

TS_PRENORM = 1024
TS_FWD = 512
TS_BWD = 256
TS_WGRAD = 2048
SUB_ROWS = 256


def _params(n_grid=1, vmem=VMEM_LIMIT):
    return pltpu.CompilerParams(dimension_semantics=("arbitrary",) * n_grid, vmem_limit_bytes=vmem)


def _full(shape):
    return pl.BlockSpec(shape, lambda *_: (0,) * len(shape))


def _resident(shape):
    return pl.BlockSpec(shape, lambda *_: (0,) * len(shape), pipeline_mode=pl.Buffered(1))


def _sigmoid(z):
    return jax.nn.sigmoid(z)


def _dsilu(z, s):
    return s * (1.0 + z * (1.0 - s))


def _dot(a, b):
    return jnp.dot(a, b, preferred_element_type=F32)


def _dot_t0(a, b):
    return lax.dot_general(a, b, (((0,), (0,)), ((), ())), preferred_element_type=F32)


def _dot_t1(a, b):
    return lax.dot_general(a, b, (((1,), (1,)), ((), ())), preferred_element_type=F32)


def _rms(x):
    return lax.rsqrt(jnp.mean(x * x, axis=-1, keepdims=True) + EPS)


def _shift_down(a, k):
    return pltpu.roll(a, k, 0)


def _shift_up(a, k):
    return pltpu.roll(a, a.shape[0] - k, 0)


def _mesh_position():
    return lax.axis_index("x"), lax.axis_index("y"), lax.axis_index("c")


def _index(pos):
    return 4 * pos[0] + 2 * pos[1] + pos[2]


def _peer(pos, d):
    x, y, c = pos
    return (1 - x if d & 4 else x, 1 - y if d & 2 else y, 1 - c if d & 1 else c)


def _remote(src, dst, send_sems, recv_sems, k, to):
    return pltpu.make_async_remote_copy(src_ref=src, dst_ref=dst, send_sem=send_sems.at[k], recv_sem=recv_sems.at[k],
                                        device_id=to, device_id_type=MESH)


class _Gather:
    def __init__(self, srcs, dsts, send_sems, recv_sems):
        x, y, c = _mesh_position()
        n = len(srcs)
        me, sibling = (x, y, c), (x, y, 1 - c)
        chips = [(1 - x, y), (x, 1 - y), (1 - x, 1 - y)]

        def copy(k, t, block, to, src=None):
            slot = dsts[t].at[_index(block)]
            return _remote(slot if src is None else src, slot, send_sems, recv_sems, k * n + t, to)

        tensors = range(n)
        self.local = [pltpu.make_async_copy(srcs[t], dsts[t].at[_index(me)], recv_sems.at[7 * n + t]) for t in tensors]
        self.first = [copy(1 + j, t, me, chip + (c,), srcs[t]) for t in tensors for j, chip in enumerate(chips)]
        self.first += [copy(0, t, me, sibling, srcs[t]) for t in tensors]
        self.ici_in = [copy(1 + j, t, chip + (c,), me) for t in tensors for j, chip in enumerate(chips)]
        self.passed = [copy(4 + j, t, chip + (c,), sibling) for t in tensors for j, chip in enumerate(chips)]
        self.d2d_in = [copy(0, t, sibling, me) for t in tensors]
        self.d2d_in += [copy(4 + j, t, chip + (1 - c,), me) for t in tensors for j, chip in enumerate(chips)]

    def start(self):
        for cp in self.local + self.first:
            cp.start()

    def middle(self):
        for landed, onward in zip(self.ici_in, self.passed):
            landed.wait_recv()
            onward.start()

    def finish(self):
        for cp in self.d2d_in:
            cp.wait_recv()
        for cp in self.first + self.passed:
            cp.wait_send()
        for cp in self.local:
            cp.wait()


class _Scatter:
    def __init__(self, srcs, dsts, send_sems, recv_sems):
        pos = _mesh_position()
        n = len(srcs)
        self.local = [pltpu.make_async_copy(srcs[t].at[_index(pos)], dsts[t].at[0], recv_sems.at[t]) for t in range(n)]
        self.remote = []
        for d in range(1, N_DEV):
            to = _peer(pos, d)
            self.remote += [_remote(srcs[t].at[_index(to)], dsts[t].at[d], send_sems, recv_sems, d * n + t, to)
                            for t in range(n)]

    def start(self):
        for cp in self.local + self.remote:
            cp.start()

    def middle(self):
        pass

    def finish(self):
        for cp in self.local:
            cp.wait()
        for cp in self.remote:
            cp.wait_recv()
        for cp in self.remote:
            cp.wait_send()


HBM_SPEC = pl.BlockSpec(memory_space=pltpu.HBM)
MIDDLE_STEPS_BEFORE_END = 4


def _landing_shape(exchange, src):
    return (N_DEV,) + src.shape if exchange is _Gather else src.shape


def _call(body, *, name, grid, in_specs, out_specs, out_shape, scratch_shapes, args, sides=()):
    params = _params(len(grid))
    if not sides:
        return pl.pallas_call(body, name=name, grid=grid, in_specs=in_specs, out_specs=out_specs, out_shape=out_shape,
                              scratch_shapes=scratch_shapes, compiler_params=params)(*args)
    n_in, n_out, n_scratch = len(in_specs), len(out_specs), len(scratch_shapes)
    counts = [len(srcs) for _, srcs in sides]
    ns = sum(counts)
    total = 1
    for g in grid:
        total *= g
    side_args, side_shapes, side_sems = [], [], []
    for exchange, srcs in sides:
        side_args += list(srcs)
        side_shapes += [jax.ShapeDtypeStruct(_landing_shape(exchange, s), s.dtype) for s in srcs]
        side_sems += [pltpu.SemaphoreType.DMA((N_DEV * len(srcs),))] * 2

    def wrapped(*refs):
        ins, side_in = refs[:n_in], refs[n_in:n_in + ns]
        outs = refs[n_in + ns:n_in + ns + n_out]
        side_out = refs[n_in + ns + n_out:n_in + 2 * ns + n_out]
        rest = refs[n_in + 2 * ns + n_out:]
        scratch, sems = rest[:n_scratch], rest[n_scratch:]
        step = pl.program_id(0)
        for axis in range(1, len(grid)):
            step = step * grid[axis] + pl.program_id(axis)

        def exchanges():
            built, at = [], 0
            for k, (exchange, _) in enumerate(sides):
                built.append(exchange(side_in[at:at + counts[k]], side_out[at:at + counts[k]],
                                      sems[2 * k], sems[2 * k + 1]))
                at += counts[k]
            return built

        @pl.when(step == 0)
        def _():
            for ex in exchanges():
                ex.start()

        @pl.when(step == max(total - MIDDLE_STEPS_BEFORE_END, 0))
        def _():
            for ex in exchanges():
                ex.middle()

        body(*ins, *outs, *scratch)

        @pl.when(step == total - 1)
        def _():
            for ex in exchanges():
                ex.finish()

    return pl.pallas_call(
        wrapped, name=name, grid=grid, in_specs=list(in_specs) + [HBM_SPEC] * ns,
        out_specs=list(out_specs) + [HBM_SPEC] * ns, out_shape=list(out_shape) + side_shapes,
        scratch_shapes=list(scratch_shapes) + side_sems, compiler_params=params)(*args, *side_args)


def _prenorm_gather(x, g, wie, small, woe, wio, woo, ts):
    S = x.shape[0]
    nt = S // ts

    def body(x_ref, g_ref, wie_ref, sm_ref, woe_ref, wio_ref, woo_ref,
             h_ref, wie_out, sm_out, woe_out, wio_b, woo_b, cast_ref, small_ref, cast2_ref, send_sems, recv_sems):
        step = pl.program_id(0)
        gather = lambda: _Gather([cast_ref, small_ref, cast2_ref], [wie_out, sm_out, woe_out], send_sems, recv_sems)

        @pl.when(step == 0)
        def _():
            cast_ref[...] = wie_ref[...].astype(BF16)
            small_ref[...] = sm_ref[...]
            cast2_ref[...] = woe_ref[...].astype(BF16)
            gather().start()
            wio_b[...] = wio_ref[...].astype(BF16)
            woo_b[...] = woo_ref[...].astype(BF16)

        xv = x_ref[...]
        h_ref[...] = ((xv * _rms(xv)) * g_ref[...]).astype(BF16)

        @pl.when(step == nt - 1)
        def _():
            exchange = gather()
            exchange.middle()
            exchange.finish()

    tile = pl.BlockSpec((ts, D), lambda i: (i, 0))
    sems = pltpu.SemaphoreType.DMA((N_DEV * 3,))
    shards = [wie, small, woe, wio, woo]
    return pl.pallas_call(
        body, name="prenorm_gather", grid=(nt,),
        in_specs=[tile, _full((1, D))] + [_full(a.shape) for a in shards],
        out_specs=[tile, HBM_SPEC, HBM_SPEC, HBM_SPEC] + [_full(a.shape) for a in shards[3:]],
        out_shape=[jax.ShapeDtypeStruct((S, D), BF16), jax.ShapeDtypeStruct((N_DEV,) + wie.shape, BF16),
                   jax.ShapeDtypeStruct((N_DEV,) + small.shape, F32), jax.ShapeDtypeStruct((N_DEV,) + woe.shape, BF16)]
        + [jax.ShapeDtypeStruct(a.shape, BF16) for a in shards[3:]],
        scratch_shapes=[pltpu.VMEM(wie.shape, BF16), pltpu.VMEM(small.shape, F32), pltpu.VMEM(woe.shape, BF16),
                        sems, sems],
        compiler_params=_params(),
    )(x, g, wie, small, woe, wio, woo)


def _even_mixer(proj, halo, row0, cw_ref, pw_ref):
    xa, gb, gc, za, xp, zp = proj
    nr = xa.shape[0]
    ha = gc * xa
    ha_ext = jnp.concatenate([halo[0], ha], axis=0)
    ha_m1 = _shift_down(ha_ext, 1)[HALO:]
    ha_m2 = _shift_down(ha_ext, 2)[HALO:]
    conv = cw_ref[2:3, :] * ha + cw_ref[1:2, :] * ha_m1 + cw_ref[0:1, :] * ha_m2
    sig_a = _sigmoid(za)
    silu_a = za * sig_a

    xp_ext = jnp.concatenate([halo[1], xp], axis=0)
    pos = row0 + lax.broadcasted_iota(jnp.int32, (nr, 1), 0)
    pooled, inv_cnt, mixed = [], [], []
    for g, w in enumerate(POOL_WINDOWS):
        cols = slice(g * GROUP, (g + 1) * GROUP)
        s = xp_ext[:, cols]
        k = 1
        while k < w:
            s = s + _shift_down(s, k)
            k *= 2
        inv = 1.0 / jnp.minimum(pos + 1, w).astype(F32)
        pg = s[HALO:] * inv - xp[:, cols]
        pooled.append(pg)
        inv_cnt.append(inv)
        mixed.append(_dot(pg.astype(BF16), pw_ref[g]))
    mixed = jnp.concatenate(mixed, axis=1)
    sig_b = _sigmoid(zp)
    silu_b = zp * sig_b
    return dict(xa=xa, gb=gb, gc=gc, za=za, xp=xp, zp=zp, ha=ha, ha_m1=ha_m1, ha_m2=ha_m2, conv=conv,
                sig_a=sig_a, silu_a=silu_a, pooled=pooled, inv_cnt=inv_cnt, mixed=mixed, sig_b=sig_b,
                silu_b=silu_b)


def _columns(a):
    return [a[:, k * A_W:(k + 1) * A_W].astype(F32) for k in range(6)]


def _even_forward_fused(h, x, win_b, cw, pw_b, ps, wout_b, gpost, gpre_next, ts, sides=()):
    S = x.shape[0]

    def body(h_ref, x_ref, wi_ref, cw_ref, pw_ref, ps_ref, wo_ref, gp_ref, gn_ref,
             p_ref, m_ref, x1_ref, h1_ref, mix_ref, carry_ref):
        i = pl.program_id(0)

        @pl.when(i == 0)
        def _():
            carry_ref[...] = jnp.zeros_like(carry_ref)

        for r0 in range(0, ts, SUB_ROWS):
            rows = slice(r0, r0 + SUB_ROWS)
            proj = _dot(h_ref[rows, :], wi_ref[...])
            p_ref[rows, :] = proj.astype(BF16)
            fw = _even_mixer(_columns(proj), (carry_ref[:, 0:A_W], carry_ref[:, A_W:D]), i * ts + r0, cw_ref, pw_ref)
            carry_ref[:, 0:A_W] = fw["ha"][SUB_ROWS - HALO:]
            carry_ref[:, A_W:D] = fw["xp"][SUB_ROWS - HALO:]
            mix_ref[rows, 0:A_W] = (fw["gb"] * fw["conv"] * fw["silu_a"]).astype(BF16)
            mix_ref[rows, A_W:D] = (fw["mixed"] * ps_ref[...] * fw["silu_b"]).astype(BF16)
            m = _dot(mix_ref[rows, :], wo_ref[...])
            m_ref[rows, :] = m.astype(BF16)
            x1 = x_ref[rows, :] + (m * _rms(m)) * gp_ref[...]
            x1_ref[rows, :] = x1
            h1_ref[rows, :] = ((x1 * _rms(x1)) * gn_ref[...]).astype(BF16)

    tile = pl.BlockSpec((ts, D), lambda i: (i, 0))
    return _call(
        body, name="even_forward", grid=(S // ts,),
        in_specs=[tile, tile, _resident((D, PROJ)), _full((3, A_W)), _full((N_GROUPS, GROUP, GROUP)), _full((1, B_W)),
                  _resident((D, D)), _full((1, D)), _full((1, D))],
        out_specs=[pl.BlockSpec((ts, PROJ), lambda i: (i, 0)), tile, tile, tile],
        out_shape=[jax.ShapeDtypeStruct((S, PROJ), BF16), jax.ShapeDtypeStruct((S, D), BF16),
                   jax.ShapeDtypeStruct((S, D), F32), jax.ShapeDtypeStruct((S, D), BF16)],
        scratch_shapes=[pltpu.VMEM((ts, D), BF16), pltpu.VMEM((HALO, D), F32)],
        args=(h, x, win_b, cw, pw_b, ps, wout_b, gpost, gpre_next), sides=sides)


def _even_backward_fused(g1, m0, p, x, cw, pw_b, pwt_b, ps, woutt_b, wint_b, gpost, gpre, ts):
    S = g1.shape[0]
    nt = S // ts

    def body(g_ref, m_ref, p_ref, halo_ref, x_ref, cw_ref, pw_ref, pwt_ref, ps_ref, wot_ref, wit_ref, gp_ref, gn_ref,
             dp_ref, dx_ref, dwob_ref, dcw_ref, dpw_ref, dps_ref, dgp_ref, dgn_ref,
             carry_ref, mix_ref, dm_ref, dwo_ref):
        step = pl.program_id(0)
        i = nt - 1 - step

        @pl.when(step == 0)
        def _():
            carry_ref[...] = jnp.zeros_like(carry_ref)
            dwo_ref[...] = jnp.zeros_like(dwo_ref)
            dcw_ref[...] = jnp.zeros_like(dcw_ref)
            dpw_ref[...] = jnp.zeros_like(dpw_ref)
            dps_ref[...] = jnp.zeros_like(dps_ref)
            dgp_ref[...] = jnp.zeros_like(dgp_ref)
            dgn_ref[...] = jnp.zeros_like(dgn_ref)

        ps_v = ps_ref[...]
        sums = [None] * 6
        add = lambda k, part: sums.__setitem__(k, part if sums[k] is None else sums[k] + part)
        for r0 in range(ts - SUB_ROWS, -1, -SUB_ROWS):
            rows = slice(r0, r0 + SUB_ROWS)
            g = g_ref[rows, :]
            m = m_ref[rows, :].astype(F32)
            q = _rms(m)
            n = m * q
            add(0, jnp.sum(g * n, axis=0, keepdims=True))
            dn = g * gp_ref[...]
            dm = q * (dn - n * jnp.mean(dn * n, axis=-1, keepdims=True))
            dm_ref[rows, :] = dm.astype(BF16)
            dmix = _dot(dm_ref[rows, :], wot_ref[...])
            dya = dmix[:, 0:A_W]
            dyb = dmix[:, A_W:D]

            if r0 == 0:
                before = _columns(halo_ref[...])
                keep = jnp.where(i == 0, 0.0, 1.0).astype(F32)
                halo = (before[2] * before[0] * keep, before[4] * keep)
            else:
                before = _columns(p_ref[r0 - HALO:r0, :])
                halo = (before[2] * before[0], before[4])
            fw = _even_mixer(_columns(p_ref[rows, :]), halo, i * ts + r0, cw_ref, pw_ref)
            mix_ref[rows, 0:A_W] = (fw["gb"] * fw["conv"] * fw["silu_a"]).astype(BF16)
            mix_ref[rows, A_W:D] = (fw["mixed"] * ps_v * fw["silu_b"]).astype(BF16)

            t = dya * fw["gb"]
            dconv = t * fw["silu_a"]
            dgb = dya * fw["conv"] * fw["silu_a"]
            dza = t * fw["conv"] * _dsilu(fw["za"], fw["sig_a"])
            add(2, jnp.sum(dconv * fw["ha"], axis=0, keepdims=True))
            add(3, jnp.sum(dconv * fw["ha_m1"], axis=0, keepdims=True))
            add(4, jnp.sum(dconv * fw["ha_m2"], axis=0, keepdims=True))
            dconv_ext = jnp.concatenate([dconv, carry_ref[:, 0:A_W]], axis=0)
            dha = (cw_ref[2:3, :] * dconv + cw_ref[1:2, :] * _shift_up(dconv_ext, 1)[:SUB_ROWS]
                   + cw_ref[0:1, :] * _shift_up(dconv_ext, 2)[:SUB_ROWS])
            dgc = dha * fw["xa"]
            dxa = dha * fw["gc"]

            u = dyb * fw["mixed"]
            add(1, jnp.sum(u * fw["silu_b"], axis=0, keepdims=True))
            dzp = u * ps_v * _dsilu(fw["zp"], fw["sig_b"])
            dmixed = (dyb * ps_v * fw["silu_b"]).astype(BF16)
            dxp, e_first = [], []
            for gi, w in enumerate(POOL_WINDOWS):
                cols = slice(gi * GROUP, (gi + 1) * GROUP)
                dmg = dmixed[:, cols]
                dpooled = _dot(dmg, pwt_ref[gi])
                dpw_ref[gi] += _dot_t0(fw["pooled"][gi].astype(BF16), dmg)
                e = dpooled * fw["inv_cnt"][gi]
                e_first.append(e[0:HALO])
                s = jnp.concatenate([e, carry_ref[:, A_W + gi * GROUP:A_W + (gi + 1) * GROUP]], axis=0)
                k = 1
                while k < w:
                    s = s + _shift_up(s, k)
                    k *= 2
                dxp.append(s[:SUB_ROWS] - dpooled)
            carry_ref[:, 0:A_W] = dconv[0:HALO]
            carry_ref[:, A_W:D] = jnp.concatenate(e_first, axis=1)

            dp_ref[rows, 0:512] = dxa.astype(BF16)
            dp_ref[rows, 512:1024] = dgb.astype(BF16)
            dp_ref[rows, 1024:1536] = dgc.astype(BF16)
            dp_ref[rows, 1536:2048] = dza.astype(BF16)
            dp_ref[rows, 2048:2560] = jnp.concatenate(dxp, axis=1).astype(BF16)
            dp_ref[rows, 2560:3072] = dzp.astype(BF16)

            dh = _dot(dp_ref[rows, :], wit_ref[...])
            xv = x_ref[rows, :]
            r = _rms(xv)
            xn = xv * r
            add(5, jnp.sum(dh * xn, axis=0, keepdims=True))
            dxn = dh * gn_ref[...]
            dx_ref[rows, :] = g + r * (dxn - xn * jnp.mean(dxn * xn, axis=-1, keepdims=True))

        dwo_ref[...] += _dot_t0(mix_ref[...], dm_ref[...])
        dgp_ref[...] += sums[0]
        dps_ref[...] += sums[1]
        dcw_ref[2:3, :] += sums[2]
        dcw_ref[1:2, :] += sums[3]
        dcw_ref[0:1, :] += sums[4]
        dgn_ref[...] += sums[5]

        @pl.when(step == nt - 1)
        def _():
            dwob_ref[...] = dwo_ref[...].astype(BF16)

    rev = lambda s: (nt - 1 - s, 0)
    tile = pl.BlockSpec((ts, D), rev)
    wide = pl.BlockSpec((ts, PROJ), rev)
    bpt = ts // HALO
    halo_map = lambda s: (jnp.maximum((nt - 1 - s) * bpt - 1, 0), 0)
    groups = _full((N_GROUPS, GROUP, GROUP))
    vec = _full((1, D))
    return pl.pallas_call(
        body, name="even_backward", grid=(nt,),
        in_specs=[tile, tile, wide, pl.BlockSpec((HALO, PROJ), halo_map), tile, _full((3, A_W)), groups, groups,
                  _full((1, B_W)), _resident((D, D)), _resident((PROJ, D)), vec, vec],
        out_specs=[wide, tile, _full((D, D)), _full((8, A_W)), groups, _full((1, B_W)), vec, vec],
        out_shape=[jax.ShapeDtypeStruct((S, PROJ), BF16), jax.ShapeDtypeStruct((S, D), F32),
                   jax.ShapeDtypeStruct((D, D), BF16), jax.ShapeDtypeStruct((8, A_W), F32),
                   jax.ShapeDtypeStruct((N_GROUPS, GROUP, GROUP), F32), jax.ShapeDtypeStruct((1, B_W), F32),
                   jax.ShapeDtypeStruct((1, D), F32), jax.ShapeDtypeStruct((1, D), F32)],
        scratch_shapes=[pltpu.VMEM((HALO, D), F32), pltpu.VMEM((ts, D), BF16), pltpu.VMEM((ts, D), BF16),
                        pltpu.VMEM((D, D), F32)],
        compiler_params=_params(),
    )(g1, m0, p, p, x, cw, pw_b, pwt_b, ps, woutt_b, wint_b, gpost, gpre)


def _gate_matmuls(w_ref, in_ref, out_ref, chunks, bias_ref=None):
    for h in range(N_HEADS):
        cols = slice(h * GROUP, (h + 1) * GROUP)
        wide = jnp.concatenate([in_ref[chunk, cols] for chunk in chunks], axis=1)
        res = _dot(w_ref[h], wide)
        for k, chunk in enumerate(chunks):
            part = res[:, k * GROUP:(k + 1) * GROUP]
            out_ref[chunk, cols] = part if bias_ref is None else part + bias_ref[h]


def _odd_forward_parts(v, lg_ref, lb_ref, wt_ref, bfull_ref, vln_ref, sv_ref, r0, nr):
    rows = slice(r0, r0 + nr)
    mu = jnp.mean(v, axis=-1, keepdims=True)
    vc = v - mu
    rstd = lax.rsqrt(jnp.mean(vc * vc, axis=-1, keepdims=True) + EPS)
    vh = vc * rstd
    vln_ref[rows, :] = (vh * lg_ref[...] + lb_ref[...]).astype(BF16)
    _gate_matmuls(wt_ref, vln_ref, sv_ref, [slice(r0 + k * GROUP, r0 + (k + 1) * GROUP) for k in range(nr // GROUP)],
                  bfull_ref)
    return vh, rstd


def _odd_forward(h, x1, tgt, win_b, lg, lb, wt_b, bfull, wout_b, gpost, ts):
    S = x1.shape[0]

    def body(h_ref, x_ref, t_ref, wi_ref, lg_ref, lb_ref, wt_ref, bfull_ref, wo_ref, gp_ref,
             p_ref, m_ref, g_ref, loss_ref, vln_ref, sv_ref, y_ref):
        @pl.when(pl.program_id(0) == 0)
        def _():
            loss_ref[...] = jnp.zeros_like(loss_ref)

        loss = None
        for r0 in range(0, ts, SUB_ROWS):
            rows = slice(r0, r0 + SUB_ROWS)
            proj = _dot(h_ref[rows, :], wi_ref[...])
            p_ref[rows, :] = proj.astype(BF16)
            u, v, z = proj[:, 0:D], proj[:, D:2 * D], proj[:, 2 * D:3 * D]
            _odd_forward_parts(v, lg_ref, lb_ref, wt_ref, bfull_ref, vln_ref, sv_ref, r0, SUB_ROWS)
            y_ref[rows, :] = (u * sv_ref[rows, :] * (z * _sigmoid(z))).astype(BF16)
            m = _dot(y_ref[rows, :], wo_ref[...])
            m_ref[rows, :] = m.astype(BF16)
            x2 = x_ref[rows, :] + (m * _rms(m)) * gp_ref[...]
            err = x2 - t_ref[rows, :]
            g_ref[rows, :] = err * (1.0 / D)
            part = jnp.sum(err * err, axis=0, keepdims=True)
            loss = part if loss is None else loss + part
        loss_ref[...] += loss

    tile = pl.BlockSpec((ts, D), lambda i: (i, 0))
    small = _full((N_HEADS, GROUP, GROUP))
    return pl.pallas_call(
        body, name="odd_forward", grid=(S // ts,),
        in_specs=[pl.BlockSpec((ts, D), lambda i: (i, 0)), tile, tile, _full((D, PROJ)), _full((1, D)), _full((1, D)),
                  small, small, _full((D, D)), _full((1, D))],
        out_specs=[pl.BlockSpec((ts, PROJ), lambda i: (i, 0)), tile, tile, _full((1, D))],
        out_shape=[jax.ShapeDtypeStruct((S, PROJ), BF16), jax.ShapeDtypeStruct((S, D), BF16),
                   jax.ShapeDtypeStruct((S, D), F32), jax.ShapeDtypeStruct((1, D), F32)],
        scratch_shapes=[pltpu.VMEM((ts, D), BF16), pltpu.VMEM((ts, D), F32), pltpu.VMEM((ts, D), BF16)],
        compiler_params=_params(),
    )(h, x1, tgt, win_b, lg, lb, wt_b, bfull, wout_b, gpost)


def _odd_backward(g2, m1, p, x1, lg, lb, wt_b, wtt_b, bfull, woutt_b, wint_b, gpost, gpre, ts):
    S = g2.shape[0]
    nt = S // ts
    sub = ts // 2

    def body(g_ref, m_ref, p_ref, x_ref, lg_ref, lb_ref, wt_ref, wtt_ref, bfull_ref, wot_ref, wit_ref, gp_ref, gn_ref,
             dp_ref, dx_ref, dwob_ref, dws_ref, dbs_ref, dlg_ref, dlb_ref, dgp_ref, dgn_ref,
             vln_ref, sv_ref, dsvb_ref, dvln_ref, dsum_ref, y_ref, dm_ref, dwo_ref):
        step = pl.program_id(0)

        @pl.when(step == 0)
        def _():
            dwo_ref[...] = jnp.zeros_like(dwo_ref)
            dws_ref[...] = jnp.zeros_like(dws_ref)
            dsum_ref[...] = jnp.zeros_like(dsum_ref)
            dlg_ref[...] = jnp.zeros_like(dlg_ref)
            dlb_ref[...] = jnp.zeros_like(dlb_ref)
            dgp_ref[...] = jnp.zeros_like(dgp_ref)
            dgn_ref[...] = jnp.zeros_like(dgn_ref)

        tril = (lax.broadcasted_iota(jnp.int32, (GROUP, GROUP), 0)
                >= lax.broadcasted_iota(jnp.int32, (GROUP, GROUP), 1))
        sums = [None] * 4
        add = lambda k, part: sums.__setitem__(k, part if sums[k] is None else sums[k] + part)

        def post_norm(c):
            g = g_ref[c["rows"], :]
            m = m_ref[c["rows"], :].astype(F32)
            q = _rms(m)
            n = m * q
            add(0, jnp.sum(g * n, axis=0, keepdims=True))
            dn = g * gp_ref[...]
            dm = q * (dn - n * jnp.mean(dn * n, axis=-1, keepdims=True))
            dm_ref[c["rows"], :] = dm.astype(BF16)

        def out_projection(c):
            c["dy"] = _dot(dm_ref[c["rows"], :], wot_ref[...])

        def layernorm(c):
            v = p_ref[c["rows"], D:2 * D].astype(F32)
            mu = jnp.mean(v, axis=-1, keepdims=True)
            vc = v - mu
            c["rstd"] = lax.rsqrt(jnp.mean(vc * vc, axis=-1, keepdims=True) + EPS)
            c["vh"] = vc * c["rstd"]
            vln_ref[c["rows"], :] = (c["vh"] * lg_ref[...] + lb_ref[...]).astype(BF16)

        def gate_matmuls(c):
            _gate_matmuls(wt_ref, vln_ref, sv_ref, c["chunks"], bfull_ref)

        def gating(c):
            rows = c["rows"]
            u = p_ref[rows, 0:D].astype(F32)
            z = p_ref[rows, 2 * D:3 * D].astype(F32)
            sv = sv_ref[rows, :]
            sig = _sigmoid(z)
            sz = z * sig
            y_ref[rows, :] = (u * sv * sz).astype(BF16)
            dy = c.pop("dy")
            t = dy * sz
            dsv = t * u
            dsvb_ref[rows, :] = dsv.astype(BF16)
            for k in range(sub // GROUP):
                dsum_ref[...] += dsv[k * GROUP:(k + 1) * GROUP]
            dp_ref[rows, 0:D] = (t * sv).astype(BF16)
            dp_ref[rows, 2 * D:3 * D] = (dy * u * sv * _dsilu(z, sig)).astype(BF16)

        def gate_backward_matmuls(c):
            for chunk in c["chunks"]:
                for h in range(N_HEADS):
                    cols = slice(h * GROUP, (h + 1) * GROUP)
                    dvln_ref[chunk, cols] = _dot(wtt_ref[h], dsvb_ref[chunk, cols])
                    dws_ref[h] += jnp.where(tril, _dot_t1(dsvb_ref[chunk, cols], vln_ref[chunk, cols]), 0.0)

        def layernorm_backward(c):
            vh, rstd = c.pop("vh"), c.pop("rstd")
            dvln = dvln_ref[c["rows"], :]
            add(1, jnp.sum(dvln * vh, axis=0, keepdims=True))
            add(2, jnp.sum(dvln, axis=0, keepdims=True))
            dvh = dvln * lg_ref[...]
            dv = rstd * (dvh - jnp.mean(dvh, axis=-1, keepdims=True)
                         - vh * jnp.mean(dvh * vh, axis=-1, keepdims=True))
            dp_ref[c["rows"], D:2 * D] = dv.astype(BF16)

        def in_projection(c):
            c["dh"] = _dot(dp_ref[c["rows"], :], wit_ref[...])

        def pre_norm(c):
            dh = c.pop("dh")
            xv = x_ref[c["rows"], :]
            r = _rms(xv)
            xn = xv * r
            add(3, jnp.sum(dh * xn, axis=0, keepdims=True))
            dxn = dh * gn_ref[...]
            dx_ref[c["rows"], :] = g_ref[c["rows"], :] + r * (dxn - xn * jnp.mean(dxn * xn, axis=-1, keepdims=True))

        phases = [post_norm, out_projection, layernorm, gate_matmuls, gating, gate_backward_matmuls,
                  layernorm_backward, in_projection, pre_norm]
        groups = [dict(rows=slice(r0, r0 + sub),
                       chunks=[slice(r0 + k * GROUP, r0 + (k + 1) * GROUP) for k in range(sub // GROUP)])
                  for r0 in range(0, ts, sub)]
        for group in groups:
            for phase in phases:
                phase(group)

        dwo_ref[...] += _dot_t0(y_ref[...], dm_ref[...])
        dgp_ref[...] += sums[0]
        dlg_ref[...] += sums[1]
        dlb_ref[...] += sums[2]
        dgn_ref[...] += sums[3]

        @pl.when(step == nt - 1)
        def _():
            dwob_ref[...] = dwo_ref[...].astype(BF16)
            ones = jnp.ones((8, GROUP), F32)
            for h in range(N_HEADS):
                cols = slice(h * GROUP, (h + 1) * GROUP)
                total = lax.dot_general(ones, dsum_ref[:, cols], (((1,), (1,)), ((), ())),
                                        precision=lax.Precision.HIGHEST, preferred_element_type=F32)
                dbs_ref[h:h + 1, :] = total[0:1, :]

    tile = pl.BlockSpec((ts, D), lambda i: (i, 0))
    wide = pl.BlockSpec((ts, PROJ), lambda i: (i, 0))
    small = _full((N_HEADS, GROUP, GROUP))
    heads = _resident((N_HEADS, GROUP, GROUP))
    vec = _full((1, D))
    return pl.pallas_call(
        body, name="odd_backward", grid=(nt,),
        in_specs=[tile, tile, wide, tile, vec, vec, heads, heads, heads, _resident((D, D)), _resident((PROJ, D)),
                  vec, vec],
        out_specs=[wide, tile, _full((D, D)), small, _full((N_HEADS, GROUP)), vec, vec, vec, vec],
        out_shape=[jax.ShapeDtypeStruct((S, PROJ), BF16), jax.ShapeDtypeStruct((S, D), F32),
                   jax.ShapeDtypeStruct((D, D), BF16),
                   jax.ShapeDtypeStruct((N_HEADS, GROUP, GROUP), F32), jax.ShapeDtypeStruct((N_HEADS, GROUP), F32),
                   jax.ShapeDtypeStruct((1, D), F32), jax.ShapeDtypeStruct((1, D), F32),
                   jax.ShapeDtypeStruct((1, D), F32), jax.ShapeDtypeStruct((1, D), F32)],
        scratch_shapes=[pltpu.VMEM((ts, D), BF16), pltpu.VMEM((ts, D), F32), pltpu.VMEM((ts, D), BF16),
                        pltpu.VMEM((ts, D), F32), pltpu.VMEM((GROUP, D), F32), pltpu.VMEM((ts, D), BF16),
                        pltpu.VMEM((ts, D), BF16), pltpu.VMEM((D, D), F32)],
        compiler_params=_params(),
    )(g2, m1, p, x1, lg, lb, wt_b, wtt_b, bfull, woutt_b, wint_b, gpost, gpre)


def _weight_grad_scatter(h, dp, ts, name, sides=()):
    S = h.shape[0]
    ts = min(ts, S)
    nt = S // ts
    panel = 2 * SHARD_IN
    last = N_CHIPS - 1

    def body(h_ref, dp_ref, parts_ref, acc_ref, half_ref, swap_ref, sum_ref, send_sems, recv_sems):
        s, t = pl.program_id(0), pl.program_id(1)
        x, y, c = _mesh_position()
        sibling = (x, y, 1 - c)

        def swap(k):
            return _remote(half_ref.at[k, 1], swap_ref.at[k], send_sems, recv_sems, k, sibling)

        def chip_sum(k):
            flip = last - k
            to = (x ^ (flip >> 1), y ^ (flip & 1), c)
            return _remote(sum_ref.at[k], parts_ref.at[1 + flip], send_sems, recv_sems, 3 + flip, to)

        own = pltpu.make_async_copy(half_ref.at[last, 0], parts_ref.at[0], recv_sems.at[7])
        to_sibling = _remote(half_ref.at[last, 1], parts_ref.at[1], send_sems, recv_sems, 3, sibling)

        @pl.when(t == 0)
        def _():
            acc_ref[...] = jnp.zeros_like(acc_ref)

        for k in range(last):
            @pl.when((s == k + 1) & (t == 0))
            def _():
                swap(k).wait_recv()
                sum_ref[k] = (half_ref[k, 0].astype(F32) + swap_ref[k].astype(F32)).astype(BF16)
                chip_sum(k).start()

        acc_ref[...] += _dot_t0(h_ref[...], dp_ref[...])

        @pl.when(t == nt - 1)
        def _():
            for core in range(2):
                @pl.when(c == core)
                def _():
                    half_ref[s, core] = acc_ref[:, 0:SHARD_IN].astype(BF16)
                    half_ref[s, 1 - core] = acc_ref[:, SHARD_IN:panel].astype(BF16)
            for k in range(last):
                @pl.when(s == k)
                def _():
                    swap(k).start()

        @pl.when((s == last) & (t == nt - 1))
        def _():
            own.start()
            to_sibling.start()
            own.wait()
            _remote(half_ref.at[last, 1], parts_ref.at[1], send_sems, recv_sems, 3, sibling).wait_recv()
            for k in range(last):
                chip_sum(k).wait_recv()
            to_sibling.wait_send()
            for k in range(last):
                swap(k).wait_send()
                chip_sum(k).wait_send()

    chip_panel = lambda s, t: (t, (2 * lax.axis_index("x") + lax.axis_index("y")) ^ (last - s))
    sems = pltpu.SemaphoreType.DMA((8,))
    return _call(
        body, name=name, grid=(N_CHIPS, nt),
        in_specs=[pl.BlockSpec((ts, D), lambda s, t: (t, 0)), pl.BlockSpec((ts, panel), chip_panel)],
        out_specs=[HBM_SPEC], out_shape=[jax.ShapeDtypeStruct((N_CHIPS + 1, D, SHARD_IN), BF16)],
        scratch_shapes=[pltpu.VMEM((D, panel), F32), pltpu.VMEM((N_CHIPS, 2, D, SHARD_IN), BF16),
                        pltpu.VMEM((last, D, SHARD_IN), BF16), pltpu.VMEM((last, D, SHARD_IN), BF16), sems, sems],
        args=(h, dp), sides=sides)


def _adamw(w, g, m, v):
    m = ADAM_B1 * m + (1.0 - ADAM_B1) * g
    v = ADAM_B2 * v + (1.0 - ADAM_B2) * (g * g)
    m_hat = m / (1.0 - ADAM_B1 ** ADAM_STEP)
    v_hat = v / (1.0 - ADAM_B2 ** ADAM_STEP)
    delta = -ADAM_LR * (m_hat / (jnp.sqrt(v_hat) + ADAM_EPS) + ADAM_WD * w)
    return delta, m, v


def _sum_update(parts, w, m, v, name):
    R, C = w.shape
    rb = min(R, 256)
    n_parts = parts.shape[0]

    def body(p_ref, w_ref, m_ref, v_ref, g_out, d_out, m_out, v_out):
        acc = p_ref[0].astype(F32)
        for d in range(1, n_parts):
            acc = acc + p_ref[d].astype(F32)
        g_out[...] = acc
        delta, m_new, v_new = _adamw(w_ref[...], acc, m_ref[...], v_ref[...])
        d_out[...] = delta
        m_out[...] = m_new
        v_out[...] = v_new

    blk = pl.BlockSpec((rb, C), lambda i: (i, 0))
    return pl.pallas_call(
        body, name=name, grid=(R // rb,),
        in_specs=[pl.BlockSpec((n_parts, rb, C), lambda i: (0, i, 0)), blk, blk, blk], out_specs=[blk] * 4,
        out_shape=[jax.ShapeDtypeStruct((R, C), F32)] * 4, compiler_params=_params(),
    )(parts, w, m, v)


def _small_sum_update(partials, row_of, weights, loss_parts):
    n_p, n_w = len(partials), len(weights)

    def body(*refs):
        p_refs = refs[:n_p]
        loss_ref = refs[n_p]
        w_refs = refs[n_p + 1:n_p + 1 + 3 * n_w]
        loss_out = refs[n_p + 1 + 3 * n_w]
        outs = refs[n_p + 2 + 3 * n_w:]

        def total(ref):
            acc = ref[0].astype(F32)
            for d in range(1, N_DEV):
                acc = acc + ref[d].astype(F32)
            return acc

        lsum = jnp.sum(total(loss_ref), axis=-1, keepdims=True) * (0.5 / D)
        loss_out[...] = jnp.broadcast_to(lsum, loss_out.shape)
        for k in range(n_p):
            i, r0 = row_of[k]
            g = total(p_refs[k])
            w_ref, m_ref, v_ref = w_refs[3 * i:3 * i + 3]
            g_out, d_out, m_out, v_out = outs[4 * i:4 * i + 4]
            if g.ndim == 2:
                at = (slice(r0, r0 + g.shape[0]), slice(None))
            else:
                at = (slice(None),) * g.ndim
            delta, m_new, v_new = _adamw(w_ref[at], g, m_ref[at], v_ref[at])
            g_out[at] = g
            d_out[at] = delta
            m_out[at] = m_new
            v_out[at] = v_new

    vm = pl.BlockSpec(memory_space=pltpu.VMEM)
    flat_w = [a for wmv in weights for a in wmv]
    out_shape = [jax.ShapeDtypeStruct((1, 128), F32)]
    for w, _, _ in weights:
        out_shape += [jax.ShapeDtypeStruct(w.shape, F32)] * 4
    return pl.pallas_call(
        body, name="small_update", in_specs=[vm] * (n_p + 1 + 3 * n_w), out_specs=[vm] * len(out_shape),
        out_shape=out_shape, compiler_params=pltpu.CompilerParams(vmem_limit_bytes=VMEM_LIMIT),
    )(*partials, loss_parts, *flat_w)


def _pack_small_shard(cw, lg, lb):
    return (jnp.pad(cw, ((0, 13), (0, 64))) + jnp.pad(lg, ((8, 7), (0, 0))) + jnp.pad(lb, ((9, 6), (0, 0))))


def kernel(x, pre_norm, post_norm, even_w_in, even_conv_w, even_pool_w, even_pool_scale, even_w_out, odd_w_in, odd_ln_g, odd_ln_b, odd_w_s, odd_b_s, odd_w_out, loss_target, m_pre_norm, m_post_norm, m_even_w_in, m_even_conv_w, m_even_pool_w, m_even_pool_scale, m_even_w_out, m_odd_w_in, m_odd_ln_g, m_odd_ln_b, m_odd_w_s, m_odd_b_s, m_odd_w_out, v_pre_norm, v_post_norm, v_even_w_in, v_even_conv_w, v_even_pool_w, v_even_pool_scale, v_even_w_out, v_odd_w_in, v_odd_ln_g, v_odd_ln_b, v_odd_w_s, v_odd_b_s, v_odd_w_out):
    S = x.shape[1]
    xs = x.reshape(S, D)
    tgt = loss_target.reshape(S, D)

    gpre0, gpre1 = pre_norm[0:1], pre_norm[1:2]
    gpost0, gpost1 = post_norm[0:1], post_norm[1:2]
    small_shard = _pack_small_shard(even_conv_w[0], odd_ln_g, odd_ln_b)
    h0, wie_g, small_g, woe_g, wio_b, woo_b = _prenorm_gather(xs, gpre0, even_w_in[0], small_shard, even_w_out[0],
                                                              odd_w_in[0], odd_w_out[0], TS_PRENORM)
    wie = jnp.transpose(wie_g, (1, 0, 2)).reshape(D, PROJ)
    wie_t = jnp.transpose(wie_g, (0, 2, 1)).reshape(PROJ, D)
    woe = woe_g.reshape(D, D)
    woe_t = woe.T
    conv_w = jnp.transpose(small_g[:, 0:3, 0:64], (1, 0, 2)).reshape(3, A_W)
    ln_g = small_g[:, 8, :].reshape(1, D)
    ln_b = small_g[:, 9, :].reshape(1, D)

    pool_w_b = even_pool_w[0].astype(BF16)
    pool_wt_b = jnp.swapaxes(even_pool_w[0], 1, 2).astype(BF16)
    ws_tril = jnp.tril(odd_w_s[0])
    ws_b = ws_tril.astype(BF16)
    wst_b = jnp.swapaxes(ws_tril, 1, 2).astype(BF16)
    b_full = jnp.broadcast_to(odd_b_s[0][:, :, None], (N_HEADS, GROUP, GROUP))

    p0, m0, x1, h1, wio_g, woo_g = _even_forward_fused(h0, xs, wie, conv_w, pool_w_b, even_pool_scale, woe, gpost0,
                                                       gpre1, TS_FWD, sides=[(_Gather, [wio_b, woo_b])])
    wio = jnp.transpose(wio_g, (1, 0, 2)).reshape(D, PROJ)
    wio_t = jnp.transpose(wio_g, (0, 2, 1)).reshape(PROJ, D)
    woo = woo_g.reshape(D, D)
    woo_t = woo.T
    p1, m1, g2, loss_vec = _odd_forward(h1, x1, tgt, wio, ln_g, ln_b, ws_b, b_full, woo, gpost1, TS_FWD)

    dp1, g1, dwoo, dws, dbs, dlg, dlb, dgpost1, dgpre1 = _odd_backward(
        g2, m1, p1, x1, ln_g, ln_b, ws_b, wst_b, b_full, woo_t, wio_t, gpost1, gpre1, TS_BWD)
    by_owner = lambda dwo: dwo.reshape(N_DEV, SHARD_OUT, D)
    (dwio_parts, dwoo_parts, dlg_parts, dlb_parts,
     dws_g, dbs_g, gpost1_g, gpre1_g, loss_g) = _weight_grad_scatter(
        h1, dp1, TS_WGRAD, "odd_weight_grad",
        sides=[(_Scatter, [by_owner(dwoo), dlg.reshape(N_DEV, 1, 128), dlb.reshape(N_DEV, 1, 128)]),
               (_Gather, [dws.astype(BF16), dbs, dgpost1, dgpre1, loss_vec])])
    dp0, gx, dwoe, dcw, dpw, dps, dgpost0, dgpre0 = _even_backward_fused(
        g1, m0, p0, xs, conv_w, pool_w_b, pool_wt_b, even_pool_scale, woe_t, wie_t, gpost0, gpre0, TS_FWD)
    dcw_by_owner = jnp.transpose(dcw[0:3].reshape(3, N_DEV, 64), (1, 0, 2))
    dwie_parts, dwoe_parts, dcw_parts, dpw_g, dps_g, gpost0_g, gpre0_g = _weight_grad_scatter(
        h0, dp0, TS_WGRAD, "even_weight_grad",
        sides=[(_Scatter, [by_owner(dwoe), dcw_by_owner]), (_Gather, [dpw.astype(BF16), dps, dgpost0, dgpre0])])

    g_wie, d_wie, nm_wie, nv_wie = _sum_update(dwie_parts, even_w_in[0], m_even_w_in[0], v_even_w_in[0],
                                               "update_even_w_in")
    g_wio, d_wio, nm_wio, nv_wio = _sum_update(dwio_parts, odd_w_in[0], m_odd_w_in[0], v_odd_w_in[0],
                                               "update_odd_w_in")
    g_woe, d_woe, nm_woe, nv_woe = _sum_update(dwoe_parts, even_w_out[0], m_even_w_out[0], v_even_w_out[0],
                                               "update_even_w_out")
    g_woo, d_woo, nm_woo, nv_woo = _sum_update(dwoo_parts, odd_w_out[0], m_odd_w_out[0], v_odd_w_out[0],
                                               "update_odd_w_out")

    partials = [gpre0_g, gpre1_g, gpost0_g, gpost1_g, dpw_g, dps_g, dws_g, dbs_g, dcw_parts, dlg_parts, dlb_parts]
    row_of = [(0, 0), (0, 1), (1, 0), (1, 1), (2, 0), (3, 0), (4, 0), (5, 0), (6, 0), (7, 0), (8, 0)]
    weights = [(pre_norm, m_pre_norm, v_pre_norm), (post_norm, m_post_norm, v_post_norm),
               (even_pool_w[0], m_even_pool_w[0], v_even_pool_w[0]),
               (even_pool_scale, m_even_pool_scale, v_even_pool_scale),
               (odd_w_s[0], m_odd_w_s[0], v_odd_w_s[0]), (odd_b_s[0], m_odd_b_s[0], v_odd_b_s[0]),
               (even_conv_w[0], m_even_conv_w[0], v_even_conv_w[0]),
               (odd_ln_g, m_odd_ln_g, v_odd_ln_g), (odd_ln_b, m_odd_ln_b, v_odd_ln_b)]
    small = _small_sum_update(partials, row_of, weights, loss_g)
    loss = small[0][0, 0]

    def leaves(k, big):
        pre, post, pw, psc, ws, bs, cw, lg, lb = [small[1 + 4 * i + k] for i in range(len(weights))]
        wie_k, woe_k, wio_k, woo_k = big
        return [pre, post, wie_k[None], cw[None], pw[None], psc, woe_k[None], wio_k[None], lg, lb, ws[None], bs[None],
                woo_k[None]]

    outs = [loss, gx.reshape(1, S, D)]
    outs += leaves(0, (g_wie, g_woe, g_wio, g_woo))
    outs += leaves(1, (d_wie, d_woe, d_wio, d_woo))
    outs += leaves(2, (nm_wie, nm_woe, nm_wio, nm_woo))
    outs += leaves(3, (nv_wie, nv_woe, nv_wio, nv_woo))
    return tuple(outs)
```

```python
import jax
import jax.numpy as jnp
from jax import lax
from jax.experimental import pallas as pl
from jax.experimental.pallas import tpu as pltpu

F32 = jnp.float32
BF16 = jnp.bfloat16
MESH = pl.DeviceIdType.MESH

D = 1024
EPS = 1e-6
A_W = 512
B_W = 512
POOL_WINDOWS = (2, 4, 8, 16)
GROUP = 128
N_GROUPS = 4
N_HEADS = 8
HALO = 16
PROJ = 3072
N_DEV = 8
N_CHIPS = 4
SHARD_IN = PROJ // N_DEV
SHARD_OUT = D // N_DEV

ADAM_LR = 0.001
ADAM_B1 = 0.9
ADAM_B2 = 0.999
ADAM_EPS = 1e-08
ADAM_WD = 0.01
ADAM_STEP = 10

VMEM_LIMIT = 54 * 1024 * 1024

TS_PRENORM = 1024
TS_FWD = 512
TS_BWD = 256
TS_WGRAD = 2048
SUB_ROWS = 256


def _params(n_grid=1, vmem=VMEM_LIMIT):
    return pltpu.CompilerParams(dimension_semantics=("arbitrary",) * n_grid, vmem_limit_bytes=vmem)


def _full(shape):
    return pl.BlockSpec(shape, lambda *_: (0,) * len(shape))


def _resident(shape):
    return pl.BlockSpec(shape, lambda *_: (0,) * len(shape), pipeline_mode=pl.Buffered(1))


def _sigmoid(z):
    return jax.nn.sigmoid(z)


def _dsilu(z, s):
    return s * (1.0 + z * (1.0 - s))


def _dot(a, b):
    return jnp.dot(a, b, preferred_element_type=F32)


def _dot_t0(a, b):
    return lax.dot_general(a, b, (((0,), (0,)), ((), ())), preferred_element_type=F32)


def _dot_t1(a, b):
    return lax.dot_general(a, b, (((1,), (1,)), ((), ())), preferred_element_type=F32)


def _rms(x):
    return lax.rsqrt(jnp.mean(x * x, axis=-1, keepdims=True) + EPS)


def _shift_down(a, k):
    return pltpu.roll(a, k, 0)


def _shift_up(a, k):
    return pltpu.roll(a, a.shape[0] - k, 0)


def _mesh_position():
    return lax.axis_index("x"), lax.axis_index("y"), lax.axis_index("c")


def _index(pos):
    return 4 * pos[0] + 2 * pos[1] + pos[2]


def _peer(pos, d):
    x, y, c = pos
    return (1 - x if d & 4 else x, 1 - y if d & 2 else y, 1 - c if d & 1 else c)


def _remote(src, dst, send_sems, recv_sems, k, to):
    return pltpu.make_async_remote_copy(src_ref=src, dst_ref=dst, send_sem=send_sems.at[k], recv_sem=recv_sems.at[k],
                                        device_id=to, device_id_type=MESH)


class _Gather:
    def __init__(self, srcs, dsts, send_sems, recv_sems):
        x, y, c = _mesh_position()
        n = len(srcs)
        me, sibling = (x, y, c), (x, y, 1 - c)
        chips = [(1 - x, y), (x, 1 - y), (1 - x, 1 - y)]

        def copy(k, t, block, to, src=None):
            slot = dsts[t].at[_index(block)]
            return _remote(slot if src is None else src, slot, send_sems, recv_sems, k * n + t, to)

        tensors = range(n)
        self.local = [pltpu.make_async_copy(srcs[t], dsts[t].at[_index(me)], recv_sems.at[7 * n + t]) for t in tensors]
        self.first = [copy(1 + j, t, me, chip + (c,), srcs[t]) for t in tensors for j, chip in enumerate(chips)]
        self.first += [copy(0, t, me, sibling, srcs[t]) for t in tensors]
        self.ici_in = [copy(1 + j, t, chip + (c,), me) for t in tensors for j, chip in enumerate(chips)]
        self.passed = [copy(4 + j, t, chip + (c,), sibling) for t in tensors for j, chip in enumerate(chips)]
        self.d2d_in = [copy(0, t, sibling, me) for t in tensors]
        self.d2d_in += [copy(4 + j, t, chip + (1 - c,), me) for t in tensors for j, chip in enumerate(chips)]

    def start(self):
        for cp in self.local + self.first:
            cp.start()

    def middle(self):
        for landed, onward in zip(self.ici_in, self.passed):
            landed.wait_recv()
            onward.start()

    def finish(self):
        for cp in self.d2d_in:
            cp.wait_recv()
        for cp in self.first + self.passed:
            cp.wait_send()
        for cp in self.local:
            cp.wait()


class _Scatter:
    def __init__(self, srcs, dsts, send_sems, recv_sems):
        pos = _mesh_position()
        n = len(srcs)
        self.local = [pltpu.make_async_copy(srcs[t].at[_index(pos)], dsts[t].at[0], recv_sems.at[t]) for t in range(n)]
        self.remote = []
        for d in range(1, N_DEV):
            to = _peer(pos, d)
            self.remote += [_remote(srcs[t].at[_index(to)], dsts[t].at[d], send_sems, recv_sems, d * n + t, to)
                            for t in range(n)]

    def start(self):
        for cp in self.local + self.remote:
            cp.start()

    def middle(self):
        pass

    def finish(self):
        for cp in self.local:
            cp.wait()
        for cp in self.remote:
            cp.wait_recv()
        for cp in self.remote:
            cp.wait_send()


HBM_SPEC = pl.BlockSpec(memory_space=pltpu.HBM)
MIDDLE_STEPS_BEFORE_END = 4


def _landing_shape(exchange, src):
    return (N_DEV,) + src.shape if exchange is _Gather else src.shape


def _call(body, *, name, grid, in_specs, out_specs, out_shape, scratch_shapes, args, sides=()):
    params = _params(len(grid))
    if not sides:
        return pl.pallas_call(body, name=name, grid=grid, in_specs=in_specs, out_specs=out_specs, out_shape=out_shape,
                              scratch_shapes=scratch_shapes, compiler_params=params)(*args)
    n_in, n_out, n_scratch = len(in_specs), len(out_specs), len(scratch_shapes)
    counts = [len(srcs) for _, srcs in sides]
    ns = sum(counts)
    total = 1
    for g in grid:
        total *= g
    side_args, side_shapes, side_sems = [], [], []
    for exchange, srcs in sides:
        side_args += list(srcs)
        side_shapes += [jax.ShapeDtypeStruct(_landing_shape(exchange, s), s.dtype) for s in srcs]
        side_sems += [pltpu.SemaphoreType.DMA((N_DEV * len(srcs),))] * 2

    def wrapped(*refs):
        ins, side_in = refs[:n_in], refs[n_in:n_in + ns]
        outs = refs[n_in + ns:n_in + ns + n_out]
        side_out = refs[n_in + ns + n_out:n_in + 2 * ns + n_out]
        rest = refs[n_in + 2 * ns + n_out:]
        scratch, sems = rest[:n_scratch], rest[n_scratch:]
        step = pl.program_id(0)
        for axis in range(1, len(grid)):
            step = step * grid[axis] + pl.program_id(axis)

        def exchanges():
            built, at = [], 0
            for k, (exchange, _) in enumerate(sides):
                built.append(exchange(side_in[at:at + counts[k]], side_out[at:at + counts[k]],
                                      sems[2 * k], sems[2 * k + 1]))
                at += counts[k]
            return built

        @pl.when(step == 0)
        def _():
            for ex in exchanges():
                ex.start()

        @pl.when(step == max(total - MIDDLE_STEPS_BEFORE_END, 0))
        def _():
            for ex in exchanges():
                ex.middle()

        body(*ins, *outs, *scratch)

        @pl.when(step == total - 1)
        def _():
            for ex in exchanges():
                ex.finish()

    return pl.pallas_call(
        wrapped, name=name, grid=grid, in_specs=list(in_specs) + [HBM_SPEC] * ns,
        out_specs=list(out_specs) + [HBM_SPEC] * ns, out_shape=list(out_shape) + side_shapes,
        scratch_shapes=list(scratch_shapes) + side_sems, compiler_params=params)(*args, *side_args)


def _prenorm_gather(x, g, wie, small, woe, wio, woo, ts):
    S = x.shape[0]
    nt = S // ts

    def body(x_ref, g_ref, wie_ref, sm_ref, woe_ref, wio_ref, woo_ref,
             h_ref, wie_out, sm_out, woe_out, wio_b, woo_b, cast_ref, small_ref, cast2_ref, send_sems, recv_sems):
        step = pl.program_id(0)
        gather = lambda: _Gather([cast_ref, small_ref, cast2_ref], [wie_out, sm_out, woe_out], send_sems, recv_sems)

        @pl.when(step == 0)
        def _():
            cast_ref[...] = wie_ref[...].astype(BF16)
            small_ref[...] = sm_ref[...]
            cast2_ref[...] = woe_ref[...].astype(BF16)
            gather().start()
            wio_b[...] = wio_ref[...].astype(BF16)
            woo_b[...] = woo_ref[...].astype(BF16)

        xv = x_ref[...]
        h_ref[...] = ((xv * _rms(xv)) * g_ref[...]).astype(BF16)

        @pl.when(step == nt - 1)
        def _():
            exchange = gather()
            exchange.middle()
            exchange.finish()

    tile = pl.BlockSpec((ts, D), lambda i: (i, 0))
    sems = pltpu.SemaphoreType.DMA((N_DEV * 3,))
    shards = [wie, small, woe, wio, woo]
    return pl.pallas_call(
        body, name="prenorm_gather", grid=(nt,),
        in_specs=[tile, _full((1, D))] + [_full(a.shape) for a in shards],
        out_specs=[tile, HBM_SPEC, HBM_SPEC, HBM_SPEC] + [_full(a.shape) for a in shards[3:]],
        out_shape=[jax.ShapeDtypeStruct((S, D), BF16), jax.ShapeDtypeStruct((N_DEV,) + wie.shape, BF16),
                   jax.ShapeDtypeStruct((N_DEV,) + small.shape, F32), jax.ShapeDtypeStruct((N_DEV,) + woe.shape, BF16)]
        + [jax.ShapeDtypeStruct(a.shape, BF16) for a in shards[3:]],
        scratch_shapes=[pltpu.VMEM(wie.shape, BF16), pltpu.VMEM(small.shape, F32), pltpu.VMEM(woe.shape, BF16),
                        sems, sems],
        compiler_params=_params(),
    )(x, g, wie, small, woe, wio, woo)


def _even_mixer(proj, halo, row0, cw_ref, pw_ref):
    xa, gb, gc, za, xp, zp = proj
    nr = xa.shape[0]
    ha = gc * xa
    ha_ext = jnp.concatenate([halo[0], ha], axis=0)
    ha_m1 = _shift_down(ha_ext, 1)[HALO:]
    ha_m2 = _shift_down(ha_ext, 2)[HALO:]
    conv = cw_ref[2:3, :] * ha + cw_ref[1:2, :] * ha_m1 + cw_ref[0:1, :] * ha_m2
    sig_a = _sigmoid(za)
    silu_a = za * sig_a

    xp_ext = jnp.concatenate([halo[1], xp], axis=0)
    pos = row0 + lax.broadcasted_iota(jnp.int32, (nr, 1), 0)
    pooled, inv_cnt, mixed = [], [], []
    for g, w in enumerate(POOL_WINDOWS):
        cols = slice(g * GROUP, (g + 1) * GROUP)
        s = xp_ext[:, cols]
        k = 1
        while k < w:
            s = s + _shift_down(s, k)
            k *= 2
        inv = 1.0 / jnp.minimum(pos + 1, w).astype(F32)
        pg = s[HALO:] * inv - xp[:, cols]
        pooled.append(pg)
        inv_cnt.append(inv)
        mixed.append(_dot(pg.astype(BF16), pw_ref[g]))
    mixed = jnp.concatenate(mixed, axis=1)
    sig_b = _sigmoid(zp)
    silu_b = zp * sig_b
    return dict(xa=xa, gb=gb, gc=gc, za=za, xp=xp, zp=zp, ha=ha, ha_m1=ha_m1, ha_m2=ha_m2, conv=conv,
                sig_a=sig_a, silu_a=silu_a, pooled=pooled, inv_cnt=inv_cnt, mixed=mixed, sig_b=sig_b,
                silu_b=silu_b)


def _columns(a):
    return [a[:, k * A_W:(k + 1) * A_W].astype(F32) for k in range(6)]


def _even_forward_fused(h, x, win_b, cw, pw_b, ps, wout_b, gpost, gpre_next, ts, sides=()):
    S = x.shape[0]

    def body(h_ref, x_ref, wi_ref, cw_ref, pw_ref, ps_ref, wo_ref, gp_ref, gn_ref,
             p_ref, m_ref, x1_ref, h1_ref, wot_ref, mix_ref, carry_ref):
        i = pl.program_id(0)

        @pl.when(i == 0)
        def _():
            carry_ref[...] = jnp.zeros_like(carry_ref)
            wot_ref[...] = wo_ref[...].T

        for r0 in range(0, ts, SUB_ROWS):
            rows = slice(r0, r0 + SUB_ROWS)
            proj = _dot(h_ref[rows, :], wi_ref[...])
            p_ref[rows, :] = proj.astype(BF16)
            fw = _even_mixer(_columns(proj), (carry_ref[:, 0:A_W], carry_ref[:, A_W:D]), i * ts + r0, cw_ref, pw_ref)
            carry_ref[:, 0:A_W] = fw["ha"][SUB_ROWS - HALO:]
            carry_ref[:, A_W:D] = fw["xp"][SUB_ROWS - HALO:]
            mix_ref[rows, 0:A_W] = (fw["gb"] * fw["conv"] * fw["silu_a"]).astype(BF16)
            mix_ref[rows, A_W:D] = (fw["mixed"] * ps_ref[...] * fw["silu_b"]).astype(BF16)
            m = _dot(mix_ref[rows, :], wo_ref[...])
            m_ref[rows, :] = m.astype(BF16)
            x1 = x_ref[rows, :] + (m * _rms(m)) * gp_ref[...]
            x1_ref[rows, :] = x1
            h1_ref[rows, :] = ((x1 * _rms(x1)) * gn_ref[...]).astype(BF16)

    tile = pl.BlockSpec((ts, D), lambda i: (i, 0))
    return _call(
        body, name="even_forward", grid=(S // ts,),
        in_specs=[tile, tile, _resident((D, PROJ)), _full((3, A_W)), _full((N_GROUPS, GROUP, GROUP)), _full((1, B_W)),
                  _resident((D, D)), _full((1, D)), _full((1, D))],
        out_specs=[pl.BlockSpec((ts, PROJ), lambda i: (i, 0)), tile, tile, tile, _full((D, D))],
        out_shape=[jax.ShapeDtypeStruct((S, PROJ), BF16), jax.ShapeDtypeStruct((S, D), BF16),
                   jax.ShapeDtypeStruct((S, D), F32), jax.ShapeDtypeStruct((S, D), BF16),
                   jax.ShapeDtypeStruct((D, D), BF16)],
        scratch_shapes=[pltpu.VMEM((ts, D), BF16), pltpu.VMEM((HALO, D), F32)],
        args=(h, x, win_b, cw, pw_b, ps, wout_b, gpost, gpre_next), sides=sides)


def _even_backward_fused(g1, m0, p, x, cw, pw_b, pwt_b, ps, woutt_b, wint_b, gpost, gpre, ts):
    S = g1.shape[0]
    nt = S // ts

    def body(g_ref, m_ref, p_ref, halo_ref, x_ref, cw_ref, pw_ref, pwt_ref, ps_ref, wot_ref, wit_ref, gp_ref, gn_ref,
             dp_ref, dx_ref, dwob_ref, dcw_ref, dpwb_ref, dps_ref, dgp_ref, dgn_ref,
             carry_ref, mix_ref, dm_ref, dwo_ref, dpw_ref):
        step = pl.program_id(0)
        i = nt - 1 - step

        @pl.when(step == 0)
        def _():
            carry_ref[...] = jnp.zeros_like(carry_ref)
            dwo_ref[...] = jnp.zeros_like(dwo_ref)
            dcw_ref[...] = jnp.zeros_like(dcw_ref)
            dpw_ref[...] = jnp.zeros_like(dpw_ref)
            dps_ref[...] = jnp.zeros_like(dps_ref)
            dgp_ref[...] = jnp.zeros_like(dgp_ref)
            dgn_ref[...] = jnp.zeros_like(dgn_ref)

        ps_v = ps_ref[...]
        sums = [None] * 6
        add = lambda k, part: sums.__setitem__(k, part if sums[k] is None else sums[k] + part)
        for r0 in range(ts - SUB_ROWS, -1, -SUB_ROWS):
            rows = slice(r0, r0 + SUB_ROWS)
            g = g_ref[rows, :]
            m = m_ref[rows, :].astype(F32)
            q = _rms(m)
            n = m * q
            add(0, jnp.sum(g * n, axis=0, keepdims=True))
            dn = g * gp_ref[...]
            dm = q * (dn - n * jnp.mean(dn * n, axis=-1, keepdims=True))
            dm_ref[rows, :] = dm.astype(BF16)
            dmix = _dot(dm_ref[rows, :], wot_ref[...])
            dya = dmix[:, 0:A_W]
            dyb = dmix[:, A_W:D]

            if r0 == 0:
                before = _columns(halo_ref[...])
                keep = jnp.where(i == 0, 0.0, 1.0).astype(F32)
                halo = (before[2] * before[0] * keep, before[4] * keep)
            else:
                before = _columns(p_ref[r0 - HALO:r0, :])
                halo = (before[2] * before[0], before[4])
            fw = _even_mixer(_columns(p_ref[rows, :]), halo, i * ts + r0, cw_ref, pw_ref)
            mix_ref[rows, 0:A_W] = (fw["gb"] * fw["conv"] * fw["silu_a"]).astype(BF16)
            mix_ref[rows, A_W:D] = (fw["mixed"] * ps_v * fw["silu_b"]).astype(BF16)

            t = dya * fw["gb"]
            dconv = t * fw["silu_a"]
            dgb = dya * fw["conv"] * fw["silu_a"]
            dza = t * fw["conv"] * _dsilu(fw["za"], fw["sig_a"])
            add(2, jnp.sum(dconv * fw["ha"], axis=0, keepdims=True))
            add(3, jnp.sum(dconv * fw["ha_m1"], axis=0, keepdims=True))
            add(4, jnp.sum(dconv * fw["ha_m2"], axis=0, keepdims=True))
            dconv_ext = jnp.concatenate([dconv, carry_ref[:, 0:A_W]], axis=0)
            dha = (cw_ref[2:3, :] * dconv + cw_ref[1:2, :] * _shift_up(dconv_ext, 1)[:SUB_ROWS]
                   + cw_ref[0:1, :] * _shift_up(dconv_ext, 2)[:SUB_ROWS])
            dgc = dha * fw["xa"]
            dxa = dha * fw["gc"]

            u = dyb * fw["mixed"]
            add(1, jnp.sum(u * fw["silu_b"], axis=0, keepdims=True))
            dzp = u * ps_v * _dsilu(fw["zp"], fw["sig_b"])
            dmixed = (dyb * ps_v * fw["silu_b"]).astype(BF16)
            dxp, e_first = [], []
            for gi, w in enumerate(POOL_WINDOWS):
                cols = slice(gi * GROUP, (gi + 1) * GROUP)
                dmg = dmixed[:, cols]
                dpooled = _dot(dmg, pwt_ref[gi])
                dpw_ref[gi] += _dot_t0(fw["pooled"][gi].astype(BF16), dmg)
                e = dpooled * fw["inv_cnt"][gi]
                e_first.append(e[0:HALO])
                s = jnp.concatenate([e, carry_ref[:, A_W + gi * GROUP:A_W + (gi + 1) * GROUP]], axis=0)
                k = 1
                while k < w:
                    s = s + _shift_up(s, k)
                    k *= 2
                dxp.append(s[:SUB_ROWS] - dpooled)
            carry_ref[:, 0:A_W] = dconv[0:HALO]
            carry_ref[:, A_W:D] = jnp.concatenate(e_first, axis=1)

            dp_ref[rows, 0:512] = dxa.astype(BF16)
            dp_ref[rows, 512:1024] = dgb.astype(BF16)
            dp_ref[rows, 1024:1536] = dgc.astype(BF16)
            dp_ref[rows, 1536:2048] = dza.astype(BF16)
            dp_ref[rows, 2048:2560] = jnp.concatenate(dxp, axis=1).astype(BF16)
            dp_ref[rows, 2560:3072] = dzp.astype(BF16)

            dh = _dot(dp_ref[rows, :], wit_ref[...])
            xv = x_ref[rows, :]
            r = _rms(xv)
            xn = xv * r
            add(5, jnp.sum(dh * xn, axis=0, keepdims=True))
            dxn = dh * gn_ref[...]
            dx_ref[rows, :] = g + r * (dxn - xn * jnp.mean(dxn * xn, axis=-1, keepdims=True))

        dwo_ref[...] += _dot_t0(mix_ref[...], dm_ref[...])
        dgp_ref[...] += sums[0]
        dps_ref[...] += sums[1]
        dcw_ref[2:3, :] += sums[2]
        dcw_ref[1:2, :] += sums[3]
        dcw_ref[0:1, :] += sums[4]
        dgn_ref[...] += sums[5]

        @pl.when(step == nt - 1)
        def _():
            dwob_ref[...] = dwo_ref[...].astype(BF16)
            dpwb_ref[...] = dpw_ref[...].astype(BF16)

    rev = lambda s: (nt - 1 - s, 0)
    tile = pl.BlockSpec((ts, D), rev)
    wide = pl.BlockSpec((ts, PROJ), rev)
    bpt = ts // HALO
    halo_map = lambda s: (jnp.maximum((nt - 1 - s) * bpt - 1, 0), 0)
    groups = _full((N_GROUPS, GROUP, GROUP))
    vec = _full((1, D))
    return pl.pallas_call(
        body, name="even_backward", grid=(nt,),
        in_specs=[tile, tile, wide, pl.BlockSpec((HALO, PROJ), halo_map), tile, _full((3, A_W)), groups, groups,
                  _full((1, B_W)), _resident((D, D)), _resident((PROJ, D)), vec, vec],
        out_specs=[wide, tile, _full((D, D)), _full((8, A_W)), groups, _full((1, B_W)), vec, vec],
        out_shape=[jax.ShapeDtypeStruct((S, PROJ), BF16), jax.ShapeDtypeStruct((S, D), F32),
                   jax.ShapeDtypeStruct((D, D), BF16), jax.ShapeDtypeStruct((8, A_W), F32),
                   jax.ShapeDtypeStruct((N_GROUPS, GROUP, GROUP), BF16), jax.ShapeDtypeStruct((1, B_W), F32),
                   jax.ShapeDtypeStruct((1, D), F32), jax.ShapeDtypeStruct((1, D), F32)],
        scratch_shapes=[pltpu.VMEM((HALO, D), F32), pltpu.VMEM((ts, D), BF16), pltpu.VMEM((ts, D), BF16),
                        pltpu.VMEM((D, D), F32), pltpu.VMEM((N_GROUPS, GROUP, GROUP), F32)],
        compiler_params=_params(),
    )(g1, m0, p, p, x, cw, pw_b, pwt_b, ps, woutt_b, wint_b, gpost, gpre)


def _gate_matmuls(w_ref, in_ref, out_ref, chunks, bias_ref=None):
    for h in range(N_HEADS):
        cols = slice(h * GROUP, (h + 1) * GROUP)
        wide = jnp.concatenate([in_ref[chunk, cols] for chunk in chunks], axis=1)
        res = _dot(w_ref[h], wide)
        for k, chunk in enumerate(chunks):
            part = res[:, k * GROUP:(k + 1) * GROUP]
            out_ref[chunk, cols] = part if bias_ref is None else part + bias_ref[h]


def _odd_forward_parts(v, lg_ref, lb_ref, wt_ref, bfull_ref, vln_ref, sv_ref, r0, nr):
    rows = slice(r0, r0 + nr)
    mu = jnp.mean(v, axis=-1, keepdims=True)
    vc = v - mu
    rstd = lax.rsqrt(jnp.mean(vc * vc, axis=-1, keepdims=True) + EPS)
    vh = vc * rstd
    vln_ref[rows, :] = (vh * lg_ref[...] + lb_ref[...]).astype(BF16)
    _gate_matmuls(wt_ref, vln_ref, sv_ref, [slice(r0 + k * GROUP, r0 + (k + 1) * GROUP) for k in range(nr // GROUP)],
                  bfull_ref)
    return vh, rstd


def _odd_forward(h, x1, tgt, win_b, lg, lb, wt_b, bfull, wout_b, gpost, ts):
    S = x1.shape[0]

    def body(h_ref, x_ref, t_ref, wi_ref, lg_ref, lb_ref, wt_ref, bfull_ref, wo_ref, gp_ref,
             p_ref, m_ref, g_ref, loss_ref, wot_ref, vln_ref, sv_ref, y_ref):
        @pl.when(pl.program_id(0) == 0)
        def _():
            loss_ref[...] = jnp.zeros_like(loss_ref)
            wot_ref[...] = wo_ref[...].T

        loss = None
        for r0 in range(0, ts, SUB_ROWS):
            rows = slice(r0, r0 + SUB_ROWS)
            proj = _dot(h_ref[rows, :], wi_ref[...])
            p_ref[rows, :] = proj.astype(BF16)
            u, v, z = proj[:, 0:D], proj[:, D:2 * D], proj[:, 2 * D:3 * D]
            _odd_forward_parts(v, lg_ref, lb_ref, wt_ref, bfull_ref, vln_ref, sv_ref, r0, SUB_ROWS)
            y_ref[rows, :] = (u * sv_ref[rows, :] * (z * _sigmoid(z))).astype(BF16)
            m = _dot(y_ref[rows, :], wo_ref[...])
            m_ref[rows, :] = m.astype(BF16)
            x2 = x_ref[rows, :] + (m * _rms(m)) * gp_ref[...]
            err = x2 - t_ref[rows, :]
            g_ref[rows, :] = err * (1.0 / D)
            part = jnp.sum(err * err, axis=0, keepdims=True)
            loss = part if loss is None else loss + part
        loss_ref[...] += loss

    tile = pl.BlockSpec((ts, D), lambda i: (i, 0))
    small = _full((N_HEADS, GROUP, GROUP))
    return pl.pallas_call(
        body, name="odd_forward", grid=(S // ts,),
        in_specs=[pl.BlockSpec((ts, D), lambda i: (i, 0)), tile, tile, _full((D, PROJ)), _full((1, D)), _full((1, D)),
                  small, small, _full((D, D)), _full((1, D))],
        out_specs=[pl.BlockSpec((ts, PROJ), lambda i: (i, 0)), tile, tile, _full((1, D)), _full((D, D))],
        out_shape=[jax.ShapeDtypeStruct((S, PROJ), BF16), jax.ShapeDtypeStruct((S, D), BF16),
                   jax.ShapeDtypeStruct((S, D), F32), jax.ShapeDtypeStruct((1, D), F32),
                   jax.ShapeDtypeStruct((D, D), BF16)],
        scratch_shapes=[pltpu.VMEM((ts, D), BF16), pltpu.VMEM((ts, D), F32), pltpu.VMEM((ts, D), BF16)],
        compiler_params=_params(),
    )(h, x1, tgt, win_b, lg, lb, wt_b, bfull, wout_b, gpost)


def _odd_backward(g2, m1, p, x1, lg, lb, wt_b, wtt_b, bfull, woutt_b, wint_b, gpost, gpre, ts):
    S = g2.shape[0]
    nt = S // ts
    sub = ts // 2

    def body(g_ref, m_ref, p_ref, x_ref, lg_ref, lb_ref, wt_ref, wtt_ref, bfull_ref, wot_ref, wit_ref, gp_ref, gn_ref,
             dp_ref, dx_ref, dwob_ref, dwsb_ref, dbs_ref, dlg_ref, dlb_ref, dgp_ref, dgn_ref,
             vln_ref, sv_ref, dsvb_ref, dvln_ref, dsum_ref, y_ref, dm_ref, dwo_ref, dws_ref):
        step = pl.program_id(0)

        @pl.when(step == 0)
        def _():
            dwo_ref[...] = jnp.zeros_like(dwo_ref)
            dws_ref[...] = jnp.zeros_like(dws_ref)
            dsum_ref[...] = jnp.zeros_like(dsum_ref)
            dlg_ref[...] = jnp.zeros_like(dlg_ref)
            dlb_ref[...] = jnp.zeros_like(dlb_ref)
            dgp_ref[...] = jnp.zeros_like(dgp_ref)
            dgn_ref[...] = jnp.zeros_like(dgn_ref)

        tril = (lax.broadcasted_iota(jnp.int32, (GROUP, GROUP), 0)
                >= lax.broadcasted_iota(jnp.int32, (GROUP, GROUP), 1))
        sums = [None] * 4
        add = lambda k, part: sums.__setitem__(k, part if sums[k] is None else sums[k] + part)

        def post_norm(c):
            g = g_ref[c["rows"], :]
            m = m_ref[c["rows"], :].astype(F32)
            q = _rms(m)
            n = m * q
            add(0, jnp.sum(g * n, axis=0, keepdims=True))
            dn = g * gp_ref[...]
            dm = q * (dn - n * jnp.mean(dn * n, axis=-1, keepdims=True))
            dm_ref[c["rows"], :] = dm.astype(BF16)

        def out_projection(c):
            c["dy"] = _dot(dm_ref[c["rows"], :], wot_ref[...])

        def layernorm(c):
            v = p_ref[c["rows"], D:2 * D].astype(F32)
            mu = jnp.mean(v, axis=-1, keepdims=True)
            vc = v - mu
            c["rstd"] = lax.rsqrt(jnp.mean(vc * vc, axis=-1, keepdims=True) + EPS)
            c["vh"] = vc * c["rstd"]
            vln_ref[c["rows"], :] = (c["vh"] * lg_ref[...] + lb_ref[...]).astype(BF16)

        def gate_matmuls(c):
            _gate_matmuls(wt_ref, vln_ref, sv_ref, c["chunks"], bfull_ref)

        def gating(c):
            rows = c["rows"]
            u = p_ref[rows, 0:D].astype(F32)
            z = p_ref[rows, 2 * D:3 * D].astype(F32)
            sv = sv_ref[rows, :]
            sig = _sigmoid(z)
            sz = z * sig
            y_ref[rows, :] = (u * sv * sz).astype(BF16)
            dy = c.pop("dy")
            t = dy * sz
            dsv = t * u
            dsvb_ref[rows, :] = dsv.astype(BF16)
            for k in range(sub // GROUP):
                dsum_ref[...] += dsv[k * GROUP:(k + 1) * GROUP]
            dp_ref[rows, 0:D] = (t * sv).astype(BF16)
            dp_ref[rows, 2 * D:3 * D] = (dy * u * sv * _dsilu(z, sig)).astype(BF16)

        def gate_backward_matmuls(c):
            for chunk in c["chunks"]:
                for h in range(N_HEADS):
                    cols = slice(h * GROUP, (h + 1) * GROUP)
                    dvln_ref[chunk, cols] = _dot(wtt_ref[h], dsvb_ref[chunk, cols])
                    dws_ref[h] += jnp.where(tril, _dot_t1(dsvb_ref[chunk, cols], vln_ref[chunk, cols]), 0.0)

        def layernorm_backward(c):
            vh, rstd = c.pop("vh"), c.pop("rstd")
            dvln = dvln_ref[c["rows"], :]
            add(1, jnp.sum(dvln * vh, axis=0, keepdims=True))
            add(2, jnp.sum(dvln, axis=0, keepdims=True))
            dvh = dvln * lg_ref[...]
            dv = rstd * (dvh - jnp.mean(dvh, axis=-1, keepdims=True)
                         - vh * jnp.mean(dvh * vh, axis=-1, keepdims=True))
            dp_ref[c["rows"], D:2 * D] = dv.astype(BF16)

        def in_projection(c):
            c["dh"] = _dot(dp_ref[c["rows"], :], wit_ref[...])

        def pre_norm(c):
            dh = c.pop("dh")
            xv = x_ref[c["rows"], :]
            r = _rms(xv)
            xn = xv * r
            add(3, jnp.sum(dh * xn, axis=0, keepdims=True))
            dxn = dh * gn_ref[...]
            dx_ref[c["rows"], :] = g_ref[c["rows"], :] + r * (dxn - xn * jnp.mean(dxn * xn, axis=-1, keepdims=True))

        phases = [post_norm, out_projection, layernorm, gate_matmuls, gating, gate_backward_matmuls,
                  layernorm_backward, in_projection, pre_norm]
        groups = [dict(rows=slice(r0, r0 + sub),
                       chunks=[slice(r0 + k * GROUP, r0 + (k + 1) * GROUP) for k in range(sub // GROUP)])
                  for r0 in range(0, ts, sub)]
        for group in groups:
            for phase in phases:
                phase(group)

        dwo_ref[...] += _dot_t0(y_ref[...], dm_ref[...])
        dgp_ref[...] += sums[0]
        dlg_ref[...] += sums[1]
        dlb_ref[...] += sums[2]
        dgn_ref[...] += sums[3]

        @pl.when(step == nt - 1)
        def _():
            dwob_ref[...] = dwo_ref[...].astype(BF16)
            dwsb_ref[...] = dws_ref[...].astype(BF16)
            ones = jnp.ones((8, GROUP), F32)
            for h in range(N_HEADS):
                cols = slice(h * GROUP, (h + 1) * GROUP)
                total = lax.dot_general(ones, dsum_ref[:, cols], (((1,), (1,)), ((), ())),
                                        precision=lax.Precision.HIGHEST, preferred_element_type=F32)
                dbs_ref[h:h + 1, :] = total[0:1, :]

    tile = pl.BlockSpec((ts, D), lambda i: (i, 0))
    wide = pl.BlockSpec((ts, PROJ), lambda i: (i, 0))
    small = _full((N_HEADS, GROUP, GROUP))
    heads = _resident((N_HEADS, GROUP, GROUP))
    vec = _full((1, D))
    return pl.pallas_call(
        body, name="odd_backward", grid=(nt,),
        in_specs=[tile, tile, wide, tile, vec, vec, heads, heads, heads, _resident((D, D)), _resident((PROJ, D)),
                  vec, vec],
        out_specs=[wide, tile, _full((D, D)), small, _full((N_HEADS, GROUP)), vec, vec, vec, vec],
        out_shape=[jax.ShapeDtypeStruct((S, PROJ), BF16), jax.ShapeDtypeStruct((S, D), F32),
                   jax.ShapeDtypeStruct((D, D), BF16),
                   jax.ShapeDtypeStruct((N_HEADS, GROUP, GROUP), BF16), jax.ShapeDtypeStruct((N_HEADS, GROUP), F32),
                   jax.ShapeDtypeStruct((1, D), F32), jax.ShapeDtypeStruct((1, D), F32),
                   jax.ShapeDtypeStruct((1, D), F32), jax.ShapeDtypeStruct((1, D), F32)],
        scratch_shapes=[pltpu.VMEM((ts, D), BF16), pltpu.VMEM((ts, D), F32), pltpu.VMEM((ts, D), BF16),
                        pltpu.VMEM((ts, D), F32), pltpu.VMEM((GROUP, D), F32), pltpu.VMEM((ts, D), BF16),
                        pltpu.VMEM((ts, D), BF16), pltpu.VMEM((D, D), F32), pltpu.VMEM((N_HEADS, GROUP, GROUP), F32)],
        compiler_params=_params(),
    )(g2, m1, p, x1, lg, lb, wt_b, wtt_b, bfull, woutt_b, wint_b, gpost, gpre)


def _weight_grad_scatter(h, dp, ts, name, sides=()):
    S = h.shape[0]
    ts = min(ts, S)
    nt = S // ts
    panel = 2 * SHARD_IN
    last = N_CHIPS - 1

    def body(h_ref, dp_ref, parts_ref, acc_ref, half_ref, swap_ref, sum_ref, send_sems, recv_sems):
        s, t = pl.program_id(0), pl.program_id(1)
        x, y, c = _mesh_position()
        sibling = (x, y, 1 - c)

        def swap(k):
            return _remote(half_ref.at[k, 1], swap_ref.at[k], send_sems, recv_sems, k, sibling)

        def chip_sum(k):
            flip = last - k
            to = (x ^ (flip >> 1), y ^ (flip & 1), c)
            return _remote(sum_ref.at[k], parts_ref.at[1 + flip], send_sems, recv_sems, 3 + flip, to)

        own = pltpu.make_async_copy(half_ref.at[last, 0], parts_ref.at[0], recv_sems.at[7])
        to_sibling = _remote(half_ref.at[last, 1], parts_ref.at[1], send_sems, recv_sems, 3, sibling)

        @pl.when(t == 0)
        def _():
            acc_ref[...] = jnp.zeros_like(acc_ref)

        for k in range(last):
            @pl.when((s == k + 1) & (t == 0))
            def _():
                swap(k).wait_recv()
                sum_ref[k] = (half_ref[k, 0].astype(F32) + swap_ref[k].astype(F32)).astype(BF16)
                chip_sum(k).start()

        acc_ref[...] += _dot_t0(h_ref[...], dp_ref[...])

        @pl.when(t == nt - 1)
        def _():
            for core in range(2):
                @pl.when(c == core)
                def _():
                    half_ref[s, core] = acc_ref[:, 0:SHARD_IN].astype(BF16)
                    half_ref[s, 1 - core] = acc_ref[:, SHARD_IN:panel].astype(BF16)
            for k in range(last):
                @pl.when(s == k)
                def _():
                    swap(k).start()

        @pl.when((s == last) & (t == nt - 1))
        def _():
            own.start()
            to_sibling.start()
            own.wait()
            _remote(half_ref.at[last, 1], parts_ref.at[1], send_sems, recv_sems, 3, sibling).wait_recv()
            for k in range(last):
                chip_sum(k).wait_recv()
            to_sibling.wait_send()
            for k in range(last):
                swap(k).wait_send()
                chip_sum(k).wait_send()

    chip_panel = lambda s, t: (t, (2 * lax.axis_index("x") + lax.axis_index("y")) ^ (last - s))
    sems = pltpu.SemaphoreType.DMA((8,))
    return _call(
        body, name=name, grid=(N_CHIPS, nt),
        in_specs=[pl.BlockSpec((ts, D), lambda s, t: (t, 0)), pl.BlockSpec((ts, panel), chip_panel)],
        out_specs=[HBM_SPEC], out_shape=[jax.ShapeDtypeStruct((N_CHIPS + 1, D, SHARD_IN), BF16)],
        scratch_shapes=[pltpu.VMEM((D, panel), F32), pltpu.VMEM((N_CHIPS, 2, D, SHARD_IN), BF16),
                        pltpu.VMEM((last, D, SHARD_IN), BF16), pltpu.VMEM((last, D, SHARD_IN), BF16), sems, sems],
        args=(h, dp), sides=sides)


def _adamw(w, g, m, v):
    m = ADAM_B1 * m + (1.0 - ADAM_B1) * g
    v = ADAM_B2 * v + (1.0 - ADAM_B2) * (g * g)
    m_hat = m / (1.0 - ADAM_B1 ** ADAM_STEP)
    v_hat = v / (1.0 - ADAM_B2 ** ADAM_STEP)
    delta = -ADAM_LR * (m_hat / (jnp.sqrt(v_hat) + ADAM_EPS) + ADAM_WD * w)
    return delta, m, v


def _sum_update(parts, w, m, v, name):
    R, C = w.shape
    rb = min(R, 256)
    n_parts = parts.shape[0]

    def body(p_ref, w_ref, m_ref, v_ref, g_out, d_out, m_out, v_out):
        acc = p_ref[0].astype(F32)
        for d in range(1, n_parts):
            acc = acc + p_ref[d].astype(F32)
        g_out[...] = acc
        delta, m_new, v_new = _adamw(w_ref[...], acc, m_ref[...], v_ref[...])
        d_out[...] = delta
        m_out[...] = m_new
        v_out[...] = v_new

    blk = pl.BlockSpec((rb, C), lambda i: (i, 0))
    return pl.pallas_call(
        body, name=name, grid=(R // rb,),
        in_specs=[pl.BlockSpec((n_parts, rb, C), lambda i: (0, i, 0)), blk, blk, blk], out_specs=[blk] * 4,
        out_shape=[jax.ShapeDtypeStruct((R, C), F32)] * 4, compiler_params=_params(),
    )(parts, w, m, v)


def _small_sum_update(partials, row_of, weights, loss_parts):
    n_p, n_w = len(partials), len(weights)

    def body(*refs):
        p_refs = refs[:n_p]
        loss_ref = refs[n_p]
        w_refs = refs[n_p + 1:n_p + 1 + 3 * n_w]
        loss_out = refs[n_p + 1 + 3 * n_w]
        outs = refs[n_p + 2 + 3 * n_w:]

        def total(ref):
            acc = ref[0].astype(F32)
            for d in range(1, N_DEV):
                acc = acc + ref[d].astype(F32)
            return acc

        lsum = jnp.sum(total(loss_ref), axis=-1, keepdims=True) * (0.5 / D)
        loss_out[...] = jnp.broadcast_to(lsum, loss_out.shape)
        for k in range(n_p):
            i, r0 = row_of[k]
            g = total(p_refs[k])
            w_ref, m_ref, v_ref = w_refs[3 * i:3 * i + 3]
            g_out, d_out, m_out, v_out = outs[4 * i:4 * i + 4]
            if g.ndim == 2:
                at = (slice(r0, r0 + g.shape[0]), slice(None))
            else:
                at = (slice(None),) * g.ndim
            delta, m_new, v_new = _adamw(w_ref[at], g, m_ref[at], v_ref[at])
            g_out[at] = g
            d_out[at] = delta
            m_out[at] = m_new
            v_out[at] = v_new

    vm = pl.BlockSpec(memory_space=pltpu.VMEM)
    flat_w = [a for wmv in weights for a in wmv]
    out_shape = [jax.ShapeDtypeStruct((1, 128), F32)]
    for w, _, _ in weights:
        out_shape += [jax.ShapeDtypeStruct(w.shape, F32)] * 4
    return pl.pallas_call(
        body, name="small_update", in_specs=[vm] * (n_p + 1 + 3 * n_w), out_specs=[vm] * len(out_shape),
        out_shape=out_shape, compiler_params=pltpu.CompilerParams(vmem_limit_bytes=VMEM_LIMIT),
    )(*partials, loss_parts, *flat_w)


def _pack_small_shard(cw, lg, lb):
    return (jnp.pad(cw, ((0, 13), (0, 64))) + jnp.pad(lg, ((8, 7), (0, 0))) + jnp.pad(lb, ((9, 6), (0, 0))))


def kernel(x, pre_norm, post_norm, even_w_in, even_conv_w, even_pool_w, even_pool_scale, even_w_out, odd_w_in, odd_ln_g, odd_ln_b, odd_w_s, odd_b_s, odd_w_out, loss_target, m_pre_norm, m_post_norm, m_even_w_in, m_even_conv_w, m_even_pool_w, m_even_pool_scale, m_even_w_out, m_odd_w_in, m_odd_ln_g, m_odd_ln_b, m_odd_w_s, m_odd_b_s, m_odd_w_out, v_pre_norm, v_post_norm, v_even_w_in, v_even_conv_w, v_even_pool_w, v_even_pool_scale, v_even_w_out, v_odd_w_in, v_odd_ln_g, v_odd_ln_b, v_odd_w_s, v_odd_b_s, v_odd_w_out):
    S = x.shape[1]
    xs = x.reshape(S, D)
    tgt = loss_target.reshape(S, D)

    gpre0, gpre1 = pre_norm[0:1], pre_norm[1:2]
    gpost0, gpost1 = post_norm[0:1], post_norm[1:2]
    small_shard = _pack_small_shard(even_conv_w[0], odd_ln_g, odd_ln_b)
    h0, wie_g, small_g, woe_g, wio_b, woo_b = _prenorm_gather(xs, gpre0, even_w_in[0], small_shard, even_w_out[0],
                                                              odd_w_in[0], odd_w_out[0], TS_PRENORM)
    wie = jnp.transpose(wie_g, (1, 0, 2)).reshape(D, PROJ)
    wie_t = jnp.transpose(wie_g, (0, 2, 1)).reshape(PROJ, D)
    woe = woe_g.reshape(D, D)
    conv_w = jnp.transpose(small_g[:, 0:3, 0:64], (1, 0, 2)).reshape(3, A_W)
    ln_g = small_g[:, 8, :].reshape(1, D)
    ln_b = small_g[:, 9, :].reshape(1, D)

    pool_w_b = even_pool_w[0].astype(BF16)
    pool_wt_b = jnp.swapaxes(even_pool_w[0], 1, 2).astype(BF16)
    ws_tril = jnp.tril(odd_w_s[0])
    ws_b = ws_tril.astype(BF16)
    wst_b = jnp.swapaxes(ws_tril, 1, 2).astype(BF16)
    b_full = jnp.broadcast_to(odd_b_s[0][:, :, None], (N_HEADS, GROUP, GROUP))

    p0, m0, x1, h1, woe_t, wio_g, woo_g = _even_forward_fused(h0, xs, wie, conv_w, pool_w_b, even_pool_scale, woe,
                                                              gpost0, gpre1, TS_FWD, sides=[(_Gather, [wio_b, woo_b])])
    wio = jnp.transpose(wio_g, (1, 0, 2)).reshape(D, PROJ)
    wio_t = jnp.transpose(wio_g, (0, 2, 1)).reshape(PROJ, D)
    woo = woo_g.reshape(D, D)
    p1, m1, g2, loss_vec, woo_t = _odd_forward(h1, x1, tgt, wio, ln_g, ln_b, ws_b, b_full, woo, gpost1, TS_FWD)

    dp1, g1, dwoo, dws, dbs, dlg, dlb, dgpost1, dgpre1 = _odd_backward(
        g2, m1, p1, x1, ln_g, ln_b, ws_b, wst_b, b_full, woo_t, wio_t, gpost1, gpre1, TS_BWD)
    by_owner = lambda dwo: dwo.reshape(N_DEV, SHARD_OUT, D)
    (dwio_parts, dwoo_parts, dlg_parts, dlb_parts,
     dws_g, dbs_g, gpost1_g, gpre1_g, loss_g) = _weight_grad_scatter(
        h1, dp1, TS_WGRAD, "odd_weight_grad",
        sides=[(_Scatter, [by_owner(dwoo), dlg.reshape(N_DEV, 1, 128), dlb.reshape(N_DEV, 1, 128)]),
               (_Gather, [dws, dbs, dgpost1, dgpre1, loss_vec])])
    dp0, gx, dwoe, dcw, dpw, dps, dgpost0, dgpre0 = _even_backward_fused(
        g1, m0, p0, xs, conv_w, pool_w_b, pool_wt_b, even_pool_scale, woe_t, wie_t, gpost0, gpre0, TS_FWD)
    dcw_by_owner = jnp.transpose(dcw[0:3].reshape(3, N_DEV, 64), (1, 0, 2))
    dwie_parts, dwoe_parts, dcw_parts, dpw_g, dps_g, gpost0_g, gpre0_g = _weight_grad_scatter(
        h0, dp0, TS_WGRAD, "even_weight_grad",
        sides=[(_Scatter, [by_owner(dwoe), dcw_by_owner]), (_Gather, [dpw, dps, dgpost0, dgpre0])])

    g_wie, d_wie, nm_wie, nv_wie = _sum_update(dwie_parts, even_w_in[0], m_even_w_in[0], v_even_w_in[0],
                                               "update_even_w_in")
    g_wio, d_wio, nm_wio, nv_wio = _sum_update(dwio_parts, odd_w_in[0], m_odd_w_in[0], v_odd_w_in[0],
                                               "update_odd_w_in")
    g_woe, d_woe, nm_woe, nv_woe = _sum_update(dwoe_parts, even_w_out[0], m_even_w_out[0], v_even_w_out[0],
                                               "update_even_w_out")
    g_woo, d_woo, nm_woo, nv_woo = _sum_update(dwoo_parts, odd_w_out[0], m_odd_w_out[0], v_odd_w_out[0],
                                               "update_odd_w_out")

    partials = [gpre0_g, gpre1_g, gpost0_g, gpost1_g, dpw_g, dps_g, dws_g, dbs_g, dcw_parts, dlg_parts, dlb_parts]
    row_of = [(0, 0), (0, 1), (1, 0), (1, 1), (2, 0), (3, 0), (4, 0), (5, 0), (6, 0), (7, 0), (8, 0)]
    weights = [(pre_norm, m_pre_norm, v_pre_norm), (post_norm, m_post_norm, v_post_norm),
               (even_pool_w[0], m_even_pool_w[0], v_even_pool_w[0]),
               (even_pool_scale, m_even_pool_scale, v_even_pool_scale),
               (odd_w_s[0], m_odd_w_s[0], v_odd_w_s[0]), (odd_b_s[0], m_odd_b_s[0], v_odd_b_s[0]),
               (even_conv_w[0], m_even_conv_w[0], v_even_conv_w[0]),
               (odd_ln_g, m_odd_ln_g, v_odd_ln_g), (odd_ln_b, m_odd_ln_b, v_odd_ln_b)]
    small = _small_sum_update(partials, row_of, weights, loss_g)
    loss = small[0][0, 0]

    def leaves(k, big):
        pre, post, pw, psc, ws, bs, cw, lg, lb = [small[1 + 4 * i + k] for i in range(len(weights))]
        wie_k, woe_k, wio_k, woo_k = big
        return [pre, post, wie_k[None], cw[None], pw[None], psc, woe_k[None], wio_k[None], lg, lb, ws[None], bs[None],
                woo_k[None]]

    outs = [loss, gx.reshape(1, S, D)]
    outs += leaves(0, (g_wie, g_woe, g_wio, g_woo))
    outs += leaves(1, (d_wie, d_woe, d_wio, d_woo))
    outs += leaves(2, (nm_wie, nm_woe, nm_wio, nm_woo))
    outs += leaves(3, (nv_wie, nv_woe, nv_wio, nv_woo))
    return tuple(outs)
```

```python
import jax
import jax.numpy as jnp
from jax import lax
from jax.experimental import pallas as pl
from jax.experimental.pallas import tpu as pltpu

F32 = jnp.float32
BF16 = jnp.bfloat16
MESH = pl.DeviceIdType.MESH

D = 1024
EPS = 1e-6
A_W = 512
B_W = 512
POOL_WINDOWS = (2, 4, 8, 16)
GROUP = 128
N_GROUPS = 4
N_HEADS = 8
HALO = 16
PROJ = 3072
N_DEV = 8
N_CHIPS = 4
SHARD_IN = PROJ // N_DEV
SHARD_OUT = D // N_DEV

ADAM_LR = 0.001
ADAM_B1 = 0.9
ADAM_B2 = 0.999
ADAM_EPS = 1e-08
ADAM_WD = 0.01
ADAM_STEP = 10

VMEM_LIMIT = 54 * 1024 * 1024

TS_PRENORM = 1024
TS_FWD = 512
TS_BWD = 256
TS_WGRAD = 2048
SUB_ROWS = 256


def _params(n_grid=1, vmem=VMEM_LIMIT):
    return pltpu.CompilerParams(dimension_semantics=("arbitrary",) * n_grid, vmem_limit_bytes=vmem)


def _full(shape):
    return pl.BlockSpec(shape, lambda *_: (0,) * len(shape))


def _resident(shape):
    return pl.BlockSpec(shape, lambda *_: (0,) * len(shape), pipeline_mode=pl.Buffered(1))


def _sigmoid(z):
    return jax.nn.sigmoid(z)


def _dsilu(z, s):
    return s * (1.0 + z * (1.0 - s))


def _dot(a, b):
    return jnp.dot(a, b, preferred_element_type=F32)


def _dot_t0(a, b):
    return lax.dot_general(a, b, (((0,), (0,)), ((), ())), preferred_element_type=F32)


def _dot_t1(a, b):
    return lax.dot_general(a, b, (((1,), (1,)), ((), ())), preferred_element_type=F32)


def _rms(x):
    return lax.rsqrt(jnp.mean(x * x, axis=-1, keepdims=True) + EPS)


def _shift_down(a, k):
    return pltpu.roll(a, k, 0)


def _shift_up(a, k):
    return pltpu.roll(a, a.shape[0] - k, 0)


def _mesh_position():
    return lax.axis_index("x"), lax.axis_index("y"), lax.axis_index("c")


def _index(pos):
    return 4 * pos[0] + 2 * pos[1] + pos[2]


def _peer(pos, d):
    x, y, c = pos
    return (1 - x if d & 4 else x, 1 - y if d & 2 else y, 1 - c if d & 1 else c)


def _remote(src, dst, send_sems, recv_sems, k, to):
    return pltpu.make_async_remote_copy(src_ref=src, dst_ref=dst, send_sem=send_sems.at[k], recv_sem=recv_sems.at[k],
                                        device_id=to, device_id_type=MESH)


class _Gather:
    def __init__(self, srcs, dsts, send_sems, recv_sems):
        x, y, c = _mesh_position()
        n = len(srcs)
        me, sibling = (x, y, c), (x, y, 1 - c)
        chips = [(1 - x, y), (x, 1 - y), (1 - x, 1 - y)]

        def place(t, block):
            return self.slot(srcs[t], dsts[t], _index(block))

        def copy(k, t, block, to, src=None):
            slot = place(t, block)
            return _remote(slot if src is None else src, slot, send_sems, recv_sems, k * n + t, to)

        tensors = range(n)
        self.local = [pltpu.make_async_copy(srcs[t], place(t, me), recv_sems.at[7 * n + t]) for t in tensors]
        self.first = [copy(1 + j, t, me, chip + (c,), srcs[t]) for t in tensors for j, chip in enumerate(chips)]
        self.first += [copy(0, t, me, sibling, srcs[t]) for t in tensors]
        self.ici_in = [copy(1 + j, t, chip + (c,), me) for t in tensors for j, chip in enumerate(chips)]
        self.passed = [copy(4 + j, t, chip + (c,), sibling) for t in tensors for j, chip in enumerate(chips)]
        self.d2d_in = [copy(0, t, sibling, me) for t in tensors]
        self.d2d_in += [copy(4 + j, t, chip + (1 - c,), me) for t in tensors for j, chip in enumerate(chips)]

    @staticmethod
    def slot(src, dst, k):
        if len(dst.shape) == len(src.shape):
            width = src.shape[-1]
            return dst.at[:, pl.ds(pl.multiple_of(k * width, 128), width)]
        return dst.at[k]

    def start(self):
        for cp in self.local + self.first:
            cp.start()

    def middle(self):
        for landed, onward in zip(self.ici_in, self.passed):
            landed.wait_recv()
            onward.start()

    def finish(self):
        for cp in self.d2d_in:
            cp.wait_recv()
        for cp in self.first + self.passed:
            cp.wait_send()
        for cp in self.local:
            cp.wait()


class _Scatter:
    def __init__(self, srcs, dsts, send_sems, recv_sems):
        pos = _mesh_position()
        n = len(srcs)
        self.local = [pltpu.make_async_copy(srcs[t].at[_index(pos)], dsts[t].at[0], recv_sems.at[t]) for t in range(n)]
        self.remote = []
        for d in range(1, N_DEV):
            to = _peer(pos, d)
            self.remote += [_remote(srcs[t].at[_index(to)], dsts[t].at[d], send_sems, recv_sems, d * n + t, to)
                            for t in range(n)]

    def start(self):
        for cp in self.local + self.remote:
            cp.start()

    def middle(self):
        pass

    def finish(self):
        for cp in self.local:
            cp.wait()
        for cp in self.remote:
            cp.wait_recv()
        for cp in self.remote:
            cp.wait_send()


HBM_SPEC = pl.BlockSpec(memory_space=pltpu.HBM)
MIDDLE_STEPS_BEFORE_END = 4


class _GatherColumns(_Gather):
    pass


def _landing_shape(exchange, src):
    if exchange is _GatherColumns:
        return src.shape[:-1] + (N_DEV * src.shape[-1],)
    return (N_DEV,) + src.shape if exchange is _Gather else src.shape


def _call(body, *, name, grid, in_specs, out_specs, out_shape, scratch_shapes, args, sides=()):
    params = _params(len(grid))
    if not sides:
        return pl.pallas_call(body, name=name, grid=grid, in_specs=in_specs, out_specs=out_specs, out_shape=out_shape,
                              scratch_shapes=scratch_shapes, compiler_params=params)(*args)
    n_in, n_out, n_scratch = len(in_specs), len(out_specs), len(scratch_shapes)
    counts = [len(srcs) for _, srcs in sides]
    ns = sum(counts)
    total = 1
    for g in grid:
        total *= g
    side_args, side_shapes, side_sems = [], [], []
    for exchange, srcs in sides:
        side_args += list(srcs)
        side_shapes += [jax.ShapeDtypeStruct(_landing_shape(exchange, s), s.dtype) for s in srcs]
        side_sems += [pltpu.SemaphoreType.DMA((N_DEV * len(srcs),))] * 2

    def wrapped(*refs):
        ins, side_in = refs[:n_in], refs[n_in:n_in + ns]
        outs = refs[n_in + ns:n_in + ns + n_out]
        side_out = refs[n_in + ns + n_out:n_in + 2 * ns + n_out]
        rest = refs[n_in + 2 * ns + n_out:]
        scratch, sems = rest[:n_scratch], rest[n_scratch:]
        step = pl.program_id(0)
        for axis in range(1, len(grid)):
            step = step * grid[axis] + pl.program_id(axis)

        def exchanges():
            built, at = [], 0
            for k, (exchange, _) in enumerate(sides):
                built.append(exchange(side_in[at:at + counts[k]], side_out[at:at + counts[k]],
                                      sems[2 * k], sems[2 * k + 1]))
                at += counts[k]
            return built

        @pl.when(step == 0)
        def _():
            for ex in exchanges():
                ex.start()

        @pl.when(step == max(total - MIDDLE_STEPS_BEFORE_END, 0))
        def _():
            for ex in exchanges():
                ex.middle()

        body(*ins, *outs, *scratch)

        @pl.when(step == total - 1)
        def _():
            for ex in exchanges():
                ex.finish()

    return pl.pallas_call(
        wrapped, name=name, grid=grid, in_specs=list(in_specs) + [HBM_SPEC] * ns,
        out_specs=list(out_specs) + [HBM_SPEC] * ns, out_shape=list(out_shape) + side_shapes,
        scratch_shapes=list(scratch_shapes) + side_sems, compiler_params=params)(*args, *side_args)


def _prenorm_gather(x, g, wie, small, woe, wio, woo, ts):
    S = x.shape[0]
    nt = S // ts

    def body(x_ref, g_ref, wie_ref, sm_ref, woe_ref, wio_ref, woo_ref,
             h_ref, wie_out, sm_out, woe_out, wio_b, woo_b, cast_ref, small_ref, cast2_ref, send_sems, recv_sems):
        step = pl.program_id(0)
        gather = lambda: _Gather([cast_ref, small_ref, cast2_ref], [wie_out, sm_out, woe_out], send_sems, recv_sems)

        @pl.when(step == 0)
        def _():
            cast_ref[...] = wie_ref[...].astype(BF16)
            small_ref[...] = sm_ref[...]
            cast2_ref[...] = woe_ref[...].astype(BF16)
            gather().start()
            wio_b[...] = wio_ref[...].astype(BF16)
            woo_b[...] = woo_ref[...].astype(BF16)

        xv = x_ref[...]
        h_ref[...] = ((xv * _rms(xv)) * g_ref[...]).astype(BF16)

        @pl.when(step == nt - 1)
        def _():
            exchange = gather()
            exchange.middle()
            exchange.finish()

    tile = pl.BlockSpec((ts, D), lambda i: (i, 0))
    sems = pltpu.SemaphoreType.DMA((N_DEV * 3,))
    shards = [wie, small, woe, wio, woo]
    return pl.pallas_call(
        body, name="prenorm_gather", grid=(nt,),
        in_specs=[tile, _full((1, D))] + [_full(a.shape) for a in shards],
        out_specs=[tile, HBM_SPEC, HBM_SPEC, HBM_SPEC] + [_full(a.shape) for a in shards[3:]],
        out_shape=[jax.ShapeDtypeStruct((S, D), BF16), jax.ShapeDtypeStruct((N_DEV,) + wie.shape, BF16),
                   jax.ShapeDtypeStruct((N_DEV,) + small.shape, F32), jax.ShapeDtypeStruct((N_DEV,) + woe.shape, BF16)]
        + [jax.ShapeDtypeStruct(a.shape, BF16) for a in shards[3:]],
        scratch_shapes=[pltpu.VMEM(wie.shape, BF16), pltpu.VMEM(small.shape, F32), pltpu.VMEM(woe.shape, BF16),
                        sems, sems],
        compiler_params=_params(),
    )(x, g, wie, small, woe, wio, woo)


def _even_mixer(proj, halo, row0, cw_ref, pw_ref):
    xa, gb, gc, za, xp, zp = proj
    nr = xa.shape[0]
    ha = gc * xa
    ha_ext = jnp.concatenate([halo[0], ha], axis=0)
    ha_m1 = _shift_down(ha_ext, 1)[HALO:]
    ha_m2 = _shift_down(ha_ext, 2)[HALO:]
    conv = cw_ref[2:3, :] * ha + cw_ref[1:2, :] * ha_m1 + cw_ref[0:1, :] * ha_m2
    sig_a = _sigmoid(za)
    silu_a = za * sig_a

    xp_ext = jnp.concatenate([halo[1], xp], axis=0)
    pos = row0 + lax.broadcasted_iota(jnp.int32, (nr, 1), 0)
    pooled, inv_cnt, mixed = [], [], []
    for g, w in enumerate(POOL_WINDOWS):
        cols = slice(g * GROUP, (g + 1) * GROUP)
        s = xp_ext[:, cols]
        k = 1
        while k < w:
            s = s + _shift_down(s, k)
            k *= 2
        inv = 1.0 / jnp.minimum(pos + 1, w).astype(F32)
        pg = s[HALO:] * inv - xp[:, cols]
        pooled.append(pg)
        inv_cnt.append(inv)
        mixed.append(_dot(pg.astype(BF16), pw_ref[g]))
    mixed = jnp.concatenate(mixed, axis=1)
    sig_b = _sigmoid(zp)
    silu_b = zp * sig_b
    return dict(xa=xa, gb=gb, gc=gc, za=za, xp=xp, zp=zp, ha=ha, ha_m1=ha_m1, ha_m2=ha_m2, conv=conv,
                sig_a=sig_a, silu_a=silu_a, pooled=pooled, inv_cnt=inv_cnt, mixed=mixed, sig_b=sig_b,
                silu_b=silu_b)


def _columns(a):
    return [a[:, k * A_W:(k + 1) * A_W].astype(F32) for k in range(6)]


def _even_forward_fused(h, x, win_b, cw, pw_b, ps, wout_b, gpost, gpre_next, ts, sides=()):
    S = x.shape[0]

    def body(h_ref, x_ref, wi_ref, cw_ref, pw_ref, ps_ref, wo_ref, gp_ref, gn_ref,
             p_ref, m_ref, x1_ref, h1_ref, mix_ref, carry_ref):
        i = pl.program_id(0)

        @pl.when(i == 0)
        def _():
            carry_ref[...] = jnp.zeros_like(carry_ref)

        for r0 in range(0, ts, SUB_ROWS):
            rows = slice(r0, r0 + SUB_ROWS)
            proj = _dot(h_ref[rows, :], wi_ref[...])
            p_ref[rows, :] = proj.astype(BF16)
            fw = _even_mixer(_columns(proj), (carry_ref[:, 0:A_W], carry_ref[:, A_W:D]), i * ts + r0, cw_ref, pw_ref)
            carry_ref[:, 0:A_W] = fw["ha"][SUB_ROWS - HALO:]
            carry_ref[:, A_W:D] = fw["xp"][SUB_ROWS - HALO:]
            mix_ref[rows, 0:A_W] = (fw["gb"] * fw["conv"] * fw["silu_a"]).astype(BF16)
            mix_ref[rows, A_W:D] = (fw["mixed"] * ps_ref[...] * fw["silu_b"]).astype(BF16)
            m = _dot(mix_ref[rows, :], wo_ref[...])
            m_ref[rows, :] = m.astype(BF16)
            x1 = x_ref[rows, :] + (m * _rms(m)) * gp_ref[...]
            x1_ref[rows, :] = x1
            h1_ref[rows, :] = ((x1 * _rms(x1)) * gn_ref[...]).astype(BF16)

    tile = pl.BlockSpec((ts, D), lambda i: (i, 0))
    return _call(
        body, name="even_forward", grid=(S // ts,),
        in_specs=[tile, tile, _resident((D, PROJ)), _full((3, A_W)), _full((N_GROUPS, GROUP, GROUP)), _full((1, B_W)),
                  _resident((D, D)), _full((1, D)), _full((1, D))],
        out_specs=[pl.BlockSpec((ts, PROJ), lambda i: (i, 0)), tile, tile, tile],
        out_shape=[jax.ShapeDtypeStruct((S, PROJ), BF16), jax.ShapeDtypeStruct((S, D), BF16),
                   jax.ShapeDtypeStruct((S, D), F32), jax.ShapeDtypeStruct((S, D), BF16)],
        scratch_shapes=[pltpu.VMEM((ts, D), BF16), pltpu.VMEM((HALO, D), F32)],
        args=(h, x, win_b, cw, pw_b, ps, wout_b, gpost, gpre_next), sides=sides)


def _even_backward_fused(g1, m0, p, x, cw, pw_b, pwt_b, ps, woutt_b, wint_b, gpost, gpre, ts):
    S = g1.shape[0]
    nt = S // ts

    def body(g_ref, m_ref, p_ref, halo_ref, x_ref, cw_ref, pw_ref, pwt_ref, ps_ref, wot_ref, wit_ref, gp_ref, gn_ref,
             dp_ref, dx_ref, dwob_ref, dcw_ref, dpw_ref, dps_ref, dgp_ref, dgn_ref,
             carry_ref, mix_ref, dm_ref, dwo_ref):
        step = pl.program_id(0)
        i = nt - 1 - step

        @pl.when(step == 0)
        def _():
            carry_ref[...] = jnp.zeros_like(carry_ref)
            dwo_ref[...] = jnp.zeros_like(dwo_ref)
            dcw_ref[...] = jnp.zeros_like(dcw_ref)
            dpw_ref[...] = jnp.zeros_like(dpw_ref)
            dps_ref[...] = jnp.zeros_like(dps_ref)
            dgp_ref[...] = jnp.zeros_like(dgp_ref)
            dgn_ref[...] = jnp.zeros_like(dgn_ref)

        ps_v = ps_ref[...]
        sums = [None] * 6
        add = lambda k, part: sums.__setitem__(k, part if sums[k] is None else sums[k] + part)
        for r0 in range(ts - SUB_ROWS, -1, -SUB_ROWS):
            rows = slice(r0, r0 + SUB_ROWS)
            g = g_ref[rows, :]
            m = m_ref[rows, :].astype(F32)
            q = _rms(m)
            n = m * q
            add(0, jnp.sum(g * n, axis=0, keepdims=True))
            dn = g * gp_ref[...]
            dm = q * (dn - n * jnp.mean(dn * n, axis=-1, keepdims=True))
            dm_ref[rows, :] = dm.astype(BF16)
            dmix = _dot(dm_ref[rows, :], wot_ref[...])
            dya = dmix[:, 0:A_W]
            dyb = dmix[:, A_W:D]

            if r0 == 0:
                before = _columns(halo_ref[...])
                keep = jnp.where(i == 0, 0.0, 1.0).astype(F32)
                halo = (before[2] * before[0] * keep, before[4] * keep)
            else:
                before = _columns(p_ref[r0 - HALO:r0, :])
                halo = (before[2] * before[0], before[4])
            fw = _even_mixer(_columns(p_ref[rows, :]), halo, i * ts + r0, cw_ref, pw_ref)
            mix_ref[rows, 0:A_W] = (fw["gb"] * fw["conv"] * fw["silu_a"]).astype(BF16)
            mix_ref[rows, A_W:D] = (fw["mixed"] * ps_v * fw["silu_b"]).astype(BF16)

            t = dya * fw["gb"]
            dconv = t * fw["silu_a"]
            dgb = dya * fw["conv"] * fw["silu_a"]
            dza = t * fw["conv"] * _dsilu(fw["za"], fw["sig_a"])
            add(2, jnp.sum(dconv * fw["ha"], axis=0, keepdims=True))
            add(3, jnp.sum(dconv * fw["ha_m1"], axis=0, keepdims=True))
            add(4, jnp.sum(dconv * fw["ha_m2"], axis=0, keepdims=True))
            dconv_ext = jnp.concatenate([dconv, carry_ref[:, 0:A_W]], axis=0)
            dha = (cw_ref[2:3, :] * dconv + cw_ref[1:2, :] * _shift_up(dconv_ext, 1)[:SUB_ROWS]
                   + cw_ref[0:1, :] * _shift_up(dconv_ext, 2)[:SUB_ROWS])
            dgc = dha * fw["xa"]
            dxa = dha * fw["gc"]

            u = dyb * fw["mixed"]
            add(1, jnp.sum(u * fw["silu_b"], axis=0, keepdims=True))
            dzp = u * ps_v * _dsilu(fw["zp"], fw["sig_b"])
            dmixed = (dyb * ps_v * fw["silu_b"]).astype(BF16)
            dxp, e_first = [], []
            for gi, w in enumerate(POOL_WINDOWS):
                cols = slice(gi * GROUP, (gi + 1) * GROUP)
                dmg = dmixed[:, cols]
                dpooled = _dot(dmg, pwt_ref[gi])
                dpw_ref[gi] += _dot_t0(fw["pooled"][gi].astype(BF16), dmg)
                e = dpooled * fw["inv_cnt"][gi]
                e_first.append(e[0:HALO])
                s = jnp.concatenate([e, carry_ref[:, A_W + gi * GROUP:A_W + (gi + 1) * GROUP]], axis=0)
                k = 1
                while k < w:
                    s = s + _shift_up(s, k)
                    k *= 2
                dxp.append(s[:SUB_ROWS] - dpooled)
            carry_ref[:, 0:A_W] = dconv[0:HALO]
            carry_ref[:, A_W:D] = jnp.concatenate(e_first, axis=1)

            dp_ref[rows, 0:512] = dxa.astype(BF16)
            dp_ref[rows, 512:1024] = dgb.astype(BF16)
            dp_ref[rows, 1024:1536] = dgc.astype(BF16)
            dp_ref[rows, 1536:2048] = dza.astype(BF16)
            dp_ref[rows, 2048:2560] = jnp.concatenate(dxp, axis=1).astype(BF16)
            dp_ref[rows, 2560:3072] = dzp.astype(BF16)

            dh = _dot(dp_ref[rows, :], wit_ref[...])
            xv = x_ref[rows, :]
            r = _rms(xv)
            xn = xv * r
            add(5, jnp.sum(dh * xn, axis=0, keepdims=True))
            dxn = dh * gn_ref[...]
            dx_ref[rows, :] = g + r * (dxn - xn * jnp.mean(dxn * xn, axis=-1, keepdims=True))

        dwo_ref[...] += _dot_t0(mix_ref[...], dm_ref[...])
        dgp_ref[...] += sums[0]
        dps_ref[...] += sums[1]
        dcw_ref[2:3, :] += sums[2]
        dcw_ref[1:2, :] += sums[3]
        dcw_ref[0:1, :] += sums[4]
        dgn_ref[...] += sums[5]

        @pl.when(step == nt - 1)
        def _():
            dwob_ref[...] = dwo_ref[...].astype(BF16)

    rev = lambda s: (nt - 1 - s, 0)
    tile = pl.BlockSpec((ts, D), rev)
    wide = pl.BlockSpec((ts, PROJ), rev)
    bpt = ts // HALO
    halo_map = lambda s: (jnp.maximum((nt - 1 - s) * bpt - 1, 0), 0)
    groups = _full((N_GROUPS, GROUP, GROUP))
    vec = _full((1, D))
    return pl.pallas_call(
        body, name="even_backward", grid=(nt,),
        in_specs=[tile, tile, wide, pl.BlockSpec((HALO, PROJ), halo_map), tile, _full((3, A_W)), groups, groups,
                  _full((1, B_W)), _resident((D, D)), _resident((PROJ, D)), vec, vec],
        out_specs=[wide, tile, _full((D, D)), _full((8, A_W)), groups, _full((1, B_W)), vec, vec],
        out_shape=[jax.ShapeDtypeStruct((S, PROJ), BF16), jax.ShapeDtypeStruct((S, D), F32),
                   jax.ShapeDtypeStruct((D, D), BF16), jax.ShapeDtypeStruct((8, A_W), F32),
                   jax.ShapeDtypeStruct((N_GROUPS, GROUP, GROUP), F32), jax.ShapeDtypeStruct((1, B_W), F32),
                   jax.ShapeDtypeStruct((1, D), F32), jax.ShapeDtypeStruct((1, D), F32)],
        scratch_shapes=[pltpu.VMEM((HALO, D), F32), pltpu.VMEM((ts, D), BF16), pltpu.VMEM((ts, D), BF16),
                        pltpu.VMEM((D, D), F32)],
        compiler_params=_params(),
    )(g1, m0, p, p, x, cw, pw_b, pwt_b, ps, woutt_b, wint_b, gpost, gpre)


def _gate_matmuls(w_ref, in_ref, out_ref, chunks, bias_ref=None):
    for h in range(N_HEADS):
        cols = slice(h * GROUP, (h + 1) * GROUP)
        wide = jnp.concatenate([in_ref[chunk, cols] for chunk in chunks], axis=1)
        res = _dot(w_ref[h], wide)
        for k, chunk in enumerate(chunks):
            part = res[:, k * GROUP:(k + 1) * GROUP]
            out_ref[chunk, cols] = part if bias_ref is None else part + bias_ref[h]


def _odd_forward_parts(v, lg_ref, lb_ref, wt_ref, bfull_ref, vln_ref, sv_ref, r0, nr):
    rows = slice(r0, r0 + nr)
    mu = jnp.mean(v, axis=-1, keepdims=True)
    vc = v - mu
    rstd = lax.rsqrt(jnp.mean(vc * vc, axis=-1, keepdims=True) + EPS)
    vh = vc * rstd
    vln_ref[rows, :] = (vh * lg_ref[...] + lb_ref[...]).astype(BF16)
    _gate_matmuls(wt_ref, vln_ref, sv_ref, [slice(r0 + k * GROUP, r0 + (k + 1) * GROUP) for k in range(nr // GROUP)],
                  bfull_ref)
    return vh, rstd


def _odd_forward(h, x1, tgt, win_b, lg, lb, wt_b, bfull, wout_b, gpost, ts):
    S = x1.shape[0]

    def body(h_ref, x_ref, t_ref, wi_ref, lg_ref, lb_ref, wt_ref, bfull_ref, wo_ref, gp_ref,
             p_ref, m_ref, g_ref, loss_ref, vln_ref, sv_ref, y_ref):
        @pl.when(pl.program_id(0) == 0)
        def _():
            loss_ref[...] = jnp.zeros_like(loss_ref)

        loss = None
        for r0 in range(0, ts, SUB_ROWS):
            rows = slice(r0, r0 + SUB_ROWS)
            proj = _dot(h_ref[rows, :], wi_ref[...])
            p_ref[rows, :] = proj.astype(BF16)
            u, v, z = proj[:, 0:D], proj[:, D:2 * D], proj[:, 2 * D:3 * D]
            _odd_forward_parts(v, lg_ref, lb_ref, wt_ref, bfull_ref, vln_ref, sv_ref, r0, SUB_ROWS)
            y_ref[rows, :] = (u * sv_ref[rows, :] * (z * _sigmoid(z))).astype(BF16)
            m = _dot(y_ref[rows, :], wo_ref[...])
            m_ref[rows, :] = m.astype(BF16)
            x2 = x_ref[rows, :] + (m * _rms(m)) * gp_ref[...]
            err = x2 - t_ref[rows, :]
            g_ref[rows, :] = err * (1.0 / D)
            part = jnp.sum(err * err, axis=0, keepdims=True)
            loss = part if loss is None else loss + part
        loss_ref[...] += loss

    tile = pl.BlockSpec((ts, D), lambda i: (i, 0))
    small = _full((N_HEADS, GROUP, GROUP))
    return pl.pallas_call(
        body, name="odd_forward", grid=(S // ts,),
        in_specs=[pl.BlockSpec((ts, D), lambda i: (i, 0)), tile, tile, _full((D, PROJ)), _full((1, D)), _full((1, D)),
                  small, small, _full((D, D)), _full((1, D))],
        out_specs=[pl.BlockSpec((ts, PROJ), lambda i: (i, 0)), tile, tile, _full((1, D))],
        out_shape=[jax.ShapeDtypeStruct((S, PROJ), BF16), jax.ShapeDtypeStruct((S, D), BF16),
                   jax.ShapeDtypeStruct((S, D), F32), jax.ShapeDtypeStruct((1, D), F32)],
        scratch_shapes=[pltpu.VMEM((ts, D), BF16), pltpu.VMEM((ts, D), F32), pltpu.VMEM((ts, D), BF16)],
        compiler_params=_params(),
    )(h, x1, tgt, win_b, lg, lb, wt_b, bfull, wout_b, gpost)


def _odd_backward(g2, m1, p, x1, lg, lb, wt_b, wtt_b, bfull, woutt_b, wint_b, gpost, gpre, ts):
    S = g2.shape[0]
    nt = S // ts
    sub = ts // 2

    def body(g_ref, m_ref, p_ref, x_ref, lg_ref, lb_ref, wt_ref, wtt_ref, bfull_ref, wot_ref, wit_ref, gp_ref, gn_ref,
             dp_ref, dx_ref, dwob_ref, dws_ref, dbs_ref, dlg_ref, dlb_ref, dgp_ref, dgn_ref,
             vln_ref, sv_ref, dsvb_ref, dvln_ref, dsum_ref, y_ref, dm_ref, dwo_ref):
        step = pl.program_id(0)

        @pl.when(step == 0)
        def _():
            dwo_ref[...] = jnp.zeros_like(dwo_ref)
            dws_ref[...] = jnp.zeros_like(dws_ref)
            dsum_ref[...] = jnp.zeros_like(dsum_ref)
            dlg_ref[...] = jnp.zeros_like(dlg_ref)
            dlb_ref[...] = jnp.zeros_like(dlb_ref)
            dgp_ref[...] = jnp.zeros_like(dgp_ref)
            dgn_ref[...] = jnp.zeros_like(dgn_ref)

        tril = (lax.broadcasted_iota(jnp.int32, (GROUP, GROUP), 0)
                >= lax.broadcasted_iota(jnp.int32, (GROUP, GROUP), 1))
        sums = [None] * 4
        add = lambda k, part: sums.__setitem__(k, part if sums[k] is None else sums[k] + part)

        def post_norm(c):
            g = g_ref[c["rows"], :]
            m = m_ref[c["rows"], :].astype(F32)
            q = _rms(m)
            n = m * q
            add(0, jnp.sum(g * n, axis=0, keepdims=True))
            dn = g * gp_ref[...]
            dm = q * (dn - n * jnp.mean(dn * n, axis=-1, keepdims=True))
            dm_ref[c["rows"], :] = dm.astype(BF16)

        def out_projection(c):
            c["dy"] = _dot(dm_ref[c["rows"], :], wot_ref[...])

        def layernorm(c):
            v = p_ref[c["rows"], D:2 * D].astype(F32)
            mu = jnp.mean(v, axis=-1, keepdims=True)
            vc = v - mu
            c["rstd"] = lax.rsqrt(jnp.mean(vc * vc, axis=-1, keepdims=True) + EPS)
            c["vh"] = vc * c["rstd"]
            vln_ref[c["rows"], :] = (c["vh"] * lg_ref[...] + lb_ref[...]).astype(BF16)

        def gate_matmuls(c):
            _gate_matmuls(wt_ref, vln_ref, sv_ref, c["chunks"], bfull_ref)

        def gating(c):
            rows = c["rows"]
            u = p_ref[rows, 0:D].astype(F32)
            z = p_ref[rows, 2 * D:3 * D].astype(F32)
            sv = sv_ref[rows, :]
            sig = _sigmoid(z)
            sz = z * sig
            y_ref[rows, :] = (u * sv * sz).astype(BF16)
            dy = c.pop("dy")
            t = dy * sz
            dsv = t * u
            dsvb_ref[rows, :] = dsv.astype(BF16)
            for k in range(sub // GROUP):
                dsum_ref[...] += dsv[k * GROUP:(k + 1) * GROUP]
            dp_ref[rows, 0:D] = (t * sv).astype(BF16)
            dp_ref[rows, 2 * D:3 * D] = (dy * u * sv * _dsilu(z, sig)).astype(BF16)

        def gate_backward_matmuls(c):
            for chunk in c["chunks"]:
                for h in range(N_HEADS):
                    cols = slice(h * GROUP, (h + 1) * GROUP)
                    dvln_ref[chunk, cols] = _dot(wtt_ref[h], dsvb_ref[chunk, cols])
                    dws_ref[h] += jnp.where(tril, _dot_t1(dsvb_ref[chunk, cols], vln_ref[chunk, cols]), 0.0)

        def layernorm_backward(c):
            vh, rstd = c.pop("vh"), c.pop("rstd")
            dvln = dvln_ref[c["rows"], :]
            add(1, jnp.sum(dvln * vh, axis=0, keepdims=True))
            add(2, jnp.sum(dvln, axis=0, keepdims=True))
            dvh = dvln * lg_ref[...]
            dv = rstd * (dvh - jnp.mean(dvh, axis=-1, keepdims=True)
                         - vh * jnp.mean(dvh * vh, axis=-1, keepdims=True))
            dp_ref[c["rows"], D:2 * D] = dv.astype(BF16)

        def in_projection(c):
            c["dh"] = _dot(dp_ref[c["rows"], :], wit_ref[...])

        def pre_norm(c):
            dh = c.pop("dh")
            xv = x_ref[c["rows"], :]
            r = _rms(xv)
            xn = xv * r
            add(3, jnp.sum(dh * xn, axis=0, keepdims=True))
            dxn = dh * gn_ref[...]
            dx_ref[c["rows"], :] = g_ref[c["rows"], :] + r * (dxn - xn * jnp.mean(dxn * xn, axis=-1, keepdims=True))

        phases = [post_norm, out_projection, layernorm, gate_matmuls, gating, gate_backward_matmuls,
                  layernorm_backward, in_projection, pre_norm]
        groups = [dict(rows=slice(r0, r0 + sub),
                       chunks=[slice(r0 + k * GROUP, r0 + (k + 1) * GROUP) for k in range(sub // GROUP)])
                  for r0 in range(0, ts, sub)]
        for group in groups:
            for phase in phases:
                phase(group)

        dwo_ref[...] += _dot_t0(y_ref[...], dm_ref[...])
        dgp_ref[...] += sums[0]
        dlg_ref[...] += sums[1]
        dlb_ref[...] += sums[2]
        dgn_ref[...] += sums[3]

        @pl.when(step == nt - 1)
        def _():
            dwob_ref[...] = dwo_ref[...].astype(BF16)
            ones = jnp.ones((8, GROUP), F32)
            for h in range(N_HEADS):
                cols = slice(h * GROUP, (h + 1) * GROUP)
                total = lax.dot_general(ones, dsum_ref[:, cols], (((1,), (1,)), ((), ())),
                                        precision=lax.Precision.HIGHEST, preferred_element_type=F32)
                dbs_ref[h:h + 1, :] = total[0:1, :]

    tile = pl.BlockSpec((ts, D), lambda i: (i, 0))
    wide = pl.BlockSpec((ts, PROJ), lambda i: (i, 0))
    small = _full((N_HEADS, GROUP, GROUP))
    heads = _resident((N_HEADS, GROUP, GROUP))
    vec = _full((1, D))
    return pl.pallas_call(
        body, name="odd_backward", grid=(nt,),
        in_specs=[tile, tile, wide, tile, vec, vec, heads, heads, heads, _resident((D, D)), _resident((PROJ, D)),
                  vec, vec],
        out_specs=[wide, tile, _full((D, D)), small, _full((N_HEADS, GROUP)), vec, vec, vec, vec],
        out_shape=[jax.ShapeDtypeStruct((S, PROJ), BF16), jax.ShapeDtypeStruct((S, D), F32),
                   jax.ShapeDtypeStruct((D, D), BF16),
                   jax.ShapeDtypeStruct((N_HEADS, GROUP, GROUP), F32), jax.ShapeDtypeStruct((N_HEADS, GROUP), F32),
                   jax.ShapeDtypeStruct((1, D), F32), jax.ShapeDtypeStruct((1, D), F32),
                   jax.ShapeDtypeStruct((1, D), F32), jax.ShapeDtypeStruct((1, D), F32)],
        scratch_shapes=[pltpu.VMEM((ts, D), BF16), pltpu.VMEM((ts, D), F32), pltpu.VMEM((ts, D), BF16),
                        pltpu.VMEM((ts, D), F32), pltpu.VMEM((GROUP, D), F32), pltpu.VMEM((ts, D), BF16),
                        pltpu.VMEM((ts, D), BF16), pltpu.VMEM((D, D), F32)],
        compiler_params=_params(),
    )(g2, m1, p, x1, lg, lb, wt_b, wtt_b, bfull, woutt_b, wint_b, gpost, gpre)


def _weight_grad_scatter(h, dp, ts, name, sides=()):
    S = h.shape[0]
    ts = min(ts, S)
    nt = S // ts
    panel = 2 * SHARD_IN
    last = N_CHIPS - 1

    def body(h_ref, dp_ref, parts_ref, acc_ref, half_ref, swap_ref, sum_ref, send_sems, recv_sems):
        s, t = pl.program_id(0), pl.program_id(1)
        x, y, c = _mesh_position()
        sibling = (x, y, 1 - c)

        def swap(k):
            return _remote(half_ref.at[k, 1], swap_ref.at[k], send_sems, recv_sems, k, sibling)

        def chip_sum(k):
            flip = last - k
            to = (x ^ (flip >> 1), y ^ (flip & 1), c)
            return _remote(sum_ref.at[k], parts_ref.at[1 + flip], send_sems, recv_sems, 3 + flip, to)

        own = pltpu.make_async_copy(half_ref.at[last, 0], parts_ref.at[0], recv_sems.at[7])
        to_sibling = _remote(half_ref.at[last, 1], parts_ref.at[1], send_sems, recv_sems, 3, sibling)

        @pl.when(t == 0)
        def _():
            acc_ref[...] = jnp.zeros_like(acc_ref)

        for k in range(last):
            @pl.when((s == k + 1) & (t == 0))
            def _():
                swap(k).wait_recv()
                sum_ref[k] = (half_ref[k, 0].astype(F32) + swap_ref[k].astype(F32)).astype(BF16)
                chip_sum(k).start()

        acc_ref[...] += _dot_t0(h_ref[...], dp_ref[...])

        @pl.when(t == nt - 1)
        def _():
            for core in range(2):
                @pl.when(c == core)
                def _():
                    half_ref[s, core] = acc_ref[:, 0:SHARD_IN].astype(BF16)
                    half_ref[s, 1 - core] = acc_ref[:, SHARD_IN:panel].astype(BF16)
            for k in range(last):
                @pl.when(s == k)
                def _():
                    swap(k).start()

        @pl.when((s == last) & (t == nt - 1))
        def _():
            own.start()
            to_sibling.start()
            own.wait()
            _remote(half_ref.at[last, 1], parts_ref.at[1], send_sems, recv_sems, 3, sibling).wait_recv()
            for k in range(last):
                chip_sum(k).wait_recv()
            to_sibling.wait_send()
            for k in range(last):
                swap(k).wait_send()
                chip_sum(k).wait_send()

    chip_panel = lambda s, t: (t, (2 * lax.axis_index("x") + lax.axis_index("y")) ^ (last - s))
    sems = pltpu.SemaphoreType.DMA((8,))
    return _call(
        body, name=name, grid=(N_CHIPS, nt),
        in_specs=[pl.BlockSpec((ts, D), lambda s, t: (t, 0)), pl.BlockSpec((ts, panel), chip_panel)],
        out_specs=[HBM_SPEC], out_shape=[jax.ShapeDtypeStruct((N_CHIPS + 1, D, SHARD_IN), BF16)],
        scratch_shapes=[pltpu.VMEM((D, panel), F32), pltpu.VMEM((N_CHIPS, 2, D, SHARD_IN), BF16),
                        pltpu.VMEM((last, D, SHARD_IN), BF16), pltpu.VMEM((last, D, SHARD_IN), BF16), sems, sems],
        args=(h, dp), sides=sides)


def _adamw(w, g, m, v):
    m = ADAM_B1 * m + (1.0 - ADAM_B1) * g
    v = ADAM_B2 * v + (1.0 - ADAM_B2) * (g * g)
    m_hat = m / (1.0 - ADAM_B1 ** ADAM_STEP)
    v_hat = v / (1.0 - ADAM_B2 ** ADAM_STEP)
    delta = -ADAM_LR * (m_hat / (jnp.sqrt(v_hat) + ADAM_EPS) + ADAM_WD * w)
    return delta, m, v


def _sum_update(parts, w, m, v, name):
    R, C = w.shape
    rb = min(R, 256)
    n_parts = parts.shape[0]

    def body(p_ref, w_ref, m_ref, v_ref, g_out, d_out, m_out, v_out):
        acc = p_ref[0].astype(F32)
        for d in range(1, n_parts):
            acc = acc + p_ref[d].astype(F32)
        g_out[...] = acc
        delta, m_new, v_new = _adamw(w_ref[...], acc, m_ref[...], v_ref[...])
        d_out[...] = delta
        m_out[...] = m_new
        v_out[...] = v_new

    blk = pl.BlockSpec((rb, C), lambda i: (i, 0))
    return pl.pallas_call(
        body, name=name, grid=(R // rb,),
        in_specs=[pl.BlockSpec((n_parts, rb, C), lambda i: (0, i, 0)), blk, blk, blk], out_specs=[blk] * 4,
        out_shape=[jax.ShapeDtypeStruct((R, C), F32)] * 4, compiler_params=_params(),
    )(parts, w, m, v)


def _small_sum_update(partials, row_of, weights, loss_parts):
    n_p, n_w = len(partials), len(weights)

    def body(*refs):
        p_refs = refs[:n_p]
        loss_ref = refs[n_p]
        w_refs = refs[n_p + 1:n_p + 1 + 3 * n_w]
        loss_out = refs[n_p + 1 + 3 * n_w]
        outs = refs[n_p + 2 + 3 * n_w:]

        def total(ref):
            acc = ref[0].astype(F32)
            for d in range(1, N_DEV):
                acc = acc + ref[d].astype(F32)
            return acc

        lsum = jnp.sum(total(loss_ref), axis=-1, keepdims=True) * (0.5 / D)
        loss_out[...] = jnp.broadcast_to(lsum, loss_out.shape)
        for k in range(n_p):
            i, r0 = row_of[k]
            g = total(p_refs[k])
            w_ref, m_ref, v_ref = w_refs[3 * i:3 * i + 3]
            g_out, d_out, m_out, v_out = outs[4 * i:4 * i + 4]
            if g.ndim == 2:
                at = (slice(r0, r0 + g.shape[0]), slice(None))
            else:
                at = (slice(None),) * g.ndim
            delta, m_new, v_new = _adamw(w_ref[at], g, m_ref[at], v_ref[at])
            g_out[at] = g
            d_out[at] = delta
            m_out[at] = m_new
            v_out[at] = v_new

    vm = pl.BlockSpec(memory_space=pltpu.VMEM)
    flat_w = [a for wmv in weights for a in wmv]
    out_shape = [jax.ShapeDtypeStruct((1, 128), F32)]
    for w, _, _ in weights:
        out_shape += [jax.ShapeDtypeStruct(w.shape, F32)] * 4
    return pl.pallas_call(
        body, name="small_update", in_specs=[vm] * (n_p + 1 + 3 * n_w), out_specs=[vm] * len(out_shape),
        out_shape=out_shape, compiler_params=pltpu.CompilerParams(vmem_limit_bytes=VMEM_LIMIT),
    )(*partials, loss_parts, *flat_w)


def _pack_small_shard(cw, lg, lb):
    return (jnp.pad(cw, ((0, 13), (0, 64))) + jnp.pad(lg, ((8, 7), (0, 0))) + jnp.pad(lb, ((9, 6), (0, 0))))


def kernel(x, pre_norm, post_norm, even_w_in, even_conv_w, even_pool_w, even_pool_scale, even_w_out, odd_w_in, odd_ln_g, odd_ln_b, odd_w_s, odd_b_s, odd_w_out, loss_target, m_pre_norm, m_post_norm, m_even_w_in, m_even_conv_w, m_even_pool_w, m_even_pool_scale, m_even_w_out, m_odd_w_in, m_odd_ln_g, m_odd_ln_b, m_odd_w_s, m_odd_b_s, m_odd_w_out, v_pre_norm, v_post_norm, v_even_w_in, v_even_conv_w, v_even_pool_w, v_even_pool_scale, v_even_w_out, v_odd_w_in, v_odd_ln_g, v_odd_ln_b, v_odd_w_s, v_odd_b_s, v_odd_w_out):
    S = x.shape[1]
    xs = x.reshape(S, D)
    tgt = loss_target.reshape(S, D)

    gpre0, gpre1 = pre_norm[0:1], pre_norm[1:2]
    gpost0, gpost1 = post_norm[0:1], post_norm[1:2]
    small_shard = _pack_small_shard(even_conv_w[0], odd_ln_g, odd_ln_b)
    h0, wie_g, small_g, woe_g, wio_b, woo_b = _prenorm_gather(xs, gpre0, even_w_in[0], small_shard, even_w_out[0],
                                                              odd_w_in[0], odd_w_out[0], TS_PRENORM)
    wie = jnp.transpose(wie_g, (1, 0, 2)).reshape(D, PROJ)
    wie_t = jnp.transpose(wie_g, (0, 2, 1)).reshape(PROJ, D)
    woe = woe_g.reshape(D, D)
    woe_t = woe.T
    conv_w = jnp.transpose(small_g[:, 0:3, 0:64], (1, 0, 2)).reshape(3, A_W)
    ln_g = small_g[:, 8, :].reshape(1, D)
    ln_b = small_g[:, 9, :].reshape(1, D)

    pool_w_b = even_pool_w[0].astype(BF16)
    pool_wt_b = jnp.swapaxes(even_pool_w[0], 1, 2).astype(BF16)
    ws_tril = jnp.tril(odd_w_s[0])
    ws_b = ws_tril.astype(BF16)
    wst_b = jnp.swapaxes(ws_tril, 1, 2).astype(BF16)
    b_full = jnp.broadcast_to(odd_b_s[0][:, :, None], (N_HEADS, GROUP, GROUP))

    p0, m0, x1, h1, wio, woo_g = _even_forward_fused(
        h0, xs, wie, conv_w, pool_w_b, even_pool_scale, woe, gpost0, gpre1, TS_FWD,
        sides=[(_GatherColumns, [wio_b]), (_Gather, [woo_b])])
    wio_t = wio.T
    woo = woo_g.reshape(D, D)
    woo_t = woo.T
    p1, m1, g2, loss_vec = _odd_forward(h1, x1, tgt, wio, ln_g, ln_b, ws_b, b_full, woo, gpost1, TS_FWD)

    dp1, g1, dwoo, dws, dbs, dlg, dlb, dgpost1, dgpre1 = _odd_backward(
        g2, m1, p1, x1, ln_g, ln_b, ws_b, wst_b, b_full, woo_t, wio_t, gpost1, gpre1, TS_BWD)
    by_owner = lambda dwo: dwo.reshape(N_DEV, SHARD_OUT, D)
    (dwio_parts, dwoo_parts, dlg_parts, dlb_parts,
     dws_g, dbs_g, gpost1_g, gpre1_g, loss_g) = _weight_grad_scatter(
        h1, dp1, TS_WGRAD, "odd_weight_grad",
        sides=[(_Scatter, [by_owner(dwoo), dlg.reshape(N_DEV, 1, 128), dlb.reshape(N_DEV, 1, 128)]),
               (_Gather, [dws.astype(BF16), dbs, dgpost1, dgpre1, loss_vec])])
    dp0, gx, dwoe, dcw, dpw, dps, dgpost0, dgpre0 = _even_backward_fused(
        g1, m0, p0, xs, conv_w, pool_w_b, pool_wt_b, even_pool_scale, woe_t, wie_t, gpost0, gpre0, TS_FWD)
    dcw_by_owner = jnp.transpose(dcw[0:3].reshape(3, N_DEV, 64), (1, 0, 2))
    dwie_parts, dwoe_parts, dcw_parts, dpw_g, dps_g, gpost0_g, gpre0_g = _weight_grad_scatter(
        h0, dp0, TS_WGRAD, "even_weight_grad",
        sides=[(_Scatter, [by_owner(dwoe), dcw_by_owner]), (_Gather, [dpw.astype(BF16), dps, dgpost0, dgpre0])])

    g_wie, d_wie, nm_wie, nv_wie = _sum_update(dwie_parts, even_w_in[0], m_even_w_in[0], v_even_w_in[0],
                                               "update_even_w_in")
    g_wio, d_wio, nm_wio, nv_wio = _sum_update(dwio_parts, odd_w_in[0], m_odd_w_in[0], v_odd_w_in[0],
                                               "update_odd_w_in")
    g_woe, d_woe, nm_woe, nv_woe = _sum_update(dwoe_parts, even_w_out[0], m_even_w_out[0], v_even_w_out[0],
                                               "update_even_w_out")
    g_woo, d_woo, nm_woo, nv_woo = _sum_update(dwoo_parts, odd_w_out[0], m_odd_w_out[0], v_odd_w_out[0],
                                               "update_odd_w_out")

    partials = [gpre0_g, gpre1_g, gpost0_g, gpost1_g, dpw_g, dps_g, dws_g, dbs_g, dcw_parts, dlg_parts, dlb_parts]
    row_of = [(0, 0), (0, 1), (1, 0), (1, 1), (2, 0), (3, 0), (4, 0), (5, 0), (6, 0), (7, 0), (8, 0)]
    weights = [(pre_norm, m_pre_norm, v_pre_norm), (post_norm, m_post_norm, v_post_norm),
               (even_pool_w[0], m_even_pool_w[0], v_even_pool_w[0]),
               (even_pool_scale, m_even_pool_scale, v_even_pool_scale),
               (odd_w_s[0], m_odd_w_s[0], v_odd_w_s[0]), (odd_b_s[0], m_odd_b_s[0], v_odd_b_s[0]),
               (even_conv_w[0], m_even_conv_w[0], v_even_conv_w[0]),
               (odd_ln_g, m_odd_ln_g, v_odd_ln_g), (odd_ln_b, m_odd_ln_b, v_odd_ln_b)]
    small = _small_sum_update(partials, row_of, weights, loss_g)
    loss = small[0][0, 0]

    def leaves(k, big):
        pre, post, pw, psc, ws, bs, cw, lg, lb = [small[1 + 4 * i + k] for i in range(len(weights))]
        wie_k, woe_k, wio_k, woo_k = big
        return [pre, post, wie_k[None], cw[None], pw[None], psc, woe_k[None], wio_k[None], lg, lb, ws[None], bs[None],
                woo_k[None]]

    outs = [loss, gx.reshape(1, S, D)]
    outs += leaves(0, (g_wie, g_woe, g_wio, g_woo))
    outs += leaves(1, (d_wie, d_woe, d_wio, d_woo))
    outs += leaves(2, (nm_wie, nm_woe, nm_wio, nm_woo))
    outs += leaves(3, (nv_wie, nv_woe, nv_wio, nv_woo))
    return tuple(outs)
```

```python
import jax
import jax.numpy as jnp
from jax import lax
from jax.experimental import pallas as pl
from jax.experimental.pallas import tpu as pltpu

F32 = jnp.float32
BF16 = jnp.bfloat16
MESH = pl.DeviceIdType.MESH

D = 1024
EPS = 1e-6
A_W = 512
B_W = 512
POOL_WINDOWS = (2, 4, 8, 16)
GROUP = 128
N_GROUPS = 4
N_HEADS = 8
HALO = 16
PROJ = 3072
N_DEV = 8
N_CHIPS = 4
SHARD_IN = PROJ // N_DEV
SHARD_OUT = D // N_DEV

ADAM_LR = 0.001
ADAM_B1 = 0.9
ADAM_B2 = 0.999
ADAM_EPS = 1e-08
ADAM_WD = 0.01
ADAM_STEP = 10

VMEM_LIMIT = 54 * 1024 * 1024

TS_PRENORM = 1024
TS_FWD = 512
TS_BWD = 256
TS_WGRAD = 2048
SUB_ROWS = 256


def _params(n_grid=1, vmem=VMEM_LIMIT):
    return pltpu.CompilerParams(dimension_semantics=("arbitrary",) * n_grid, vmem_limit_bytes=vmem)


def _full(shape):
    return pl.BlockSpec(shape, lambda *_: (0,) * len(shape))


def _resident(shape):
    return pl.BlockSpec(shape, lambda *_: (0,) * len(shape), pipeline_mode=pl.Buffered(1))


def _sigmoid(z):
    return jax.nn.sigmoid(z)


def _dsilu(z, s):
    return s * (1.0 + z * (1.0 - s))


def _dot(a, b):
    return jnp.dot(a, b, preferred_element_type=F32)


def _dot_t0(a, b):
    return lax.dot_general(a, b, (((0,), (0,)), ((), ())), preferred_element_type=F32)


def _dot_t1(a, b):
    return lax.dot_general(a, b, (((1,), (1,)), ((), ())), preferred_element_type=F32)


def _rms(x):
    return lax.rsqrt(jnp.mean(x * x, axis=-1, keepdims=True) + EPS)


def _shift_down(a, k):
    return pltpu.roll(a, k, 0)


def _shift_up(a, k):
    return pltpu.roll(a, a.shape[0] - k, 0)


def _mesh_position():
    return lax.axis_index("x"), lax.axis_index("y"), lax.axis_index("c")


def _index(pos):
    return 4 * pos[0] + 2 * pos[1] + pos[2]


def _peer(pos, d):
    x, y, c = pos
    return (1 - x if d & 4 else x, 1 - y if d & 2 else y, 1 - c if d & 1 else c)


def _remote(src, dst, send_sems, recv_sems, k, to):
    return pltpu.make_async_remote_copy(src_ref=src, dst_ref=dst, send_sem=send_sems.at[k], recv_sem=recv_sems.at[k],
                                        device_id=to, device_id_type=MESH)


class _Gather:
    def __init__(self, srcs, dsts, send_sems, recv_sems):
        x, y, c = _mesh_position()
        n = len(srcs)
        me, sibling = (x, y, c), (x, y, 1 - c)
        chips = [(1 - x, y), (x, 1 - y), (1 - x, 1 - y)]

        def copy(k, t, block, to, src=None):
            slot = dsts[t].at[_index(block)]
            return _remote(slot if src is None else src, slot, send_sems, recv_sems, k * n + t, to)

        tensors = range(n)
        self.local = [pltpu.make_async_copy(srcs[t], dsts[t].at[_index(me)], recv_sems.at[7 * n + t]) for t in tensors]
        self.first = [copy(1 + j, t, me, chip + (c,), srcs[t]) for t in tensors for j, chip in enumerate(chips)]
        self.first += [copy(0, t, me, sibling, srcs[t]) for t in tensors]
        self.ici_in = [copy(1 + j, t, chip + (c,), me) for t in tensors for j, chip in enumerate(chips)]
        self.passed = [copy(4 + j, t, chip + (c,), sibling) for t in tensors for j, chip in enumerate(chips)]
        self.d2d_in = [copy(0, t, sibling, me) for t in tensors]
        self.d2d_in += [copy(4 + j, t, chip + (1 - c,), me) for t in tensors for j, chip in enumerate(chips)]

    def start(self):
        for cp in self.local + self.first:
            cp.start()

    def middle(self):
        for landed, onward in zip(self.ici_in, self.passed):
            landed.wait_recv()
            onward.start()

    def finish(self):
        for cp in self.d2d_in:
            cp.wait_recv()
        for cp in self.first + self.passed:
            cp.wait_send()
        for cp in self.local:
            cp.wait()


class _Scatter:
    def __init__(self, srcs, dsts, send_sems, recv_sems):
        pos = _mesh_position()
        n = len(srcs)
        self.local = [pltpu.make_async_copy(srcs[t].at[_index(pos)], dsts[t].at[0], recv_sems.at[t]) for t in range(n)]
        self.remote = []
        for d in range(1, N_DEV):
            to = _peer(pos, d)
            self.remote += [_remote(srcs[t].at[_index(to)], dsts[t].at[d], send_sems, recv_sems, d * n + t, to)
                            for t in range(n)]

    def start(self):
        for cp in self.local + self.remote:
            cp.start()

    def middle(self):
        pass

    def finish(self):
        for cp in self.local:
            cp.wait()
        for cp in self.remote:
            cp.wait_recv()
        for cp in self.remote:
            cp.wait_send()


HBM_SPEC = pl.BlockSpec(memory_space=pltpu.HBM)
MIDDLE_STEPS_BEFORE_END = 4


def _landing_shape(exchange, src):
    return (N_DEV,) + src.shape if exchange is _Gather else src.shape


def _call(body, *, name, grid, in_specs, out_specs, out_shape, scratch_shapes, args, sides=()):
    params = _params(len(grid))
    if not sides:
        return pl.pallas_call(body, name=name, grid=grid, in_specs=in_specs, out_specs=out_specs, out_shape=out_shape,
                              scratch_shapes=scratch_shapes, compiler_params=params)(*args)
    n_in, n_out, n_scratch = len(in_specs), len(out_specs), len(scratch_shapes)
    counts = [len(srcs) for _, srcs in sides]
    ns = sum(counts)
    total = 1
    for g in grid:
        total *= g
    side_args, side_shapes, side_sems = [], [], []
    for exchange, srcs in sides:
        side_args += list(srcs)
        side_shapes += [jax.ShapeDtypeStruct(_landing_shape(exchange, s), s.dtype) for s in srcs]
        side_sems += [pltpu.SemaphoreType.DMA((N_DEV * len(srcs),))] * 2

    def wrapped(*refs):
        ins, side_in = refs[:n_in], refs[n_in:n_in + ns]
        outs = refs[n_in + ns:n_in + ns + n_out]
        side_out = refs[n_in + ns + n_out:n_in + 2 * ns + n_out]
        rest = refs[n_in + 2 * ns + n_out:]
        scratch, sems = rest[:n_scratch], rest[n_scratch:]
        step = pl.program_id(0)
        for axis in range(1, len(grid)):
            step = step * grid[axis] + pl.program_id(axis)

        def exchanges():
            built, at = [], 0
            for k, (exchange, _) in enumerate(sides):
                built.append(exchange(side_in[at:at + counts[k]], side_out[at:at + counts[k]],
                                      sems[2 * k], sems[2 * k + 1]))
                at += counts[k]
            return built

        @pl.when(step == 0)
        def _():
            for ex in exchanges():
                ex.start()

        @pl.when(step == max(total - MIDDLE_STEPS_BEFORE_END, 0))
        def _():
            for ex in exchanges():
                ex.middle()

        body(*ins, *outs, *scratch)

        @pl.when(step == total - 1)
        def _():
            for ex in exchanges():
                ex.finish()

    return pl.pallas_call(
        wrapped, name=name, grid=grid, in_specs=list(in_specs) + [HBM_SPEC] * ns,
        out_specs=list(out_specs) + [HBM_SPEC] * ns, out_shape=list(out_shape) + side_shapes,
        scratch_shapes=list(scratch_shapes) + side_sems, compiler_params=params)(*args, *side_args)


def _prenorm_gather(x, g, wie, small, woe, wio, woo, ts):
    S = x.shape[0]
    nt = S // ts

    def body(x_ref, g_ref, wie_ref, sm_ref, woe_ref, wio_ref, woo_ref,
             h_ref, wie_out, sm_out, woe_out, wio_b, woo_b, cast_ref, small_ref, cast2_ref, send_sems, recv_sems):
        step = pl.program_id(0)
        gather = lambda: _Gather([cast_ref, small_ref, cast2_ref], [wie_out, sm_out, woe_out], send_sems, recv_sems)

        @pl.when(step == 0)
        def _():
            cast_ref[...] = wie_ref[...].astype(BF16)
            small_ref[...] = sm_ref[...]
            cast2_ref[...] = woe_ref[...].astype(BF16)
            gather().start()
            wio_b[...] = wio_ref[...].astype(BF16)
            woo_b[...] = woo_ref[...].astype(BF16)

        xv = x_ref[...]
        h_ref[...] = ((xv * _rms(xv)) * g_ref[...]).astype(BF16)

        @pl.when(step == nt - 1)
        def _():
            exchange = gather()
            exchange.middle()
            exchange.finish()

    tile = pl.BlockSpec((ts, D), lambda i: (i, 0))
    sems = pltpu.SemaphoreType.DMA((N_DEV * 3,))
    shards = [wie, small, woe, wio, woo]
    return pl.pallas_call(
        body, name="prenorm_gather", grid=(nt,),
        in_specs=[tile, _full((1, D))] + [_full(a.shape) for a in shards],
        out_specs=[tile, HBM_SPEC, HBM_SPEC, HBM_SPEC] + [_full(a.shape) for a in shards[3:]],
        out_shape=[jax.ShapeDtypeStruct((S, D), BF16), jax.ShapeDtypeStruct((N_DEV,) + wie.shape, BF16),
                   jax.ShapeDtypeStruct((N_DEV,) + small.shape, F32), jax.ShapeDtypeStruct((N_DEV,) + woe.shape, BF16)]
        + [jax.ShapeDtypeStruct(a.shape, BF16) for a in shards[3:]],
        scratch_shapes=[pltpu.VMEM(wie.shape, BF16), pltpu.VMEM(small.shape, F32), pltpu.VMEM(woe.shape, BF16),
                        sems, sems],
        compiler_params=_params(),
    )(x, g, wie, small, woe, wio, woo)


def _even_mixer(proj, halo, row0, cw_ref, pw_ref):
    xa, gb, gc, za, xp, zp = proj
    nr = xa.shape[0]
    ha = gc * xa
    ha_ext = jnp.concatenate([halo[0], ha], axis=0)
    ha_m1 = _shift_down(ha_ext, 1)[HALO:]
    ha_m2 = _shift_down(ha_ext, 2)[HALO:]
    conv = cw_ref[2:3, :] * ha + cw_ref[1:2, :] * ha_m1 + cw_ref[0:1, :] * ha_m2
    sig_a = _sigmoid(za)
    silu_a = za * sig_a

    xp_ext = jnp.concatenate([halo[1], xp], axis=0)
    pos = row0 + lax.broadcasted_iota(jnp.int32, (nr, 1), 0)
    pooled, inv_cnt, mixed = [], [], []
    for g, w in enumerate(POOL_WINDOWS):
        cols = slice(g * GROUP, (g + 1) * GROUP)
        s = xp_ext[:, cols]
        k = 1
        while k < w:
            s = s + _shift_down(s, k)
            k *= 2
        inv = 1.0 / jnp.minimum(pos + 1, w).astype(F32)
        pg = s[HALO:] * inv - xp[:, cols]
        pooled.append(pg)
        inv_cnt.append(inv)
        mixed.append(_dot(pg.astype(BF16), pw_ref[g]))
    mixed = jnp.concatenate(mixed, axis=1)
    sig_b = _sigmoid(zp)
    silu_b = zp * sig_b
    return dict(xa=xa, gb=gb, gc=gc, za=za, xp=xp, zp=zp, ha=ha, ha_m1=ha_m1, ha_m2=ha_m2, conv=conv,
                sig_a=sig_a, silu_a=silu_a, pooled=pooled, inv_cnt=inv_cnt, mixed=mixed, sig_b=sig_b,
                silu_b=silu_b)


def _columns(a):
    return [a[:, k * A_W:(k + 1) * A_W].astype(F32) for k in range(6)]


def _even_forward_fused(h, x, win_b, cw, pw_b, ps, wout_b, gpost, gpre_next, ts, sides=()):
    S = x.shape[0]

    def body(h_ref, x_ref, wi_ref, cw_ref, pw_ref, ps_ref, wo_ref, gp_ref, gn_ref,
             p_ref, m_ref, x1_ref, h1_ref, mix_ref, carry_ref):
        i = pl.program_id(0)

        @pl.when(i == 0)
        def _():
            carry_ref[...] = jnp.zeros_like(carry_ref)

        for r0 in range(0, ts, SUB_ROWS):
            rows = slice(r0, r0 + SUB_ROWS)
            proj = _dot(h_ref[rows, :], wi_ref[...])
            p_ref[rows, :] = proj.astype(BF16)
            fw = _even_mixer(_columns(proj), (carry_ref[:, 0:A_W], carry_ref[:, A_W:D]), i * ts + r0, cw_ref, pw_ref)
            carry_ref[:, 0:A_W] = fw["ha"][SUB_ROWS - HALO:]
            carry_ref[:, A_W:D] = fw["xp"][SUB_ROWS - HALO:]
            mix_ref[rows, 0:A_W] = (fw["gb"] * fw["conv"] * fw["silu_a"]).astype(BF16)
            mix_ref[rows, A_W:D] = (fw["mixed"] * ps_ref[...] * fw["silu_b"]).astype(BF16)
            m = _dot(mix_ref[rows, :], wo_ref[...])
            m_ref[rows, :] = m.astype(BF16)
            x1 = x_ref[rows, :] + (m * _rms(m)) * gp_ref[...]
            x1_ref[rows, :] = x1
            h1_ref[rows, :] = ((x1 * _rms(x1)) * gn_ref[...]).astype(BF16)

    tile = pl.BlockSpec((ts, D), lambda i: (i, 0))
    return _call(
        body, name="even_forward", grid=(S // ts,),
        in_specs=[tile, tile, _resident((D, PROJ)), _full((3, A_W)), _full((N_GROUPS, GROUP, GROUP)), _full((1, B_W)),
                  _resident((D, D)), _full((1, D)), _full((1, D))],
        out_specs=[pl.BlockSpec((ts, PROJ), lambda i: (i, 0)), tile, tile, tile],
        out_shape=[jax.ShapeDtypeStruct((S, PROJ), BF16), jax.ShapeDtypeStruct((S, D), BF16),
                   jax.ShapeDtypeStruct((S, D), F32), jax.ShapeDtypeStruct((S, D), BF16)],
        scratch_shapes=[pltpu.VMEM((ts, D), BF16), pltpu.VMEM((HALO, D), F32)],
        args=(h, x, win_b, cw, pw_b, ps, wout_b, gpost, gpre_next), sides=sides)


def _even_backward_fused(g1, m0, p, x, cw, pw_b, pwt_b, ps, woutt_b, wint_b, gpost, gpre, ts):
    S = g1.shape[0]
    nt = S // ts

    def body(g_ref, m_ref, p_ref, halo_ref, x_ref, cw_ref, pw_ref, pwt_ref, ps_ref, wot_ref, wit_ref, gp_ref, gn_ref,
             dp_ref, dx_ref, dwob_ref, dcw_ref, dpw_ref, dps_ref, dgp_ref, dgn_ref,
             carry_ref, mix_ref, dm_ref, dwo_ref):
        step = pl.program_id(0)
        i = nt - 1 - step

        @pl.when(step == 0)
        def _():
            carry_ref[...] = jnp.zeros_like(carry_ref)
            dwo_ref[...] = jnp.zeros_like(dwo_ref)
            dcw_ref[...] = jnp.zeros_like(dcw_ref)
            dpw_ref[...] = jnp.zeros_like(dpw_ref)
            dps_ref[...] = jnp.zeros_like(dps_ref)
            dgp_ref[...] = jnp.zeros_like(dgp_ref)
            dgn_ref[...] = jnp.zeros_like(dgn_ref)

        ps_v = ps_ref[...]
        sums = [None] * 6
        add = lambda k, part: sums.__setitem__(k, part if sums[k] is None else sums[k] + part)
        for r0 in range(ts - SUB_ROWS, -1, -SUB_ROWS):
            rows = slice(r0, r0 + SUB_ROWS)
            g = g_ref[rows, :]
            m = m_ref[rows, :].astype(F32)
            q = _rms(m)
            n = m * q
            add(0, jnp.sum(g * n, axis=0, keepdims=True))
            dn = g * gp_ref[...]
            dm = q * (dn - n * jnp.mean(dn * n, axis=-1, keepdims=True))
            dm_ref[rows, :] = dm.astype(BF16)
            dmix = _dot(dm_ref[rows, :], wot_ref[...])
            dya = dmix[:, 0:A_W]
            dyb = dmix[:, A_W:D]

            if r0 == 0:
                before = _columns(halo_ref[...])
                keep = jnp.where(i == 0, 0.0, 1.0).astype(F32)
                halo = (before[2] * before[0] * keep, before[4] * keep)
            else:
                before = _columns(p_ref[r0 - HALO:r0, :])
                halo = (before[2] * before[0], before[4])
            fw = _even_mixer(_columns(p_ref[rows, :]), halo, i * ts + r0, cw_ref, pw_ref)
            mix_ref[rows, 0:A_W] = (fw["gb"] * fw["conv"] * fw["silu_a"]).astype(BF16)
            mix_ref[rows, A_W:D] = (fw["mixed"] * ps_v * fw["silu_b"]).astype(BF16)

            t = dya * fw["gb"]
            dconv = t * fw["silu_a"]
            dgb = dya * fw["conv"] * fw["silu_a"]
            dza = t * fw["conv"] * _dsilu(fw["za"], fw["sig_a"])
            add(2, jnp.sum(dconv * fw["ha"], axis=0, keepdims=True))
            add(3, jnp.sum(dconv * fw["ha_m1"], axis=0, keepdims=True))
            add(4, jnp.sum(dconv * fw["ha_m2"], axis=0, keepdims=True))
            dconv_ext = jnp.concatenate([dconv, carry_ref[:, 0:A_W]], axis=0)
            dha = (cw_ref[2:3, :] * dconv + cw_ref[1:2, :] * _shift_up(dconv_ext, 1)[:SUB_ROWS]
                   + cw_ref[0:1, :] * _shift_up(dconv_ext, 2)[:SUB_ROWS])
            dgc = dha * fw["xa"]
            dxa = dha * fw["gc"]

            u = dyb * fw["mixed"]
            add(1, jnp.sum(u * fw["silu_b"], axis=0, keepdims=True))
            dzp = u * ps_v * _dsilu(fw["zp"], fw["sig_b"])
            dmixed = (dyb * ps_v * fw["silu_b"]).astype(BF16)
            dxp, e_first = [], []
            for gi, w in enumerate(POOL_WINDOWS):
                cols = slice(gi * GROUP, (gi + 1) * GROUP)
                dmg = dmixed[:, cols]
                dpooled = _dot(dmg, pwt_ref[gi])
                dpw_ref[gi] += _dot_t0(fw["pooled"][gi].astype(BF16), dmg)
                e = dpooled * fw["inv_cnt"][gi]
                e_first.append(e[0:HALO])
                s = jnp.concatenate([e, carry_ref[:, A_W + gi * GROUP:A_W + (gi + 1) * GROUP]], axis=0)
                k = 1
                while k < w:
                    s = s + _shift_up(s, k)
                    k *= 2
                dxp.append(s[:SUB_ROWS] - dpooled)
            carry_ref[:, 0:A_W] = dconv[0:HALO]
            carry_ref[:, A_W:D] = jnp.concatenate(e_first, axis=1)

            dp_ref[rows, 0:512] = dxa.astype(BF16)
            dp_ref[rows, 512:1024] = dgb.astype(BF16)
            dp_ref[rows, 1024:1536] = dgc.astype(BF16)
            dp_ref[rows, 1536:2048] = dza.astype(BF16)
            dp_ref[rows, 2048:2560] = jnp.concatenate(dxp, axis=1).astype(BF16)
            dp_ref[rows, 2560:3072] = dzp.astype(BF16)

            dh = _dot(dp_ref[rows, :], wit_ref[...])
            xv = x_ref[rows, :]
            r = _rms(xv)
            xn = xv * r
            add(5, jnp.sum(dh * xn, axis=0, keepdims=True))
            dxn = dh * gn_ref[...]
            dx_ref[rows, :] = g + r * (dxn - xn * jnp.mean(dxn * xn, axis=-1, keepdims=True))

        dwo_ref[...] += _dot_t0(mix_ref[...], dm_ref[...])
        dgp_ref[...] += sums[0]
        dps_ref[...] += sums[1]
        dcw_ref[2:3, :] += sums[2]
        dcw_ref[1:2, :] += sums[3]
        dcw_ref[0:1, :] += sums[4]
        dgn_ref[...] += sums[5]

        @pl.when(step == nt - 1)
        def _():
            dwob_ref[...] = dwo_ref[...].astype(BF16)

    rev = lambda s: (nt - 1 - s, 0)
    tile = pl.BlockSpec((ts, D), rev)
    wide = pl.BlockSpec((ts, PROJ), rev)
    bpt = ts // HALO
    halo_map = lambda s: (jnp.maximum((nt - 1 - s) * bpt - 1, 0), 0)
    groups = _full((N_GROUPS, GROUP, GROUP))
    vec = _full((1, D))
    return pl.pallas_call(
        body, name="even_backward", grid=(nt,),
        in_specs=[tile, tile, wide, pl.BlockSpec((HALO, PROJ), halo_map), tile, _full((3, A_W)), groups, groups,
                  _full((1, B_W)), _resident((D, D)), _resident((PROJ, D)), vec, vec],
        out_specs=[wide, tile, _full((D, D)), _full((8, A_W)), groups, _full((1, B_W)), vec, vec],
        out_shape=[jax.ShapeDtypeStruct((S, PROJ), BF16), jax.ShapeDtypeStruct((S, D), F32),
                   jax.ShapeDtypeStruct((D, D), BF16), jax.ShapeDtypeStruct((8, A_W), F32),
                   jax.ShapeDtypeStruct((N_GROUPS, GROUP, GROUP), F32), jax.ShapeDtypeStruct((1, B_W), F32),
                   jax.ShapeDtypeStruct((1, D), F32), jax.ShapeDtypeStruct((1, D), F32)],
        scratch_shapes=[pltpu.VMEM((HALO, D), F32), pltpu.VMEM((ts, D), BF16), pltpu.VMEM((ts, D), BF16),
                        pltpu.VMEM((D, D), F32)],
        compiler_params=_params(),
    )(g1, m0, p, p, x, cw, pw_b, pwt_b, ps, woutt_b, wint_b, gpost, gpre)


def _gate_matmuls(w_ref, in_ref, out_ref, chunks, bias_ref=None):
    for h in range(N_HEADS):
        cols = slice(h * GROUP, (h + 1) * GROUP)
        wide = jnp.concatenate([in_ref[chunk, cols] for chunk in chunks], axis=1)
        res = _dot(w_ref[h], wide)
        for k, chunk in enumerate(chunks):
            part = res[:, k * GROUP:(k + 1) * GROUP]
            out_ref[chunk, cols] = part if bias_ref is None else part + bias_ref[h]


def _odd_forward_parts(v, lg_ref, lb_ref, wt_ref, bfull_ref, vln_ref, sv_ref, r0, nr):
    rows = slice(r0, r0 + nr)
    mu = jnp.mean(v, axis=-1, keepdims=True)
    vc = v - mu
    rstd = lax.rsqrt(jnp.mean(vc * vc, axis=-1, keepdims=True) + EPS)
    vh = vc * rstd
    vln_ref[rows, :] = (vh * lg_ref[...] + lb_ref[...]).astype(BF16)
    _gate_matmuls(wt_ref, vln_ref, sv_ref, [slice(r0 + k * GROUP, r0 + (k + 1) * GROUP) for k in range(nr // GROUP)],
                  bfull_ref)
    return vh, rstd


def _odd_forward(h, x1, tgt, win_b, lg, lb, wt_b, bfull, wout_b, gpost, ts):
    S = x1.shape[0]

    def body(h_ref, x_ref, t_ref, wi_ref, lg_ref, lb_ref, wt_ref, bfull_ref, wo_ref, gp_ref,
             p_ref, m_ref, g_ref, loss_ref, vln_ref, sv_ref, y_ref):
        @pl.when(pl.program_id(0) == 0)
        def _():
            loss_ref[...] = jnp.zeros_like(loss_ref)

        loss = None
        for r0 in range(0, ts, SUB_ROWS):
            rows = slice(r0, r0 + SUB_ROWS)
            proj = _dot(h_ref[rows, :], wi_ref[...])
            p_ref[rows, :] = proj.astype(BF16)
            u, v, z = proj[:, 0:D], proj[:, D:2 * D], proj[:, 2 * D:3 * D]
            _odd_forward_parts(v, lg_ref, lb_ref, wt_ref, bfull_ref, vln_ref, sv_ref, r0, SUB_ROWS)
            y_ref[rows, :] = (u * sv_ref[rows, :] * (z * _sigmoid(z))).astype(BF16)
            m = _dot(y_ref[rows, :], wo_ref[...])
            m_ref[rows, :] = m.astype(BF16)
            x2 = x_ref[rows, :] + (m * _rms(m)) * gp_ref[...]
            err = x2 - t_ref[rows, :]
            g_ref[rows, :] = err * (1.0 / D)
            part = jnp.sum(err * err, axis=0, keepdims=True)
            loss = part if loss is None else loss + part
        loss_ref[...] += loss

    tile = pl.BlockSpec((ts, D), lambda i: (i, 0))
    small = _full((N_HEADS, GROUP, GROUP))
    return pl.pallas_call(
        body, name="odd_forward", grid=(S // ts,),
        in_specs=[pl.BlockSpec((ts, D), lambda i: (i, 0)), tile, tile, _full((D, PROJ)), _full((1, D)), _full((1, D)),
                  small, small, _full((D, D)), _full((1, D))],
        out_specs=[pl.BlockSpec((ts, PROJ), lambda i: (i, 0)), tile, tile, _full((1, D))],
        out_shape=[jax.ShapeDtypeStruct((S, PROJ), BF16), jax.ShapeDtypeStruct((S, D), BF16),
                   jax.ShapeDtypeStruct((S, D), F32), jax.ShapeDtypeStruct((1, D), F32)],
        scratch_shapes=[pltpu.VMEM((ts, D), BF16), pltpu.VMEM((ts, D), F32), pltpu.VMEM((ts, D), BF16)],
        compiler_params=_params(),
    )(h, x1, tgt, win_b, lg, lb, wt_b, bfull, wout_b, gpost)


def _odd_backward(g2, m1, p, x1, lg, lb, wt_b, wtt_b, bfull, woutt_b, wint_b, gpost, gpre, ts):
    S = g2.shape[0]
    nt = S // ts
    sub = ts // 2

    def body(g_ref, m_ref, p_ref, x_ref, lg_ref, lb_ref, wt_ref, wtt_ref, bfull_ref, wot_ref, wit_ref, gp_ref, gn_ref,
             dp_ref, dx_ref, dwob_ref, dws_ref, dbs_ref, dlg_ref, dlb_ref, dgp_ref, dgn_ref,
             vln_ref, sv_ref, dsvb_ref, dvln_ref, dsum_ref, y_ref, dm_ref, dwo_ref):
        step = pl.program_id(0)

        @pl.when(step == 0)
        def _():
            dwo_ref[...] = jnp.zeros_like(dwo_ref)
            dws_ref[...] = jnp.zeros_like(dws_ref)
            dsum_ref[...] = jnp.zeros_like(dsum_ref)
            dlg_ref[...] = jnp.zeros_like(dlg_ref)
            dlb_ref[...] = jnp.zeros_like(dlb_ref)
            dgp_ref[...] = jnp.zeros_like(dgp_ref)
            dgn_ref[...] = jnp.zeros_like(dgn_ref)

        tril = (lax.broadcasted_iota(jnp.int32, (GROUP, GROUP), 0)
                >= lax.broadcasted_iota(jnp.int32, (GROUP, GROUP), 1))
        sums = [None] * 4
        add = lambda k, part: sums.__setitem__(k, part if sums[k] is None else sums[k] + part)

        def post_norm(c):
            g = g_ref[c["rows"], :]
            m = m_ref[c["rows"], :].astype(F32)
            q = _rms(m)
            n = m * q
            add(0, jnp.sum(g * n, axis=0, keepdims=True))
            dn = g * gp_ref[...]
            dm = q * (dn - n * jnp.mean(dn * n, axis=-1, keepdims=True))
            dm_ref[c["rows"], :] = dm.astype(BF16)

        def out_projection(c):
            c["dy"] = _dot(dm_ref[c["rows"], :], wot_ref[...])

        def layernorm(c):
            v = p_ref[c["rows"], D:2 * D].astype(F32)
            mu = jnp.mean(v, axis=-1, keepdims=True)
            vc = v - mu
            c["rstd"] = lax.rsqrt(jnp.mean(vc * vc, axis=-1, keepdims=True) + EPS)
            c["vh"] = vc * c["rstd"]
            vln_ref[c["rows"], :] = (c["vh"] * lg_ref[...] + lb_ref[...]).astype(BF16)

        def gate_matmuls(c):
            _gate_matmuls(wt_ref, vln_ref, sv_ref, c["chunks"], bfull_ref)

        def gating(c):
            rows = c["rows"]
            u = p_ref[rows, 0:D].astype(F32)
            z = p_ref[rows, 2 * D:3 * D].astype(F32)
            sv = sv_ref[rows, :]
            sig = _sigmoid(z)
            sz = z * sig
            y_ref[rows, :] = (u * sv * sz).astype(BF16)
            dy = c.pop("dy")
            t = dy * sz
            dsv = t * u
            dsvb_ref[rows, :] = dsv.astype(BF16)
            for k in range(sub // GROUP):
                dsum_ref[...] += dsv[k * GROUP:(k + 1) * GROUP]
            dp_ref[rows, 0:D] = (t * sv).astype(BF16)
            dp_ref[rows, 2 * D:3 * D] = (dy * u * sv * _dsilu(z, sig)).astype(BF16)

        def gate_backward_matmuls(c):
            for chunk in c["chunks"]:
                for h in range(N_HEADS):
                    cols = slice(h * GROUP, (h + 1) * GROUP)
                    dvln_ref[chunk, cols] = _dot(wtt_ref[h], dsvb_ref[chunk, cols])
                    dws_ref[h] += jnp.where(tril, _dot_t1(dsvb_ref[chunk, cols], vln_ref[chunk, cols]), 0.0)

        def layernorm_backward(c):
            vh, rstd = c.pop("vh"), c.pop("rstd")
            dvln = dvln_ref[c["rows"], :]
            add(1, jnp.sum(dvln * vh, axis=0, keepdims=True))
            add(2, jnp.sum(dvln, axis=0, keepdims=True))
            dvh = dvln * lg_ref[...]
            dv = rstd * (dvh - jnp.mean(dvh, axis=-1, keepdims=True)
                         - vh * jnp.mean(dvh * vh, axis=-1, keepdims=True))
            dp_ref[c["rows"], D:2 * D] = dv.astype(BF16)

        def in_projection(c):
            c["dh"] = _dot(dp_ref[c["rows"], :], wit_ref[...])

        def pre_norm(c):
            dh = c.pop("dh")
            xv = x_ref[c["rows"], :]
            r = _rms(xv)
            xn = xv * r
            add(3, jnp.sum(dh * xn, axis=0, keepdims=True))
            dxn = dh * gn_ref[...]
            dx_ref[c["rows"], :] = g_ref[c["rows"], :] + r * (dxn - xn * jnp.mean(dxn * xn, axis=-1, keepdims=True))

        phases = [post_norm, out_projection, layernorm, gate_matmuls, gating, gate_backward_matmuls,
                  layernorm_backward, in_projection, pre_norm]
        groups = [dict(rows=slice(r0, r0 + sub),
                       chunks=[slice(r0 + k * GROUP, r0 + (k + 1) * GROUP) for k in range(sub // GROUP)])
                  for r0 in range(0, ts, sub)]
        for group in groups:
            for phase in phases:
                phase(group)

        dwo_ref[...] += _dot_t0(y_ref[...], dm_ref[...])
        dgp_ref[...] += sums[0]
        dlg_ref[...] += sums[1]
        dlb_ref[...] += sums[2]
        dgn_ref[...] += sums[3]

        @pl.when(step == nt - 1)
        def _():
            dwob_ref[...] = dwo_ref[...].astype(BF16)
            ones = jnp.ones((8, GROUP), F32)
            for h in range(N_HEADS):
                cols = slice(h * GROUP, (h + 1) * GROUP)
                total = lax.dot_general(ones, dsum_ref[:, cols], (((1,), (1,)), ((), ())),
                                        precision=lax.Precision.HIGHEST, preferred_element_type=F32)
                dbs_ref[h:h + 1, :] = total[0:1, :]

    tile = pl.BlockSpec((ts, D), lambda i: (i, 0))
    wide = pl.BlockSpec((ts, PROJ), lambda i: (i, 0))
    small = _full((N_HEADS, GROUP, GROUP))
    heads = _resident((N_HEADS, GROUP, GROUP))
    vec = _full((1, D))
    return pl.pallas_call(
        body, name="odd_backward", grid=(nt,),
        in_specs=[tile, tile, wide, tile, vec, vec, heads, heads, heads, _resident((D, D)), _resident((PROJ, D)),
                  vec, vec],
        out_specs=[wide, tile, _full((D, D)), small, _full((N_HEADS, GROUP)), vec, vec, vec, vec],
        out_shape=[jax.ShapeDtypeStruct((S, PROJ), BF16), jax.ShapeDtypeStruct((S, D), F32),
                   jax.ShapeDtypeStruct((D, D), BF16),
                   jax.ShapeDtypeStruct((N_HEADS, GROUP, GROUP), F32), jax.ShapeDtypeStruct((N_HEADS, GROUP), F32),
                   jax.ShapeDtypeStruct((1, D), F32), jax.ShapeDtypeStruct((1, D), F32),
                   jax.ShapeDtypeStruct((1, D), F32), jax.ShapeDtypeStruct((1, D), F32)],
        scratch_shapes=[pltpu.VMEM((ts, D), BF16), pltpu.VMEM((ts, D), F32), pltpu.VMEM((ts, D), BF16),
                        pltpu.VMEM((ts, D), F32), pltpu.VMEM((GROUP, D), F32), pltpu.VMEM((ts, D), BF16),
                        pltpu.VMEM((ts, D), BF16), pltpu.VMEM((D, D), F32)],
        compiler_params=_params(),
    )(g2, m1, p, x1, lg, lb, wt_b, wtt_b, bfull, woutt_b, wint_b, gpost, gpre)


def _weight_grad_scatter(h, dp, ts, name, sides=()):
    S = h.shape[0]
    ts = min(ts, S)
    nt = S // ts
    panel = 2 * SHARD_IN
    last = N_CHIPS - 1

    def body(h_ref, dp_ref, parts_ref, acc_ref, half_ref, swap_ref, sum_ref, send_sems, recv_sems):
        s, t = pl.program_id(0), pl.program_id(1)
        x, y, c = _mesh_position()
        sibling = (x, y, 1 - c)

        def swap(k):
            return _remote(half_ref.at[k, 1], swap_ref.at[k], send_sems, recv_sems, k, sibling)

        def chip_sum(k):
            flip = last - k
            to = (x ^ (flip >> 1), y ^ (flip & 1), c)
            return _remote(sum_ref.at[k], parts_ref.at[1 + flip], send_sems, recv_sems, 3 + flip, to)

        own = pltpu.make_async_copy(half_ref.at[last, 0], parts_ref.at[0], recv_sems.at[7])
        to_sibling = _remote(half_ref.at[last, 1], parts_ref.at[1], send_sems, recv_sems, 3, sibling)

        @pl.when(t == 0)
        def _():
            acc_ref[...] = jnp.zeros_like(acc_ref)

        for k in range(last):
            @pl.when((s == k + 1) & (t == 0))
            def _():
                swap(k).wait_recv()
                sum_ref[k] = (half_ref[k, 0].astype(F32) + swap_ref[k].astype(F32)).astype(BF16)
                chip_sum(k).start()

        acc_ref[...] += _dot_t0(h_ref[...], dp_ref[...])

        @pl.when(t == nt - 1)
        def _():
            for core in range(2):
                @pl.when(c == core)
                def _():
                    half_ref[s, core] = acc_ref[:, 0:SHARD_IN].astype(BF16)
                    half_ref[s, 1 - core] = acc_ref[:, SHARD_IN:panel].astype(BF16)
            for k in range(last):
                @pl.when(s == k)
                def _():
                    swap(k).start()

        @pl.when((s == last) & (t == nt - 1))
        def _():
            own.start()
            to_sibling.start()
            own.wait()
            _remote(half_ref.at[last, 1], parts_ref.at[1], send_sems, recv_sems, 3, sibling).wait_recv()
            for k in range(last):
                chip_sum(k).wait_recv()
            to_sibling.wait_send()
            for k in range(last):
                swap(k).wait_send()
                chip_sum(k).wait_send()

    chip_panel = lambda s, t: (t, (2 * lax.axis_index("x") + lax.axis_index("y")) ^ (last - s))
    sems = pltpu.SemaphoreType.DMA((8,))
    return _call(
        body, name=name, grid=(N_CHIPS, nt),
        in_specs=[pl.BlockSpec((ts, D), lambda s, t: (t, 0)), pl.BlockSpec((ts, panel), chip_panel)],
        out_specs=[HBM_SPEC], out_shape=[jax.ShapeDtypeStruct((N_CHIPS + 1, D, SHARD_IN), BF16)],
        scratch_shapes=[pltpu.VMEM((D, panel), F32), pltpu.VMEM((N_CHIPS, 2, D, SHARD_IN), BF16),
                        pltpu.VMEM((last, D, SHARD_IN), BF16), pltpu.VMEM((last, D, SHARD_IN), BF16), sems, sems],
        args=(h, dp), sides=sides)


def _adamw(w, g, m, v):
    m = ADAM_B1 * m + (1.0 - ADAM_B1) * g
    v = ADAM_B2 * v + (1.0 - ADAM_B2) * (g * g)
    m_hat = m / (1.0 - ADAM_B1 ** ADAM_STEP)
    v_hat = v / (1.0 - ADAM_B2 ** ADAM_STEP)
    delta = -ADAM_LR * (m_hat / (jnp.sqrt(v_hat) + ADAM_EPS) + ADAM_WD * w)
    return delta, m, v


UPDATE_STEPS = 4


def _sum_update(parts, weights):
    n_w = len(weights)

    def body(*refs):
        p_refs, wmv_refs, outs = refs[:n_w], refs[n_w:4 * n_w], refs[4 * n_w:]
        for i in range(n_w):
            acc = p_refs[i][0].astype(F32)
            for d in range(1, parts[i].shape[0]):
                acc = acc + p_refs[i][d].astype(F32)
            w_ref, m_ref, v_ref = wmv_refs[3 * i:3 * i + 3]
            g_out, d_out, m_out, v_out = outs[4 * i:4 * i + 4]
            g_out[...] = acc
            delta, m_new, v_new = _adamw(w_ref[...], acc, m_ref[...], v_ref[...])
            d_out[...] = delta
            m_out[...] = m_new
            v_out[...] = v_new

    part_specs, wmv_specs, out_specs, out_shape = [], [], [], []
    for p, (w, _, _) in zip(parts, weights):
        R, C = w.shape
        rb = R // UPDATE_STEPS
        blk = pl.BlockSpec((rb, C), lambda i: (i, 0))
        part_specs.append(pl.BlockSpec((p.shape[0], rb, C), lambda i: (0, i, 0)))
        wmv_specs += [blk] * 3
        out_specs += [blk] * 4
        out_shape += [jax.ShapeDtypeStruct((R, C), F32)] * 4
    flat = pl.pallas_call(
        body, name="update_large_weights", grid=(UPDATE_STEPS,), in_specs=part_specs + wmv_specs,
        out_specs=out_specs, out_shape=out_shape, compiler_params=_params(),
    )(*parts, *[a for wmv in weights for a in wmv])
    return [flat[4 * i:4 * i + 4] for i in range(n_w)]


def _small_sum_update(partials, row_of, weights, loss_parts):
    n_p, n_w = len(partials), len(weights)

    def body(*refs):
        p_refs = refs[:n_p]
        loss_ref = refs[n_p]
        w_refs = refs[n_p + 1:n_p + 1 + 3 * n_w]
        loss_out = refs[n_p + 1 + 3 * n_w]
        outs = refs[n_p + 2 + 3 * n_w:]

        def total(ref):
            acc = ref[0].astype(F32)
            for d in range(1, N_DEV):
                acc = acc + ref[d].astype(F32)
            return acc

        lsum = jnp.sum(total(loss_ref), axis=-1, keepdims=True) * (0.5 / D)
        loss_out[...] = jnp.broadcast_to(lsum, loss_out.shape)
        for k in range(n_p):
            i, r0 = row_of[k]
            g = total(p_refs[k])
            w_ref, m_ref, v_ref = w_refs[3 * i:3 * i + 3]
            g_out, d_out, m_out, v_out = outs[4 * i:4 * i + 4]
            if g.ndim == 2:
                at = (slice(r0, r0 + g.shape[0]), slice(None))
            else:
                at = (slice(None),) * g.ndim
            delta, m_new, v_new = _adamw(w_ref[at], g, m_ref[at], v_ref[at])
            g_out[at] = g
            d_out[at] = delta
            m_out[at] = m_new
            v_out[at] = v_new

    vm = pl.BlockSpec(memory_space=pltpu.VMEM)
    flat_w = [a for wmv in weights for a in wmv]
    out_shape = [jax.ShapeDtypeStruct((1, 128), F32)]
    for w, _, _ in weights:
        out_shape += [jax.ShapeDtypeStruct(w.shape, F32)] * 4
    return pl.pallas_call(
        body, name="small_update", in_specs=[vm] * (n_p + 1 + 3 * n_w), out_specs=[vm] * len(out_shape),
        out_shape=out_shape, compiler_params=pltpu.CompilerParams(vmem_limit_bytes=VMEM_LIMIT),
    )(*partials, loss_parts, *flat_w)


def _pack_small_shard(cw, lg, lb):
    return (jnp.pad(cw, ((0, 13), (0, 64))) + jnp.pad(lg, ((8, 7), (0, 0))) + jnp.pad(lb, ((9, 6), (0, 0))))


def kernel(x, pre_norm, post_norm, even_w_in, even_conv_w, even_pool_w, even_pool_scale, even_w_out, odd_w_in, odd_ln_g, odd_ln_b, odd_w_s, odd_b_s, odd_w_out, loss_target, m_pre_norm, m_post_norm, m_even_w_in, m_even_conv_w, m_even_pool_w, m_even_pool_scale, m_even_w_out, m_odd_w_in, m_odd_ln_g, m_odd_ln_b, m_odd_w_s, m_odd_b_s, m_odd_w_out, v_pre_norm, v_post_norm, v_even_w_in, v_even_conv_w, v_even_pool_w, v_even_pool_scale, v_even_w_out, v_odd_w_in, v_odd_ln_g, v_odd_ln_b, v_odd_w_s, v_odd_b_s, v_odd_w_out):
    S = x.shape[1]
    xs = x.reshape(S, D)
    tgt = loss_target.reshape(S, D)

    gpre0, gpre1 = pre_norm[0:1], pre_norm[1:2]
    gpost0, gpost1 = post_norm[0:1], post_norm[1:2]
    small_shard = _pack_small_shard(even_conv_w[0], odd_ln_g, odd_ln_b)
    h0, wie_g, small_g, woe_g, wio_b, woo_b = _prenorm_gather(xs, gpre0, even_w_in[0], small_shard, even_w_out[0],
                                                              odd_w_in[0], odd_w_out[0], TS_PRENORM)
    wie = jnp.transpose(wie_g, (1, 0, 2)).reshape(D, PROJ)
    wie_t = jnp.transpose(wie_g, (0, 2, 1)).reshape(PROJ, D)
    woe = woe_g.reshape(D, D)
    woe_t = woe.T
    conv_w = jnp.transpose(small_g[:, 0:3, 0:64], (1, 0, 2)).reshape(3, A_W)
    ln_g = small_g[:, 8, :].reshape(1, D)
    ln_b = small_g[:, 9, :].reshape(1, D)

    pool_w_b = even_pool_w[0].astype(BF16)
    pool_wt_b = jnp.swapaxes(even_pool_w[0], 1, 2).astype(BF16)
    ws_tril = jnp.tril(odd_w_s[0])
    ws_b = ws_tril.astype(BF16)
    wst_b = jnp.swapaxes(ws_tril, 1, 2).astype(BF16)
    b_full = jnp.broadcast_to(odd_b_s[0][:, :, None], (N_HEADS, GROUP, GROUP))

    p0, m0, x1, h1, wio_g, woo_g = _even_forward_fused(h0, xs, wie, conv_w, pool_w_b, even_pool_scale, woe, gpost0,
                                                       gpre1, TS_FWD, sides=[(_Gather, [wio_b, woo_b])])
    wio = jnp.transpose(wio_g, (1, 0, 2)).reshape(D, PROJ)
    wio_t = jnp.transpose(wio_g, (0, 2, 1)).reshape(PROJ, D)
    woo = woo_g.reshape(D, D)
    woo_t = woo.T
    p1, m1, g2, loss_vec = _odd_forward(h1, x1, tgt, wio, ln_g, ln_b, ws_b, b_full, woo, gpost1, TS_FWD)

    dp1, g1, dwoo, dws, dbs, dlg, dlb, dgpost1, dgpre1 = _odd_backward(
        g2, m1, p1, x1, ln_g, ln_b, ws_b, wst_b, b_full, woo_t, wio_t, gpost1, gpre1, TS_BWD)
    by_owner = lambda dwo: dwo.reshape(N_DEV, SHARD_OUT, D)
    (dwio_parts, dwoo_parts, dlg_parts, dlb_parts,
     dws_g, dbs_g, gpost1_g, gpre1_g, loss_g) = _weight_grad_scatter(
        h1, dp1, TS_WGRAD, "odd_weight_grad",
        sides=[(_Scatter, [by_owner(dwoo), dlg.reshape(N_DEV, 1, 128), dlb.reshape(N_DEV, 1, 128)]),
               (_Gather, [dws.astype(BF16), dbs, dgpost1, dgpre1, loss_vec])])
    dp0, gx, dwoe, dcw, dpw, dps, dgpost0, dgpre0 = _even_backward_fused(
        g1, m0, p0, xs, conv_w, pool_w_b, pool_wt_b, even_pool_scale, woe_t, wie_t, gpost0, gpre0, TS_FWD)
    dcw_by_owner = jnp.transpose(dcw[0:3].reshape(3, N_DEV, 64), (1, 0, 2))
    dwie_parts, dwoe_parts, dcw_parts, dpw_g, dps_g, gpost0_g, gpre0_g = _weight_grad_scatter(
        h0, dp0, TS_WGRAD, "even_weight_grad",
        sides=[(_Scatter, [by_owner(dwoe), dcw_by_owner]), (_Gather, [dpw.astype(BF16), dps, dgpost0, dgpre0])])

    ((g_wie, d_wie, nm_wie, nv_wie), (g_wio, d_wio, nm_wio, nv_wio), (g_woe, d_woe, nm_woe, nv_woe),
     (g_woo, d_woo, nm_woo, nv_woo)) = _sum_update(
        [dwie_parts, dwio_parts, dwoe_parts, dwoo_parts],
        [(even_w_in[0], m_even_w_in[0], v_even_w_in[0]), (odd_w_in[0], m_odd_w_in[0], v_odd_w_in[0]),
         (even_w_out[0], m_even_w_out[0], v_even_w_out[0]), (odd_w_out[0], m_odd_w_out[0], v_odd_w_out[0])])

    partials = [gpre0_g, gpre1_g, gpost0_g, gpost1_g, dpw_g, dps_g, dws_g, dbs_g, dcw_parts, dlg_parts, dlb_parts]
    row_of = [(0, 0), (0, 1), (1, 0), (1, 1), (2, 0), (3, 0), (4, 0), (5, 0), (6, 0), (7, 0), (8, 0)]
    weights = [(pre_norm, m_pre_norm, v_pre_norm), (post_norm, m_post_norm, v_post_norm),
               (even_pool_w[0], m_even_pool_w[0], v_even_pool_w[0]),
               (even_pool_scale, m_even_pool_scale, v_even_pool_scale),
               (odd_w_s[0], m_odd_w_s[0], v_odd_w_s[0]), (odd_b_s[0], m_odd_b_s[0], v_odd_b_s[0]),
               (even_conv_w[0], m_even_conv_w[0], v_even_conv_w[0]),
               (odd_ln_g, m_odd_ln_g, v_odd_ln_g), (odd_ln_b, m_odd_ln_b, v_odd_ln_b)]
    small = _small_sum_update(partials, row_of, weights, loss_g)
    loss = small[0][0, 0]

    def leaves(k, big):
        pre, post, pw, psc, ws, bs, cw, lg, lb = [small[1 + 4 * i + k] for i in range(len(weights))]
        wie_k, woe_k, wio_k, woo_k = big
        return [pre, post, wie_k[None], cw[None], pw[None], psc, woe_k[None], wio_k[None], lg, lb, ws[None], bs[None],
                woo_k[None]]

    outs = [loss, gx.reshape(1, S, D)]
    outs += leaves(0, (g_wie, g_woe, g_wio, g_woo))
    outs += leaves(1, (d_wie, d_woe, d_wio, d_woo))
    outs += leaves(2, (nm_wie, nm_woe, nm_wio, nm_woo))
    outs += leaves(3, (nv_wie, nv_woe, nv_wio, nv_woo))
    return tuple(outs)
```

```python
import jax
import jax.numpy as jnp
from jax import lax
from jax.experimental import pallas as pl
from jax.experimental.pallas import tpu as pltpu

F32 = jnp.float32
BF16 = jnp.bfloat16
MESH = pl.DeviceIdType.MESH

D = 1024
EPS = 1e-6
A_W = 512
B_W = 512
POOL_WINDOWS = (2, 4, 8, 16)
GROUP = 128
N_GROUPS = 4
N_HEADS = 8
HALO = 16
PROJ = 3072
N_DEV = 8
N_CHIPS = 4
SHARD_IN = PROJ // N_DEV
SHARD_OUT = D // N_DEV

ADAM_LR = 0.001
ADAM_B1 = 0.9
ADAM_B2 = 0.999
ADAM_EPS = 1e-08
ADAM_WD = 0.01
ADAM_STEP = 10

VMEM_LIMIT = 54 * 1024 * 1024

TS_PRENORM = 1024
TS_FWD = 512
TS_BWD = 256
TS_WGRAD = 2048
SUB_ROWS = 256


def _params(n_grid=1, vmem=VMEM_LIMIT):
    return pltpu.CompilerParams(dimension_semantics=("arbitrary",) * n_grid, vmem_limit_bytes=vmem)


def _full(shape):
    return pl.BlockSpec(shape, lambda *_: (0,) * len(shape))


def _resident(shape):
    return pl.BlockSpec(shape, lambda *_: (0,) * len(shape), pipeline_mode=pl.Buffered(1))


def _sigmoid(z):
    return jax.nn.sigmoid(z)


def _dsilu(z, s):
    return s * (1.0 + z * (1.0 - s))


def _dot(a, b):
    return jnp.dot(a, b, preferred_element_type=F32)


def _dot_t0(a, b):
    return lax.dot_general(a, b, (((0,), (0,)), ((), ())), preferred_element_type=F32)


def _dot_t1(a, b):
    return lax.dot_general(a, b, (((1,), (1,)), ((), ())), preferred_element_type=F32)


def _rms(x):
    return lax.rsqrt(jnp.mean(x * x, axis=-1, keepdims=True) + EPS)


def _shift_down(a, k):
    return pltpu.roll(a, k, 0)


def _shift_up(a, k):
    return pltpu.roll(a, a.shape[0] - k, 0)


def _mesh_position():
    return lax.axis_index("x"), lax.axis_index("y"), lax.axis_index("c")


def _index(pos):
    return 4 * pos[0] + 2 * pos[1] + pos[2]


def _peer(pos, d):
    x, y, c = pos
    return (1 - x if d & 4 else x, 1 - y if d & 2 else y, 1 - c if d & 1 else c)


def _remote(src, dst, send_sems, recv_sems, k, to):
    return pltpu.make_async_remote_copy(src_ref=src, dst_ref=dst, send_sem=send_sems.at[k], recv_sem=recv_sems.at[k],
                                        device_id=to, device_id_type=MESH)


class _Gather:
    def __init__(self, srcs, dsts, send_sems, recv_sems):
        x, y, c = _mesh_position()
        n = len(srcs)
        me, sibling = (x, y, c), (x, y, 1 - c)
        chips = [(1 - x, y), (x, 1 - y), (1 - x, 1 - y)]

        def copy(k, t, block, to, src=None):
            slot = dsts[t].at[_index(block)]
            return _remote(slot if src is None else src, slot, send_sems, recv_sems, k * n + t, to)

        tensors = range(n)
        self.local = [pltpu.make_async_copy(srcs[t], dsts[t].at[_index(me)], recv_sems.at[7 * n + t]) for t in tensors]
        self.first = [copy(1 + j, t, me, chip + (c,), srcs[t]) for t in tensors for j, chip in enumerate(chips)]
        self.first += [copy(0, t, me, sibling, srcs[t]) for t in tensors]
        self.ici_in = [copy(1 + j, t, chip + (c,), me) for t in tensors for j, chip in enumerate(chips)]
        self.passed = [copy(4 + j, t, chip + (c,), sibling) for t in tensors for j, chip in enumerate(chips)]
        self.d2d_in = [copy(0, t, sibling, me) for t in tensors]
        self.d2d_in += [copy(4 + j, t, chip + (1 - c,), me) for t in tensors for j, chip in enumerate(chips)]

    def start(self):
        for cp in self.local + self.first:
            cp.start()

    def middle(self):
        for landed, onward in zip(self.ici_in, self.passed):
            landed.wait_recv()
            onward.start()

    def finish(self):
        for cp in self.d2d_in:
            cp.wait_recv()
        for cp in self.first + self.passed:
            cp.wait_send()
        for cp in self.local:
            cp.wait()


class _Scatter:
    def __init__(self, srcs, dsts, send_sems, recv_sems):
        pos = _mesh_position()
        n = len(srcs)
        self.local = [pltpu.make_async_copy(srcs[t].at[_index(pos)], dsts[t].at[0], recv_sems.at[t]) for t in range(n)]
        self.remote = []
        for d in range(1, N_DEV):
            to = _peer(pos, d)
            self.remote += [_remote(srcs[t].at[_index(to)], dsts[t].at[d], send_sems, recv_sems, d * n + t, to)
                            for t in range(n)]

    def start(self):
        for cp in self.local + self.remote:
            cp.start()

    def middle(self):
        pass

    def finish(self):
        for cp in self.local:
            cp.wait()
        for cp in self.remote:
            cp.wait_recv()
        for cp in self.remote:
            cp.wait_send()


HBM_SPEC = pl.BlockSpec(memory_space=pltpu.HBM)
MIDDLE_STEPS_BEFORE_END = 4


def _landing_shape(exchange, src):
    return (N_DEV,) + src.shape if exchange is _Gather else src.shape


def _call(body, *, name, grid, in_specs, out_specs, out_shape, scratch_shapes, args, sides=()):
    params = _params(len(grid))
    if not sides:
        return pl.pallas_call(body, name=name, grid=grid, in_specs=in_specs, out_specs=out_specs, out_shape=out_shape,
                              scratch_shapes=scratch_shapes, compiler_params=params)(*args)
    n_in, n_out, n_scratch = len(in_specs), len(out_specs), len(scratch_shapes)
    counts = [len(srcs) for _, srcs in sides]
    ns = sum(counts)
    total = 1
    for g in grid:
        total *= g
    side_args, side_shapes, side_sems = [], [], []
    for exchange, srcs in sides:
        side_args += list(srcs)
        side_shapes += [jax.ShapeDtypeStruct(_landing_shape(exchange, s), s.dtype) for s in srcs]
        side_sems += [pltpu.SemaphoreType.DMA((N_DEV * len(srcs),))] * 2

    def wrapped(*refs):
        ins, side_in = refs[:n_in], refs[n_in:n_in + ns]
        outs = refs[n_in + ns:n_in + ns + n_out]
        side_out = refs[n_in + ns + n_out:n_in + 2 * ns + n_out]
        rest = refs[n_in + 2 * ns + n_out:]
        scratch, sems = rest[:n_scratch], rest[n_scratch:]
        step = pl.program_id(0)
        for axis in range(1, len(grid)):
            step = step * grid[axis] + pl.program_id(axis)

        def exchanges():
            built, at = [], 0
            for k, (exchange, _) in enumerate(sides):
                built.append(exchange(side_in[at:at + counts[k]], side_out[at:at + counts[k]],
                                      sems[2 * k], sems[2 * k + 1]))
                at += counts[k]
            return built

        @pl.when(step == 0)
        def _():
            for ex in exchanges():
                ex.start()

        @pl.when(step == max(total - MIDDLE_STEPS_BEFORE_END, 0))
        def _():
            for ex in exchanges():
                ex.middle()

        body(*ins, *outs, *scratch)

        @pl.when(step == total - 1)
        def _():
            for ex in exchanges():
                ex.finish()

    return pl.pallas_call(
        wrapped, name=name, grid=grid, in_specs=list(in_specs) + [HBM_SPEC] * ns,
        out_specs=list(out_specs) + [HBM_SPEC] * ns, out_shape=list(out_shape) + side_shapes,
        scratch_shapes=list(scratch_shapes) + side_sems, compiler_params=params)(*args, *side_args)


def _prenorm_gather(x, g, wie, small, woe, wio, woo, ts):
    S = x.shape[0]
    nt = S // ts

    def body(x_ref, g_ref, wie_ref, sm_ref, woe_ref, wio_ref, woo_ref,
             h_ref, wie_out, sm_out, woe_out, wio_b, woo_b, cast_ref, small_ref, cast2_ref, send_sems, recv_sems):
        step = pl.program_id(0)
        gather = lambda: _Gather([cast_ref, small_ref, cast2_ref], [wie_out, sm_out, woe_out], send_sems, recv_sems)

        @pl.when(step == 0)
        def _():
            cast_ref[...] = wie_ref[...].astype(BF16)
            small_ref[...] = sm_ref[...]
            cast2_ref[...] = woe_ref[...].astype(BF16)
            gather().start()
            wio_b[...] = wio_ref[...].astype(BF16)
            woo_b[...] = woo_ref[...].astype(BF16)

        xv = x_ref[...]
        h_ref[...] = ((xv * _rms(xv)) * g_ref[...]).astype(BF16)

        @pl.when(step == nt - 1)
        def _():
            exchange = gather()
            exchange.middle()
            exchange.finish()

    tile = pl.BlockSpec((ts, D), lambda i: (i, 0))
    sems = pltpu.SemaphoreType.DMA((N_DEV * 3,))
    shards = [wie, small, woe, wio, woo]
    return pl.pallas_call(
        body, name="prenorm_gather", grid=(nt,),
        in_specs=[tile, _full((1, D))] + [_full(a.shape) for a in shards],
        out_specs=[tile, HBM_SPEC, HBM_SPEC, HBM_SPEC] + [_full(a.shape) for a in shards[3:]],
        out_shape=[jax.ShapeDtypeStruct((S, D), BF16), jax.ShapeDtypeStruct((N_DEV,) + wie.shape, BF16),
                   jax.ShapeDtypeStruct((N_DEV,) + small.shape, F32), jax.ShapeDtypeStruct((N_DEV,) + woe.shape, BF16)]
        + [jax.ShapeDtypeStruct(a.shape, BF16) for a in shards[3:]],
        scratch_shapes=[pltpu.VMEM(wie.shape, BF16), pltpu.VMEM(small.shape, F32), pltpu.VMEM(woe.shape, BF16),
                        sems, sems],
        compiler_params=_params(),
    )(x, g, wie, small, woe, wio, woo)


def _even_mixer(proj, halo, row0, cw_ref, pw_ref):
    xa, gb, gc, za, xp, zp = proj
    nr = xa.shape[0]
    ha = gc * xa
    ha_ext = jnp.concatenate([halo[0], ha], axis=0)
    ha_m1 = _shift_down(ha_ext, 1)[HALO:]
    ha_m2 = _shift_down(ha_ext, 2)[HALO:]
    conv = cw_ref[2:3, :] * ha + cw_ref[1:2, :] * ha_m1 + cw_ref[0:1, :] * ha_m2
    sig_a = _sigmoid(za)
    silu_a = za * sig_a

    xp_ext = jnp.concatenate([halo[1], xp], axis=0)
    pos = row0 + lax.broadcasted_iota(jnp.int32, (nr, 1), 0)
    pooled, inv_cnt, mixed = [], [], []
    for g, w in enumerate(POOL_WINDOWS):
        cols = slice(g * GROUP, (g + 1) * GROUP)
        s = xp_ext[:, cols]
        k = 1
        while k < w:
            s = s + _shift_down(s, k)
            k *= 2
        inv = 1.0 / jnp.minimum(pos + 1, w).astype(F32)
        pg = s[HALO:] * inv - xp[:, cols]
        pooled.append(pg)
        inv_cnt.append(inv)
        mixed.append(_dot(pg.astype(BF16), pw_ref[g]))
    mixed = jnp.concatenate(mixed, axis=1)
    sig_b = _sigmoid(zp)
    silu_b = zp * sig_b
    return dict(xa=xa, gb=gb, gc=gc, za=za, xp=xp, zp=zp, ha=ha, ha_m1=ha_m1, ha_m2=ha_m2, conv=conv,
                sig_a=sig_a, silu_a=silu_a, pooled=pooled, inv_cnt=inv_cnt, mixed=mixed, sig_b=sig_b,
                silu_b=silu_b)


def _columns(a):
    return [a[:, k * A_W:(k + 1) * A_W].astype(F32) for k in range(6)]


def _even_forward_fused(h, x, win_b, cw, pw_b, ps, wout_b, gpost, gpre_next, ts, sides=()):
    S = x.shape[0]

    def body(h_ref, x_ref, wi_ref, cw_ref, pw_ref, ps_ref, wo_ref, gp_ref, gn_ref,
             p_ref, m_ref, x1_ref, h1_ref, mix_ref, carry_ref):
        i = pl.program_id(0)

        @pl.when(i == 0)
        def _():
            carry_ref[...] = jnp.zeros_like(carry_ref)

        for r0 in range(0, ts, SUB_ROWS):
            rows = slice(r0, r0 + SUB_ROWS)
            proj = _dot(h_ref[rows, :], wi_ref[...])
            p_ref[rows, :] = proj.astype(BF16)
            fw = _even_mixer(_columns(proj), (carry_ref[:, 0:A_W], carry_ref[:, A_W:D]), i * ts + r0, cw_ref, pw_ref)
            carry_ref[:, 0:A_W] = fw["ha"][SUB_ROWS - HALO:]
            carry_ref[:, A_W:D] = fw["xp"][SUB_ROWS - HALO:]
            mix_ref[rows, 0:A_W] = (fw["gb"] * fw["conv"] * fw["silu_a"]).astype(BF16)
            mix_ref[rows, A_W:D] = (fw["mixed"] * ps_ref[...] * fw["silu_b"]).astype(BF16)
            m = _dot(mix_ref[rows, :], wo_ref[...])
            m_ref[rows, :] = m.astype(BF16)
            x1 = x_ref[rows, :] + (m * _rms(m)) * gp_ref[...]
            x1_ref[rows, :] = x1
            h1_ref[rows, :] = ((x1 * _rms(x1)) * gn_ref[...]).astype(BF16)

    tile = pl.BlockSpec((ts, D), lambda i: (i, 0))
    return _call(
        body, name="even_forward", grid=(S // ts,),
        in_specs=[tile, tile, _resident((D, PROJ)), _full((3, A_W)), _full((N_GROUPS, GROUP, GROUP)), _full((1, B_W)),
                  _resident((D, D)), _full((1, D)), _full((1, D))],
        out_specs=[pl.BlockSpec((ts, PROJ), lambda i: (i, 0)), tile, tile, tile],
        out_shape=[jax.ShapeDtypeStruct((S, PROJ), BF16), jax.ShapeDtypeStruct((S, D), BF16),
                   jax.ShapeDtypeStruct((S, D), F32), jax.ShapeDtypeStruct((S, D), BF16)],
        scratch_shapes=[pltpu.VMEM((ts, D), BF16), pltpu.VMEM((HALO, D), F32)],
        args=(h, x, win_b, cw, pw_b, ps, wout_b, gpost, gpre_next), sides=sides)


def _even_backward_fused(g1, m0, p, x, cw, pw_b, pwt_b, ps, woutt_b, wint_b, gpost, gpre, ts):
    S = g1.shape[0]
    nt = S // ts

    def body(g_ref, m_ref, p_ref, halo_ref, x_ref, cw_ref, pw_ref, pwt_ref, ps_ref, wot_ref, wit_ref, gp_ref, gn_ref,
             dp_ref, dx_ref, dwob_ref, dcw_ref, dpw_ref, dps_ref, dgp_ref, dgn_ref,
             carry_ref, mix_ref, dm_ref, dwo_ref):
        step = pl.program_id(0)
        i = nt - 1 - step

        @pl.when(step == 0)
        def _():
            carry_ref[...] = jnp.zeros_like(carry_ref)
            dwo_ref[...] = jnp.zeros_like(dwo_ref)
            dcw_ref[...] = jnp.zeros_like(dcw_ref)
            dpw_ref[...] = jnp.zeros_like(dpw_ref)
            dps_ref[...] = jnp.zeros_like(dps_ref)
            dgp_ref[...] = jnp.zeros_like(dgp_ref)
            dgn_ref[...] = jnp.zeros_like(dgn_ref)

        ps_v = ps_ref[...]
        sums = [None] * 6
        add = lambda k, part: sums.__setitem__(k, part if sums[k] is None else sums[k] + part)
        for r0 in range(ts - SUB_ROWS, -1, -SUB_ROWS):
            rows = slice(r0, r0 + SUB_ROWS)
            g = g_ref[rows, :]
            m = m_ref[rows, :].astype(F32)
            q = _rms(m)
            n = m * q
            add(0, jnp.sum(g * n, axis=0, keepdims=True))
            dn = g * gp_ref[...]
            dm = q * (dn - n * jnp.mean(dn * n, axis=-1, keepdims=True))
            dm_ref[rows, :] = dm.astype(BF16)
            dmix = _dot(dm_ref[rows, :], wot_ref[...])
            dya = dmix[:, 0:A_W]
            dyb = dmix[:, A_W:D]

            if r0 == 0:
                before = _columns(halo_ref[...])
                keep = jnp.where(i == 0, 0.0, 1.0).astype(F32)
                halo = (before[2] * before[0] * keep, before[4] * keep)
            else:
                before = _columns(p_ref[r0 - HALO:r0, :])
                halo = (before[2] * before[0], before[4])
            fw = _even_mixer(_columns(p_ref[rows, :]), halo, i * ts + r0, cw_ref, pw_ref)
            mix_ref[rows, 0:A_W] = (fw["gb"] * fw["conv"] * fw["silu_a"]).astype(BF16)
            mix_ref[rows, A_W:D] = (fw["mixed"] * ps_v * fw["silu_b"]).astype(BF16)

            t = dya * fw["gb"]
            dconv = t * fw["silu_a"]
            dgb = dya * fw["conv"] * fw["silu_a"]
            dza = t * fw["conv"] * _dsilu(fw["za"], fw["sig_a"])
            add(2, jnp.sum(dconv * fw["ha"], axis=0, keepdims=True))
            add(3, jnp.sum(dconv * fw["ha_m1"], axis=0, keepdims=True))
            add(4, jnp.sum(dconv * fw["ha_m2"], axis=0, keepdims=True))
            dconv_ext = jnp.concatenate([dconv, carry_ref[:, 0:A_W]], axis=0)
            dha = (cw_ref[2:3, :] * dconv + cw_ref[1:2, :] * _shift_up(dconv_ext, 1)[:SUB_ROWS]
                   + cw_ref[0:1, :] * _shift_up(dconv_ext, 2)[:SUB_ROWS])
            dgc = dha * fw["xa"]
            dxa = dha * fw["gc"]

            u = dyb * fw["mixed"]
            add(1, jnp.sum(u * fw["silu_b"], axis=0, keepdims=True))
            dzp = u * ps_v * _dsilu(fw["zp"], fw["sig_b"])
            dmixed = (dyb * ps_v * fw["silu_b"]).astype(BF16)
            dxp, e_first = [], []
            for gi, w in enumerate(POOL_WINDOWS):
                cols = slice(gi * GROUP, (gi + 1) * GROUP)
                dmg = dmixed[:, cols]
                dpooled = _dot(dmg, pwt_ref[gi])
                dpw_ref[gi] += _dot_t0(fw["pooled"][gi].astype(BF16), dmg)
                e = dpooled * fw["inv_cnt"][gi]
                e_first.append(e[0:HALO])
                s = jnp.concatenate([e, carry_ref[:, A_W + gi * GROUP:A_W + (gi + 1) * GROUP]], axis=0)
                k = 1
                while k < w:
                    s = s + _shift_up(s, k)
                    k *= 2
                dxp.append(s[:SUB_ROWS] - dpooled)
            carry_ref[:, 0:A_W] = dconv[0:HALO]
            carry_ref[:, A_W:D] = jnp.concatenate(e_first, axis=1)

            dp_ref[rows, 0:512] = dxa.astype(BF16)
            dp_ref[rows, 512:1024] = dgb.astype(BF16)
            dp_ref[rows, 1024:1536] = dgc.astype(BF16)
            dp_ref[rows, 1536:2048] = dza.astype(BF16)
            dp_ref[rows, 2048:2560] = jnp.concatenate(dxp, axis=1).astype(BF16)
            dp_ref[rows, 2560:3072] = dzp.astype(BF16)

            dh = _dot(dp_ref[rows, :], wit_ref[...])
            xv = x_ref[rows, :]
            r = _rms(xv)
            xn = xv * r
            add(5, jnp.sum(dh * xn, axis=0, keepdims=True))
            dxn = dh * gn_ref[...]
            dx_ref[rows, :] = g + r * (dxn - xn * jnp.mean(dxn * xn, axis=-1, keepdims=True))

        dwo_ref[...] += _dot_t0(mix_ref[...], dm_ref[...])
        dgp_ref[...] += sums[0]
        dps_ref[...] += sums[1]
        dcw_ref[2:3, :] += sums[2]
        dcw_ref[1:2, :] += sums[3]
        dcw_ref[0:1, :] += sums[4]
        dgn_ref[...] += sums[5]

        @pl.when(step == nt - 1)
        def _():
            dwob_ref[...] = dwo_ref[...].astype(BF16)

    rev = lambda s: (nt - 1 - s, 0)
    tile = pl.BlockSpec((ts, D), rev)
    wide = pl.BlockSpec((ts, PROJ), rev)
    bpt = ts // HALO
    halo_map = lambda s: (jnp.maximum((nt - 1 - s) * bpt - 1, 0), 0)
    groups = _full((N_GROUPS, GROUP, GROUP))
    vec = _full((1, D))
    return pl.pallas_call(
        body, name="even_backward", grid=(nt,),
        in_specs=[tile, tile, wide, pl.BlockSpec((HALO, PROJ), halo_map), tile, _full((3, A_W)), groups, groups,
                  _full((1, B_W)), _resident((D, D)), _resident((PROJ, D)), vec, vec],
        out_specs=[wide, tile, _full((D, D)), _full((8, A_W)), groups, _full((1, B_W)), vec, vec],
        out_shape=[jax.ShapeDtypeStruct((S, PROJ), BF16), jax.ShapeDtypeStruct((S, D), F32),
                   jax.ShapeDtypeStruct((D, D), BF16), jax.ShapeDtypeStruct((8, A_W), F32),
                   jax.ShapeDtypeStruct((N_GROUPS, GROUP, GROUP), F32), jax.ShapeDtypeStruct((1, B_W), F32),
                   jax.ShapeDtypeStruct((1, D), F32), jax.ShapeDtypeStruct((1, D), F32)],
        scratch_shapes=[pltpu.VMEM((HALO, D), F32), pltpu.VMEM((ts, D), BF16), pltpu.VMEM((ts, D), BF16),
                        pltpu.VMEM((D, D), F32)],
        compiler_params=_params(),
    )(g1, m0, p, p, x, cw, pw_b, pwt_b, ps, woutt_b, wint_b, gpost, gpre)


def _gate_matmuls(w_ref, in_ref, out_ref, chunks, bias_ref=None):
    for h in range(N_HEADS):
        cols = slice(h * GROUP, (h + 1) * GROUP)
        wide = jnp.concatenate([in_ref[chunk, cols] for chunk in chunks], axis=1)
        res = _dot(w_ref[h], wide)
        for k, chunk in enumerate(chunks):
            part = res[:, k * GROUP:(k + 1) * GROUP]
            out_ref[chunk, cols] = part if bias_ref is None else part + bias_ref[h]


def _odd_forward_parts(v, lg_ref, lb_ref, wt_ref, bfull_ref, vln_ref, sv_ref, r0, nr):
    rows = slice(r0, r0 + nr)
    mu = jnp.mean(v, axis=-1, keepdims=True)
    vc = v - mu
    rstd = lax.rsqrt(jnp.mean(vc * vc, axis=-1, keepdims=True) + EPS)
    vh = vc * rstd
    vln_ref[rows, :] = (vh * lg_ref[...] + lb_ref[...]).astype(BF16)
    _gate_matmuls(wt_ref, vln_ref, sv_ref, [slice(r0 + k * GROUP, r0 + (k + 1) * GROUP) for k in range(nr // GROUP)],
                  bfull_ref)
    return vh, rstd


def _odd_forward(h, x1, tgt, win_b, lg, lb, wt_b, bfull, wout_b, gpost, ts):
    S = x1.shape[0]

    def body(h_ref, x_ref, t_ref, wi_ref, lg_ref, lb_ref, wt_ref, bfull_ref, wo_ref, gp_ref,
             p_ref, m_ref, g_ref, loss_ref, vln_ref, sv_ref, y_ref):
        @pl.when(pl.program_id(0) == 0)
        def _():
            loss_ref[...] = jnp.zeros_like(loss_ref)

        loss = None
        for r0 in range(0, ts, SUB_ROWS):
            rows = slice(r0, r0 + SUB_ROWS)
            proj = _dot(h_ref[rows, :], wi_ref[...])
            p_ref[rows, :] = proj.astype(BF16)
            u, v, z = proj[:, 0:D], proj[:, D:2 * D], proj[:, 2 * D:3 * D]
            _odd_forward_parts(v, lg_ref, lb_ref, wt_ref, bfull_ref, vln_ref, sv_ref, r0, SUB_ROWS)
            y_ref[rows, :] = (u * sv_ref[rows, :] * (z * _sigmoid(z))).astype(BF16)
            m = _dot(y_ref[rows, :], wo_ref[...])
            m_ref[rows, :] = m.astype(BF16)
            x2 = x_ref[rows, :] + (m * _rms(m)) * gp_ref[...]
            err = x2 - t_ref[rows, :]
            g_ref[rows, :] = err * (1.0 / D)
            part = jnp.sum(err * err, axis=0, keepdims=True)
            loss = part if loss is None else loss + part
        loss_ref[...] += loss

    tile = pl.BlockSpec((ts, D), lambda i: (i, 0))
    small = _full((N_HEADS, GROUP, GROUP))
    return pl.pallas_call(
        body, name="odd_forward", grid=(S // ts,),
        in_specs=[pl.BlockSpec((ts, D), lambda i: (i, 0)), tile, tile, _full((D, PROJ)), _full((1, D)), _full((1, D)),
                  small, small, _full((D, D)), _full((1, D))],
        out_specs=[pl.BlockSpec((ts, PROJ), lambda i: (i, 0)), tile, tile, _full((1, D))],
        out_shape=[jax.ShapeDtypeStruct((S, PROJ), BF16), jax.ShapeDtypeStruct((S, D), BF16),
                   jax.ShapeDtypeStruct((S, D), F32), jax.ShapeDtypeStruct((1, D), F32)],
        scratch_shapes=[pltpu.VMEM((ts, D), BF16), pltpu.VMEM((ts, D), F32), pltpu.VMEM((ts, D), BF16)],
        compiler_params=_params(),
    )(h, x1, tgt, win_b, lg, lb, wt_b, bfull, wout_b, gpost)


def _odd_backward(g2, m1, p, x1, lg, lb, wt_b, wtt_b, bfull, woutt_b, wint_b, gpost, gpre, ts):
    S = g2.shape[0]
    nt = S // ts
    sub = ts // 2

    def body(g_ref, m_ref, p_ref, x_ref, lg_ref, lb_ref, wt_ref, wtt_ref, bfull_ref, wot_ref, wit_ref, gp_ref, gn_ref,
             dp_ref, dx_ref, dwob_ref, dws_ref, dbs_ref, dlg_ref, dlb_ref, dgp_ref, dgn_ref,
             vln_ref, sv_ref, dsvb_ref, dvln_ref, dsum_ref, y_ref, dm_ref, dwo_ref):
        step = pl.program_id(0)

        @pl.when(step == 0)
        def _():
            dwo_ref[...] = jnp.zeros_like(dwo_ref)
            dws_ref[...] = jnp.zeros_like(dws_ref)
            dsum_ref[...] = jnp.zeros_like(dsum_ref)
            dlg_ref[...] = jnp.zeros_like(dlg_ref)
            dlb_ref[...] = jnp.zeros_like(dlb_ref)
            dgp_ref[...] = jnp.zeros_like(dgp_ref)
            dgn_ref[...] = jnp.zeros_like(dgn_ref)

        tril = (lax.broadcasted_iota(jnp.int32, (GROUP, GROUP), 0)
                >= lax.broadcasted_iota(jnp.int32, (GROUP, GROUP), 1))
        sums = [None] * 4
        add = lambda k, part: sums.__setitem__(k, part if sums[k] is None else sums[k] + part)

        def post_norm(c):
            g = g_ref[c["rows"], :]
            m = m_ref[c["rows"], :].astype(F32)
            q = _rms(m)
            n = m * q
            add(0, jnp.sum(g * n, axis=0, keepdims=True))
            dn = g * gp_ref[...]
            dm = q * (dn - n * jnp.mean(dn * n, axis=-1, keepdims=True))
            dm_ref[c["rows"], :] = dm.astype(BF16)

        def out_projection(c):
            c["dy"] = _dot(dm_ref[c["rows"], :], wot_ref[...])

        def layernorm(c):
            v = p_ref[c["rows"], D:2 * D].astype(F32)
            mu = jnp.mean(v, axis=-1, keepdims=True)
            vc = v - mu
            c["rstd"] = lax.rsqrt(jnp.mean(vc * vc, axis=-1, keepdims=True) + EPS)
            c["vh"] = vc * c["rstd"]
            vln_ref[c["rows"], :] = (c["vh"] * lg_ref[...] + lb_ref[...]).astype(BF16)

        def gate_matmuls(c):
            _gate_matmuls(wt_ref, vln_ref, sv_ref, c["chunks"], bfull_ref)

        def gating(c):
            rows = c["rows"]
            u = p_ref[rows, 0:D].astype(F32)
            z = p_ref[rows, 2 * D:3 * D].astype(F32)
            sv = sv_ref[rows, :]
            sig = _sigmoid(z)
            sz = z * sig
            y_ref[rows, :] = (u * sv * sz).astype(BF16)
            dy = c.pop("dy")
            t = dy * sz
            dsv = t * u
            dsvb_ref[rows, :] = dsv.astype(BF16)
            for k in range(sub // GROUP):
                dsum_ref[...] += dsv[k * GROUP:(k + 1) * GROUP]
            dp_ref[rows, 0:D] = (t * sv).astype(BF16)
            dp_ref[rows, 2 * D:3 * D] = (dy * u * sv * _dsilu(z, sig)).astype(BF16)

        def gate_backward_matmuls(c):
            for chunk in c["chunks"]:
                for h in range(N_HEADS):
                    cols = slice(h * GROUP, (h + 1) * GROUP)
                    dvln_ref[chunk, cols] = _dot(wtt_ref[h], dsvb_ref[chunk, cols])
                    dws_ref[h] += jnp.where(tril, _dot_t1(dsvb_ref[chunk, cols], vln_ref[chunk, cols]), 0.0)

        def layernorm_backward(c):
            vh, rstd = c.pop("vh"), c.pop("rstd")
            dvln = dvln_ref[c["rows"], :]
            add(1, jnp.sum(dvln * vh, axis=0, keepdims=True))
            add(2, jnp.sum(dvln, axis=0, keepdims=True))
            dvh = dvln * lg_ref[...]
            dv = rstd * (dvh - jnp.mean(dvh, axis=-1, keepdims=True)
                         - vh * jnp.mean(dvh * vh, axis=-1, keepdims=True))
            dp_ref[c["rows"], D:2 * D] = dv.astype(BF16)

        def in_projection(c):
            c["dh"] = _dot(dp_ref[c["rows"], :], wit_ref[...])

        def pre_norm(c):
            dh = c.pop("dh")
            xv = x_ref[c["rows"], :]
            r = _rms(xv)
            xn = xv * r
            add(3, jnp.sum(dh * xn, axis=0, keepdims=True))
            dxn = dh * gn_ref[...]
            dx_ref[c["rows"], :] = g_ref[c["rows"], :] + r * (dxn - xn * jnp.mean(dxn * xn, axis=-1, keepdims=True))

        phases = [post_norm, out_projection, layernorm, gate_matmuls, gating, gate_backward_matmuls,
                  layernorm_backward, in_projection, pre_norm]
        groups = [dict(rows=slice(r0, r0 + sub),
                       chunks=[slice(r0 + k * GROUP, r0 + (k + 1) * GROUP) for k in range(sub // GROUP)])
                  for r0 in range(0, ts, sub)]
        for group in groups:
            for phase in phases:
                phase(group)

        dwo_ref[...] += _dot_t0(y_ref[...], dm_ref[...])
        dgp_ref[...] += sums[0]
        dlg_ref[...] += sums[1]
        dlb_ref[...] += sums[2]
        dgn_ref[...] += sums[3]

        @pl.when(step == nt - 1)
        def _():
            dwob_ref[...] = dwo_ref[...].astype(BF16)
            ones = jnp.ones((8, GROUP), F32)
            for h in range(N_HEADS):
                cols = slice(h * GROUP, (h + 1) * GROUP)
                total = lax.dot_general(ones, dsum_ref[:, cols], (((1,), (1,)), ((), ())),
                                        precision=lax.Precision.HIGHEST, preferred_element_type=F32)
                dbs_ref[h:h + 1, :] = total[0:1, :]

    tile = pl.BlockSpec((ts, D), lambda i: (i, 0))
    wide = pl.BlockSpec((ts, PROJ), lambda i: (i, 0))
    small = _full((N_HEADS, GROUP, GROUP))
    heads = _resident((N_HEADS, GROUP, GROUP))
    vec = _full((1, D))
    return pl.pallas_call(
        body, name="odd_backward", grid=(nt,),
        in_specs=[tile, tile, wide, tile, vec, vec, heads, heads, heads, _resident((D, D)), _resident((PROJ, D)),
                  vec, vec],
        out_specs=[wide, tile, _full((D, D)), small, _full((N_HEADS, GROUP)), vec, vec, vec, vec],
        out_shape=[jax.ShapeDtypeStruct((S, PROJ), BF16), jax.ShapeDtypeStruct((S, D), F32),
                   jax.ShapeDtypeStruct((D, D), BF16),
                   jax.ShapeDtypeStruct((N_HEADS, GROUP, GROUP), F32), jax.ShapeDtypeStruct((N_HEADS, GROUP), F32),
                   jax.ShapeDtypeStruct((1, D), F32), jax.ShapeDtypeStruct((1, D), F32),
                   jax.ShapeDtypeStruct((1, D), F32), jax.ShapeDtypeStruct((1, D), F32)],
        scratch_shapes=[pltpu.VMEM((ts, D), BF16), pltpu.VMEM((ts, D), F32), pltpu.VMEM((ts, D), BF16),
                        pltpu.VMEM((ts, D), F32), pltpu.VMEM((GROUP, D), F32), pltpu.VMEM((ts, D), BF16),
                        pltpu.VMEM((ts, D), BF16), pltpu.VMEM((D, D), F32)],
        compiler_params=_params(),
    )(g2, m1, p, x1, lg, lb, wt_b, wtt_b, bfull, woutt_b, wint_b, gpost, gpre)


def _weight_grad_scatter(h, dp, ts, name, sides=()):
    S = h.shape[0]
    ts = min(ts, S)
    nt = S // ts
    panel = 2 * SHARD_IN
    last = N_CHIPS - 1

    def body(h_ref, dp_ref, parts_ref, acc_ref, half_ref, swap_ref, sum_ref, send_sems, recv_sems):
        s, t = pl.program_id(0), pl.program_id(1)
        x, y, c = _mesh_position()
        sibling = (x, y, 1 - c)

        def swap(k):
            return _remote(half_ref.at[k, 1], swap_ref.at[k], send_sems, recv_sems, k, sibling)

        def chip_sum(k):
            flip = last - k
            to = (x ^ (flip >> 1), y ^ (flip & 1), c)
            return _remote(sum_ref.at[k], parts_ref.at[1 + flip], send_sems, recv_sems, 3 + flip, to)

        own = pltpu.make_async_copy(half_ref.at[last, 0], parts_ref.at[0], recv_sems.at[7])
        to_sibling = _remote(half_ref.at[last, 1], parts_ref.at[1], send_sems, recv_sems, 3, sibling)

        @pl.when(t == 0)
        def _():
            acc_ref[...] = jnp.zeros_like(acc_ref)

        for k in range(last):
            @pl.when((s == k + 1) & (t == 0))
            def _():
                swap(k).wait_recv()
                sum_ref[k] = (half_ref[k, 0].astype(F32) + swap_ref[k].astype(F32)).astype(BF16)
                chip_sum(k).start()

        acc_ref[...] += _dot_t0(h_ref[...], dp_ref[...])

        @pl.when(t == nt - 1)
        def _():
            for core in range(2):
                @pl.when(c == core)
                def _():
                    half_ref[s, core] = acc_ref[:, 0:SHARD_IN].astype(BF16)
                    half_ref[s, 1 - core] = acc_ref[:, SHARD_IN:panel].astype(BF16)
            for k in range(last):
                @pl.when(s == k)
                def _():
                    swap(k).start()

        @pl.when((s == last) & (t == nt - 1))
        def _():
            own.start()
            to_sibling.start()
            own.wait()
            _remote(half_ref.at[last, 1], parts_ref.at[1], send_sems, recv_sems, 3, sibling).wait_recv()
            for k in range(last):
                chip_sum(k).wait_recv()
            to_sibling.wait_send()
            for k in range(last):
                swap(k).wait_send()
                chip_sum(k).wait_send()

    chip_panel = lambda s, t: (t, (2 * lax.axis_index("x") + lax.axis_index("y")) ^ (last - s))
    sems = pltpu.SemaphoreType.DMA((8,))
    return _call(
        body, name=name, grid=(N_CHIPS, nt),
        in_specs=[pl.BlockSpec((ts, D), lambda s, t: (t, 0)), pl.BlockSpec((ts, panel), chip_panel)],
        out_specs=[HBM_SPEC], out_shape=[jax.ShapeDtypeStruct((N_CHIPS + 1, D, SHARD_IN), BF16)],
        scratch_shapes=[pltpu.VMEM((D, panel), F32), pltpu.VMEM((N_CHIPS, 2, D, SHARD_IN), BF16),
                        pltpu.VMEM((last, D, SHARD_IN), BF16), pltpu.VMEM((last, D, SHARD_IN), BF16), sems, sems],
        args=(h, dp), sides=sides)


def _adamw(w, g, m, v):
    m = ADAM_B1 * m + (1.0 - ADAM_B1) * g
    v = ADAM_B2 * v + (1.0 - ADAM_B2) * (g * g)
    m_hat = m / (1.0 - ADAM_B1 ** ADAM_STEP)
    v_hat = v / (1.0 - ADAM_B2 ** ADAM_STEP)
    delta = -ADAM_LR * (m_hat / (jnp.sqrt(v_hat) + ADAM_EPS) + ADAM_WD * w)
    return delta, m, v


UPDATE_STEPS = 4


def _update(parts, weights, partials, row_of, small_weights, loss_parts):
    n_w, n_p, n_s = len(weights), len(partials), len(small_weights)

    def body(*refs):
        it = iter(refs)
        take = lambda n: [next(it) for _ in range(n)]
        p_refs, wmv_refs, sp_refs, (loss_ref,), sw_refs = take(n_w), take(3 * n_w), take(n_p), take(1), take(3 * n_s)
        outs, (loss_out,), s_outs = take(4 * n_w), take(1), take(4 * n_s)

        for i in range(n_w):
            acc = p_refs[i][0].astype(F32)
            for d in range(1, parts[i].shape[0]):
                acc = acc + p_refs[i][d].astype(F32)
            w_ref, m_ref, v_ref = wmv_refs[3 * i:3 * i + 3]
            g_out, d_out, m_out, v_out = outs[4 * i:4 * i + 4]
            g_out[...] = acc
            delta, m_new, v_new = _adamw(w_ref[...], acc, m_ref[...], v_ref[...])
            d_out[...] = delta
            m_out[...] = m_new
            v_out[...] = v_new

        @pl.when(pl.program_id(0) == 0)
        def _():
            def total(ref):
                acc = ref[0].astype(F32)
                for d in range(1, N_DEV):
                    acc = acc + ref[d].astype(F32)
                return acc

            lsum = jnp.sum(total(loss_ref), axis=-1, keepdims=True) * (0.5 / D)
            loss_out[...] = jnp.broadcast_to(lsum, loss_out.shape)
            for k in range(n_p):
                i, r0 = row_of[k]
                g = total(sp_refs[k])
                w_ref, m_ref, v_ref = sw_refs[3 * i:3 * i + 3]
                g_out, d_out, m_out, v_out = s_outs[4 * i:4 * i + 4]
                if g.ndim == 2:
                    at = (slice(r0, r0 + g.shape[0]), slice(None))
                else:
                    at = (slice(None),) * g.ndim
                delta, m_new, v_new = _adamw(w_ref[at], g, m_ref[at], v_ref[at])
                g_out[at] = g
                d_out[at] = delta
                m_out[at] = m_new
                v_out[at] = v_new

    part_specs, wmv_specs, out_specs, out_shape = [], [], [], []
    for p, (w, _, _) in zip(parts, weights):
        R, C = w.shape
        rb = R // UPDATE_STEPS
        blk = pl.BlockSpec((rb, C), lambda i: (i, 0))
        part_specs.append(pl.BlockSpec((p.shape[0], rb, C), lambda i: (0, i, 0)))
        wmv_specs += [blk] * 3
        out_specs += [blk] * 4
        out_shape += [jax.ShapeDtypeStruct((R, C), F32)] * 4
    small_in = list(partials) + [loss_parts] + [a for wmv in small_weights for a in wmv]
    small_shape = [jax.ShapeDtypeStruct((1, 128), F32)]
    for w, _, _ in small_weights:
        small_shape += [jax.ShapeDtypeStruct(w.shape, F32)] * 4
    flat = pl.pallas_call(
        body, name="update", grid=(UPDATE_STEPS,),
        in_specs=part_specs + wmv_specs + [_full(a.shape) for a in small_in],
        out_specs=out_specs + [_full(a.shape) for a in small_shape], out_shape=out_shape + small_shape,
        compiler_params=_params(),
    )(*parts, *[a for wmv in weights for a in wmv], *small_in)
    return [flat[4 * i:4 * i + 4] for i in range(n_w)], flat[4 * n_w], flat[4 * n_w + 1:]


def _pack_small_shard(cw, lg, lb):
    return (jnp.pad(cw, ((0, 13), (0, 64))) + jnp.pad(lg, ((8, 7), (0, 0))) + jnp.pad(lb, ((9, 6), (0, 0))))


def kernel(x, pre_norm, post_norm, even_w_in, even_conv_w, even_pool_w, even_pool_scale, even_w_out, odd_w_in, odd_ln_g, odd_ln_b, odd_w_s, odd_b_s, odd_w_out, loss_target, m_pre_norm, m_post_norm, m_even_w_in, m_even_conv_w, m_even_pool_w, m_even_pool_scale, m_even_w_out, m_odd_w_in, m_odd_ln_g, m_odd_ln_b, m_odd_w_s, m_odd_b_s, m_odd_w_out, v_pre_norm, v_post_norm, v_even_w_in, v_even_conv_w, v_even_pool_w, v_even_pool_scale, v_even_w_out, v_odd_w_in, v_odd_ln_g, v_odd_ln_b, v_odd_w_s, v_odd_b_s, v_odd_w_out):
    S = x.shape[1]
    xs = x.reshape(S, D)
    tgt = loss_target.reshape(S, D)

    gpre0, gpre1 = pre_norm[0:1], pre_norm[1:2]
    gpost0, gpost1 = post_norm[0:1], post_norm[1:2]
    small_shard = _pack_small_shard(even_conv_w[0], odd_ln_g, odd_ln_b)
    h0, wie_g, small_g, woe_g, wio_b, woo_b = _prenorm_gather(xs, gpre0, even_w_in[0], small_shard, even_w_out[0],
                                                              odd_w_in[0], odd_w_out[0], TS_PRENORM)
    wie = jnp.transpose(wie_g, (1, 0, 2)).reshape(D, PROJ)
    wie_t = jnp.transpose(wie_g, (0, 2, 1)).reshape(PROJ, D)
    woe = woe_g.reshape(D, D)
    woe_t = woe.T
    conv_w = jnp.transpose(small_g[:, 0:3, 0:64], (1, 0, 2)).reshape(3, A_W)
    ln_g = small_g[:, 8, :].reshape(1, D)
    ln_b = small_g[:, 9, :].reshape(1, D)

    pool_w_b = even_pool_w[0].astype(BF16)
    pool_wt_b = jnp.swapaxes(even_pool_w[0], 1, 2).astype(BF16)
    ws_tril = jnp.tril(odd_w_s[0])
    ws_b = ws_tril.astype(BF16)
    wst_b = jnp.swapaxes(ws_tril, 1, 2).astype(BF16)
    b_full = jnp.broadcast_to(odd_b_s[0][:, :, None], (N_HEADS, GROUP, GROUP))

    p0, m0, x1, h1, wio_g, woo_g = _even_forward_fused(h0, xs, wie, conv_w, pool_w_b, even_pool_scale, woe, gpost0,
                                                       gpre1, TS_FWD, sides=[(_Gather, [wio_b, woo_b])])
    wio = jnp.transpose(wio_g, (1, 0, 2)).reshape(D, PROJ)
    wio_t = jnp.transpose(wio_g, (0, 2, 1)).reshape(PROJ, D)
    woo = woo_g.reshape(D, D)
    woo_t = woo.T
    p1, m1, g2, loss_vec = _odd_forward(h1, x1, tgt, wio, ln_g, ln_b, ws_b, b_full, woo, gpost1, TS_FWD)

    dp1, g1, dwoo, dws, dbs, dlg, dlb, dgpost1, dgpre1 = _odd_backward(
        g2, m1, p1, x1, ln_g, ln_b, ws_b, wst_b, b_full, woo_t, wio_t, gpost1, gpre1, TS_BWD)
    by_owner = lambda dwo: dwo.reshape(N_DEV, SHARD_OUT, D)
    (dwio_parts, dwoo_parts, dlg_parts, dlb_parts,
     dws_g, dbs_g, gpost1_g, gpre1_g, loss_g) = _weight_grad_scatter(
        h1, dp1, TS_WGRAD, "odd_weight_grad",
        sides=[(_Scatter, [by_owner(dwoo), dlg.reshape(N_DEV, 1, 128), dlb.reshape(N_DEV, 1, 128)]),
               (_Gather, [dws.astype(BF16), dbs, dgpost1, dgpre1, loss_vec])])
    dp0, gx, dwoe, dcw, dpw, dps, dgpost0, dgpre0 = _even_backward_fused(
        g1, m0, p0, xs, conv_w, pool_w_b, pool_wt_b, even_pool_scale, woe_t, wie_t, gpost0, gpre0, TS_FWD)
    dcw_by_owner = jnp.transpose(dcw[0:3].reshape(3, N_DEV, 64), (1, 0, 2))
    dwie_parts, dwoe_parts, dcw_parts, dpw_g, dps_g, gpost0_g, gpre0_g = _weight_grad_scatter(
        h0, dp0, TS_WGRAD, "even_weight_grad",
        sides=[(_Scatter, [by_owner(dwoe), dcw_by_owner]), (_Gather, [dpw.astype(BF16), dps, dgpost0, dgpre0])])

    large_parts = [dwie_parts, dwio_parts, dwoe_parts, dwoo_parts]
    large = [(even_w_in[0], m_even_w_in[0], v_even_w_in[0]), (odd_w_in[0], m_odd_w_in[0], v_odd_w_in[0]),
             (even_w_out[0], m_even_w_out[0], v_even_w_out[0]), (odd_w_out[0], m_odd_w_out[0], v_odd_w_out[0])]
    partials = [gpre0_g, gpre1_g, gpost0_g, gpost1_g, dpw_g, dps_g, dws_g, dbs_g, dcw_parts, dlg_parts, dlb_parts]
    row_of = [(0, 0), (0, 1), (1, 0), (1, 1), (2, 0), (3, 0), (4, 0), (5, 0), (6, 0), (7, 0), (8, 0)]
    weights = [(pre_norm, m_pre_norm, v_pre_norm), (post_norm, m_post_norm, v_post_norm),
               (even_pool_w[0], m_even_pool_w[0], v_even_pool_w[0]),
               (even_pool_scale, m_even_pool_scale, v_even_pool_scale),
               (odd_w_s[0], m_odd_w_s[0], v_odd_w_s[0]), (odd_b_s[0], m_odd_b_s[0], v_odd_b_s[0]),
               (even_conv_w[0], m_even_conv_w[0], v_even_conv_w[0]),
               (odd_ln_g, m_odd_ln_g, v_odd_ln_g), (odd_ln_b, m_odd_ln_b, v_odd_ln_b)]
    (wie_out, wio_out, woe_out, woo_out), loss_row, small = _update(large_parts, large, partials, row_of, weights,
                                                                    loss_g)
    loss = loss_row[0, 0]

    def leaves(k, big):
        pre, post, pw, psc, ws, bs, cw, lg, lb = [small[4 * i + k] for i in range(len(weights))]
        wie_k, woe_k, wio_k, woo_k = big
        return [pre, post, wie_k[None], cw[None], pw[None], psc, woe_k[None], wio_k[None], lg, lb, ws[None], bs[None],
                woo_k[None]]

    outs = [loss, gx.reshape(1, S, D)]
    for k in range(4):
        outs += leaves(k, (wie_out[k], woe_out[k], wio_out[k], woo_out[k]))
    return tuple(outs)
```

```python
import jax
import jax.numpy as jnp
from jax import lax
from jax.experimental import pallas as pl
from jax.experimental.pallas import tpu as pltpu

F32 = jnp.float32
BF16 = jnp.bfloat16
MESH = pl.DeviceIdType.MESH

D = 1024
EPS = 1e-6
A_W = 512
B_W = 512
POOL_WINDOWS = (2, 4, 8, 16)
GROUP = 128
N_GROUPS = 4
N_HEADS = 8
HALO = 16
PROJ = 3072
N_DEV = 8
N_CHIPS = 4
SHARD_IN = PROJ // N_DEV
SHARD_OUT = D // N_DEV

ADAM_LR = 0.001
ADAM_B1 = 0.9
ADAM_B2 = 0.999
ADAM_EPS = 1e-08
ADAM_WD = 0.01
ADAM_STEP = 10

VMEM_LIMIT = 54 * 1024 * 1024

TS_PRENORM = 1024
TS_FWD = 512
TS_BWD = 256
TS_WGRAD = 2048
SUB_ROWS = 256


def _params(n_grid=1, vmem=VMEM_LIMIT):
    return pltpu.CompilerParams(dimension_semantics=("arbitrary",) * n_grid, vmem_limit_bytes=vmem)


def _full(shape):
    return pl.BlockSpec(shape, lambda *_: (0,) * len(shape))


def _resident(shape):
    return pl.BlockSpec(shape, lambda *_: (0,) * len(shape), pipeline_mode=pl.Buffered(1))


def _sigmoid(z):
    return jax.nn.sigmoid(z)


def _dsilu(z, s):
    return s * (1.0 + z * (1.0 - s))


def _dot(a, b):
    return jnp.dot(a, b, preferred_element_type=F32)


def _dot_t0(a, b):
    return lax.dot_general(a, b, (((0,), (0,)), ((), ())), preferred_element_type=F32)


def _dot_t1(a, b):
    return lax.dot_general(a, b, (((1,), (1,)), ((), ())), preferred_element_type=F32)


def _rms(x):
    return lax.rsqrt(jnp.mean(x * x, axis=-1, keepdims=True) + EPS)


def _shift_down(a, k):
    return pltpu.roll(a, k, 0)


def _shift_up(a, k):
    return pltpu.roll(a, a.shape[0] - k, 0)


def _mesh_position():
    return lax.axis_index("x"), lax.axis_index("y"), lax.axis_index("c")


def _index(pos):
    return 4 * pos[0] + 2 * pos[1] + pos[2]


def _peer(pos, d):
    x, y, c = pos
    return (1 - x if d & 4 else x, 1 - y if d & 2 else y, 1 - c if d & 1 else c)


def _remote(src, dst, send_sems, recv_sems, k, to):
    return pltpu.make_async_remote_copy(src_ref=src, dst_ref=dst, send_sem=send_sems.at[k], recv_sem=recv_sems.at[k],
                                        device_id=to, device_id_type=MESH)


class _Gather:
    def __init__(self, srcs, dsts, send_sems, recv_sems):
        x, y, c = _mesh_position()
        n = len(srcs)
        me, sibling = (x, y, c), (x, y, 1 - c)
        chips = [(1 - x, y), (x, 1 - y), (1 - x, 1 - y)]

        def copy(k, t, block, to, src=None):
            slot = dsts[t].at[_index(block)]
            return _remote(slot if src is None else src, slot, send_sems, recv_sems, k * n + t, to)

        tensors = range(n)
        self.local = [pltpu.make_async_copy(srcs[t], dsts[t].at[_index(me)], recv_sems.at[7 * n + t]) for t in tensors]
        self.first = [copy(1 + j, t, me, chip + (c,), srcs[t]) for t in tensors for j, chip in enumerate(chips)]
        self.first += [copy(0, t, me, sibling, srcs[t]) for t in tensors]
        self.ici_in = [copy(1 + j, t, chip + (c,), me) for t in tensors for j, chip in enumerate(chips)]
        self.passed = [copy(4 + j, t, chip + (c,), sibling) for t in tensors for j, chip in enumerate(chips)]
        self.d2d_in = [copy(0, t, sibling, me) for t in tensors]
        self.d2d_in += [copy(4 + j, t, chip + (1 - c,), me) for t in tensors for j, chip in enumerate(chips)]

    def start(self):
        for cp in self.local + self.first:
            cp.start()

    def middle(self):
        for landed, onward in zip(self.ici_in, self.passed):
            landed.wait_recv()
            onward.start()

    def finish(self):
        for cp in self.d2d_in:
            cp.wait_recv()
        for cp in self.first + self.passed:
            cp.wait_send()
        for cp in self.local:
            cp.wait()


class _Scatter:
    def __init__(self, srcs, dsts, send_sems, recv_sems):
        pos = _mesh_position()
        n = len(srcs)
        self.local = [pltpu.make_async_copy(srcs[t].at[_index(pos)], dsts[t].at[0], recv_sems.at[t]) for t in range(n)]
        self.remote = []
        for d in range(1, N_DEV):
            to = _peer(pos, d)
            self.remote += [_remote(srcs[t].at[_index(to)], dsts[t].at[d], send_sems, recv_sems, d * n + t, to)
                            for t in range(n)]

    def start(self):
        for cp in self.local + self.remote:
            cp.start()

    def middle(self):
        pass

    def finish(self):
        for cp in self.local:
            cp.wait()
        for cp in self.remote:
            cp.wait_recv()
        for cp in self.remote:
            cp.wait_send()


HBM_SPEC = pl.BlockSpec(memory_space=pltpu.HBM)
MIDDLE_STEPS_BEFORE_END = 4


def _landing_shape(exchange, src):
    return (N_DEV,) + src.shape if exchange is _Gather else src.shape


def _call(body, *, name, grid, in_specs, out_specs, out_shape, scratch_shapes, args, sides=()):
    params = _params(len(grid))
    if not sides:
        return pl.pallas_call(body, name=name, grid=grid, in_specs=in_specs, out_specs=out_specs, out_shape=out_shape,
                              scratch_shapes=scratch_shapes, compiler_params=params)(*args)
    n_in, n_out, n_scratch = len(in_specs), len(out_specs), len(scratch_shapes)
    counts = [len(srcs) for _, srcs in sides]
    ns = sum(counts)
    total = 1
    for g in grid:
        total *= g
    side_args, side_shapes, side_sems = [], [], []
    for exchange, srcs in sides:
        side_args += list(srcs)
        side_shapes += [jax.ShapeDtypeStruct(_landing_shape(exchange, s), s.dtype) for s in srcs]
        side_sems += [pltpu.SemaphoreType.DMA((N_DEV * len(srcs),))] * 2

    def wrapped(*refs):
        ins, side_in = refs[:n_in], refs[n_in:n_in + ns]
        outs = refs[n_in + ns:n_in + ns + n_out]
        side_out = refs[n_in + ns + n_out:n_in + 2 * ns + n_out]
        rest = refs[n_in + 2 * ns + n_out:]
        scratch, sems = rest[:n_scratch], rest[n_scratch:]
        step = pl.program_id(0)
        for axis in range(1, len(grid)):
            step = step * grid[axis] + pl.program_id(axis)

        def exchanges():
            built, at = [], 0
            for k, (exchange, _) in enumerate(sides):
                built.append(exchange(side_in[at:at + counts[k]], side_out[at:at + counts[k]],
                                      sems[2 * k], sems[2 * k + 1]))
                at += counts[k]
            return built

        @pl.when(step == 0)
        def _():
            for ex in exchanges():
                ex.start()

        @pl.when(step == max(total - MIDDLE_STEPS_BEFORE_END, 0))
        def _():
            for ex in exchanges():
                ex.middle()

        body(*ins, *outs, *scratch)

        @pl.when(step == total - 1)
        def _():
            for ex in exchanges():
                ex.finish()

    return pl.pallas_call(
        wrapped, name=name, grid=grid, in_specs=list(in_specs) + [HBM_SPEC] * ns,
        out_specs=list(out_specs) + [HBM_SPEC] * ns, out_shape=list(out_shape) + side_shapes,
        scratch_shapes=list(scratch_shapes) + side_sems, compiler_params=params)(*args, *side_args)


def _prenorm_gather(x, g, wie, small, woe, wio, woo, pool_w, w_s, b_s, ts):
    S = x.shape[0]
    nt = S // ts

    def body(x_ref, g_ref, wie_ref, sm_ref, woe_ref, wio_ref, woo_ref, pw_ref, ws_ref, bs_ref,
             h_ref, wie_out, sm_out, woe_out, wio_b, woo_b, pw_b, pwt_b, ws_b, wst_b, b_full,
             cast_ref, small_ref, cast2_ref, send_sems, recv_sems):
        step = pl.program_id(0)
        gather = lambda: _Gather([cast_ref, small_ref, cast2_ref], [wie_out, sm_out, woe_out], send_sems, recv_sems)

        @pl.when(step == 0)
        def _():
            cast_ref[...] = wie_ref[...].astype(BF16)
            small_ref[...] = sm_ref[...]
            cast2_ref[...] = woe_ref[...].astype(BF16)
            gather().start()
            wio_b[...] = wio_ref[...].astype(BF16)
            woo_b[...] = woo_ref[...].astype(BF16)
            for gi in range(N_GROUPS):
                w = pw_ref[gi]
                pw_b[gi] = w.astype(BF16)
                pwt_b[gi] = w.T.astype(BF16)
            tril = (lax.broadcasted_iota(jnp.int32, (GROUP, GROUP), 0)
                    >= lax.broadcasted_iota(jnp.int32, (GROUP, GROUP), 1))
            for hd in range(N_HEADS):
                w = jnp.where(tril, ws_ref[hd], 0.0)
                ws_b[hd] = w.astype(BF16)
                wst_b[hd] = w.T.astype(BF16)
                b_full[hd] = jnp.broadcast_to(bs_ref[hd:hd + 1, :], (GROUP, GROUP)).T

        xv = x_ref[...]
        h_ref[...] = ((xv * _rms(xv)) * g_ref[...]).astype(BF16)

        @pl.when(step == nt - 1)
        def _():
            exchange = gather()
            exchange.middle()
            exchange.finish()

    tile = pl.BlockSpec((ts, D), lambda i: (i, 0))
    sems = pltpu.SemaphoreType.DMA((N_DEV * 3,))
    shards = [wie, small, woe, wio, woo, pool_w, w_s, b_s]
    prepared = [(wio.shape, BF16), (woo.shape, BF16), (pool_w.shape, BF16), (pool_w.shape, BF16), (w_s.shape, BF16),
                (w_s.shape, BF16), (w_s.shape, F32)]
    return pl.pallas_call(
        body, name="prenorm_gather", grid=(nt,),
        in_specs=[tile, _full((1, D))] + [_full(a.shape) for a in shards],
        out_specs=[tile, HBM_SPEC, HBM_SPEC, HBM_SPEC] + [_full(shape) for shape, _ in prepared],
        out_shape=[jax.ShapeDtypeStruct((S, D), BF16), jax.ShapeDtypeStruct((N_DEV,) + wie.shape, BF16),
                   jax.ShapeDtypeStruct((N_DEV,) + small.shape, F32), jax.ShapeDtypeStruct((N_DEV,) + woe.shape, BF16)]
        + [jax.ShapeDtypeStruct(shape, dtype) for shape, dtype in prepared],
        scratch_shapes=[pltpu.VMEM(wie.shape, BF16), pltpu.VMEM(small.shape, F32), pltpu.VMEM(woe.shape, BF16),
                        sems, sems],
        compiler_params=_params(),
    )(x, g, wie, small, woe, wio, woo, pool_w, w_s, b_s)


def _even_mixer(proj, halo, row0, cw_ref, pw_ref):
    xa, gb, gc, za, xp, zp = proj
    nr = xa.shape[0]
    ha = gc * xa
    ha_ext = jnp.concatenate([halo[0], ha], axis=0)
    ha_m1 = _shift_down(ha_ext, 1)[HALO:]
    ha_m2 = _shift_down(ha_ext, 2)[HALO:]
    conv = cw_ref[2:3, :] * ha + cw_ref[1:2, :] * ha_m1 + cw_ref[0:1, :] * ha_m2
    sig_a = _sigmoid(za)
    silu_a = za * sig_a

    xp_ext = jnp.concatenate([halo[1], xp], axis=0)
    pos = row0 + lax.broadcasted_iota(jnp.int32, (nr, 1), 0)
    pooled, inv_cnt, mixed = [], [], []
    for g, w in enumerate(POOL_WINDOWS):
        cols = slice(g * GROUP, (g + 1) * GROUP)
        s = xp_ext[:, cols]
        k = 1
        while k < w:
            s = s + _shift_down(s, k)
            k *= 2
        inv = 1.0 / jnp.minimum(pos + 1, w).astype(F32)
        pg = s[HALO:] * inv - xp[:, cols]
        pooled.append(pg)
        inv_cnt.append(inv)
        mixed.append(_dot(pg.astype(BF16), pw_ref[g]))
    mixed = jnp.concatenate(mixed, axis=1)
    sig_b = _sigmoid(zp)
    silu_b = zp * sig_b
    return dict(xa=xa, gb=gb, gc=gc, za=za, xp=xp, zp=zp, ha=ha, ha_m1=ha_m1, ha_m2=ha_m2, conv=conv,
                sig_a=sig_a, silu_a=silu_a, pooled=pooled, inv_cnt=inv_cnt, mixed=mixed, sig_b=sig_b,
                silu_b=silu_b)


def _columns(a):
    return [a[:, k * A_W:(k + 1) * A_W].astype(F32) for k in range(6)]


def _even_forward_fused(h, x, win_b, cw, pw_b, ps, wout_b, gpost, gpre_next, ts, sides=()):
    S = x.shape[0]

    def body(h_ref, x_ref, wi_ref, cw_ref, pw_ref, ps_ref, wo_ref, gp_ref, gn_ref,
             p_ref, m_ref, x1_ref, h1_ref, mix_ref, carry_ref):
        i = pl.program_id(0)

        @pl.when(i == 0)
        def _():
            carry_ref[...] = jnp.zeros_like(carry_ref)

        for r0 in range(0, ts, SUB_ROWS):
            rows = slice(r0, r0 + SUB_ROWS)
            proj = _dot(h_ref[rows, :], wi_ref[...])
            p_ref[rows, :] = proj.astype(BF16)
            fw = _even_mixer(_columns(proj), (carry_ref[:, 0:A_W], carry_ref[:, A_W:D]), i * ts + r0, cw_ref, pw_ref)
            carry_ref[:, 0:A_W] = fw["ha"][SUB_ROWS - HALO:]
            carry_ref[:, A_W:D] = fw["xp"][SUB_ROWS - HALO:]
            mix_ref[rows, 0:A_W] = (fw["gb"] * fw["conv"] * fw["silu_a"]).astype(BF16)
            mix_ref[rows, A_W:D] = (fw["mixed"] * ps_ref[...] * fw["silu_b"]).astype(BF16)
            m = _dot(mix_ref[rows, :], wo_ref[...])
            m_ref[rows, :] = m.astype(BF16)
            x1 = x_ref[rows, :] + (m * _rms(m)) * gp_ref[...]
            x1_ref[rows, :] = x1
            h1_ref[rows, :] = ((x1 * _rms(x1)) * gn_ref[...]).astype(BF16)

    tile = pl.BlockSpec((ts, D), lambda i: (i, 0))
    return _call(
        body, name="even_forward", grid=(S // ts,),
        in_specs=[tile, tile, _resident((D, PROJ)), _full((3, A_W)), _full((N_GROUPS, GROUP, GROUP)), _full((1, B_W)),
                  _resident((D, D)), _full((1, D)), _full((1, D))],
        out_specs=[pl.BlockSpec((ts, PROJ), lambda i: (i, 0)), tile, tile, tile],
        out_shape=[jax.ShapeDtypeStruct((S, PROJ), BF16), jax.ShapeDtypeStruct((S, D), BF16),
                   jax.ShapeDtypeStruct((S, D), F32), jax.ShapeDtypeStruct((S, D), BF16)],
        scratch_shapes=[pltpu.VMEM((ts, D), BF16), pltpu.VMEM((HALO, D), F32)],
        args=(h, x, win_b, cw, pw_b, ps, wout_b, gpost, gpre_next), sides=sides)


def _even_backward_fused(g1, m0, p, x, cw, pw_b, pwt_b, ps, woutt_b, wint_b, gpost, gpre, ts):
    S = g1.shape[0]
    nt = S // ts

    def body(g_ref, m_ref, p_ref, halo_ref, x_ref, cw_ref, pw_ref, pwt_ref, ps_ref, wot_ref, wit_ref, gp_ref, gn_ref,
             dp_ref, dx_ref, dwob_ref, dcw_ref, dpw_ref, dps_ref, dgp_ref, dgn_ref,
             carry_ref, mix_ref, dm_ref, dwo_ref):
        step = pl.program_id(0)
        i = nt - 1 - step

        @pl.when(step == 0)
        def _():
            carry_ref[...] = jnp.zeros_like(carry_ref)
            dwo_ref[...] = jnp.zeros_like(dwo_ref)
            dcw_ref[...] = jnp.zeros_like(dcw_ref)
            dpw_ref[...] = jnp.zeros_like(dpw_ref)
            dps_ref[...] = jnp.zeros_like(dps_ref)
            dgp_ref[...] = jnp.zeros_like(dgp_ref)
            dgn_ref[...] = jnp.zeros_like(dgn_ref)

        ps_v = ps_ref[...]
        sums = [None] * 6
        add = lambda k, part: sums.__setitem__(k, part if sums[k] is None else sums[k] + part)
        for r0 in range(ts - SUB_ROWS, -1, -SUB_ROWS):
            rows = slice(r0, r0 + SUB_ROWS)
            g = g_ref[rows, :]
            m = m_ref[rows, :].astype(F32)
            q = _rms(m)
            n = m * q
            add(0, jnp.sum(g * n, axis=0, keepdims=True))
            dn = g * gp_ref[...]
            dm = q * (dn - n * jnp.mean(dn * n, axis=-1, keepdims=True))
            dm_ref[rows, :] = dm.astype(BF16)
            dmix = _dot(dm_ref[rows, :], wot_ref[...])
            dya = dmix[:, 0:A_W]
            dyb = dmix[:, A_W:D]

            if r0 == 0:
                before = _columns(halo_ref[...])
                keep = jnp.where(i == 0, 0.0, 1.0).astype(F32)
                halo = (before[2] * before[0] * keep, before[4] * keep)
            else:
                before = _columns(p_ref[r0 - HALO:r0, :])
                halo = (before[2] * before[0], before[4])
            fw = _even_mixer(_columns(p_ref[rows, :]), halo, i * ts + r0, cw_ref, pw_ref)
            mix_ref[rows, 0:A_W] = (fw["gb"] * fw["conv"] * fw["silu_a"]).astype(BF16)
            mix_ref[rows, A_W:D] = (fw["mixed"] * ps_v * fw["silu_b"]).astype(BF16)

            t = dya * fw["gb"]
            dconv = t * fw["silu_a"]
            dgb = dya * fw["conv"] * fw["silu_a"]
            dza = t * fw["conv"] * _dsilu(fw["za"], fw["sig_a"])
            add(2, jnp.sum(dconv * fw["ha"], axis=0, keepdims=True))
            add(3, jnp.sum(dconv * fw["ha_m1"], axis=0, keepdims=True))
            add(4, jnp.sum(dconv * fw["ha_m2"], axis=0, keepdims=True))
            dconv_ext = jnp.concatenate([dconv, carry_ref[:, 0:A_W]], axis=0)
            dha = (cw_ref[2:3, :] * dconv + cw_ref[1:2, :] * _shift_up(dconv_ext, 1)[:SUB_ROWS]
                   + cw_ref[0:1, :] * _shift_up(dconv_ext, 2)[:SUB_ROWS])
            dgc = dha * fw["xa"]
            dxa = dha * fw["gc"]

            u = dyb * fw["mixed"]
            add(1, jnp.sum(u * fw["silu_b"], axis=0, keepdims=True))
            dzp = u * ps_v * _dsilu(fw["zp"], fw["sig_b"])
            dmixed = (dyb * ps_v * fw["silu_b"]).astype(BF16)
            dxp, e_first = [], []
            for gi, w in enumerate(POOL_WINDOWS):
                cols = slice(gi * GROUP, (gi + 1) * GROUP)
                dmg = dmixed[:, cols]
                dpooled = _dot(dmg, pwt_ref[gi])
                dpw_ref[gi] += _dot_t0(fw["pooled"][gi].astype(BF16), dmg)
                e = dpooled * fw["inv_cnt"][gi]
                e_first.append(e[0:HALO])
                s = jnp.concatenate([e, carry_ref[:, A_W + gi * GROUP:A_W + (gi + 1) * GROUP]], axis=0)
                k = 1
                while k < w:
                    s = s + _shift_up(s, k)
                    k *= 2
                dxp.append(s[:SUB_ROWS] - dpooled)
            carry_ref[:, 0:A_W] = dconv[0:HALO]
            carry_ref[:, A_W:D] = jnp.concatenate(e_first, axis=1)

            dp_ref[rows, 0:512] = dxa.astype(BF16)
            dp_ref[rows, 512:1024] = dgb.astype(BF16)
            dp_ref[rows, 1024:1536] = dgc.astype(BF16)
            dp_ref[rows, 1536:2048] = dza.astype(BF16)
            dp_ref[rows, 2048:2560] = jnp.concatenate(dxp, axis=1).astype(BF16)
            dp_ref[rows, 2560:3072] = dzp.astype(BF16)

            dh = _dot(dp_ref[rows, :], wit_ref[...])
            xv = x_ref[rows, :]
            r = _rms(xv)
            xn = xv * r
            add(5, jnp.sum(dh * xn, axis=0, keepdims=True))
            dxn = dh * gn_ref[...]
            dx_ref[rows, :] = g + r * (dxn - xn * jnp.mean(dxn * xn, axis=-1, keepdims=True))

        dwo_ref[...] += _dot_t0(mix_ref[...], dm_ref[...])
        dgp_ref[...] += sums[0]
        dps_ref[...] += sums[1]
        dcw_ref[2:3, :] += sums[2]
        dcw_ref[1:2, :] += sums[3]
        dcw_ref[0:1, :] += sums[4]
        dgn_ref[...] += sums[5]

        @pl.when(step == nt - 1)
        def _():
            dwob_ref[...] = dwo_ref[...].astype(BF16)

    rev = lambda s: (nt - 1 - s, 0)
    tile = pl.BlockSpec((ts, D), rev)
    wide = pl.BlockSpec((ts, PROJ), rev)
    bpt = ts // HALO
    halo_map = lambda s: (jnp.maximum((nt - 1 - s) * bpt - 1, 0), 0)
    groups = _full((N_GROUPS, GROUP, GROUP))
    vec = _full((1, D))
    return pl.pallas_call(
        body, name="even_backward", grid=(nt,),
        in_specs=[tile, tile, wide, pl.BlockSpec((HALO, PROJ), halo_map), tile, _full((3, A_W)), groups, groups,
                  _full((1, B_W)), _resident((D, D)), _resident((PROJ, D)), vec, vec],
        out_specs=[wide, tile, _full((D, D)), _full((8, A_W)), groups, _full((1, B_W)), vec, vec],
        out_shape=[jax.ShapeDtypeStruct((S, PROJ), BF16), jax.ShapeDtypeStruct((S, D), F32),
                   jax.ShapeDtypeStruct((D, D), BF16), jax.ShapeDtypeStruct((8, A_W), F32),
                   jax.ShapeDtypeStruct((N_GROUPS, GROUP, GROUP), F32), jax.ShapeDtypeStruct((1, B_W), F32),
                   jax.ShapeDtypeStruct((1, D), F32), jax.ShapeDtypeStruct((1, D), F32)],
        scratch_shapes=[pltpu.VMEM((HALO, D), F32), pltpu.VMEM((ts, D), BF16), pltpu.VMEM((ts, D), BF16),
                        pltpu.VMEM((D, D), F32)],
        compiler_params=_params(),
    )(g1, m0, p, p, x, cw, pw_b, pwt_b, ps, woutt_b, wint_b, gpost, gpre)


def _gate_matmuls(w_ref, in_ref, out_ref, chunks, bias_ref=None):
    for h in range(N_HEADS):
        cols = slice(h * GROUP, (h + 1) * GROUP)
        wide = jnp.concatenate([in_ref[chunk, cols] for chunk in chunks], axis=1)
        res = _dot(w_ref[h], wide)
        for k, chunk in enumerate(chunks):
            part = res[:, k * GROUP:(k + 1) * GROUP]
            out_ref[chunk, cols] = part if bias_ref is None else part + bias_ref[h]


def _odd_forward_parts(v, lg_ref, lb_ref, wt_ref, bfull_ref, vln_ref, sv_ref, r0, nr):
    rows = slice(r0, r0 + nr)
    mu = jnp.mean(v, axis=-1, keepdims=True)
    vc = v - mu
    rstd = lax.rsqrt(jnp.mean(vc * vc, axis=-1, keepdims=True) + EPS)
    vh = vc * rstd
    vln_ref[rows, :] = (vh * lg_ref[...] + lb_ref[...]).astype(BF16)
    _gate_matmuls(wt_ref, vln_ref, sv_ref, [slice(r0 + k * GROUP, r0 + (k + 1) * GROUP) for k in range(nr // GROUP)],
                  bfull_ref)
    return vh, rstd


def _odd_forward(h, x1, tgt, win_b, lg, lb, wt_b, bfull, wout_b, gpost, ts):
    S = x1.shape[0]

    def body(h_ref, x_ref, t_ref, wi_ref, lg_ref, lb_ref, wt_ref, bfull_ref, wo_ref, gp_ref,
             p_ref, m_ref, g_ref, loss_ref, vln_ref, sv_ref, y_ref):
        @pl.when(pl.program_id(0) == 0)
        def _():
            loss_ref[...] = jnp.zeros_like(loss_ref)

        loss = None
        for r0 in range(0, ts, SUB_ROWS):
            rows = slice(r0, r0 + SUB_ROWS)
            proj = _dot(h_ref[rows, :], wi_ref[...])
            p_ref[rows, :] = proj.astype(BF16)
            u, v, z = proj[:, 0:D], proj[:, D:2 * D], proj[:, 2 * D:3 * D]
            _odd_forward_parts(v, lg_ref, lb_ref, wt_ref, bfull_ref, vln_ref, sv_ref, r0, SUB_ROWS)
            y_ref[rows, :] = (u * sv_ref[rows, :] * (z * _sigmoid(z))).astype(BF16)
            m = _dot(y_ref[rows, :], wo_ref[...])
            m_ref[rows, :] = m.astype(BF16)
            x2 = x_ref[rows, :] + (m * _rms(m)) * gp_ref[...]
            err = x2 - t_ref[rows, :]
            g_ref[rows, :] = err * (1.0 / D)
            part = jnp.sum(err * err, axis=0, keepdims=True)
            loss = part if loss is None else loss + part
        loss_ref[...] += loss

    tile = pl.BlockSpec((ts, D), lambda i: (i, 0))
    small = _full((N_HEADS, GROUP, GROUP))
    return pl.pallas_call(
        body, name="odd_forward", grid=(S // ts,),
        in_specs=[pl.BlockSpec((ts, D), lambda i: (i, 0)), tile, tile, _full((D, PROJ)), _full((1, D)), _full((1, D)),
                  small, small, _full((D, D)), _full((1, D))],
        out_specs=[pl.BlockSpec((ts, PROJ), lambda i: (i, 0)), tile, tile, _full((1, D))],
        out_shape=[jax.ShapeDtypeStruct((S, PROJ), BF16), jax.ShapeDtypeStruct((S, D), BF16),
                   jax.ShapeDtypeStruct((S, D), F32), jax.ShapeDtypeStruct((1, D), F32)],
        scratch_shapes=[pltpu.VMEM((ts, D), BF16), pltpu.VMEM((ts, D), F32), pltpu.VMEM((ts, D), BF16)],
        compiler_params=_params(),
    )(h, x1, tgt, win_b, lg, lb, wt_b, bfull, wout_b, gpost)


def _odd_backward(g2, m1, p, x1, lg, lb, wt_b, wtt_b, bfull, woutt_b, wint_b, gpost, gpre, ts):
    S = g2.shape[0]
    nt = S // ts
    sub = ts // 2

    def body(g_ref, m_ref, p_ref, x_ref, lg_ref, lb_ref, wt_ref, wtt_ref, bfull_ref, wot_ref, wit_ref, gp_ref, gn_ref,
             dp_ref, dx_ref, dwob_ref, dws_ref, dbs_ref, dlg_ref, dlb_ref, dgp_ref, dgn_ref,
             vln_ref, sv_ref, dsvb_ref, dvln_ref, dsum_ref, y_ref, dm_ref, dwo_ref):
        step = pl.program_id(0)

        @pl.when(step == 0)
        def _():
            dwo_ref[...] = jnp.zeros_like(dwo_ref)
            dws_ref[...] = jnp.zeros_like(dws_ref)
            dsum_ref[...] = jnp.zeros_like(dsum_ref)
            dlg_ref[...] = jnp.zeros_like(dlg_ref)
            dlb_ref[...] = jnp.zeros_like(dlb_ref)
            dgp_ref[...] = jnp.zeros_like(dgp_ref)
            dgn_ref[...] = jnp.zeros_like(dgn_ref)

        tril = (lax.broadcasted_iota(jnp.int32, (GROUP, GROUP), 0)
                >= lax.broadcasted_iota(jnp.int32, (GROUP, GROUP), 1))
        sums = [None] * 4
        add = lambda k, part: sums.__setitem__(k, part if sums[k] is None else sums[k] + part)

        def post_norm(c):
            g = g_ref[c["rows"], :]
            m = m_ref[c["rows"], :].astype(F32)
            q = _rms(m)
            n = m * q
            add(0, jnp.sum(g * n, axis=0, keepdims=True))
            dn = g * gp_ref[...]
            dm = q * (dn - n * jnp.mean(dn * n, axis=-1, keepdims=True))
            dm_ref[c["rows"], :] = dm.astype(BF16)

        def out_projection(c):
            c["dy"] = _dot(dm_ref[c["rows"], :], wot_ref[...])

        def layernorm(c):
            v = p_ref[c["rows"], D:2 * D].astype(F32)
            mu = jnp.mean(v, axis=-1, keepdims=True)
            vc = v - mu
            c["rstd"] = lax.rsqrt(jnp.mean(vc * vc, axis=-1, keepdims=True) + EPS)
            c["vh"] = vc * c["rstd"]
            vln_ref[c["rows"], :] = (c["vh"] * lg_ref[...] + lb_ref[...]).astype(BF16)

        def gate_matmuls(c):
            _gate_matmuls(wt_ref, vln_ref, sv_ref, c["chunks"], bfull_ref)

        def gating(c):
            rows = c["rows"]
            u = p_ref[rows, 0:D].astype(F32)
            z = p_ref[rows, 2 * D:3 * D].astype(F32)
            sv = sv_ref[rows, :]
            sig = _sigmoid(z)
            sz = z * sig
            y_ref[rows, :] = (u * sv * sz).astype(BF16)
            dy = c.pop("dy")
            t = dy * sz
            dsv = t * u
            dsvb_ref[rows, :] = dsv.astype(BF16)
            for k in range(sub // GROUP):
                dsum_ref[...] += dsv[k * GROUP:(k + 1) * GROUP]
            dp_ref[rows, 0:D] = (t * sv).astype(BF16)
            dp_ref[rows, 2 * D:3 * D] = (dy * u * sv * _dsilu(z, sig)).astype(BF16)

        def gate_backward_matmuls(c):
            for chunk in c["chunks"]:
                for h in range(N_HEADS):
                    cols = slice(h * GROUP, (h + 1) * GROUP)
                    dvln_ref[chunk, cols] = _dot(wtt_ref[h], dsvb_ref[chunk, cols])
                    dws_ref[h] += jnp.where(tril, _dot_t1(dsvb_ref[chunk, cols], vln_ref[chunk, cols]), 0.0)

        def layernorm_backward(c):
            vh, rstd = c.pop("vh"), c.pop("rstd")
            dvln = dvln_ref[c["rows"], :]
            add(1, jnp.sum(dvln * vh, axis=0, keepdims=True))
            add(2, jnp.sum(dvln, axis=0, keepdims=True))
            dvh = dvln * lg_ref[...]
            dv = rstd * (dvh - jnp.mean(dvh, axis=-1, keepdims=True)
                         - vh * jnp.mean(dvh * vh, axis=-1, keepdims=True))
            dp_ref[c["rows"], D:2 * D] = dv.astype(BF16)

        def in_projection(c):
            c["dh"] = _dot(dp_ref[c["rows"], :], wit_ref[...])

        def pre_norm(c):
            dh = c.pop("dh")
            xv = x_ref[c["rows"], :]
            r = _rms(xv)
            xn = xv * r
            add(3, jnp.sum(dh * xn, axis=0, keepdims=True))
            dxn = dh * gn_ref[...]
            dx_ref[c["rows"], :] = g_ref[c["rows"], :] + r * (dxn - xn * jnp.mean(dxn * xn, axis=-1, keepdims=True))

        phases = [post_norm, out_projection, layernorm, gate_matmuls, gating, gate_backward_matmuls,
                  layernorm_backward, in_projection, pre_norm]
        groups = [dict(rows=slice(r0, r0 + sub),
                       chunks=[slice(r0 + k * GROUP, r0 + (k + 1) * GROUP) for k in range(sub // GROUP)])
                  for r0 in range(0, ts, sub)]
        for group in groups:
            for phase in phases:
                phase(group)

        dwo_ref[...] += _dot_t0(y_ref[...], dm_ref[...])
        dgp_ref[...] += sums[0]
        dlg_ref[...] += sums[1]
        dlb_ref[...] += sums[2]
        dgn_ref[...] += sums[3]

        @pl.when(step == nt - 1)
        def _():
            dwob_ref[...] = dwo_ref[...].astype(BF16)
            ones = jnp.ones((8, GROUP), F32)
            for h in range(N_HEADS):
                cols = slice(h * GROUP, (h + 1) * GROUP)
                total = lax.dot_general(ones, dsum_ref[:, cols], (((1,), (1,)), ((), ())),
                                        precision=lax.Precision.HIGHEST, preferred_element_type=F32)
                dbs_ref[h:h + 1, :] = total[0:1, :]

    tile = pl.BlockSpec((ts, D), lambda i: (i, 0))
    wide = pl.BlockSpec((ts, PROJ), lambda i: (i, 0))
    small = _full((N_HEADS, GROUP, GROUP))
    heads = _resident((N_HEADS, GROUP, GROUP))
    vec = _full((1, D))
    return pl.pallas_call(
        body, name="odd_backward", grid=(nt,),
        in_specs=[tile, tile, wide, tile, vec, vec, heads, heads, heads, _resident((D, D)), _resident((PROJ, D)),
                  vec, vec],
        out_specs=[wide, tile, _full((D, D)), small, _full((N_HEADS, GROUP)), vec, vec, vec, vec],
        out_shape=[jax.ShapeDtypeStruct((S, PROJ), BF16), jax.ShapeDtypeStruct((S, D), F32),
                   jax.ShapeDtypeStruct((D, D), BF16),
                   jax.ShapeDtypeStruct((N_HEADS, GROUP, GROUP), F32), jax.ShapeDtypeStruct((N_HEADS, GROUP), F32),
                   jax.ShapeDtypeStruct((1, D), F32), jax.ShapeDtypeStruct((1, D), F32),
                   jax.ShapeDtypeStruct((1, D), F32), jax.ShapeDtypeStruct((1, D), F32)],
        scratch_shapes=[pltpu.VMEM((ts, D), BF16), pltpu.VMEM((ts, D), F32), pltpu.VMEM((ts, D), BF16),
                        pltpu.VMEM((ts, D), F32), pltpu.VMEM((GROUP, D), F32), pltpu.VMEM((ts, D), BF16),
                        pltpu.VMEM((ts, D), BF16), pltpu.VMEM((D, D), F32)],
        compiler_params=_params(),
    )(g2, m1, p, x1, lg, lb, wt_b, wtt_b, bfull, woutt_b, wint_b, gpost, gpre)


def _weight_grad_scatter(h, dp, ts, name, sides=()):
    S = h.shape[0]
    ts = min(ts, S)
    nt = S // ts
    panel = 2 * SHARD_IN
    last = N_CHIPS - 1

    def body(h_ref, dp_ref, parts_ref, acc_ref, half_ref, swap_ref, sum_ref, send_sems, recv_sems):
        s, t = pl.program_id(0), pl.program_id(1)
        x, y, c = _mesh_position()
        sibling = (x, y, 1 - c)

        def swap(k):
            return _remote(half_ref.at[k, 1], swap_ref.at[k], send_sems, recv_sems, k, sibling)

        def chip_sum(k):
            flip = last - k
            to = (x ^ (flip >> 1), y ^ (flip & 1), c)
            return _remote(sum_ref.at[k], parts_ref.at[1 + flip], send_sems, recv_sems, 3 + flip, to)

        own = pltpu.make_async_copy(half_ref.at[last, 0], parts_ref.at[0], recv_sems.at[7])
        to_sibling = _remote(half_ref.at[last, 1], parts_ref.at[1], send_sems, recv_sems, 3, sibling)

        @pl.when(t == 0)
        def _():
            acc_ref[...] = jnp.zeros_like(acc_ref)

        for k in range(last):
            @pl.when((s == k + 1) & (t == 0))
            def _():
                swap(k).wait_recv()
                sum_ref[k] = (half_ref[k, 0].astype(F32) + swap_ref[k].astype(F32)).astype(BF16)
                chip_sum(k).start()

        acc_ref[...] += _dot_t0(h_ref[...], dp_ref[...])

        @pl.when(t == nt - 1)
        def _():
            for core in range(2):
                @pl.when(c == core)
                def _():
                    half_ref[s, core] = acc_ref[:, 0:SHARD_IN].astype(BF16)
                    half_ref[s, 1 - core] = acc_ref[:, SHARD_IN:panel].astype(BF16)
            for k in range(last):
                @pl.when(s == k)
                def _():
                    swap(k).start()

        @pl.when((s == last) & (t == nt - 1))
        def _():
            own.start()
            to_sibling.start()
            own.wait()
            _remote(half_ref.at[last, 1], parts_ref.at[1], send_sems, recv_sems, 3, sibling).wait_recv()
            for k in range(last):
                chip_sum(k).wait_recv()
            to_sibling.wait_send()
            for k in range(last):
                swap(k).wait_send()
                chip_sum(k).wait_send()

    chip_panel = lambda s, t: (t, (2 * lax.axis_index("x") + lax.axis_index("y")) ^ (last - s))
    sems = pltpu.SemaphoreType.DMA((8,))
    return _call(
        body, name=name, grid=(N_CHIPS, nt),
        in_specs=[pl.BlockSpec((ts, D), lambda s, t: (t, 0)), pl.BlockSpec((ts, panel), chip_panel)],
        out_specs=[HBM_SPEC], out_shape=[jax.ShapeDtypeStruct((N_CHIPS + 1, D, SHARD_IN), BF16)],
        scratch_shapes=[pltpu.VMEM((D, panel), F32), pltpu.VMEM((N_CHIPS, 2, D, SHARD_IN), BF16),
                        pltpu.VMEM((last, D, SHARD_IN), BF16), pltpu.VMEM((last, D, SHARD_IN), BF16), sems, sems],
        args=(h, dp), sides=sides)


def _adamw(w, g, m, v):
    m = ADAM_B1 * m + (1.0 - ADAM_B1) * g
    v = ADAM_B2 * v + (1.0 - ADAM_B2) * (g * g)
    m_hat = m / (1.0 - ADAM_B1 ** ADAM_STEP)
    v_hat = v / (1.0 - ADAM_B2 ** ADAM_STEP)
    delta = -ADAM_LR * (m_hat / (jnp.sqrt(v_hat) + ADAM_EPS) + ADAM_WD * w)
    return delta, m, v


UPDATE_STEPS = 4


def _sum_update(parts, weights):
    n_w = len(weights)

    def body(*refs):
        p_refs, wmv_refs, outs = refs[:n_w], refs[n_w:4 * n_w], refs[4 * n_w:]
        for i in range(n_w):
            acc = p_refs[i][0].astype(F32)
            for d in range(1, parts[i].shape[0]):
                acc = acc + p_refs[i][d].astype(F32)
            w_ref, m_ref, v_ref = wmv_refs[3 * i:3 * i + 3]
            g_out, d_out, m_out, v_out = outs[4 * i:4 * i + 4]
            g_out[...] = acc
            delta, m_new, v_new = _adamw(w_ref[...], acc, m_ref[...], v_ref[...])
            d_out[...] = delta
            m_out[...] = m_new
            v_out[...] = v_new

    part_specs, wmv_specs, out_specs, out_shape = [], [], [], []
    for p, (w, _, _) in zip(parts, weights):
        R, C = w.shape
        rb = R // UPDATE_STEPS
        blk = pl.BlockSpec((rb, C), lambda i: (i, 0))
        part_specs.append(pl.BlockSpec((p.shape[0], rb, C), lambda i: (0, i, 0)))
        wmv_specs += [blk] * 3
        out_specs += [blk] * 4
        out_shape += [jax.ShapeDtypeStruct((R, C), F32)] * 4
    flat = pl.pallas_call(
        body, name="update_large_weights", grid=(UPDATE_STEPS,), in_specs=part_specs + wmv_specs,
        out_specs=out_specs, out_shape=out_shape, compiler_params=_params(),
    )(*parts, *[a for wmv in weights for a in wmv])
    return [flat[4 * i:4 * i + 4] for i in range(n_w)]


def _small_sum_update(partials, row_of, weights, loss_parts):
    n_p, n_w = len(partials), len(weights)

    def body(*refs):
        p_refs = refs[:n_p]
        loss_ref = refs[n_p]
        w_refs = refs[n_p + 1:n_p + 1 + 3 * n_w]
        loss_out = refs[n_p + 1 + 3 * n_w]
        outs = refs[n_p + 2 + 3 * n_w:]

        def total(ref):
            acc = ref[0].astype(F32)
            for d in range(1, N_DEV):
                acc = acc + ref[d].astype(F32)
            return acc

        lsum = jnp.sum(total(loss_ref), axis=-1, keepdims=True) * (0.5 / D)
        loss_out[...] = jnp.broadcast_to(lsum, loss_out.shape)
        for k in range(n_p):
            i, r0 = row_of[k]
            g = total(p_refs[k])
            w_ref, m_ref, v_ref = w_refs[3 * i:3 * i + 3]
            g_out, d_out, m_out, v_out = outs[4 * i:4 * i + 4]
            if g.ndim == 2:
                at = (slice(r0, r0 + g.shape[0]), slice(None))
            else:
                at = (slice(None),) * g.ndim
            delta, m_new, v_new = _adamw(w_ref[at], g, m_ref[at], v_ref[at])
            g_out[at] = g
            d_out[at] = delta
            m_out[at] = m_new
            v_out[at] = v_new

    vm = pl.BlockSpec(memory_space=pltpu.VMEM)
    flat_w = [a for wmv in weights for a in wmv]
    out_shape = [jax.ShapeDtypeStruct((1, 128), F32)]
    for w, _, _ in weights:
        out_shape += [jax.ShapeDtypeStruct(w.shape, F32)] * 4
    return pl.pallas_call(
        body, name="small_update", in_specs=[vm] * (n_p + 1 + 3 * n_w), out_specs=[vm] * len(out_shape),
        out_shape=out_shape, compiler_params=pltpu.CompilerParams(vmem_limit_bytes=VMEM_LIMIT),
    )(*partials, loss_parts, *flat_w)


def _pack_small_shard(cw, lg, lb):
    return (jnp.pad(cw, ((0, 13), (0, 64))) + jnp.pad(lg, ((8, 7), (0, 0))) + jnp.pad(lb, ((9, 6), (0, 0))))


def kernel(x, pre_norm, post_norm, even_w_in, even_conv_w, even_pool_w, even_pool_scale, even_w_out, odd_w_in, odd_ln_g, odd_ln_b, odd_w_s, odd_b_s, odd_w_out, loss_target, m_pre_norm, m_post_norm, m_even_w_in, m_even_conv_w, m_even_pool_w, m_even_pool_scale, m_even_w_out, m_odd_w_in, m_odd_ln_g, m_odd_ln_b, m_odd_w_s, m_odd_b_s, m_odd_w_out, v_pre_norm, v_post_norm, v_even_w_in, v_even_conv_w, v_even_pool_w, v_even_pool_scale, v_even_w_out, v_odd_w_in, v_odd_ln_g, v_odd_ln_b, v_odd_w_s, v_odd_b_s, v_odd_w_out):
    S = x.shape[1]
    xs = x.reshape(S, D)
    tgt = loss_target.reshape(S, D)

    gpre0, gpre1 = pre_norm[0:1], pre_norm[1:2]
    gpost0, gpost1 = post_norm[0:1], post_norm[1:2]
    small_shard = _pack_small_shard(even_conv_w[0], odd_ln_g, odd_ln_b)
    (h0, wie_g, small_g, woe_g, wio_b, woo_b, pool_w_b, pool_wt_b, ws_b, wst_b, b_full) = _prenorm_gather(
        xs, gpre0, even_w_in[0], small_shard, even_w_out[0], odd_w_in[0], odd_w_out[0], even_pool_w[0], odd_w_s[0],
        odd_b_s[0], TS_PRENORM)
    wie = jnp.transpose(wie_g, (1, 0, 2)).reshape(D, PROJ)
    wie_t = jnp.transpose(wie_g, (0, 2, 1)).reshape(PROJ, D)
    woe = woe_g.reshape(D, D)
    woe_t = woe.T
    conv_w = jnp.transpose(small_g[:, 0:3, 0:64], (1, 0, 2)).reshape(3, A_W)
    ln_g = small_g[:, 8, :].reshape(1, D)
    ln_b = small_g[:, 9, :].reshape(1, D)

    p0, m0, x1, h1, wio_g, woo_g = _even_forward_fused(h0, xs, wie, conv_w, pool_w_b, even_pool_scale, woe, gpost0,
                                                       gpre1, TS_FWD, sides=[(_Gather, [wio_b, woo_b])])
    wio = jnp.transpose(wio_g, (1, 0, 2)).reshape(D, PROJ)
    wio_t = jnp.transpose(wio_g, (0, 2, 1)).reshape(PROJ, D)
    woo = woo_g.reshape(D, D)
    woo_t = woo.T
    p1, m1, g2, loss_vec = _odd_forward(h1, x1, tgt, wio, ln_g, ln_b, ws_b, b_full, woo, gpost1, TS_FWD)

    dp1, g1, dwoo, dws, dbs, dlg, dlb, dgpost1, dgpre1 = _odd_backward(
        g2, m1, p1, x1, ln_g, ln_b, ws_b, wst_b, b_full, woo_t, wio_t, gpost1, gpre1, TS_BWD)
    by_owner = lambda dwo: dwo.reshape(N_DEV, SHARD_OUT, D)
    (dwio_parts, dwoo_parts, dlg_parts, dlb_parts,
     dws_g, dbs_g, gpost1_g, gpre1_g, loss_g) = _weight_grad_scatter(
        h1, dp1, TS_WGRAD, "odd_weight_grad",
        sides=[(_Scatter, [by_owner(dwoo), dlg.reshape(N_DEV, 1, 128), dlb.reshape(N_DEV, 1, 128)]),
               (_Gather, [dws.astype(BF16), dbs, dgpost1, dgpre1, loss_vec])])
    dp0, gx, dwoe, dcw, dpw, dps, dgpost0, dgpre0 = _even_backward_fused(
        g1, m0, p0, xs, conv_w, pool_w_b, pool_wt_b, even_pool_scale, woe_t, wie_t, gpost0, gpre0, TS_FWD)
    dcw_by_owner = jnp.transpose(dcw[0:3].reshape(3, N_DEV, 64), (1, 0, 2))
    dwie_parts, dwoe_parts, dcw_parts, dpw_g, dps_g, gpost0_g, gpre0_g = _weight_grad_scatter(
        h0, dp0, TS_WGRAD, "even_weight_grad",
        sides=[(_Scatter, [by_owner(dwoe), dcw_by_owner]), (_Gather, [dpw.astype(BF16), dps, dgpost0, dgpre0])])

    ((g_wie, d_wie, nm_wie, nv_wie), (g_wio, d_wio, nm_wio, nv_wio), (g_woe, d_woe, nm_woe, nv_woe),
     (g_woo, d_woo, nm_woo, nv_woo)) = _sum_update(
        [dwie_parts, dwio_parts, dwoe_parts, dwoo_parts],
        [(even_w_in[0], m_even_w_in[0], v_even_w_in[0]), (odd_w_in[0], m_odd_w_in[0], v_odd_w_in[0]),
         (even_w_out[0], m_even_w_out[0], v_even_w_out[0]), (odd_w_out[0], m_odd_w_out[0], v_odd_w_out[0])])

    partials = [gpre0_g, gpre1_g, gpost0_g, gpost1_g, dpw_g, dps_g, dws_g, dbs_g, dcw_parts, dlg_parts, dlb_parts]
    row_of = [(0, 0), (0, 1), (1, 0), (1, 1), (2, 0), (3, 0), (4, 0), (5, 0), (6, 0), (7, 0), (8, 0)]
    weights = [(pre_norm, m_pre_norm, v_pre_norm), (post_norm, m_post_norm, v_post_norm),
               (even_pool_w[0], m_even_pool_w[0], v_even_pool_w[0]),
               (even_pool_scale, m_even_pool_scale, v_even_pool_scale),
               (odd_w_s[0], m_odd_w_s[0], v_odd_w_s[0]), (odd_b_s[0], m_odd_b_s[0], v_odd_b_s[0]),
               (even_conv_w[0], m_even_conv_w[0], v_even_conv_w[0]),
               (odd_ln_g, m_odd_ln_g, v_odd_ln_g), (odd_ln_b, m_odd_ln_b, v_odd_ln_b)]
    small = _small_sum_update(partials, row_of, weights, loss_g)
    loss = small[0][0, 0]

    def leaves(k, big):
        pre, post, pw, psc, ws, bs, cw, lg, lb = [small[1 + 4 * i + k] for i in range(len(weights))]
        wie_k, woe_k, wio_k, woo_k = big
        return [pre, post, wie_k[None], cw[None], pw[None], psc, woe_k[None], wio_k[None], lg, lb, ws[None], bs[None],
                woo_k[None]]

    outs = [loss, gx.reshape(1, S, D)]
    outs += leaves(0, (g_wie, g_woe, g_wio, g_woo))
    outs += leaves(1, (d_wie, d_woe, d_wio, d_woo))
    outs += leaves(2, (nm_wie, nm_woe, nm_wio, nm_woo))
    outs += leaves(3, (nv_wie, nv_woe, nv_wio, nv_woo))
    return tuple(outs)
```

```python
import jax
import jax.numpy as jnp
from jax import lax
from jax.experimental import pallas as pl
from jax.experimental.pallas import tpu as pltpu

F32 = jnp.float32
BF16 = jnp.bfloat16
MESH = pl.DeviceIdType.MESH

D = 1024
EPS = 1e-6
A_W = 512
B_W = 512
POOL_WINDOWS = (2, 4, 8, 16)
GROUP = 128
N_GROUPS = 4
N_HEADS = 8
HALO = 16
PROJ = 3072
N_DEV = 8
N_CHIPS = 4
SHARD_IN = PROJ // N_DEV
SHARD_OUT = D // N_DEV

ADAM_LR = 0.001
ADAM_B1 = 0.9
ADAM_B2 = 0.999
ADAM_EPS = 1e-08
ADAM_WD = 0.01
ADAM_STEP = 10

VMEM_LIMIT = 54 * 1024 * 1024

TS_PRENORM = 1024
TS_FWD = 512
TS_BWD = 256
TS_WGRAD = 4096
SUB_ROWS = 256


def _params(n_grid=1, vmem=VMEM_LIMIT):
    return pltpu.CompilerParams(dimension_semantics=("arbitrary",) * n_grid, vmem_limit_bytes=vmem)


def _full(shape):
    return pl.BlockSpec(shape, lambda *_: (0,) * len(shape))


def _resident(shape):
    return pl.BlockSpec(shape, lambda *_: (0,) * len(shape), pipeline_mode=pl.Buffered(1))


def _sigmoid(z):
    return jax.nn.sigmoid(z)


def _dsilu(z, s):
    return s * (1.0 + z * (1.0 - s))


def _dot(a, b):
    return jnp.dot(a, b, preferred_element_type=F32)


def _dot_t0(a, b):
    return lax.dot_general(a, b, (((0,), (0,)), ((), ())), preferred_element_type=F32)


def _dot_t1(a, b):
    return lax.dot_general(a, b, (((1,), (1,)), ((), ())), preferred_element_type=F32)


def _rms(x):
    return lax.rsqrt(jnp.mean(x * x, axis=-1, keepdims=True) + EPS)


def _shift_down(a, k):
    return pltpu.roll(a, k, 0)


def _shift_up(a, k):
    return pltpu.roll(a, a.shape[0] - k, 0)


def _mesh_position():
    return lax.axis_index("x"), lax.axis_index("y"), lax.axis_index("c")


def _index(pos):
    return 4 * pos[0] + 2 * pos[1] + pos[2]


def _peer(pos, d):
    x, y, c = pos
    return (1 - x if d & 4 else x, 1 - y if d & 2 else y, 1 - c if d & 1 else c)


def _remote(src, dst, send_sems, recv_sems, k, to):
    return pltpu.make_async_remote_copy(src_ref=src, dst_ref=dst, send_sem=send_sems.at[k], recv_sem=recv_sems.at[k],
                                        device_id=to, device_id_type=MESH)


class _Gather:
    def __init__(self, srcs, dsts, send_sems, recv_sems):
        x, y, c = _mesh_position()
        n = len(srcs)
        me, sibling = (x, y, c), (x, y, 1 - c)
        chips = [(1 - x, y), (x, 1 - y), (1 - x, 1 - y)]

        def copy(k, t, block, to, src=None):
            slot = dsts[t].at[_index(block)]
            return _remote(slot if src is None else src, slot, send_sems, recv_sems, k * n + t, to)

        tensors = range(n)
        self.local = [pltpu.make_async_copy(srcs[t], dsts[t].at[_index(me)], recv_sems.at[7 * n + t]) for t in tensors]
        self.first = [copy(1 + j, t, me, chip + (c,), srcs[t]) for t in tensors for j, chip in enumerate(chips)]
        self.first += [copy(0, t, me, sibling, srcs[t]) for t in tensors]
        self.ici_in = [copy(1 + j, t, chip + (c,), me) for t in tensors for j, chip in enumerate(chips)]
        self.passed = [copy(4 + j, t, chip + (c,), sibling) for t in tensors for j, chip in enumerate(chips)]
        self.d2d_in = [copy(0, t, sibling, me) for t in tensors]
        self.d2d_in += [copy(4 + j, t, chip + (1 - c,), me) for t in tensors for j, chip in enumerate(chips)]

    def start(self):
        for cp in self.local + self.first:
            cp.start()

    def middle(self):
        for landed, onward in zip(self.ici_in, self.passed):
            landed.wait_recv()
            onward.start()

    def finish(self):
        for cp in self.d2d_in:
            cp.wait_recv()
        for cp in self.first + self.passed:
            cp.wait_send()
        for cp in self.local:
            cp.wait()


class _Scatter:
    def __init__(self, srcs, dsts, send_sems, recv_sems):
        pos = _mesh_position()
        n = len(srcs)
        self.local = [pltpu.make_async_copy(srcs[t].at[_index(pos)], dsts[t].at[0], recv_sems.at[t]) for t in range(n)]
        self.remote = []
        for d in range(1, N_DEV):
            to = _peer(pos, d)
            self.remote += [_remote(srcs[t].at[_index(to)], dsts[t].at[d], send_sems, recv_sems, d * n + t, to)
                            for t in range(n)]

    def start(self):
        for cp in self.local + self.remote:
            cp.start()

    def middle(self):
        pass

    def finish(self):
        for cp in self.local:
            cp.wait()
        for cp in self.remote:
            cp.wait_recv()
        for cp in self.remote:
            cp.wait_send()


HBM_SPEC = pl.BlockSpec(memory_space=pltpu.HBM)
MIDDLE_STEPS_BEFORE_END = 4


def _landing_shape(exchange, src):
    return (N_DEV,) + src.shape if exchange is _Gather else src.shape


def _call(body, *, name, grid, in_specs, out_specs, out_shape, scratch_shapes, args, sides=()):
    params = _params(len(grid))
    if not sides:
        return pl.pallas_call(body, name=name, grid=grid, in_specs=in_specs, out_specs=out_specs, out_shape=out_shape,
                              scratch_shapes=scratch_shapes, compiler_params=params)(*args)
    n_in, n_out, n_scratch = len(in_specs), len(out_specs), len(scratch_shapes)
    counts = [len(srcs) for _, srcs in sides]
    ns = sum(counts)
    total = 1
    for g in grid:
        total *= g
    side_args, side_shapes, side_sems = [], [], []
    for exchange, srcs in sides:
        side_args += list(srcs)
        side_shapes += [jax.ShapeDtypeStruct(_landing_shape(exchange, s), s.dtype) for s in srcs]
        side_sems += [pltpu.SemaphoreType.DMA((N_DEV * len(srcs),))] * 2

    def wrapped(*refs):
        ins, side_in = refs[:n_in], refs[n_in:n_in + ns]
        outs = refs[n_in + ns:n_in + ns + n_out]
        side_out = refs[n_in + ns + n_out:n_in + 2 * ns + n_out]
        rest = refs[n_in + 2 * ns + n_out:]
        scratch, sems = rest[:n_scratch], rest[n_scratch:]
        step = pl.program_id(0)
        for axis in range(1, len(grid)):
            step = step * grid[axis] + pl.program_id(axis)

        def exchanges():
            built, at = [], 0
            for k, (exchange, _) in enumerate(sides):
                built.append(exchange(side_in[at:at + counts[k]], side_out[at:at + counts[k]],
                                      sems[2 * k], sems[2 * k + 1]))
                at += counts[k]
            return built

        @pl.when(step == 0)
        def _():
            for ex in exchanges():
                ex.start()

        @pl.when(step == max(total - MIDDLE_STEPS_BEFORE_END, 0))
        def _():
            for ex in exchanges():
                ex.middle()

        body(*ins, *outs, *scratch)

        @pl.when(step == total - 1)
        def _():
            for ex in exchanges():
                ex.finish()

    return pl.pallas_call(
        wrapped, name=name, grid=grid, in_specs=list(in_specs) + [HBM_SPEC] * ns,
        out_specs=list(out_specs) + [HBM_SPEC] * ns, out_shape=list(out_shape) + side_shapes,
        scratch_shapes=list(scratch_shapes) + side_sems, compiler_params=params)(*args, *side_args)


def _prenorm_gather(x, g, wie, small, woe, wio, woo, ts):
    S = x.shape[0]
    nt = S // ts

    def body(x_ref, g_ref, wie_ref, sm_ref, woe_ref, wio_ref, woo_ref,
             h_ref, wie_out, sm_out, woe_out, wio_b, woo_b, cast_ref, small_ref, cast2_ref, send_sems, recv_sems):
        step = pl.program_id(0)
        gather = lambda: _Gather([cast_ref, small_ref, cast2_ref], [wie_out, sm_out, woe_out], send_sems, recv_sems)

        @pl.when(step == 0)
        def _():
            cast_ref[...] = wie_ref[...].astype(BF16)
            small_ref[...] = sm_ref[...]
            cast2_ref[...] = woe_ref[...].astype(BF16)
            gather().start()
            wio_b[...] = wio_ref[...].astype(BF16)
            woo_b[...] = woo_ref[...].astype(BF16)

        xv = x_ref[...]
        h_ref[...] = ((xv * _rms(xv)) * g_ref[...]).astype(BF16)

        @pl.when(step == nt - 1)
        def _():
            exchange = gather()
            exchange.middle()
            exchange.finish()

    tile = pl.BlockSpec((ts, D), lambda i: (i, 0))
    sems = pltpu.SemaphoreType.DMA((N_DEV * 3,))
    shards = [wie, small, woe, wio, woo]
    return pl.pallas_call(
        body, name="prenorm_gather", grid=(nt,),
        in_specs=[tile, _full((1, D))] + [_full(a.shape) for a in shards],
        out_specs=[tile, HBM_SPEC, HBM_SPEC, HBM_SPEC] + [_full(a.shape) for a in shards[3:]],
        out_shape=[jax.ShapeDtypeStruct((S, D), BF16), jax.ShapeDtypeStruct((N_DEV,) + wie.shape, BF16),
                   jax.ShapeDtypeStruct((N_DEV,) + small.shape, F32), jax.ShapeDtypeStruct((N_DEV,) + woe.shape, BF16)]
        + [jax.ShapeDtypeStruct(a.shape, BF16) for a in shards[3:]],
        scratch_shapes=[pltpu.VMEM(wie.shape, BF16), pltpu.VMEM(small.shape, F32), pltpu.VMEM(woe.shape, BF16),
                        sems, sems],
        compiler_params=_params(),
    )(x, g, wie, small, woe, wio, woo)


def _even_mixer(proj, halo, row0, cw_ref, pw_ref):
    xa, gb, gc, za, xp, zp = proj
    nr = xa.shape[0]
    ha = gc * xa
    ha_ext = jnp.concatenate([halo[0], ha], axis=0)
    ha_m1 = _shift_down(ha_ext, 1)[HALO:]
    ha_m2 = _shift_down(ha_ext, 2)[HALO:]
    conv = cw_ref[2:3, :] * ha + cw_ref[1:2, :] * ha_m1 + cw_ref[0:1, :] * ha_m2
    sig_a = _sigmoid(za)
    silu_a = za * sig_a

    xp_ext = jnp.concatenate([halo[1], xp], axis=0)
    pos = row0 + lax.broadcasted_iota(jnp.int32, (nr, 1), 0)
    pooled, inv_cnt, mixed = [], [], []
    for g, w in enumerate(POOL_WINDOWS):
        cols = slice(g * GROUP, (g + 1) * GROUP)
        s = xp_ext[:, cols]
        k = 1
        while k < w:
            s = s + _shift_down(s, k)
            k *= 2
        inv = 1.0 / jnp.minimum(pos + 1, w).astype(F32)
        pg = s[HALO:] * inv - xp[:, cols]
        pooled.append(pg)
        inv_cnt.append(inv)
        mixed.append(_dot(pg.astype(BF16), pw_ref[g]))
    mixed = jnp.concatenate(mixed, axis=1)
    sig_b = _sigmoid(zp)
    silu_b = zp * sig_b
    return dict(xa=xa, gb=gb, gc=gc, za=za, xp=xp, zp=zp, ha=ha, ha_m1=ha_m1, ha_m2=ha_m2, conv=conv,
                sig_a=sig_a, silu_a=silu_a, pooled=pooled, inv_cnt=inv_cnt, mixed=mixed, sig_b=sig_b,
                silu_b=silu_b)


def _columns(a):
    return [a[:, k * A_W:(k + 1) * A_W].astype(F32) for k in range(6)]


def _even_forward_fused(h, x, win_b, cw, pw_b, ps, wout_b, gpost, gpre_next, ts, sides=()):
    S = x.shape[0]

    def body(h_ref, x_ref, wi_ref, cw_ref, pw_ref, ps_ref, wo_ref, gp_ref, gn_ref,
             p_ref, m_ref, x1_ref, h1_ref, mix_ref, carry_ref):
        i = pl.program_id(0)

        @pl.when(i == 0)
        def _():
            carry_ref[...] = jnp.zeros_like(carry_ref)

        for r0 in range(0, ts, SUB_ROWS):
            rows = slice(r0, r0 + SUB_ROWS)
            proj = _dot(h_ref[rows, :], wi_ref[...])
            p_ref[rows, :] = proj.astype(BF16)
            fw = _even_mixer(_columns(proj), (carry_ref[:, 0:A_W], carry_ref[:, A_W:D]), i * ts + r0, cw_ref, pw_ref)
            carry_ref[:, 0:A_W] = fw["ha"][SUB_ROWS - HALO:]
            carry_ref[:, A_W:D] = fw["xp"][SUB_ROWS - HALO:]
            mix_ref[rows, 0:A_W] = (fw["gb"] * fw["conv"] * fw["silu_a"]).astype(BF16)
            mix_ref[rows, A_W:D] = (fw["mixed"] * ps_ref[...] * fw["silu_b"]).astype(BF16)
            m = _dot(mix_ref[rows, :], wo_ref[...])
            m_ref[rows, :] = m.astype(BF16)
            x1 = x_ref[rows, :] + (m * _rms(m)) * gp_ref[...]
            x1_ref[rows, :] = x1
            h1_ref[rows, :] = ((x1 * _rms(x1)) * gn_ref[...]).astype(BF16)

    tile = pl.BlockSpec((ts, D), lambda i: (i, 0))
    return _call(
        body, name="even_forward", grid=(S // ts,),
        in_specs=[tile, tile, _resident((D, PROJ)), _full((3, A_W)), _full((N_GROUPS, GROUP, GROUP)), _full((1, B_W)),
                  _resident((D, D)), _full((1, D)), _full((1, D))],
        out_specs=[pl.BlockSpec((ts, PROJ), lambda i: (i, 0)), tile, tile, tile],
        out_shape=[jax.ShapeDtypeStruct((S, PROJ), BF16), jax.ShapeDtypeStruct((S, D), BF16),
                   jax.ShapeDtypeStruct((S, D), F32), jax.ShapeDtypeStruct((S, D), BF16)],
        scratch_shapes=[pltpu.VMEM((ts, D), BF16), pltpu.VMEM((HALO, D), F32)],
        args=(h, x, win_b, cw, pw_b, ps, wout_b, gpost, gpre_next), sides=sides)


def _even_backward_fused(g1, m0, p, x, cw, pw_b, pwt_b, ps, woutt_b, wint_b, gpost, gpre, ts):
    S = g1.shape[0]
    nt = S // ts

    def body(g_ref, m_ref, p_ref, halo_ref, x_ref, cw_ref, pw_ref, pwt_ref, ps_ref, wot_ref, wit_ref, gp_ref, gn_ref,
             dp_ref, dx_ref, dwob_ref, dcw_ref, dpw_ref, dps_ref, dgp_ref, dgn_ref,
             carry_ref, mix_ref, dm_ref, dwo_ref):
        step = pl.program_id(0)
        i = nt - 1 - step

        @pl.when(step == 0)
        def _():
            carry_ref[...] = jnp.zeros_like(carry_ref)
            dwo_ref[...] = jnp.zeros_like(dwo_ref)
            dcw_ref[...] = jnp.zeros_like(dcw_ref)
            dpw_ref[...] = jnp.zeros_like(dpw_ref)
            dps_ref[...] = jnp.zeros_like(dps_ref)
            dgp_ref[...] = jnp.zeros_like(dgp_ref)
            dgn_ref[...] = jnp.zeros_like(dgn_ref)

        ps_v = ps_ref[...]
        sums = [None] * 6
        add = lambda k, part: sums.__setitem__(k, part if sums[k] is None else sums[k] + part)
        for r0 in range(ts - SUB_ROWS, -1, -SUB_ROWS):
            rows = slice(r0, r0 + SUB_ROWS)
            g = g_ref[rows, :]
            m = m_ref[rows, :].astype(F32)
            q = _rms(m)
            n = m * q
            add(0, jnp.sum(g * n, axis=0, keepdims=True))
            dn = g * gp_ref[...]
            dm = q * (dn - n * jnp.mean(dn * n, axis=-1, keepdims=True))
            dm_ref[rows, :] = dm.astype(BF16)
            dmix = _dot(dm_ref[rows, :], wot_ref[...])
            dya = dmix[:, 0:A_W]
            dyb = dmix[:, A_W:D]

            if r0 == 0:
                before = _columns(halo_ref[...])
                keep = jnp.where(i == 0, 0.0, 1.0).astype(F32)
                halo = (before[2] * before[0] * keep, before[4] * keep)
            else:
                before = _columns(p_ref[r0 - HALO:r0, :])
                halo = (before[2] * before[0], before[4])
            fw = _even_mixer(_columns(p_ref[rows, :]), halo, i * ts + r0, cw_ref, pw_ref)
            mix_ref[rows, 0:A_W] = (fw["gb"] * fw["conv"] * fw["silu_a"]).astype(BF16)
            mix_ref[rows, A_W:D] = (fw["mixed"] * ps_v * fw["silu_b"]).astype(BF16)

            t = dya * fw["gb"]
            dconv = t * fw["silu_a"]
            dgb = dya * fw["conv"] * fw["silu_a"]
            dza = t * fw["conv"] * _dsilu(fw["za"], fw["sig_a"])
            add(2, jnp.sum(dconv * fw["ha"], axis=0, keepdims=True))
            add(3, jnp.sum(dconv * fw["ha_m1"], axis=0, keepdims=True))
            add(4, jnp.sum(dconv * fw["ha_m2"], axis=0, keepdims=True))
            dconv_ext = jnp.concatenate([dconv, carry_ref[:, 0:A_W]], axis=0)
            dha = (cw_ref[2:3, :] * dconv + cw_ref[1:2, :] * _shift_up(dconv_ext, 1)[:SUB_ROWS]
                   + cw_ref[0:1, :] * _shift_up(dconv_ext, 2)[:SUB_ROWS])
            dgc = dha * fw["xa"]
            dxa = dha * fw["gc"]

            u = dyb * fw["mixed"]
            add(1, jnp.sum(u * fw["silu_b"], axis=0, keepdims=True))
            dzp = u * ps_v * _dsilu(fw["zp"], fw["sig_b"])
            dmixed = (dyb * ps_v * fw["silu_b"]).astype(BF16)
            dxp, e_first = [], []
            for gi, w in enumerate(POOL_WINDOWS):
                cols = slice(gi * GROUP, (gi + 1) * GROUP)
                dmg = dmixed[:, cols]
                dpooled = _dot(dmg, pwt_ref[gi])
                dpw_ref[gi] += _dot_t0(fw["pooled"][gi].astype(BF16), dmg)
                e = dpooled * fw["inv_cnt"][gi]
                e_first.append(e[0:HALO])
                s = jnp.concatenate([e, carry_ref[:, A_W + gi * GROUP:A_W + (gi + 1) * GROUP]], axis=0)
                k = 1
                while k < w:
                    s = s + _shift_up(s, k)
                    k *= 2
                dxp.append(s[:SUB_ROWS] - dpooled)
            carry_ref[:, 0:A_W] = dconv[0:HALO]
            carry_ref[:, A_W:D] = jnp.concatenate(e_first, axis=1)

            dp_ref[rows, 0:512] = dxa.astype(BF16)
            dp_ref[rows, 512:1024] = dgb.astype(BF16)
            dp_ref[rows, 1024:1536] = dgc.astype(BF16)
            dp_ref[rows, 1536:2048] = dza.astype(BF16)
            dp_ref[rows, 2048:2560] = jnp.concatenate(dxp, axis=1).astype(BF16)
            dp_ref[rows, 2560:3072] = dzp.astype(BF16)

            dh = _dot(dp_ref[rows, :], wit_ref[...])
            xv = x_ref[rows, :]
            r = _rms(xv)
            xn = xv * r
            add(5, jnp.sum(dh * xn, axis=0, keepdims=True))
            dxn = dh * gn_ref[...]
            dx_ref[rows, :] = g + r * (dxn - xn * jnp.mean(dxn * xn, axis=-1, keepdims=True))

        dwo_ref[...] += _dot_t0(mix_ref[...], dm_ref[...])
        dgp_ref[...] += sums[0]
        dps_ref[...] += sums[1]
        dcw_ref[2:3, :] += sums[2]
        dcw_ref[1:2, :] += sums[3]
        dcw_ref[0:1, :] += sums[4]
        dgn_ref[...] += sums[5]

        @pl.when(step == nt - 1)
        def _():
            dwob_ref[...] = dwo_ref[...].astype(BF16)

    rev = lambda s: (nt - 1 - s, 0)
    tile = pl.BlockSpec((ts, D), rev)
    wide = pl.BlockSpec((ts, PROJ), rev)
    bpt = ts // HALO
    halo_map = lambda s: (jnp.maximum((nt - 1 - s) * bpt - 1, 0), 0)
    groups = _full((N_GROUPS, GROUP, GROUP))
    vec = _full((1, D))
    return pl.pallas_call(
        body, name="even_backward", grid=(nt,),
        in_specs=[tile, tile, wide, pl.BlockSpec((HALO, PROJ), halo_map), tile, _full((3, A_W)), groups, groups,
                  _full((1, B_W)), _resident((D, D)), _resident((PROJ, D)), vec, vec],
        out_specs=[wide, tile, _full((D, D)), _full((8, A_W)), groups, _full((1, B_W)), vec, vec],
        out_shape=[jax.ShapeDtypeStruct((S, PROJ), BF16), jax.ShapeDtypeStruct((S, D), F32),
                   jax.ShapeDtypeStruct((D, D), BF16), jax.ShapeDtypeStruct((8, A_W), F32),
                   jax.ShapeDtypeStruct((N_GROUPS, GROUP, GROUP), F32), jax.ShapeDtypeStruct((1, B_W), F32),
                   jax.ShapeDtypeStruct((1, D), F32), jax.ShapeDtypeStruct((1, D), F32)],
        scratch_shapes=[pltpu.VMEM((HALO, D), F32), pltpu.VMEM((ts, D), BF16), pltpu.VMEM((ts, D), BF16),
                        pltpu.VMEM((D, D), F32)],
        compiler_params=_params(),
    )(g1, m0, p, p, x, cw, pw_b, pwt_b, ps, woutt_b, wint_b, gpost, gpre)


def _gate_matmuls(w_ref, in_ref, out_ref, chunks, bias_ref=None):
    for h in range(N_HEADS):
        cols = slice(h * GROUP, (h + 1) * GROUP)
        wide = jnp.concatenate([in_ref[chunk, cols] for chunk in chunks], axis=1)
        res = _dot(w_ref[h], wide)
        for k, chunk in enumerate(chunks):
            part = res[:, k * GROUP:(k + 1) * GROUP]
            out_ref[chunk, cols] = part if bias_ref is None else part + bias_ref[h]


def _odd_forward_parts(v, lg_ref, lb_ref, wt_ref, bfull_ref, vln_ref, sv_ref, r0, nr):
    rows = slice(r0, r0 + nr)
    mu = jnp.mean(v, axis=-1, keepdims=True)
    vc = v - mu
    rstd = lax.rsqrt(jnp.mean(vc * vc, axis=-1, keepdims=True) + EPS)
    vh = vc * rstd
    vln_ref[rows, :] = (vh * lg_ref[...] + lb_ref[...]).astype(BF16)
    _gate_matmuls(wt_ref, vln_ref, sv_ref, [slice(r0 + k * GROUP, r0 + (k + 1) * GROUP) for k in range(nr // GROUP)],
                  bfull_ref)
    return vh, rstd


def _odd_forward(h, x1, tgt, win_b, lg, lb, wt_b, bfull, wout_b, gpost, ts):
    S = x1.shape[0]

    def body(h_ref, x_ref, t_ref, wi_ref, lg_ref, lb_ref, wt_ref, bfull_ref, wo_ref, gp_ref,
             p_ref, m_ref, g_ref, loss_ref, vln_ref, sv_ref, y_ref):
        @pl.when(pl.program_id(0) == 0)
        def _():
            loss_ref[...] = jnp.zeros_like(loss_ref)

        loss = None
        for r0 in range(0, ts, SUB_ROWS):
            rows = slice(r0, r0 + SUB_ROWS)
            proj = _dot(h_ref[rows, :], wi_ref[...])
            p_ref[rows, :] = proj.astype(BF16)
            u, v, z = proj[:, 0:D], proj[:, D:2 * D], proj[:, 2 * D:3 * D]
            _odd_forward_parts(v, lg_ref, lb_ref, wt_ref, bfull_ref, vln_ref, sv_ref, r0, SUB_ROWS)
            y_ref[rows, :] = (u * sv_ref[rows, :] * (z * _sigmoid(z))).astype(BF16)
            m = _dot(y_ref[rows, :], wo_ref[...])
            m_ref[rows, :] = m.astype(BF16)
            x2 = x_ref[rows, :] + (m * _rms(m)) * gp_ref[...]
            err = x2 - t_ref[rows, :]
            g_ref[rows, :] = err * (1.0 / D)
            part = jnp.sum(err * err, axis=0, keepdims=True)
            loss = part if loss is None else loss + part
        loss_ref[...] += loss

    tile = pl.BlockSpec((ts, D), lambda i: (i, 0))
    small = _full((N_HEADS, GROUP, GROUP))
    return pl.pallas_call(
        body, name="odd_forward", grid=(S // ts,),
        in_specs=[pl.BlockSpec((ts, D), lambda i: (i, 0)), tile, tile, _full((D, PROJ)), _full((1, D)), _full((1, D)),
                  small, small, _full((D, D)), _full((1, D))],
        out_specs=[pl.BlockSpec((ts, PROJ), lambda i: (i, 0)), tile, tile, _full((1, D))],
        out_shape=[jax.ShapeDtypeStruct((S, PROJ), BF16), jax.ShapeDtypeStruct((S, D), BF16),
                   jax.ShapeDtypeStruct((S, D), F32), jax.ShapeDtypeStruct((1, D), F32)],
        scratch_shapes=[pltpu.VMEM((ts, D), BF16), pltpu.VMEM((ts, D), F32), pltpu.VMEM((ts, D), BF16)],
        compiler_params=_params(),
    )(h, x1, tgt, win_b, lg, lb, wt_b, bfull, wout_b, gpost)


def _odd_backward(g2, m1, p, x1, lg, lb, wt_b, wtt_b, bfull, woutt_b, wint_b, gpost, gpre, ts):
    S = g2.shape[0]
    nt = S // ts
    sub = ts // 2

    def body(g_ref, m_ref, p_ref, x_ref, lg_ref, lb_ref, wt_ref, wtt_ref, bfull_ref, wot_ref, wit_ref, gp_ref, gn_ref,
             dp_ref, dx_ref, dwob_ref, dws_ref, dbs_ref, dlg_ref, dlb_ref, dgp_ref, dgn_ref,
             vln_ref, sv_ref, dsvb_ref, dvln_ref, dsum_ref, y_ref, dm_ref, dwo_ref):
        step = pl.program_id(0)

        @pl.when(step == 0)
        def _():
            dwo_ref[...] = jnp.zeros_like(dwo_ref)
            dws_ref[...] = jnp.zeros_like(dws_ref)
            dsum_ref[...] = jnp.zeros_like(dsum_ref)
            dlg_ref[...] = jnp.zeros_like(dlg_ref)
            dlb_ref[...] = jnp.zeros_like(dlb_ref)
            dgp_ref[...] = jnp.zeros_like(dgp_ref)
            dgn_ref[...] = jnp.zeros_like(dgn_ref)

        tril = (lax.broadcasted_iota(jnp.int32, (GROUP, GROUP), 0)
                >= lax.broadcasted_iota(jnp.int32, (GROUP, GROUP), 1))
        sums = [None] * 4
        add = lambda k, part: sums.__setitem__(k, part if sums[k] is None else sums[k] + part)

        def post_norm(c):
            g = g_ref[c["rows"], :]
            m = m_ref[c["rows"], :].astype(F32)
            q = _rms(m)
            n = m * q
            add(0, jnp.sum(g * n, axis=0, keepdims=True))
            dn = g * gp_ref[...]
            dm = q * (dn - n * jnp.mean(dn * n, axis=-1, keepdims=True))
            dm_ref[c["rows"], :] = dm.astype(BF16)

        def out_projection(c):
            c["dy"] = _dot(dm_ref[c["rows"], :], wot_ref[...])

        def layernorm(c):
            v = p_ref[c["rows"], D:2 * D].astype(F32)
            mu = jnp.mean(v, axis=-1, keepdims=True)
            vc = v - mu
            c["rstd"] = lax.rsqrt(jnp.mean(vc * vc, axis=-1, keepdims=True) + EPS)
            c["vh"] = vc * c["rstd"]
            vln_ref[c["rows"], :] = (c["vh"] * lg_ref[...] + lb_ref[...]).astype(BF16)

        def gate_matmuls(c):
            _gate_matmuls(wt_ref, vln_ref, sv_ref, c["chunks"], bfull_ref)

        def gating(c):
            rows = c["rows"]
            u = p_ref[rows, 0:D].astype(F32)
            z = p_ref[rows, 2 * D:3 * D].astype(F32)
            sv = sv_ref[rows, :]
            sig = _sigmoid(z)
            sz = z * sig
            y_ref[rows, :] = (u * sv * sz).astype(BF16)
            dy = c.pop("dy")
            t = dy * sz
            dsv = t * u
            dsvb_ref[rows, :] = dsv.astype(BF16)
            for k in range(sub // GROUP):
                dsum_ref[...] += dsv[k * GROUP:(k + 1) * GROUP]
            dp_ref[rows, 0:D] = (t * sv).astype(BF16)
            dp_ref[rows, 2 * D:3 * D] = (dy * u * sv * _dsilu(z, sig)).astype(BF16)

        def gate_backward_matmuls(c):
            for chunk in c["chunks"]:
                for h in range(N_HEADS):
                    cols = slice(h * GROUP, (h + 1) * GROUP)
                    dvln_ref[chunk, cols] = _dot(wtt_ref[h], dsvb_ref[chunk, cols])
                    dws_ref[h] += jnp.where(tril, _dot_t1(dsvb_ref[chunk, cols], vln_ref[chunk, cols]), 0.0)

        def layernorm_backward(c):
            vh, rstd = c.pop("vh"), c.pop("rstd")
            dvln = dvln_ref[c["rows"], :]
            add(1, jnp.sum(dvln * vh, axis=0, keepdims=True))
            add(2, jnp.sum(dvln, axis=0, keepdims=True))
            dvh = dvln * lg_ref[...]
            dv = rstd * (dvh - jnp.mean(dvh, axis=-1, keepdims=True)
                         - vh * jnp.mean(dvh * vh, axis=-1, keepdims=True))
            dp_ref[c["rows"], D:2 * D] = dv.astype(BF16)

        def in_projection(c):
            c["dh"] = _dot(dp_ref[c["rows"], :], wit_ref[...])

        def pre_norm(c):
            dh = c.pop("dh")
            xv = x_ref[c["rows"], :]
            r = _rms(xv)
            xn = xv * r
            add(3, jnp.sum(dh * xn, axis=0, keepdims=True))
            dxn = dh * gn_ref[...]
            dx_ref[c["rows"], :] = g_ref[c["rows"], :] + r * (dxn - xn * jnp.mean(dxn * xn, axis=-1, keepdims=True))

        phases = [post_norm, out_projection, layernorm, gate_matmuls, gating, gate_backward_matmuls,
                  layernorm_backward, in_projection, pre_norm]
        groups = [dict(rows=slice(r0, r0 + sub),
                       chunks=[slice(r0 + k * GROUP, r0 + (k + 1) * GROUP) for k in range(sub // GROUP)])
                  for r0 in range(0, ts, sub)]
        for group in groups:
            for phase in phases:
                phase(group)

        dwo_ref[...] += _dot_t0(y_ref[...], dm_ref[...])
        dgp_ref[...] += sums[0]
        dlg_ref[...] += sums[1]
        dlb_ref[...] += sums[2]
        dgn_ref[...] += sums[3]

        @pl.when(step == nt - 1)
        def _():
            dwob_ref[...] = dwo_ref[...].astype(BF16)
            ones = jnp.ones((8, GROUP), F32)
            for h in range(N_HEADS):
                cols = slice(h * GROUP, (h + 1) * GROUP)
                total = lax.dot_general(ones, dsum_ref[:, cols], (((1,), (1,)), ((), ())),
                                        precision=lax.Precision.HIGHEST, preferred_element_type=F32)
                dbs_ref[h:h + 1, :] = total[0:1, :]

    tile = pl.BlockSpec((ts, D), lambda i: (i, 0))
    wide = pl.BlockSpec((ts, PROJ), lambda i: (i, 0))
    small = _full((N_HEADS, GROUP, GROUP))
    heads = _resident((N_HEADS, GROUP, GROUP))
    vec = _full((1, D))
    return pl.pallas_call(
        body, name="odd_backward", grid=(nt,),
        in_specs=[tile, tile, wide, tile, vec, vec, heads, heads, heads, _resident((D, D)), _resident((PROJ, D)),
                  vec, vec],
        out_specs=[wide, tile, _full((D, D)), small, _full((N_HEADS, GROUP)), vec, vec, vec, vec],
        out_shape=[jax.ShapeDtypeStruct((S, PROJ), BF16), jax.ShapeDtypeStruct((S, D), F32),
                   jax.ShapeDtypeStruct((D, D), BF16),
                   jax.ShapeDtypeStruct((N_HEADS, GROUP, GROUP), F32), jax.ShapeDtypeStruct((N_HEADS, GROUP), F32),
                   jax.ShapeDtypeStruct((1, D), F32), jax.ShapeDtypeStruct((1, D), F32),
                   jax.ShapeDtypeStruct((1, D), F32), jax.ShapeDtypeStruct((1, D), F32)],
        scratch_shapes=[pltpu.VMEM((ts, D), BF16), pltpu.VMEM((ts, D), F32), pltpu.VMEM((ts, D), BF16),
                        pltpu.VMEM((ts, D), F32), pltpu.VMEM((GROUP, D), F32), pltpu.VMEM((ts, D), BF16),
                        pltpu.VMEM((ts, D), BF16), pltpu.VMEM((D, D), F32)],
        compiler_params=_params(),
    )(g2, m1, p, x1, lg, lb, wt_b, wtt_b, bfull, woutt_b, wint_b, gpost, gpre)


def _weight_grad_scatter(h, dp, ts, name, sides=()):
    S = h.shape[0]
    ts = min(ts, S)
    nt = S // ts
    panel = 2 * SHARD_IN
    last = N_CHIPS - 1

    def body(h_ref, dp_ref, parts_ref, acc_ref, half_ref, swap_ref, sum_ref, send_sems, recv_sems):
        s, t = pl.program_id(0), pl.program_id(1)
        x, y, c = _mesh_position()
        sibling = (x, y, 1 - c)

        def swap(k):
            return _remote(half_ref.at[k, 1], swap_ref.at[k], send_sems, recv_sems, k, sibling)

        def chip_sum(k):
            flip = last - k
            to = (x ^ (flip >> 1), y ^ (flip & 1), c)
            return _remote(sum_ref.at[k], parts_ref.at[1 + flip], send_sems, recv_sems, 3 + flip, to)

        own = pltpu.make_async_copy(half_ref.at[last, 0], parts_ref.at[0], recv_sems.at[7])
        to_sibling = _remote(half_ref.at[last, 1], parts_ref.at[1], send_sems, recv_sems, 3, sibling)

        @pl.when(t == 0)
        def _():
            acc_ref[...] = jnp.zeros_like(acc_ref)

        for k in range(last):
            @pl.when((s == k + 1) & (t == 0))
            def _():
                swap(k).wait_recv()
                sum_ref[k] = (half_ref[k, 0].astype(F32) + swap_ref[k].astype(F32)).astype(BF16)
                chip_sum(k).start()

        acc_ref[...] += _dot_t0(h_ref[...], dp_ref[...])

        @pl.when(t == nt - 1)
        def _():
            for core in range(2):
                @pl.when(c == core)
                def _():
                    half_ref[s, core] = acc_ref[:, 0:SHARD_IN].astype(BF16)
                    half_ref[s, 1 - core] = acc_ref[:, SHARD_IN:panel].astype(BF16)
            for k in range(last):
                @pl.when(s == k)
                def _():
                    swap(k).start()

        @pl.when((s == last) & (t == nt - 1))
        def _():
            own.start()
            to_sibling.start()
            own.wait()
            _remote(half_ref.at[last, 1], parts_ref.at[1], send_sems, recv_sems, 3, sibling).wait_recv()
            for k in range(last):
                chip_sum(k).wait_recv()
            to_sibling.wait_send()
            for k in range(last):
                swap(k).wait_send()
                chip_sum(k).wait_send()

    chip_panel = lambda s, t: (t, (2 * lax.axis_index("x") + lax.axis_index("y")) ^ (last - s))
    sems = pltpu.SemaphoreType.DMA((8,))
    return _call(
        body, name=name, grid=(N_CHIPS, nt),
        in_specs=[pl.BlockSpec((ts, D), lambda s, t: (t, 0)), pl.BlockSpec((ts, panel), chip_panel)],
        out_specs=[HBM_SPEC], out_shape=[jax.ShapeDtypeStruct((N_CHIPS + 1, D, SHARD_IN), BF16)],
        scratch_shapes=[pltpu.VMEM((D, panel), F32), pltpu.VMEM((N_CHIPS, 2, D, SHARD_IN), BF16),
                        pltpu.VMEM((last, D, SHARD_IN), BF16), pltpu.VMEM((last, D, SHARD_IN), BF16), sems, sems],
        args=(h, dp), sides=sides)


def _adamw(w, g, m, v):
    m = ADAM_B1 * m + (1.0 - ADAM_B1) * g
    v = ADAM_B2 * v + (1.0 - ADAM_B2) * (g * g)
    m_hat = m / (1.0 - ADAM_B1 ** ADAM_STEP)
    v_hat = v / (1.0 - ADAM_B2 ** ADAM_STEP)
    delta = -ADAM_LR * (m_hat / (jnp.sqrt(v_hat) + ADAM_EPS) + ADAM_WD * w)
    return delta, m, v


UPDATE_STEPS = 4


def _sum_update(parts, weights):
    n_w = len(weights)

    def body(*refs):
        p_refs, wmv_refs, outs = refs[:n_w], refs[n_w:4 * n_w], refs[4 * n_w:]
        for i in range(n_w):
            acc = p_refs[i][0].astype(F32)
            for d in range(1, parts[i].shape[0]):
                acc = acc + p_refs[i][d].astype(F32)
            w_ref, m_ref, v_ref = wmv_refs[3 * i:3 * i + 3]
            g_out, d_out, m_out, v_out = outs[4 * i:4 * i + 4]
            g_out[...] = acc
            delta, m_new, v_new = _adamw(w_ref[...], acc, m_ref[...], v_ref[...])
            d_out[...] = delta
            m_out[...] = m_new
            v_out[...] = v_new

    part_specs, wmv_specs, out_specs, out_shape = [], [], [], []
    for p, (w, _, _) in zip(parts, weights):
        R, C = w.shape
        rb = R // UPDATE_STEPS
        blk = pl.BlockSpec((rb, C), lambda i: (i, 0))
        part_specs.append(pl.BlockSpec((p.shape[0], rb, C), lambda i: (0, i, 0)))
        wmv_specs += [blk] * 3
        out_specs += [blk] * 4
        out_shape += [jax.ShapeDtypeStruct((R, C), F32)] * 4
    flat = pl.pallas_call(
        body, name="update_large_weights", grid=(UPDATE_STEPS,), in_specs=part_specs + wmv_specs,
        out_specs=out_specs, out_shape=out_shape, compiler_params=_params(),
    )(*parts, *[a for wmv in weights for a in wmv])
    return [flat[4 * i:4 * i + 4] for i in range(n_w)]


def _small_sum_update(partials, row_of, weights, loss_parts):
    n_p, n_w = len(partials), len(weights)

    def body(*refs):
        p_refs = refs[:n_p]
        loss_ref = refs[n_p]
        w_refs = refs[n_p + 1:n_p + 1 + 3 * n_w]
        loss_out = refs[n_p + 1 + 3 * n_w]
        outs = refs[n_p + 2 + 3 * n_w:]

        def total(ref):
            acc = ref[0].astype(F32)
            for d in range(1, N_DEV):
                acc = acc + ref[d].astype(F32)
            return acc

        lsum = jnp.sum(total(loss_ref), axis=-1, keepdims=True) * (0.5 / D)
        loss_out[...] = jnp.broadcast_to(lsum, loss_out.shape)
        for k in range(n_p):
            i, r0 = row_of[k]
            g = total(p_refs[k])
            w_ref, m_ref, v_ref = w_refs[3 * i:3 * i + 3]
            g_out, d_out, m_out, v_out = outs[4 * i:4 * i + 4]
            if g.ndim == 2:
                at = (slice(r0, r0 + g.shape[0]), slice(None))
            else:
                at = (slice(None),) * g.ndim
            delta, m_new, v_new = _adamw(w_ref[at], g, m_ref[at], v_ref[at])
            g_out[at] = g
            d_out[at] = delta
            m_out[at] = m_new
            v_out[at] = v_new

    vm = pl.BlockSpec(memory_space=pltpu.VMEM)
    flat_w = [a for wmv in weights for a in wmv]
    out_shape = [jax.ShapeDtypeStruct((1, 128), F32)]
    for w, _, _ in weights:
        out_shape += [jax.ShapeDtypeStruct(w.shape, F32)] * 4
    return pl.pallas_call(
        body, name="small_update", in_specs=[vm] * (n_p + 1 + 3 * n_w), out_specs=[vm] * len(out_shape),
        out_shape=out_shape, compiler_params=pltpu.CompilerParams(vmem_limit_bytes=VMEM_LIMIT),
    )(*partials, loss_parts, *flat_w)


def _pack_small_shard(cw, lg, lb):
    return (jnp.pad(cw, ((0, 13), (0, 64))) + jnp.pad(lg, ((8, 7), (0, 0))) + jnp.pad(lb, ((9, 6), (0, 0))))


def kernel(x, pre_norm, post_norm, even_w_in, even_conv_w, even_pool_w, even_pool_scale, even_w_out, odd_w_in, odd_ln_g, odd_ln_b, odd_w_s, odd_b_s, odd_w_out, loss_target, m_pre_norm, m_post_norm, m_even_w_in, m_even_conv_w, m_even_pool_w, m_even_pool_scale, m_even_w_out, m_odd_w_in, m_odd_ln_g, m_odd_ln_b, m_odd_w_s, m_odd_b_s, m_odd_w_out, v_pre_norm, v_post_norm, v_even_w_in, v_even_conv_w, v_even_pool_w, v_even_pool_scale, v_even_w_out, v_odd_w_in, v_odd_ln_g, v_odd_ln_b, v_odd_w_s, v_odd_b_s, v_odd_w_out):
    S = x.shape[1]
    xs = x.reshape(S, D)
    tgt = loss_target.reshape(S, D)

    gpre0, gpre1 = pre_norm[0:1], pre_norm[1:2]
    gpost0, gpost1 = post_norm[0:1], post_norm[1:2]
    small_shard = _pack_small_shard(even_conv_w[0], odd_ln_g, odd_ln_b)
    h0, wie_g, small_g, woe_g, wio_b, woo_b = _prenorm_gather(xs, gpre0, even_w_in[0], small_shard, even_w_out[0],
                                                              odd_w_in[0], odd_w_out[0], TS_PRENORM)
    wie = jnp.transpose(wie_g, (1, 0, 2)).reshape(D, PROJ)
    wie_t = jnp.transpose(wie_g, (0, 2, 1)).reshape(PROJ, D)
    woe = woe_g.reshape(D, D)
    woe_t = woe.T
    conv_w = jnp.transpose(small_g[:, 0:3, 0:64], (1, 0, 2)).reshape(3, A_W)
    ln_g = small_g[:, 8, :].reshape(1, D)
    ln_b = small_g[:, 9, :].reshape(1, D)

    pool_w_b = even_pool_w[0].astype(BF16)
    pool_wt_b = jnp.swapaxes(even_pool_w[0], 1, 2).astype(BF16)
    ws_tril = jnp.tril(odd_w_s[0])
    ws_b = ws_tril.astype(BF16)
    wst_b = jnp.swapaxes(ws_tril, 1, 2).astype(BF16)
    b_full = jnp.broadcast_to(odd_b_s[0][:, :, None], (N_HEADS, GROUP, GROUP))

    p0, m0, x1, h1, wio_g, woo_g = _even_forward_fused(h0, xs, wie, conv_w, pool_w_b, even_pool_scale, woe, gpost0,
                                                       gpre1, TS_FWD, sides=[(_Gather, [wio_b, woo_b])])
    wio = jnp.transpose(wio_g, (1, 0, 2)).reshape(D, PROJ)
    wio_t = jnp.transpose(wio_g, (0, 2, 1)).reshape(PROJ, D)
    woo = woo_g.reshape(D, D)
    woo_t = woo.T
    p1, m1, g2, loss_vec = _odd_forward(h1, x1, tgt, wio, ln_g, ln_b, ws_b, b_full, woo, gpost1, TS_FWD)

    dp1, g1, dwoo, dws, dbs, dlg, dlb, dgpost1, dgpre1 = _odd_backward(
        g2, m1, p1, x1, ln_g, ln_b, ws_b, wst_b, b_full, woo_t, wio_t, gpost1, gpre1, TS_BWD)
    by_owner = lambda dwo: dwo.reshape(N_DEV, SHARD_OUT, D)
    (dwio_parts, dwoo_parts, dlg_parts, dlb_parts,
     dws_g, dbs_g, gpost1_g, gpre1_g, loss_g) = _weight_grad_scatter(
        h1, dp1, TS_WGRAD, "odd_weight_grad",
        sides=[(_Scatter, [by_owner(dwoo), dlg.reshape(N_DEV, 1, 128), dlb.reshape(N_DEV, 1, 128)]),
               (_Gather, [dws.astype(BF16), dbs, dgpost1, dgpre1, loss_vec])])
    dp0, gx, dwoe, dcw, dpw, dps, dgpost0, dgpre0 = _even_backward_fused(
        g1, m0, p0, xs, conv_w, pool_w_b, pool_wt_b, even_pool_scale, woe_t, wie_t, gpost0, gpre0, TS_FWD)
    dcw_by_owner = jnp.transpose(dcw[0:3].reshape(3, N_DEV, 64), (1, 0, 2))
    dwie_parts, dwoe_parts, dcw_parts, dpw_g, dps_g, gpost0_g, gpre0_g = _weight_grad_scatter(
        h0, dp0, TS_WGRAD, "even_weight_grad",
        sides=[(_Scatter, [by_owner(dwoe), dcw_by_owner]), (_Gather, [dpw.astype(BF16), dps, dgpost0, dgpre0])])

    ((g_wie, d_wie, nm_wie, nv_wie), (g_wio, d_wio, nm_wio, nv_wio), (g_woe, d_woe, nm_woe, nv_woe),
     (g_woo, d_woo, nm_woo, nv_woo)) = _sum_update(
        [dwie_parts, dwio_parts, dwoe_parts, dwoo_parts],
        [(even_w_in[0], m_even_w_in[0], v_even_w_in[0]), (odd_w_in[0], m_odd_w_in[0], v_odd_w_in[0]),
         (even_w_out[0], m_even_w_out[0], v_even_w_out[0]), (odd_w_out[0], m_odd_w_out[0], v_odd_w_out[0])])

    partials = [gpre0_g, gpre1_g, gpost0_g, gpost1_g, dpw_g, dps_g, dws_g, dbs_g, dcw_parts, dlg_parts, dlb_parts]
    row_of = [(0, 0), (0, 1), (1, 0), (1, 1), (2, 0), (3, 0), (4, 0), (5, 0), (6, 0), (7, 0), (8, 0)]
    weights = [(pre_norm, m_pre_norm, v_pre_norm), (post_norm, m_post_norm, v_post_norm),
               (even_pool_w[0], m_even_pool_w[0], v_even_pool_w[0]),
               (even_pool_scale, m_even_pool_scale, v_even_pool_scale),
               (odd_w_s[0], m_odd_w_s[0], v_odd_w_s[0]), (odd_b_s[0], m_odd_b_s[0], v_odd_b_s[0]),
               (even_conv_w[0], m_even_conv_w[0], v_even_conv_w[0]),
               (odd_ln_g, m_odd_ln_g, v_odd_ln_g), (odd_ln_b, m_odd_ln_b, v_odd_ln_b)]
    small = _small_sum_update(partials, row_of, weights, loss_g)
    loss = small[0][0, 0]

    def leaves(k, big):
        pre, post, pw, psc, ws, bs, cw, lg, lb = [small[1 + 4 * i + k] for i in range(len(weights))]
        wie_k, woe_k, wio_k, woo_k = big
        return [pre, post, wie_k[None], cw[None], pw[None], psc, woe_k[None], wio_k[None], lg, lb, ws[None], bs[None],
                woo_k[None]]

    outs = [loss, gx.reshape(1, S, D)]
    outs += leaves(0, (g_wie, g_woe, g_wio, g_woo))
    outs += leaves(1, (d_wie, d_woe, d_wio, d_woo))
    outs += leaves(2, (nm_wie, nm_woe, nm_wio, nm_woo))
    outs += leaves(3, (nv_wie, nv_woe, nv_wio, nv_woo))
    return tuple(outs)
```

```python
import jax
import jax.numpy as jnp
from jax import lax
from jax.experimental import pallas as pl
from jax.experimental.pallas import tpu as pltpu

F32 = jnp.float32
BF16 = jnp.bfloat16
MESH = pl.DeviceIdType.MESH

D = 1024
EPS = 1e-6
A_W = 512
B_W = 512
POOL_WINDOWS = (2, 4, 8, 16)
GROUP = 128
N_GROUPS = 4
N_HEADS = 8
HALO = 16
PROJ = 3072
N_DEV = 8
N_CHIPS = 4
SHARD_IN = PROJ // N_DEV
SHARD_OUT = D // N_DEV

ADAM_LR = 0.001
ADAM_B1 = 0.9
ADAM_B2 = 0.999
ADAM_EPS = 1e-08
ADAM_WD = 0.01
ADAM_STEP = 10

VMEM_LIMIT = 54 * 1024 * 1024

TS_PRENORM = 1024
TS_FWD = 512
TS_BWD = 256
TS_WGRAD = 2048
SUB_ROWS = 256


def _params(n_grid=1, vmem=VMEM_LIMIT):
    return pltpu.CompilerParams(dimension_semantics=("arbitrary",) * n_grid, vmem_limit_bytes=vmem)


def _full(shape):
    return pl.BlockSpec(shape, lambda *_: (0,) * len(shape))


def _resident(shape):
    return pl.BlockSpec(shape, lambda *_: (0,) * len(shape), pipeline_mode=pl.Buffered(1))


def _sigmoid(z):
    return jax.nn.sigmoid(z)


def _dsilu(z, s):
    return s * (1.0 + z * (1.0 - s))


def _dot(a, b):
    return jnp.dot(a, b, preferred_element_type=F32)


def _dot_t0(a, b):
    return lax.dot_general(a, b, (((0,), (0,)), ((), ())), preferred_element_type=F32)


def _dot_t1(a, b):
    return lax.dot_general(a, b, (((1,), (1,)), ((), ())), preferred_element_type=F32)


def _rms(x):
    return lax.rsqrt(jnp.mean(x * x, axis=-1, keepdims=True) + EPS)


def _shift_down(a, k):
    return pltpu.roll(a, k, 0)


def _shift_up(a, k):
    return pltpu.roll(a, a.shape[0] - k, 0)


def _mesh_position():
    return lax.axis_index("x"), lax.axis_index("y"), lax.axis_index("c")


def _index(pos):
    return 4 * pos[0] + 2 * pos[1] + pos[2]


def _peer(pos, d):
    x, y, c = pos
    return (1 - x if d & 4 else x, 1 - y if d & 2 else y, 1 - c if d & 1 else c)


def _remote(src, dst, send_sems, recv_sems, k, to):
    return pltpu.make_async_remote_copy(src_ref=src, dst_ref=dst, send_sem=send_sems.at[k], recv_sem=recv_sems.at[k],
                                        device_id=to, device_id_type=MESH)


class _Gather:
    def __init__(self, srcs, dsts, send_sems, recv_sems):
        x, y, c = _mesh_position()
        n = len(srcs)
        me, sibling = (x, y, c), (x, y, 1 - c)
        chips = [(1 - x, y), (x, 1 - y), (1 - x, 1 - y)]

        def copy(k, t, block, to, src=None):
            slot = dsts[t].at[_index(block)]
            return _remote(slot if src is None else src, slot, send_sems, recv_sems, k * n + t, to)

        tensors = range(n)
        self.local = [pltpu.make_async_copy(srcs[t], dsts[t].at[_index(me)], recv_sems.at[7 * n + t]) for t in tensors]
        self.first = [copy(1 + j, t, me, chip + (c,), srcs[t]) for t in tensors for j, chip in enumerate(chips)]
        self.first += [copy(0, t, me, sibling, srcs[t]) for t in tensors]
        self.ici_in = [copy(1 + j, t, chip + (c,), me) for t in tensors for j, chip in enumerate(chips)]
        self.passed = [copy(4 + j, t, chip + (c,), sibling) for t in tensors for j, chip in enumerate(chips)]
        self.d2d_in = [copy(0, t, sibling, me) for t in tensors]
        self.d2d_in += [copy(4 + j, t, chip + (1 - c,), me) for t in tensors for j, chip in enumerate(chips)]

    def start(self):
        for cp in self.local + self.first:
            cp.start()

    def middle(self):
        for landed, onward in zip(self.ici_in, self.passed):
            landed.wait_recv()
            onward.start()

    def finish(self):
        for cp in self.d2d_in:
            cp.wait_recv()
        for cp in self.first + self.passed:
            cp.wait_send()
        for cp in self.local:
            cp.wait()


class _Scatter:
    def __init__(self, srcs, dsts, send_sems, recv_sems):
        pos = _mesh_position()
        n = len(srcs)
        self.local = [pltpu.make_async_copy(srcs[t].at[_index(pos)], dsts[t].at[0], recv_sems.at[t]) for t in range(n)]
        self.remote = []
        for d in range(1, N_DEV):
            to = _peer(pos, d)
            self.remote += [_remote(srcs[t].at[_index(to)], dsts[t].at[d], send_sems, recv_sems, d * n + t, to)
                            for t in range(n)]

    def start(self):
        for cp in self.local + self.remote:
            cp.start()

    def middle(self):
        pass

    def finish(self):
        for cp in self.local:
            cp.wait()
        for cp in self.remote:
            cp.wait_recv()
        for cp in self.remote:
            cp.wait_send()


HBM_SPEC = pl.BlockSpec(memory_space=pltpu.HBM)
MIDDLE_STEPS_BEFORE_END = 2


def _landing_shape(exchange, src):
    return (N_DEV,) + src.shape if exchange is _Gather else src.shape


def _call(body, *, name, grid, in_specs, out_specs, out_shape, scratch_shapes, args, sides=()):
    params = _params(len(grid))
    if not sides:
        return pl.pallas_call(body, name=name, grid=grid, in_specs=in_specs, out_specs=out_specs, out_shape=out_shape,
                              scratch_shapes=scratch_shapes, compiler_params=params)(*args)
    n_in, n_out, n_scratch = len(in_specs), len(out_specs), len(scratch_shapes)
    counts = [len(srcs) for _, srcs in sides]
    ns = sum(counts)
    total = 1
    for g in grid:
        total *= g
    side_args, side_shapes, side_sems = [], [], []
    for exchange, srcs in sides:
        side_args += list(srcs)
        side_shapes += [jax.ShapeDtypeStruct(_landing_shape(exchange, s), s.dtype) for s in srcs]
        side_sems += [pltpu.SemaphoreType.DMA((N_DEV * len(srcs),))] * 2

    def wrapped(*refs):
        ins, side_in = refs[:n_in], refs[n_in:n_in + ns]
        outs = refs[n_in + ns:n_in + ns + n_out]
        side_out = refs[n_in + ns + n_out:n_in + 2 * ns + n_out]
        rest = refs[n_in + 2 * ns + n_out:]
        scratch, sems = rest[:n_scratch], rest[n_scratch:]
        step = pl.program_id(0)
        for axis in range(1, len(grid)):
            step = step * grid[axis] + pl.program_id(axis)

        def exchanges():
            built, at = [], 0
            for k, (exchange, _) in enumerate(sides):
                built.append(exchange(side_in[at:at + counts[k]], side_out[at:at + counts[k]],
                                      sems[2 * k], sems[2 * k + 1]))
                at += counts[k]
            return built

        @pl.when(step == 0)
        def _():
            for ex in exchanges():
                ex.start()

        @pl.when(step == max(total - MIDDLE_STEPS_BEFORE_END, 0))
        def _():
            for ex in exchanges():
                ex.middle()

        body(*ins, *outs, *scratch)

        @pl.when(step == total - 1)
        def _():
            for ex in exchanges():
                ex.finish()

    return pl.pallas_call(
        wrapped, name=name, grid=grid, in_specs=list(in_specs) + [HBM_SPEC] * ns,
        out_specs=list(out_specs) + [HBM_SPEC] * ns, out_shape=list(out_shape) + side_shapes,
        scratch_shapes=list(scratch_shapes) + side_sems, compiler_params=params)(*args, *side_args)


def _prenorm_gather(x, g, wie, small, woe, wio, woo, ts):
    S = x.shape[0]
    nt = S // ts

    def body(x_ref, g_ref, wie_ref, sm_ref, woe_ref, wio_ref, woo_ref,
             h_ref, wie_out, sm_out, woe_out, wio_b, woo_b, cast_ref, small_ref, cast2_ref, send_sems, recv_sems):
        step = pl.program_id(0)
        gather = lambda: _Gather([cast_ref, small_ref, cast2_ref], [wie_out, sm_out, woe_out], send_sems, recv_sems)

        @pl.when(step == 0)
        def _():
            cast_ref[...] = wie_ref[...].astype(BF16)
            small_ref[...] = sm_ref[...]
            cast2_ref[...] = woe_ref[...].astype(BF16)
            gather().start()
            wio_b[...] = wio_ref[...].astype(BF16)
            woo_b[...] = woo_ref[...].astype(BF16)

        xv = x_ref[...]
        h_ref[...] = ((xv * _rms(xv)) * g_ref[...]).astype(BF16)

        @pl.when(step == nt - 1)
        def _():
            exchange = gather()
            exchange.middle()
            exchange.finish()

    tile = pl.BlockSpec((ts, D), lambda i: (i, 0))
    sems = pltpu.SemaphoreType.DMA((N_DEV * 3,))
    shards = [wie, small, woe, wio, woo]
    return pl.pallas_call(
        body, name="prenorm_gather", grid=(nt,),
        in_specs=[tile, _full((1, D))] + [_full(a.shape) for a in shards],
        out_specs=[tile, HBM_SPEC, HBM_SPEC, HBM_SPEC] + [_full(a.shape) for a in shards[3:]],
        out_shape=[jax.ShapeDtypeStruct((S, D), BF16), jax.ShapeDtypeStruct((N_DEV,) + wie.shape, BF16),
                   jax.ShapeDtypeStruct((N_DEV,) + small.shape, F32), jax.ShapeDtypeStruct((N_DEV,) + woe.shape, BF16)]
        + [jax.ShapeDtypeStruct(a.shape, BF16) for a in shards[3:]],
        scratch_shapes=[pltpu.VMEM(wie.shape, BF16), pltpu.VMEM(small.shape, F32), pltpu.VMEM(woe.shape, BF16),
                        sems, sems],
        compiler_params=_params(),
    )(x, g, wie, small, woe, wio, woo)


def _even_mixer(proj, halo, row0, cw_ref, pw_ref):
    xa, gb, gc, za, xp, zp = proj
    nr = xa.shape[0]
    ha = gc * xa
    ha_ext = jnp.concatenate([halo[0], ha], axis=0)
    ha_m1 = _shift_down(ha_ext, 1)[HALO:]
    ha_m2 = _shift_down(ha_ext, 2)[HALO:]
    conv = cw_ref[2:3, :] * ha + cw_ref[1:2, :] * ha_m1 + cw_ref[0:1, :] * ha_m2
    sig_a = _sigmoid(za)
    silu_a = za * sig_a

    xp_ext = jnp.concatenate([halo[1], xp], axis=0)
    pos = row0 + lax.broadcasted_iota(jnp.int32, (nr, 1), 0)
    pooled, inv_cnt, mixed = [], [], []
    for g, w in enumerate(POOL_WINDOWS):
        cols = slice(g * GROUP, (g + 1) * GROUP)
        s = xp_ext[:, cols]
        k = 1
        while k < w:
            s = s + _shift_down(s, k)
            k *= 2
        inv = 1.0 / jnp.minimum(pos + 1, w).astype(F32)
        pg = s[HALO:] * inv - xp[:, cols]
        pooled.append(pg)
        inv_cnt.append(inv)
        mixed.append(_dot(pg.astype(BF16), pw_ref[g]))
    mixed = jnp.concatenate(mixed, axis=1)
    sig_b = _sigmoid(zp)
    silu_b = zp * sig_b
    return dict(xa=xa, gb=gb, gc=gc, za=za, xp=xp, zp=zp, ha=ha, ha_m1=ha_m1, ha_m2=ha_m2, conv=conv,
                sig_a=sig_a, silu_a=silu_a, pooled=pooled, inv_cnt=inv_cnt, mixed=mixed, sig_b=sig_b,
                silu_b=silu_b)


def _columns(a):
    return [a[:, k * A_W:(k + 1) * A_W].astype(F32) for k in range(6)]


def _even_forward_fused(h, x, win_b, cw, pw_b, ps, wout_b, gpost, gpre_next, ts, sides=()):
    S = x.shape[0]

    def body(h_ref, x_ref, wi_ref, cw_ref, pw_ref, ps_ref, wo_ref, gp_ref, gn_ref,
             p_ref, m_ref, x1_ref, h1_ref, mix_ref, carry_ref):
        i = pl.program_id(0)

        @pl.when(i == 0)
        def _():
            carry_ref[...] = jnp.zeros_like(carry_ref)

        for r0 in range(0, ts, SUB_ROWS):
            rows = slice(r0, r0 + SUB_ROWS)
            proj = _dot(h_ref[rows, :], wi_ref[...])
            p_ref[rows, :] = proj.astype(BF16)
            fw = _even_mixer(_columns(proj), (carry_ref[:, 0:A_W], carry_ref[:, A_W:D]), i * ts + r0, cw_ref, pw_ref)
            carry_ref[:, 0:A_W] = fw["ha"][SUB_ROWS - HALO:]
            carry_ref[:, A_W:D] = fw["xp"][SUB_ROWS - HALO:]
            mix_ref[rows, 0:A_W] = (fw["gb"] * fw["conv"] * fw["silu_a"]).astype(BF16)
            mix_ref[rows, A_W:D] = (fw["mixed"] * ps_ref[...] * fw["silu_b"]).astype(BF16)
            m = _dot(mix_ref[rows, :], wo_ref[...])
            m_ref[rows, :] = m.astype(BF16)
            x1 = x_ref[rows, :] + (m * _rms(m)) * gp_ref[...]
            x1_ref[rows, :] = x1
            h1_ref[rows, :] = ((x1 * _rms(x1)) * gn_ref[...]).astype(BF16)

    tile = pl.BlockSpec((ts, D), lambda i: (i, 0))
    return _call(
        body, name="even_forward", grid=(S // ts,),
        in_specs=[tile, tile, _resident((D, PROJ)), _full((3, A_W)), _full((N_GROUPS, GROUP, GROUP)), _full((1, B_W)),
                  _resident((D, D)), _full((1, D)), _full((1, D))],
        out_specs=[pl.BlockSpec((ts, PROJ), lambda i: (i, 0)), tile, tile, tile],
        out_shape=[jax.ShapeDtypeStruct((S, PROJ), BF16), jax.ShapeDtypeStruct((S, D), BF16),
                   jax.ShapeDtypeStruct((S, D), F32), jax.ShapeDtypeStruct((S, D), BF16)],
        scratch_shapes=[pltpu.VMEM((ts, D), BF16), pltpu.VMEM((HALO, D), F32)],
        args=(h, x, win_b, cw, pw_b, ps, wout_b, gpost, gpre_next), sides=sides)


def _even_backward_fused(g1, m0, p, x, cw, pw_b, pwt_b, ps, woutt_b, wint_b, gpost, gpre, ts):
    S = g1.shape[0]
    nt = S // ts

    def body(g_ref, m_ref, p_ref, halo_ref, x_ref, cw_ref, pw_ref, pwt_ref, ps_ref, wot_ref, wit_ref, gp_ref, gn_ref,
             dp_ref, dx_ref, dwob_ref, dcw_ref, dpw_ref, dps_ref, dgp_ref, dgn_ref,
             carry_ref, mix_ref, dm_ref, dwo_ref):
        step = pl.program_id(0)
        i = nt - 1 - step

        @pl.when(step == 0)
        def _():
            carry_ref[...] = jnp.zeros_like(carry_ref)
            dwo_ref[...] = jnp.zeros_like(dwo_ref)
            dcw_ref[...] = jnp.zeros_like(dcw_ref)
            dpw_ref[...] = jnp.zeros_like(dpw_ref)
            dps_ref[...] = jnp.zeros_like(dps_ref)
            dgp_ref[...] = jnp.zeros_like(dgp_ref)
            dgn_ref[...] = jnp.zeros_like(dgn_ref)

        ps_v = ps_ref[...]
        sums = [None] * 6
        add = lambda k, part: sums.__setitem__(k, part if sums[k] is None else sums[k] + part)
        for r0 in range(ts - SUB_ROWS, -1, -SUB_ROWS):
            rows = slice(r0, r0 + SUB_ROWS)
            g = g_ref[rows, :]
            m = m_ref[rows, :].astype(F32)
            q = _rms(m)
            n = m * q
            add(0, jnp.sum(g * n, axis=0, keepdims=True))
            dn = g * gp_ref[...]
            dm = q * (dn - n * jnp.mean(dn * n, axis=-1, keepdims=True))
            dm_ref[rows, :] = dm.astype(BF16)
            dmix = _dot(dm_ref[rows, :], wot_ref[...])
            dya = dmix[:, 0:A_W]
            dyb = dmix[:, A_W:D]

            if r0 == 0:
                before = _columns(halo_ref[...])
                keep = jnp.where(i == 0, 0.0, 1.0).astype(F32)
                halo = (before[2] * before[0] * keep, before[4] * keep)
            else:
                before = _columns(p_ref[r0 - HALO:r0, :])
                halo = (before[2] * before[0], before[4])
            fw = _even_mixer(_columns(p_ref[rows, :]), halo, i * ts + r0, cw_ref, pw_ref)
            mix_ref[rows, 0:A_W] = (fw["gb"] * fw["conv"] * fw["silu_a"]).astype(BF16)
            mix_ref[rows, A_W:D] = (fw["mixed"] * ps_v * fw["silu_b"]).astype(BF16)

            t = dya * fw["gb"]
            dconv = t * fw["silu_a"]
            dgb = dya * fw["conv"] * fw["silu_a"]
            dza = t * fw["conv"] * _dsilu(fw["za"], fw["sig_a"])
            add(2, jnp.sum(dconv * fw["ha"], axis=0, keepdims=True))
            add(3, jnp.sum(dconv * fw["ha_m1"], axis=0, keepdims=True))
            add(4, jnp.sum(dconv * fw["ha_m2"], axis=0, keepdims=True))
            dconv_ext = jnp.concatenate([dconv, carry_ref[:, 0:A_W]], axis=0)
            dha = (cw_ref[2:3, :] * dconv + cw_ref[1:2, :] * _shift_up(dconv_ext, 1)[:SUB_ROWS]
                   + cw_ref[0:1, :] * _shift_up(dconv_ext, 2)[:SUB_ROWS])
            dgc = dha * fw["xa"]
            dxa = dha * fw["gc"]

            u = dyb * fw["mixed"]
            add(1, jnp.sum(u * fw["silu_b"], axis=0, keepdims=True))
            dzp = u * ps_v * _dsilu(fw["zp"], fw["sig_b"])
            dmixed = (dyb * ps_v * fw["silu_b"]).astype(BF16)
            dxp, e_first = [], []
            for gi, w in enumerate(POOL_WINDOWS):
                cols = slice(gi * GROUP, (gi + 1) * GROUP)
                dmg = dmixed[:, cols]
                dpooled = _dot(dmg, pwt_ref[gi])
                dpw_ref[gi] += _dot_t0(fw["pooled"][gi].astype(BF16), dmg)
                e = dpooled * fw["inv_cnt"][gi]
                e_first.append(e[0:HALO])
                s = jnp.concatenate([e, carry_ref[:, A_W + gi * GROUP:A_W + (gi + 1) * GROUP]], axis=0)
                k = 1
                while k < w:
                    s = s + _shift_up(s, k)
                    k *= 2
                dxp.append(s[:SUB_ROWS] - dpooled)
            carry_ref[:, 0:A_W] = dconv[0:HALO]
            carry_ref[:, A_W:D] = jnp.concatenate(e_first, axis=1)

            dp_ref[rows, 0:512] = dxa.astype(BF16)
            dp_ref[rows, 512:1024] = dgb.astype(BF16)
            dp_ref[rows, 1024:1536] = dgc.astype(BF16)
            dp_ref[rows, 1536:2048] = dza.astype(BF16)
            dp_ref[rows, 2048:2560] = jnp.concatenate(dxp, axis=1).astype(BF16)
            dp_ref[rows, 2560:3072] = dzp.astype(BF16)

            dh = _dot(dp_ref[rows, :], wit_ref[...])
            xv = x_ref[rows, :]
            r = _rms(xv)
            xn = xv * r
            add(5, jnp.sum(dh * xn, axis=0, keepdims=True))
            dxn = dh * gn_ref[...]
            dx_ref[rows, :] = g + r * (dxn - xn * jnp.mean(dxn * xn, axis=-1, keepdims=True))

        dwo_ref[...] += _dot_t0(mix_ref[...], dm_ref[...])
        dgp_ref[...] += sums[0]
        dps_ref[...] += sums[1]
        dcw_ref[2:3, :] += sums[2]
        dcw_ref[1:2, :] += sums[3]
        dcw_ref[0:1, :] += sums[4]
        dgn_ref[...] += sums[5]

        @pl.when(step == nt - 1)
        def _():
            dwob_ref[...] = dwo_ref[...].astype(BF16)

    rev = lambda s: (nt - 1 - s, 0)
    tile = pl.BlockSpec((ts, D), rev)
    wide = pl.BlockSpec((ts, PROJ), rev)
    bpt = ts // HALO
    halo_map = lambda s: (jnp.maximum((nt - 1 - s) * bpt - 1, 0), 0)
    groups = _full((N_GROUPS, GROUP, GROUP))
    vec = _full((1, D))
    return pl.pallas_call(
        body, name="even_backward", grid=(nt,),
        in_specs=[tile, tile, wide, pl.BlockSpec((HALO, PROJ), halo_map), tile, _full((3, A_W)), groups, groups,
                  _full((1, B_W)), _resident((D, D)), _resident((PROJ, D)), vec, vec],
        out_specs=[wide, tile, _full((D, D)), _full((8, A_W)), groups, _full((1, B_W)), vec, vec],
        out_shape=[jax.ShapeDtypeStruct((S, PROJ), BF16), jax.ShapeDtypeStruct((S, D), F32),
                   jax.ShapeDtypeStruct((D, D), BF16), jax.ShapeDtypeStruct((8, A_W), F32),
                   jax.ShapeDtypeStruct((N_GROUPS, GROUP, GROUP), F32), jax.ShapeDtypeStruct((1, B_W), F32),
                   jax.ShapeDtypeStruct((1, D), F32), jax.ShapeDtypeStruct((1, D), F32)],
        scratch_shapes=[pltpu.VMEM((HALO, D), F32), pltpu.VMEM((ts, D), BF16), pltpu.VMEM((ts, D), BF16),
                        pltpu.VMEM((D, D), F32)],
        compiler_params=_params(),
    )(g1, m0, p, p, x, cw, pw_b, pwt_b, ps, woutt_b, wint_b, gpost, gpre)


def _gate_matmuls(w_ref, in_ref, out_ref, chunks, bias_ref=None):
    for h in range(N_HEADS):
        cols = slice(h * GROUP, (h + 1) * GROUP)
        wide = jnp.concatenate([in_ref[chunk, cols] for chunk in chunks], axis=1)
        res = _dot(w_ref[h], wide)
        for k, chunk in enumerate(chunks):
            part = res[:, k * GROUP:(k + 1) * GROUP]
            out_ref[chunk, cols] = part if bias_ref is None else part + bias_ref[h]


def _odd_forward_parts(v, lg_ref, lb_ref, wt_ref, bfull_ref, vln_ref, sv_ref, r0, nr):
    rows = slice(r0, r0 + nr)
    mu = jnp.mean(v, axis=-1, keepdims=True)
    vc = v - mu
    rstd = lax.rsqrt(jnp.mean(vc * vc, axis=-1, keepdims=True) + EPS)
    vh = vc * rstd
    vln_ref[rows, :] = (vh * lg_ref[...] + lb_ref[...]).astype(BF16)
    _gate_matmuls(wt_ref, vln_ref, sv_ref, [slice(r0 + k * GROUP, r0 + (k + 1) * GROUP) for k in range(nr // GROUP)],
                  bfull_ref)
    return vh, rstd


def _odd_forward(h, x1, tgt, win_b, lg, lb, wt_b, bfull, wout_b, gpost, ts):
    S = x1.shape[0]

    def body(h_ref, x_ref, t_ref, wi_ref, lg_ref, lb_ref, wt_ref, bfull_ref, wo_ref, gp_ref,
             p_ref, m_ref, g_ref, loss_ref, vln_ref, sv_ref, y_ref):
        @pl.when(pl.program_id(0) == 0)
        def _():
            loss_ref[...] = jnp.zeros_like(loss_ref)

        loss = None
        for r0 in range(0, ts, SUB_ROWS):
            rows = slice(r0, r0 + SUB_ROWS)
            proj = _dot(h_ref[rows, :], wi_ref[...])
            p_ref[rows, :] = proj.astype(BF16)
            u, v, z = proj[:, 0:D], proj[:, D:2 * D], proj[:, 2 * D:3 * D]
            _odd_forward_parts(v, lg_ref, lb_ref, wt_ref, bfull_ref, vln_ref, sv_ref, r0, SUB_ROWS)
            y_ref[rows, :] = (u * sv_ref[rows, :] * (z * _sigmoid(z))).astype(BF16)
            m = _dot(y_ref[rows, :], wo_ref[...])
            m_ref[rows, :] = m.astype(BF16)
            x2 = x_ref[rows, :] + (m * _rms(m)) * gp_ref[...]
            err = x2 - t_ref[rows, :]
            g_ref[rows, :] = err * (1.0 / D)
            part = jnp.sum(err * err, axis=0, keepdims=True)
            loss = part if loss is None else loss + part
        loss_ref[...] += loss

    tile = pl.BlockSpec((ts, D), lambda i: (i, 0))
    small = _full((N_HEADS, GROUP, GROUP))
    return pl.pallas_call(
        body, name="odd_forward", grid=(S // ts,),
        in_specs=[pl.BlockSpec((ts, D), lambda i: (i, 0)), tile, tile, _full((D, PROJ)), _full((1, D)), _full((1, D)),
                  small, small, _full((D, D)), _full((1, D))],
        out_specs=[pl.BlockSpec((ts, PROJ), lambda i: (i, 0)), tile, tile, _full((1, D))],
        out_shape=[jax.ShapeDtypeStruct((S, PROJ), BF16), jax.ShapeDtypeStruct((S, D), BF16),
                   jax.ShapeDtypeStruct((S, D), F32), jax.ShapeDtypeStruct((1, D), F32)],
        scratch_shapes=[pltpu.VMEM((ts, D), BF16), pltpu.VMEM((ts, D), F32), pltpu.VMEM((ts, D), BF16)],
        compiler_params=_params(),
    )(h, x1, tgt, win_b, lg, lb, wt_b, bfull, wout_b, gpost)


def _odd_backward(g2, m1, p, x1, lg, lb, wt_b, wtt_b, bfull, woutt_b, wint_b, gpost, gpre, ts):
    S = g2.shape[0]
    nt = S // ts
    sub = ts // 2

    def body(g_ref, m_ref, p_ref, x_ref, lg_ref, lb_ref, wt_ref, wtt_ref, bfull_ref, wot_ref, wit_ref, gp_ref, gn_ref,
             dp_ref, dx_ref, dwob_ref, dws_ref, dbs_ref, dlg_ref, dlb_ref, dgp_ref, dgn_ref,
             vln_ref, sv_ref, dsvb_ref, dvln_ref, dsum_ref, y_ref, dm_ref, dwo_ref):
        step = pl.program_id(0)

        @pl.when(step == 0)
        def _():
            dwo_ref[...] = jnp.zeros_like(dwo_ref)
            dws_ref[...] = jnp.zeros_like(dws_ref)
            dsum_ref[...] = jnp.zeros_like(dsum_ref)
            dlg_ref[...] = jnp.zeros_like(dlg_ref)
            dlb_ref[...] = jnp.zeros_like(dlb_ref)
            dgp_ref[...] = jnp.zeros_like(dgp_ref)
            dgn_ref[...] = jnp.zeros_like(dgn_ref)

        tril = (lax.broadcasted_iota(jnp.int32, (GROUP, GROUP), 0)
                >= lax.broadcasted_iota(jnp.int32, (GROUP, GROUP), 1))
        sums = [None] * 4
        add = lambda k, part: sums.__setitem__(k, part if sums[k] is None else sums[k] + part)

        def post_norm(c):
            g = g_ref[c["rows"], :]
            m = m_ref[c["rows"], :].astype(F32)
            q = _rms(m)
            n = m * q
            add(0, jnp.sum(g * n, axis=0, keepdims=True))
            dn = g * gp_ref[...]
            dm = q * (dn - n * jnp.mean(dn * n, axis=-1, keepdims=True))
            dm_ref[c["rows"], :] = dm.astype(BF16)

        def out_projection(c):
            c["dy"] = _dot(dm_ref[c["rows"], :], wot_ref[...])

        def layernorm(c):
            v = p_ref[c["rows"], D:2 * D].astype(F32)
            mu = jnp.mean(v, axis=-1, keepdims=True)
            vc = v - mu
            c["rstd"] = lax.rsqrt(jnp.mean(vc * vc, axis=-1, keepdims=True) + EPS)
            c["vh"] = vc * c["rstd"]
            vln_ref[c["rows"], :] = (c["vh"] * lg_ref[...] + lb_ref[...]).astype(BF16)

        def gate_matmuls(c):
            _gate_matmuls(wt_ref, vln_ref, sv_ref, c["chunks"], bfull_ref)

        def gating(c):
            rows = c["rows"]
            u = p_ref[rows, 0:D].astype(F32)
            z = p_ref[rows, 2 * D:3 * D].astype(F32)
            sv = sv_ref[rows, :]
            sig = _sigmoid(z)
            sz = z * sig
            y_ref[rows, :] = (u * sv * sz).astype(BF16)
            dy = c.pop("dy")
            t = dy * sz
            dsv = t * u
            dsvb_ref[rows, :] = dsv.astype(BF16)
            for k in range(sub // GROUP):
                dsum_ref[...] += dsv[k * GROUP:(k + 1) * GROUP]
            dp_ref[rows, 0:D] = (t * sv).astype(BF16)
            dp_ref[rows, 2 * D:3 * D] = (dy * u * sv * _dsilu(z, sig)).astype(BF16)

        def gate_backward_matmuls(c):
            for chunk in c["chunks"]:
                for h in range(N_HEADS):
                    cols = slice(h * GROUP, (h + 1) * GROUP)
                    dvln_ref[chunk, cols] = _dot(wtt_ref[h], dsvb_ref[chunk, cols])
                    dws_ref[h] += jnp.where(tril, _dot_t1(dsvb_ref[chunk, cols], vln_ref[chunk, cols]), 0.0)

        def layernorm_backward(c):
            vh, rstd = c.pop("vh"), c.pop("rstd")
            dvln = dvln_ref[c["rows"], :]
            add(1, jnp.sum(dvln * vh, axis=0, keepdims=True))
            add(2, jnp.sum(dvln, axis=0, keepdims=True))
            dvh = dvln * lg_ref[...]
            dv = rstd * (dvh - jnp.mean(dvh, axis=-1, keepdims=True)
                         - vh * jnp.mean(dvh * vh, axis=-1, keepdims=True))
            dp_ref[c["rows"], D:2 * D] = dv.astype(BF16)

        def in_projection(c):
            c["dh"] = _dot(dp_ref[c["rows"], :], wit_ref[...])

        def pre_norm(c):
            dh = c.pop("dh")
            xv = x_ref[c["rows"], :]
            r = _rms(xv)
            xn = xv * r
            add(3, jnp.sum(dh * xn, axis=0, keepdims=True))
            dxn = dh * gn_ref[...]
            dx_ref[c["rows"], :] = g_ref[c["rows"], :] + r * (dxn - xn * jnp.mean(dxn * xn, axis=-1, keepdims=True))

        phases = [post_norm, out_projection, layernorm, gate_matmuls, gating, gate_backward_matmuls,
                  layernorm_backward, in_projection, pre_norm]
        groups = [dict(rows=slice(r0, r0 + sub),
                       chunks=[slice(r0 + k * GROUP, r0 + (k + 1) * GROUP) for k in range(sub // GROUP)])
                  for r0 in range(0, ts, sub)]
        for group in groups:
            for phase in phases:
                phase(group)

        dwo_ref[...] += _dot_t0(y_ref[...], dm_ref[...])
        dgp_ref[...] += sums[0]
        dlg_ref[...] += sums[1]
        dlb_ref[...] += sums[2]
        dgn_ref[...] += sums[3]

        @pl.when(step == nt - 1)
        def _():
            dwob_ref[...] = dwo_ref[...].astype(BF16)
            ones = jnp.ones((8, GROUP), F32)
            for h in range(N_HEADS):
                cols = slice(h * GROUP, (h + 1) * GROUP)
                total = lax.dot_general(ones, dsum_ref[:, cols], (((1,), (1,)), ((), ())),
                                        precision=lax.Precision.HIGHEST, preferred_element_type=F32)
                dbs_ref[h:h + 1, :] = total[0:1, :]

    tile = pl.BlockSpec((ts, D), lambda i: (i, 0))
    wide = pl.BlockSpec((ts, PROJ), lambda i: (i, 0))
    small = _full((N_HEADS, GROUP, GROUP))
    heads = _resident((N_HEADS, GROUP, GROUP))
    vec = _full((1, D))
    return pl.pallas_call(
        body, name="odd_backward", grid=(nt,),
        in_specs=[tile, tile, wide, tile, vec, vec, heads, heads, heads, _resident((D, D)), _resident((PROJ, D)),
                  vec, vec],
        out_specs=[wide, tile, _full((D, D)), small, _full((N_HEADS, GROUP)), vec, vec, vec, vec],
        out_shape=[jax.ShapeDtypeStruct((S, PROJ), BF16), jax.ShapeDtypeStruct((S, D), F32),
                   jax.ShapeDtypeStruct((D, D), BF16),
                   jax.ShapeDtypeStruct((N_HEADS, GROUP, GROUP), F32), jax.ShapeDtypeStruct((N_HEADS, GROUP), F32),
                   jax.ShapeDtypeStruct((1, D), F32), jax.ShapeDtypeStruct((1, D), F32),
                   jax.ShapeDtypeStruct((1, D), F32), jax.ShapeDtypeStruct((1, D), F32)],
        scratch_shapes=[pltpu.VMEM((ts, D), BF16), pltpu.VMEM((ts, D), F32), pltpu.VMEM((ts, D), BF16),
                        pltpu.VMEM((ts, D), F32), pltpu.VMEM((GROUP, D), F32), pltpu.VMEM((ts, D), BF16),
                        pltpu.VMEM((ts, D), BF16), pltpu.VMEM((D, D), F32)],
        compiler_params=_params(),
    )(g2, m1, p, x1, lg, lb, wt_b, wtt_b, bfull, woutt_b, wint_b, gpost, gpre)


def _weight_grad_scatter(h, dp, ts, name, sides=()):
    S = h.shape[0]
    ts = min(ts, S)
    nt = S // ts
    panel = 2 * SHARD_IN
    last = N_CHIPS - 1

    def body(h_ref, dp_ref, parts_ref, acc_ref, half_ref, swap_ref, sum_ref, send_sems, recv_sems):
        s, t = pl.program_id(0), pl.program_id(1)
        x, y, c = _mesh_position()
        sibling = (x, y, 1 - c)

        def swap(k):
            return _remote(half_ref.at[k, 1], swap_ref.at[k], send_sems, recv_sems, k, sibling)

        def chip_sum(k):
            flip = last - k
            to = (x ^ (flip >> 1), y ^ (flip & 1), c)
            return _remote(sum_ref.at[k], parts_ref.at[1 + flip], send_sems, recv_sems, 3 + flip, to)

        own = pltpu.make_async_copy(half_ref.at[last, 0], parts_ref.at[0], recv_sems.at[7])
        to_sibling = _remote(half_ref.at[last, 1], parts_ref.at[1], send_sems, recv_sems, 3, sibling)

        @pl.when(t == 0)
        def _():
            acc_ref[...] = jnp.zeros_like(acc_ref)

        for k in range(last):
            @pl.when((s == k + 1) & (t == 0))
            def _():
                swap(k).wait_recv()
                sum_ref[k] = (half_ref[k, 0].astype(F32) + swap_ref[k].astype(F32)).astype(BF16)
                chip_sum(k).start()

        acc_ref[...] += _dot_t0(h_ref[...], dp_ref[...])

        @pl.when(t == nt - 1)
        def _():
            for core in range(2):
                @pl.when(c == core)
                def _():
                    half_ref[s, core] = acc_ref[:, 0:SHARD_IN].astype(BF16)
                    half_ref[s, 1 - core] = acc_ref[:, SHARD_IN:panel].astype(BF16)
            for k in range(last):
                @pl.when(s == k)
                def _():
                    swap(k).start()

        @pl.when((s == last) & (t == nt - 1))
        def _():
            own.start()
            to_sibling.start()
            own.wait()
            _remote(half_ref.at[last, 1], parts_ref.at[1], send_sems, recv_sems, 3, sibling).wait_recv()
            for k in range(last):
                chip_sum(k).wait_recv()
            to_sibling.wait_send()
            for k in range(last):
                swap(k).wait_send()
                chip_sum(k).wait_send()

    chip_panel = lambda s, t: (t, (2 * lax.axis_index("x") + lax.axis_index("y")) ^ (last - s))
    sems = pltpu.SemaphoreType.DMA((8,))
    return _call(
        body, name=name, grid=(N_CHIPS, nt),
        in_specs=[pl.BlockSpec((ts, D), lambda s, t: (t, 0)), pl.BlockSpec((ts, panel), chip_panel)],
        out_specs=[HBM_SPEC], out_shape=[jax.ShapeDtypeStruct((N_CHIPS + 1, D, SHARD_IN), BF16)],
        scratch_shapes=[pltpu.VMEM((D, panel), F32), pltpu.VMEM((N_CHIPS, 2, D, SHARD_IN), BF16),
                        pltpu.VMEM((last, D, SHARD_IN), BF16), pltpu.VMEM((last, D, SHARD_IN), BF16), sems, sems],
        args=(h, dp), sides=sides)


def _adamw(w, g, m, v):
    m = ADAM_B1 * m + (1.0 - ADAM_B1) * g
    v = ADAM_B2 * v + (1.0 - ADAM_B2) * (g * g)
    m_hat = m / (1.0 - ADAM_B1 ** ADAM_STEP)
    v_hat = v / (1.0 - ADAM_B2 ** ADAM_STEP)
    delta = -ADAM_LR * (m_hat / (jnp.sqrt(v_hat) + ADAM_EPS) + ADAM_WD * w)
    return delta, m, v


UPDATE_STEPS = 4


def _sum_update(parts, weights):
    n_w = len(weights)

    def body(*refs):
        p_refs, wmv_refs, outs = refs[:n_w], refs[n_w:4 * n_w], refs[4 * n_w:]
        for i in range(n_w):
            acc = p_refs[i][0].astype(F32)
            for d in range(1, parts[i].shape[0]):
                acc = acc + p_refs[i][d].astype(F32)
            w_ref, m_ref, v_ref = wmv_refs[3 * i:3 * i + 3]
            g_out, d_out, m_out, v_out = outs[4 * i:4 * i + 4]
            g_out[...] = acc
            delta, m_new, v_new = _adamw(w_ref[...], acc, m_ref[...], v_ref[...])
            d_out[...] = delta
            m_out[...] = m_new
            v_out[...] = v_new

    part_specs, wmv_specs, out_specs, out_shape = [], [], [], []
    for p, (w, _, _) in zip(parts, weights):
        R, C = w.shape
        rb = R // UPDATE_STEPS
        blk = pl.BlockSpec((rb, C), lambda i: (i, 0))
        part_specs.append(pl.BlockSpec((p.shape[0], rb, C), lambda i: (0, i, 0)))
        wmv_specs += [blk] * 3
        out_specs += [blk] * 4
        out_shape += [jax.ShapeDtypeStruct((R, C), F32)] * 4
    flat = pl.pallas_call(
        body, name="update_large_weights", grid=(UPDATE_STEPS,), in_specs=part_specs + wmv_specs,
        out_specs=out_specs, out_shape=out_shape, compiler_params=_params(),
    )(*parts, *[a for wmv in weights for a in wmv])
    return [flat[4 * i:4 * i + 4] for i in range(n_w)]


def _small_sum_update(partials, row_of, weights, loss_parts):
    n_p, n_w = len(partials), len(weights)

    def body(*refs):
        p_refs = refs[:n_p]
        loss_ref = refs[n_p]
        w_refs = refs[n_p + 1:n_p + 1 + 3 * n_w]
        loss_out = refs[n_p + 1 + 3 * n_w]
        outs = refs[n_p + 2 + 3 * n_w:]

        def total(ref):
            acc = ref[0].astype(F32)
            for d in range(1, N_DEV):
                acc = acc + ref[d].astype(F32)
            return acc

        lsum = jnp.sum(total(loss_ref), axis=-1, keepdims=True) * (0.5 / D)
        loss_out[...] = jnp.broadcast_to(lsum, loss_out.shape)
        for k in range(n_p):
            i, r0 = row_of[k]
            g = total(p_refs[k])
            w_ref, m_ref, v_ref = w_refs[3 * i:3 * i + 3]
            g_out, d_out, m_out, v_out = outs[4 * i:4 * i + 4]
            if g.ndim == 2:
                at = (slice(r0, r0 + g.shape[0]), slice(None))
            else:
                at = (slice(None),) * g.ndim
            delta, m_new, v_new = _adamw(w_ref[at], g, m_ref[at], v_ref[at])
            g_out[at] = g
            d_out[at] = delta
            m_out[at] = m_new
            v_out[at] = v_new

    vm = pl.BlockSpec(memory_space=pltpu.VMEM)
    flat_w = [a for wmv in weights for a in wmv]
    out_shape = [jax.ShapeDtypeStruct((1, 128), F32)]
    for w, _, _ in weights:
        out_shape += [jax.ShapeDtypeStruct(w.shape, F32)] * 4
    return pl.pallas_call(
        body, name="small_update", in_specs=[vm] * (n_p + 1 + 3 * n_w), out_specs=[vm] * len(out_shape),
        out_shape=out_shape, compiler_params=pltpu.CompilerParams(vmem_limit_bytes=VMEM_LIMIT),
    )(*partials, loss_parts, *flat_w)


def _pack_small_shard(cw, lg, lb):
    return (jnp.pad(cw, ((0, 13), (0, 64))) + jnp.pad(lg, ((8, 7), (0, 0))) + jnp.pad(lb, ((9, 6), (0, 0))))


def kernel(x, pre_norm, post_norm, even_w_in, even_conv_w, even_pool_w, even_pool_scale, even_w_out, odd_w_in, odd_ln_g, odd_ln_b, odd_w_s, odd_b_s, odd_w_out, loss_target, m_pre_norm, m_post_norm, m_even_w_in, m_even_conv_w, m_even_pool_w, m_even_pool_scale, m_even_w_out, m_odd_w_in, m_odd_ln_g, m_odd_ln_b, m_odd_w_s, m_odd_b_s, m_odd_w_out, v_pre_norm, v_post_norm, v_even_w_in, v_even_conv_w, v_even_pool_w, v_even_pool_scale, v_even_w_out, v_odd_w_in, v_odd_ln_g, v_odd_ln_b, v_odd_w_s, v_odd_b_s, v_odd_w_out):
    S = x.shape[1]
    xs = x.reshape(S, D)
    tgt = loss_target.reshape(S, D)

    gpre0, gpre1 = pre_norm[0:1], pre_norm[1:2]
    gpost0, gpost1 = post_norm[0:1], post_norm[1:2]
    small_shard = _pack_small_shard(even_conv_w[0], odd_ln_g, odd_ln_b)
    h0, wie_g, small_g, woe_g, wio_b, woo_b = _prenorm_gather(xs, gpre0, even_w_in[0], small_shard, even_w_out[0],
                                                              odd_w_in[0], odd_w_out[0], TS_PRENORM)
    wie = jnp.transpose(wie_g, (1, 0, 2)).reshape(D, PROJ)
    wie_t = jnp.transpose(wie_g, (0, 2, 1)).reshape(PROJ, D)
    woe = woe_g.reshape(D, D)
    woe_t = woe.T
    conv_w = jnp.transpose(small_g[:, 0:3, 0:64], (1, 0, 2)).reshape(3, A_W)
    ln_g = small_g[:, 8, :].reshape(1, D)
    ln_b = small_g[:, 9, :].reshape(1, D)

    pool_w_b = even_pool_w[0].astype(BF16)
    pool_wt_b = jnp.swapaxes(even_pool_w[0], 1, 2).astype(BF16)
    ws_tril = jnp.tril(odd_w_s[0])
    ws_b = ws_tril.astype(BF16)
    wst_b = jnp.swapaxes(ws_tril, 1, 2).astype(BF16)
    b_full = jnp.broadcast_to(odd_b_s[0][:, :, None], (N_HEADS, GROUP, GROUP))

    p0, m0, x1, h1, wio_g, woo_g = _even_forward_fused(h0, xs, wie, conv_w, pool_w_b, even_pool_scale, woe, gpost0,
                                                       gpre1, TS_FWD, sides=[(_Gather, [wio_b, woo_b])])
    wio = jnp.transpose(wio_g, (1, 0, 2)).reshape(D, PROJ)
    wio_t = jnp.transpose(wio_g, (0, 2, 1)).reshape(PROJ, D)
    woo = woo_g.reshape(D, D)
    woo_t = woo.T
    p1, m1, g2, loss_vec = _odd_forward(h1, x1, tgt, wio, ln_g, ln_b, ws_b, b_full, woo, gpost1, TS_FWD)

    dp1, g1, dwoo, dws, dbs, dlg, dlb, dgpost1, dgpre1 = _odd_backward(
        g2, m1, p1, x1, ln_g, ln_b, ws_b, wst_b, b_full, woo_t, wio_t, gpost1, gpre1, TS_BWD)
    by_owner = lambda dwo: dwo.reshape(N_DEV, SHARD_OUT, D)
    (dwio_parts, dwoo_parts, dlg_parts, dlb_parts,
     dws_g, dbs_g, gpost1_g, gpre1_g, loss_g) = _weight_grad_scatter(
        h1, dp1, TS_WGRAD, "odd_weight_grad",
        sides=[(_Scatter, [by_owner(dwoo), dlg.reshape(N_DEV, 1, 128), dlb.reshape(N_DEV, 1, 128)]),
               (_Gather, [dws.astype(BF16), dbs, dgpost1, dgpre1, loss_vec])])
    dp0, gx, dwoe, dcw, dpw, dps, dgpost0, dgpre0 = _even_backward_fused(
        g1, m0, p0, xs, conv_w, pool_w_b, pool_wt_b, even_pool_scale, woe_t, wie_t, gpost0, gpre0, TS_FWD)
    dcw_by_owner = jnp.transpose(dcw[0:3].reshape(3, N_DEV, 64), (1, 0, 2))
    dwie_parts, dwoe_parts, dcw_parts, dpw_g, dps_g, gpost0_g, gpre0_g = _weight_grad_scatter(
        h0, dp0, TS_WGRAD, "even_weight_grad",
        sides=[(_Scatter, [by_owner(dwoe), dcw_by_owner]), (_Gather, [dpw.astype(BF16), dps, dgpost0, dgpre0])])

    ((g_wie, d_wie, nm_wie, nv_wie), (g_wio, d_wio, nm_wio, nv_wio), (g_woe, d_woe, nm_woe, nv_woe),
     (g_woo, d_woo, nm_woo, nv_woo)) = _sum_update(
        [dwie_parts, dwio_parts, dwoe_parts, dwoo_parts],
        [(even_w_in[0], m_even_w_in[0], v_even_w_in[0]), (odd_w_in[0], m_odd_w_in[0], v_odd_w_in[0]),
         (even_w_out[0], m_even_w_out[0], v_even_w_out[0]), (odd_w_out[0], m_odd_w_out[0], v_odd_w_out[0])])

    partials = [gpre0_g, gpre1_g, gpost0_g, gpost1_g, dpw_g, dps_g, dws_g, dbs_g, dcw_parts, dlg_parts, dlb_parts]
    row_of = [(0, 0), (0, 1), (1, 0), (1, 1), (2, 0), (3, 0), (4, 0), (5, 0), (6, 0), (7, 0), (8, 0)]
    weights = [(pre_norm, m_pre_norm, v_pre_norm), (post_norm, m_post_norm, v_post_norm),
               (even_pool_w[0], m_even_pool_w[0], v_even_pool_w[0]),
               (even_pool_scale, m_even_pool_scale, v_even_pool_scale),
               (odd_w_s[0], m_odd_w_s[0], v_odd_w_s[0]), (odd_b_s[0], m_odd_b_s[0], v_odd_b_s[0]),
               (even_conv_w[0], m_even_conv_w[0], v_even_conv_w[0]),
               (odd_ln_g, m_odd_ln_g, v_odd_ln_g), (odd_ln_b, m_odd_ln_b, v_odd_ln_b)]
    small = _small_sum_update(partials, row_of, weights, loss_g)
    loss = small[0][0, 0]

    def leaves(k, big):
        pre, post, pw, psc, ws, bs, cw, lg, lb = [small[1 + 4 * i + k] for i in range(len(weights))]
        wie_k, woe_k, wio_k, woo_k = big
        return [pre, post, wie_k[None], cw[None], pw[None], psc, woe_k[None], wio_k[None], lg, lb, ws[None], bs[None],
                woo_k[None]]

    outs = [loss, gx.reshape(1, S, D)]
    outs += leaves(0, (g_wie, g_woe, g_wio, g_woo))
    outs += leaves(1, (d_wie, d_woe, d_wio, d_woo))
    outs += leaves(2, (nm_wie, nm_woe, nm_wio, nm_woo))
    outs += leaves(3, (nv_wie, nv_woe, nv_wio, nv_woo))
    return tuple(outs)
```

```python
import jax
import jax.numpy as jnp
from jax import lax
from jax.experimental import pallas as pl
from jax.experimental.pallas import tpu as pltpu

F32 = jnp.float32
BF16 = jnp.bfloat16
MESH = pl.DeviceIdType.MESH

D = 1024
EPS = 1e-6
A_W = 512
B_W = 512
POOL_WINDOWS = (2, 4, 8, 16)
GROUP = 128
N_GROUPS = 4
N_HEADS = 8
HALO = 16
PROJ = 3072
N_DEV = 8
N_CHIPS = 4
SHARD_IN = PROJ // N_DEV
SHARD_OUT = D // N_DEV

ADAM_LR = 0.001
ADAM_B1 = 0.9
ADAM_B2 = 0.999
ADAM_EPS = 1e-08
ADAM_WD = 0.01
ADAM_STEP = 10

VMEM_LIMIT = 54 * 1024 * 1024
VMEM_LIMIT_ODD_BACKWARD = 57 * 1024 * 1024

TS_PRENORM = 1024
TS_FWD = 512
TS_BWD = 256
TS_WGRAD = 2048
SUB_ROWS = 256


def _params(n_grid=1, vmem=VMEM_LIMIT):
    return pltpu.CompilerParams(dimension_semantics=("arbitrary",) * n_grid, vmem_limit_bytes=vmem)


def _full(shape):
    return pl.BlockSpec(shape, lambda *_: (0,) * len(shape))


def _resident(shape):
    return pl.BlockSpec(shape, lambda *_: (0,) * len(shape), pipeline_mode=pl.Buffered(1))


def _sigmoid(z):
    return jax.nn.sigmoid(z)


def _dsilu(z, s):
    return s * (1.0 + z * (1.0 - s))


def _dot(a, b):
    return jnp.dot(a, b, preferred_element_type=F32)


def _dot_t0(a, b):
    return lax.dot_general(a, b, (((0,), (0,)), ((), ())), preferred_element_type=F32)


def _dot_t1(a, b):
    return lax.dot_general(a, b, (((1,), (1,)), ((), ())), preferred_element_type=F32)


def _rms(x):
    return lax.rsqrt(jnp.mean(x * x, axis=-1, keepdims=True) + EPS)


def _shift_down(a, k):
    return pltpu.roll(a, k, 0)


def _shift_up(a, k):
    return pltpu.roll(a, a.shape[0] - k, 0)


def _mesh_position():
    return lax.axis_index("x"), lax.axis_index("y"), lax.axis_index("c")


def _index(pos):
    return 4 * pos[0] + 2 * pos[1] + pos[2]


def _peer(pos, d):
    x, y, c = pos
    return (1 - x if d & 4 else x, 1 - y if d & 2 else y, 1 - c if d & 1 else c)


def _remote(src, dst, send_sems, recv_sems, k, to):
    return pltpu.make_async_remote_copy(src_ref=src, dst_ref=dst, send_sem=send_sems.at[k], recv_sem=recv_sems.at[k],
                                        device_id=to, device_id_type=MESH)


class _Gather:
    def __init__(self, srcs, dsts, send_sems, recv_sems):
        x, y, c = _mesh_position()
        n = len(srcs)
        me, sibling = (x, y, c), (x, y, 1 - c)
        chips = [(1 - x, y), (x, 1 - y), (1 - x, 1 - y)]

        def copy(k, t, block, to, src=None):
            slot = dsts[t].at[_index(block)]
            return _remote(slot if src is None else src, slot, send_sems, recv_sems, k * n + t, to)

        tensors = range(n)
        self.local = [pltpu.make_async_copy(srcs[t], dsts[t].at[_index(me)], recv_sems.at[7 * n + t]) for t in tensors]
        self.first = [copy(1 + j, t, me, chip + (c,), srcs[t]) for t in tensors for j, chip in enumerate(chips)]
        self.first += [copy(0, t, me, sibling, srcs[t]) for t in tensors]
        self.ici_in = [copy(1 + j, t, chip + (c,), me) for t in tensors for j, chip in enumerate(chips)]
        self.passed = [copy(4 + j, t, chip + (c,), sibling) for t in tensors for j, chip in enumerate(chips)]
        self.d2d_in = [copy(0, t, sibling, me) for t in tensors]
        self.d2d_in += [copy(4 + j, t, chip + (1 - c,), me) for t in tensors for j, chip in enumerate(chips)]

    def start(self):
        for cp in self.local + self.first:
            cp.start()

    def middle(self):
        for landed, onward in zip(self.ici_in, self.passed):
            landed.wait_recv()
            onward.start()

    def finish(self):
        for cp in self.d2d_in:
            cp.wait_recv()
        for cp in self.first + self.passed:
            cp.wait_send()
        for cp in self.local:
            cp.wait()


class _Scatter:
    def __init__(self, srcs, dsts, send_sems, recv_sems):
        pos = _mesh_position()
        n = len(srcs)
        self.local = [pltpu.make_async_copy(srcs[t].at[_index(pos)], dsts[t].at[0], recv_sems.at[t]) for t in range(n)]
        self.remote = []
        for d in range(1, N_DEV):
            to = _peer(pos, d)
            self.remote += [_remote(srcs[t].at[_index(to)], dsts[t].at[d], send_sems, recv_sems, d * n + t, to)
                            for t in range(n)]

    def start(self):
        for cp in self.local + self.remote:
            cp.start()

    def middle(self):
        pass

    def finish(self):
        for cp in self.local:
            cp.wait()
        for cp in self.remote:
            cp.wait_recv()
        for cp in self.remote:
            cp.wait_send()


HBM_SPEC = pl.BlockSpec(memory_space=pltpu.HBM)
MIDDLE_STEPS_BEFORE_END = 4


def _landing_shape(exchange, src):
    return (N_DEV,) + src.shape if exchange is _Gather else src.shape


def _call(body, *, name, grid, in_specs, out_specs, out_shape, scratch_shapes, args, sides=()):
    params = _params(len(grid))
    if not sides:
        return pl.pallas_call(body, name=name, grid=grid, in_specs=in_specs, out_specs=out_specs, out_shape=out_shape,
                              scratch_shapes=scratch_shapes, compiler_params=params)(*args)
    n_in, n_out, n_scratch = len(in_specs), len(out_specs), len(scratch_shapes)
    counts = [len(srcs) for _, srcs in sides]
    ns = sum(counts)
    total = 1
    for g in grid:
        total *= g
    side_args, side_shapes, side_sems = [], [], []
    for exchange, srcs in sides:
        side_args += list(srcs)
        side_shapes += [jax.ShapeDtypeStruct(_landing_shape(exchange, s), s.dtype) for s in srcs]
        side_sems += [pltpu.SemaphoreType.DMA((N_DEV * len(srcs),))] * 2

    def wrapped(*refs):
        ins, side_in = refs[:n_in], refs[n_in:n_in + ns]
        outs = refs[n_in + ns:n_in + ns + n_out]
        side_out = refs[n_in + ns + n_out:n_in + 2 * ns + n_out]
        rest = refs[n_in + 2 * ns + n_out:]
        scratch, sems = rest[:n_scratch], rest[n_scratch:]
        step = pl.program_id(0)
        for axis in range(1, len(grid)):
            step = step * grid[axis] + pl.program_id(axis)

        def exchanges():
            built, at = [], 0
            for k, (exchange, _) in enumerate(sides):
                built.append(exchange(side_in[at:at + counts[k]], side_out[at:at + counts[k]],
                                      sems[2 * k], sems[2 * k + 1]))
                at += counts[k]
            return built

        @pl.when(step == 0)
        def _():
            for ex in exchanges():
                ex.start()

        @pl.when(step == max(total - MIDDLE_STEPS_BEFORE_END, 0))
        def _():
            for ex in exchanges():
                ex.middle()

        body(*ins, *outs, *scratch)

        @pl.when(step == total - 1)
        def _():
            for ex in exchanges():
                ex.finish()

    return pl.pallas_call(
        wrapped, name=name, grid=grid, in_specs=list(in_specs) + [HBM_SPEC] * ns,
        out_specs=list(out_specs) + [HBM_SPEC] * ns, out_shape=list(out_shape) + side_shapes,
        scratch_shapes=list(scratch_shapes) + side_sems, compiler_params=params)(*args, *side_args)


def _prenorm_gather(x, g, wie, small, woe, wio, woo, ts):
    S = x.shape[0]
    nt = S // ts

    def body(x_ref, g_ref, wie_ref, sm_ref, woe_ref, wio_ref, woo_ref,
             h_ref, wie_out, sm_out, woe_out, wio_b, woo_b, cast_ref, small_ref, cast2_ref, send_sems, recv_sems):
        step = pl.program_id(0)
        gather = lambda: _Gather([cast_ref, small_ref, cast2_ref], [wie_out, sm_out, woe_out], send_sems, recv_sems)

        @pl.when(step == 0)
        def _():
            cast_ref[...] = wie_ref[...].astype(BF16)
            small_ref[...] = sm_ref[...]
            cast2_ref[...] = woe_ref[...].astype(BF16)
            gather().start()
            wio_b[...] = wio_ref[...].astype(BF16)
            woo_b[...] = woo_ref[...].astype(BF16)

        xv = x_ref[...]
        h_ref[...] = ((xv * _rms(xv)) * g_ref[...]).astype(BF16)

        @pl.when(step == nt - 1)
        def _():
            exchange = gather()
            exchange.middle()
            exchange.finish()

    tile = pl.BlockSpec((ts, D), lambda i: (i, 0))
    sems = pltpu.SemaphoreType.DMA((N_DEV * 3,))
    shards = [wie, small, woe, wio, woo]
    return pl.pallas_call(
        body, name="prenorm_gather", grid=(nt,),
        in_specs=[tile, _full((1, D))] + [_full(a.shape) for a in shards],
        out_specs=[tile, HBM_SPEC, HBM_SPEC, HBM_SPEC] + [_full(a.shape) for a in shards[3:]],
        out_shape=[jax.ShapeDtypeStruct((S, D), BF16), jax.ShapeDtypeStruct((N_DEV,) + wie.shape, BF16),
                   jax.ShapeDtypeStruct((N_DEV,) + small.shape, F32), jax.ShapeDtypeStruct((N_DEV,) + woe.shape, BF16)]
        + [jax.ShapeDtypeStruct(a.shape, BF16) for a in shards[3:]],
        scratch_shapes=[pltpu.VMEM(wie.shape, BF16), pltpu.VMEM(small.shape, F32), pltpu.VMEM(woe.shape, BF16),
                        sems, sems],
        compiler_params=_params(),
    )(x, g, wie, small, woe, wio, woo)


def _even_mixer(proj, halo, row0, cw_ref, pw_ref):
    xa, gb, gc, za, xp, zp = proj
    nr = xa.shape[0]
    ha = gc * xa
    ha_ext = jnp.concatenate([halo[0], ha], axis=0)
    ha_m1 = _shift_down(ha_ext, 1)[HALO:]
    ha_m2 = _shift_down(ha_ext, 2)[HALO:]
    conv = cw_ref[2:3, :] * ha + cw_ref[1:2, :] * ha_m1 + cw_ref[0:1, :] * ha_m2
    sig_a = _sigmoid(za)
    silu_a = za * sig_a

    xp_ext = jnp.concatenate([halo[1], xp], axis=0)
    pos = row0 + lax.broadcasted_iota(jnp.int32, (nr, 1), 0)
    pooled, inv_cnt, mixed = [], [], []
    for g, w in enumerate(POOL_WINDOWS):
        cols = slice(g * GROUP, (g + 1) * GROUP)
        s = xp_ext[:, cols]
        k = 1
        while k < w:
            s = s + _shift_down(s, k)
            k *= 2
        inv = 1.0 / jnp.minimum(pos + 1, w).astype(F32)
        pg = s[HALO:] * inv - xp[:, cols]
        pooled.append(pg)
        inv_cnt.append(inv)
        mixed.append(_dot(pg.astype(BF16), pw_ref[g]))
    mixed = jnp.concatenate(mixed, axis=1)
    sig_b = _sigmoid(zp)
    silu_b = zp * sig_b
    return dict(xa=xa, gb=gb, gc=gc, za=za, xp=xp, zp=zp, ha=ha, ha_m1=ha_m1, ha_m2=ha_m2, conv=conv,
                sig_a=sig_a, silu_a=silu_a, pooled=pooled, inv_cnt=inv_cnt, mixed=mixed, sig_b=sig_b,
                silu_b=silu_b)


def _columns(a):
    return [a[:, k * A_W:(k + 1) * A_W].astype(F32) for k in range(6)]


def _even_forward_fused(h, x, win_b, cw, pw_b, ps, wout_b, gpost, gpre_next, ts, sides=()):
    S = x.shape[0]

    def body(h_ref, x_ref, wi_ref, cw_ref, pw_ref, ps_ref, wo_ref, gp_ref, gn_ref,
             p_ref, m_ref, x1_ref, h1_ref, mix_ref, carry_ref):
        i = pl.program_id(0)

        @pl.when(i == 0)
        def _():
            carry_ref[...] = jnp.zeros_like(carry_ref)

        for r0 in range(0, ts, SUB_ROWS):
            rows = slice(r0, r0 + SUB_ROWS)
            proj = _dot(h_ref[rows, :], wi_ref[...])
            p_ref[rows, :] = proj.astype(BF16)
            fw = _even_mixer(_columns(proj), (carry_ref[:, 0:A_W], carry_ref[:, A_W:D]), i * ts + r0, cw_ref, pw_ref)
            carry_ref[:, 0:A_W] = fw["ha"][SUB_ROWS - HALO:]
            carry_ref[:, A_W:D] = fw["xp"][SUB_ROWS - HALO:]
            mix_ref[rows, 0:A_W] = (fw["gb"] * fw["conv"] * fw["silu_a"]).astype(BF16)
            mix_ref[rows, A_W:D] = (fw["mixed"] * ps_ref[...] * fw["silu_b"]).astype(BF16)
            m = _dot(mix_ref[rows, :], wo_ref[...])
            m_ref[rows, :] = m.astype(BF16)
            x1 = x_ref[rows, :] + (m * _rms(m)) * gp_ref[...]
            x1_ref[rows, :] = x1
            h1_ref[rows, :] = ((x1 * _rms(x1)) * gn_ref[...]).astype(BF16)

    tile = pl.BlockSpec((ts, D), lambda i: (i, 0))
    return _call(
        body, name="even_forward", grid=(S // ts,),
        in_specs=[tile, tile, _resident((D, PROJ)), _full((3, A_W)), _full((N_GROUPS, GROUP, GROUP)), _full((1, B_W)),
                  _resident((D, D)), _full((1, D)), _full((1, D))],
        out_specs=[pl.BlockSpec((ts, PROJ), lambda i: (i, 0)), tile, tile, tile],
        out_shape=[jax.ShapeDtypeStruct((S, PROJ), BF16), jax.ShapeDtypeStruct((S, D), BF16),
                   jax.ShapeDtypeStruct((S, D), F32), jax.ShapeDtypeStruct((S, D), BF16)],
        scratch_shapes=[pltpu.VMEM((ts, D), BF16), pltpu.VMEM((HALO, D), F32)],
        args=(h, x, win_b, cw, pw_b, ps, wout_b, gpost, gpre_next), sides=sides)


def _even_backward_fused(g1, m0, p, x, cw, pw_b, pwt_b, ps, woutt_b, wint_b, gpost, gpre, ts):
    S = g1.shape[0]
    nt = S // ts

    def body(g_ref, m_ref, p_ref, halo_ref, x_ref, cw_ref, pw_ref, pwt_ref, ps_ref, wot_ref, wit_ref, gp_ref, gn_ref,
             dp_ref, dx_ref, dwob_ref, dcw_ref, dpw_ref, dps_ref, dgp_ref, dgn_ref,
             carry_ref, mix_ref, dm_ref, dwo_ref):
        step = pl.program_id(0)
        i = nt - 1 - step

        @pl.when(step == 0)
        def _():
            carry_ref[...] = jnp.zeros_like(carry_ref)
            dwo_ref[...] = jnp.zeros_like(dwo_ref)
            dcw_ref[...] = jnp.zeros_like(dcw_ref)
            dpw_ref[...] = jnp.zeros_like(dpw_ref)
            dps_ref[...] = jnp.zeros_like(dps_ref)
            dgp_ref[...] = jnp.zeros_like(dgp_ref)
            dgn_ref[...] = jnp.zeros_like(dgn_ref)

        ps_v = ps_ref[...]
        sums = [None] * 6
        add = lambda k, part: sums.__setitem__(k, part if sums[k] is None else sums[k] + part)
        for r0 in range(ts - SUB_ROWS, -1, -SUB_ROWS):
            rows = slice(r0, r0 + SUB_ROWS)
            g = g_ref[rows, :]
            m = m_ref[rows, :].astype(F32)
            q = _rms(m)
            n = m * q
            add(0, jnp.sum(g * n, axis=0, keepdims=True))
            dn = g * gp_ref[...]
            dm = q * (dn - n * jnp.mean(dn * n, axis=-1, keepdims=True))
            dm_ref[rows, :] = dm.astype(BF16)
            dmix = _dot(dm_ref[rows, :], wot_ref[...])
            dya = dmix[:, 0:A_W]
            dyb = dmix[:, A_W:D]

            if r0 == 0:
                before = _columns(halo_ref[...])
                keep = jnp.where(i == 0, 0.0, 1.0).astype(F32)
                halo = (before[2] * before[0] * keep, before[4] * keep)
            else:
                before = _columns(p_ref[r0 - HALO:r0, :])
                halo = (before[2] * before[0], before[4])
            fw = _even_mixer(_columns(p_ref[rows, :]), halo, i * ts + r0, cw_ref, pw_ref)
            mix_ref[rows, 0:A_W] = (fw["gb"] * fw["conv"] * fw["silu_a"]).astype(BF16)
            mix_ref[rows, A_W:D] = (fw["mixed"] * ps_v * fw["silu_b"]).astype(BF16)

            t = dya * fw["gb"]
            dconv = t * fw["silu_a"]
            dgb = dya * fw["conv"] * fw["silu_a"]
            dza = t * fw["conv"] * _dsilu(fw["za"], fw["sig_a"])
            add(2, jnp.sum(dconv * fw["ha"], axis=0, keepdims=True))
            add(3, jnp.sum(dconv * fw["ha_m1"], axis=0, keepdims=True))
            add(4, jnp.sum(dconv * fw["ha_m2"], axis=0, keepdims=True))
            dconv_ext = jnp.concatenate([dconv, carry_ref[:, 0:A_W]], axis=0)
            dha = (cw_ref[2:3, :] * dconv + cw_ref[1:2, :] * _shift_up(dconv_ext, 1)[:SUB_ROWS]
                   + cw_ref[0:1, :] * _shift_up(dconv_ext, 2)[:SUB_ROWS])
            dgc = dha * fw["xa"]
            dxa = dha * fw["gc"]

            u = dyb * fw["mixed"]
            add(1, jnp.sum(u * fw["silu_b"], axis=0, keepdims=True))
            dzp = u * ps_v * _dsilu(fw["zp"], fw["sig_b"])
            dmixed = (dyb * ps_v * fw["silu_b"]).astype(BF16)
            dxp, e_first = [], []
            for gi, w in enumerate(POOL_WINDOWS):
                cols = slice(gi * GROUP, (gi + 1) * GROUP)
                dmg = dmixed[:, cols]
                dpooled = _dot(dmg, pwt_ref[gi])
                dpw_ref[gi] += _dot_t0(fw["pooled"][gi].astype(BF16), dmg)
                e = dpooled * fw["inv_cnt"][gi]
                e_first.append(e[0:HALO])
                s = jnp.concatenate([e, carry_ref[:, A_W + gi * GROUP:A_W + (gi + 1) * GROUP]], axis=0)
                k = 1
                while k < w:
                    s = s + _shift_up(s, k)
                    k *= 2
                dxp.append(s[:SUB_ROWS] - dpooled)
            carry_ref[:, 0:A_W] = dconv[0:HALO]
            carry_ref[:, A_W:D] = jnp.concatenate(e_first, axis=1)

            dp_ref[rows, 0:512] = dxa.astype(BF16)
            dp_ref[rows, 512:1024] = dgb.astype(BF16)
            dp_ref[rows, 1024:1536] = dgc.astype(BF16)
            dp_ref[rows, 1536:2048] = dza.astype(BF16)
            dp_ref[rows, 2048:2560] = jnp.concatenate(dxp, axis=1).astype(BF16)
            dp_ref[rows, 2560:3072] = dzp.astype(BF16)

            dh = _dot(dp_ref[rows, :], wit_ref[...])
            xv = x_ref[rows, :]
            r = _rms(xv)
            xn = xv * r
            add(5, jnp.sum(dh * xn, axis=0, keepdims=True))
            dxn = dh * gn_ref[...]
            dx_ref[rows, :] = g + r * (dxn - xn * jnp.mean(dxn * xn, axis=-1, keepdims=True))

        dwo_ref[...] += _dot_t0(mix_ref[...], dm_ref[...])
        dgp_ref[...] += sums[0]
        dps_ref[...] += sums[1]
        dcw_ref[2:3, :] += sums[2]
        dcw_ref[1:2, :] += sums[3]
        dcw_ref[0:1, :] += sums[4]
        dgn_ref[...] += sums[5]

        @pl.when(step == nt - 1)
        def _():
            dwob_ref[...] = dwo_ref[...].astype(BF16)

    rev = lambda s: (nt - 1 - s, 0)
    tile = pl.BlockSpec((ts, D), rev)
    wide = pl.BlockSpec((ts, PROJ), rev)
    bpt = ts // HALO
    halo_map = lambda s: (jnp.maximum((nt - 1 - s) * bpt - 1, 0), 0)
    groups = _full((N_GROUPS, GROUP, GROUP))
    vec = _full((1, D))
    return pl.pallas_call(
        body, name="even_backward", grid=(nt,),
        in_specs=[tile, tile, wide, pl.BlockSpec((HALO, PROJ), halo_map), tile, _full((3, A_W)), groups, groups,
                  _full((1, B_W)), _resident((D, D)), _resident((PROJ, D)), vec, vec],
        out_specs=[wide, tile, _full((D, D)), _full((8, A_W)), groups, _full((1, B_W)), vec, vec],
        out_shape=[jax.ShapeDtypeStruct((S, PROJ), BF16), jax.ShapeDtypeStruct((S, D), F32),
                   jax.ShapeDtypeStruct((D, D), BF16), jax.ShapeDtypeStruct((8, A_W), F32),
                   jax.ShapeDtypeStruct((N_GROUPS, GROUP, GROUP), F32), jax.ShapeDtypeStruct((1, B_W), F32),
                   jax.ShapeDtypeStruct((1, D), F32), jax.ShapeDtypeStruct((1, D), F32)],
        scratch_shapes=[pltpu.VMEM((HALO, D), F32), pltpu.VMEM((ts, D), BF16), pltpu.VMEM((ts, D), BF16),
                        pltpu.VMEM((D, D), F32)],
        compiler_params=_params(),
    )(g1, m0, p, p, x, cw, pw_b, pwt_b, ps, woutt_b, wint_b, gpost, gpre)


def _gate_matmuls(w_ref, in_ref, out_ref, chunks, bias_ref=None):
    for h in range(N_HEADS):
        cols = slice(h * GROUP, (h + 1) * GROUP)
        wide = jnp.concatenate([in_ref[chunk, cols] for chunk in chunks], axis=1)
        res = _dot(w_ref[h], wide)
        for k, chunk in enumerate(chunks):
            part = res[:, k * GROUP:(k + 1) * GROUP]
            out_ref[chunk, cols] = part if bias_ref is None else part + bias_ref[h]


def _odd_forward_parts(v, lg_ref, lb_ref, wt_ref, bfull_ref, vln_ref, sv_ref, r0, nr):
    rows = slice(r0, r0 + nr)
    mu = jnp.mean(v, axis=-1, keepdims=True)
    vc = v - mu
    rstd = lax.rsqrt(jnp.mean(vc * vc, axis=-1, keepdims=True) + EPS)
    vh = vc * rstd
    vln_ref[rows, :] = (vh * lg_ref[...] + lb_ref[...]).astype(BF16)
    _gate_matmuls(wt_ref, vln_ref, sv_ref, [slice(r0 + k * GROUP, r0 + (k + 1) * GROUP) for k in range(nr // GROUP)],
                  bfull_ref)
    return vh, rstd


def _odd_forward(h, x1, tgt, win_b, lg, lb, wt_b, bfull, wout_b, gpost, ts):
    S = x1.shape[0]

    def body(h_ref, x_ref, t_ref, wi_ref, lg_ref, lb_ref, wt_ref, bfull_ref, wo_ref, gp_ref,
             p_ref, m_ref, g_ref, loss_ref, vln_ref, sv_ref, y_ref):
        @pl.when(pl.program_id(0) == 0)
        def _():
            loss_ref[...] = jnp.zeros_like(loss_ref)

        loss = None
        for r0 in range(0, ts, SUB_ROWS):
            rows = slice(r0, r0 + SUB_ROWS)
            proj = _dot(h_ref[rows, :], wi_ref[...])
            p_ref[rows, :] = proj.astype(BF16)
            u, v, z = proj[:, 0:D], proj[:, D:2 * D], proj[:, 2 * D:3 * D]
            _odd_forward_parts(v, lg_ref, lb_ref, wt_ref, bfull_ref, vln_ref, sv_ref, r0, SUB_ROWS)
            y_ref[rows, :] = (u * sv_ref[rows, :] * (z * _sigmoid(z))).astype(BF16)
            m = _dot(y_ref[rows, :], wo_ref[...])
            m_ref[rows, :] = m.astype(BF16)
            x2 = x_ref[rows, :] + (m * _rms(m)) * gp_ref[...]
            err = x2 - t_ref[rows, :]
            g_ref[rows, :] = err * (1.0 / D)
            part = jnp.sum(err * err, axis=0, keepdims=True)
            loss = part if loss is None else loss + part
        loss_ref[...] += loss

    tile = pl.BlockSpec((ts, D), lambda i: (i, 0))
    small = _full((N_HEADS, GROUP, GROUP))
    return pl.pallas_call(
        body, name="odd_forward", grid=(S // ts,),
        in_specs=[pl.BlockSpec((ts, D), lambda i: (i, 0)), tile, tile, _full((D, PROJ)), _full((1, D)), _full((1, D)),
                  small, small, _full((D, D)), _full((1, D))],
        out_specs=[pl.BlockSpec((ts, PROJ), lambda i: (i, 0)), tile, tile, _full((1, D))],
        out_shape=[jax.ShapeDtypeStruct((S, PROJ), BF16), jax.ShapeDtypeStruct((S, D), BF16),
                   jax.ShapeDtypeStruct((S, D), F32), jax.ShapeDtypeStruct((1, D), F32)],
        scratch_shapes=[pltpu.VMEM((ts, D), BF16), pltpu.VMEM((ts, D), F32), pltpu.VMEM((ts, D), BF16)],
        compiler_params=_params(),
    )(h, x1, tgt, win_b, lg, lb, wt_b, bfull, wout_b, gpost)


def _odd_backward(g2, m1, p, x1, lg, lb, wt_b, wtt_b, bfull, woutt_b, wint_b, gpost, gpre, ts):
    S = g2.shape[0]
    nt = S // ts
    sub = ts // 2

    def body(g_ref, m_ref, p_ref, x_ref, lg_ref, lb_ref, wt_ref, wtt_ref, bfull_ref, wot_ref, wit_ref, gp_ref, gn_ref,
             dp_ref, dx_ref, dwob_ref, dws_ref, dbs_ref, dlg_ref, dlb_ref, dgp_ref, dgn_ref,
             vln_ref, sv_ref, dsvb_ref, dvln_ref, dsum_ref, y_ref, dm_ref, dwo_ref):
        step = pl.program_id(0)

        @pl.when(step == 0)
        def _():
            dwo_ref[...] = jnp.zeros_like(dwo_ref)
            dws_ref[...] = jnp.zeros_like(dws_ref)
            dsum_ref[...] = jnp.zeros_like(dsum_ref)
            dlg_ref[...] = jnp.zeros_like(dlg_ref)
            dlb_ref[...] = jnp.zeros_like(dlb_ref)
            dgp_ref[...] = jnp.zeros_like(dgp_ref)
            dgn_ref[...] = jnp.zeros_like(dgn_ref)

        tril = (lax.broadcasted_iota(jnp.int32, (GROUP, GROUP), 0)
                >= lax.broadcasted_iota(jnp.int32, (GROUP, GROUP), 1))
        sums = [None] * 4
        add = lambda k, part: sums.__setitem__(k, part if sums[k] is None else sums[k] + part)

        def post_norm(c):
            g = g_ref[c["rows"], :]
            m = m_ref[c["rows"], :].astype(F32)
            q = _rms(m)
            n = m * q
            add(0, jnp.sum(g * n, axis=0, keepdims=True))
            dn = g * gp_ref[...]
            dm = q * (dn - n * jnp.mean(dn * n, axis=-1, keepdims=True))
            dm_ref[c["rows"], :] = dm.astype(BF16)

        def out_projection(c):
            c["dy"] = _dot(dm_ref[c["rows"], :], wot_ref[...])

        def layernorm(c):
            v = p_ref[c["rows"], D:2 * D].astype(F32)
            mu = jnp.mean(v, axis=-1, keepdims=True)
            vc = v - mu
            c["rstd"] = lax.rsqrt(jnp.mean(vc * vc, axis=-1, keepdims=True) + EPS)
            c["vh"] = vc * c["rstd"]
            vln_ref[c["rows"], :] = (c["vh"] * lg_ref[...] + lb_ref[...]).astype(BF16)

        def gate_matmuls(c):
            _gate_matmuls(wt_ref, vln_ref, sv_ref, c["chunks"], bfull_ref)

        def gating(c):
            rows = c["rows"]
            u = p_ref[rows, 0:D].astype(F32)
            z = p_ref[rows, 2 * D:3 * D].astype(F32)
            sv = sv_ref[rows, :]
            sig = _sigmoid(z)
            sz = z * sig
            y_ref[rows, :] = (u * sv * sz).astype(BF16)
            dy = c.pop("dy")
            t = dy * sz
            dsv = t * u
            dsvb_ref[rows, :] = dsv.astype(BF16)
            for k in range(sub // GROUP):
                dsum_ref[...] += dsv[k * GROUP:(k + 1) * GROUP]
            dp_ref[rows, 0:D] = (t * sv).astype(BF16)
            dp_ref[rows, 2 * D:3 * D] = (dy * u * sv * _dsilu(z, sig)).astype(BF16)

        def gate_backward_matmuls(c):
            for chunk in c["chunks"]:
                for h in range(N_HEADS):
                    cols = slice(h * GROUP, (h + 1) * GROUP)
                    dvln_ref[chunk, cols] = _dot(wtt_ref[h], dsvb_ref[chunk, cols])
                    dws_ref[h] += jnp.where(tril, _dot_t1(dsvb_ref[chunk, cols], vln_ref[chunk, cols]), 0.0)

        def layernorm_backward(c):
            vh, rstd = c.pop("vh"), c.pop("rstd")
            dvln = dvln_ref[c["rows"], :]
            add(1, jnp.sum(dvln * vh, axis=0, keepdims=True))
            add(2, jnp.sum(dvln, axis=0, keepdims=True))
            dvh = dvln * lg_ref[...]
            dv = rstd * (dvh - jnp.mean(dvh, axis=-1, keepdims=True)
                         - vh * jnp.mean(dvh * vh, axis=-1, keepdims=True))
            dp_ref[c["rows"], D:2 * D] = dv.astype(BF16)

        def in_projection(c):
            c["dh"] = _dot(dp_ref[c["rows"], :], wit_ref[...])

        def pre_norm(c):
            dh = c.pop("dh")
            xv = x_ref[c["rows"], :]
            r = _rms(xv)
            xn = xv * r
            add(3, jnp.sum(dh * xn, axis=0, keepdims=True))
            dxn = dh * gn_ref[...]
            dx_ref[c["rows"], :] = g_ref[c["rows"], :] + r * (dxn - xn * jnp.mean(dxn * xn, axis=-1, keepdims=True))

        phases = [post_norm, out_projection, layernorm, gate_matmuls, gating, gate_backward_matmuls,
                  layernorm_backward, in_projection, pre_norm]
        groups = [dict(rows=slice(r0, r0 + sub),
                       chunks=[slice(r0 + k * GROUP, r0 + (k + 1) * GROUP) for k in range(sub // GROUP)])
                  for r0 in range(0, ts, sub)]
        for group in groups:
            for phase in phases:
                phase(group)

        dwo_ref[...] += _dot_t0(y_ref[...], dm_ref[...])
        dgp_ref[...] += sums[0]
        dlg_ref[...] += sums[1]
        dlb_ref[...] += sums[2]
        dgn_ref[...] += sums[3]

        @pl.when(step == nt - 1)
        def _():
            dwob_ref[...] = dwo_ref[...].astype(BF16)
            ones = jnp.ones((8, GROUP), F32)
            for h in range(N_HEADS):
                cols = slice(h * GROUP, (h + 1) * GROUP)
                total = lax.dot_general(ones, dsum_ref[:, cols], (((1,), (1,)), ((), ())),
                                        precision=lax.Precision.HIGHEST, preferred_element_type=F32)
                dbs_ref[h:h + 1, :] = total[0:1, :]

    tile = pl.BlockSpec((ts, D), lambda i: (i, 0))
    wide = pl.BlockSpec((ts, PROJ), lambda i: (i, 0))
    small = _full((N_HEADS, GROUP, GROUP))
    heads = _resident((N_HEADS, GROUP, GROUP))
    vec = _full((1, D))
    return pl.pallas_call(
        body, name="odd_backward", grid=(nt,),
        in_specs=[tile, tile, wide, tile, vec, vec, heads, heads, heads, _resident((D, D)), _resident((PROJ, D)),
                  vec, vec],
        out_specs=[wide, tile, _full((D, D)), small, _full((N_HEADS, GROUP)), vec, vec, vec, vec],
        out_shape=[jax.ShapeDtypeStruct((S, PROJ), BF16), jax.ShapeDtypeStruct((S, D), F32),
                   jax.ShapeDtypeStruct((D, D), BF16),
                   jax.ShapeDtypeStruct((N_HEADS, GROUP, GROUP), F32), jax.ShapeDtypeStruct((N_HEADS, GROUP), F32),
                   jax.ShapeDtypeStruct((1, D), F32), jax.ShapeDtypeStruct((1, D), F32),
                   jax.ShapeDtypeStruct((1, D), F32), jax.ShapeDtypeStruct((1, D), F32)],
        scratch_shapes=[pltpu.VMEM((ts, D), BF16), pltpu.VMEM((ts, D), F32), pltpu.VMEM((ts, D), BF16),
                        pltpu.VMEM((ts, D), F32), pltpu.VMEM((GROUP, D), F32), pltpu.VMEM((ts, D), BF16),
                        pltpu.VMEM((ts, D), BF16), pltpu.VMEM((D, D), F32)],
        compiler_params=_params(1, VMEM_LIMIT_ODD_BACKWARD),
    )(g2, m1, p, x1, lg, lb, wt_b, wtt_b, bfull, woutt_b, wint_b, gpost, gpre)


def _weight_grad_scatter(h, dp, ts, name, sides=()):
    S = h.shape[0]
    ts = min(ts, S)
    nt = S // ts
    panel = 2 * SHARD_IN
    last = N_CHIPS - 1

    def body(h_ref, dp_ref, parts_ref, acc_ref, half_ref, swap_ref, sum_ref, send_sems, recv_sems):
        s, t = pl.program_id(0), pl.program_id(1)
        x, y, c = _mesh_position()
        sibling = (x, y, 1 - c)

        def swap(k):
            return _remote(half_ref.at[k, 1], swap_ref.at[k], send_sems, recv_sems, k, sibling)

        def chip_sum(k):
            flip = last - k
            to = (x ^ (flip >> 1), y ^ (flip & 1), c)
            return _remote(sum_ref.at[k], parts_ref.at[1 + flip], send_sems, recv_sems, 3 + flip, to)

        own = pltpu.make_async_copy(half_ref.at[last, 0], parts_ref.at[0], recv_sems.at[7])
        to_sibling = _remote(half_ref.at[last, 1], parts_ref.at[1], send_sems, recv_sems, 3, sibling)

        @pl.when(t == 0)
        def _():
            acc_ref[...] = jnp.zeros_like(acc_ref)

        for k in range(last):
            @pl.when((s == k + 1) & (t == 0))
            def _():
                swap(k).wait_recv()
                sum_ref[k] = (half_ref[k, 0].astype(F32) + swap_ref[k].astype(F32)).astype(BF16)
                chip_sum(k).start()

        acc_ref[...] += _dot_t0(h_ref[...], dp_ref[...])

        @pl.when(t == nt - 1)
        def _():
            for core in range(2):
                @pl.when(c == core)
                def _():
                    half_ref[s, core] = acc_ref[:, 0:SHARD_IN].astype(BF16)
                    half_ref[s, 1 - core] = acc_ref[:, SHARD_IN:panel].astype(BF16)
            for k in range(last):
                @pl.when(s == k)
                def _():
                    swap(k).start()

        @pl.when((s == last) & (t == nt - 1))
        def _():
            own.start()
            to_sibling.start()
            own.wait()
            _remote(half_ref.at[last, 1], parts_ref.at[1], send_sems, recv_sems, 3, sibling).wait_recv()
            for k in range(last):
                chip_sum(k).wait_recv()
            to_sibling.wait_send()
            for k in range(last):
                swap(k).wait_send()
                chip_sum(k).wait_send()

    chip_panel = lambda s, t: (t, (2 * lax.axis_index("x") + lax.axis_index("y")) ^ (last - s))
    sems = pltpu.SemaphoreType.DMA((8,))
    return _call(
        body, name=name, grid=(N_CHIPS, nt),
        in_specs=[pl.BlockSpec((ts, D), lambda s, t: (t, 0)), pl.BlockSpec((ts, panel), chip_panel)],
        out_specs=[HBM_SPEC], out_shape=[jax.ShapeDtypeStruct((N_CHIPS + 1, D, SHARD_IN), BF16)],
        scratch_shapes=[pltpu.VMEM((D, panel), F32), pltpu.VMEM((N_CHIPS, 2, D, SHARD_IN), BF16),
                        pltpu.VMEM((last, D, SHARD_IN), BF16), pltpu.VMEM((last, D, SHARD_IN), BF16), sems, sems],
        args=(h, dp), sides=sides)


def _adamw(w, g, m, v):
    m = ADAM_B1 * m + (1.0 - ADAM_B1) * g
    v = ADAM_B2 * v + (1.0 - ADAM_B2) * (g * g)
    m_hat = m / (1.0 - ADAM_B1 ** ADAM_STEP)
    v_hat = v / (1.0 - ADAM_B2 ** ADAM_STEP)
    delta = -ADAM_LR * (m_hat / (jnp.sqrt(v_hat) + ADAM_EPS) + ADAM_WD * w)
    return delta, m, v


UPDATE_STEPS = 4


def _sum_update(parts, weights):
    n_w = len(weights)

    def body(*refs):
        p_refs, wmv_refs, outs = refs[:n_w], refs[n_w:4 * n_w], refs[4 * n_w:]
        for i in range(n_w):
            acc = p_refs[i][0].astype(F32)
            for d in range(1, parts[i].shape[0]):
                acc = acc + p_refs[i][d].astype(F32)
            w_ref, m_ref, v_ref = wmv_refs[3 * i:3 * i + 3]
            g_out, d_out, m_out, v_out = outs[4 * i:4 * i + 4]
            g_out[...] = acc
            delta, m_new, v_new = _adamw(w_ref[...], acc, m_ref[...], v_ref[...])
            d_out[...] = delta
            m_out[...] = m_new
            v_out[...] = v_new

    part_specs, wmv_specs, out_specs, out_shape = [], [], [], []
    for p, (w, _, _) in zip(parts, weights):
        R, C = w.shape
        rb = R // UPDATE_STEPS
        blk = pl.BlockSpec((rb, C), lambda i: (i, 0))
        part_specs.append(pl.BlockSpec((p.shape[0], rb, C), lambda i: (0, i, 0)))
        wmv_specs += [blk] * 3
        out_specs += [blk] * 4
        out_shape += [jax.ShapeDtypeStruct((R, C), F32)] * 4
    flat = pl.pallas_call(
        body, name="update_large_weights", grid=(UPDATE_STEPS,), in_specs=part_specs + wmv_specs,
        out_specs=out_specs, out_shape=out_shape, compiler_params=_params(),
    )(*parts, *[a for wmv in weights for a in wmv])
    return [flat[4 * i:4 * i + 4] for i in range(n_w)]


def _small_sum_update(partials, row_of, weights, loss_parts):
    n_p, n_w = len(partials), len(weights)

    def body(*refs):
        p_refs = refs[:n_p]
        loss_ref = refs[n_p]
        w_refs = refs[n_p + 1:n_p + 1 + 3 * n_w]
        loss_out = refs[n_p + 1 + 3 * n_w]
        outs = refs[n_p + 2 + 3 * n_w:]

        def total(ref):
            acc = ref[0].astype(F32)
            for d in range(1, N_DEV):
                acc = acc + ref[d].astype(F32)
            return acc

        lsum = jnp.sum(total(loss_ref), axis=-1, keepdims=True) * (0.5 / D)
        loss_out[...] = jnp.broadcast_to(lsum, loss_out.shape)
        for k in range(n_p):
            i, r0 = row_of[k]
            g = total(p_refs[k])
            w_ref, m_ref, v_ref = w_refs[3 * i:3 * i + 3]
            g_out, d_out, m_out, v_out = outs[4 * i:4 * i + 4]
            if g.ndim == 2:
                at = (slice(r0, r0 + g.shape[0]), slice(None))
            else:
                at = (slice(None),) * g.ndim
            delta, m_new, v_new = _adamw(w_ref[at], g, m_ref[at], v_ref[at])
            g_out[at] = g
            d_out[at] = delta
            m_out[at] = m_new
            v_out[at] = v_new

    vm = pl.BlockSpec(memory_space=pltpu.VMEM)
    flat_w = [a for wmv in weights for a in wmv]
    out_shape = [jax.ShapeDtypeStruct((1, 128), F32)]
    for w, _, _ in weights:
        out_shape += [jax.ShapeDtypeStruct(w.shape, F32)] * 4
    return pl.pallas_call(
        body, name="small_update", in_specs=[vm] * (n_p + 1 + 3 * n_w), out_specs=[vm] * len(out_shape),
        out_shape=out_shape, compiler_params=pltpu.CompilerParams(vmem_limit_bytes=VMEM_LIMIT),
    )(*partials, loss_parts, *flat_w)


def _pack_small_shard(cw, lg, lb):
    return (jnp.pad(cw, ((0, 13), (0, 64))) + jnp.pad(lg, ((8, 7), (0, 0))) + jnp.pad(lb, ((9, 6), (0, 0))))


def kernel(x, pre_norm, post_norm, even_w_in, even_conv_w, even_pool_w, even_pool_scale, even_w_out, odd_w_in, odd_ln_g, odd_ln_b, odd_w_s, odd_b_s, odd_w_out, loss_target, m_pre_norm, m_post_norm, m_even_w_in, m_even_conv_w, m_even_pool_w, m_even_pool_scale, m_even_w_out, m_odd_w_in, m_odd_ln_g, m_odd_ln_b, m_odd_w_s, m_odd_b_s, m_odd_w_out, v_pre_norm, v_post_norm, v_even_w_in, v_even_conv_w, v_even_pool_w, v_even_pool_scale, v_even_w_out, v_odd_w_in, v_odd_ln_g, v_odd_ln_b, v_odd_w_s, v_odd_b_s, v_odd_w_out):
    S = x.shape[1]
    xs = x.reshape(S, D)
    tgt = loss_target.reshape(S, D)

    gpre0, gpre1 = pre_norm[0:1], pre_norm[1:2]
    gpost0, gpost1 = post_norm[0:1], post_norm[1:2]
    small_shard = _pack_small_shard(even_conv_w[0], odd_ln_g, odd_ln_b)
    h0, wie_g, small_g, woe_g, wio_b, woo_b = _prenorm_gather(xs, gpre0, even_w_in[0], small_shard, even_w_out[0],
                                                              odd_w_in[0], odd_w_out[0], TS_PRENORM)
    wie = jnp.transpose(wie_g, (1, 0, 2)).reshape(D, PROJ)
    wie_t = jnp.transpose(wie_g, (0, 2, 1)).reshape(PROJ, D)
    woe = woe_g.reshape(D, D)
    woe_t = woe.T
    conv_w = jnp.transpose(small_g[:, 0:3, 0:64], (1, 0, 2)).reshape(3, A_W)
    ln_g = small_g[:, 8, :].reshape(1, D)
    ln_b = small_g[:, 9, :].reshape(1, D)

    pool_w_b = even_pool_w[0].astype(BF16)
    pool_wt_b = jnp.swapaxes(even_pool_w[0], 1, 2).astype(BF16)
    ws_tril = jnp.tril(odd_w_s[0])
    ws_b = ws_tril.astype(BF16)
    wst_b = jnp.swapaxes(ws_tril, 1, 2).astype(BF16)
    b_full = jnp.broadcast_to(odd_b_s[0][:, :, None], (N_HEADS, GROUP, GROUP))

    p0, m0, x1, h1, wio_g, woo_g = _even_forward_fused(h0, xs, wie, conv_w, pool_w_b, even_pool_scale, woe, gpost0,
                                                       gpre1, TS_FWD, sides=[(_Gather, [wio_b, woo_b])])
    wio = jnp.transpose(wio_g, (1, 0, 2)).reshape(D, PROJ)
    wio_t = jnp.transpose(wio_g, (0, 2, 1)).reshape(PROJ, D)
    woo = woo_g.reshape(D, D)
    woo_t = woo.T
    p1, m1, g2, loss_vec = _odd_forward(h1, x1, tgt, wio, ln_g, ln_b, ws_b, b_full, woo, gpost1, TS_FWD)

    dp1, g1, dwoo, dws, dbs, dlg, dlb, dgpost1, dgpre1 = _odd_backward(
        g2, m1, p1, x1, ln_g, ln_b, ws_b, wst_b, b_full, woo_t, wio_t, gpost1, gpre1, TS_FWD)
    by_owner = lambda dwo: dwo.reshape(N_DEV, SHARD_OUT, D)
    (dwio_parts, dwoo_parts, dlg_parts, dlb_parts,
     dws_g, dbs_g, gpost1_g, gpre1_g, loss_g) = _weight_grad_scatter(
        h1, dp1, TS_WGRAD, "odd_weight_grad",
        sides=[(_Scatter, [by_owner(dwoo), dlg.reshape(N_DEV, 1, 128), dlb.reshape(N_DEV, 1, 128)]),
               (_Gather, [dws.astype(BF16), dbs, dgpost1, dgpre1, loss_vec])])
    dp0, gx, dwoe, dcw, dpw, dps, dgpost0, dgpre0 = _even_backward_fused(
        g1, m0, p0, xs, conv_w, pool_w_b, pool_wt_b, even_pool_scale, woe_t, wie_t, gpost0, gpre0, TS_FWD)
    dcw_by_owner = jnp.transpose(dcw[0:3].reshape(3, N_DEV, 64), (1, 0, 2))
    dwie_parts, dwoe_parts, dcw_parts, dpw_g, dps_g, gpost0_g, gpre0_g = _weight_grad_scatter(
        h0, dp0, TS_WGRAD, "even_weight_grad",
        sides=[(_Scatter, [by_owner(dwoe), dcw_by_owner]), (_Gather, [dpw.astype(BF16), dps, dgpost0, dgpre0])])

    ((g_wie, d_wie, nm_wie, nv_wie), (g_wio, d_wio, nm_wio, nv_wio), (g_woe, d_woe, nm_woe, nv_woe),
     (g_woo, d_woo, nm_woo, nv_woo)) = _sum_update(
        [dwie_parts, dwio_parts, dwoe_parts, dwoo_parts],
        [(even_w_in[0], m_even_w_in[0], v_even_w_in[0]), (odd_w_in[0], m_odd_w_in[0], v_odd_w_in[0]),
         (even_w_out[0], m_even_w_out[0], v_even_w_out[0]), (odd_w_out[0], m_odd_w_out[0], v_odd_w_out[0])])

    partials = [gpre0_g, gpre1_g, gpost0_g, gpost1_g, dpw_g, dps_g, dws_g, dbs_g, dcw_parts, dlg_parts, dlb_parts]
    row_of = [(0, 0), (0, 1), (1, 0), (1, 1), (2, 0), (3, 0), (4, 0), (5, 0), (6, 0), (7, 0), (8, 0)]
    weights = [(pre_norm, m_pre_norm, v_pre_norm), (post_norm, m_post_norm, v_post_norm),
               (even_pool_w[0], m_even_pool_w[0], v_even_pool_w[0]),
               (even_pool_scale, m_even_pool_scale, v_even_pool_scale),
               (odd_w_s[0], m_odd_w_s[0], v_odd_w_s[0]), (odd_b_s[0], m_odd_b_s[0], v_odd_b_s[0]),
               (even_conv_w[0], m_even_conv_w[0], v_even_conv_w[0]),
               (odd_ln_g, m_odd_ln_g, v_odd_ln_g), (odd_ln_b, m_odd_ln_b, v_odd_ln_b)]
    small = _small_sum_update(partials, row_of, weights, loss_g)
    loss = small[0][0, 0]

    def leaves(k, big):
        pre, post, pw, psc, ws, bs, cw, lg, lb = [small[1 + 4 * i + k] for i in range(len(weights))]
        wie_k, woe_k, wio_k, woo_k = big
        return [pre, post, wie_k[None], cw[None], pw[None], psc, woe_k[None], wio_k[None], lg, lb, ws[None], bs[None],
                woo_k[None]]

    outs = [loss, gx.reshape(1, S, D)]
    outs += leaves(0, (g_wie, g_woe, g_wio, g_woo))
    outs += leaves(1, (d_wie, d_woe, d_wio, d_woo))
    outs += leaves(2, (nm_wie, nm_woe, nm_wio, nm_woo))
    outs += leaves(3, (nv_wie, nv_woe, nv_wio, nv_woo))
    return tuple(outs)
```

```python
import jax
import jax.numpy as jnp
from jax import lax
from jax.experimental import pallas as pl
from jax.experimental.pallas import tpu as pltpu

F32 = jnp.float32
BF16 = jnp.bfloat16
MESH = pl.DeviceIdType.MESH

D = 1024
EPS = 1e-6
A_W = 512
B_W = 512
POOL_WINDOWS = (2, 4, 8, 16)
GROUP = 128
N_GROUPS = 4
N_HEADS = 8
HALO = 16
PROJ = 3072
N_DEV = 8
N_CHIPS = 4
SHARD_IN = PROJ // N_DEV
SHARD_OUT = D // N_DEV

ADAM_LR = 0.001
ADAM_B1 = 0.9
ADAM_B2 = 0.999
ADAM_EPS = 1e-08
ADAM_WD = 0.01
ADAM_STEP = 10

VMEM_LIMIT = 54 * 1024 * 1024

TS_PRENORM = 1024
TS_FWD = 512
TS_BWD = 256
TS_WGRAD = 2048
SUB_ROWS = 256


def _params(n_grid=1, vmem=VMEM_LIMIT):
    return pltpu.CompilerParams(dimension_semantics=("arbitrary",) * n_grid, vmem_limit_bytes=vmem)


def _full(shape):
    return pl.BlockSpec(shape, lambda *_: (0,) * len(shape))


def _resident(shape):
    return pl.BlockSpec(shape, lambda *_: (0,) * len(shape), pipeline_mode=pl.Buffered(1))


def _sigmoid(z):
    return jax.nn.sigmoid(z)


def _dsilu(z, s):
    return s * (1.0 + z * (1.0 - s))


def _dot(a, b):
    return jnp.dot(a, b, preferred_element_type=F32)


def _dot_t0(a, b):
    return lax.dot_general(a, b, (((0,), (0,)), ((), ())), preferred_element_type=F32)


def _dot_t1(a, b):
    return lax.dot_general(a, b, (((1,), (1,)), ((), ())), preferred_element_type=F32)


def _rms(x):
    return lax.rsqrt(jnp.mean(x * x, axis=-1, keepdims=True) + EPS)


def _shift_down(a, k):
    return pltpu.roll(a, k, 0)


def _shift_up(a, k):
    return pltpu.roll(a, a.shape[0] - k, 0)


def _mesh_position():
    return lax.axis_index("x"), lax.axis_index("y"), lax.axis_index("c")


def _index(pos):
    return 4 * pos[0] + 2 * pos[1] + pos[2]


def _peer(pos, d):
    x, y, c = pos
    return (1 - x if d & 4 else x, 1 - y if d & 2 else y, 1 - c if d & 1 else c)


def _remote(src, dst, send_sems, recv_sems, k, to):
    return pltpu.make_async_remote_copy(src_ref=src, dst_ref=dst, send_sem=send_sems.at[k], recv_sem=recv_sems.at[k],
                                        device_id=to, device_id_type=MESH)


class _Gather:
    def __init__(self, srcs, dsts, send_sems, recv_sems):
        x, y, c = _mesh_position()
        n = len(srcs)
        me, sibling = (x, y, c), (x, y, 1 - c)
        chips = [(1 - x, y), (x, 1 - y), (1 - x, 1 - y)]

        def copy(k, t, block, to, src=None):
            slot = dsts[t].at[_index(block)]
            return _remote(slot if src is None else src, slot, send_sems, recv_sems, k * n + t, to)

        tensors = range(n)
        self.local = [pltpu.make_async_copy(srcs[t], dsts[t].at[_index(me)], recv_sems.at[7 * n + t]) for t in tensors]
        self.first = [copy(1 + j, t, me, chip + (c,), srcs[t]) for t in tensors for j, chip in enumerate(chips)]
        self.first += [copy(0, t, me, sibling, srcs[t]) for t in tensors]
        self.ici_in = [copy(1 + j, t, chip + (c,), me) for t in tensors for j, chip in enumerate(chips)]
        self.passed = [copy(4 + j, t, chip + (c,), sibling) for t in tensors for j, chip in enumerate(chips)]
        self.d2d_in = [copy(0, t, sibling, me) for t in tensors]
        self.d2d_in += [copy(4 + j, t, chip + (1 - c,), me) for t in tensors for j, chip in enumerate(chips)]

    def start(self):
        for cp in self.local + self.first:
            cp.start()

    def middle(self):
        for landed, onward in zip(self.ici_in, self.passed):
            landed.wait_recv()
            onward.start()

    def finish(self):
        for cp in self.d2d_in:
            cp.wait_recv()
        for cp in self.first + self.passed:
            cp.wait_send()
        for cp in self.local:
            cp.wait()


class _Scatter:
    def __init__(self, srcs, dsts, send_sems, recv_sems):
        pos = _mesh_position()
        n = len(srcs)
        self.local = [pltpu.make_async_copy(srcs[t].at[_index(pos)], dsts[t].at[0], recv_sems.at[t]) for t in range(n)]
        self.remote = []
        for d in range(1, N_DEV):
            to = _peer(pos, d)
            self.remote += [_remote(srcs[t].at[_index(to)], dsts[t].at[d], send_sems, recv_sems, d * n + t, to)
                            for t in range(n)]

    def start(self):
        for cp in self.local + self.remote:
            cp.start()

    def middle(self):
        pass

    def finish(self):
        for cp in self.local:
            cp.wait()
        for cp in self.remote:
            cp.wait_recv()
        for cp in self.remote:
            cp.wait_send()


HBM_SPEC = pl.BlockSpec(memory_space=pltpu.HBM)
MIDDLE_STEPS_BEFORE_END = 4


def _landing_shape(exchange, src):
    return (N_DEV,) + src.shape if exchange is _Gather else src.shape


def _call(body, *, name, grid, in_specs, out_specs, out_shape, scratch_shapes, args, sides=()):
    params = _params(len(grid))
    if not sides:
        return pl.pallas_call(body, name=name, grid=grid, in_specs=in_specs, out_specs=out_specs, out_shape=out_shape,
                              scratch_shapes=scratch_shapes, compiler_params=params)(*args)
    n_in, n_out, n_scratch = len(in_specs), len(out_specs), len(scratch_shapes)
    counts = [len(srcs) for _, srcs in sides]
    ns = sum(counts)
    total = 1
    for g in grid:
        total *= g
    side_args, side_shapes, side_sems = [], [], []
    for exchange, srcs in sides:
        side_args += list(srcs)
        side_shapes += [jax.ShapeDtypeStruct(_landing_shape(exchange, s), s.dtype) for s in srcs]
        side_sems += [pltpu.SemaphoreType.DMA((N_DEV * len(srcs),))] * 2

    def wrapped(*refs):
        ins, side_in = refs[:n_in], refs[n_in:n_in + ns]
        outs = refs[n_in + ns:n_in + ns + n_out]
        side_out = refs[n_in + ns + n_out:n_in + 2 * ns + n_out]
        rest = refs[n_in + 2 * ns + n_out:]
        scratch, sems = rest[:n_scratch], rest[n_scratch:]
        step = pl.program_id(0)
        for axis in range(1, len(grid)):
            step = step * grid[axis] + pl.program_id(axis)

        def exchanges():
            built, at = [], 0
            for k, (exchange, _) in enumerate(sides):
                built.append(exchange(side_in[at:at + counts[k]], side_out[at:at + counts[k]],
                                      sems[2 * k], sems[2 * k + 1]))
                at += counts[k]
            return built

        @pl.when(step == 0)
        def _():
            for ex in exchanges():
                ex.start()

        @pl.when(step == max(total - MIDDLE_STEPS_BEFORE_END, 0))
        def _():
            for ex in exchanges():
                ex.middle()

        body(*ins, *outs, *scratch)

        @pl.when(step == total - 1)
        def _():
            for ex in exchanges():
                ex.finish()

    return pl.pallas_call(
        wrapped, name=name, grid=grid, in_specs=list(in_specs) + [HBM_SPEC] * ns,
        out_specs=list(out_specs) + [HBM_SPEC] * ns, out_shape=list(out_shape) + side_shapes,
        scratch_shapes=list(scratch_shapes) + side_sems, compiler_params=params)(*args, *side_args)


def _prenorm_gather(x, g, wie, small, woe, wio, woo, ts):
    S = x.shape[0]
    nt = S // ts

    def body(x_ref, g_ref, wie_ref, sm_ref, woe_ref, wio_ref, woo_ref,
             h_ref, wie_out, sm_out, woe_out, wio_b, woo_b, cast_ref, small_ref, cast2_ref, send_sems, recv_sems):
        step = pl.program_id(0)
        gather = lambda: _Gather([cast_ref, small_ref, cast2_ref], [wie_out, sm_out, woe_out], send_sems, recv_sems)

        @pl.when(step == 0)
        def _():
            cast_ref[...] = wie_ref[...].astype(BF16)
            small_ref[...] = sm_ref[...]
            cast2_ref[...] = woe_ref[...].astype(BF16)
            gather().start()
            wio_b[...] = wio_ref[...].astype(BF16)
            woo_b[...] = woo_ref[...].astype(BF16)

        xv = x_ref[...]
        h_ref[...] = ((xv * _rms(xv)) * g_ref[...]).astype(BF16)

        @pl.when(step == nt - 1)
        def _():
            exchange = gather()
            exchange.middle()
            exchange.finish()

    tile = pl.BlockSpec((ts, D), lambda i: (i, 0))
    sems = pltpu.SemaphoreType.DMA((N_DEV * 3,))
    shards = [wie, small, woe, wio, woo]
    return pl.pallas_call(
        body, name="prenorm_gather", grid=(nt,),
        in_specs=[tile, _full((1, D))] + [_full(a.shape) for a in shards],
        out_specs=[tile, HBM_SPEC, HBM_SPEC, HBM_SPEC] + [_full(a.shape) for a in shards[3:]],
        out_shape=[jax.ShapeDtypeStruct((S, D), BF16), jax.ShapeDtypeStruct((N_DEV,) + wie.shape, BF16),
                   jax.ShapeDtypeStruct((N_DEV,) + small.shape, F32), jax.ShapeDtypeStruct((N_DEV,) + woe.shape, BF16)]
        + [jax.ShapeDtypeStruct(a.shape, BF16) for a in shards[3:]],
        scratch_shapes=[pltpu.VMEM(wie.shape, BF16), pltpu.VMEM(small.shape, F32), pltpu.VMEM(woe.shape, BF16),
                        sems, sems],
        compiler_params=_params(),
    )(x, g, wie, small, woe, wio, woo)


def _even_mixer(proj, halo, row0, cw_ref, pw_ref):
    xa, gb, gc, za, xp, zp = proj
    nr = xa.shape[0]
    ha = gc * xa
    ha_ext = jnp.concatenate([halo[0], ha], axis=0)
    ha_m1 = _shift_down(ha_ext, 1)[HALO:]
    ha_m2 = _shift_down(ha_ext, 2)[HALO:]
    conv = cw_ref[2:3, :] * ha + cw_ref[1:2, :] * ha_m1 + cw_ref[0:1, :] * ha_m2
    sig_a = _sigmoid(za)
    silu_a = za * sig_a

    xp_ext = jnp.concatenate([halo[1], xp], axis=0)
    pos = row0 + lax.broadcasted_iota(jnp.int32, (nr, 1), 0)
    pooled, inv_cnt, mixed = [], [], []
    for g, w in enumerate(POOL_WINDOWS):
        cols = slice(g * GROUP, (g + 1) * GROUP)
        s = xp_ext[:, cols]
        k = 1
        while k < w:
            s = s + _shift_down(s, k)
            k *= 2
        inv = 1.0 / jnp.minimum(pos + 1, w).astype(F32)
        pg = s[HALO:] * inv - xp[:, cols]
        pooled.append(pg)
        inv_cnt.append(inv)
        mixed.append(_dot(pg.astype(BF16), pw_ref[g]))
    mixed = jnp.concatenate(mixed, axis=1)
    sig_b = _sigmoid(zp)
    silu_b = zp * sig_b
    return dict(xa=xa, gb=gb, gc=gc, za=za, xp=xp, zp=zp, ha=ha, ha_m1=ha_m1, ha_m2=ha_m2, conv=conv,
                sig_a=sig_a, silu_a=silu_a, pooled=pooled, inv_cnt=inv_cnt, mixed=mixed, sig_b=sig_b,
                silu_b=silu_b)


def _columns(a):
    return [a[:, k * A_W:(k + 1) * A_W].astype(F32) for k in range(6)]


def _even_forward_fused(h, x, win_b, cw, pw_b, ps, wout_b, gpost, gpre_next, ts, sides=()):
    S = x.shape[0]

    def body(h_ref, x_ref, wi_ref, cw_ref, pw_ref, ps_ref, wo_ref, gp_ref, gn_ref,
             p_ref, m_ref, x1_ref, h1_ref, mix_ref, carry_ref):
        i = pl.program_id(0)

        @pl.when(i == 0)
        def _():
            carry_ref[...] = jnp.zeros_like(carry_ref)

        for r0 in range(0, ts, SUB_ROWS):
            rows = slice(r0, r0 + SUB_ROWS)
            proj = _dot(h_ref[rows, :], wi_ref[...])
            p_ref[rows, :] = proj.astype(BF16)
            fw = _even_mixer(_columns(proj), (carry_ref[:, 0:A_W], carry_ref[:, A_W:D]), i * ts + r0, cw_ref, pw_ref)
            carry_ref[:, 0:A_W] = fw["ha"][SUB_ROWS - HALO:]
            carry_ref[:, A_W:D] = fw["xp"][SUB_ROWS - HALO:]
            mix_ref[rows, 0:A_W] = (fw["gb"] * fw["conv"] * fw["silu_a"]).astype(BF16)
            mix_ref[rows, A_W:D] = (fw["mixed"] * ps_ref[...] * fw["silu_b"]).astype(BF16)
            m = _dot(mix_ref[rows, :], wo_ref[...])
            m_ref[rows, :] = m.astype(BF16)
            x1 = x_ref[rows, :] + (m * _rms(m)) * gp_ref[...]
            x1_ref[rows, :] = x1
            h1_ref[rows, :] = ((x1 * _rms(x1)) * gn_ref[...]).astype(BF16)

    tile = pl.BlockSpec((ts, D), lambda i: (i, 0))
    return _call(
        body, name="even_forward", grid=(S // ts,),
        in_specs=[tile, tile, _resident((D, PROJ)), _full((3, A_W)), _full((N_GROUPS, GROUP, GROUP)), _full((1, B_W)),
                  _resident((D, D)), _full((1, D)), _full((1, D))],
        out_specs=[pl.BlockSpec((ts, PROJ), lambda i: (i, 0)), tile, tile, tile],
        out_shape=[jax.ShapeDtypeStruct((S, PROJ), BF16), jax.ShapeDtypeStruct((S, D), BF16),
                   jax.ShapeDtypeStruct((S, D), F32), jax.ShapeDtypeStruct((S, D), BF16)],
        scratch_shapes=[pltpu.VMEM((ts, D), BF16), pltpu.VMEM((HALO, D), F32)],
        args=(h, x, win_b, cw, pw_b, ps, wout_b, gpost, gpre_next), sides=sides)


def _even_backward_fused(g1, m0, p, x, cw, pw_b, pwt_b, ps, woutt_b, wint_b, gpost, gpre, ts):
    S = g1.shape[0]
    nt = S // ts

    def body(g_ref, m_ref, p_ref, halo_ref, x_ref, cw_ref, pw_ref, pwt_ref, ps_ref, wot_ref, wit_ref, gp_ref, gn_ref,
             dp_ref, dx_ref, dwob_ref, dcw_ref, dpw_ref, dps_ref, dgp_ref, dgn_ref,
             carry_ref, mix_ref, dm_ref, dwo_ref):
        step = pl.program_id(0)
        i = nt - 1 - step

        @pl.when(step == 0)
        def _():
            carry_ref[...] = jnp.zeros_like(carry_ref)
            dwo_ref[...] = jnp.zeros_like(dwo_ref)
            dcw_ref[...] = jnp.zeros_like(dcw_ref)
            dpw_ref[...] = jnp.zeros_like(dpw_ref)
            dps_ref[...] = jnp.zeros_like(dps_ref)
            dgp_ref[...] = jnp.zeros_like(dgp_ref)
            dgn_ref[...] = jnp.zeros_like(dgn_ref)

        ps_v = ps_ref[...]
        sums = [None] * 6
        add = lambda k, part: sums.__setitem__(k, part if sums[k] is None else sums[k] + part)
        for r0 in range(ts - SUB_ROWS, -1, -SUB_ROWS):
            rows = slice(r0, r0 + SUB_ROWS)
            g = g_ref[rows, :]
            m = m_ref[rows, :].astype(F32)
            q = _rms(m)
            n = m * q
            add(0, jnp.sum(g * n, axis=0, keepdims=True))
            dn = g * gp_ref[...]
            dm = q * (dn - n * jnp.mean(dn * n, axis=-1, keepdims=True))
            dm_ref[rows, :] = dm.astype(BF16)
            dmix = _dot(dm_ref[rows, :], wot_ref[...])
            dya = dmix[:, 0:A_W]
            dyb = dmix[:, A_W:D]

            if r0 == 0:
                before = _columns(halo_ref[...])
                keep = jnp.where(i == 0, 0.0, 1.0).astype(F32)
                halo = (before[2] * before[0] * keep, before[4] * keep)
            else:
                before = _columns(p_ref[r0 - HALO:r0, :])
                halo = (before[2] * before[0], before[4])
            fw = _even_mixer(_columns(p_ref[rows, :]), halo, i * ts + r0, cw_ref, pw_ref)
            mix_ref[rows, 0:A_W] = (fw["gb"] * fw["conv"] * fw["silu_a"]).astype(BF16)
            mix_ref[rows, A_W:D] = (fw["mixed"] * ps_v * fw["silu_b"]).astype(BF16)

            t = dya * fw["gb"]
            dconv = t * fw["silu_a"]
            dgb = dya * fw["conv"] * fw["silu_a"]
            dza = t * fw["conv"] * _dsilu(fw["za"], fw["sig_a"])
            add(2, jnp.sum(dconv * fw["ha"], axis=0, keepdims=True))
            add(3, jnp.sum(dconv * fw["ha_m1"], axis=0, keepdims=True))
            add(4, jnp.sum(dconv * fw["ha_m2"], axis=0, keepdims=True))
            dconv_ext = jnp.concatenate([dconv, carry_ref[:, 0:A_W]], axis=0)
            dha = (cw_ref[2:3, :] * dconv + cw_ref[1:2, :] * _shift_up(dconv_ext, 1)[:SUB_ROWS]
                   + cw_ref[0:1, :] * _shift_up(dconv_ext, 2)[:SUB_ROWS])
            dgc = dha * fw["xa"]
            dxa = dha * fw["gc"]

            u = dyb * fw["mixed"]
            add(1, jnp.sum(u * fw["silu_b"], axis=0, keepdims=True))
            dzp = u * ps_v * _dsilu(fw["zp"], fw["sig_b"])
            dmixed = (dyb * ps_v * fw["silu_b"]).astype(BF16)
            dxp, e_first = [], []
            for gi, w in enumerate(POOL_WINDOWS):
                cols = slice(gi * GROUP, (gi + 1) * GROUP)
                dmg = dmixed[:, cols]
                dpooled = _dot(dmg, pwt_ref[gi])
                dpw_ref[gi] += _dot_t0(fw["pooled"][gi].astype(BF16), dmg)
                e = dpooled * fw["inv_cnt"][gi]
                e_first.append(e[0:HALO])
                s = jnp.concatenate([e, carry_ref[:, A_W + gi * GROUP:A_W + (gi + 1) * GROUP]], axis=0)
                k = 1
                while k < w:
                    s = s + _shift_up(s, k)
                    k *= 2
                dxp.append(s[:SUB_ROWS] - dpooled)
            carry_ref[:, 0:A_W] = dconv[0:HALO]
            carry_ref[:, A_W:D] = jnp.concatenate(e_first, axis=1)

            dp_ref[rows, 0:512] = dxa.astype(BF16)
            dp_ref[rows, 512:1024] = dgb.astype(BF16)
            dp_ref[rows, 1024:1536] = dgc.astype(BF16)
            dp_ref[rows, 1536:2048] = dza.astype(BF16)
            dp_ref[rows, 2048:2560] = jnp.concatenate(dxp, axis=1).astype(BF16)
            dp_ref[rows, 2560:3072] = dzp.astype(BF16)

            dh = _dot(dp_ref[rows, :], wit_ref[...])
            xv = x_ref[rows, :]
            r = _rms(xv)
            xn = xv * r
            add(5, jnp.sum(dh * xn, axis=0, keepdims=True))
            dxn = dh * gn_ref[...]
            dx_ref[rows, :] = g + r * (dxn - xn * jnp.mean(dxn * xn, axis=-1, keepdims=True))

        dwo_ref[...] += _dot_t0(mix_ref[...], dm_ref[...])
        dgp_ref[...] += sums[0]
        dps_ref[...] += sums[1]
        dcw_ref[2:3, :] += sums[2]
        dcw_ref[1:2, :] += sums[3]
        dcw_ref[0:1, :] += sums[4]
        dgn_ref[...] += sums[5]

        @pl.when(step == nt - 1)
        def _():
            dwob_ref[...] = dwo_ref[...].astype(BF16)

    rev = lambda s: (nt - 1 - s, 0)
    tile = pl.BlockSpec((ts, D), rev)
    wide = pl.BlockSpec((ts, PROJ), rev)
    bpt = ts // HALO
    halo_map = lambda s: (jnp.maximum((nt - 1 - s) * bpt - 1, 0), 0)
    groups = _full((N_GROUPS, GROUP, GROUP))
    vec = _full((1, D))
    return pl.pallas_call(
        body, name="even_backward", grid=(nt,),
        in_specs=[tile, tile, wide, pl.BlockSpec((HALO, PROJ), halo_map), tile, _full((3, A_W)), groups, groups,
                  _full((1, B_W)), _resident((D, D)), _resident((PROJ, D)), vec, vec],
        out_specs=[wide, tile, _full((D, D)), _full((8, A_W)), groups, _full((1, B_W)), vec, vec],
        out_shape=[jax.ShapeDtypeStruct((S, PROJ), BF16), jax.ShapeDtypeStruct((S, D), F32),
                   jax.ShapeDtypeStruct((D, D), BF16), jax.ShapeDtypeStruct((8, A_W), F32),
                   jax.ShapeDtypeStruct((N_GROUPS, GROUP, GROUP), F32), jax.ShapeDtypeStruct((1, B_W), F32),
                   jax.ShapeDtypeStruct((1, D), F32), jax.ShapeDtypeStruct((1, D), F32)],
        scratch_shapes=[pltpu.VMEM((HALO, D), F32), pltpu.VMEM((ts, D), BF16), pltpu.VMEM((ts, D), BF16),
                        pltpu.VMEM((D, D), F32)],
        compiler_params=_params(),
    )(g1, m0, p, p, x, cw, pw_b, pwt_b, ps, woutt_b, wint_b, gpost, gpre)


def _gate_matmuls(w_ref, in_ref, out_ref, chunks, bias_ref=None):
    for h in range(N_HEADS):
        cols = slice(h * GROUP, (h + 1) * GROUP)
        wide = jnp.concatenate([in_ref[chunk, cols] for chunk in chunks], axis=1)
        res = _dot(w_ref[h], wide)
        for k, chunk in enumerate(chunks):
            part = res[:, k * GROUP:(k + 1) * GROUP]
            out_ref[chunk, cols] = part if bias_ref is None else part + bias_ref[h]


def _odd_forward_parts(v, lg_ref, lb_ref, wt_ref, bfull_ref, vln_ref, sv_ref, r0, nr):
    rows = slice(r0, r0 + nr)
    mu = jnp.mean(v, axis=-1, keepdims=True)
    vc = v - mu
    rstd = lax.rsqrt(jnp.mean(vc * vc, axis=-1, keepdims=True) + EPS)
    vh = vc * rstd
    vln_ref[rows, :] = (vh * lg_ref[...] + lb_ref[...]).astype(BF16)
    _gate_matmuls(wt_ref, vln_ref, sv_ref, [slice(r0 + k * GROUP, r0 + (k + 1) * GROUP) for k in range(nr // GROUP)],
                  bfull_ref)
    return vh, rstd


def _odd_forward(h, x1, tgt, win_b, lg, lb, wt_b, bfull, wout_b, gpost, ts):
    S = x1.shape[0]

    def body(h_ref, x_ref, t_ref, wi_ref, lg_ref, lb_ref, wt_ref, bfull_ref, wo_ref, gp_ref,
             p_ref, m_ref, g_ref, loss_ref, vln_ref, sv_ref, y_ref):
        @pl.when(pl.program_id(0) == 0)
        def _():
            loss_ref[...] = jnp.zeros_like(loss_ref)

        loss = None
        for r0 in range(0, ts, SUB_ROWS):
            rows = slice(r0, r0 + SUB_ROWS)
            proj = _dot(h_ref[rows, :], wi_ref[...])
            p_ref[rows, :] = proj.astype(BF16)
            u, v, z = proj[:, 0:D], proj[:, D:2 * D], proj[:, 2 * D:3 * D]
            _odd_forward_parts(v, lg_ref, lb_ref, wt_ref, bfull_ref, vln_ref, sv_ref, r0, SUB_ROWS)
            y_ref[rows, :] = (u * sv_ref[rows, :] * (z * _sigmoid(z))).astype(BF16)
            m = _dot(y_ref[rows, :], wo_ref[...])
            m_ref[rows, :] = m.astype(BF16)
            x2 = x_ref[rows, :] + (m * _rms(m)) * gp_ref[...]
            err = x2 - t_ref[rows, :]
            g_ref[rows, :] = (err * (1.0 / D)).astype(BF16)
            part = jnp.sum(err * err, axis=0, keepdims=True)
            loss = part if loss is None else loss + part
        loss_ref[...] += loss

    tile = pl.BlockSpec((ts, D), lambda i: (i, 0))
    small = _full((N_HEADS, GROUP, GROUP))
    return pl.pallas_call(
        body, name="odd_forward", grid=(S // ts,),
        in_specs=[pl.BlockSpec((ts, D), lambda i: (i, 0)), tile, tile, _full((D, PROJ)), _full((1, D)), _full((1, D)),
                  small, small, _full((D, D)), _full((1, D))],
        out_specs=[pl.BlockSpec((ts, PROJ), lambda i: (i, 0)), tile, tile, _full((1, D))],
        out_shape=[jax.ShapeDtypeStruct((S, PROJ), BF16), jax.ShapeDtypeStruct((S, D), BF16),
                   jax.ShapeDtypeStruct((S, D), BF16), jax.ShapeDtypeStruct((1, D), F32)],
        scratch_shapes=[pltpu.VMEM((ts, D), BF16), pltpu.VMEM((ts, D), F32), pltpu.VMEM((ts, D), BF16)],
        compiler_params=_params(),
    )(h, x1, tgt, win_b, lg, lb, wt_b, bfull, wout_b, gpost)


def _odd_backward(g2, m1, p, x1, lg, lb, wt_b, wtt_b, bfull, woutt_b, wint_b, gpost, gpre, ts):
    S = g2.shape[0]
    nt = S // ts
    sub = ts // 2

    def body(g_ref, m_ref, p_ref, x_ref, lg_ref, lb_ref, wt_ref, wtt_ref, bfull_ref, wot_ref, wit_ref, gp_ref, gn_ref,
             dp_ref, dx_ref, dwob_ref, dws_ref, dbs_ref, dlg_ref, dlb_ref, dgp_ref, dgn_ref,
             vln_ref, sv_ref, dsvb_ref, dvln_ref, dsum_ref, y_ref, dm_ref, dwo_ref):
        step = pl.program_id(0)

        @pl.when(step == 0)
        def _():
            dwo_ref[...] = jnp.zeros_like(dwo_ref)
            dws_ref[...] = jnp.zeros_like(dws_ref)
            dsum_ref[...] = jnp.zeros_like(dsum_ref)
            dlg_ref[...] = jnp.zeros_like(dlg_ref)
            dlb_ref[...] = jnp.zeros_like(dlb_ref)
            dgp_ref[...] = jnp.zeros_like(dgp_ref)
            dgn_ref[...] = jnp.zeros_like(dgn_ref)

        tril = (lax.broadcasted_iota(jnp.int32, (GROUP, GROUP), 0)
                >= lax.broadcasted_iota(jnp.int32, (GROUP, GROUP), 1))
        sums = [None] * 4
        add = lambda k, part: sums.__setitem__(k, part if sums[k] is None else sums[k] + part)

        def post_norm(c):
            g = g_ref[c["rows"], :].astype(F32)
            m = m_ref[c["rows"], :].astype(F32)
            q = _rms(m)
            n = m * q
            add(0, jnp.sum(g * n, axis=0, keepdims=True))
            dn = g * gp_ref[...]
            dm = q * (dn - n * jnp.mean(dn * n, axis=-1, keepdims=True))
            dm_ref[c["rows"], :] = dm.astype(BF16)

        def out_projection(c):
            c["dy"] = _dot(dm_ref[c["rows"], :], wot_ref[...])

        def layernorm(c):
            v = p_ref[c["rows"], D:2 * D].astype(F32)
            mu = jnp.mean(v, axis=-1, keepdims=True)
            vc = v - mu
            c["rstd"] = lax.rsqrt(jnp.mean(vc * vc, axis=-1, keepdims=True) + EPS)
            c["vh"] = vc * c["rstd"]
            vln_ref[c["rows"], :] = (c["vh"] * lg_ref[...] + lb_ref[...]).astype(BF16)

        def gate_matmuls(c):
            _gate_matmuls(wt_ref, vln_ref, sv_ref, c["chunks"], bfull_ref)

        def gating(c):
            rows = c["rows"]
            u = p_ref[rows, 0:D].astype(F32)
            z = p_ref[rows, 2 * D:3 * D].astype(F32)
            sv = sv_ref[rows, :]
            sig = _sigmoid(z)
            sz = z * sig
            y_ref[rows, :] = (u * sv * sz).astype(BF16)
            dy = c.pop("dy")
            t = dy * sz
            dsv = t * u
            dsvb_ref[rows, :] = dsv.astype(BF16)
            for k in range(sub // GROUP):
                dsum_ref[...] += dsv[k * GROUP:(k + 1) * GROUP]
            dp_ref[rows, 0:D] = (t * sv).astype(BF16)
            dp_ref[rows, 2 * D:3 * D] = (dy * u * sv * _dsilu(z, sig)).astype(BF16)

        def gate_backward_matmuls(c):
            for chunk in c["chunks"]:
                for h in range(N_HEADS):
                    cols = slice(h * GROUP, (h + 1) * GROUP)
                    dvln_ref[chunk, cols] = _dot(wtt_ref[h], dsvb_ref[chunk, cols])
                    dws_ref[h] += jnp.where(tril, _dot_t1(dsvb_ref[chunk, cols], vln_ref[chunk, cols]), 0.0)

        def layernorm_backward(c):
            vh, rstd = c.pop("vh"), c.pop("rstd")
            dvln = dvln_ref[c["rows"], :]
            add(1, jnp.sum(dvln * vh, axis=0, keepdims=True))
            add(2, jnp.sum(dvln, axis=0, keepdims=True))
            dvh = dvln * lg_ref[...]
            dv = rstd * (dvh - jnp.mean(dvh, axis=-1, keepdims=True)
                         - vh * jnp.mean(dvh * vh, axis=-1, keepdims=True))
            dp_ref[c["rows"], D:2 * D] = dv.astype(BF16)

        def in_projection(c):
            c["dh"] = _dot(dp_ref[c["rows"], :], wit_ref[...])

        def pre_norm(c):
            dh = c.pop("dh")
            xv = x_ref[c["rows"], :]
            r = _rms(xv)
            xn = xv * r
            add(3, jnp.sum(dh * xn, axis=0, keepdims=True))
            dxn = dh * gn_ref[...]
            dx_ref[c["rows"], :] = (g_ref[c["rows"], :].astype(F32)
                                    + r * (dxn - xn * jnp.mean(dxn * xn, axis=-1, keepdims=True)))

        phases = [post_norm, out_projection, layernorm, gate_matmuls, gating, gate_backward_matmuls,
                  layernorm_backward, in_projection, pre_norm]
        groups = [dict(rows=slice(r0, r0 + sub),
                       chunks=[slice(r0 + k * GROUP, r0 + (k + 1) * GROUP) for k in range(sub // GROUP)])
                  for r0 in range(0, ts, sub)]
        for group in groups:
            for phase in phases:
                phase(group)

        dwo_ref[...] += _dot_t0(y_ref[...], dm_ref[...])
        dgp_ref[...] += sums[0]
        dlg_ref[...] += sums[1]
        dlb_ref[...] += sums[2]
        dgn_ref[...] += sums[3]

        @pl.when(step == nt - 1)
        def _():
            dwob_ref[...] = dwo_ref[...].astype(BF16)
            ones = jnp.ones((8, GROUP), F32)
            for h in range(N_HEADS):
                cols = slice(h * GROUP, (h + 1) * GROUP)
                total = lax.dot_general(ones, dsum_ref[:, cols], (((1,), (1,)), ((), ())),
                                        precision=lax.Precision.HIGHEST, preferred_element_type=F32)
                dbs_ref[h:h + 1, :] = total[0:1, :]

    tile = pl.BlockSpec((ts, D), lambda i: (i, 0))
    wide = pl.BlockSpec((ts, PROJ), lambda i: (i, 0))
    small = _full((N_HEADS, GROUP, GROUP))
    heads = _resident((N_HEADS, GROUP, GROUP))
    vec = _full((1, D))
    return pl.pallas_call(
        body, name="odd_backward", grid=(nt,),
        in_specs=[tile, tile, wide, tile, vec, vec, heads, heads, heads, _resident((D, D)), _resident((PROJ, D)),
                  vec, vec],
        out_specs=[wide, tile, _full((D, D)), small, _full((N_HEADS, GROUP)), vec, vec, vec, vec],
        out_shape=[jax.ShapeDtypeStruct((S, PROJ), BF16), jax.ShapeDtypeStruct((S, D), F32),
                   jax.ShapeDtypeStruct((D, D), BF16),
                   jax.ShapeDtypeStruct((N_HEADS, GROUP, GROUP), F32), jax.ShapeDtypeStruct((N_HEADS, GROUP), F32),
                   jax.ShapeDtypeStruct((1, D), F32), jax.ShapeDtypeStruct((1, D), F32),
                   jax.ShapeDtypeStruct((1, D), F32), jax.ShapeDtypeStruct((1, D), F32)],
        scratch_shapes=[pltpu.VMEM((ts, D), BF16), pltpu.VMEM((ts, D), F32), pltpu.VMEM((ts, D), BF16),
                        pltpu.VMEM((ts, D), F32), pltpu.VMEM((GROUP, D), F32), pltpu.VMEM((ts, D), BF16),
                        pltpu.VMEM((ts, D), BF16), pltpu.VMEM((D, D), F32)],
        compiler_params=_params(),
    )(g2, m1, p, x1, lg, lb, wt_b, wtt_b, bfull, woutt_b, wint_b, gpost, gpre)


def _weight_grad_scatter(h, dp, ts, name, sides=()):
    S = h.shape[0]
    ts = min(ts, S)
    nt = S // ts
    panel = 2 * SHARD_IN
    last = N_CHIPS - 1

    def body(h_ref, dp_ref, parts_ref, acc_ref, half_ref, swap_ref, sum_ref, send_sems, recv_sems):
        s, t = pl.program_id(0), pl.program_id(1)
        x, y, c = _mesh_position()
        sibling = (x, y, 1 - c)

        def swap(k):
            return _remote(half_ref.at[k, 1], swap_ref.at[k], send_sems, recv_sems, k, sibling)

        def chip_sum(k):
            flip = last - k
            to = (x ^ (flip >> 1), y ^ (flip & 1), c)
            return _remote(sum_ref.at[k], parts_ref.at[1 + flip], send_sems, recv_sems, 3 + flip, to)

        own = pltpu.make_async_copy(half_ref.at[last, 0], parts_ref.at[0], recv_sems.at[7])
        to_sibling = _remote(half_ref.at[last, 1], parts_ref.at[1], send_sems, recv_sems, 3, sibling)

        @pl.when(t == 0)
        def _():
            acc_ref[...] = jnp.zeros_like(acc_ref)

        for k in range(last):
            @pl.when((s == k + 1) & (t == 0))
            def _():
                swap(k).wait_recv()
                sum_ref[k] = (half_ref[k, 0].astype(F32) + swap_ref[k].astype(F32)).astype(BF16)
                chip_sum(k).start()

        acc_ref[...] += _dot_t0(h_ref[...], dp_ref[...])

        @pl.when(t == nt - 1)
        def _():
            for core in range(2):
                @pl.when(c == core)
                def _():
                    half_ref[s, core] = acc_ref[:, 0:SHARD_IN].astype(BF16)
                    half_ref[s, 1 - core] = acc_ref[:, SHARD_IN:panel].astype(BF16)
            for k in range(last):
                @pl.when(s == k)
                def _():
                    swap(k).start()

        @pl.when((s == last) & (t == nt - 1))
        def _():
            own.start()
            to_sibling.start()
            own.wait()
            _remote(half_ref.at[last, 1], parts_ref.at[1], send_sems, recv_sems, 3, sibling).wait_recv()
            for k in range(last):
                chip_sum(k).wait_recv()
            to_sibling.wait_send()
            for k in range(last):
                swap(k).wait_send()
                chip_sum(k).wait_send()

    chip_panel = lambda s, t: (t, (2 * lax.axis_index("x") + lax.axis_index("y")) ^ (last - s))
    sems = pltpu.SemaphoreType.DMA((8,))
    return _call(
        body, name=name, grid=(N_CHIPS, nt),
        in_specs=[pl.BlockSpec((ts, D), lambda s, t: (t, 0)), pl.BlockSpec((ts, panel), chip_panel)],
        out_specs=[HBM_SPEC], out_shape=[jax.ShapeDtypeStruct((N_CHIPS + 1, D, SHARD_IN), BF16)],
        scratch_shapes=[pltpu.VMEM((D, panel), F32), pltpu.VMEM((N_CHIPS, 2, D, SHARD_IN), BF16),
                        pltpu.VMEM((last, D, SHARD_IN), BF16), pltpu.VMEM((last, D, SHARD_IN), BF16), sems, sems],
        args=(h, dp), sides=sides)


def _adamw(w, g, m, v):
    m = ADAM_B1 * m + (1.0 - ADAM_B1) * g
    v = ADAM_B2 * v + (1.0 - ADAM_B2) * (g * g)
    m_hat = m / (1.0 - ADAM_B1 ** ADAM_STEP)
    v_hat = v / (1.0 - ADAM_B2 ** ADAM_STEP)
    delta = -ADAM_LR * (m_hat / (jnp.sqrt(v_hat) + ADAM_EPS) + ADAM_WD * w)
    return delta, m, v


UPDATE_STEPS = 4


def _sum_update(parts, weights):
    n_w = len(weights)

    def body(*refs):
        p_refs, wmv_refs, outs = refs[:n_w], refs[n_w:4 * n_w], refs[4 * n_w:]
        for i in range(n_w):
            acc = p_refs[i][0].astype(F32)
            for d in range(1, parts[i].shape[0]):
                acc = acc + p_refs[i][d].astype(F32)
            w_ref, m_ref, v_ref = wmv_refs[3 * i:3 * i + 3]
            g_out, d_out, m_out, v_out = outs[4 * i:4 * i + 4]
            g_out[...] = acc
            delta, m_new, v_new = _adamw(w_ref[...], acc, m_ref[...], v_ref[...])
            d_out[...] = delta
            m_out[...] = m_new
            v_out[...] = v_new

    part_specs, wmv_specs, out_specs, out_shape = [], [], [], []
    for p, (w, _, _) in zip(parts, weights):
        R, C = w.shape
        rb = R // UPDATE_STEPS
        blk = pl.BlockSpec((rb, C), lambda i: (i, 0))
        part_specs.append(pl.BlockSpec((p.shape[0], rb, C), lambda i: (0, i, 0)))
        wmv_specs += [blk] * 3
        out_specs += [blk] * 4
        out_shape += [jax.ShapeDtypeStruct((R, C), F32)] * 4
    flat = pl.pallas_call(
        body, name="update_large_weights", grid=(UPDATE_STEPS,), in_specs=part_specs + wmv_specs,
        out_specs=out_specs, out_shape=out_shape, compiler_params=_params(),
    )(*parts, *[a for wmv in weights for a in wmv])
    return [flat[4 * i:4 * i + 4] for i in range(n_w)]


def _small_sum_update(partials, row_of, weights, loss_parts):
    n_p, n_w = len(partials), len(weights)

    def body(*refs):
        p_refs = refs[:n_p]
        loss_ref = refs[n_p]
        w_refs = refs[n_p + 1:n_p + 1 + 3 * n_w]
        loss_out = refs[n_p + 1 + 3 * n_w]
        outs = refs[n_p + 2 + 3 * n_w:]

        def total(ref):
            acc = ref[0].astype(F32)
            for d in range(1, N_DEV):
                acc = acc + ref[d].astype(F32)
            return acc

        lsum = jnp.sum(total(loss_ref), axis=-1, keepdims=True) * (0.5 / D)
        loss_out[...] = jnp.broadcast_to(lsum, loss_out.shape)
        for k in range(n_p):
            i, r0 = row_of[k]
            g = total(p_refs[k])
            w_ref, m_ref, v_ref = w_refs[3 * i:3 * i + 3]
            g_out, d_out, m_out, v_out = outs[4 * i:4 * i + 4]
            if g.ndim == 2:
                at = (slice(r0, r0 + g.shape[0]), slice(None))
            else:
                at = (slice(None),) * g.ndim
            delta, m_new, v_new = _adamw(w_ref[at], g, m_ref[at], v_ref[at])
            g_out[at] = g
            d_out[at] = delta
            m_out[at] = m_new
            v_out[at] = v_new

    vm = pl.BlockSpec(memory_space=pltpu.VMEM)
    flat_w = [a for wmv in weights for a in wmv]
    out_shape = [jax.ShapeDtypeStruct((1, 128), F32)]
    for w, _, _ in weights:
        out_shape += [jax.ShapeDtypeStruct(w.shape, F32)] * 4
    return pl.pallas_call(
        body, name="small_update", in_specs=[vm] * (n_p + 1 + 3 * n_w), out_specs=[vm] * len(out_shape),
        out_shape=out_shape, compiler_params=pltpu.CompilerParams(vmem_limit_bytes=VMEM_LIMIT),
    )(*partials, loss_parts, *flat_w)


def _pack_small_shard(cw, lg, lb):
    return (jnp.pad(cw, ((0, 13), (0, 64))) + jnp.pad(lg, ((8, 7), (0, 0))) + jnp.pad(lb, ((9, 6), (0, 0))))


def kernel(x, pre_norm, post_norm, even_w_in, even_conv_w, even_pool_w, even_pool_scale, even_w_out, odd_w_in, odd_ln_g, odd_ln_b, odd_w_s, odd_b_s, odd_w_out, loss_target, m_pre_norm, m_post_norm, m_even_w_in, m_even_conv_w, m_even_pool_w, m_even_pool_scale, m_even_w_out, m_odd_w_in, m_odd_ln_g, m_odd_ln_b, m_odd_w_s, m_odd_b_s, m_odd_w_out, v_pre_norm, v_post_norm, v_even_w_in, v_even_conv_w, v_even_pool_w, v_even_pool_scale, v_even_w_out, v_odd_w_in, v_odd_ln_g, v_odd_ln_b, v_odd_w_s, v_odd_b_s, v_odd_w_out):
    S = x.shape[1]
    xs = x.reshape(S, D)
    tgt = loss_target.reshape(S, D)

    gpre0, gpre1 = pre_norm[0:1], pre_norm[1:2]
    gpost0, gpost1 = post_norm[0:1], post_norm[1:2]
    small_shard = _pack_small_shard(even_conv_w[0], odd_ln_g, odd_ln_b)
    h0, wie_g, small_g, woe_g, wio_b, woo_b = _prenorm_gather(xs, gpre0, even_w_in[0], small_shard, even_w_out[0],
                                                              odd_w_in[0], odd_w_out[0], TS_PRENORM)
    wie = jnp.transpose(wie_g, (1, 0, 2)).reshape(D, PROJ)
    wie_t = jnp.transpose(wie_g, (0, 2, 1)).reshape(PROJ, D)
    woe = woe_g.reshape(D, D)
    woe_t = woe.T
    conv_w = jnp.transpose(small_g[:, 0:3, 0:64], (1, 0, 2)).reshape(3, A_W)
    ln_g = small_g[:, 8, :].reshape(1, D)
    ln_b = small_g[:, 9, :].reshape(1, D)

    pool_w_b = even_pool_w[0].astype(BF16)
    pool_wt_b = jnp.swapaxes(even_pool_w[0], 1, 2).astype(BF16)
    ws_tril = jnp.tril(odd_w_s[0])
    ws_b = ws_tril.astype(BF16)
    wst_b = jnp.swapaxes(ws_tril, 1, 2).astype(BF16)
    b_full = jnp.broadcast_to(odd_b_s[0][:, :, None], (N_HEADS, GROUP, GROUP))

    p0, m0, x1, h1, wio_g, woo_g = _even_forward_fused(h0, xs, wie, conv_w, pool_w_b, even_pool_scale, woe, gpost0,
                                                       gpre1, TS_FWD, sides=[(_Gather, [wio_b, woo_b])])
    wio = jnp.transpose(wio_g, (1, 0, 2)).reshape(D, PROJ)
    wio_t = jnp.transpose(wio_g, (0, 2, 1)).reshape(PROJ, D)
    woo = woo_g.reshape(D, D)
    woo_t = woo.T
    p1, m1, g2, loss_vec = _odd_forward(h1, x1, tgt, wio, ln_g, ln_b, ws_b, b_full, woo, gpost1, TS_FWD)

    dp1, g1, dwoo, dws, dbs, dlg, dlb, dgpost1, dgpre1 = _odd_backward(
        g2, m1, p1, x1, ln_g, ln_b, ws_b, wst_b, b_full, woo_t, wio_t, gpost1, gpre1, TS_BWD)
    by_owner = lambda dwo: dwo.reshape(N_DEV, SHARD_OUT, D)
    (dwio_parts, dwoo_parts, dlg_parts, dlb_parts,
     dws_g, dbs_g, gpost1_g, gpre1_g, loss_g) = _weight_grad_scatter(
        h1, dp1, TS_WGRAD, "odd_weight_grad",
        sides=[(_Scatter, [by_owner(dwoo), dlg.reshape(N_DEV, 1, 128), dlb.reshape(N_DEV, 1, 128)]),
               (_Gather, [dws.astype(BF16), dbs, dgpost1, dgpre1, loss_vec])])
    dp0, gx, dwoe, dcw, dpw, dps, dgpost0, dgpre0 = _even_backward_fused(
        g1, m0, p0, xs, conv_w, pool_w_b, pool_wt_b, even_pool_scale, woe_t, wie_t, gpost0, gpre0, TS_FWD)
    dcw_by_owner = jnp.transpose(dcw[0:3].reshape(3, N_DEV, 64), (1, 0, 2))
    dwie_parts, dwoe_parts, dcw_parts, dpw_g, dps_g, gpost0_g, gpre0_g = _weight_grad_scatter(
        h0, dp0, TS_WGRAD, "even_weight_grad",
        sides=[(_Scatter, [by_owner(dwoe), dcw_by_owner]), (_Gather, [dpw.astype(BF16), dps, dgpost0, dgpre0])])

    ((g_wie, d_wie, nm_wie, nv_wie), (g_wio, d_wio, nm_wio, nv_wio), (g_woe, d_woe, nm_woe, nv_woe),
     (g_woo, d_woo, nm_woo, nv_woo)) = _sum_update(
        [dwie_parts, dwio_parts, dwoe_parts, dwoo_parts],
        [(even_w_in[0], m_even_w_in[0], v_even_w_in[0]), (odd_w_in[0], m_odd_w_in[0], v_odd_w_in[0]),
         (even_w_out[0], m_even_w_out[0], v_even_w_out[0]), (odd_w_out[0], m_odd_w_out[0], v_odd_w_out[0])])

    partials = [gpre0_g, gpre1_g, gpost0_g, gpost1_g, dpw_g, dps_g, dws_g, dbs_g, dcw_parts, dlg_parts, dlb_parts]
    row_of = [(0, 0), (0, 1), (1, 0), (1, 1), (2, 0), (3, 0), (4, 0), (5, 0), (6, 0), (7, 0), (8, 0)]
    weights = [(pre_norm, m_pre_norm, v_pre_norm), (post_norm, m_post_norm, v_post_norm),
               (even_pool_w[0], m_even_pool_w[0], v_even_pool_w[0]),
               (even_pool_scale, m_even_pool_scale, v_even_pool_scale),
               (odd_w_s[0], m_odd_w_s[0], v_odd_w_s[0]), (odd_b_s[0], m_odd_b_s[0], v_odd_b_s[0]),
               (even_conv_w[0], m_even_conv_w[0], v_even_conv_w[0]),
               (odd_ln_g, m_odd_ln_g, v_odd_ln_g), (odd_ln_b, m_odd_ln_b, v_odd_ln_b)]
    small = _small_sum_update(partials, row_of, weights, loss_g)
    loss = small[0][0, 0]

    def leaves(k, big):
        pre, post, pw, psc, ws, bs, cw, lg, lb = [small[1 + 4 * i + k] for i in range(len(weights))]
        wie_k, woe_k, wio_k, woo_k = big
        return [pre, post, wie_k[None], cw[None], pw[None], psc, woe_k[None], wio_k[None], lg, lb, ws[None], bs[None],
                woo_k[None]]

    outs = [loss, gx.reshape(1, S, D)]
    outs += leaves(0, (g_wie, g_woe, g_wio, g_woo))
    outs += leaves(1, (d_wie, d_woe, d_wio, d_woo))
    outs += leaves(2, (nm_wie, nm_woe, nm_wio, nm_woo))
    outs += leaves(3, (nv_wie, nv_woe, nv_wio, nv_woo))
    return tuple(outs)
```

```python
import jax
import jax.numpy as jnp
from jax import lax
from jax.experimental import pallas as pl
from jax.experimental.pallas import tpu as pltpu

F32 = jnp.float32
BF16 = jnp.bfloat16
MESH = pl.DeviceIdType.MESH

D = 1024
EPS = 1e-6
A_W = 512
B_W = 512
POOL_WINDOWS = (2, 4, 8, 16)
GROUP = 128
N_GROUPS = 4
N_HEADS = 8
HALO = 16
PROJ = 3072
N_DEV = 8
N_CHIPS = 4
SHARD_IN = PROJ // N_DEV
SHARD_OUT = D // N_DEV

ADAM_LR = 0.001
ADAM_B1 = 0.9
ADAM_B2 = 0.999
ADAM_EPS = 1e-08
ADAM_WD = 0.01
ADAM_STEP = 10

VMEM_LIMIT = 54 * 1024 * 1024

TS_PRENORM = 1024
TS_FWD = 512
TS_BWD = 256
TS_WGRAD = 2048
SUB_ROWS = 256


def _params(n_grid=1, vmem=VMEM_LIMIT):
    return pltpu.CompilerParams(dimension_semantics=("arbitrary",) * n_grid, vmem_limit_bytes=vmem)


def _full(shape):
    return pl.BlockSpec(shape, lambda *_: (0,) * len(shape))


def _resident(shape):
    return pl.BlockSpec(shape, lambda *_: (0,) * len(shape), pipeline_mode=pl.Buffered(1))


def _sigmoid(z):
    return jax.nn.sigmoid(z)


def _dsilu(z, s):
    return s * (1.0 + z * (1.0 - s))


def _dot(a, b):
    return jnp.dot(a, b, preferred_element_type=F32)


def _dot_t0(a, b):
    return lax.dot_general(a, b, (((0,), (0,)), ((), ())), preferred_element_type=F32)


def _dot_t1(a, b):
    return lax.dot_general(a, b, (((1,), (1,)), ((), ())), preferred_element_type=F32)


def _rms(x):
    return lax.rsqrt(jnp.mean(x * x, axis=-1, keepdims=True) + EPS)


def _shift_down(a, k):
    return pltpu.roll(a, k, 0)


def _shift_up(a, k):
    return pltpu.roll(a, a.shape[0] - k, 0)


def _mesh_position():
    return lax.axis_index("x"), lax.axis_index("y"), lax.axis_index("c")


def _index(pos):
    return 4 * pos[0] + 2 * pos[1] + pos[2]


def _peer(pos, d):
    x, y, c = pos
    return (1 - x if d & 4 else x, 1 - y if d & 2 else y, 1 - c if d & 1 else c)


def _remote(src, dst, send_sems, recv_sems, k, to):
    return pltpu.make_async_remote_copy(src_ref=src, dst_ref=dst, send_sem=send_sems.at[k], recv_sem=recv_sems.at[k],
                                        device_id=to, device_id_type=MESH)


class _Gather:
    def __init__(self, srcs, dsts, send_sems, recv_sems):
        x, y, c = _mesh_position()
        n = len(srcs)
        me, sibling = (x, y, c), (x, y, 1 - c)
        chips = [(1 - x, y), (x, 1 - y), (1 - x, 1 - y)]

        def copy(k, t, block, to, src=None):
            slot = dsts[t].at[_index(block)]
            return _remote(slot if src is None else src, slot, send_sems, recv_sems, k * n + t, to)

        tensors = range(n)
        self.local = [pltpu.make_async_copy(srcs[t], dsts[t].at[_index(me)], recv_sems.at[7 * n + t]) for t in tensors]
        self.first = [copy(1 + j, t, me, chip + (c,), srcs[t]) for t in tensors for j, chip in enumerate(chips)]
        self.first += [copy(0, t, me, sibling, srcs[t]) for t in tensors]
        self.ici_in = [copy(1 + j, t, chip + (c,), me) for t in tensors for j, chip in enumerate(chips)]
        self.passed = [copy(4 + j, t, chip + (c,), sibling) for t in tensors for j, chip in enumerate(chips)]
        self.d2d_in = [copy(0, t, sibling, me) for t in tensors]
        self.d2d_in += [copy(4 + j, t, chip + (1 - c,), me) for t in tensors for j, chip in enumerate(chips)]

    def start(self):
        for cp in self.local + self.first:
            cp.start()

    def middle(self):
        for landed, onward in zip(self.ici_in, self.passed):
            landed.wait_recv()
            onward.start()

    def finish(self):
        for cp in self.d2d_in:
            cp.wait_recv()
        for cp in self.first + self.passed:
            cp.wait_send()
        for cp in self.local:
            cp.wait()


class _Scatter:
    def __init__(self, srcs, dsts, send_sems, recv_sems):
        pos = _mesh_position()
        n = len(srcs)
        self.local = [pltpu.make_async_copy(srcs[t].at[_index(pos)], dsts[t].at[0], recv_sems.at[t]) for t in range(n)]
        self.remote = []
        for d in range(1, N_DEV):
            to = _peer(pos, d)
            self.remote += [_remote(srcs[t].at[_index(to)], dsts[t].at[d], send_sems, recv_sems, d * n + t, to)
                            for t in range(n)]

    def start(self):
        for cp in self.local + self.remote:
            cp.start()

    def middle(self):
        pass

    def finish(self):
        for cp in self.local:
            cp.wait()
        for cp in self.remote:
            cp.wait_recv()
        for cp in self.remote:
            cp.wait_send()


HBM_SPEC = pl.BlockSpec(memory_space=pltpu.HBM)
MIDDLE_STEPS_BEFORE_END = 4


def _landing_shape(exchange, src):
    return (N_DEV,) + src.shape if exchange is _Gather else src.shape


def _call(body, *, name, grid, in_specs, out_specs, out_shape, scratch_shapes, args, sides=()):
    params = _params(len(grid))
    if not sides:
        return pl.pallas_call(body, name=name, grid=grid, in_specs=in_specs, out_specs=out_specs, out_shape=out_shape,
                              scratch_shapes=scratch_shapes, compiler_params=params)(*args)
    n_in, n_out, n_scratch = len(in_specs), len(out_specs), len(scratch_shapes)
    counts = [len(srcs) for _, srcs in sides]
    ns = sum(counts)
    total = 1
    for g in grid:
        total *= g
    side_args, side_shapes, side_sems = [], [], []
    for exchange, srcs in sides:
        side_args += list(srcs)
        side_shapes += [jax.ShapeDtypeStruct(_landing_shape(exchange, s), s.dtype) for s in srcs]
        side_sems += [pltpu.SemaphoreType.DMA((N_DEV * len(srcs),))] * 2

    def wrapped(*refs):
        ins, side_in = refs[:n_in], refs[n_in:n_in + ns]
        outs = refs[n_in + ns:n_in + ns + n_out]
        side_out = refs[n_in + ns + n_out:n_in + 2 * ns + n_out]
        rest = refs[n_in + 2 * ns + n_out:]
        scratch, sems = rest[:n_scratch], rest[n_scratch:]
        step = pl.program_id(0)
        for axis in range(1, len(grid)):
            step = step * grid[axis] + pl.program_id(axis)

        def exchanges():
            built, at = [], 0
            for k, (exchange, _) in enumerate(sides):
                built.append(exchange(side_in[at:at + counts[k]], side_out[at:at + counts[k]],
                                      sems[2 * k], sems[2 * k + 1]))
                at += counts[k]
            return built

        @pl.when(step == 0)
        def _():
            for ex in exchanges():
                ex.start()

        @pl.when(step == max(total - MIDDLE_STEPS_BEFORE_END, 0))
        def _():
            for ex in exchanges():
                ex.middle()

        body(*ins, *outs, *scratch)

        @pl.when(step == total - 1)
        def _():
            for ex in exchanges():
                ex.finish()

    return pl.pallas_call(
        wrapped, name=name, grid=grid, in_specs=list(in_specs) + [HBM_SPEC] * ns,
        out_specs=list(out_specs) + [HBM_SPEC] * ns, out_shape=list(out_shape) + side_shapes,
        scratch_shapes=list(scratch_shapes) + side_sems, compiler_params=params)(*args, *side_args)


def _prenorm_gather(x, g, wie, small, woe, wio, woo, ts):
    S = x.shape[0]
    nt = S // ts

    def body(x_ref, g_ref, wie_ref, sm_ref, woe_ref, wio_ref, woo_ref,
             h_ref, wie_out, sm_out, woe_out, wio_b, woo_b, cast_ref, small_ref, cast2_ref, send_sems, recv_sems):
        step = pl.program_id(0)
        gather = lambda: _Gather([cast_ref, small_ref, cast2_ref], [wie_out, sm_out, woe_out], send_sems, recv_sems)

        @pl.when(step == 0)
        def _():
            cast_ref[...] = wie_ref[...].astype(BF16)
            small_ref[...] = sm_ref[...]
            cast2_ref[...] = woe_ref[...].astype(BF16)
            gather().start()
            wio_b[...] = wio_ref[...].astype(BF16)
            woo_b[...] = woo_ref[...].astype(BF16)

        xv = x_ref[...]
        h_ref[...] = ((xv * _rms(xv)) * g_ref[...]).astype(BF16)

        @pl.when(step == nt - 1)
        def _():
            exchange = gather()
            exchange.middle()
            exchange.finish()

    tile = pl.BlockSpec((ts, D), lambda i: (i, 0))
    sems = pltpu.SemaphoreType.DMA((N_DEV * 3,))
    shards = [wie, small, woe, wio, woo]
    return pl.pallas_call(
        body, name="prenorm_gather", grid=(nt,),
        in_specs=[tile, _full((1, D))] + [_full(a.shape) for a in shards],
        out_specs=[tile, HBM_SPEC, HBM_SPEC, HBM_SPEC] + [_full(a.shape) for a in shards[3:]],
        out_shape=[jax.ShapeDtypeStruct((S, D), BF16), jax.ShapeDtypeStruct((N_DEV,) + wie.shape, BF16),
                   jax.ShapeDtypeStruct((N_DEV,) + small.shape, F32), jax.ShapeDtypeStruct((N_DEV,) + woe.shape, BF16)]
        + [jax.ShapeDtypeStruct(a.shape, BF16) for a in shards[3:]],
        scratch_shapes=[pltpu.VMEM(wie.shape, BF16), pltpu.VMEM(small.shape, F32), pltpu.VMEM(woe.shape, BF16),
                        sems, sems],
        compiler_params=_params(),
    )(x, g, wie, small, woe, wio, woo)


def _even_mixer(proj, halo, row0, cw_ref, pw_ref):
    xa, gb, gc, za, xp, zp = proj
    nr = xa.shape[0]
    ha = gc * xa
    ha_ext = jnp.concatenate([halo[0], ha], axis=0)
    ha_m1 = _shift_down(ha_ext, 1)[HALO:]
    ha_m2 = _shift_down(ha_ext, 2)[HALO:]
    conv = cw_ref[2:3, :] * ha + cw_ref[1:2, :] * ha_m1 + cw_ref[0:1, :] * ha_m2
    sig_a = _sigmoid(za)
    silu_a = za * sig_a

    xp_ext = jnp.concatenate([halo[1], xp], axis=0)
    pos = row0 + lax.broadcasted_iota(jnp.int32, (nr, 1), 0)
    pooled, inv_cnt, mixed = [], [], []
    for g, w in enumerate(POOL_WINDOWS):
        cols = slice(g * GROUP, (g + 1) * GROUP)
        s = xp_ext[:, cols]
        k = 1
        while k < w:
            s = s + _shift_down(s, k)
            k *= 2
        inv = 1.0 / jnp.minimum(pos + 1, w).astype(F32)
        pg = s[HALO:] * inv - xp[:, cols]
        pooled.append(pg)
        inv_cnt.append(inv)
        mixed.append(_dot(pg.astype(BF16), pw_ref[g]))
    mixed = jnp.concatenate(mixed, axis=1)
    sig_b = _sigmoid(zp)
    silu_b = zp * sig_b
    return dict(xa=xa, gb=gb, gc=gc, za=za, xp=xp, zp=zp, ha=ha, ha_m1=ha_m1, ha_m2=ha_m2, conv=conv,
                sig_a=sig_a, silu_a=silu_a, pooled=pooled, inv_cnt=inv_cnt, mixed=mixed, sig_b=sig_b,
                silu_b=silu_b)


def _columns(a):
    return [a[:, k * A_W:(k + 1) * A_W].astype(F32) for k in range(6)]


def _even_forward_fused(h, x, win_b, cw, pw_b, ps, wout_b, gpost, gpre_next, ts, sides=()):
    S = x.shape[0]

    def body(h_ref, x_ref, wi_ref, cw_ref, pw_ref, ps_ref, wo_ref, gp_ref, gn_ref,
             p_ref, m_ref, x1_ref, h1_ref, mix_ref, carry_ref):
        i = pl.program_id(0)

        @pl.when(i == 0)
        def _():
            carry_ref[...] = jnp.zeros_like(carry_ref)

        for r0 in range(0, ts, SUB_ROWS):
            rows = slice(r0, r0 + SUB_ROWS)
            proj = _dot(h_ref[rows, :], wi_ref[...])
            p_ref[rows, :] = proj.astype(BF16)
            fw = _even_mixer(_columns(proj), (carry_ref[:, 0:A_W], carry_ref[:, A_W:D]), i * ts + r0, cw_ref, pw_ref)
            carry_ref[:, 0:A_W] = fw["ha"][SUB_ROWS - HALO:]
            carry_ref[:, A_W:D] = fw["xp"][SUB_ROWS - HALO:]
            mix_ref[rows, 0:A_W] = (fw["gb"] * fw["conv"] * fw["silu_a"]).astype(BF16)
            mix_ref[rows, A_W:D] = (fw["mixed"] * ps_ref[...] * fw["silu_b"]).astype(BF16)
            m = _dot(mix_ref[rows, :], wo_ref[...])
            m_ref[rows, :] = m.astype(BF16)
            x1 = x_ref[rows, :] + (m * _rms(m)) * gp_ref[...]
            x1_ref[rows, :] = x1
            h1_ref[rows, :] = ((x1 * _rms(x1)) * gn_ref[...]).astype(BF16)

    tile = pl.BlockSpec((ts, D), lambda i: (i, 0))
    return _call(
        body, name="even_forward", grid=(S // ts,),
        in_specs=[tile, tile, _resident((D, PROJ)), _full((3, A_W)), _full((N_GROUPS, GROUP, GROUP)), _full((1, B_W)),
                  _resident((D, D)), _full((1, D)), _full((1, D))],
        out_specs=[pl.BlockSpec((ts, PROJ), lambda i: (i, 0)), tile, tile, tile],
        out_shape=[jax.ShapeDtypeStruct((S, PROJ), BF16), jax.ShapeDtypeStruct((S, D), BF16),
                   jax.ShapeDtypeStruct((S, D), F32), jax.ShapeDtypeStruct((S, D), BF16)],
        scratch_shapes=[pltpu.VMEM((ts, D), BF16), pltpu.VMEM((HALO, D), F32)],
        args=(h, x, win_b, cw, pw_b, ps, wout_b, gpost, gpre_next), sides=sides)


def _even_backward_fused(g1, m0, p, x, cw, pw_b, pwt_b, ps, woutt_b, wint_b, gpost, gpre, ts):
    S = g1.shape[0]
    nt = S // ts

    def body(g_ref, m_ref, p_ref, halo_ref, x_ref, cw_ref, pw_ref, pwt_ref, ps_ref, wot_ref, wit_ref, gp_ref, gn_ref,
             dp_ref, dx_ref, dwob_ref, dcw_ref, dpw_ref, dps_ref, dgp_ref, dgn_ref,
             carry_ref, mix_ref, dm_ref, dwo_ref):
        step = pl.program_id(0)
        i = nt - 1 - step

        @pl.when(step == 0)
        def _():
            carry_ref[...] = jnp.zeros_like(carry_ref)
            dwo_ref[...] = jnp.zeros_like(dwo_ref)
            dcw_ref[...] = jnp.zeros_like(dcw_ref)
            dpw_ref[...] = jnp.zeros_like(dpw_ref)
            dps_ref[...] = jnp.zeros_like(dps_ref)
            dgp_ref[...] = jnp.zeros_like(dgp_ref)
            dgn_ref[...] = jnp.zeros_like(dgn_ref)

        ps_v = ps_ref[...]
        sums = [None] * 6
        add = lambda k, part: sums.__setitem__(k, part if sums[k] is None else sums[k] + part)
        for r0 in range(ts - SUB_ROWS, -1, -SUB_ROWS):
            rows = slice(r0, r0 + SUB_ROWS)
            g = g_ref[rows, :].astype(F32)
            m = m_ref[rows, :].astype(F32)
            q = _rms(m)
            n = m * q
            add(0, jnp.sum(g * n, axis=0, keepdims=True))
            dn = g * gp_ref[...]
            dm = q * (dn - n * jnp.mean(dn * n, axis=-1, keepdims=True))
            dm_ref[rows, :] = dm.astype(BF16)
            dmix = _dot(dm_ref[rows, :], wot_ref[...])
            dya = dmix[:, 0:A_W]
            dyb = dmix[:, A_W:D]

            if r0 == 0:
                before = _columns(halo_ref[...])
                keep = jnp.where(i == 0, 0.0, 1.0).astype(F32)
                halo = (before[2] * before[0] * keep, before[4] * keep)
            else:
                before = _columns(p_ref[r0 - HALO:r0, :])
                halo = (before[2] * before[0], before[4])
            fw = _even_mixer(_columns(p_ref[rows, :]), halo, i * ts + r0, cw_ref, pw_ref)
            mix_ref[rows, 0:A_W] = (fw["gb"] * fw["conv"] * fw["silu_a"]).astype(BF16)
            mix_ref[rows, A_W:D] = (fw["mixed"] * ps_v * fw["silu_b"]).astype(BF16)

            t = dya * fw["gb"]
            dconv = t * fw["silu_a"]
            dgb = dya * fw["conv"] * fw["silu_a"]
            dza = t * fw["conv"] * _dsilu(fw["za"], fw["sig_a"])
            add(2, jnp.sum(dconv * fw["ha"], axis=0, keepdims=True))
            add(3, jnp.sum(dconv * fw["ha_m1"], axis=0, keepdims=True))
            add(4, jnp.sum(dconv * fw["ha_m2"], axis=0, keepdims=True))
            dconv_ext = jnp.concatenate([dconv, carry_ref[:, 0:A_W]], axis=0)
            dha = (cw_ref[2:3, :] * dconv + cw_ref[1:2, :] * _shift_up(dconv_ext, 1)[:SUB_ROWS]
                   + cw_ref[0:1, :] * _shift_up(dconv_ext, 2)[:SUB_ROWS])
            dgc = dha * fw["xa"]
            dxa = dha * fw["gc"]

            u = dyb * fw["mixed"]
            add(1, jnp.sum(u * fw["silu_b"], axis=0, keepdims=True))
            dzp = u * ps_v * _dsilu(fw["zp"], fw["sig_b"])
            dmixed = (dyb * ps_v * fw["silu_b"]).astype(BF16)
            dxp, e_first = [], []
            for gi, w in enumerate(POOL_WINDOWS):
                cols = slice(gi * GROUP, (gi + 1) * GROUP)
                dmg = dmixed[:, cols]
                dpooled = _dot(dmg, pwt_ref[gi])
                dpw_ref[gi] += _dot_t0(fw["pooled"][gi].astype(BF16), dmg)
                e = dpooled * fw["inv_cnt"][gi]
                e_first.append(e[0:HALO])
                s = jnp.concatenate([e, carry_ref[:, A_W + gi * GROUP:A_W + (gi + 1) * GROUP]], axis=0)
                k = 1
                while k < w:
                    s = s + _shift_up(s, k)
                    k *= 2
                dxp.append(s[:SUB_ROWS] - dpooled)
            carry_ref[:, 0:A_W] = dconv[0:HALO]
            carry_ref[:, A_W:D] = jnp.concatenate(e_first, axis=1)

            dp_ref[rows, 0:512] = dxa.astype(BF16)
            dp_ref[rows, 512:1024] = dgb.astype(BF16)
            dp_ref[rows, 1024:1536] = dgc.astype(BF16)
            dp_ref[rows, 1536:2048] = dza.astype(BF16)
            dp_ref[rows, 2048:2560] = jnp.concatenate(dxp, axis=1).astype(BF16)
            dp_ref[rows, 2560:3072] = dzp.astype(BF16)

            dh = _dot(dp_ref[rows, :], wit_ref[...])
            xv = x_ref[rows, :]
            r = _rms(xv)
            xn = xv * r
            add(5, jnp.sum(dh * xn, axis=0, keepdims=True))
            dxn = dh * gn_ref[...]
            dx_ref[rows, :] = g + r * (dxn - xn * jnp.mean(dxn * xn, axis=-1, keepdims=True))

        dwo_ref[...] += _dot_t0(mix_ref[...], dm_ref[...])
        dgp_ref[...] += sums[0]
        dps_ref[...] += sums[1]
        dcw_ref[2:3, :] += sums[2]
        dcw_ref[1:2, :] += sums[3]
        dcw_ref[0:1, :] += sums[4]
        dgn_ref[...] += sums[5]

        @pl.when(step == nt - 1)
        def _():
            dwob_ref[...] = dwo_ref[...].astype(BF16)

    rev = lambda s: (nt - 1 - s, 0)
    tile = pl.BlockSpec((ts, D), rev)
    wide = pl.BlockSpec((ts, PROJ), rev)
    bpt = ts // HALO
    halo_map = lambda s: (jnp.maximum((nt - 1 - s) * bpt - 1, 0), 0)
    groups = _full((N_GROUPS, GROUP, GROUP))
    vec = _full((1, D))
    return pl.pallas_call(
        body, name="even_backward", grid=(nt,),
        in_specs=[tile, tile, wide, pl.BlockSpec((HALO, PROJ), halo_map), tile, _full((3, A_W)), groups, groups,
                  _full((1, B_W)), _resident((D, D)), _resident((PROJ, D)), vec, vec],
        out_specs=[wide, tile, _full((D, D)), _full((8, A_W)), groups, _full((1, B_W)), vec, vec],
        out_shape=[jax.ShapeDtypeStruct((S, PROJ), BF16), jax.ShapeDtypeStruct((S, D), F32),
                   jax.ShapeDtypeStruct((D, D), BF16), jax.ShapeDtypeStruct((8, A_W), F32),
                   jax.ShapeDtypeStruct((N_GROUPS, GROUP, GROUP), F32), jax.ShapeDtypeStruct((1, B_W), F32),
                   jax.ShapeDtypeStruct((1, D), F32), jax.ShapeDtypeStruct((1, D), F32)],
        scratch_shapes=[pltpu.VMEM((HALO, D), F32), pltpu.VMEM((ts, D), BF16), pltpu.VMEM((ts, D), BF16),
                        pltpu.VMEM((D, D), F32)],
        compiler_params=_params(),
    )(g1, m0, p, p, x, cw, pw_b, pwt_b, ps, woutt_b, wint_b, gpost, gpre)


def _gate_matmuls(w_ref, in_ref, out_ref, chunks, bias_ref=None):
    for h in range(N_HEADS):
        cols = slice(h * GROUP, (h + 1) * GROUP)
        wide = jnp.concatenate([in_ref[chunk, cols] for chunk in chunks], axis=1)
        res = _dot(w_ref[h], wide)
        for k, chunk in enumerate(chunks):
            part = res[:, k * GROUP:(k + 1) * GROUP]
            out_ref[chunk, cols] = part if bias_ref is None else part + bias_ref[h]


def _odd_forward_parts(v, lg_ref, lb_ref, wt_ref, bfull_ref, vln_ref, sv_ref, r0, nr):
    rows = slice(r0, r0 + nr)
    mu = jnp.mean(v, axis=-1, keepdims=True)
    vc = v - mu
    rstd = lax.rsqrt(jnp.mean(vc * vc, axis=-1, keepdims=True) + EPS)
    vh = vc * rstd
    vln_ref[rows, :] = (vh * lg_ref[...] + lb_ref[...]).astype(BF16)
    _gate_matmuls(wt_ref, vln_ref, sv_ref, [slice(r0 + k * GROUP, r0 + (k + 1) * GROUP) for k in range(nr // GROUP)],
                  bfull_ref)
    return vh, rstd


def _odd_forward(h, x1, tgt, win_b, lg, lb, wt_b, bfull, wout_b, gpost, ts):
    S = x1.shape[0]

    def body(h_ref, x_ref, t_ref, wi_ref, lg_ref, lb_ref, wt_ref, bfull_ref, wo_ref, gp_ref,
             p_ref, m_ref, g_ref, loss_ref, vln_ref, sv_ref, y_ref):
        @pl.when(pl.program_id(0) == 0)
        def _():
            loss_ref[...] = jnp.zeros_like(loss_ref)

        loss = None
        for r0 in range(0, ts, SUB_ROWS):
            rows = slice(r0, r0 + SUB_ROWS)
            proj = _dot(h_ref[rows, :], wi_ref[...])
            p_ref[rows, :] = proj.astype(BF16)
            u, v, z = proj[:, 0:D], proj[:, D:2 * D], proj[:, 2 * D:3 * D]
            _odd_forward_parts(v, lg_ref, lb_ref, wt_ref, bfull_ref, vln_ref, sv_ref, r0, SUB_ROWS)
            y_ref[rows, :] = (u * sv_ref[rows, :] * (z * _sigmoid(z))).astype(BF16)
            m = _dot(y_ref[rows, :], wo_ref[...])
            m_ref[rows, :] = m.astype(BF16)
            x2 = x_ref[rows, :] + (m * _rms(m)) * gp_ref[...]
            err = x2 - t_ref[rows, :]
            g_ref[rows, :] = (err * (1.0 / D)).astype(BF16)
            part = jnp.sum(err * err, axis=0, keepdims=True)
            loss = part if loss is None else loss + part
        loss_ref[...] += loss

    tile = pl.BlockSpec((ts, D), lambda i: (i, 0))
    small = _full((N_HEADS, GROUP, GROUP))
    return pl.pallas_call(
        body, name="odd_forward", grid=(S // ts,),
        in_specs=[pl.BlockSpec((ts, D), lambda i: (i, 0)), tile, tile, _full((D, PROJ)), _full((1, D)), _full((1, D)),
                  small, small, _full((D, D)), _full((1, D))],
        out_specs=[pl.BlockSpec((ts, PROJ), lambda i: (i, 0)), tile, tile, _full((1, D))],
        out_shape=[jax.ShapeDtypeStruct((S, PROJ), BF16), jax.ShapeDtypeStruct((S, D), BF16),
                   jax.ShapeDtypeStruct((S, D), BF16), jax.ShapeDtypeStruct((1, D), F32)],
        scratch_shapes=[pltpu.VMEM((ts, D), BF16), pltpu.VMEM((ts, D), F32), pltpu.VMEM((ts, D), BF16)],
        compiler_params=_params(),
    )(h, x1, tgt, win_b, lg, lb, wt_b, bfull, wout_b, gpost)


def _odd_backward(g2, m1, p, x1, lg, lb, wt_b, wtt_b, bfull, woutt_b, wint_b, gpost, gpre, ts):
    S = g2.shape[0]
    nt = S // ts
    sub = ts // 2

    def body(g_ref, m_ref, p_ref, x_ref, lg_ref, lb_ref, wt_ref, wtt_ref, bfull_ref, wot_ref, wit_ref, gp_ref, gn_ref,
             dp_ref, dx_ref, dwob_ref, dws_ref, dbs_ref, dlg_ref, dlb_ref, dgp_ref, dgn_ref,
             vln_ref, sv_ref, dsvb_ref, dvln_ref, dsum_ref, y_ref, dm_ref, dwo_ref):
        step = pl.program_id(0)

        @pl.when(step == 0)
        def _():
            dwo_ref[...] = jnp.zeros_like(dwo_ref)
            dws_ref[...] = jnp.zeros_like(dws_ref)
            dsum_ref[...] = jnp.zeros_like(dsum_ref)
            dlg_ref[...] = jnp.zeros_like(dlg_ref)
            dlb_ref[...] = jnp.zeros_like(dlb_ref)
            dgp_ref[...] = jnp.zeros_like(dgp_ref)
            dgn_ref[...] = jnp.zeros_like(dgn_ref)

        tril = (lax.broadcasted_iota(jnp.int32, (GROUP, GROUP), 0)
                >= lax.broadcasted_iota(jnp.int32, (GROUP, GROUP), 1))
        sums = [None] * 4
        add = lambda k, part: sums.__setitem__(k, part if sums[k] is None else sums[k] + part)

        def post_norm(c):
            g = g_ref[c["rows"], :].astype(F32)
            m = m_ref[c["rows"], :].astype(F32)
            q = _rms(m)
            n = m * q
            add(0, jnp.sum(g * n, axis=0, keepdims=True))
            dn = g * gp_ref[...]
            dm = q * (dn - n * jnp.mean(dn * n, axis=-1, keepdims=True))
            dm_ref[c["rows"], :] = dm.astype(BF16)

        def out_projection(c):
            c["dy"] = _dot(dm_ref[c["rows"], :], wot_ref[...])

        def layernorm(c):
            v = p_ref[c["rows"], D:2 * D].astype(F32)
            mu = jnp.mean(v, axis=-1, keepdims=True)
            vc = v - mu
            c["rstd"] = lax.rsqrt(jnp.mean(vc * vc, axis=-1, keepdims=True) + EPS)
            c["vh"] = vc * c["rstd"]
            vln_ref[c["rows"], :] = (c["vh"] * lg_ref[...] + lb_ref[...]).astype(BF16)

        def gate_matmuls(c):
            _gate_matmuls(wt_ref, vln_ref, sv_ref, c["chunks"], bfull_ref)

        def gating(c):
            rows = c["rows"]
            u = p_ref[rows, 0:D].astype(F32)
            z = p_ref[rows, 2 * D:3 * D].astype(F32)
            sv = sv_ref[rows, :]
            sig = _sigmoid(z)
            sz = z * sig
            y_ref[rows, :] = (u * sv * sz).astype(BF16)
            dy = c.pop("dy")
            t = dy * sz
            dsv = t * u
            dsvb_ref[rows, :] = dsv.astype(BF16)
            for k in range(sub // GROUP):
                dsum_ref[...] += dsv[k * GROUP:(k + 1) * GROUP]
            dp_ref[rows, 0:D] = (t * sv).astype(BF16)
            dp_ref[rows, 2 * D:3 * D] = (dy * u * sv * _dsilu(z, sig)).astype(BF16)

        def gate_backward_matmuls(c):
            for chunk in c["chunks"]:
                for h in range(N_HEADS):
                    cols = slice(h * GROUP, (h + 1) * GROUP)
                    dvln_ref[chunk, cols] = _dot(wtt_ref[h], dsvb_ref[chunk, cols])
                    dws_ref[h] += jnp.where(tril, _dot_t1(dsvb_ref[chunk, cols], vln_ref[chunk, cols]), 0.0)

        def layernorm_backward(c):
            vh, rstd = c.pop("vh"), c.pop("rstd")
            dvln = dvln_ref[c["rows"], :]
            add(1, jnp.sum(dvln * vh, axis=0, keepdims=True))
            add(2, jnp.sum(dvln, axis=0, keepdims=True))
            dvh = dvln * lg_ref[...]
            dv = rstd * (dvh - jnp.mean(dvh, axis=-1, keepdims=True)
                         - vh * jnp.mean(dvh * vh, axis=-1, keepdims=True))
            dp_ref[c["rows"], D:2 * D] = dv.astype(BF16)

        def in_projection(c):
            c["dh"] = _dot(dp_ref[c["rows"], :], wit_ref[...])

        def pre_norm(c):
            dh = c.pop("dh")
            xv = x_ref[c["rows"], :]
            r = _rms(xv)
            xn = xv * r
            add(3, jnp.sum(dh * xn, axis=0, keepdims=True))
            dxn = dh * gn_ref[...]
            dx_ref[c["rows"], :] = (g_ref[c["rows"], :].astype(F32)
                                    + r * (dxn - xn * jnp.mean(dxn * xn, axis=-1, keepdims=True))).astype(BF16)

        phases = [post_norm, out_projection, layernorm, gate_matmuls, gating, gate_backward_matmuls,
                  layernorm_backward, in_projection, pre_norm]
        groups = [dict(rows=slice(r0, r0 + sub),
                       chunks=[slice(r0 + k * GROUP, r0 + (k + 1) * GROUP) for k in range(sub // GROUP)])
                  for r0 in range(0, ts, sub)]
        for group in groups:
            for phase in phases:
                phase(group)

        dwo_ref[...] += _dot_t0(y_ref[...], dm_ref[...])
        dgp_ref[...] += sums[0]
        dlg_ref[...] += sums[1]
        dlb_ref[...] += sums[2]
        dgn_ref[...] += sums[3]

        @pl.when(step == nt - 1)
        def _():
            dwob_ref[...] = dwo_ref[...].astype(BF16)
            ones = jnp.ones((8, GROUP), F32)
            for h in range(N_HEADS):
                cols = slice(h * GROUP, (h + 1) * GROUP)
                total = lax.dot_general(ones, dsum_ref[:, cols], (((1,), (1,)), ((), ())),
                                        precision=lax.Precision.HIGHEST, preferred_element_type=F32)
                dbs_ref[h:h + 1, :] = total[0:1, :]

    tile = pl.BlockSpec((ts, D), lambda i: (i, 0))
    wide = pl.BlockSpec((ts, PROJ), lambda i: (i, 0))
    small = _full((N_HEADS, GROUP, GROUP))
    heads = _resident((N_HEADS, GROUP, GROUP))
    vec = _full((1, D))
    return pl.pallas_call(
        body, name="odd_backward", grid=(nt,),
        in_specs=[tile, tile, wide, tile, vec, vec, heads, heads, heads, _resident((D, D)), _resident((PROJ, D)),
                  vec, vec],
        out_specs=[wide, tile, _full((D, D)), small, _full((N_HEADS, GROUP)), vec, vec, vec, vec],
        out_shape=[jax.ShapeDtypeStruct((S, PROJ), BF16), jax.ShapeDtypeStruct((S, D), BF16),
                   jax.ShapeDtypeStruct((D, D), BF16),
                   jax.ShapeDtypeStruct((N_HEADS, GROUP, GROUP), F32), jax.ShapeDtypeStruct((N_HEADS, GROUP), F32),
                   jax.ShapeDtypeStruct((1, D), F32), jax.ShapeDtypeStruct((1, D), F32),
                   jax.ShapeDtypeStruct((1, D), F32), jax.ShapeDtypeStruct((1, D), F32)],
        scratch_shapes=[pltpu.VMEM((ts, D), BF16), pltpu.VMEM((ts, D), F32), pltpu.VMEM((ts, D), BF16),
                        pltpu.VMEM((ts, D), F32), pltpu.VMEM((GROUP, D), F32), pltpu.VMEM((ts, D), BF16),
                        pltpu.VMEM((ts, D), BF16), pltpu.VMEM((D, D), F32)],
        compiler_params=_params(),
    )(g2, m1, p, x1, lg, lb, wt_b, wtt_b, bfull, woutt_b, wint_b, gpost, gpre)


def _weight_grad_scatter(h, dp, ts, name, sides=()):
    S = h.shape[0]
    ts = min(ts, S)
    nt = S // ts
    panel = 2 * SHARD_IN
    last = N_CHIPS - 1

    def body(h_ref, dp_ref, parts_ref, acc_ref, half_ref, swap_ref, sum_ref, send_sems, recv_sems):
        s, t = pl.program_id(0), pl.program_id(1)
        x, y, c = _mesh_position()
        sibling = (x, y, 1 - c)

        def swap(k):
            return _remote(half_ref.at[k, 1], swap_ref.at[k], send_sems, recv_sems, k, sibling)

        def chip_sum(k):
            flip = last - k
            to = (x ^ (flip >> 1), y ^ (flip & 1), c)
            return _remote(sum_ref.at[k], parts_ref.at[1 + flip], send_sems, recv_sems, 3 + flip, to)

        own = pltpu.make_async_copy(half_ref.at[last, 0], parts_ref.at[0], recv_sems.at[7])
        to_sibling = _remote(half_ref.at[last, 1], parts_ref.at[1], send_sems, recv_sems, 3, sibling)

        @pl.when(t == 0)
        def _():
            acc_ref[...] = jnp.zeros_like(acc_ref)

        for k in range(last):
            @pl.when((s == k + 1) & (t == 0))
            def _():
                swap(k).wait_recv()
                sum_ref[k] = (half_ref[k, 0].astype(F32) + swap_ref[k].astype(F32)).astype(BF16)
                chip_sum(k).start()

        acc_ref[...] += _dot_t0(h_ref[...], dp_ref[...])

        @pl.when(t == nt - 1)
        def _():
            for core in range(2):
                @pl.when(c == core)
                def _():
                    half_ref[s, core] = acc_ref[:, 0:SHARD_IN].astype(BF16)
                    half_ref[s, 1 - core] = acc_ref[:, SHARD_IN:panel].astype(BF16)
            for k in range(last):
                @pl.when(s == k)
                def _():
                    swap(k).start()

        @pl.when((s == last) & (t == nt - 1))
        def _():
            own.start()
            to_sibling.start()
            own.wait()
            _remote(half_ref.at[last, 1], parts_ref.at[1], send_sems, recv_sems, 3, sibling).wait_recv()
            for k in range(last):
                chip_sum(k).wait_recv()
            to_sibling.wait_send()
            for k in range(last):
                swap(k).wait_send()
                chip_sum(k).wait_send()

    chip_panel = lambda s, t: (t, (2 * lax.axis_index("x") + lax.axis_index("y")) ^ (last - s))
    sems = pltpu.SemaphoreType.DMA((8,))
    return _call(
        body, name=name, grid=(N_CHIPS, nt),
        in_specs=[pl.BlockSpec((ts, D), lambda s, t: (t, 0)), pl.BlockSpec((ts, panel), chip_panel)],
        out_specs=[HBM_SPEC], out_shape=[jax.ShapeDtypeStruct((N_CHIPS + 1, D, SHARD_IN), BF16)],
        scratch_shapes=[pltpu.VMEM((D, panel), F32), pltpu.VMEM((N_CHIPS, 2, D, SHARD_IN), BF16),
                        pltpu.VMEM((last, D, SHARD_IN), BF16), pltpu.VMEM((last, D, SHARD_IN), BF16), sems, sems],
        args=(h, dp), sides=sides)


def _adamw(w, g, m, v):
    m = ADAM_B1 * m + (1.0 - ADAM_B1) * g
    v = ADAM_B2 * v + (1.0 - ADAM_B2) * (g * g)
    m_hat = m / (1.0 - ADAM_B1 ** ADAM_STEP)
    v_hat = v / (1.0 - ADAM_B2 ** ADAM_STEP)
    delta = -ADAM_LR * (m_hat / (jnp.sqrt(v_hat) + ADAM_EPS) + ADAM_WD * w)
    return delta, m, v


UPDATE_STEPS = 4


def _sum_update(parts, weights):
    n_w = len(weights)

    def body(*refs):
        p_refs, wmv_refs, outs = refs[:n_w], refs[n_w:4 * n_w], refs[4 * n_w:]
        for i in range(n_w):
            acc = p_refs[i][0].astype(F32)
            for d in range(1, parts[i].shape[0]):
                acc = acc + p_refs[i][d].astype(F32)
            w_ref, m_ref, v_ref = wmv_refs[3 * i:3 * i + 3]
            g_out, d_out, m_out, v_out = outs[4 * i:4 * i + 4]
            g_out[...] = acc
            delta, m_new, v_new = _adamw(w_ref[...], acc, m_ref[...], v_ref[...])
            d_out[...] = delta
            m_out[...] = m_new
            v_out[...] = v_new

    part_specs, wmv_specs, out_specs, out_shape = [], [], [], []
    for p, (w, _, _) in zip(parts, weights):
        R, C = w.shape
        rb = R // UPDATE_STEPS
        blk = pl.BlockSpec((rb, C), lambda i: (i, 0))
        part_specs.append(pl.BlockSpec((p.shape[0], rb, C), lambda i: (0, i, 0)))
        wmv_specs += [blk] * 3
        out_specs += [blk] * 4
        out_shape += [jax.ShapeDtypeStruct((R, C), F32)] * 4
    flat = pl.pallas_call(
        body, name="update_large_weights", grid=(UPDATE_STEPS,), in_specs=part_specs + wmv_specs,
        out_specs=out_specs, out_shape=out_shape, compiler_params=_params(),
    )(*parts, *[a for wmv in weights for a in wmv])
    return [flat[4 * i:4 * i + 4] for i in range(n_w)]


def _small_sum_update(partials, row_of, weights, loss_parts):
    n_p, n_w = len(partials), len(weights)

    def body(*refs):
        p_refs = refs[:n_p]
        loss_ref = refs[n_p]
        w_refs = refs[n_p + 1:n_p + 1 + 3 * n_w]
        loss_out = refs[n_p + 1 + 3 * n_w]
        outs = refs[n_p + 2 + 3 * n_w:]

        def total(ref):
            acc = ref[0].astype(F32)
            for d in range(1, N_DEV):
                acc = acc + ref[d].astype(F32)
            return acc

        lsum = jnp.sum(total(loss_ref), axis=-1, keepdims=True) * (0.5 / D)
        loss_out[...] = jnp.broadcast_to(lsum, loss_out.shape)
        for k in range(n_p):
            i, r0 = row_of[k]
            g = total(p_refs[k])
            w_ref, m_ref, v_ref = w_refs[3 * i:3 * i + 3]
            g_out, d_out, m_out, v_out = outs[4 * i:4 * i + 4]
            if g.ndim == 2:
                at = (slice(r0, r0 + g.shape[0]), slice(None))
            else:
                at = (slice(None),) * g.ndim
            delta, m_new, v_new = _adamw(w_ref[at], g, m_ref[at], v_ref[at])
            g_out[at] = g
            d_out[at] = delta
            m_out[at] = m_new
            v_out[at] = v_new

    vm = pl.BlockSpec(memory_space=pltpu.VMEM)
    flat_w = [a for wmv in weights for a in wmv]
    out_shape = [jax.ShapeDtypeStruct((1, 128), F32)]
    for w, _, _ in weights:
        out_shape += [jax.ShapeDtypeStruct(w.shape, F32)] * 4
    return pl.pallas_call(
        body, name="small_update", in_specs=[vm] * (n_p + 1 + 3 * n_w), out_specs=[vm] * len(out_shape),
        out_shape=out_shape, compiler_params=pltpu.CompilerParams(vmem_limit_bytes=VMEM_LIMIT),
    )(*partials, loss_parts, *flat_w)


def _pack_small_shard(cw, lg, lb):
    return (jnp.pad(cw, ((0, 13), (0, 64))) + jnp.pad(lg, ((8, 7), (0, 0))) + jnp.pad(lb, ((9, 6), (0, 0))))


def kernel(x, pre_norm, post_norm, even_w_in, even_conv_w, even_pool_w, even_pool_scale, even_w_out, odd_w_in, odd_ln_g, odd_ln_b, odd_w_s, odd_b_s, odd_w_out, loss_target, m_pre_norm, m_post_norm, m_even_w_in, m_even_conv_w, m_even_pool_w, m_even_pool_scale, m_even_w_out, m_odd_w_in, m_odd_ln_g, m_odd_ln_b, m_odd_w_s, m_odd_b_s, m_odd_w_out, v_pre_norm, v_post_norm, v_even_w_in, v_even_conv_w, v_even_pool_w, v_even_pool_scale, v_even_w_out, v_odd_w_in, v_odd_ln_g, v_odd_ln_b, v_odd_w_s, v_odd_b_s, v_odd_w_out):
    S = x.shape[1]
    xs = x.reshape(S, D)
    tgt = loss_target.reshape(S, D)

    gpre0, gpre1 = pre_norm[0:1], pre_norm[1:2]
    gpost0, gpost1 = post_norm[0:1], post_norm[1:2]
    small_shard = _pack_small_shard(even_conv_w[0], odd_ln_g, odd_ln_b)
    h0, wie_g, small_g, woe_g, wio_b, woo_b = _prenorm_gather(xs, gpre0, even_w_in[0], small_shard, even_w_out[0],
                                                              odd_w_in[0], odd_w_out[0], TS_PRENORM)
    wie = jnp.transpose(wie_g, (1, 0, 2)).reshape(D, PROJ)
    wie_t = jnp.transpose(wie_g, (0, 2, 1)).reshape(PROJ, D)
    woe = woe_g.reshape(D, D)
    woe_t = woe.T
    conv_w = jnp.transpose(small_g[:, 0:3, 0:64], (1, 0, 2)).reshape(3, A_W)
    ln_g = small_g[:, 8, :].reshape(1, D)
    ln_b = small_g[:, 9, :].reshape(1, D)

    pool_w_b = even_pool_w[0].astype(BF16)
    pool_wt_b = jnp.swapaxes(even_pool_w[0], 1, 2).astype(BF16)
    ws_tril = jnp.tril(odd_w_s[0])
    ws_b = ws_tril.astype(BF16)
    wst_b = jnp.swapaxes(ws_tril, 1, 2).astype(BF16)
    b_full = jnp.broadcast_to(odd_b_s[0][:, :, None], (N_HEADS, GROUP, GROUP))

    p0, m0, x1, h1, wio_g, woo_g = _even_forward_fused(h0, xs, wie, conv_w, pool_w_b, even_pool_scale, woe, gpost0,
                                                       gpre1, TS_FWD, sides=[(_Gather, [wio_b, woo_b])])
    wio = jnp.transpose(wio_g, (1, 0, 2)).reshape(D, PROJ)
    wio_t = jnp.transpose(wio_g, (0, 2, 1)).reshape(PROJ, D)
    woo = woo_g.reshape(D, D)
    woo_t = woo.T
    p1, m1, g2, loss_vec = _odd_forward(h1, x1, tgt, wio, ln_g, ln_b, ws_b, b_full, woo, gpost1, TS_FWD)

    dp1, g1, dwoo, dws, dbs, dlg, dlb, dgpost1, dgpre1 = _odd_backward(
        g2, m1, p1, x1, ln_g, ln_b, ws_b, wst_b, b_full, woo_t, wio_t, gpost1, gpre1, TS_BWD)
    by_owner = lambda dwo: dwo.reshape(N_DEV, SHARD_OUT, D)
    (dwio_parts, dwoo_parts, dlg_parts, dlb_parts,
     dws_g, dbs_g, gpost1_g, gpre1_g, loss_g) = _weight_grad_scatter(
        h1, dp1, TS_WGRAD, "odd_weight_grad",
        sides=[(_Scatter, [by_owner(dwoo), dlg.reshape(N_DEV, 1, 128), dlb.reshape(N_DEV, 1, 128)]),
               (_Gather, [dws.astype(BF16), dbs, dgpost1, dgpre1, loss_vec])])
    dp0, gx, dwoe, dcw, dpw, dps, dgpost0, dgpre0 = _even_backward_fused(
        g1, m0, p0, xs, conv_w, pool_w_b, pool_wt_b, even_pool_scale, woe_t, wie_t, gpost0, gpre0, TS_FWD)
    dcw_by_owner = jnp.transpose(dcw[0:3].reshape(3, N_DEV, 64), (1, 0, 2))
    dwie_parts, dwoe_parts, dcw_parts, dpw_g, dps_g, gpost0_g, gpre0_g = _weight_grad_scatter(
        h0, dp0, TS_WGRAD, "even_weight_grad",
        sides=[(_Scatter, [by_owner(dwoe), dcw_by_owner]), (_Gather, [dpw.astype(BF16), dps, dgpost0, dgpre0])])

    ((g_wie, d_wie, nm_wie, nv_wie), (g_wio, d_wio, nm_wio, nv_wio), (g_woe, d_woe, nm_woe, nv_woe),
     (g_woo, d_woo, nm_woo, nv_woo)) = _sum_update(
        [dwie_parts, dwio_parts, dwoe_parts, dwoo_parts],
        [(even_w_in[0], m_even_w_in[0], v_even_w_in[0]), (odd_w_in[0], m_odd_w_in[0], v_odd_w_in[0]),
         (even_w_out[0], m_even_w_out[0], v_even_w_out[0]), (odd_w_out[0], m_odd_w_out[0], v_odd_w_out[0])])

    partials = [gpre0_g, gpre1_g, gpost0_g, gpost1_g, dpw_g, dps_g, dws_g, dbs_g, dcw_parts, dlg_parts, dlb_parts]
    row_of = [(0, 0), (0, 1), (1, 0), (1, 1), (2, 0), (3, 0), (4, 0), (5, 0), (6, 0), (7, 0), (8, 0)]
    weights = [(pre_norm, m_pre_norm, v_pre_norm), (post_norm, m_post_norm, v_post_norm),
               (even_pool_w[0], m_even_pool_w[0], v_even_pool_w[0]),
               (even_pool_scale, m_even_pool_scale, v_even_pool_scale),
               (odd_w_s[0], m_odd_w_s[0], v_odd_w_s[0]), (odd_b_s[0], m_odd_b_s[0], v_odd_b_s[0]),
               (even_conv_w[0], m_even_conv_w[0], v_even_conv_w[0]),
               (odd_ln_g, m_odd_ln_g, v_odd_ln_g), (odd_ln_b, m_odd_ln_b, v_odd_ln_b)]
    small = _small_sum_update(partials, row_of, weights, loss_g)
    loss = small[0][0, 0]

    def leaves(k, big):
        pre, post, pw, psc, ws, bs, cw, lg, lb = [small[1 + 4 * i + k] for i in range(len(weights))]
        wie_k, woe_k, wio_k, woo_k = big
        return [pre, post, wie_k[None], cw[None], pw[None], psc, woe_k[None], wio_k[None], lg, lb, ws[None], bs[None],
                woo_k[None]]

    outs = [loss, gx.reshape(1, S, D)]
    outs += leaves(0, (g_wie, g_woe, g_wio, g_woo))
    outs += leaves(1, (d_wie, d_woe, d_wio, d_woo))
    outs += leaves(2, (nm_wie, nm_woe, nm_wio, nm_woo))
    outs += leaves(3, (nv_wie, nv_woe, nv_wio, nv_woo))
    return tuple(outs)
```

```python
import jax
import jax.numpy as jnp
from jax import lax
from jax.experimental import pallas as pl
from jax.experimental.pallas import tpu as pltpu

F32 = jnp.float32
BF16 = jnp.bfloat16
MESH = pl.DeviceIdType.MESH

D = 1024
EPS = 1e-6
A_W = 512
B_W = 512
POOL_WINDOWS = (2, 4, 8, 16)
GROUP = 128
N_GROUPS = 4
N_HEADS = 8
HALO = 16
PROJ = 3072
N_DEV = 8
N_CHIPS = 4
SHARD_IN = PROJ // N_DEV
SHARD_OUT = D // N_DEV

ADAM_LR = 0.001
ADAM_B1 = 0.9
ADAM_B2 = 0.999
ADAM_EPS = 1e-08
ADAM_WD = 0.01
ADAM_STEP = 10

VMEM_LIMIT = 54 * 1024 * 1024

TS_PRENORM = 1024
TS_FWD = 512
TS_BWD = 256
TS_WGRAD = 2048
SUB_ROWS = 256


def _params(n_grid=1, vmem=VMEM_LIMIT):
    return pltpu.CompilerParams(dimension_semantics=("arbitrary",) * n_grid, vmem_limit_bytes=vmem)


def _full(shape):
    return pl.BlockSpec(shape, lambda *_: (0,) * len(shape))


def _resident(shape):
    return pl.BlockSpec(shape, lambda *_: (0,) * len(shape), pipeline_mode=pl.Buffered(1))


def _load_columns(blocks_ref, whole_ref, sems):
    copies = [pltpu.make_async_copy(blocks_ref.at[k], whole_ref.at[:, k * SHARD_IN:(k + 1) * SHARD_IN], sems.at[k])
              for k in range(N_DEV)]
    for cp in copies:
        cp.start()
    for cp in copies:
        cp.wait()


def _sigmoid(z):
    return jax.nn.sigmoid(z)


def _dsilu(z, s):
    return s * (1.0 + z * (1.0 - s))


def _dot(a, b):
    return jnp.dot(a, b, preferred_element_type=F32)


def _dot_t0(a, b):
    return lax.dot_general(a, b, (((0,), (0,)), ((), ())), preferred_element_type=F32)


def _dot_t1(a, b):
    return lax.dot_general(a, b, (((1,), (1,)), ((), ())), preferred_element_type=F32)


def _rms(x):
    return lax.rsqrt(jnp.mean(x * x, axis=-1, keepdims=True) + EPS)


def _shift_down(a, k):
    return pltpu.roll(a, k, 0)


def _shift_up(a, k):
    return pltpu.roll(a, a.shape[0] - k, 0)


def _mesh_position():
    return lax.axis_index("x"), lax.axis_index("y"), lax.axis_index("c")


def _index(pos):
    return 4 * pos[0] + 2 * pos[1] + pos[2]


def _peer(pos, d):
    x, y, c = pos
    return (1 - x if d & 4 else x, 1 - y if d & 2 else y, 1 - c if d & 1 else c)


def _remote(src, dst, send_sems, recv_sems, k, to):
    return pltpu.make_async_remote_copy(src_ref=src, dst_ref=dst, send_sem=send_sems.at[k], recv_sem=recv_sems.at[k],
                                        device_id=to, device_id_type=MESH)


class _Gather:
    def __init__(self, srcs, dsts, send_sems, recv_sems):
        x, y, c = _mesh_position()
        n = len(srcs)
        me, sibling = (x, y, c), (x, y, 1 - c)
        chips = [(1 - x, y), (x, 1 - y), (1 - x, 1 - y)]

        def copy(k, t, block, to, src=None):
            slot = dsts[t].at[_index(block)]
            return _remote(slot if src is None else src, slot, send_sems, recv_sems, k * n + t, to)

        tensors = range(n)
        self.local = [pltpu.make_async_copy(srcs[t], dsts[t].at[_index(me)], recv_sems.at[7 * n + t]) for t in tensors]
        self.first = [copy(1 + j, t, me, chip + (c,), srcs[t]) for t in tensors for j, chip in enumerate(chips)]
        self.first += [copy(0, t, me, sibling, srcs[t]) for t in tensors]
        self.ici_in = [copy(1 + j, t, chip + (c,), me) for t in tensors for j, chip in enumerate(chips)]
        self.passed = [copy(4 + j, t, chip + (c,), sibling) for t in tensors for j, chip in enumerate(chips)]
        self.d2d_in = [copy(0, t, sibling, me) for t in tensors]
        self.d2d_in += [copy(4 + j, t, chip + (1 - c,), me) for t in tensors for j, chip in enumerate(chips)]

    def start(self):
        for cp in self.local + self.first:
            cp.start()

    def middle(self):
        for landed, onward in zip(self.ici_in, self.passed):
            landed.wait_recv()
            onward.start()

    def finish(self):
        for cp in self.d2d_in:
            cp.wait_recv()
        for cp in self.first + self.passed:
            cp.wait_send()
        for cp in self.local:
            cp.wait()


class _Scatter:
    def __init__(self, srcs, dsts, send_sems, recv_sems):
        pos = _mesh_position()
        n = len(srcs)
        self.local = [pltpu.make_async_copy(srcs[t].at[_index(pos)], dsts[t].at[0], recv_sems.at[t]) for t in range(n)]
        self.remote = []
        for d in range(1, N_DEV):
            to = _peer(pos, d)
            self.remote += [_remote(srcs[t].at[_index(to)], dsts[t].at[d], send_sems, recv_sems, d * n + t, to)
                            for t in range(n)]

    def start(self):
        for cp in self.local + self.remote:
            cp.start()

    def middle(self):
        pass

    def finish(self):
        for cp in self.local:
            cp.wait()
        for cp in self.remote:
            cp.wait_recv()
        for cp in self.remote:
            cp.wait_send()


HBM_SPEC = pl.BlockSpec(memory_space=pltpu.HBM)
MIDDLE_STEPS_BEFORE_END = 4


def _landing_shape(exchange, src):
    return (N_DEV,) + src.shape if exchange is _Gather else src.shape


def _call(body, *, name, grid, in_specs, out_specs, out_shape, scratch_shapes, args, sides=()):
    params = _params(len(grid))
    if not sides:
        return pl.pallas_call(body, name=name, grid=grid, in_specs=in_specs, out_specs=out_specs, out_shape=out_shape,
                              scratch_shapes=scratch_shapes, compiler_params=params)(*args)
    n_in, n_out, n_scratch = len(in_specs), len(out_specs), len(scratch_shapes)
    counts = [len(srcs) for _, srcs in sides]
    ns = sum(counts)
    total = 1
    for g in grid:
        total *= g
    side_args, side_shapes, side_sems = [], [], []
    for exchange, srcs in sides:
        side_args += list(srcs)
        side_shapes += [jax.ShapeDtypeStruct(_landing_shape(exchange, s), s.dtype) for s in srcs]
        side_sems += [pltpu.SemaphoreType.DMA((N_DEV * len(srcs),))] * 2

    def wrapped(*refs):
        ins, side_in = refs[:n_in], refs[n_in:n_in + ns]
        outs = refs[n_in + ns:n_in + ns + n_out]
        side_out = refs[n_in + ns + n_out:n_in + 2 * ns + n_out]
        rest = refs[n_in + 2 * ns + n_out:]
        scratch, sems = rest[:n_scratch], rest[n_scratch:]
        step = pl.program_id(0)
        for axis in range(1, len(grid)):
            step = step * grid[axis] + pl.program_id(axis)

        def exchanges():
            built, at = [], 0
            for k, (exchange, _) in enumerate(sides):
                built.append(exchange(side_in[at:at + counts[k]], side_out[at:at + counts[k]],
                                      sems[2 * k], sems[2 * k + 1]))
                at += counts[k]
            return built

        @pl.when(step == 0)
        def _():
            for ex in exchanges():
                ex.start()

        @pl.when(step == max(total - MIDDLE_STEPS_BEFORE_END, 0))
        def _():
            for ex in exchanges():
                ex.middle()

        body(*ins, *outs, *scratch)

        @pl.when(step == total - 1)
        def _():
            for ex in exchanges():
                ex.finish()

    return pl.pallas_call(
        wrapped, name=name, grid=grid, in_specs=list(in_specs) + [HBM_SPEC] * ns,
        out_specs=list(out_specs) + [HBM_SPEC] * ns, out_shape=list(out_shape) + side_shapes,
        scratch_shapes=list(scratch_shapes) + side_sems, compiler_params=params)(*args, *side_args)


def _prenorm_gather(x, g, wie, small, woe, wio, woo, ts):
    S = x.shape[0]
    nt = S // ts

    def body(x_ref, g_ref, wie_ref, sm_ref, woe_ref, wio_ref, woo_ref,
             h_ref, wie_out, sm_out, woe_out, wio_b, woo_b, cast_ref, small_ref, cast2_ref, send_sems, recv_sems):
        step = pl.program_id(0)
        gather = lambda: _Gather([cast_ref, small_ref, cast2_ref], [wie_out, sm_out, woe_out], send_sems, recv_sems)

        @pl.when(step == 0)
        def _():
            cast_ref[...] = wie_ref[...].astype(BF16)
            small_ref[...] = sm_ref[...]
            cast2_ref[...] = woe_ref[...].astype(BF16)
            gather().start()
            wio_b[...] = wio_ref[...].astype(BF16)
            woo_b[...] = woo_ref[...].astype(BF16)

        xv = x_ref[...]
        h_ref[...] = ((xv * _rms(xv)) * g_ref[...]).astype(BF16)

        @pl.when(step == nt - 1)
        def _():
            exchange = gather()
            exchange.middle()
            exchange.finish()

    tile = pl.BlockSpec((ts, D), lambda i: (i, 0))
    sems = pltpu.SemaphoreType.DMA((N_DEV * 3,))
    shards = [wie, small, woe, wio, woo]
    return pl.pallas_call(
        body, name="prenorm_gather", grid=(nt,),
        in_specs=[tile, _full((1, D))] + [_full(a.shape) for a in shards],
        out_specs=[tile, HBM_SPEC, HBM_SPEC, HBM_SPEC] + [_full(a.shape) for a in shards[3:]],
        out_shape=[jax.ShapeDtypeStruct((S, D), BF16), jax.ShapeDtypeStruct((N_DEV,) + wie.shape, BF16),
                   jax.ShapeDtypeStruct((N_DEV,) + small.shape, F32), jax.ShapeDtypeStruct((N_DEV,) + woe.shape, BF16)]
        + [jax.ShapeDtypeStruct(a.shape, BF16) for a in shards[3:]],
        scratch_shapes=[pltpu.VMEM(wie.shape, BF16), pltpu.VMEM(small.shape, F32), pltpu.VMEM(woe.shape, BF16),
                        sems, sems],
        compiler_params=_params(),
    )(x, g, wie, small, woe, wio, woo)


def _even_mixer(proj, halo, row0, cw_ref, pw_ref):
    xa, gb, gc, za, xp, zp = proj
    nr = xa.shape[0]
    ha = gc * xa
    ha_ext = jnp.concatenate([halo[0], ha], axis=0)
    ha_m1 = _shift_down(ha_ext, 1)[HALO:]
    ha_m2 = _shift_down(ha_ext, 2)[HALO:]
    conv = cw_ref[2:3, :] * ha + cw_ref[1:2, :] * ha_m1 + cw_ref[0:1, :] * ha_m2
    sig_a = _sigmoid(za)
    silu_a = za * sig_a

    xp_ext = jnp.concatenate([halo[1], xp], axis=0)
    pos = row0 + lax.broadcasted_iota(jnp.int32, (nr, 1), 0)
    pooled, inv_cnt, mixed = [], [], []
    for g, w in enumerate(POOL_WINDOWS):
        cols = slice(g * GROUP, (g + 1) * GROUP)
        s = xp_ext[:, cols]
        k = 1
        while k < w:
            s = s + _shift_down(s, k)
            k *= 2
        inv = 1.0 / jnp.minimum(pos + 1, w).astype(F32)
        pg = s[HALO:] * inv - xp[:, cols]
        pooled.append(pg)
        inv_cnt.append(inv)
        mixed.append(_dot(pg.astype(BF16), pw_ref[g]))
    mixed = jnp.concatenate(mixed, axis=1)
    sig_b = _sigmoid(zp)
    silu_b = zp * sig_b
    return dict(xa=xa, gb=gb, gc=gc, za=za, xp=xp, zp=zp, ha=ha, ha_m1=ha_m1, ha_m2=ha_m2, conv=conv,
                sig_a=sig_a, silu_a=silu_a, pooled=pooled, inv_cnt=inv_cnt, mixed=mixed, sig_b=sig_b,
                silu_b=silu_b)


def _columns(a):
    return [a[:, k * A_W:(k + 1) * A_W].astype(F32) for k in range(6)]


def _even_forward_fused(h, x, win_b, cw, pw_b, ps, wout_b, gpost, gpre_next, ts, sides=()):
    S = x.shape[0]

    def body(h_ref, x_ref, wi_blocks, cw_ref, pw_ref, ps_ref, wo_ref, gp_ref, gn_ref,
             p_ref, m_ref, x1_ref, h1_ref, mix_ref, carry_ref, wi_ref, wi_sems):
        i = pl.program_id(0)

        @pl.when(i == 0)
        def _():
            carry_ref[...] = jnp.zeros_like(carry_ref)
            _load_columns(wi_blocks, wi_ref, wi_sems)

        for r0 in range(0, ts, SUB_ROWS):
            rows = slice(r0, r0 + SUB_ROWS)
            proj = _dot(h_ref[rows, :], wi_ref[...])
            p_ref[rows, :] = proj.astype(BF16)
            fw = _even_mixer(_columns(proj), (carry_ref[:, 0:A_W], carry_ref[:, A_W:D]), i * ts + r0, cw_ref, pw_ref)
            carry_ref[:, 0:A_W] = fw["ha"][SUB_ROWS - HALO:]
            carry_ref[:, A_W:D] = fw["xp"][SUB_ROWS - HALO:]
            mix_ref[rows, 0:A_W] = (fw["gb"] * fw["conv"] * fw["silu_a"]).astype(BF16)
            mix_ref[rows, A_W:D] = (fw["mixed"] * ps_ref[...] * fw["silu_b"]).astype(BF16)
            m = _dot(mix_ref[rows, :], wo_ref[...])
            m_ref[rows, :] = m.astype(BF16)
            x1 = x_ref[rows, :] + (m * _rms(m)) * gp_ref[...]
            x1_ref[rows, :] = x1
            h1_ref[rows, :] = ((x1 * _rms(x1)) * gn_ref[...]).astype(BF16)

    tile = pl.BlockSpec((ts, D), lambda i: (i, 0))
    return _call(
        body, name="even_forward", grid=(S // ts,),
        in_specs=[tile, tile, HBM_SPEC, _full((3, A_W)), _full((N_GROUPS, GROUP, GROUP)), _full((1, B_W)),
                  _resident((D, D)), _full((1, D)), _full((1, D))],
        out_specs=[pl.BlockSpec((ts, PROJ), lambda i: (i, 0)), tile, tile, tile],
        out_shape=[jax.ShapeDtypeStruct((S, PROJ), BF16), jax.ShapeDtypeStruct((S, D), BF16),
                   jax.ShapeDtypeStruct((S, D), F32), jax.ShapeDtypeStruct((S, D), BF16)],
        scratch_shapes=[pltpu.VMEM((ts, D), BF16), pltpu.VMEM((HALO, D), F32), pltpu.VMEM((D, PROJ), BF16),
                        pltpu.SemaphoreType.DMA((N_DEV,))],
        args=(h, x, win_b, cw, pw_b, ps, wout_b, gpost, gpre_next), sides=sides)


def _even_backward_fused(g1, m0, p, x, cw, pw_b, pwt_b, ps, woutt_b, wint_b, gpost, gpre, ts):
    S = g1.shape[0]
    nt = S // ts

    def body(g_ref, m_ref, p_ref, halo_ref, x_ref, cw_ref, pw_ref, pwt_ref, ps_ref, wot_ref, wit_ref, gp_ref, gn_ref,
             dp_ref, dx_ref, dwob_ref, dcw_ref, dpw_ref, dps_ref, dgp_ref, dgn_ref,
             carry_ref, mix_ref, dm_ref, dwo_ref):
        step = pl.program_id(0)
        i = nt - 1 - step

        @pl.when(step == 0)
        def _():
            carry_ref[...] = jnp.zeros_like(carry_ref)
            dwo_ref[...] = jnp.zeros_like(dwo_ref)
            dcw_ref[...] = jnp.zeros_like(dcw_ref)
            dpw_ref[...] = jnp.zeros_like(dpw_ref)
            dps_ref[...] = jnp.zeros_like(dps_ref)
            dgp_ref[...] = jnp.zeros_like(dgp_ref)
            dgn_ref[...] = jnp.zeros_like(dgn_ref)

        ps_v = ps_ref[...]
        sums = [None] * 6
        add = lambda k, part: sums.__setitem__(k, part if sums[k] is None else sums[k] + part)
        for r0 in range(ts - SUB_ROWS, -1, -SUB_ROWS):
            rows = slice(r0, r0 + SUB_ROWS)
            g = g_ref[rows, :]
            m = m_ref[rows, :].astype(F32)
            q = _rms(m)
            n = m * q
            add(0, jnp.sum(g * n, axis=0, keepdims=True))
            dn = g * gp_ref[...]
            dm = q * (dn - n * jnp.mean(dn * n, axis=-1, keepdims=True))
            dm_ref[rows, :] = dm.astype(BF16)
            dmix = _dot(dm_ref[rows, :], wot_ref[...])
            dya = dmix[:, 0:A_W]
            dyb = dmix[:, A_W:D]

            if r0 == 0:
                before = _columns(halo_ref[...])
                keep = jnp.where(i == 0, 0.0, 1.0).astype(F32)
                halo = (before[2] * before[0] * keep, before[4] * keep)
            else:
                before = _columns(p_ref[r0 - HALO:r0, :])
                halo = (before[2] * before[0], before[4])
            fw = _even_mixer(_columns(p_ref[rows, :]), halo, i * ts + r0, cw_ref, pw_ref)
            mix_ref[rows, 0:A_W] = (fw["gb"] * fw["conv"] * fw["silu_a"]).astype(BF16)
            mix_ref[rows, A_W:D] = (fw["mixed"] * ps_v * fw["silu_b"]).astype(BF16)

            t = dya * fw["gb"]
            dconv = t * fw["silu_a"]
            dgb = dya * fw["conv"] * fw["silu_a"]
            dza = t * fw["conv"] * _dsilu(fw["za"], fw["sig_a"])
            add(2, jnp.sum(dconv * fw["ha"], axis=0, keepdims=True))
            add(3, jnp.sum(dconv * fw["ha_m1"], axis=0, keepdims=True))
            add(4, jnp.sum(dconv * fw["ha_m2"], axis=0, keepdims=True))
            dconv_ext = jnp.concatenate([dconv, carry_ref[:, 0:A_W]], axis=0)
            dha = (cw_ref[2:3, :] * dconv + cw_ref[1:2, :] * _shift_up(dconv_ext, 1)[:SUB_ROWS]
                   + cw_ref[0:1, :] * _shift_up(dconv_ext, 2)[:SUB_ROWS])
            dgc = dha * fw["xa"]
            dxa = dha * fw["gc"]

            u = dyb * fw["mixed"]
            add(1, jnp.sum(u * fw["silu_b"], axis=0, keepdims=True))
            dzp = u * ps_v * _dsilu(fw["zp"], fw["sig_b"])
            dmixed = (dyb * ps_v * fw["silu_b"]).astype(BF16)
            dxp, e_first = [], []
            for gi, w in enumerate(POOL_WINDOWS):
                cols = slice(gi * GROUP, (gi + 1) * GROUP)
                dmg = dmixed[:, cols]
                dpooled = _dot(dmg, pwt_ref[gi])
                dpw_ref[gi] += _dot_t0(fw["pooled"][gi].astype(BF16), dmg)
                e = dpooled * fw["inv_cnt"][gi]
                e_first.append(e[0:HALO])
                s = jnp.concatenate([e, carry_ref[:, A_W + gi * GROUP:A_W + (gi + 1) * GROUP]], axis=0)
                k = 1
                while k < w:
                    s = s + _shift_up(s, k)
                    k *= 2
                dxp.append(s[:SUB_ROWS] - dpooled)
            carry_ref[:, 0:A_W] = dconv[0:HALO]
            carry_ref[:, A_W:D] = jnp.concatenate(e_first, axis=1)

            dp_ref[rows, 0:512] = dxa.astype(BF16)
            dp_ref[rows, 512:1024] = dgb.astype(BF16)
            dp_ref[rows, 1024:1536] = dgc.astype(BF16)
            dp_ref[rows, 1536:2048] = dza.astype(BF16)
            dp_ref[rows, 2048:2560] = jnp.concatenate(dxp, axis=1).astype(BF16)
            dp_ref[rows, 2560:3072] = dzp.astype(BF16)

            dh = _dot(dp_ref[rows, :], wit_ref[...])
            xv = x_ref[rows, :]
            r = _rms(xv)
            xn = xv * r
            add(5, jnp.sum(dh * xn, axis=0, keepdims=True))
            dxn = dh * gn_ref[...]
            dx_ref[rows, :] = g + r * (dxn - xn * jnp.mean(dxn * xn, axis=-1, keepdims=True))

        dwo_ref[...] += _dot_t0(mix_ref[...], dm_ref[...])
        dgp_ref[...] += sums[0]
        dps_ref[...] += sums[1]
        dcw_ref[2:3, :] += sums[2]
        dcw_ref[1:2, :] += sums[3]
        dcw_ref[0:1, :] += sums[4]
        dgn_ref[...] += sums[5]

        @pl.when(step == nt - 1)
        def _():
            dwob_ref[...] = dwo_ref[...].astype(BF16)

    rev = lambda s: (nt - 1 - s, 0)
    tile = pl.BlockSpec((ts, D), rev)
    wide = pl.BlockSpec((ts, PROJ), rev)
    bpt = ts // HALO
    halo_map = lambda s: (jnp.maximum((nt - 1 - s) * bpt - 1, 0), 0)
    groups = _full((N_GROUPS, GROUP, GROUP))
    vec = _full((1, D))
    return pl.pallas_call(
        body, name="even_backward", grid=(nt,),
        in_specs=[tile, tile, wide, pl.BlockSpec((HALO, PROJ), halo_map), tile, _full((3, A_W)), groups, groups,
                  _full((1, B_W)), _resident((D, D)), _resident((PROJ, D)), vec, vec],
        out_specs=[wide, tile, _full((D, D)), _full((8, A_W)), groups, _full((1, B_W)), vec, vec],
        out_shape=[jax.ShapeDtypeStruct((S, PROJ), BF16), jax.ShapeDtypeStruct((S, D), F32),
                   jax.ShapeDtypeStruct((D, D), BF16), jax.ShapeDtypeStruct((8, A_W), F32),
                   jax.ShapeDtypeStruct((N_GROUPS, GROUP, GROUP), F32), jax.ShapeDtypeStruct((1, B_W), F32),
                   jax.ShapeDtypeStruct((1, D), F32), jax.ShapeDtypeStruct((1, D), F32)],
        scratch_shapes=[pltpu.VMEM((HALO, D), F32), pltpu.VMEM((ts, D), BF16), pltpu.VMEM((ts, D), BF16),
                        pltpu.VMEM((D, D), F32)],
        compiler_params=_params(),
    )(g1, m0, p, p, x, cw, pw_b, pwt_b, ps, woutt_b, wint_b, gpost, gpre)


def _gate_matmuls(w_ref, in_ref, out_ref, chunks, bias_ref=None):
    for h in range(N_HEADS):
        cols = slice(h * GROUP, (h + 1) * GROUP)
        wide = jnp.concatenate([in_ref[chunk, cols] for chunk in chunks], axis=1)
        res = _dot(w_ref[h], wide)
        for k, chunk in enumerate(chunks):
            part = res[:, k * GROUP:(k + 1) * GROUP]
            out_ref[chunk, cols] = part if bias_ref is None else part + bias_ref[h]


def _odd_forward_parts(v, lg_ref, lb_ref, wt_ref, bfull_ref, vln_ref, sv_ref, r0, nr):
    rows = slice(r0, r0 + nr)
    mu = jnp.mean(v, axis=-1, keepdims=True)
    vc = v - mu
    rstd = lax.rsqrt(jnp.mean(vc * vc, axis=-1, keepdims=True) + EPS)
    vh = vc * rstd
    vln_ref[rows, :] = (vh * lg_ref[...] + lb_ref[...]).astype(BF16)
    _gate_matmuls(wt_ref, vln_ref, sv_ref, [slice(r0 + k * GROUP, r0 + (k + 1) * GROUP) for k in range(nr // GROUP)],
                  bfull_ref)
    return vh, rstd


def _odd_forward(h, x1, tgt, win_b, lg, lb, wt_b, bfull, wout_b, gpost, ts):
    S = x1.shape[0]

    def body(h_ref, x_ref, t_ref, wi_blocks, lg_ref, lb_ref, wt_ref, bfull_ref, wo_ref, gp_ref,
             p_ref, m_ref, g_ref, loss_ref, vln_ref, sv_ref, y_ref, wi_ref, wi_sems):
        @pl.when(pl.program_id(0) == 0)
        def _():
            loss_ref[...] = jnp.zeros_like(loss_ref)
            _load_columns(wi_blocks, wi_ref, wi_sems)

        loss = None
        for r0 in range(0, ts, SUB_ROWS):
            rows = slice(r0, r0 + SUB_ROWS)
            proj = _dot(h_ref[rows, :], wi_ref[...])
            p_ref[rows, :] = proj.astype(BF16)
            u, v, z = proj[:, 0:D], proj[:, D:2 * D], proj[:, 2 * D:3 * D]
            _odd_forward_parts(v, lg_ref, lb_ref, wt_ref, bfull_ref, vln_ref, sv_ref, r0, SUB_ROWS)
            y_ref[rows, :] = (u * sv_ref[rows, :] * (z * _sigmoid(z))).astype(BF16)
            m = _dot(y_ref[rows, :], wo_ref[...])
            m_ref[rows, :] = m.astype(BF16)
            x2 = x_ref[rows, :] + (m * _rms(m)) * gp_ref[...]
            err = x2 - t_ref[rows, :]
            g_ref[rows, :] = (err * (1.0 / D)).astype(BF16)
            part = jnp.sum(err * err, axis=0, keepdims=True)
            loss = part if loss is None else loss + part
        loss_ref[...] += loss

    tile = pl.BlockSpec((ts, D), lambda i: (i, 0))
    small = _full((N_HEADS, GROUP, GROUP))
    return pl.pallas_call(
        body, name="odd_forward", grid=(S // ts,),
        in_specs=[pl.BlockSpec((ts, D), lambda i: (i, 0)), tile, tile, HBM_SPEC, _full((1, D)), _full((1, D)),
                  small, small, _full((D, D)), _full((1, D))],
        out_specs=[pl.BlockSpec((ts, PROJ), lambda i: (i, 0)), tile, tile, _full((1, D))],
        out_shape=[jax.ShapeDtypeStruct((S, PROJ), BF16), jax.ShapeDtypeStruct((S, D), BF16),
                   jax.ShapeDtypeStruct((S, D), BF16), jax.ShapeDtypeStruct((1, D), F32)],
        scratch_shapes=[pltpu.VMEM((ts, D), BF16), pltpu.VMEM((ts, D), F32), pltpu.VMEM((ts, D), BF16),
                        pltpu.VMEM((D, PROJ), BF16), pltpu.SemaphoreType.DMA((N_DEV,))],
        compiler_params=_params(),
    )(h, x1, tgt, win_b, lg, lb, wt_b, bfull, wout_b, gpost)


def _odd_backward(g2, m1, p, x1, lg, lb, wt_b, wtt_b, bfull, woutt_b, wint_b, gpost, gpre, ts):
    S = g2.shape[0]
    nt = S // ts
    sub = ts // 2

    def body(g_ref, m_ref, p_ref, x_ref, lg_ref, lb_ref, wt_ref, wtt_ref, bfull_ref, wot_ref, wit_ref, gp_ref, gn_ref,
             dp_ref, dx_ref, dwob_ref, dws_ref, dbs_ref, dlg_ref, dlb_ref, dgp_ref, dgn_ref,
             vln_ref, sv_ref, dsvb_ref, dvln_ref, dsum_ref, y_ref, dm_ref, dwo_ref):
        step = pl.program_id(0)

        @pl.when(step == 0)
        def _():
            dwo_ref[...] = jnp.zeros_like(dwo_ref)
            dws_ref[...] = jnp.zeros_like(dws_ref)
            dsum_ref[...] = jnp.zeros_like(dsum_ref)
            dlg_ref[...] = jnp.zeros_like(dlg_ref)
            dlb_ref[...] = jnp.zeros_like(dlb_ref)
            dgp_ref[...] = jnp.zeros_like(dgp_ref)
            dgn_ref[...] = jnp.zeros_like(dgn_ref)

        tril = (lax.broadcasted_iota(jnp.int32, (GROUP, GROUP), 0)
                >= lax.broadcasted_iota(jnp.int32, (GROUP, GROUP), 1))
        sums = [None] * 4
        add = lambda k, part: sums.__setitem__(k, part if sums[k] is None else sums[k] + part)

        def post_norm(c):
            g = g_ref[c["rows"], :].astype(F32)
            m = m_ref[c["rows"], :].astype(F32)
            q = _rms(m)
            n = m * q
            add(0, jnp.sum(g * n, axis=0, keepdims=True))
            dn = g * gp_ref[...]
            dm = q * (dn - n * jnp.mean(dn * n, axis=-1, keepdims=True))
            dm_ref[c["rows"], :] = dm.astype(BF16)

        def out_projection(c):
            c["dy"] = _dot(dm_ref[c["rows"], :], wot_ref[...])

        def layernorm(c):
            v = p_ref[c["rows"], D:2 * D].astype(F32)
            mu = jnp.mean(v, axis=-1, keepdims=True)
            vc = v - mu
            c["rstd"] = lax.rsqrt(jnp.mean(vc * vc, axis=-1, keepdims=True) + EPS)
            c["vh"] = vc * c["rstd"]
            vln_ref[c["rows"], :] = (c["vh"] * lg_ref[...] + lb_ref[...]).astype(BF16)

        def gate_matmuls(c):
            _gate_matmuls(wt_ref, vln_ref, sv_ref, c["chunks"], bfull_ref)

        def gating(c):
            rows = c["rows"]
            u = p_ref[rows, 0:D].astype(F32)
            z = p_ref[rows, 2 * D:3 * D].astype(F32)
            sv = sv_ref[rows, :]
            sig = _sigmoid(z)
            sz = z * sig
            y_ref[rows, :] = (u * sv * sz).astype(BF16)
            dy = c.pop("dy")
            t = dy * sz
            dsv = t * u
            dsvb_ref[rows, :] = dsv.astype(BF16)
            for k in range(sub // GROUP):
                dsum_ref[...] += dsv[k * GROUP:(k + 1) * GROUP]
            dp_ref[rows, 0:D] = (t * sv).astype(BF16)
            dp_ref[rows, 2 * D:3 * D] = (dy * u * sv * _dsilu(z, sig)).astype(BF16)

        def gate_backward_matmuls(c):
            for chunk in c["chunks"]:
                for h in range(N_HEADS):
                    cols = slice(h * GROUP, (h + 1) * GROUP)
                    dvln_ref[chunk, cols] = _dot(wtt_ref[h], dsvb_ref[chunk, cols])
                    dws_ref[h] += jnp.where(tril, _dot_t1(dsvb_ref[chunk, cols], vln_ref[chunk, cols]), 0.0)

        def layernorm_backward(c):
            vh, rstd = c.pop("vh"), c.pop("rstd")
            dvln = dvln_ref[c["rows"], :]
            add(1, jnp.sum(dvln * vh, axis=0, keepdims=True))
            add(2, jnp.sum(dvln, axis=0, keepdims=True))
            dvh = dvln * lg_ref[...]
            dv = rstd * (dvh - jnp.mean(dvh, axis=-1, keepdims=True)
                         - vh * jnp.mean(dvh * vh, axis=-1, keepdims=True))
            dp_ref[c["rows"], D:2 * D] = dv.astype(BF16)

        def in_projection(c):
            c["dh"] = _dot(dp_ref[c["rows"], :], wit_ref[...])

        def pre_norm(c):
            dh = c.pop("dh")
            xv = x_ref[c["rows"], :]
            r = _rms(xv)
            xn = xv * r
            add(3, jnp.sum(dh * xn, axis=0, keepdims=True))
            dxn = dh * gn_ref[...]
            dx_ref[c["rows"], :] = (g_ref[c["rows"], :].astype(F32)
                                    + r * (dxn - xn * jnp.mean(dxn * xn, axis=-1, keepdims=True)))

        phases = [post_norm, out_projection, layernorm, gate_matmuls, gating, gate_backward_matmuls,
                  layernorm_backward, in_projection, pre_norm]
        groups = [dict(rows=slice(r0, r0 + sub),
                       chunks=[slice(r0 + k * GROUP, r0 + (k + 1) * GROUP) for k in range(sub // GROUP)])
                  for r0 in range(0, ts, sub)]
        for group in groups:
            for phase in phases:
                phase(group)

        dwo_ref[...] += _dot_t0(y_ref[...], dm_ref[...])
        dgp_ref[...] += sums[0]
        dlg_ref[...] += sums[1]
        dlb_ref[...] += sums[2]
        dgn_ref[...] += sums[3]

        @pl.when(step == nt - 1)
        def _():
            dwob_ref[...] = dwo_ref[...].astype(BF16)
            ones = jnp.ones((8, GROUP), F32)
            for h in range(N_HEADS):
                cols = slice(h * GROUP, (h + 1) * GROUP)
                total = lax.dot_general(ones, dsum_ref[:, cols], (((1,), (1,)), ((), ())),
                                        precision=lax.Precision.HIGHEST, preferred_element_type=F32)
                dbs_ref[h:h + 1, :] = total[0:1, :]

    tile = pl.BlockSpec((ts, D), lambda i: (i, 0))
    wide = pl.BlockSpec((ts, PROJ), lambda i: (i, 0))
    small = _full((N_HEADS, GROUP, GROUP))
    heads = _resident((N_HEADS, GROUP, GROUP))
    vec = _full((1, D))
    return pl.pallas_call(
        body, name="odd_backward", grid=(nt,),
        in_specs=[tile, tile, wide, tile, vec, vec, heads, heads, heads, _resident((D, D)), _resident((PROJ, D)),
                  vec, vec],
        out_specs=[wide, tile, _full((D, D)), small, _full((N_HEADS, GROUP)), vec, vec, vec, vec],
        out_shape=[jax.ShapeDtypeStruct((S, PROJ), BF16), jax.ShapeDtypeStruct((S, D), F32),
                   jax.ShapeDtypeStruct((D, D), BF16),
                   jax.ShapeDtypeStruct((N_HEADS, GROUP, GROUP), F32), jax.ShapeDtypeStruct((N_HEADS, GROUP), F32),
                   jax.ShapeDtypeStruct((1, D), F32), jax.ShapeDtypeStruct((1, D), F32),
                   jax.ShapeDtypeStruct((1, D), F32), jax.ShapeDtypeStruct((1, D), F32)],
        scratch_shapes=[pltpu.VMEM((ts, D), BF16), pltpu.VMEM((ts, D), F32), pltpu.VMEM((ts, D), BF16),
                        pltpu.VMEM((ts, D), F32), pltpu.VMEM((GROUP, D), F32), pltpu.VMEM((ts, D), BF16),
                        pltpu.VMEM((ts, D), BF16), pltpu.VMEM((D, D), F32)],
        compiler_params=_params(),
    )(g2, m1, p, x1, lg, lb, wt_b, wtt_b, bfull, woutt_b, wint_b, gpost, gpre)


def _weight_grad_scatter(h, dp, ts, name, sides=()):
    S = h.shape[0]
    ts = min(ts, S)
    nt = S // ts
    panel = 2 * SHARD_IN
    last = N_CHIPS - 1

    def body(h_ref, dp_ref, parts_ref, acc_ref, half_ref, swap_ref, sum_ref, send_sems, recv_sems):
        s, t = pl.program_id(0), pl.program_id(1)
        x, y, c = _mesh_position()
        sibling = (x, y, 1 - c)

        def swap(k):
            return _remote(half_ref.at[k, 1], swap_ref.at[k], send_sems, recv_sems, k, sibling)

        def chip_sum(k):
            flip = last - k
            to = (x ^ (flip >> 1), y ^ (flip & 1), c)
            return _remote(sum_ref.at[k], parts_ref.at[1 + flip], send_sems, recv_sems, 3 + flip, to)

        own = pltpu.make_async_copy(half_ref.at[last, 0], parts_ref.at[0], recv_sems.at[7])
        to_sibling = _remote(half_ref.at[last, 1], parts_ref.at[1], send_sems, recv_sems, 3, sibling)

        @pl.when(t == 0)
        def _():
            acc_ref[...] = jnp.zeros_like(acc_ref)

        for k in range(last):
            @pl.when((s == k + 1) & (t == 0))
            def _():
                swap(k).wait_recv()
                sum_ref[k] = (half_ref[k, 0].astype(F32) + swap_ref[k].astype(F32)).astype(BF16)
                chip_sum(k).start()

        acc_ref[...] += _dot_t0(h_ref[...], dp_ref[...])

        @pl.when(t == nt - 1)
        def _():
            for core in range(2):
                @pl.when(c == core)
                def _():
                    half_ref[s, core] = acc_ref[:, 0:SHARD_IN].astype(BF16)
                    half_ref[s, 1 - core] = acc_ref[:, SHARD_IN:panel].astype(BF16)
            for k in range(last):
                @pl.when(s == k)
                def _():
                    swap(k).start()

        @pl.when((s == last) & (t == nt - 1))
        def _():
            own.start()
            to_sibling.start()
            own.wait()
            _remote(half_ref.at[last, 1], parts_ref.at[1], send_sems, recv_sems, 3, sibling).wait_recv()
            for k in range(last):
                chip_sum(k).wait_recv()
            to_sibling.wait_send()
            for k in range(last):
                swap(k).wait_send()
                chip_sum(k).wait_send()

    chip_panel = lambda s, t: (t, (2 * lax.axis_index("x") + lax.axis_index("y")) ^ (last - s))
    sems = pltpu.SemaphoreType.DMA((8,))
    return _call(
        body, name=name, grid=(N_CHIPS, nt),
        in_specs=[pl.BlockSpec((ts, D), lambda s, t: (t, 0)), pl.BlockSpec((ts, panel), chip_panel)],
        out_specs=[HBM_SPEC], out_shape=[jax.ShapeDtypeStruct((N_CHIPS + 1, D, SHARD_IN), BF16)],
        scratch_shapes=[pltpu.VMEM((D, panel), F32), pltpu.VMEM((N_CHIPS, 2, D, SHARD_IN), BF16),
                        pltpu.VMEM((last, D, SHARD_IN), BF16), pltpu.VMEM((last, D, SHARD_IN), BF16), sems, sems],
        args=(h, dp), sides=sides)


def _adamw(w, g, m, v):
    m = ADAM_B1 * m + (1.0 - ADAM_B1) * g
    v = ADAM_B2 * v + (1.0 - ADAM_B2) * (g * g)
    m_hat = m / (1.0 - ADAM_B1 ** ADAM_STEP)
    v_hat = v / (1.0 - ADAM_B2 ** ADAM_STEP)
    delta = -ADAM_LR * (m_hat / (jnp.sqrt(v_hat) + ADAM_EPS) + ADAM_WD * w)
    return delta, m, v


UPDATE_STEPS = 4


def _sum_update(parts, weights):
    n_w = len(weights)

    def body(*refs):
        p_refs, wmv_refs, outs = refs[:n_w], refs[n_w:4 * n_w], refs[4 * n_w:]
        for i in range(n_w):
            acc = p_refs[i][0].astype(F32)
            for d in range(1, parts[i].shape[0]):
                acc = acc + p_refs[i][d].astype(F32)
            w_ref, m_ref, v_ref = wmv_refs[3 * i:3 * i + 3]
            g_out, d_out, m_out, v_out = outs[4 * i:4 * i + 4]
            g_out[...] = acc
            delta, m_new, v_new = _adamw(w_ref[...], acc, m_ref[...], v_ref[...])
            d_out[...] = delta
            m_out[...] = m_new
            v_out[...] = v_new

    part_specs, wmv_specs, out_specs, out_shape = [], [], [], []
    for p, (w, _, _) in zip(parts, weights):
        R, C = w.shape
        rb = R // UPDATE_STEPS
        blk = pl.BlockSpec((rb, C), lambda i: (i, 0))
        part_specs.append(pl.BlockSpec((p.shape[0], rb, C), lambda i: (0, i, 0)))
        wmv_specs += [blk] * 3
        out_specs += [blk] * 4
        out_shape += [jax.ShapeDtypeStruct((R, C), F32)] * 4
    flat = pl.pallas_call(
        body, name="update_large_weights", grid=(UPDATE_STEPS,), in_specs=part_specs + wmv_specs,
        out_specs=out_specs, out_shape=out_shape, compiler_params=_params(),
    )(*parts, *[a for wmv in weights for a in wmv])
    return [flat[4 * i:4 * i + 4] for i in range(n_w)]


def _small_sum_update(partials, row_of, weights, loss_parts):
    n_p, n_w = len(partials), len(weights)

    def body(*refs):
        p_refs = refs[:n_p]
        loss_ref = refs[n_p]
        w_refs = refs[n_p + 1:n_p + 1 + 3 * n_w]
        loss_out = refs[n_p + 1 + 3 * n_w]
        outs = refs[n_p + 2 + 3 * n_w:]

        def total(ref):
            acc = ref[0].astype(F32)
            for d in range(1, N_DEV):
                acc = acc + ref[d].astype(F32)
            return acc

        lsum = jnp.sum(total(loss_ref), axis=-1, keepdims=True) * (0.5 / D)
        loss_out[...] = jnp.broadcast_to(lsum, loss_out.shape)
        for k in range(n_p):
            i, r0 = row_of[k]
            g = total(p_refs[k])
            w_ref, m_ref, v_ref = w_refs[3 * i:3 * i + 3]
            g_out, d_out, m_out, v_out = outs[4 * i:4 * i + 4]
            if g.ndim == 2:
                at = (slice(r0, r0 + g.shape[0]), slice(None))
            else:
                at = (slice(None),) * g.ndim
            delta, m_new, v_new = _adamw(w_ref[at], g, m_ref[at], v_ref[at])
            g_out[at] = g
            d_out[at] = delta
            m_out[at] = m_new
            v_out[at] = v_new

    vm = pl.BlockSpec(memory_space=pltpu.VMEM)
    flat_w = [a for wmv in weights for a in wmv]
    out_shape = [jax.ShapeDtypeStruct((1, 128), F32)]
    for w, _, _ in weights:
        out_shape += [jax.ShapeDtypeStruct(w.shape, F32)] * 4
    return pl.pallas_call(
        body, name="small_update", in_specs=[vm] * (n_p + 1 + 3 * n_w), out_specs=[vm] * len(out_shape),
        out_shape=out_shape, compiler_params=pltpu.CompilerParams(vmem_limit_bytes=VMEM_LIMIT),
    )(*partials, loss_parts, *flat_w)


def _pack_small_shard(cw, lg, lb):
    return (jnp.pad(cw, ((0, 13), (0, 64))) + jnp.pad(lg, ((8, 7), (0, 0))) + jnp.pad(lb, ((9, 6), (0, 0))))


def kernel(x, pre_norm, post_norm, even_w_in, even_conv_w, even_pool_w, even_pool_scale, even_w_out, odd_w_in, odd_ln_g, odd_ln_b, odd_w_s, odd_b_s, odd_w_out, loss_target, m_pre_norm, m_post_norm, m_even_w_in, m_even_conv_w, m_even_pool_w, m_even_pool_scale, m_even_w_out, m_odd_w_in, m_odd_ln_g, m_odd_ln_b, m_odd_w_s, m_odd_b_s, m_odd_w_out, v_pre_norm, v_post_norm, v_even_w_in, v_even_conv_w, v_even_pool_w, v_even_pool_scale, v_even_w_out, v_odd_w_in, v_odd_ln_g, v_odd_ln_b, v_odd_w_s, v_odd_b_s, v_odd_w_out):
    S = x.shape[1]
    xs = x.reshape(S, D)
    tgt = loss_target.reshape(S, D)

    gpre0, gpre1 = pre_norm[0:1], pre_norm[1:2]
    gpost0, gpost1 = post_norm[0:1], post_norm[1:2]
    small_shard = _pack_small_shard(even_conv_w[0], odd_ln_g, odd_ln_b)
    h0, wie_g, small_g, woe_g, wio_b, woo_b = _prenorm_gather(xs, gpre0, even_w_in[0], small_shard, even_w_out[0],
                                                              odd_w_in[0], odd_w_out[0], TS_PRENORM)
    wie_t = jnp.transpose(wie_g, (0, 2, 1)).reshape(PROJ, D)
    woe = woe_g.reshape(D, D)
    woe_t = woe.T
    conv_w = jnp.transpose(small_g[:, 0:3, 0:64], (1, 0, 2)).reshape(3, A_W)
    ln_g = small_g[:, 8, :].reshape(1, D)
    ln_b = small_g[:, 9, :].reshape(1, D)

    pool_w_b = even_pool_w[0].astype(BF16)
    pool_wt_b = jnp.swapaxes(even_pool_w[0], 1, 2).astype(BF16)
    ws_tril = jnp.tril(odd_w_s[0])
    ws_b = ws_tril.astype(BF16)
    wst_b = jnp.swapaxes(ws_tril, 1, 2).astype(BF16)
    b_full = jnp.broadcast_to(odd_b_s[0][:, :, None], (N_HEADS, GROUP, GROUP))

    p0, m0, x1, h1, wio_g, woo_g = _even_forward_fused(h0, xs, wie_g, conv_w, pool_w_b, even_pool_scale, woe, gpost0,
                                                       gpre1, TS_FWD, sides=[(_Gather, [wio_b, woo_b])])
    wio_t = jnp.transpose(wio_g, (0, 2, 1)).reshape(PROJ, D)
    woo = woo_g.reshape(D, D)
    woo_t = woo.T
    p1, m1, g2, loss_vec = _odd_forward(h1, x1, tgt, wio_g, ln_g, ln_b, ws_b, b_full, woo, gpost1, TS_FWD)

    dp1, g1, dwoo, dws, dbs, dlg, dlb, dgpost1, dgpre1 = _odd_backward(
        g2, m1, p1, x1, ln_g, ln_b, ws_b, wst_b, b_full, woo_t, wio_t, gpost1, gpre1, TS_BWD)
    by_owner = lambda dwo: dwo.reshape(N_DEV, SHARD_OUT, D)
    (dwio_parts, dwoo_parts, dlg_parts, dlb_parts,
     dws_g, dbs_g, gpost1_g, gpre1_g, loss_g) = _weight_grad_scatter(
        h1, dp1, TS_WGRAD, "odd_weight_grad",
        sides=[(_Scatter, [by_owner(dwoo), dlg.reshape(N_DEV, 1, 128), dlb.reshape(N_DEV, 1, 128)]),
               (_Gather, [dws.astype(BF16), dbs, dgpost1, dgpre1, loss_vec])])
    dp0, gx, dwoe, dcw, dpw, dps, dgpost0, dgpre0 = _even_backward_fused(
        g1, m0, p0, xs, conv_w, pool_w_b, pool_wt_b, even_pool_scale, woe_t, wie_t, gpost0, gpre0, TS_FWD)
    dcw_by_owner = jnp.transpose(dcw[0:3].reshape(3, N_DEV, 64), (1, 0, 2))
    dwie_parts, dwoe_parts, dcw_parts, dpw_g, dps_g, gpost0_g, gpre0_g = _weight_grad_scatter(
        h0, dp0, TS_WGRAD, "even_weight_grad",
        sides=[(_Scatter, [by_owner(dwoe), dcw_by_owner]), (_Gather, [dpw.astype(BF16), dps, dgpost0, dgpre0])])

    ((g_wie, d_wie, nm_wie, nv_wie), (g_wio, d_wio, nm_wio, nv_wio), (g_woe, d_woe, nm_woe, nv_woe),
     (g_woo, d_woo, nm_woo, nv_woo)) = _sum_update(
        [dwie_parts, dwio_parts, dwoe_parts, dwoo_parts],
        [(even_w_in[0], m_even_w_in[0], v_even_w_in[0]), (odd_w_in[0], m_odd_w_in[0], v_odd_w_in[0]),
         (even_w_out[0], m_even_w_out[0], v_even_w_out[0]), (odd_w_out[0], m_odd_w_out[0], v_odd_w_out[0])])

    partials = [gpre0_g, gpre1_g, gpost0_g, gpost1_g, dpw_g, dps_g, dws_g, dbs_g, dcw_parts, dlg_parts, dlb_parts]
    row_of = [(0, 0), (0, 1), (1, 0), (1, 1), (2, 0), (3, 0), (4, 0), (5, 0), (6, 0), (7, 0), (8, 0)]
    weights = [(pre_norm, m_pre_norm, v_pre_norm), (post_norm, m_post_norm, v_post_norm),
               (even_pool_w[0], m_even_pool_w[0], v_even_pool_w[0]),
               (even_pool_scale, m_even_pool_scale, v_even_pool_scale),
               (odd_w_s[0], m_odd_w_s[0], v_odd_w_s[0]), (odd_b_s[0], m_odd_b_s[0], v_odd_b_s[0]),
               (even_conv_w[0], m_even_conv_w[0], v_even_conv_w[0]),
               (odd_ln_g, m_odd_ln_g, v_odd_ln_g), (odd_ln_b, m_odd_ln_b, v_odd_ln_b)]
    small = _small_sum_update(partials, row_of, weights, loss_g)
    loss = small[0][0, 0]

    def leaves(k, big):
        pre, post, pw, psc, ws, bs, cw, lg, lb = [small[1 + 4 * i + k] for i in range(len(weights))]
        wie_k, woe_k, wio_k, woo_k = big
        return [pre, post, wie_k[None], cw[None], pw[None], psc, woe_k[None], wio_k[None], lg, lb, ws[None], bs[None],
                woo_k[None]]

    outs = [loss, gx.reshape(1, S, D)]
    outs += leaves(0, (g_wie, g_woe, g_wio, g_woo))
    outs += leaves(1, (d_wie, d_woe, d_wio, d_woo))
    outs += leaves(2, (nm_wie, nm_woe, nm_wio, nm_woo))
    outs += leaves(3, (nv_wie, nv_woe, nv_wio, nv_woo))
    return tuple(outs)
```

```python
import jax
import jax.numpy as jnp
from jax import lax
from jax.experimental import pallas as pl
from jax.experimental.pallas import tpu as pltpu

F32 = jnp.float32
BF16 = jnp.bfloat16
MESH = pl.DeviceIdType.MESH

D = 1024
EPS = 1e-6
A_W = 512
B_W = 512
POOL_WINDOWS = (2, 4, 8, 16)
GROUP = 128
N_GROUPS = 4
N_HEADS = 8
HALO = 16
PROJ = 3072
N_DEV = 8
N_CHIPS = 4
SHARD_IN = PROJ // N_DEV
SHARD_OUT = D // N_DEV

ADAM_LR = 0.001
ADAM_B1 = 0.9
ADAM_B2 = 0.999
ADAM_EPS = 1e-08
ADAM_WD = 0.01
ADAM_STEP = 10

VMEM_LIMIT = 54 * 1024 * 1024

TS_PRENORM = 1024
TS_FWD = 512
TS_BWD = 256
TS_WGRAD = 2048
SUB_ROWS = 256


def _params(n_grid=1, vmem=VMEM_LIMIT):
    return pltpu.CompilerParams(dimension_semantics=("arbitrary",) * n_grid, vmem_limit_bytes=vmem)


def _full(shape):
    return pl.BlockSpec(shape, lambda *_: (0,) * len(shape))


def _resident(shape):
    return pl.BlockSpec(shape, lambda *_: (0,) * len(shape), pipeline_mode=pl.Buffered(1))


def _load_columns(blocks_ref, whole_ref, sems):
    copies = [pltpu.make_async_copy(blocks_ref.at[k], whole_ref.at[:, k * SHARD_IN:(k + 1) * SHARD_IN], sems.at[k])
              for k in range(N_DEV)]
    for cp in copies:
        cp.start()
    for cp in copies:
        cp.wait()


def _sigmoid(z):
    return jax.nn.sigmoid(z)


def _dsilu(z, s):
    return s * (1.0 + z * (1.0 - s))


def _dot(a, b):
    return jnp.dot(a, b, preferred_element_type=F32)


def _dot_t0(a, b):
    return lax.dot_general(a, b, (((0,), (0,)), ((), ())), preferred_element_type=F32)


def _dot_t1(a, b):
    return lax.dot_general(a, b, (((1,), (1,)), ((), ())), preferred_element_type=F32)


def _rms(x):
    return lax.rsqrt(jnp.mean(x * x, axis=-1, keepdims=True) + EPS)


def _shift_down(a, k):
    return pltpu.roll(a, k, 0)


def _shift_up(a, k):
    return pltpu.roll(a, a.shape[0] - k, 0)


def _mesh_position():
    return lax.axis_index("x"), lax.axis_index("y"), lax.axis_index("c")


def _index(pos):
    return 4 * pos[0] + 2 * pos[1] + pos[2]


def _peer(pos, d):
    x, y, c = pos
    return (1 - x if d & 4 else x, 1 - y if d & 2 else y, 1 - c if d & 1 else c)


def _remote(src, dst, send_sems, recv_sems, k, to):
    return pltpu.make_async_remote_copy(src_ref=src, dst_ref=dst, send_sem=send_sems.at[k], recv_sem=recv_sems.at[k],
                                        device_id=to, device_id_type=MESH)


class _Gather:
    def __init__(self, srcs, dsts, send_sems, recv_sems):
        x, y, c = _mesh_position()
        n = len(srcs)
        me, sibling = (x, y, c), (x, y, 1 - c)
        chips = [(1 - x, y), (x, 1 - y), (1 - x, 1 - y)]

        def copy(k, t, block, to, src=None):
            slot = dsts[t].at[_index(block)]
            return _remote(slot if src is None else src, slot, send_sems, recv_sems, k * n + t, to)

        tensors = range(n)
        self.local = [pltpu.make_async_copy(srcs[t], dsts[t].at[_index(me)], recv_sems.at[7 * n + t]) for t in tensors]
        self.first = [copy(1 + j, t, me, chip + (c,), srcs[t]) for t in tensors for j, chip in enumerate(chips)]
        self.first += [copy(0, t, me, sibling, srcs[t]) for t in tensors]
        self.ici_in = [copy(1 + j, t, chip + (c,), me) for t in tensors for j, chip in enumerate(chips)]
        self.passed = [copy(4 + j, t, chip + (c,), sibling) for t in tensors for j, chip in enumerate(chips)]
        self.d2d_in = [copy(0, t, sibling, me) for t in tensors]
        self.d2d_in += [copy(4 + j, t, chip + (1 - c,), me) for t in tensors for j, chip in enumerate(chips)]

    def start(self):
        for cp in self.local + self.first:
            cp.start()

    def middle(self):
        for landed, onward in zip(self.ici_in, self.passed):
            landed.wait_recv()
            onward.start()

    def finish(self):
        for cp in self.d2d_in:
            cp.wait_recv()
        for cp in self.first + self.passed:
            cp.wait_send()
        for cp in self.local:
            cp.wait()


class _Scatter:
    def __init__(self, srcs, dsts, send_sems, recv_sems):
        pos = _mesh_position()
        n = len(srcs)
        self.local = [pltpu.make_async_copy(srcs[t].at[_index(pos)], dsts[t].at[0], recv_sems.at[t]) for t in range(n)]
        self.remote = []
        for d in range(1, N_DEV):
            to = _peer(pos, d)
            self.remote += [_remote(srcs[t].at[_index(to)], dsts[t].at[d], send_sems, recv_sems, d * n + t, to)
                            for t in range(n)]

    def start(self):
        for cp in self.local + self.remote:
            cp.start()

    def middle(self):
        pass

    def finish(self):
        for cp in self.local:
            cp.wait()
        for cp in self.remote:
            cp.wait_recv()
        for cp in self.remote:
            cp.wait_send()


HBM_SPEC = pl.BlockSpec(memory_space=pltpu.HBM)
MIDDLE_STEPS_BEFORE_END = 4


def _landing_shape(exchange, src):
    return (N_DEV,) + src.shape if exchange is _Gather else src.shape


def _call(body, *, name, grid, in_specs, out_specs, out_shape, scratch_shapes, args, sides=()):
    params = _params(len(grid))
    if not sides:
        return pl.pallas_call(body, name=name, grid=grid, in_specs=in_specs, out_specs=out_specs, out_shape=out_shape,
                              scratch_shapes=scratch_shapes, compiler_params=params)(*args)
    n_in, n_out, n_scratch = len(in_specs), len(out_specs), len(scratch_shapes)
    counts = [len(srcs) for _, srcs in sides]
    ns = sum(counts)
    total = 1
    for g in grid:
        total *= g
    side_args, side_shapes, side_sems = [], [], []
    for exchange, srcs in sides:
        side_args += list(srcs)
        side_shapes += [jax.ShapeDtypeStruct(_landing_shape(exchange, s), s.dtype) for s in srcs]
        side_sems += [pltpu.SemaphoreType.DMA((N_DEV * len(srcs),))] * 2

    def wrapped(*refs):
        ins, side_in = refs[:n_in], refs[n_in:n_in + ns]
        outs = refs[n_in + ns:n_in + ns + n_out]
        side_out = refs[n_in + ns + n_out:n_in + 2 * ns + n_out]
        rest = refs[n_in + 2 * ns + n_out:]
        scratch, sems = rest[:n_scratch], rest[n_scratch:]
        step = pl.program_id(0)
        for axis in range(1, len(grid)):
            step = step * grid[axis] + pl.program_id(axis)

        def exchanges():
            built, at = [], 0
            for k, (exchange, _) in enumerate(sides):
                built.append(exchange(side_in[at:at + counts[k]], side_out[at:at + counts[k]],
                                      sems[2 * k], sems[2 * k + 1]))
                at += counts[k]
            return built

        @pl.when(step == 0)
        def _():
            for ex in exchanges():
                ex.start()

        @pl.when(step == max(total - MIDDLE_STEPS_BEFORE_END, 0))
        def _():
            for ex in exchanges():
                ex.middle()

        body(*ins, *outs, *scratch)

        @pl.when(step == total - 1)
        def _():
            for ex in exchanges():
                ex.finish()

    return pl.pallas_call(
        wrapped, name=name, grid=grid, in_specs=list(in_specs) + [HBM_SPEC] * ns,
        out_specs=list(out_specs) + [HBM_SPEC] * ns, out_shape=list(out_shape) + side_shapes,
        scratch_shapes=list(scratch_shapes) + side_sems, compiler_params=params)(*args, *side_args)


def _prenorm_gather(x, g, wie, small, woe, wio, woo, ts):
    S = x.shape[0]
    nt = S // ts

    def body(x_ref, g_ref, wie_ref, sm_ref, woe_ref, wio_ref, woo_ref,
             h_ref, wie_out, sm_out, woe_out, wio_b, woo_b, cast_ref, small_ref, cast2_ref, send_sems, recv_sems):
        step = pl.program_id(0)
        gather = lambda: _Gather([cast_ref, small_ref, cast2_ref], [wie_out, sm_out, woe_out], send_sems, recv_sems)

        @pl.when(step == 0)
        def _():
            cast_ref[...] = wie_ref[...].astype(BF16)
            small_ref[...] = sm_ref[...]
            cast2_ref[...] = woe_ref[...].astype(BF16)
            gather().start()
            wio_b[...] = wio_ref[...].astype(BF16)
            woo_b[...] = woo_ref[...].astype(BF16)

        xv = x_ref[...]
        h_ref[...] = ((xv * _rms(xv)) * g_ref[...]).astype(BF16)

        @pl.when(step == nt - 1)
        def _():
            exchange = gather()
            exchange.middle()
            exchange.finish()

    tile = pl.BlockSpec((ts, D), lambda i: (i, 0))
    sems = pltpu.SemaphoreType.DMA((N_DEV * 3,))
    shards = [wie, small, woe, wio, woo]
    return pl.pallas_call(
        body, name="prenorm_gather", grid=(nt,),
        in_specs=[tile, _full((1, D))] + [_full(a.shape) for a in shards],
        out_specs=[tile, HBM_SPEC, HBM_SPEC, HBM_SPEC] + [_full(a.shape) for a in shards[3:]],
        out_shape=[jax.ShapeDtypeStruct((S, D), BF16), jax.ShapeDtypeStruct((N_DEV,) + wie.shape, BF16),
                   jax.ShapeDtypeStruct((N_DEV,) + small.shape, F32), jax.ShapeDtypeStruct((N_DEV,) + woe.shape, BF16)]
        + [jax.ShapeDtypeStruct(a.shape, BF16) for a in shards[3:]],
        scratch_shapes=[pltpu.VMEM(wie.shape, BF16), pltpu.VMEM(small.shape, F32), pltpu.VMEM(woe.shape, BF16),
                        sems, sems],
        compiler_params=_params(),
    )(x, g, wie, small, woe, wio, woo)


def _even_mixer(proj, halo, row0, cw_ref, pw_ref):
    xa, gb, gc, za, xp, zp = proj
    nr = xa.shape[0]
    ha = gc * xa
    ha_ext = jnp.concatenate([halo[0], ha], axis=0)
    ha_m1 = _shift_down(ha_ext, 1)[HALO:]
    ha_m2 = _shift_down(ha_ext, 2)[HALO:]
    conv = cw_ref[2:3, :] * ha + cw_ref[1:2, :] * ha_m1 + cw_ref[0:1, :] * ha_m2
    sig_a = _sigmoid(za)
    silu_a = za * sig_a

    xp_ext = jnp.concatenate([halo[1], xp], axis=0)
    pos = row0 + lax.broadcasted_iota(jnp.int32, (nr, 1), 0)
    pooled, inv_cnt, mixed = [], [], []
    for g, w in enumerate(POOL_WINDOWS):
        cols = slice(g * GROUP, (g + 1) * GROUP)
        s = xp_ext[:, cols]
        k = 1
        while k < w:
            s = s + _shift_down(s, k)
            k *= 2
        inv = 1.0 / jnp.minimum(pos + 1, w).astype(F32)
        pg = s[HALO:] * inv - xp[:, cols]
        pooled.append(pg)
        inv_cnt.append(inv)
        mixed.append(_dot(pg.astype(BF16), pw_ref[g]))
    mixed = jnp.concatenate(mixed, axis=1)
    sig_b = _sigmoid(zp)
    silu_b = zp * sig_b
    return dict(xa=xa, gb=gb, gc=gc, za=za, xp=xp, zp=zp, ha=ha, ha_m1=ha_m1, ha_m2=ha_m2, conv=conv,
                sig_a=sig_a, silu_a=silu_a, pooled=pooled, inv_cnt=inv_cnt, mixed=mixed, sig_b=sig_b,
                silu_b=silu_b)


def _columns(a):
    return [a[:, k * A_W:(k + 1) * A_W].astype(F32) for k in range(6)]


def _even_forward_fused(h, x, win_b, cw, pw_b, ps, wout_b, gpost, gpre_next, ts, sides=()):
    S = x.shape[0]

    def body(h_ref, x_ref, wi_blocks, cw_ref, pw_ref, ps_ref, wo_ref, gp_ref, gn_ref,
             p_ref, m_ref, x1_ref, h1_ref, mix_ref, carry_ref, wi_ref, wi_sems):
        i = pl.program_id(0)

        @pl.when(i == 0)
        def _():
            carry_ref[...] = jnp.zeros_like(carry_ref)
            _load_columns(wi_blocks, wi_ref, wi_sems)

        for r0 in range(0, ts, SUB_ROWS):
            rows = slice(r0, r0 + SUB_ROWS)
            proj = _dot(h_ref[rows, :], wi_ref[...])
            p_ref[rows, :] = proj.astype(BF16)
            fw = _even_mixer(_columns(proj), (carry_ref[:, 0:A_W], carry_ref[:, A_W:D]), i * ts + r0, cw_ref, pw_ref)
            carry_ref[:, 0:A_W] = fw["ha"][SUB_ROWS - HALO:]
            carry_ref[:, A_W:D] = fw["xp"][SUB_ROWS - HALO:]
            mix_ref[rows, 0:A_W] = (fw["gb"] * fw["conv"] * fw["silu_a"]).astype(BF16)
            mix_ref[rows, A_W:D] = (fw["mixed"] * ps_ref[...] * fw["silu_b"]).astype(BF16)
            m = _dot(mix_ref[rows, :], wo_ref[...])
            m_ref[rows, :] = m.astype(BF16)
            x1 = x_ref[rows, :] + (m * _rms(m)) * gp_ref[...]
            x1_ref[rows, :] = x1
            h1_ref[rows, :] = ((x1 * _rms(x1)) * gn_ref[...]).astype(BF16)

    tile = pl.BlockSpec((ts, D), lambda i: (i, 0))
    return _call(
        body, name="even_forward", grid=(S // ts,),
        in_specs=[tile, tile, HBM_SPEC, _full((3, A_W)), _full((N_GROUPS, GROUP, GROUP)), _full((1, B_W)),
                  _resident((D, D)), _full((1, D)), _full((1, D))],
        out_specs=[pl.BlockSpec((ts, PROJ), lambda i: (i, 0)), tile, tile, tile],
        out_shape=[jax.ShapeDtypeStruct((S, PROJ), BF16), jax.ShapeDtypeStruct((S, D), BF16),
                   jax.ShapeDtypeStruct((S, D), F32), jax.ShapeDtypeStruct((S, D), BF16)],
        scratch_shapes=[pltpu.VMEM((ts, D), BF16), pltpu.VMEM((HALO, D), F32), pltpu.VMEM((D, PROJ), BF16),
                        pltpu.SemaphoreType.DMA((N_DEV,))],
        args=(h, x, win_b, cw, pw_b, ps, wout_b, gpost, gpre_next), sides=sides)


def _even_backward_fused(g1, m0, p, x, cw, pw_b, pwt_b, ps, woutt_b, wint_b, gpost, gpre, ts):
    S = g1.shape[0]
    nt = S // ts

    def body(g_ref, m_ref, p_ref, halo_ref, x_ref, cw_ref, pw_ref, pwt_ref, ps_ref, wot_ref, wit_ref, gp_ref, gn_ref,
             dp_ref, dx_ref, dwob_ref, dcw_ref, dpw_ref, dps_ref, dgp_ref, dgn_ref,
             carry_ref, mix_ref, dm_ref, dwo_ref):
        step = pl.program_id(0)
        i = nt - 1 - step

        @pl.when(step == 0)
        def _():
            carry_ref[...] = jnp.zeros_like(carry_ref)
            dwo_ref[...] = jnp.zeros_like(dwo_ref)
            dcw_ref[...] = jnp.zeros_like(dcw_ref)
            dpw_ref[...] = jnp.zeros_like(dpw_ref)
            dps_ref[...] = jnp.zeros_like(dps_ref)
            dgp_ref[...] = jnp.zeros_like(dgp_ref)
            dgn_ref[...] = jnp.zeros_like(dgn_ref)

        ps_v = ps_ref[...]
        sums = [None] * 6
        add = lambda k, part: sums.__setitem__(k, part if sums[k] is None else sums[k] + part)
        for r0 in range(ts - SUB_ROWS, -1, -SUB_ROWS):
            rows = slice(r0, r0 + SUB_ROWS)
            g = g_ref[rows, :]
            m = m_ref[rows, :].astype(F32)
            q = _rms(m)
            n = m * q
            add(0, jnp.sum(g * n, axis=0, keepdims=True))
            dn = g * gp_ref[...]
            dm = q * (dn - n * jnp.mean(dn * n, axis=-1, keepdims=True))
            dm_ref[rows, :] = dm.astype(BF16)
            dmix = _dot(dm_ref[rows, :], wot_ref[...])
            dya = dmix[:, 0:A_W]
            dyb = dmix[:, A_W:D]

            if r0 == 0:
                before = _columns(halo_ref[...])
                keep = jnp.where(i == 0, 0.0, 1.0).astype(F32)
                halo = (before[2] * before[0] * keep, before[4] * keep)
            else:
                before = _columns(p_ref[r0 - HALO:r0, :])
                halo = (before[2] * before[0], before[4])
            fw = _even_mixer(_columns(p_ref[rows, :]), halo, i * ts + r0, cw_ref, pw_ref)
            mix_ref[rows, 0:A_W] = (fw["gb"] * fw["conv"] * fw["silu_a"]).astype(BF16)
            mix_ref[rows, A_W:D] = (fw["mixed"] * ps_v * fw["silu_b"]).astype(BF16)

            t = dya * fw["gb"]
            dconv = t * fw["silu_a"]
            dgb = dya * fw["conv"] * fw["silu_a"]
            dza = t * fw["conv"] * _dsilu(fw["za"], fw["sig_a"])
            add(2, jnp.sum(dconv * fw["ha"], axis=0, keepdims=True))
            add(3, jnp.sum(dconv * fw["ha_m1"], axis=0, keepdims=True))
            add(4, jnp.sum(dconv * fw["ha_m2"], axis=0, keepdims=True))
            dconv_ext = jnp.concatenate([dconv, carry_ref[:, 0:A_W]], axis=0)
            dha = (cw_ref[2:3, :] * dconv + cw_ref[1:2, :] * _shift_up(dconv_ext, 1)[:SUB_ROWS]
                   + cw_ref[0:1, :] * _shift_up(dconv_ext, 2)[:SUB_ROWS])
            dgc = dha * fw["xa"]
            dxa = dha * fw["gc"]

            u = dyb * fw["mixed"]
            add(1, jnp.sum(u * fw["silu_b"], axis=0, keepdims=True))
            dzp = u * ps_v * _dsilu(fw["zp"], fw["sig_b"])
            dmixed = (dyb * ps_v * fw["silu_b"]).astype(BF16)
            dxp, e_first = [], []
            for gi, w in enumerate(POOL_WINDOWS):
                cols = slice(gi * GROUP, (gi + 1) * GROUP)
                dmg = dmixed[:, cols]
                dpooled = _dot(dmg, pwt_ref[gi])
                dpw_ref[gi] += _dot_t0(fw["pooled"][gi].astype(BF16), dmg)
                e = dpooled * fw["inv_cnt"][gi]
                e_first.append(e[0:HALO])
                s = jnp.concatenate([e, carry_ref[:, A_W + gi * GROUP:A_W + (gi + 1) * GROUP]], axis=0)
                k = 1
                while k < w:
                    s = s + _shift_up(s, k)
                    k *= 2
                dxp.append(s[:SUB_ROWS] - dpooled)
            carry_ref[:, 0:A_W] = dconv[0:HALO]
            carry_ref[:, A_W:D] = jnp.concatenate(e_first, axis=1)

            dp_ref[rows, 0:512] = dxa.astype(BF16)
            dp_ref[rows, 512:1024] = dgb.astype(BF16)
            dp_ref[rows, 1024:1536] = dgc.astype(BF16)
            dp_ref[rows, 1536:2048] = dza.astype(BF16)
            dp_ref[rows, 2048:2560] = jnp.concatenate(dxp, axis=1).astype(BF16)
            dp_ref[rows, 2560:3072] = dzp.astype(BF16)

            dh = _dot(dp_ref[rows, :], wit_ref[...])
            xv = x_ref[rows, :]
            r = _rms(xv)
            xn = xv * r
            add(5, jnp.sum(dh * xn, axis=0, keepdims=True))
            dxn = dh * gn_ref[...]
            dx_ref[rows, :] = g + r * (dxn - xn * jnp.mean(dxn * xn, axis=-1, keepdims=True))

        dwo_ref[...] += _dot_t0(mix_ref[...], dm_ref[...])
        dgp_ref[...] += sums[0]
        dps_ref[...] += sums[1]
        dcw_ref[2:3, :] += sums[2]
        dcw_ref[1:2, :] += sums[3]
        dcw_ref[0:1, :] += sums[4]
        dgn_ref[...] += sums[5]

        @pl.when(step == nt - 1)
        def _():
            dwob_ref[...] = dwo_ref[...].astype(BF16)

    rev = lambda s: (nt - 1 - s, 0)
    tile = pl.BlockSpec((ts, D), rev)
    wide = pl.BlockSpec((ts, PROJ), rev)
    bpt = ts // HALO
    halo_map = lambda s: (jnp.maximum((nt - 1 - s) * bpt - 1, 0), 0)
    groups = _full((N_GROUPS, GROUP, GROUP))
    vec = _full((1, D))
    return pl.pallas_call(
        body, name="even_backward", grid=(nt,),
        in_specs=[tile, tile, wide, pl.BlockSpec((HALO, PROJ), halo_map), tile, _full((3, A_W)), groups, groups,
                  _full((1, B_W)), _resident((D, D)), _resident((PROJ, D)), vec, vec],
        out_specs=[wide, tile, _full((D, D)), _full((8, A_W)), groups, _full((1, B_W)), vec, vec],
        out_shape=[jax.ShapeDtypeStruct((S, PROJ), BF16), jax.ShapeDtypeStruct((S, D), F32),
                   jax.ShapeDtypeStruct((D, D), BF16), jax.ShapeDtypeStruct((8, A_W), F32),
                   jax.ShapeDtypeStruct((N_GROUPS, GROUP, GROUP), F32), jax.ShapeDtypeStruct((1, B_W), F32),
                   jax.ShapeDtypeStruct((1, D), F32), jax.ShapeDtypeStruct((1, D), F32)],
        scratch_shapes=[pltpu.VMEM((HALO, D), F32), pltpu.VMEM((ts, D), BF16), pltpu.VMEM((ts, D), BF16),
                        pltpu.VMEM((D, D), F32)],
        compiler_params=_params(),
    )(g1, m0, p, p, x, cw, pw_b, pwt_b, ps, woutt_b, wint_b, gpost, gpre)


def _gate_matmuls(w_ref, in_ref, out_ref, chunks, bias_ref=None):
    for h in range(N_HEADS):
        cols = slice(h * GROUP, (h + 1) * GROUP)
        wide = jnp.concatenate([in_ref[chunk, cols] for chunk in chunks], axis=1)
        res = _dot(w_ref[h], wide)
        for k, chunk in enumerate(chunks):
            part = res[:, k * GROUP:(k + 1) * GROUP]
            out_ref[chunk, cols] = part if bias_ref is None else part + bias_ref[h]


def _odd_forward_parts(v, lg_ref, lb_ref, wt_ref, bfull_ref, vln_ref, sv_ref, r0, nr):
    rows = slice(r0, r0 + nr)
    mu = jnp.mean(v, axis=-1, keepdims=True)
    vc = v - mu
    rstd = lax.rsqrt(jnp.mean(vc * vc, axis=-1, keepdims=True) + EPS)
    vh = vc * rstd
    vln_ref[rows, :] = (vh * lg_ref[...] + lb_ref[...]).astype(BF16)
    _gate_matmuls(wt_ref, vln_ref, sv_ref, [slice(r0 + k * GROUP, r0 + (k + 1) * GROUP) for k in range(nr // GROUP)],
                  bfull_ref)
    return vh, rstd


def _odd_forward(h, x1, tgt, win_b, lg, lb, wt_b, bfull, wout_b, gpost, ts):
    S = x1.shape[0]

    def body(h_ref, x_ref, t_ref, wi_blocks, lg_ref, lb_ref, wt_ref, bfull_ref, wo_ref, gp_ref,
             p_ref, m_ref, g_ref, loss_ref, vln_ref, sv_ref, y_ref, wi_ref, wi_sems):
        @pl.when(pl.program_id(0) == 0)
        def _():
            loss_ref[...] = jnp.zeros_like(loss_ref)
            _load_columns(wi_blocks, wi_ref, wi_sems)

        loss = None
        for r0 in range(0, ts, SUB_ROWS):
            rows = slice(r0, r0 + SUB_ROWS)
            proj = _dot(h_ref[rows, :], wi_ref[...])
            p_ref[rows, :] = proj.astype(BF16)
            u, v, z = proj[:, 0:D], proj[:, D:2 * D], proj[:, 2 * D:3 * D]
            _odd_forward_parts(v, lg_ref, lb_ref, wt_ref, bfull_ref, vln_ref, sv_ref, r0, SUB_ROWS)
            y_ref[rows, :] = (u * sv_ref[rows, :] * (z * _sigmoid(z))).astype(BF16)
            m = _dot(y_ref[rows, :], wo_ref[...])
            m_ref[rows, :] = m.astype(BF16)
            x2 = x_ref[rows, :] + (m * _rms(m)) * gp_ref[...]
            err = x2 - t_ref[rows, :]
            g_ref[rows, :] = (err * (1.0 / D)).astype(BF16)
            part = jnp.sum(err * err, axis=0, keepdims=True)
            loss = part if loss is None else loss + part
        loss_ref[...] += loss

    tile = pl.BlockSpec((ts, D), lambda i: (i, 0))
    small = _full((N_HEADS, GROUP, GROUP))
    return pl.pallas_call(
        body, name="odd_forward", grid=(S // ts,),
        in_specs=[pl.BlockSpec((ts, D), lambda i: (i, 0)), tile, tile, HBM_SPEC, _full((1, D)), _full((1, D)),
                  small, small, _full((D, D)), _full((1, D))],
        out_specs=[pl.BlockSpec((ts, PROJ), lambda i: (i, 0)), tile, tile, _full((1, D))],
        out_shape=[jax.ShapeDtypeStruct((S, PROJ), BF16), jax.ShapeDtypeStruct((S, D), BF16),
                   jax.ShapeDtypeStruct((S, D), BF16), jax.ShapeDtypeStruct((1, D), F32)],
        scratch_shapes=[pltpu.VMEM((ts, D), BF16), pltpu.VMEM((ts, D), F32), pltpu.VMEM((ts, D), BF16),
                        pltpu.VMEM((D, PROJ), BF16), pltpu.SemaphoreType.DMA((N_DEV,))],
        compiler_params=_params(),
    )(h, x1, tgt, win_b, lg, lb, wt_b, bfull, wout_b, gpost)


def _odd_backward(g2, m1, p, x1, lg, lb, wt_b, wtt_b, bfull, woutt_b, wint_b, gpost, gpre, ts):
    S = g2.shape[0]
    nt = S // ts
    sub = ts // 2

    def body(g_ref, m_ref, p_ref, x_ref, lg_ref, lb_ref, wt_ref, wtt_ref, bfull_ref, wot_ref, wit_ref, gp_ref, gn_ref,
             dp_ref, dx_ref, dwob_ref, dws_ref, dbs_ref, dlg_ref, dlb_ref, dgp_ref, dgn_ref,
             vln_ref, sv_ref, dsvb_ref, dvln_ref, dsum_ref, y_ref, dm_ref, dwo_ref):
        step = pl.program_id(0)

        @pl.when(step == 0)
        def _():
            dwo_ref[...] = jnp.zeros_like(dwo_ref)
            dws_ref[...] = jnp.zeros_like(dws_ref)
            dsum_ref[...] = jnp.zeros_like(dsum_ref)
            dlg_ref[...] = jnp.zeros_like(dlg_ref)
            dlb_ref[...] = jnp.zeros_like(dlb_ref)
            dgp_ref[...] = jnp.zeros_like(dgp_ref)
            dgn_ref[...] = jnp.zeros_like(dgn_ref)

        tril = (lax.broadcasted_iota(jnp.int32, (GROUP, GROUP), 0)
                >= lax.broadcasted_iota(jnp.int32, (GROUP, GROUP), 1))
        sums = [None] * 4
        add = lambda k, part: sums.__setitem__(k, part if sums[k] is None else sums[k] + part)

        def post_norm(c):
            g = g_ref[c["rows"], :].astype(F32)
            m = m_ref[c["rows"], :].astype(F32)
            q = _rms(m)
            n = m * q
            add(0, jnp.sum(g * n, axis=0, keepdims=True))
            dn = g * gp_ref[...]
            dm = q * (dn - n * jnp.mean(dn * n, axis=-1, keepdims=True))
            dm_ref[c["rows"], :] = dm.astype(BF16)

        def out_projection(c):
            c["dy"] = _dot(dm_ref[c["rows"], :], wot_ref[...])

        def layernorm(c):
            v = p_ref[c["rows"], D:2 * D].astype(F32)
            mu = jnp.mean(v, axis=-1, keepdims=True)
            vc = v - mu
            c["rstd"] = lax.rsqrt(jnp.mean(vc * vc, axis=-1, keepdims=True) + EPS)
            c["vh"] = vc * c["rstd"]
            vln_ref[c["rows"], :] = (c["vh"] * lg_ref[...] + lb_ref[...]).astype(BF16)

        def gate_matmuls(c):
            _gate_matmuls(wt_ref, vln_ref, sv_ref, c["chunks"], bfull_ref)

        def gating(c):
            rows = c["rows"]
            u = p_ref[rows, 0:D].astype(F32)
            z = p_ref[rows, 2 * D:3 * D].astype(F32)
            sv = sv_ref[rows, :]
            sig = _sigmoid(z)
            sz = z * sig
            y_ref[rows, :] = (u * sv * sz).astype(BF16)
            dy = c.pop("dy")
            t = dy * sz
            dsv = t * u
            dsvb_ref[rows, :] = dsv.astype(BF16)
            for k in range(sub // GROUP):
                dsum_ref[...] += dsv[k * GROUP:(k + 1) * GROUP]
            dp_ref[rows, 0:D] = (t * sv).astype(BF16)
            dp_ref[rows, 2 * D:3 * D] = (dy * u * sv * _dsilu(z, sig)).astype(BF16)

        def gate_backward_matmuls(c):
            for chunk in c["chunks"]:
                for h in range(N_HEADS):
                    cols = slice(h * GROUP, (h + 1) * GROUP)
                    dvln_ref[chunk, cols] = _dot(wtt_ref[h], dsvb_ref[chunk, cols])
                    dws_ref[h] += jnp.where(tril, _dot_t1(dsvb_ref[chunk, cols], vln_ref[chunk, cols]), 0.0)

        def layernorm_backward(c):
            vh, rstd = c.pop("vh"), c.pop("rstd")
            dvln = dvln_ref[c["rows"], :]
            add(1, jnp.sum(dvln * vh, axis=0, keepdims=True))
            add(2, jnp.sum(dvln, axis=0, keepdims=True))
            dvh = dvln * lg_ref[...]
            dv = rstd * (dvh - jnp.mean(dvh, axis=-1, keepdims=True)
                         - vh * jnp.mean(dvh * vh, axis=-1, keepdims=True))
            dp_ref[c["rows"], D:2 * D] = dv.astype(BF16)

        def in_projection(c):
            c["dh"] = _dot(dp_ref[c["rows"], :], wit_ref[...])

        def pre_norm(c):
            dh = c.pop("dh")
            xv = x_ref[c["rows"], :]
            r = _rms(xv)
            xn = xv * r
            add(3, jnp.sum(dh * xn, axis=0, keepdims=True))
            dxn = dh * gn_ref[...]
            dx_ref[c["rows"], :] = (g_ref[c["rows"], :].astype(F32)
                                    + r * (dxn - xn * jnp.mean(dxn * xn, axis=-1, keepdims=True)))

        phases = [post_norm, out_projection, layernorm, gate_matmuls, gating, gate_backward_matmuls,
                  layernorm_backward, in_projection, pre_norm]
        groups = [dict(rows=slice(r0, r0 + sub),
                       chunks=[slice(r0 + k * GROUP, r0 + (k + 1) * GROUP) for k in range(sub // GROUP)])
                  for r0 in range(0, ts, sub)]
        for group in groups:
            for phase in phases:
                phase(group)

        dwo_ref[...] += _dot_t0(y_ref[...], dm_ref[...])
        dgp_ref[...] += sums[0]
        dlg_ref[...] += sums[1]
        dlb_ref[...] += sums[2]
        dgn_ref[...] += sums[3]

        @pl.when(step == nt - 1)
        def _():
            dwob_ref[...] = dwo_ref[...].astype(BF16)
            ones = jnp.ones((8, GROUP), F32)
            for h in range(N_HEADS):
                cols = slice(h * GROUP, (h + 1) * GROUP)
                total = lax.dot_general(ones, dsum_ref[:, cols], (((1,), (1,)), ((), ())),
                                        precision=lax.Precision.HIGHEST, preferred_element_type=F32)
                dbs_ref[h:h + 1, :] = total[0:1, :]

    tile = pl.BlockSpec((ts, D), lambda i: (i, 0))
    wide = pl.BlockSpec((ts, PROJ), lambda i: (i, 0))
    small = _full((N_HEADS, GROUP, GROUP))
    heads = _resident((N_HEADS, GROUP, GROUP))
    vec = _full((1, D))
    return pl.pallas_call(
        body, name="odd_backward", grid=(nt,),
        in_specs=[tile, tile, wide, tile, vec, vec, heads, heads, heads, _resident((D, D)), _resident((PROJ, D)),
                  vec, vec],
        out_specs=[wide, tile, _full((D, D)), small, _full((N_HEADS, GROUP)), vec, vec, vec, vec],
        out_shape=[jax.ShapeDtypeStruct((S, PROJ), BF16), jax.ShapeDtypeStruct((S, D), F32),
                   jax.ShapeDtypeStruct((D, D), BF16),
                   jax.ShapeDtypeStruct((N_HEADS, GROUP, GROUP), F32), jax.ShapeDtypeStruct((N_HEADS, GROUP), F32),
                   jax.ShapeDtypeStruct((1, D), F32), jax.ShapeDtypeStruct((1, D), F32),
                   jax.ShapeDtypeStruct((1, D), F32), jax.ShapeDtypeStruct((1, D), F32)],
        scratch_shapes=[pltpu.VMEM((ts, D), BF16), pltpu.VMEM((ts, D), F32), pltpu.VMEM((ts, D), BF16),
                        pltpu.VMEM((ts, D), F32), pltpu.VMEM((GROUP, D), F32), pltpu.VMEM((ts, D), BF16),
                        pltpu.VMEM((ts, D), BF16), pltpu.VMEM((D, D), F32)],
        compiler_params=_params(),
    )(g2, m1, p, x1, lg, lb, wt_b, wtt_b, bfull, woutt_b, wint_b, gpost, gpre)


def _weight_grad_scatter(h, dp, ts, name, sides=()):
    S = h.shape[0]
    ts = min(ts, S)
    nt = S // ts
    panel = 2 * SHARD_IN
    last = N_CHIPS - 1

    def body(h_ref, dp_ref, parts_ref, acc_ref, half_ref, swap_ref, sum_ref, send_sems, recv_sems):
        s, t = pl.program_id(0), pl.program_id(1)
        x, y, c = _mesh_position()
        sibling = (x, y, 1 - c)

        def swap(k):
            return _remote(half_ref.at[k, 1], swap_ref.at[k], send_sems, recv_sems, k, sibling)

        def chip_sum(k):
            flip = last - k
            to = (x ^ (flip >> 1), y ^ (flip & 1), c)
            return _remote(sum_ref.at[k], parts_ref.at[1 + flip], send_sems, recv_sems, 3 + flip, to)

        own = pltpu.make_async_copy(half_ref.at[last, 0], parts_ref.at[0], recv_sems.at[7])
        to_sibling = _remote(half_ref.at[last, 1], parts_ref.at[1], send_sems, recv_sems, 3, sibling)

        @pl.when(t == 0)
        def _():
            acc_ref[...] = jnp.zeros_like(acc_ref)

        for k in range(last):
            @pl.when((s == k + 1) & (t == 0))
            def _():
                swap(k).wait_recv()
                sum_ref[k] = (half_ref[k, 0].astype(F32) + swap_ref[k].astype(F32)).astype(BF16)
                chip_sum(k).start()

        acc_ref[...] += _dot_t0(h_ref[...], dp_ref[...])

        @pl.when(t == nt - 1)
        def _():
            for core in range(2):
                @pl.when(c == core)
                def _():
                    half_ref[s, core] = acc_ref[:, 0:SHARD_IN].astype(BF16)
                    half_ref[s, 1 - core] = acc_ref[:, SHARD_IN:panel].astype(BF16)
            for k in range(last):
                @pl.when(s == k)
                def _():
                    swap(k).start()

        @pl.when((s == last) & (t == nt - 1))
        def _():
            own.start()
            to_sibling.start()
            own.wait()
            _remote(half_ref.at[last, 1], parts_ref.at[1], send_sems, recv_sems, 3, sibling).wait_recv()
            for k in range(last):
                chip_sum(k).wait_recv()
            to_sibling.wait_send()
            for k in range(last):
                swap(k).wait_send()
                chip_sum(k).wait_send()

    chip_panel = lambda s, t: (t, (2 * lax.axis_index("x") + lax.axis_index("y")) ^ (last - s))
    sems = pltpu.SemaphoreType.DMA((8,))
    return _call(
        body, name=name, grid=(N_CHIPS, nt),
        in_specs=[pl.BlockSpec((ts, D), lambda s, t: (t, 0)), pl.BlockSpec((ts, panel), chip_panel)],
        out_specs=[HBM_SPEC], out_shape=[jax.ShapeDtypeStruct((N_CHIPS + 1, D, SHARD_IN), BF16)],
        scratch_shapes=[pltpu.VMEM((D, panel), F32), pltpu.VMEM((N_CHIPS, 2, D, SHARD_IN), BF16),
                        pltpu.VMEM((last, D, SHARD_IN), BF16), pltpu.VMEM((last, D, SHARD_IN), BF16), sems, sems],
        args=(h, dp), sides=sides)


def _adamw(w, g, m, v):
    m = ADAM_B1 * m + (1.0 - ADAM_B1) * g
    v = ADAM_B2 * v + (1.0 - ADAM_B2) * (g * g)
    m_hat = m / (1.0 - ADAM_B1 ** ADAM_STEP)
    v_hat = v / (1.0 - ADAM_B2 ** ADAM_STEP)
    delta = -ADAM_LR * (m_hat / (jnp.sqrt(v_hat) + ADAM_EPS) + ADAM_WD * w)
    return delta, m, v


UPDATE_STEPS = 4


def _sum_update(parts, weights):
    n_w = len(weights)

    def body(*refs):
        p_refs, wmv_refs, outs = refs[:n_w], refs[n_w:4 * n_w], refs[4 * n_w:]
        for i in range(n_w):
            acc = p_refs[i][0].astype(F32)
            for d in range(1, parts[i].shape[0]):
                acc = acc + p_refs[i][d].astype(F32)
            w_ref, m_ref, v_ref = wmv_refs[3 * i:3 * i + 3]
            g_out, d_out, m_out, v_out = outs[4 * i:4 * i + 4]
            g_out[...] = acc
            delta, m_new, v_new = _adamw(w_ref[...], acc, m_ref[...], v_ref[...])
            d_out[...] = delta
            m_out[...] = m_new
            v_out[...] = v_new

    part_specs, wmv_specs, out_specs, out_shape = [], [], [], []
    for p, (w, _, _) in zip(parts, weights):
        R, C = w.shape
        rb = R // UPDATE_STEPS
        blk = pl.BlockSpec((rb, C), lambda i: (i, 0))
        part_specs.append(pl.BlockSpec((p.shape[0], rb, C), lambda i: (0, i, 0)))
        wmv_specs += [blk] * 3
        out_specs += [blk] * 4
        out_shape += [jax.ShapeDtypeStruct((R, C), F32)] * 4
    flat = pl.pallas_call(
        body, name="update_large_weights", grid=(UPDATE_STEPS,), in_specs=part_specs + wmv_specs,
        out_specs=out_specs, out_shape=out_shape, compiler_params=_params(),
    )(*parts, *[a for wmv in weights for a in wmv])
    return [flat[4 * i:4 * i + 4] for i in range(n_w)]


def _small_sum_update(partials, row_of, weights, loss_parts):
    n_p, n_w = len(partials), len(weights)

    def body(*refs):
        p_refs = refs[:n_p]
        loss_ref = refs[n_p]
        w_refs = refs[n_p + 1:n_p + 1 + 3 * n_w]
        loss_out = refs[n_p + 1 + 3 * n_w]
        outs = refs[n_p + 2 + 3 * n_w:]

        def total(ref):
            acc = ref[0].astype(F32)
            for d in range(1, N_DEV):
                acc = acc + ref[d].astype(F32)
            return acc

        lsum = jnp.sum(total(loss_ref), axis=-1, keepdims=True) * (0.5 / D)
        loss_out[...] = jnp.broadcast_to(lsum, loss_out.shape)
        for k in range(n_p):
            i, r0 = row_of[k]
            g = total(p_refs[k])
            w_ref, m_ref, v_ref = w_refs[3 * i:3 * i + 3]
            g_out, d_out, m_out, v_out = outs[4 * i:4 * i + 4]
            if g.ndim == 2:
                at = (slice(r0, r0 + g.shape[0]), slice(None))
            else:
                at = (slice(None),) * g.ndim
            delta, m_new, v_new = _adamw(w_ref[at], g, m_ref[at], v_ref[at])
            g_out[at] = g
            d_out[at] = delta
            m_out[at] = m_new
            v_out[at] = v_new

    vm = pl.BlockSpec(memory_space=pltpu.VMEM)
    flat_w = [a for wmv in weights for a in wmv]
    out_shape = [jax.ShapeDtypeStruct((1, 128), F32)]
    for w, _, _ in weights:
        out_shape += [jax.ShapeDtypeStruct(w.shape, F32)] * 4
    return pl.pallas_call(
        body, name="small_update", in_specs=[vm] * (n_p + 1 + 3 * n_w), out_specs=[vm] * len(out_shape),
        out_shape=out_shape, compiler_params=pltpu.CompilerParams(vmem_limit_bytes=VMEM_LIMIT),
    )(*partials, loss_parts, *flat_w)


def _pack_small_shard(cw, lg, lb):
    return (jnp.pad(cw, ((0, 13), (0, 64))) + jnp.pad(lg, ((8, 7), (0, 0))) + jnp.pad(lb, ((9, 6), (0, 0))))


def kernel(x, pre_norm, post_norm, even_w_in, even_conv_w, even_pool_w, even_pool_scale, even_w_out, odd_w_in, odd_ln_g, odd_ln_b, odd_w_s, odd_b_s, odd_w_out, loss_target, m_pre_norm, m_post_norm, m_even_w_in, m_even_conv_w, m_even_pool_w, m_even_pool_scale, m_even_w_out, m_odd_w_in, m_odd_ln_g, m_odd_ln_b, m_odd_w_s, m_odd_b_s, m_odd_w_out, v_pre_norm, v_post_norm, v_even_w_in, v_even_conv_w, v_even_pool_w, v_even_pool_scale, v_even_w_out, v_odd_w_in, v_odd_ln_g, v_odd_ln_b, v_odd_w_s, v_odd_b_s, v_odd_w_out):
    S = x.shape[1]
    xs = x.reshape(S, D)
    tgt = loss_target.reshape(S, D)

    gpre0, gpre1 = pre_norm[0:1], pre_norm[1:2]
    gpost0, gpost1 = post_norm[0:1], post_norm[1:2]
    small_shard = _pack_small_shard(even_conv_w[0], odd_ln_g, odd_ln_b)
    h0, wie_g, small_g, woe_g, wio_b, woo_b = _prenorm_gather(xs, gpre0, even_w_in[0], small_shard, even_w_out[0],
                                                              odd_w_in[0], odd_w_out[0], TS_PRENORM)
    wie_t = jnp.transpose(wie_g, (0, 2, 1)).reshape(PROJ, D)
    woe = woe_g.reshape(D, D)
    woe_t = woe.T
    conv_w = jnp.transpose(small_g[:, 0:3, 0:64], (1, 0, 2)).reshape(3, A_W)
    ln_g = small_g[:, 8, :].reshape(1, D)
    ln_b = small_g[:, 9, :].reshape(1, D)

    pool_w_b = even_pool_w[0].astype(BF16)
    pool_wt_b = jnp.swapaxes(even_pool_w[0], 1, 2).astype(BF16)
    ws_tril = jnp.tril(odd_w_s[0])
    ws_b = ws_tril.astype(BF16)
    wst_b = jnp.swapaxes(ws_tril, 1, 2).astype(BF16)
    b_full = jnp.broadcast_to(odd_b_s[0][:, :, None], (N_HEADS, GROUP, GROUP))

    p0, m0, x1, h1, wio_g, woo_g = _even_forward_fused(h0, xs, wie_g, conv_w, pool_w_b, even_pool_scale, woe, gpost0,
                                                       gpre1, TS_FWD, sides=[(_Gather, [wio_b, woo_b])])
    wio_t = jnp.transpose(wio_g, (0, 2, 1)).reshape(PROJ, D)
    woo = woo_g.reshape(D, D)
    woo_t = woo.T
    p1, m1, g2, loss_vec = _odd_forward(h1, x1, tgt, wio_g, ln_g, ln_b, ws_b, b_full, woo, gpost1, TS_FWD)

    dp1, g1, dwoo, dws, dbs, dlg, dlb, dgpost1, dgpre1 = _odd_backward(
        g2, m1, p1, x1, ln_g, ln_b, ws_b, wst_b, b_full, woo_t, wio_t, gpost1, gpre1, TS_FWD)
    by_owner = lambda dwo: dwo.reshape(N_DEV, SHARD_OUT, D)
    (dwio_parts, dwoo_parts, dlg_parts, dlb_parts,
     dws_g, dbs_g, gpost1_g, gpre1_g, loss_g) = _weight_grad_scatter(
        h1, dp1, TS_WGRAD, "odd_weight_grad",
        sides=[(_Scatter, [by_owner(dwoo), dlg.reshape(N_DEV, 1, 128), dlb.reshape(N_DEV, 1, 128)]),
               (_Gather, [dws.astype(BF16), dbs, dgpost1, dgpre1, loss_vec])])
    dp0, gx, dwoe, dcw, dpw, dps, dgpost0, dgpre0 = _even_backward_fused(
        g1, m0, p0, xs, conv_w, pool_w_b, pool_wt_b, even_pool_scale, woe_t, wie_t, gpost0, gpre0, TS_FWD)
    dcw_by_owner = jnp.transpose(dcw[0:3].reshape(3, N_DEV, 64), (1, 0, 2))
    dwie_parts, dwoe_parts, dcw_parts, dpw_g, dps_g, gpost0_g, gpre0_g = _weight_grad_scatter(
        h0, dp0, TS_WGRAD, "even_weight_grad",
        sides=[(_Scatter, [by_owner(dwoe), dcw_by_owner]), (_Gather, [dpw.astype(BF16), dps, dgpost0, dgpre0])])

    ((g_wie, d_wie, nm_wie, nv_wie), (g_wio, d_wio, nm_wio, nv_wio), (g_woe, d_woe, nm_woe, nv_woe),
     (g_woo, d_woo, nm_woo, nv_woo)) = _sum_update(
        [dwie_parts, dwio_parts, dwoe_parts, dwoo_parts],
        [(even_w_in[0], m_even_w_in[0], v_even_w_in[0]), (odd_w_in[0], m_odd_w_in[0], v_odd_w_in[0]),
         (even_w_out[0], m_even_w_out[0], v_even_w_out[0]), (odd_w_out[0], m_odd_w_out[0], v_odd_w_out[0])])

    partials = [gpre0_g, gpre1_g, gpost0_g, gpost1_g, dpw_g, dps_g, dws_g, dbs_g, dcw_parts, dlg_parts, dlb_parts]
    row_of = [(0, 0), (0, 1), (1, 0), (1, 1), (2, 0), (3, 0), (4, 0), (5, 0), (6, 0), (7, 0), (8, 0)]
    weights = [(pre_norm, m_pre_norm, v_pre_norm), (post_norm, m_post_norm, v_post_norm),
               (even_pool_w[0], m_even_pool_w[0], v_even_pool_w[0]),
               (even_pool_scale, m_even_pool_scale, v_even_pool_scale),
               (odd_w_s[0], m_odd_w_s[0], v_odd_w_s[0]), (odd_b_s[0], m_odd_b_s[0], v_odd_b_s[0]),
               (even_conv_w[0], m_even_conv_w[0], v_even_conv_w[0]),
               (odd_ln_g, m_odd_ln_g, v_odd_ln_g), (odd_ln_b, m_odd_ln_b, v_odd_ln_b)]
    small = _small_sum_update(partials, row_of, weights, loss_g)
    loss = small[0][0, 0]

    def leaves(k, big):
        pre, post, pw, psc, ws, bs, cw, lg, lb = [small[1 + 4 * i + k] for i in range(len(weights))]
        wie_k, woe_k, wio_k, woo_k = big
        return [pre, post, wie_k[None], cw[None], pw[None], psc, woe_k[None], wio_k[None], lg, lb, ws[None], bs[None],
                woo_k[None]]

    outs = [loss, gx.reshape(1, S, D)]
    outs += leaves(0, (g_wie, g_woe, g_wio, g_woo))
    outs += leaves(1, (d_wie, d_woe, d_wio, d_woo))
    outs += leaves(2, (nm_wie, nm_woe, nm_wio, nm_woo))
    outs += leaves(3, (nv_wie, nv_woe, nv_wio, nv_woo))
    return tuple(outs)
```

```python
import jax
import jax.numpy as jnp
from jax import lax
from jax.experimental import pallas as pl
from jax.experimental.pallas import tpu as pltpu

F32 = jnp.float32
BF16 = jnp.bfloat16
MESH = pl.DeviceIdType.MESH

D = 1024
EPS = 1e-6
A_W = 512
B_W = 512
POOL_WINDOWS = (2, 4, 8, 16)
GROUP = 128
N_GROUPS = 4
N_HEADS = 8
HALO = 16
PROJ = 3072
N_DEV = 8
N_CHIPS = 4
SHARD_IN = PROJ // N_DEV
SHARD_OUT = D // N_DEV

ADAM_LR = 0.001
ADAM_B1 = 0.9
ADAM_B2 = 0.999
ADAM_EPS = 1e-08
ADAM_WD = 0.01
ADAM_STEP = 10

VMEM_LIMIT = 54 * 1024 * 1024

TS_PRENORM = 1024
TS_FWD = 512
TS_BWD = 256
TS_WGRAD = 2048
SUB_ROWS = 256


def _params(n_grid=1, vmem=VMEM_LIMIT):
    return pltpu.CompilerParams(dimension_semantics=("arbitrary",) * n_grid, vmem_limit_bytes=vmem)


def _full(shape):
    return pl.BlockSpec(shape, lambda *_: (0,) * len(shape))


def _resident(shape):
    return pl.BlockSpec(shape, lambda *_: (0,) * len(shape), pipeline_mode=pl.Buffered(1))


def _load_columns(blocks_ref, whole_ref, sems):
    copies = [pltpu.make_async_copy(blocks_ref.at[k], whole_ref.at[:, k * SHARD_IN:(k + 1) * SHARD_IN], sems.at[k])
              for k in range(N_DEV)]
    for cp in copies:
        cp.start()
    for cp in copies:
        cp.wait()


def _sigmoid(z):
    return jax.nn.sigmoid(z)


def _dsilu(z, s):
    return s * (1.0 + z * (1.0 - s))


def _dot(a, b):
    return jnp.dot(a, b, preferred_element_type=F32)


def _dot_t0(a, b):
    return lax.dot_general(a, b, (((0,), (0,)), ((), ())), preferred_element_type=F32)


def _dot_t1(a, b):
    return lax.dot_general(a, b, (((1,), (1,)), ((), ())), preferred_element_type=F32)


def _rms(x):
    return lax.rsqrt(jnp.mean(x * x, axis=-1, keepdims=True) + EPS)


def _shift_down(a, k):
    return pltpu.roll(a, k, 0)


def _shift_up(a, k):
    return pltpu.roll(a, a.shape[0] - k, 0)


def _mesh_position():
    return lax.axis_index("x"), lax.axis_index("y"), lax.axis_index("c")


def _index(pos):
    return 4 * pos[0] + 2 * pos[1] + pos[2]


def _peer(pos, d):
    x, y, c = pos
    return (1 - x if d & 4 else x, 1 - y if d & 2 else y, 1 - c if d & 1 else c)


def _remote(src, dst, send_sems, recv_sems, k, to):
    return pltpu.make_async_remote_copy(src_ref=src, dst_ref=dst, send_sem=send_sems.at[k], recv_sem=recv_sems.at[k],
                                        device_id=to, device_id_type=MESH)


class _Gather:
    def __init__(self, srcs, dsts, send_sems, recv_sems):
        x, y, c = _mesh_position()
        n = len(srcs)
        me, sibling = (x, y, c), (x, y, 1 - c)
        near = [(1 - x, y), (x, 1 - y)]
        far = (1 - x, 1 - y)
        x_first = c == 0
        relay_from = (jnp.where(x_first, 1 - x, x), jnp.where(x_first, y, 1 - y))
        relay_to = (jnp.where(x_first, x, 1 - x), jnp.where(x_first, 1 - y, y))

        def copy(k, t, block, to, src=None):
            slot = dsts[t].at[_index(block)]
            return _remote(slot if src is None else src, slot, send_sems, recv_sems, k * n + t, to)

        tensors = range(n)
        self.local = [pltpu.make_async_copy(srcs[t], dsts[t].at[_index(me)], recv_sems.at[7 * n + t]) for t in tensors]
        self.first = [copy(1 + j, t, me, chip + (c,), srcs[t]) for t in tensors for j, chip in enumerate(near)]
        self.first += [copy(0, t, me, sibling, srcs[t]) for t in tensors]
        self.near_in = [copy(1 + j, t, chip + (c,), me) for t in tensors for j, chip in enumerate(near)]
        self.near_passed = [copy(4 + j, t, chip + (c,), sibling) for t in tensors for j, chip in enumerate(near)]
        self.relayed = [copy(3, t, relay_from + (c,), relay_to + (c,)) for t in tensors]
        self.far_in = [copy(3, t, far + (c,), me) for t in tensors]
        self.far_passed = [copy(6, t, far + (c,), sibling) for t in tensors]
        self.d2d_in = [copy(0, t, sibling, me) for t in tensors]
        self.d2d_in += [copy(4 + j, t, chip + (1 - c,), me) for t in tensors for j, chip in enumerate(near + [far])]

    def start(self):
        for cp in self.local + self.first:
            cp.start()

    def relay(self):
        for landed, onward in zip(self.near_in, self.near_passed):
            landed.wait_recv()
            onward.start()
        for cp in self.relayed:
            cp.start()

    def middle(self):
        for landed, onward in zip(self.far_in, self.far_passed):
            landed.wait_recv()
            onward.start()

    def finish(self):
        for cp in self.d2d_in:
            cp.wait_recv()
        for cp in self.first + self.near_passed + self.relayed + self.far_passed:
            cp.wait_send()
        for cp in self.local:
            cp.wait()


class _Scatter:
    def __init__(self, srcs, dsts, send_sems, recv_sems):
        pos = _mesh_position()
        n = len(srcs)
        self.local = [pltpu.make_async_copy(srcs[t].at[_index(pos)], dsts[t].at[0], recv_sems.at[t]) for t in range(n)]
        self.remote = []
        for d in range(1, N_DEV):
            to = _peer(pos, d)
            self.remote += [_remote(srcs[t].at[_index(to)], dsts[t].at[d], send_sems, recv_sems, d * n + t, to)
                            for t in range(n)]

    def start(self):
        for cp in self.local + self.remote:
            cp.start()

    def relay(self):
        pass

    def middle(self):
        pass

    def finish(self):
        for cp in self.local:
            cp.wait()
        for cp in self.remote:
            cp.wait_recv()
        for cp in self.remote:
            cp.wait_send()


HBM_SPEC = pl.BlockSpec(memory_space=pltpu.HBM)
MIDDLE_STEPS_BEFORE_END = 4


def _landing_shape(exchange, src):
    return (N_DEV,) + src.shape if exchange is _Gather else src.shape


def _call(body, *, name, grid, in_specs, out_specs, out_shape, scratch_shapes, args, sides=()):
    params = _params(len(grid))
    if not sides:
        return pl.pallas_call(body, name=name, grid=grid, in_specs=in_specs, out_specs=out_specs, out_shape=out_shape,
                              scratch_shapes=scratch_shapes, compiler_params=params)(*args)
    n_in, n_out, n_scratch = len(in_specs), len(out_specs), len(scratch_shapes)
    counts = [len(srcs) for _, srcs in sides]
    ns = sum(counts)
    total = 1
    for g in grid:
        total *= g
    assert total // 2 < total - MIDDLE_STEPS_BEFORE_END
    side_args, side_shapes, side_sems = [], [], []
    for exchange, srcs in sides:
        side_args += list(srcs)
        side_shapes += [jax.ShapeDtypeStruct(_landing_shape(exchange, s), s.dtype) for s in srcs]
        side_sems += [pltpu.SemaphoreType.DMA((N_DEV * len(srcs),))] * 2

    def wrapped(*refs):
        ins, side_in = refs[:n_in], refs[n_in:n_in + ns]
        outs = refs[n_in + ns:n_in + ns + n_out]
        side_out = refs[n_in + ns + n_out:n_in + 2 * ns + n_out]
        rest = refs[n_in + 2 * ns + n_out:]
        scratch, sems = rest[:n_scratch], rest[n_scratch:]
        step = pl.program_id(0)
        for axis in range(1, len(grid)):
            step = step * grid[axis] + pl.program_id(axis)

        def exchanges():
            built, at = [], 0
            for k, (exchange, _) in enumerate(sides):
                built.append(exchange(side_in[at:at + counts[k]], side_out[at:at + counts[k]],
                                      sems[2 * k], sems[2 * k + 1]))
                at += counts[k]
            return built

        @pl.when(step == 0)
        def _():
            for ex in exchanges():
                ex.start()

        @pl.when(step == total // 2)
        def _():
            for ex in exchanges():
                ex.relay()

        @pl.when(step == total - MIDDLE_STEPS_BEFORE_END)
        def _():
            for ex in exchanges():
                ex.middle()

        body(*ins, *outs, *scratch)

        @pl.when(step == total - 1)
        def _():
            for ex in exchanges():
                ex.finish()

    return pl.pallas_call(
        wrapped, name=name, grid=grid, in_specs=list(in_specs) + [HBM_SPEC] * ns,
        out_specs=list(out_specs) + [HBM_SPEC] * ns, out_shape=list(out_shape) + side_shapes,
        scratch_shapes=list(scratch_shapes) + side_sems, compiler_params=params)(*args, *side_args)


def _prenorm_gather(x, g, wie, small, woe, wio, woo, ts):
    S = x.shape[0]
    nt = S // ts

    def body(x_ref, g_ref, wie_ref, sm_ref, woe_ref, wio_ref, woo_ref,
             h_ref, wie_out, sm_out, woe_out, wio_b, woo_b, cast_ref, small_ref, cast2_ref, send_sems, recv_sems):
        step = pl.program_id(0)
        gather = lambda: _Gather([cast_ref, small_ref, cast2_ref], [wie_out, sm_out, woe_out], send_sems, recv_sems)

        @pl.when(step == 0)
        def _():
            cast_ref[...] = wie_ref[...].astype(BF16)
            small_ref[...] = sm_ref[...]
            cast2_ref[...] = woe_ref[...].astype(BF16)
            gather().start()
            wio_b[...] = wio_ref[...].astype(BF16)
            woo_b[...] = woo_ref[...].astype(BF16)

        xv = x_ref[...]
        h_ref[...] = ((xv * _rms(xv)) * g_ref[...]).astype(BF16)

        @pl.when(step == nt // 2)
        def _():
            gather().relay()

        @pl.when(step == nt - 1)
        def _():
            exchange = gather()
            exchange.middle()
            exchange.finish()

    tile = pl.BlockSpec((ts, D), lambda i: (i, 0))
    sems = pltpu.SemaphoreType.DMA((N_DEV * 3,))
    shards = [wie, small, woe, wio, woo]
    return pl.pallas_call(
        body, name="prenorm_gather", grid=(nt,),
        in_specs=[tile, _full((1, D))] + [_full(a.shape) for a in shards],
        out_specs=[tile, HBM_SPEC, HBM_SPEC, HBM_SPEC] + [_full(a.shape) for a in shards[3:]],
        out_shape=[jax.ShapeDtypeStruct((S, D), BF16), jax.ShapeDtypeStruct((N_DEV,) + wie.shape, BF16),
                   jax.ShapeDtypeStruct((N_DEV,) + small.shape, F32), jax.ShapeDtypeStruct((N_DEV,) + woe.shape, BF16)]
        + [jax.ShapeDtypeStruct(a.shape, BF16) for a in shards[3:]],
        scratch_shapes=[pltpu.VMEM(wie.shape, BF16), pltpu.VMEM(small.shape, F32), pltpu.VMEM(woe.shape, BF16),
                        sems, sems],
        compiler_params=_params(),
    )(x, g, wie, small, woe, wio, woo)


def _even_mixer(proj, halo, row0, cw_ref, pw_ref):
    xa, gb, gc, za, xp, zp = proj
    nr = xa.shape[0]
    ha = gc * xa
    ha_ext = jnp.concatenate([halo[0], ha], axis=0)
    ha_m1 = _shift_down(ha_ext, 1)[HALO:]
    ha_m2 = _shift_down(ha_ext, 2)[HALO:]
    conv = cw_ref[2:3, :] * ha + cw_ref[1:2, :] * ha_m1 + cw_ref[0:1, :] * ha_m2
    sig_a = _sigmoid(za)
    silu_a = za * sig_a

    xp_ext = jnp.concatenate([halo[1], xp], axis=0)
    pos = row0 + lax.broadcasted_iota(jnp.int32, (nr, 1), 0)
    pooled, inv_cnt, mixed = [], [], []
    for g, w in enumerate(POOL_WINDOWS):
        cols = slice(g * GROUP, (g + 1) * GROUP)
        s = xp_ext[:, cols]
        k = 1
        while k < w:
            s = s + _shift_down(s, k)
            k *= 2
        inv = 1.0 / jnp.minimum(pos + 1, w).astype(F32)
        pg = s[HALO:] * inv - xp[:, cols]
        pooled.append(pg)
        inv_cnt.append(inv)
        mixed.append(_dot(pg.astype(BF16), pw_ref[g]))
    mixed = jnp.concatenate(mixed, axis=1)
    sig_b = _sigmoid(zp)
    silu_b = zp * sig_b
    return dict(xa=xa, gb=gb, gc=gc, za=za, xp=xp, zp=zp, ha=ha, ha_m1=ha_m1, ha_m2=ha_m2, conv=conv,
                sig_a=sig_a, silu_a=silu_a, pooled=pooled, inv_cnt=inv_cnt, mixed=mixed, sig_b=sig_b,
                silu_b=silu_b)


def _columns(a):
    return [a[:, k * A_W:(k + 1) * A_W].astype(F32) for k in range(6)]


def _even_forward_fused(h, x, win_b, cw, pw_b, ps, wout_b, gpost, gpre_next, ts, sides=()):
    S = x.shape[0]

    def body(h_ref, x_ref, wi_blocks, cw_ref, pw_ref, ps_ref, wo_ref, gp_ref, gn_ref,
             p_ref, m_ref, x1_ref, h1_ref, mix_ref, carry_ref, wi_ref, wi_sems):
        i = pl.program_id(0)

        @pl.when(i == 0)
        def _():
            carry_ref[...] = jnp.zeros_like(carry_ref)
            _load_columns(wi_blocks, wi_ref, wi_sems)

        for r0 in range(0, ts, SUB_ROWS):
            rows = slice(r0, r0 + SUB_ROWS)
            proj = _dot(h_ref[rows, :], wi_ref[...])
            p_ref[rows, :] = proj.astype(BF16)
            fw = _even_mixer(_columns(proj), (carry_ref[:, 0:A_W], carry_ref[:, A_W:D]), i * ts + r0, cw_ref, pw_ref)
            carry_ref[:, 0:A_W] = fw["ha"][SUB_ROWS - HALO:]
            carry_ref[:, A_W:D] = fw["xp"][SUB_ROWS - HALO:]
            mix_ref[rows, 0:A_W] = (fw["gb"] * fw["conv"] * fw["silu_a"]).astype(BF16)
            mix_ref[rows, A_W:D] = (fw["mixed"] * ps_ref[...] * fw["silu_b"]).astype(BF16)
            m = _dot(mix_ref[rows, :], wo_ref[...])
            m_ref[rows, :] = m.astype(BF16)
            x1 = x_ref[rows, :] + (m * _rms(m)) * gp_ref[...]
            x1_ref[rows, :] = x1
            h1_ref[rows, :] = ((x1 * _rms(x1)) * gn_ref[...]).astype(BF16)

    tile = pl.BlockSpec((ts, D), lambda i: (i, 0))
    return _call(
        body, name="even_forward", grid=(S // ts,),
        in_specs=[tile, tile, HBM_SPEC, _full((3, A_W)), _full((N_GROUPS, GROUP, GROUP)), _full((1, B_W)),
                  _resident((D, D)), _full((1, D)), _full((1, D))],
        out_specs=[pl.BlockSpec((ts, PROJ), lambda i: (i, 0)), tile, tile, tile],
        out_shape=[jax.ShapeDtypeStruct((S, PROJ), BF16), jax.ShapeDtypeStruct((S, D), BF16),
                   jax.ShapeDtypeStruct((S, D), F32), jax.ShapeDtypeStruct((S, D), BF16)],
        scratch_shapes=[pltpu.VMEM((ts, D), BF16), pltpu.VMEM((HALO, D), F32), pltpu.VMEM((D, PROJ), BF16),
                        pltpu.SemaphoreType.DMA((N_DEV,))],
        args=(h, x, win_b, cw, pw_b, ps, wout_b, gpost, gpre_next), sides=sides)


def _even_backward_fused(g1, m0, p, x, cw, pw_b, pwt_b, ps, woutt_b, wint_b, gpost, gpre, ts):
    S = g1.shape[0]
    nt = S // ts

    def body(g_ref, m_ref, p_ref, halo_ref, x_ref, cw_ref, pw_ref, pwt_ref, ps_ref, wot_ref, wit_ref, gp_ref, gn_ref,
             dp_ref, dx_ref, dwob_ref, dcw_ref, dpw_ref, dps_ref, dgp_ref, dgn_ref,
             carry_ref, mix_ref, dm_ref, dwo_ref):
        step = pl.program_id(0)
        i = nt - 1 - step

        @pl.when(step == 0)
        def _():
            carry_ref[...] = jnp.zeros_like(carry_ref)
            dwo_ref[...] = jnp.zeros_like(dwo_ref)
            dcw_ref[...] = jnp.zeros_like(dcw_ref)
            dpw_ref[...] = jnp.zeros_like(dpw_ref)
            dps_ref[...] = jnp.zeros_like(dps_ref)
            dgp_ref[...] = jnp.zeros_like(dgp_ref)
            dgn_ref[...] = jnp.zeros_like(dgn_ref)

        ps_v = ps_ref[...]
        sums = [None] * 6
        add = lambda k, part: sums.__setitem__(k, part if sums[k] is None else sums[k] + part)
        for r0 in range(ts - SUB_ROWS, -1, -SUB_ROWS):
            rows = slice(r0, r0 + SUB_ROWS)
            g = g_ref[rows, :]
            m = m_ref[rows, :].astype(F32)
            q = _rms(m)
            n = m * q
            add(0, jnp.sum(g * n, axis=0, keepdims=True))
            dn = g * gp_ref[...]
            dm = q * (dn - n * jnp.mean(dn * n, axis=-1, keepdims=True))
            dm_ref[rows, :] = dm.astype(BF16)
            dmix = _dot(dm_ref[rows, :], wot_ref[...])
            dya = dmix[:, 0:A_W]
            dyb = dmix[:, A_W:D]

            if r0 == 0:
                before = _columns(halo_ref[...])
                keep = jnp.where(i == 0, 0.0, 1.0).astype(F32)
                halo = (before[2] * before[0] * keep, before[4] * keep)
            else:
                before = _columns(p_ref[r0 - HALO:r0, :])
                halo = (before[2] * before[0], before[4])
            fw = _even_mixer(_columns(p_ref[rows, :]), halo, i * ts + r0, cw_ref, pw_ref)
            mix_ref[rows, 0:A_W] = (fw["gb"] * fw["conv"] * fw["silu_a"]).astype(BF16)
            mix_ref[rows, A_W:D] = (fw["mixed"] * ps_v * fw["silu_b"]).astype(BF16)

            t = dya * fw["gb"]
            dconv = t * fw["silu_a"]
            dgb = dya * fw["conv"] * fw["silu_a"]
            dza = t * fw["conv"] * _dsilu(fw["za"], fw["sig_a"])
            add(2, jnp.sum(dconv * fw["ha"], axis=0, keepdims=True))
            add(3, jnp.sum(dconv * fw["ha_m1"], axis=0, keepdims=True))
            add(4, jnp.sum(dconv * fw["ha_m2"], axis=0, keepdims=True))
            dconv_ext = jnp.concatenate([dconv, carry_ref[:, 0:A_W]], axis=0)
            dha = (cw_ref[2:3, :] * dconv + cw_ref[1:2, :] * _shift_up(dconv_ext, 1)[:SUB_ROWS]
                   + cw_ref[0:1, :] * _shift_up(dconv_ext, 2)[:SUB_ROWS])
            dgc = dha * fw["xa"]
            dxa = dha * fw["gc"]

            u = dyb * fw["mixed"]
            add(1, jnp.sum(u * fw["silu_b"], axis=0, keepdims=True))
            dzp = u * ps_v * _dsilu(fw["zp"], fw["sig_b"])
            dmixed = (dyb * ps_v * fw["silu_b"]).astype(BF16)
            dxp, e_first = [], []
            for gi, w in enumerate(POOL_WINDOWS):
                cols = slice(gi * GROUP, (gi + 1) * GROUP)
                dmg = dmixed[:, cols]
                dpooled = _dot(dmg, pwt_ref[gi])
                dpw_ref[gi] += _dot_t0(fw["pooled"][gi].astype(BF16), dmg)
                e = dpooled * fw["inv_cnt"][gi]
                e_first.append(e[0:HALO])
                s = jnp.concatenate([e, carry_ref[:, A_W + gi * GROUP:A_W + (gi + 1) * GROUP]], axis=0)
                k = 1
                while k < w:
                    s = s + _shift_up(s, k)
                    k *= 2
                dxp.append(s[:SUB_ROWS] - dpooled)
            carry_ref[:, 0:A_W] = dconv[0:HALO]
            carry_ref[:, A_W:D] = jnp.concatenate(e_first, axis=1)

            dp_ref[rows, 0:512] = dxa.astype(BF16)
            dp_ref[rows, 512:1024] = dgb.astype(BF16)
            dp_ref[rows, 1024:1536] = dgc.astype(BF16)
            dp_ref[rows, 1536:2048] = dza.astype(BF16)
            dp_ref[rows, 2048:2560] = jnp.concatenate(dxp, axis=1).astype(BF16)
            dp_ref[rows, 2560:3072] = dzp.astype(BF16)

            dh = _dot(dp_ref[rows, :], wit_ref[...])
            xv = x_ref[rows, :]
            r = _rms(xv)
            xn = xv * r
            add(5, jnp.sum(dh * xn, axis=0, keepdims=True))
            dxn = dh * gn_ref[...]
            dx_ref[rows, :] = g + r * (dxn - xn * jnp.mean(dxn * xn, axis=-1, keepdims=True))

        dwo_ref[...] += _dot_t0(mix_ref[...], dm_ref[...])
        dgp_ref[...] += sums[0]
        dps_ref[...] += sums[1]
        dcw_ref[2:3, :] += sums[2]
        dcw_ref[1:2, :] += sums[3]
        dcw_ref[0:1, :] += sums[4]
        dgn_ref[...] += sums[5]

        @pl.when(step == nt - 1)
        def _():
            dwob_ref[...] = dwo_ref[...].astype(BF16)

    rev = lambda s: (nt - 1 - s, 0)
    tile = pl.BlockSpec((ts, D), rev)
    wide = pl.BlockSpec((ts, PROJ), rev)
    bpt = ts // HALO
    halo_map = lambda s: (jnp.maximum((nt - 1 - s) * bpt - 1, 0), 0)
    groups = _full((N_GROUPS, GROUP, GROUP))
    vec = _full((1, D))
    return pl.pallas_call(
        body, name="even_backward", grid=(nt,),
        in_specs=[tile, tile, wide, pl.BlockSpec((HALO, PROJ), halo_map), tile, _full((3, A_W)), groups, groups,
                  _full((1, B_W)), _resident((D, D)), _resident((PROJ, D)), vec, vec],
        out_specs=[wide, tile, _full((D, D)), _full((8, A_W)), groups, _full((1, B_W)), vec, vec],
        out_shape=[jax.ShapeDtypeStruct((S, PROJ), BF16), jax.ShapeDtypeStruct((S, D), F32),
                   jax.ShapeDtypeStruct((D, D), BF16), jax.ShapeDtypeStruct((8, A_W), F32),
                   jax.ShapeDtypeStruct((N_GROUPS, GROUP, GROUP), F32), jax.ShapeDtypeStruct((1, B_W), F32),
                   jax.ShapeDtypeStruct((1, D), F32), jax.ShapeDtypeStruct((1, D), F32)],
        scratch_shapes=[pltpu.VMEM((HALO, D), F32), pltpu.VMEM((ts, D), BF16), pltpu.VMEM((ts, D), BF16),
                        pltpu.VMEM((D, D), F32)],
        compiler_params=_params(),
    )(g1, m0, p, p, x, cw, pw_b, pwt_b, ps, woutt_b, wint_b, gpost, gpre)


def _gate_matmuls(w_ref, in_ref, out_ref, chunks, bias_ref=None):
    for h in range(N_HEADS):
        cols = slice(h * GROUP, (h + 1) * GROUP)
        wide = jnp.concatenate([in_ref[chunk, cols] for chunk in chunks], axis=1)
        res = _dot(w_ref[h], wide)
        for k, chunk in enumerate(chunks):
            part = res[:, k * GROUP:(k + 1) * GROUP]
            out_ref[chunk, cols] = part if bias_ref is None else part + bias_ref[h]


def _odd_forward_parts(v, lg_ref, lb_ref, wt_ref, bfull_ref, vln_ref, sv_ref, r0, nr):
    rows = slice(r0, r0 + nr)
    mu = jnp.mean(v, axis=-1, keepdims=True)
    vc = v - mu
    rstd = lax.rsqrt(jnp.mean(vc * vc, axis=-1, keepdims=True) + EPS)
    vh = vc * rstd
    vln_ref[rows, :] = (vh * lg_ref[...] + lb_ref[...]).astype(BF16)
    _gate_matmuls(wt_ref, vln_ref, sv_ref, [slice(r0 + k * GROUP, r0 + (k + 1) * GROUP) for k in range(nr // GROUP)],
                  bfull_ref)
    return vh, rstd


def _odd_forward(h, x1, tgt, win_b, lg, lb, wt_b, bfull, wout_b, gpost, ts):
    S = x1.shape[0]

    def body(h_ref, x_ref, t_ref, wi_blocks, lg_ref, lb_ref, wt_ref, bfull_ref, wo_ref, gp_ref,
             p_ref, m_ref, g_ref, loss_ref, vln_ref, sv_ref, y_ref, wi_ref, wi_sems):
        @pl.when(pl.program_id(0) == 0)
        def _():
            loss_ref[...] = jnp.zeros_like(loss_ref)
            _load_columns(wi_blocks, wi_ref, wi_sems)

        loss = None
        for r0 in range(0, ts, SUB_ROWS):
            rows = slice(r0, r0 + SUB_ROWS)
            proj = _dot(h_ref[rows, :], wi_ref[...])
            p_ref[rows, :] = proj.astype(BF16)
            u, v, z = proj[:, 0:D], proj[:, D:2 * D], proj[:, 2 * D:3 * D]
            _odd_forward_parts(v, lg_ref, lb_ref, wt_ref, bfull_ref, vln_ref, sv_ref, r0, SUB_ROWS)
            y_ref[rows, :] = (u * sv_ref[rows, :] * (z * _sigmoid(z))).astype(BF16)
            m = _dot(y_ref[rows, :], wo_ref[...])
            m_ref[rows, :] = m.astype(BF16)
            x2 = x_ref[rows, :] + (m * _rms(m)) * gp_ref[...]
            err = x2 - t_ref[rows, :]
            g_ref[rows, :] = (err * (1.0 / D)).astype(BF16)
            part = jnp.sum(err * err, axis=0, keepdims=True)
            loss = part if loss is None else loss + part
        loss_ref[...] += loss

    tile = pl.BlockSpec((ts, D), lambda i: (i, 0))
    small = _full((N_HEADS, GROUP, GROUP))
    return pl.pallas_call(
        body, name="odd_forward", grid=(S // ts,),
        in_specs=[pl.BlockSpec((ts, D), lambda i: (i, 0)), tile, tile, HBM_SPEC, _full((1, D)), _full((1, D)),
                  small, small, _full((D, D)), _full((1, D))],
        out_specs=[pl.BlockSpec((ts, PROJ), lambda i: (i, 0)), tile, tile, _full((1, D))],
        out_shape=[jax.ShapeDtypeStruct((S, PROJ), BF16), jax.ShapeDtypeStruct((S, D), BF16),
                   jax.ShapeDtypeStruct((S, D), BF16), jax.ShapeDtypeStruct((1, D), F32)],
        scratch_shapes=[pltpu.VMEM((ts, D), BF16), pltpu.VMEM((ts, D), F32), pltpu.VMEM((ts, D), BF16),
                        pltpu.VMEM((D, PROJ), BF16), pltpu.SemaphoreType.DMA((N_DEV,))],
        compiler_params=_params(),
    )(h, x1, tgt, win_b, lg, lb, wt_b, bfull, wout_b, gpost)


def _odd_backward(g2, m1, p, x1, lg, lb, wt_b, wtt_b, bfull, woutt_b, wint_b, gpost, gpre, ts):
    S = g2.shape[0]
    nt = S // ts
    sub = ts // 2

    def body(g_ref, m_ref, p_ref, x_ref, lg_ref, lb_ref, wt_ref, wtt_ref, bfull_ref, wot_ref, wit_ref, gp_ref, gn_ref,
             dp_ref, dx_ref, dwob_ref, dws_ref, dbs_ref, dlg_ref, dlb_ref, dgp_ref, dgn_ref,
             vln_ref, sv_ref, dsvb_ref, dvln_ref, dsum_ref, y_ref, dm_ref, dwo_ref):
        step = pl.program_id(0)

        @pl.when(step == 0)
        def _():
            dwo_ref[...] = jnp.zeros_like(dwo_ref)
            dws_ref[...] = jnp.zeros_like(dws_ref)
            dsum_ref[...] = jnp.zeros_like(dsum_ref)
            dlg_ref[...] = jnp.zeros_like(dlg_ref)
            dlb_ref[...] = jnp.zeros_like(dlb_ref)
            dgp_ref[...] = jnp.zeros_like(dgp_ref)
            dgn_ref[...] = jnp.zeros_like(dgn_ref)

        tril = (lax.broadcasted_iota(jnp.int32, (GROUP, GROUP), 0)
                >= lax.broadcasted_iota(jnp.int32, (GROUP, GROUP), 1))
        sums = [None] * 4
        add = lambda k, part: sums.__setitem__(k, part if sums[k] is None else sums[k] + part)

        def post_norm(c):
            g = g_ref[c["rows"], :].astype(F32)
            m = m_ref[c["rows"], :].astype(F32)
            q = _rms(m)
            n = m * q
            add(0, jnp.sum(g * n, axis=0, keepdims=True))
            dn = g * gp_ref[...]
            dm = q * (dn - n * jnp.mean(dn * n, axis=-1, keepdims=True))
            dm_ref[c["rows"], :] = dm.astype(BF16)

        def out_projection(c):
            c["dy"] = _dot(dm_ref[c["rows"], :], wot_ref[...])

        def layernorm(c):
            v = p_ref[c["rows"], D:2 * D].astype(F32)
            mu = jnp.mean(v, axis=-1, keepdims=True)
            vc = v - mu
            c["rstd"] = lax.rsqrt(jnp.mean(vc * vc, axis=-1, keepdims=True) + EPS)
            c["vh"] = vc * c["rstd"]
            vln_ref[c["rows"], :] = (c["vh"] * lg_ref[...] + lb_ref[...]).astype(BF16)

        def gate_matmuls(c):
            _gate_matmuls(wt_ref, vln_ref, sv_ref, c["chunks"], bfull_ref)

        def gating(c):
            rows = c["rows"]
            u = p_ref[rows, 0:D].astype(F32)
            z = p_ref[rows, 2 * D:3 * D].astype(F32)
            sv = sv_ref[rows, :]
            sig = _sigmoid(z)
            sz = z * sig
            y_ref[rows, :] = (u * sv * sz).astype(BF16)
            dy = c.pop("dy")
            t = dy * sz
            dsv = t * u
            dsvb_ref[rows, :] = dsv.astype(BF16)
            for k in range(sub // GROUP):
                dsum_ref[...] += dsv[k * GROUP:(k + 1) * GROUP]
            dp_ref[rows, 0:D] = (t * sv).astype(BF16)
            dp_ref[rows, 2 * D:3 * D] = (dy * u * sv * _dsilu(z, sig)).astype(BF16)

        def gate_backward_matmuls(c):
            for chunk in c["chunks"]:
                for h in range(N_HEADS):
                    cols = slice(h * GROUP, (h + 1) * GROUP)
                    dvln_ref[chunk, cols] = _dot(wtt_ref[h], dsvb_ref[chunk, cols])
                    dws_ref[h] += jnp.where(tril, _dot_t1(dsvb_ref[chunk, cols], vln_ref[chunk, cols]), 0.0)

        def layernorm_backward(c):
            vh, rstd = c.pop("vh"), c.pop("rstd")
            dvln = dvln_ref[c["rows"], :]
            add(1, jnp.sum(dvln * vh, axis=0, keepdims=True))
            add(2, jnp.sum(dvln, axis=0, keepdims=True))
            dvh = dvln * lg_ref[...]
            dv = rstd * (dvh - jnp.mean(dvh, axis=-1, keepdims=True)
                         - vh * jnp.mean(dvh * vh, axis=-1, keepdims=True))
            dp_ref[c["rows"], D:2 * D] = dv.astype(BF16)

        def in_projection(c):
            c["dh"] = _dot(dp_ref[c["rows"], :], wit_ref[...])

        def pre_norm(c):
            dh = c.pop("dh")
            xv = x_ref[c["rows"], :]
            r = _rms(xv)
            xn = xv * r
            add(3, jnp.sum(dh * xn, axis=0, keepdims=True))
            dxn = dh * gn_ref[...]
            dx_ref[c["rows"], :] = (g_ref[c["rows"], :].astype(F32)
                                    + r * (dxn - xn * jnp.mean(dxn * xn, axis=-1, keepdims=True)))

        phases = [post_norm, out_projection, layernorm, gate_matmuls, gating, gate_backward_matmuls,
                  layernorm_backward, in_projection, pre_norm]
        groups = [dict(rows=slice(r0, r0 + sub),
                       chunks=[slice(r0 + k * GROUP, r0 + (k + 1) * GROUP) for k in range(sub // GROUP)])
                  for r0 in range(0, ts, sub)]
        for group in groups:
            for phase in phases:
                phase(group)

        dwo_ref[...] += _dot_t0(y_ref[...], dm_ref[...])
        dgp_ref[...] += sums[0]
        dlg_ref[...] += sums[1]
        dlb_ref[...] += sums[2]
        dgn_ref[...] += sums[3]

        @pl.when(step == nt - 1)
        def _():
            dwob_ref[...] = dwo_ref[...].astype(BF16)
            ones = jnp.ones((8, GROUP), F32)
            for h in range(N_HEADS):
                cols = slice(h * GROUP, (h + 1) * GROUP)
                total = lax.dot_general(ones, dsum_ref[:, cols], (((1,), (1,)), ((), ())),
                                        precision=lax.Precision.HIGHEST, preferred_element_type=F32)
                dbs_ref[h:h + 1, :] = total[0:1, :]

    tile = pl.BlockSpec((ts, D), lambda i: (i, 0))
    wide = pl.BlockSpec((ts, PROJ), lambda i: (i, 0))
    small = _full((N_HEADS, GROUP, GROUP))
    heads = _resident((N_HEADS, GROUP, GROUP))
    vec = _full((1, D))
    return pl.pallas_call(
        body, name="odd_backward", grid=(nt,),
        in_specs=[tile, tile, wide, tile, vec, vec, heads, heads, heads, _resident((D, D)), _resident((PROJ, D)),
                  vec, vec],
        out_specs=[wide, tile, _full((D, D)), small, _full((N_HEADS, GROUP)), vec, vec, vec, vec],
        out_shape=[jax.ShapeDtypeStruct((S, PROJ), BF16), jax.ShapeDtypeStruct((S, D), F32),
                   jax.ShapeDtypeStruct((D, D), BF16),
                   jax.ShapeDtypeStruct((N_HEADS, GROUP, GROUP), F32), jax.ShapeDtypeStruct((N_HEADS, GROUP), F32),
                   jax.ShapeDtypeStruct((1, D), F32), jax.ShapeDtypeStruct((1, D), F32),
                   jax.ShapeDtypeStruct((1, D), F32), jax.ShapeDtypeStruct((1, D), F32)],
        scratch_shapes=[pltpu.VMEM((ts, D), BF16), pltpu.VMEM((ts, D), F32), pltpu.VMEM((ts, D), BF16),
                        pltpu.VMEM((ts, D), F32), pltpu.VMEM((GROUP, D), F32), pltpu.VMEM((ts, D), BF16),
                        pltpu.VMEM((ts, D), BF16), pltpu.VMEM((D, D), F32)],
        compiler_params=_params(),
    )(g2, m1, p, x1, lg, lb, wt_b, wtt_b, bfull, woutt_b, wint_b, gpost, gpre)


def _weight_grad_scatter(h, dp, ts, name, sides=()):
    S = h.shape[0]
    ts = min(ts, S)
    nt = S // ts
    panel = 2 * SHARD_IN
    last = N_CHIPS - 1

    def body(h_ref, dp_ref, parts_ref, acc_ref, half_ref, swap_ref, sum_ref, send_sems, recv_sems):
        s, t = pl.program_id(0), pl.program_id(1)
        x, y, c = _mesh_position()
        sibling = (x, y, 1 - c)

        def swap(k):
            return _remote(half_ref.at[k, 1], swap_ref.at[k], send_sems, recv_sems, k, sibling)

        def chip_sum(k):
            flip = last - k
            to = (x ^ (flip >> 1), y ^ (flip & 1), c)
            return _remote(sum_ref.at[k], parts_ref.at[1 + flip], send_sems, recv_sems, 3 + flip, to)

        own = pltpu.make_async_copy(half_ref.at[last, 0], parts_ref.at[0], recv_sems.at[7])
        to_sibling = _remote(half_ref.at[last, 1], parts_ref.at[1], send_sems, recv_sems, 3, sibling)

        @pl.when(t == 0)
        def _():
            acc_ref[...] = jnp.zeros_like(acc_ref)

        for k in range(last):
            @pl.when((s == k + 1) & (t == 0))
            def _():
                swap(k).wait_recv()
                sum_ref[k] = (half_ref[k, 0].astype(F32) + swap_ref[k].astype(F32)).astype(BF16)
                chip_sum(k).start()

        acc_ref[...] += _dot_t0(h_ref[...], dp_ref[...])

        @pl.when(t == nt - 1)
        def _():
            for core in range(2):
                @pl.when(c == core)
                def _():
                    half_ref[s, core] = acc_ref[:, 0:SHARD_IN].astype(BF16)
                    half_ref[s, 1 - core] = acc_ref[:, SHARD_IN:panel].astype(BF16)
            for k in range(last):
                @pl.when(s == k)
                def _():
                    swap(k).start()

        @pl.when((s == last) & (t == nt - 1))
        def _():
            own.start()
            to_sibling.start()
            own.wait()
            _remote(half_ref.at[last, 1], parts_ref.at[1], send_sems, recv_sems, 3, sibling).wait_recv()
            for k in range(last):
                chip_sum(k).wait_recv()
            to_sibling.wait_send()
            for k in range(last):
                swap(k).wait_send()
                chip_sum(k).wait_send()

    chip_panel = lambda s, t: (t, (2 * lax.axis_index("x") + lax.axis_index("y")) ^ (last - s))
    sems = pltpu.SemaphoreType.DMA((8,))
    return _call(
        body, name=name, grid=(N_CHIPS, nt),
        in_specs=[pl.BlockSpec((ts, D), lambda s, t: (t, 0)), pl.BlockSpec((ts, panel), chip_panel)],
        out_specs=[HBM_SPEC], out_shape=[jax.ShapeDtypeStruct((N_CHIPS + 1, D, SHARD_IN), BF16)],
        scratch_shapes=[pltpu.VMEM((D, panel), F32), pltpu.VMEM((N_CHIPS, 2, D, SHARD_IN), BF16),
                        pltpu.VMEM((last, D, SHARD_IN), BF16), pltpu.VMEM((last, D, SHARD_IN), BF16), sems, sems],
        args=(h, dp), sides=sides)


def _adamw(w, g, m, v):
    m = ADAM_B1 * m + (1.0 - ADAM_B1) * g
    v = ADAM_B2 * v + (1.0 - ADAM_B2) * (g * g)
    m_hat = m / (1.0 - ADAM_B1 ** ADAM_STEP)
    v_hat = v / (1.0 - ADAM_B2 ** ADAM_STEP)
    delta = -ADAM_LR * (m_hat / (jnp.sqrt(v_hat) + ADAM_EPS) + ADAM_WD * w)
    return delta, m, v


UPDATE_STEPS = 4


def _sum_update(parts, weights):
    n_w = len(weights)

    def body(*refs):
        p_refs, wmv_refs, outs = refs[:n_w], refs[n_w:4 * n_w], refs[4 * n_w:]
        for i in range(n_w):
            acc = p_refs[i][0].astype(F32)
            for d in range(1, parts[i].shape[0]):
                acc = acc + p_refs[i][d].astype(F32)
            w_ref, m_ref, v_ref = wmv_refs[3 * i:3 * i + 3]
            g_out, d_out, m_out, v_out = outs[4 * i:4 * i + 4]
            g_out[...] = acc
            delta, m_new, v_new = _adamw(w_ref[...], acc, m_ref[...], v_ref[...])
            d_out[...] = delta
            m_out[...] = m_new
            v_out[...] = v_new

    part_specs, wmv_specs, out_specs, out_shape = [], [], [], []
    for p, (w, _, _) in zip(parts, weights):
        R, C = w.shape
        rb = R // UPDATE_STEPS
        blk = pl.BlockSpec((rb, C), lambda i: (i, 0))
        part_specs.append(pl.BlockSpec((p.shape[0], rb, C), lambda i: (0, i, 0)))
        wmv_specs += [blk] * 3
        out_specs += [blk] * 4
        out_shape += [jax.ShapeDtypeStruct((R, C), F32)] * 4
    flat = pl.pallas_call(
        body, name="update_large_weights", grid=(UPDATE_STEPS,), in_specs=part_specs + wmv_specs,
        out_specs=out_specs, out_shape=out_shape, compiler_params=_params(),
    )(*parts, *[a for wmv in weights for a in wmv])
    return [flat[4 * i:4 * i + 4] for i in range(n_w)]


def _small_sum_update(partials, row_of, weights, loss_parts):
    n_p, n_w = len(partials), len(weights)

    def body(*refs):
        p_refs = refs[:n_p]
        loss_ref = refs[n_p]
        w_refs = refs[n_p + 1:n_p + 1 + 3 * n_w]
        loss_out = refs[n_p + 1 + 3 * n_w]
        outs = refs[n_p + 2 + 3 * n_w:]

        def total(ref):
            acc = ref[0].astype(F32)
            for d in range(1, N_DEV):
                acc = acc + ref[d].astype(F32)
            return acc

        lsum = jnp.sum(total(loss_ref), axis=-1, keepdims=True) * (0.5 / D)
        loss_out[...] = jnp.broadcast_to(lsum, loss_out.shape)
        for k in range(n_p):
            i, r0 = row_of[k]
            g = total(p_refs[k])
            w_ref, m_ref, v_ref = w_refs[3 * i:3 * i + 3]
            g_out, d_out, m_out, v_out = outs[4 * i:4 * i + 4]
            if g.ndim == 2:
                at = (slice(r0, r0 + g.shape[0]), slice(None))
            else:
                at = (slice(None),) * g.ndim
            delta, m_new, v_new = _adamw(w_ref[at], g, m_ref[at], v_ref[at])
            g_out[at] = g
            d_out[at] = delta
            m_out[at] = m_new
            v_out[at] = v_new

    vm = pl.BlockSpec(memory_space=pltpu.VMEM)
    flat_w = [a for wmv in weights for a in wmv]
    out_shape = [jax.ShapeDtypeStruct((1, 128), F32)]
    for w, _, _ in weights:
        out_shape += [jax.ShapeDtypeStruct(w.shape, F32)] * 4
    return pl.pallas_call(
        body, name="small_update", in_specs=[vm] * (n_p + 1 + 3 * n_w), out_specs=[vm] * len(out_shape),
        out_shape=out_shape, compiler_params=pltpu.CompilerParams(vmem_limit_bytes=VMEM_LIMIT),
    )(*partials, loss_parts, *flat_w)


def _pack_small_shard(cw, lg, lb):
    return (jnp.pad(cw, ((0, 13), (0, 64))) + jnp.pad(lg, ((8, 7), (0, 0))) + jnp.pad(lb, ((9, 6), (0, 0))))


def kernel(x, pre_norm, post_norm, even_w_in, even_conv_w, even_pool_w, even_pool_scale, even_w_out, odd_w_in, odd_ln_g, odd_ln_b, odd_w_s, odd_b_s, odd_w_out, loss_target, m_pre_norm, m_post_norm, m_even_w_in, m_even_conv_w, m_even_pool_w, m_even_pool_scale, m_even_w_out, m_odd_w_in, m_odd_ln_g, m_odd_ln_b, m_odd_w_s, m_odd_b_s, m_odd_w_out, v_pre_norm, v_post_norm, v_even_w_in, v_even_conv_w, v_even_pool_w, v_even_pool_scale, v_even_w_out, v_odd_w_in, v_odd_ln_g, v_odd_ln_b, v_odd_w_s, v_odd_b_s, v_odd_w_out):
    S = x.shape[1]
    xs = x.reshape(S, D)
    tgt = loss_target.reshape(S, D)

    gpre0, gpre1 = pre_norm[0:1], pre_norm[1:2]
    gpost0, gpost1 = post_norm[0:1], post_norm[1:2]
    small_shard = _pack_small_shard(even_conv_w[0], odd_ln_g, odd_ln_b)
    h0, wie_g, small_g, woe_g, wio_b, woo_b = _prenorm_gather(xs, gpre0, even_w_in[0], small_shard, even_w_out[0],
                                                              odd_w_in[0], odd_w_out[0], TS_PRENORM)
    wie_t = jnp.transpose(wie_g, (0, 2, 1)).reshape(PROJ, D)
    woe = woe_g.reshape(D, D)
    woe_t = woe.T
    conv_w = jnp.transpose(small_g[:, 0:3, 0:64], (1, 0, 2)).reshape(3, A_W)
    ln_g = small_g[:, 8, :].reshape(1, D)
    ln_b = small_g[:, 9, :].reshape(1, D)

    pool_w_b = even_pool_w[0].astype(BF16)
    pool_wt_b = jnp.swapaxes(even_pool_w[0], 1, 2).astype(BF16)
    ws_tril = jnp.tril(odd_w_s[0])
    ws_b = ws_tril.astype(BF16)
    wst_b = jnp.swapaxes(ws_tril, 1, 2).astype(BF16)
    b_full = jnp.broadcast_to(odd_b_s[0][:, :, None], (N_HEADS, GROUP, GROUP))

    p0, m0, x1, h1, wio_g, woo_g = _even_forward_fused(h0, xs, wie_g, conv_w, pool_w_b, even_pool_scale, woe, gpost0,
                                                       gpre1, TS_FWD, sides=[(_Gather, [wio_b, woo_b])])
    wio_t = jnp.transpose(wio_g, (0, 2, 1)).reshape(PROJ, D)
    woo = woo_g.reshape(D, D)
    woo_t = woo.T
    p1, m1, g2, loss_vec = _odd_forward(h1, x1, tgt, wio_g, ln_g, ln_b, ws_b, b_full, woo, gpost1, TS_FWD)

    dp1, g1, dwoo, dws, dbs, dlg, dlb, dgpost1, dgpre1 = _odd_backward(
        g2, m1, p1, x1, ln_g, ln_b, ws_b, wst_b, b_full, woo_t, wio_t, gpost1, gpre1, TS_FWD)
    by_owner = lambda dwo: dwo.reshape(N_DEV, SHARD_OUT, D)
    (dwio_parts, dwoo_parts, dlg_parts, dlb_parts,
     dws_g, dbs_g, gpost1_g, gpre1_g, loss_g) = _weight_grad_scatter(
        h1, dp1, TS_WGRAD, "odd_weight_grad",
        sides=[(_Scatter, [by_owner(dwoo), dlg.reshape(N_DEV, 1, 128), dlb.reshape(N_DEV, 1, 128)]),
               (_Gather, [dws.astype(BF16), dbs, dgpost1, dgpre1, loss_vec])])
    dp0, gx, dwoe, dcw, dpw, dps, dgpost0, dgpre0 = _even_backward_fused(
        g1, m0, p0, xs, conv_w, pool_w_b, pool_wt_b, even_pool_scale, woe_t, wie_t, gpost0, gpre0, TS_FWD)
    dcw_by_owner = jnp.transpose(dcw[0:3].reshape(3, N_DEV, 64), (1, 0, 2))
    dwie_parts, dwoe_parts, dcw_parts, dpw_g, dps_g, gpost0_g, gpre0_g = _weight_grad_scatter(
        h0, dp0, TS_WGRAD, "even_weight_grad",
        sides=[(_Scatter, [by_owner(dwoe), dcw_by_owner]), (_Gather, [dpw.astype(BF16), dps, dgpost0, dgpre0])])

    ((g_wie, d_wie, nm_wie, nv_wie), (g_wio, d_wio, nm_wio, nv_wio), (g_woe, d_woe, nm_woe, nv_woe),
     (g_woo, d_woo, nm_woo, nv_woo)) = _sum_update(
        [dwie_parts, dwio_parts, dwoe_parts, dwoo_parts],
        [(even_w_in[0], m_even_w_in[0], v_even_w_in[0]), (odd_w_in[0], m_odd_w_in[0], v_odd_w_in[0]),
         (even_w_out[0], m_even_w_out[0], v_even_w_out[0]), (odd_w_out[0], m_odd_w_out[0], v_odd_w_out[0])])

    partials = [gpre0_g, gpre1_g, gpost0_g, gpost1_g, dpw_g, dps_g, dws_g, dbs_g, dcw_parts, dlg_parts, dlb_parts]
    row_of = [(0, 0), (0, 1), (1, 0), (1, 1), (2, 0), (3, 0), (4, 0), (5, 0), (6, 0), (7, 0), (8, 0)]
    weights = [(pre_norm, m_pre_norm, v_pre_norm), (post_norm, m_post_norm, v_post_norm),
               (even_pool_w[0], m_even_pool_w[0], v_even_pool_w[0]),
               (even_pool_scale, m_even_pool_scale, v_even_pool_scale),
               (odd_w_s[0], m_odd_w_s[0], v_odd_w_s[0]), (odd_b_s[0], m_odd_b_s[0], v_odd_b_s[0]),
               (even_conv_w[0], m_even_conv_w[0], v_even_conv_w[0]),
               (odd_ln_g, m_odd_ln_g, v_odd_ln_g), (odd_ln_b, m_odd_ln_b, v_odd_ln_b)]
    small = _small_sum_update(partials, row_of, weights, loss_g)
    loss = small[0][0, 0]

    def leaves(k, big):
        pre, post, pw, psc, ws, bs, cw, lg, lb = [small[1 + 4 * i + k] for i in range(len(weights))]
        wie_k, woe_k, wio_k, woo_k = big
        return [pre, post, wie_k[None], cw[None], pw[None], psc, woe_k[None], wio_k[None], lg, lb, ws[None], bs[None],
                woo_k[None]]

    outs = [loss, gx.reshape(1, S, D)]
    outs += leaves(0, (g_wie, g_woe, g_wio, g_woo))
    outs += leaves(1, (d_wie, d_woe, d_wio, d_woo))
    outs += leaves(2, (nm_wie, nm_woe, nm_wio, nm_woo))
    outs += leaves(3, (nv_wie, nv_woe, nv_wio, nv_woo))
    return tuple(outs)
```

```python
import jax
import jax.numpy as jnp
from jax import lax
from jax.experimental import pallas as pl
from jax.experimental.pallas import tpu as pltpu

F32 = jnp.float32
BF16 = jnp.bfloat16
MESH = pl.DeviceIdType.MESH

D = 1024
EPS = 1e-6
A_W = 512
B_W = 512
POOL_WINDOWS = (2, 4, 8, 16)
GROUP = 128
N_GROUPS = 4
N_HEADS = 8
HALO = 16
PROJ = 3072
N_DEV = 8
N_CHIPS = 4
SHARD_IN = PROJ // N_DEV
SHARD_OUT = D // N_DEV

ADAM_LR = 0.001
ADAM_B1 = 0.9
ADAM_B2 = 0.999
ADAM_EPS = 1e-08
ADAM_WD = 0.01
ADAM_STEP = 10

VMEM_LIMIT = 54 * 1024 * 1024

TS_PRENORM = 1024
TS_FWD = 512
TS_BWD = 256
TS_WGRAD = 2048
SUB_ROWS = 256


def _params(n_grid=1, vmem=VMEM_LIMIT):
    return pltpu.CompilerParams(dimension_semantics=("arbitrary",) * n_grid, vmem_limit_bytes=vmem)


def _full(shape):
    return pl.BlockSpec(shape, lambda *_: (0,) * len(shape))


def _resident(shape):
    return pl.BlockSpec(shape, lambda *_: (0,) * len(shape), pipeline_mode=pl.Buffered(1))


def _load_columns(blocks_ref, whole_ref, sems):
    copies = [pltpu.make_async_copy(blocks_ref.at[k], whole_ref.at[:, k * SHARD_IN:(k + 1) * SHARD_IN], sems.at[k])
              for k in range(N_DEV)]
    for cp in copies:
        cp.start()
    for cp in copies:
        cp.wait()


def _sigmoid(z):
    return jax.nn.sigmoid(z)


def _dsilu(z, s):
    return s * (1.0 + z * (1.0 - s))


def _dot(a, b):
    return jnp.dot(a, b, preferred_element_type=F32)


def _dot_t0(a, b):
    return lax.dot_general(a, b, (((0,), (0,)), ((), ())), preferred_element_type=F32)


def _dot_t1(a, b):
    return lax.dot_general(a, b, (((1,), (1,)), ((), ())), preferred_element_type=F32)


def _rms(x):
    return lax.rsqrt(jnp.mean(x * x, axis=-1, keepdims=True) + EPS)


def _shift_down(a, k):
    return pltpu.roll(a, k, 0)


def _shift_up(a, k):
    return pltpu.roll(a, a.shape[0] - k, 0)


def _mesh_position():
    return lax.axis_index("x"), lax.axis_index("y"), lax.axis_index("c")


def _index(pos):
    return 4 * pos[0] + 2 * pos[1] + pos[2]


def _peer(pos, d):
    x, y, c = pos
    return (1 - x if d & 4 else x, 1 - y if d & 2 else y, 1 - c if d & 1 else c)


def _remote(src, dst, send_sems, recv_sems, k, to):
    return pltpu.make_async_remote_copy(src_ref=src, dst_ref=dst, send_sem=send_sems.at[k], recv_sem=recv_sems.at[k],
                                        device_id=to, device_id_type=MESH)


class _Gather:
    def __init__(self, srcs, dsts, send_sems, recv_sems, relays=False):
        x, y, c = _mesh_position()
        n = len(srcs)
        me, sibling = (x, y, c), (x, y, 1 - c)
        near = [(1 - x, y), (x, 1 - y)]
        far = (1 - x, 1 - y)
        x_first = c == 0
        relay_from = (jnp.where(x_first, 1 - x, x), jnp.where(x_first, y, 1 - y))
        relay_to = (jnp.where(x_first, x, 1 - x), jnp.where(x_first, 1 - y, y))

        def copy(k, t, block, to, src=None):
            slot = dsts[t].at[_index(block)]
            return _remote(slot if src is None else src, slot, send_sems, recv_sems, k * n + t, to)

        tensors = range(n)
        self.local = [pltpu.make_async_copy(srcs[t], dsts[t].at[_index(me)], recv_sems.at[7 * n + t]) for t in tensors]
        self.first = [copy(1 + j, t, me, chip + (c,), srcs[t]) for t in tensors for j, chip in enumerate(near)]
        self.first += [copy(0, t, me, sibling, srcs[t]) for t in tensors]
        self.near_in = [copy(1 + j, t, chip + (c,), me) for t in tensors for j, chip in enumerate(near)]
        self.near_passed = [copy(4 + j, t, chip + (c,), sibling) for t in tensors for j, chip in enumerate(near)]
        if relays:
            self.relayed = [copy(3, t, relay_from + (c,), relay_to + (c,)) for t in tensors]
        else:
            self.relayed = []
            self.first += [copy(3, t, me, far + (c,), srcs[t]) for t in tensors]
        self.relays = relays
        self.far_in = [copy(3, t, far + (c,), me) for t in tensors]
        self.far_passed = [copy(6, t, far + (c,), sibling) for t in tensors]
        self.d2d_in = [copy(0, t, sibling, me) for t in tensors]
        self.d2d_in += [copy(4 + j, t, chip + (1 - c,), me) for t in tensors for j, chip in enumerate(near + [far])]

    def start(self):
        for cp in self.local + self.first:
            cp.start()

    def _pass_on(self, landed_blocks, onward_copies):
        for landed, onward in zip(landed_blocks, onward_copies):
            landed.wait_recv()
            onward.start()

    def relay(self):
        if self.relays:
            self._pass_on(self.near_in, self.near_passed)
            for cp in self.relayed:
                cp.start()

    def middle(self):
        if not self.relays:
            self._pass_on(self.near_in, self.near_passed)
        self._pass_on(self.far_in, self.far_passed)

    def finish(self):
        for cp in self.d2d_in:
            cp.wait_recv()
        for cp in self.first + self.near_passed + self.relayed + self.far_passed:
            cp.wait_send()
        for cp in self.local:
            cp.wait()


class _Scatter:
    def __init__(self, srcs, dsts, send_sems, recv_sems):
        pos = _mesh_position()
        n = len(srcs)
        self.local = [pltpu.make_async_copy(srcs[t].at[_index(pos)], dsts[t].at[0], recv_sems.at[t]) for t in range(n)]
        self.remote = []
        for d in range(1, N_DEV):
            to = _peer(pos, d)
            self.remote += [_remote(srcs[t].at[_index(to)], dsts[t].at[d], send_sems, recv_sems, d * n + t, to)
                            for t in range(n)]

    def start(self):
        for cp in self.local + self.remote:
            cp.start()

    def relay(self):
        pass

    def middle(self):
        pass

    def finish(self):
        for cp in self.local:
            cp.wait()
        for cp in self.remote:
            cp.wait_recv()
        for cp in self.remote:
            cp.wait_send()


HBM_SPEC = pl.BlockSpec(memory_space=pltpu.HBM)
MIDDLE_STEPS_BEFORE_END = 4


def _landing_shape(exchange, src):
    return (N_DEV,) + src.shape if exchange is _Gather else src.shape


def _call(body, *, name, grid, in_specs, out_specs, out_shape, scratch_shapes, args, sides=()):
    params = _params(len(grid))
    if not sides:
        return pl.pallas_call(body, name=name, grid=grid, in_specs=in_specs, out_specs=out_specs, out_shape=out_shape,
                              scratch_shapes=scratch_shapes, compiler_params=params)(*args)
    n_in, n_out, n_scratch = len(in_specs), len(out_specs), len(scratch_shapes)
    counts = [len(srcs) for _, srcs in sides]
    ns = sum(counts)
    total = 1
    for g in grid:
        total *= g
    assert total // 2 < total - MIDDLE_STEPS_BEFORE_END
    side_args, side_shapes, side_sems = [], [], []
    for exchange, srcs in sides:
        side_args += list(srcs)
        side_shapes += [jax.ShapeDtypeStruct(_landing_shape(exchange, s), s.dtype) for s in srcs]
        side_sems += [pltpu.SemaphoreType.DMA((N_DEV * len(srcs),))] * 2

    def wrapped(*refs):
        ins, side_in = refs[:n_in], refs[n_in:n_in + ns]
        outs = refs[n_in + ns:n_in + ns + n_out]
        side_out = refs[n_in + ns + n_out:n_in + 2 * ns + n_out]
        rest = refs[n_in + 2 * ns + n_out:]
        scratch, sems = rest[:n_scratch], rest[n_scratch:]
        step = pl.program_id(0)
        for axis in range(1, len(grid)):
            step = step * grid[axis] + pl.program_id(axis)

        def exchanges():
            built, at = [], 0
            for k, (exchange, _) in enumerate(sides):
                built.append(exchange(side_in[at:at + counts[k]], side_out[at:at + counts[k]],
                                      sems[2 * k], sems[2 * k + 1]))
                at += counts[k]
            return built

        @pl.when(step == 0)
        def _():
            for ex in exchanges():
                ex.start()

        @pl.when(step == total // 2)
        def _():
            for ex in exchanges():
                ex.relay()

        @pl.when(step == total - MIDDLE_STEPS_BEFORE_END)
        def _():
            for ex in exchanges():
                ex.middle()

        body(*ins, *outs, *scratch)

        @pl.when(step == total - 1)
        def _():
            for ex in exchanges():
                ex.finish()

    return pl.pallas_call(
        wrapped, name=name, grid=grid, in_specs=list(in_specs) + [HBM_SPEC] * ns,
        out_specs=list(out_specs) + [HBM_SPEC] * ns, out_shape=list(out_shape) + side_shapes,
        scratch_shapes=list(scratch_shapes) + side_sems, compiler_params=params)(*args, *side_args)


def _prenorm_gather(x, g, wie, small, woe, wio, woo, ts):
    S = x.shape[0]
    nt = S // ts

    def body(x_ref, g_ref, wie_ref, sm_ref, woe_ref, wio_ref, woo_ref,
             h_ref, wie_out, sm_out, woe_out, wio_b, woo_b, cast_ref, small_ref, cast2_ref, send_sems, recv_sems):
        step = pl.program_id(0)
        gather = lambda: _Gather([cast_ref, small_ref, cast2_ref], [wie_out, sm_out, woe_out], send_sems, recv_sems,
                                 relays=True)

        @pl.when(step == 0)
        def _():
            cast_ref[...] = wie_ref[...].astype(BF16)
            small_ref[...] = sm_ref[...]
            cast2_ref[...] = woe_ref[...].astype(BF16)
            gather().start()
            wio_b[...] = wio_ref[...].astype(BF16)
            woo_b[...] = woo_ref[...].astype(BF16)

        xv = x_ref[...]
        h_ref[...] = ((xv * _rms(xv)) * g_ref[...]).astype(BF16)

        @pl.when(step == nt // 2)
        def _():
            gather().relay()

        @pl.when(step == nt - 1)
        def _():
            exchange = gather()
            exchange.middle()
            exchange.finish()

    tile = pl.BlockSpec((ts, D), lambda i: (i, 0))
    sems = pltpu.SemaphoreType.DMA((N_DEV * 3,))
    shards = [wie, small, woe, wio, woo]
    return pl.pallas_call(
        body, name="prenorm_gather", grid=(nt,),
        in_specs=[tile, _full((1, D))] + [_full(a.shape) for a in shards],
        out_specs=[tile, HBM_SPEC, HBM_SPEC, HBM_SPEC] + [_full(a.shape) for a in shards[3:]],
        out_shape=[jax.ShapeDtypeStruct((S, D), BF16), jax.ShapeDtypeStruct((N_DEV,) + wie.shape, BF16),
                   jax.ShapeDtypeStruct((N_DEV,) + small.shape, F32), jax.ShapeDtypeStruct((N_DEV,) + woe.shape, BF16)]
        + [jax.ShapeDtypeStruct(a.shape, BF16) for a in shards[3:]],
        scratch_shapes=[pltpu.VMEM(wie.shape, BF16), pltpu.VMEM(small.shape, F32), pltpu.VMEM(woe.shape, BF16),
                        sems, sems],
        compiler_params=_params(),
    )(x, g, wie, small, woe, wio, woo)


def _even_mixer(proj, halo, row0, cw_ref, pw_ref):
    xa, gb, gc, za, xp, zp = proj
    nr = xa.shape[0]
    ha = gc * xa
    ha_ext = jnp.concatenate([halo[0], ha], axis=0)
    ha_m1 = _shift_down(ha_ext, 1)[HALO:]
    ha_m2 = _shift_down(ha_ext, 2)[HALO:]
    conv = cw_ref[2:3, :] * ha + cw_ref[1:2, :] * ha_m1 + cw_ref[0:1, :] * ha_m2
    sig_a = _sigmoid(za)
    silu_a = za * sig_a

    xp_ext = jnp.concatenate([halo[1], xp], axis=0)
    pos = row0 + lax.broadcasted_iota(jnp.int32, (nr, 1), 0)
    pooled, inv_cnt, mixed = [], [], []
    for g, w in enumerate(POOL_WINDOWS):
        cols = slice(g * GROUP, (g + 1) * GROUP)
        s = xp_ext[:, cols]
        k = 1
        while k < w:
            s = s + _shift_down(s, k)
            k *= 2
        inv = 1.0 / jnp.minimum(pos + 1, w).astype(F32)
        pg = s[HALO:] * inv - xp[:, cols]
        pooled.append(pg)
        inv_cnt.append(inv)
        mixed.append(_dot(pg.astype(BF16), pw_ref[g]))
    mixed = jnp.concatenate(mixed, axis=1)
    sig_b = _sigmoid(zp)
    silu_b = zp * sig_b
    return dict(xa=xa, gb=gb, gc=gc, za=za, xp=xp, zp=zp, ha=ha, ha_m1=ha_m1, ha_m2=ha_m2, conv=conv,
                sig_a=sig_a, silu_a=silu_a, pooled=pooled, inv_cnt=inv_cnt, mixed=mixed, sig_b=sig_b,
                silu_b=silu_b)


def _columns(a):
    return [a[:, k * A_W:(k + 1) * A_W].astype(F32) for k in range(6)]


def _even_forward_fused(h, x, win_b, cw, pw_b, ps, wout_b, gpost, gpre_next, ts, sides=()):
    S = x.shape[0]

    def body(h_ref, x_ref, wi_blocks, cw_ref, pw_ref, ps_ref, wo_ref, gp_ref, gn_ref,
             p_ref, m_ref, x1_ref, h1_ref, mix_ref, carry_ref, wi_ref, wi_sems):
        i = pl.program_id(0)

        @pl.when(i == 0)
        def _():
            carry_ref[...] = jnp.zeros_like(carry_ref)
            _load_columns(wi_blocks, wi_ref, wi_sems)

        for r0 in range(0, ts, SUB_ROWS):
            rows = slice(r0, r0 + SUB_ROWS)
            proj = _dot(h_ref[rows, :], wi_ref[...])
            p_ref[rows, :] = proj.astype(BF16)
            fw = _even_mixer(_columns(proj), (carry_ref[:, 0:A_W], carry_ref[:, A_W:D]), i * ts + r0, cw_ref, pw_ref)
            carry_ref[:, 0:A_W] = fw["ha"][SUB_ROWS - HALO:]
            carry_ref[:, A_W:D] = fw["xp"][SUB_ROWS - HALO:]
            mix_ref[rows, 0:A_W] = (fw["gb"] * fw["conv"] * fw["silu_a"]).astype(BF16)
            mix_ref[rows, A_W:D] = (fw["mixed"] * ps_ref[...] * fw["silu_b"]).astype(BF16)
            m = _dot(mix_ref[rows, :], wo_ref[...])
            m_ref[rows, :] = m.astype(BF16)
            x1 = x_ref[rows, :] + (m * _rms(m)) * gp_ref[...]
            x1_ref[rows, :] = x1
            h1_ref[rows, :] = ((x1 * _rms(x1)) * gn_ref[...]).astype(BF16)

    tile = pl.BlockSpec((ts, D), lambda i: (i, 0))
    return _call(
        body, name="even_forward", grid=(S // ts,),
        in_specs=[tile, tile, HBM_SPEC, _full((3, A_W)), _full((N_GROUPS, GROUP, GROUP)), _full((1, B_W)),
                  _resident((D, D)), _full((1, D)), _full((1, D))],
        out_specs=[pl.BlockSpec((ts, PROJ), lambda i: (i, 0)), tile, tile, tile],
        out_shape=[jax.ShapeDtypeStruct((S, PROJ), BF16), jax.ShapeDtypeStruct((S, D), BF16),
                   jax.ShapeDtypeStruct((S, D), F32), jax.ShapeDtypeStruct((S, D), BF16)],
        scratch_shapes=[pltpu.VMEM((ts, D), BF16), pltpu.VMEM((HALO, D), F32), pltpu.VMEM((D, PROJ), BF16),
                        pltpu.SemaphoreType.DMA((N_DEV,))],
        args=(h, x, win_b, cw, pw_b, ps, wout_b, gpost, gpre_next), sides=sides)


def _even_backward_fused(g1, m0, p, x, cw, pw_b, pwt_b, ps, woutt_b, wint_b, gpost, gpre, ts):
    S = g1.shape[0]
    nt = S // ts

    def body(g_ref, m_ref, p_ref, halo_ref, x_ref, cw_ref, pw_ref, pwt_ref, ps_ref, wot_ref, wit_ref, gp_ref, gn_ref,
             dp_ref, dx_ref, dwob_ref, dcw_ref, dpw_ref, dps_ref, dgp_ref, dgn_ref,
             carry_ref, mix_ref, dm_ref, dwo_ref):
        step = pl.program_id(0)
        i = nt - 1 - step

        @pl.when(step == 0)
        def _():
            carry_ref[...] = jnp.zeros_like(carry_ref)
            dwo_ref[...] = jnp.zeros_like(dwo_ref)
            dcw_ref[...] = jnp.zeros_like(dcw_ref)
            dpw_ref[...] = jnp.zeros_like(dpw_ref)
            dps_ref[...] = jnp.zeros_like(dps_ref)
            dgp_ref[...] = jnp.zeros_like(dgp_ref)
            dgn_ref[...] = jnp.zeros_like(dgn_ref)

        ps_v = ps_ref[...]
        sums = [None] * 6
        add = lambda k, part: sums.__setitem__(k, part if sums[k] is None else sums[k] + part)
        for r0 in range(ts - SUB_ROWS, -1, -SUB_ROWS):
            rows = slice(r0, r0 + SUB_ROWS)
            g = g_ref[rows, :]
            m = m_ref[rows, :].astype(F32)
            q = _rms(m)
            n = m * q
            add(0, jnp.sum(g * n, axis=0, keepdims=True))
            dn = g * gp_ref[...]
            dm = q * (dn - n * jnp.mean(dn * n, axis=-1, keepdims=True))
            dm_ref[rows, :] = dm.astype(BF16)
            dmix = _dot(dm_ref[rows, :], wot_ref[...])
            dya = dmix[:, 0:A_W]
            dyb = dmix[:, A_W:D]

            if r0 == 0:
                before = _columns(halo_ref[...])
                keep = jnp.where(i == 0, 0.0, 1.0).astype(F32)
                halo = (before[2] * before[0] * keep, before[4] * keep)
            else:
                before = _columns(p_ref[r0 - HALO:r0, :])
                halo = (before[2] * before[0], before[4])
            fw = _even_mixer(_columns(p_ref[rows, :]), halo, i * ts + r0, cw_ref, pw_ref)
            mix_ref[rows, 0:A_W] = (fw["gb"] * fw["conv"] * fw["silu_a"]).astype(BF16)
            mix_ref[rows, A_W:D] = (fw["mixed"] * ps_v * fw["silu_b"]).astype(BF16)

            t = dya * fw["gb"]
            dconv = t * fw["silu_a"]
            dgb = dya * fw["conv"] * fw["silu_a"]
            dza = t * fw["conv"] * _dsilu(fw["za"], fw["sig_a"])
            add(2, jnp.sum(dconv * fw["ha"], axis=0, keepdims=True))
            add(3, jnp.sum(dconv * fw["ha_m1"], axis=0, keepdims=True))
            add(4, jnp.sum(dconv * fw["ha_m2"], axis=0, keepdims=True))
            dconv_ext = jnp.concatenate([dconv, carry_ref[:, 0:A_W]], axis=0)
            dha = (cw_ref[2:3, :] * dconv + cw_ref[1:2, :] * _shift_up(dconv_ext, 1)[:SUB_ROWS]
                   + cw_ref[0:1, :] * _shift_up(dconv_ext, 2)[:SUB_ROWS])
            dgc = dha * fw["xa"]
            dxa = dha * fw["gc"]

            u = dyb * fw["mixed"]
            add(1, jnp.sum(u * fw["silu_b"], axis=0, keepdims=True))
            dzp = u * ps_v * _dsilu(fw["zp"], fw["sig_b"])
            dmixed = (dyb * ps_v * fw["silu_b"]).astype(BF16)
            dxp, e_first = [], []
            for gi, w in enumerate(POOL_WINDOWS):
                cols = slice(gi * GROUP, (gi + 1) * GROUP)
                dmg = dmixed[:, cols]
                dpooled = _dot(dmg, pwt_ref[gi])
                dpw_ref[gi] += _dot_t0(fw["pooled"][gi].astype(BF16), dmg)
                e = dpooled * fw["inv_cnt"][gi]
                e_first.append(e[0:HALO])
                s = jnp.concatenate([e, carry_ref[:, A_W + gi * GROUP:A_W + (gi + 1) * GROUP]], axis=0)
                k = 1
                while k < w:
                    s = s + _shift_up(s, k)
                    k *= 2
                dxp.append(s[:SUB_ROWS] - dpooled)
            carry_ref[:, 0:A_W] = dconv[0:HALO]
            carry_ref[:, A_W:D] = jnp.concatenate(e_first, axis=1)

            dp_ref[rows, 0:512] = dxa.astype(BF16)
            dp_ref[rows, 512:1024] = dgb.astype(BF16)
            dp_ref[rows, 1024:1536] = dgc.astype(BF16)
            dp_ref[rows, 1536:2048] = dza.astype(BF16)
            dp_ref[rows, 2048:2560] = jnp.concatenate(dxp, axis=1).astype(BF16)
            dp_ref[rows, 2560:3072] = dzp.astype(BF16)

            dh = _dot(dp_ref[rows, :], wit_ref[...])
            xv = x_ref[rows, :]
            r = _rms(xv)
            xn = xv * r
            add(5, jnp.sum(dh * xn, axis=0, keepdims=True))
            dxn = dh * gn_ref[...]
            dx_ref[rows, :] = g + r * (dxn - xn * jnp.mean(dxn * xn, axis=-1, keepdims=True))

        dwo_ref[...] += _dot_t0(mix_ref[...], dm_ref[...])
        dgp_ref[...] += sums[0]
        dps_ref[...] += sums[1]
        dcw_ref[2:3, :] += sums[2]
        dcw_ref[1:2, :] += sums[3]
        dcw_ref[0:1, :] += sums[4]
        dgn_ref[...] += sums[5]

        @pl.when(step == nt - 1)
        def _():
            dwob_ref[...] = dwo_ref[...].astype(BF16)

    rev = lambda s: (nt - 1 - s, 0)
    tile = pl.BlockSpec((ts, D), rev)
    wide = pl.BlockSpec((ts, PROJ), rev)
    bpt = ts // HALO
    halo_map = lambda s: (jnp.maximum((nt - 1 - s) * bpt - 1, 0), 0)
    groups = _full((N_GROUPS, GROUP, GROUP))
    vec = _full((1, D))
    return pl.pallas_call(
        body, name="even_backward", grid=(nt,),
        in_specs=[tile, tile, wide, pl.BlockSpec((HALO, PROJ), halo_map), tile, _full((3, A_W)), groups, groups,
                  _full((1, B_W)), _resident((D, D)), _resident((PROJ, D)), vec, vec],
        out_specs=[wide, tile, _full((D, D)), _full((8, A_W)), groups, _full((1, B_W)), vec, vec],
        out_shape=[jax.ShapeDtypeStruct((S, PROJ), BF16), jax.ShapeDtypeStruct((S, D), F32),
                   jax.ShapeDtypeStruct((D, D), BF16), jax.ShapeDtypeStruct((8, A_W), F32),
                   jax.ShapeDtypeStruct((N_GROUPS, GROUP, GROUP), F32), jax.ShapeDtypeStruct((1, B_W), F32),
                   jax.ShapeDtypeStruct((1, D), F32), jax.ShapeDtypeStruct((1, D), F32)],
        scratch_shapes=[pltpu.VMEM((HALO, D), F32), pltpu.VMEM((ts, D), BF16), pltpu.VMEM((ts, D), BF16),
                        pltpu.VMEM((D, D), F32)],
        compiler_params=_params(),
    )(g1, m0, p, p, x, cw, pw_b, pwt_b, ps, woutt_b, wint_b, gpost, gpre)


def _gate_matmuls(w_ref, in_ref, out_ref, chunks, bias_ref=None):
    for h in range(N_HEADS):
        cols = slice(h * GROUP, (h + 1) * GROUP)
        wide = jnp.concatenate([in_ref[chunk, cols] for chunk in chunks], axis=1)
        res = _dot(w_ref[h], wide)
        for k, chunk in enumerate(chunks):
            part = res[:, k * GROUP:(k + 1) * GROUP]
            out_ref[chunk, cols] = part if bias_ref is None else part + bias_ref[h]


def _odd_forward_parts(v, lg_ref, lb_ref, wt_ref, bfull_ref, vln_ref, sv_ref, r0, nr):
    rows = slice(r0, r0 + nr)
    mu = jnp.mean(v, axis=-1, keepdims=True)
    vc = v - mu
    rstd = lax.rsqrt(jnp.mean(vc * vc, axis=-1, keepdims=True) + EPS)
    vh = vc * rstd
    vln_ref[rows, :] = (vh * lg_ref[...] + lb_ref[...]).astype(BF16)
    _gate_matmuls(wt_ref, vln_ref, sv_ref, [slice(r0 + k * GROUP, r0 + (k + 1) * GROUP) for k in range(nr // GROUP)],
                  bfull_ref)
    return vh, rstd


def _odd_forward(h, x1, tgt, win_b, lg, lb, wt_b, bfull, wout_b, gpost, ts):
    S = x1.shape[0]

    def body(h_ref, x_ref, t_ref, wi_blocks, lg_ref, lb_ref, wt_ref, bfull_ref, wo_ref, gp_ref,
             p_ref, m_ref, g_ref, loss_ref, vln_ref, sv_ref, y_ref, wi_ref, wi_sems):
        @pl.when(pl.program_id(0) == 0)
        def _():
            loss_ref[...] = jnp.zeros_like(loss_ref)
            _load_columns(wi_blocks, wi_ref, wi_sems)

        loss = None
        for r0 in range(0, ts, SUB_ROWS):
            rows = slice(r0, r0 + SUB_ROWS)
            proj = _dot(h_ref[rows, :], wi_ref[...])
            p_ref[rows, :] = proj.astype(BF16)
            u, v, z = proj[:, 0:D], proj[:, D:2 * D], proj[:, 2 * D:3 * D]
            _odd_forward_parts(v, lg_ref, lb_ref, wt_ref, bfull_ref, vln_ref, sv_ref, r0, SUB_ROWS)
            y_ref[rows, :] = (u * sv_ref[rows, :] * (z * _sigmoid(z))).astype(BF16)
            m = _dot(y_ref[rows, :], wo_ref[...])
            m_ref[rows, :] = m.astype(BF16)
            x2 = x_ref[rows, :] + (m * _rms(m)) * gp_ref[...]
            err = x2 - t_ref[rows, :]
            g_ref[rows, :] = (err * (1.0 / D)).astype(BF16)
            part = jnp.sum(err * err, axis=0, keepdims=True)
            loss = part if loss is None else loss + part
        loss_ref[...] += loss

    tile = pl.BlockSpec((ts, D), lambda i: (i, 0))
    small = _full((N_HEADS, GROUP, GROUP))
    return pl.pallas_call(
        body, name="odd_forward", grid=(S // ts,),
        in_specs=[pl.BlockSpec((ts, D), lambda i: (i, 0)), tile, tile, HBM_SPEC, _full((1, D)), _full((1, D)),
                  small, small, _full((D, D)), _full((1, D))],
        out_specs=[pl.BlockSpec((ts, PROJ), lambda i: (i, 0)), tile, tile, _full((1, D))],
        out_shape=[jax.ShapeDtypeStruct((S, PROJ), BF16), jax.ShapeDtypeStruct((S, D), BF16),
                   jax.ShapeDtypeStruct((S, D), BF16), jax.ShapeDtypeStruct((1, D), F32)],
        scratch_shapes=[pltpu.VMEM((ts, D), BF16), pltpu.VMEM((ts, D), F32), pltpu.VMEM((ts, D), BF16),
                        pltpu.VMEM((D, PROJ), BF16), pltpu.SemaphoreType.DMA((N_DEV,))],
        compiler_params=_params(),
    )(h, x1, tgt, win_b, lg, lb, wt_b, bfull, wout_b, gpost)


def _odd_backward(g2, m1, p, x1, lg, lb, wt_b, wtt_b, bfull, woutt_b, wint_b, gpost, gpre, ts):
    S = g2.shape[0]
    nt = S // ts
    sub = ts // 2

    def body(g_ref, m_ref, p_ref, x_ref, lg_ref, lb_ref, wt_ref, wtt_ref, bfull_ref, wot_ref, wit_ref, gp_ref, gn_ref,
             dp_ref, dx_ref, dwob_ref, dws_ref, dbs_ref, dlg_ref, dlb_ref, dgp_ref, dgn_ref,
             vln_ref, sv_ref, dsvb_ref, dvln_ref, dsum_ref, y_ref, dm_ref, dwo_ref):
        step = pl.program_id(0)

        @pl.when(step == 0)
        def _():
            dwo_ref[...] = jnp.zeros_like(dwo_ref)
            dws_ref[...] = jnp.zeros_like(dws_ref)
            dsum_ref[...] = jnp.zeros_like(dsum_ref)
            dlg_ref[...] = jnp.zeros_like(dlg_ref)
            dlb_ref[...] = jnp.zeros_like(dlb_ref)
            dgp_ref[...] = jnp.zeros_like(dgp_ref)
            dgn_ref[...] = jnp.zeros_like(dgn_ref)

        tril = (lax.broadcasted_iota(jnp.int32, (GROUP, GROUP), 0)
                >= lax.broadcasted_iota(jnp.int32, (GROUP, GROUP), 1))
        sums = [None] * 4
        add = lambda k, part: sums.__setitem__(k, part if sums[k] is None else sums[k] + part)

        def post_norm(c):
            g = g_ref[c["rows"], :].astype(F32)
            m = m_ref[c["rows"], :].astype(F32)
            q = _rms(m)
            n = m * q
            add(0, jnp.sum(g * n, axis=0, keepdims=True))
            dn = g * gp_ref[...]
            dm = q * (dn - n * jnp.mean(dn * n, axis=-1, keepdims=True))
            dm_ref[c["rows"], :] = dm.astype(BF16)

        def out_projection(c):
            c["dy"] = _dot(dm_ref[c["rows"], :], wot_ref[...])

        def layernorm(c):
            v = p_ref[c["rows"], D:2 * D].astype(F32)
            mu = jnp.mean(v, axis=-1, keepdims=True)
            vc = v - mu
            c["rstd"] = lax.rsqrt(jnp.mean(vc * vc, axis=-1, keepdims=True) + EPS)
            c["vh"] = vc * c["rstd"]
            vln_ref[c["rows"], :] = (c["vh"] * lg_ref[...] + lb_ref[...]).astype(BF16)

        def gate_matmuls(c):
            _gate_matmuls(wt_ref, vln_ref, sv_ref, c["chunks"], bfull_ref)

        def gating(c):
            rows = c["rows"]
            u = p_ref[rows, 0:D].astype(F32)
            z = p_ref[rows, 2 * D:3 * D].astype(F32)
            sv = sv_ref[rows, :]
            sig = _sigmoid(z)
            sz = z * sig
            y_ref[rows, :] = (u * sv * sz).astype(BF16)
            dy = c.pop("dy")
            t = dy * sz
            dsv = t * u
            dsvb_ref[rows, :] = dsv.astype(BF16)
            for k in range(sub // GROUP):
                dsum_ref[...] += dsv[k * GROUP:(k + 1) * GROUP]
            dp_ref[rows, 0:D] = (t * sv).astype(BF16)
            dp_ref[rows, 2 * D:3 * D] = (dy * u * sv * _dsilu(z, sig)).astype(BF16)

        def gate_backward_matmuls(c):
            for chunk in c["chunks"]:
                for h in range(N_HEADS):
                    cols = slice(h * GROUP, (h + 1) * GROUP)
                    dvln_ref[chunk, cols] = _dot(wtt_ref[h], dsvb_ref[chunk, cols])
                    dws_ref[h] += jnp.where(tril, _dot_t1(dsvb_ref[chunk, cols], vln_ref[chunk, cols]), 0.0)

        def layernorm_backward(c):
            vh, rstd = c.pop("vh"), c.pop("rstd")
            dvln = dvln_ref[c["rows"], :]
            add(1, jnp.sum(dvln * vh, axis=0, keepdims=True))
            add(2, jnp.sum(dvln, axis=0, keepdims=True))
            dvh = dvln * lg_ref[...]
            dv = rstd * (dvh - jnp.mean(dvh, axis=-1, keepdims=True)
                         - vh * jnp.mean(dvh * vh, axis=-1, keepdims=True))
            dp_ref[c["rows"], D:2 * D] = dv.astype(BF16)

        def in_projection(c):
            c["dh"] = _dot(dp_ref[c["rows"], :], wit_ref[...])

        def pre_norm(c):
            dh = c.pop("dh")
            xv = x_ref[c["rows"], :]
            r = _rms(xv)
            xn = xv * r
            add(3, jnp.sum(dh * xn, axis=0, keepdims=True))
            dxn = dh * gn_ref[...]
            dx_ref[c["rows"], :] = (g_ref[c["rows"], :].astype(F32)
                                    + r * (dxn - xn * jnp.mean(dxn * xn, axis=-1, keepdims=True)))

        phases = [post_norm, out_projection, layernorm, gate_matmuls, gating, gate_backward_matmuls,
                  layernorm_backward, in_projection, pre_norm]
        groups = [dict(rows=slice(r0, r0 + sub),
                       chunks=[slice(r0 + k * GROUP, r0 + (k + 1) * GROUP) for k in range(sub // GROUP)])
                  for r0 in range(0, ts, sub)]
        for group in groups:
            for phase in phases:
                phase(group)

        dwo_ref[...] += _dot_t0(y_ref[...], dm_ref[...])
        dgp_ref[...] += sums[0]
        dlg_ref[...] += sums[1]
        dlb_ref[...] += sums[2]
        dgn_ref[...] += sums[3]

        @pl.when(step == nt - 1)
        def _():
            dwob_ref[...] = dwo_ref[...].astype(BF16)
            ones = jnp.ones((8, GROUP), F32)
            for h in range(N_HEADS):
                cols = slice(h * GROUP, (h + 1) * GROUP)
                total = lax.dot_general(ones, dsum_ref[:, cols], (((1,), (1,)), ((), ())),
                                        precision=lax.Precision.HIGHEST, preferred_element_type=F32)
                dbs_ref[h:h + 1, :] = total[0:1, :]

    tile = pl.BlockSpec((ts, D), lambda i: (i, 0))
    wide = pl.BlockSpec((ts, PROJ), lambda i: (i, 0))
    small = _full((N_HEADS, GROUP, GROUP))
    heads = _resident((N_HEADS, GROUP, GROUP))
    vec = _full((1, D))
    return pl.pallas_call(
        body, name="odd_backward", grid=(nt,),
        in_specs=[tile, tile, wide, tile, vec, vec, heads, heads, heads, _resident((D, D)), _resident((PROJ, D)),
                  vec, vec],
        out_specs=[wide, tile, _full((D, D)), small, _full((N_HEADS, GROUP)), vec, vec, vec, vec],
        out_shape=[jax.ShapeDtypeStruct((S, PROJ), BF16), jax.ShapeDtypeStruct((S, D), F32),
                   jax.ShapeDtypeStruct((D, D), BF16),
                   jax.ShapeDtypeStruct((N_HEADS, GROUP, GROUP), F32), jax.ShapeDtypeStruct((N_HEADS, GROUP), F32),
                   jax.ShapeDtypeStruct((1, D), F32), jax.ShapeDtypeStruct((1, D), F32),
                   jax.ShapeDtypeStruct((1, D), F32), jax.ShapeDtypeStruct((1, D), F32)],
        scratch_shapes=[pltpu.VMEM((ts, D), BF16), pltpu.VMEM((ts, D), F32), pltpu.VMEM((ts, D), BF16),
                        pltpu.VMEM((ts, D), F32), pltpu.VMEM((GROUP, D), F32), pltpu.VMEM((ts, D), BF16),
                        pltpu.VMEM((ts, D), BF16), pltpu.VMEM((D, D), F32)],
        compiler_params=_params(),
    )(g2, m1, p, x1, lg, lb, wt_b, wtt_b, bfull, woutt_b, wint_b, gpost, gpre)


def _weight_grad_scatter(h, dp, ts, name, sides=()):
    S = h.shape[0]
    ts = min(ts, S)
    nt = S // ts
    panel = 2 * SHARD_IN
    last = N_CHIPS - 1

    def body(h_ref, dp_ref, parts_ref, acc_ref, half_ref, swap_ref, sum_ref, send_sems, recv_sems):
        s, t = pl.program_id(0), pl.program_id(1)
        x, y, c = _mesh_position()
        sibling = (x, y, 1 - c)

        def swap(k):
            return _remote(half_ref.at[k, 1], swap_ref.at[k], send_sems, recv_sems, k, sibling)

        def chip_sum(k):
            flip = last - k
            to = (x ^ (flip >> 1), y ^ (flip & 1), c)
            return _remote(sum_ref.at[k], parts_ref.at[1 + flip], send_sems, recv_sems, 3 + flip, to)

        own = pltpu.make_async_copy(half_ref.at[last, 0], parts_ref.at[0], recv_sems.at[7])
        to_sibling = _remote(half_ref.at[last, 1], parts_ref.at[1], send_sems, recv_sems, 3, sibling)

        @pl.when(t == 0)
        def _():
            acc_ref[...] = jnp.zeros_like(acc_ref)

        for k in range(last):
            @pl.when((s == k + 1) & (t == 0))
            def _():
                swap(k).wait_recv()
                sum_ref[k] = (half_ref[k, 0].astype(F32) + swap_ref[k].astype(F32)).astype(BF16)
                chip_sum(k).start()

        acc_ref[...] += _dot_t0(h_ref[...], dp_ref[...])

        @pl.when(t == nt - 1)
        def _():
            for core in range(2):
                @pl.when(c == core)
                def _():
                    half_ref[s, core] = acc_ref[:, 0:SHARD_IN].astype(BF16)
                    half_ref[s, 1 - core] = acc_ref[:, SHARD_IN:panel].astype(BF16)
            for k in range(last):
                @pl.when(s == k)
                def _():
                    swap(k).start()

        @pl.when((s == last) & (t == nt - 1))
        def _():
            own.start()
            to_sibling.start()
            own.wait()
            _remote(half_ref.at[last, 1], parts_ref.at[1], send_sems, recv_sems, 3, sibling).wait_recv()
            for k in range(last):
                chip_sum(k).wait_recv()
            to_sibling.wait_send()
            for k in range(last):
                swap(k).wait_send()
                chip_sum(k).wait_send()

    chip_panel = lambda s, t: (t, (2 * lax.axis_index("x") + lax.axis_index("y")) ^ (last - s))
    sems = pltpu.SemaphoreType.DMA((8,))
    return _call(
        body, name=name, grid=(N_CHIPS, nt),
        in_specs=[pl.BlockSpec((ts, D), lambda s, t: (t, 0)), pl.BlockSpec((ts, panel), chip_panel)],
        out_specs=[HBM_SPEC], out_shape=[jax.ShapeDtypeStruct((N_CHIPS + 1, D, SHARD_IN), BF16)],
        scratch_shapes=[pltpu.VMEM((D, panel), F32), pltpu.VMEM((N_CHIPS, 2, D, SHARD_IN), BF16),
                        pltpu.VMEM((last, D, SHARD_IN), BF16), pltpu.VMEM((last, D, SHARD_IN), BF16), sems, sems],
        args=(h, dp), sides=sides)


def _adamw(w, g, m, v):
    m = ADAM_B1 * m + (1.0 - ADAM_B1) * g
    v = ADAM_B2 * v + (1.0 - ADAM_B2) * (g * g)
    m_hat = m / (1.0 - ADAM_B1 ** ADAM_STEP)
    v_hat = v / (1.0 - ADAM_B2 ** ADAM_STEP)
    delta = -ADAM_LR * (m_hat / (jnp.sqrt(v_hat) + ADAM_EPS) + ADAM_WD * w)
    return delta, m, v


UPDATE_STEPS = 4


def _sum_update(parts, weights):
    n_w = len(weights)

    def body(*refs):
        p_refs, wmv_refs, outs = refs[:n_w], refs[n_w:4 * n_w], refs[4 * n_w:]
        for i in range(n_w):
            acc = p_refs[i][0].astype(F32)
            for d in range(1, parts[i].shape[0]):
                acc = acc + p_refs[i][d].astype(F32)
            w_ref, m_ref, v_ref = wmv_refs[3 * i:3 * i + 3]
            g_out, d_out, m_out, v_out = outs[4 * i:4 * i + 4]
            g_out[...] = acc
            delta, m_new, v_new = _adamw(w_ref[...], acc, m_ref[...], v_ref[...])
            d_out[...] = delta
            m_out[...] = m_new
            v_out[...] = v_new

    part_specs, wmv_specs, out_specs, out_shape = [], [], [], []
    for p, (w, _, _) in zip(parts, weights):
        R, C = w.shape
        rb = R // UPDATE_STEPS
        blk = pl.BlockSpec((rb, C), lambda i: (i, 0))
        part_specs.append(pl.BlockSpec((p.shape[0], rb, C), lambda i: (0, i, 0)))
        wmv_specs += [blk] * 3
        out_specs += [blk] * 4
        out_shape += [jax.ShapeDtypeStruct((R, C), F32)] * 4
    flat = pl.pallas_call(
        body, name="update_large_weights", grid=(UPDATE_STEPS,), in_specs=part_specs + wmv_specs,
        out_specs=out_specs, out_shape=out_shape, compiler_params=_params(),
    )(*parts, *[a for wmv in weights for a in wmv])
    return [flat[4 * i:4 * i + 4] for i in range(n_w)]


def _small_sum_update(partials, row_of, weights, loss_parts):
    n_p, n_w = len(partials), len(weights)

    def body(*refs):
        p_refs = refs[:n_p]
        loss_ref = refs[n_p]
        w_refs = refs[n_p + 1:n_p + 1 + 3 * n_w]
        loss_out = refs[n_p + 1 + 3 * n_w]
        outs = refs[n_p + 2 + 3 * n_w:]

        def total(ref):
            acc = ref[0].astype(F32)
            for d in range(1, N_DEV):
                acc = acc + ref[d].astype(F32)
            return acc

        lsum = jnp.sum(total(loss_ref), axis=-1, keepdims=True) * (0.5 / D)
        loss_out[...] = jnp.broadcast_to(lsum, loss_out.shape)
        for k in range(n_p):
            i, r0 = row_of[k]
            g = total(p_refs[k])
            w_ref, m_ref, v_ref = w_refs[3 * i:3 * i + 3]
            g_out, d_out, m_out, v_out = outs[4 * i:4 * i + 4]
            if g.ndim == 2:
                at = (slice(r0, r0 + g.shape[0]), slice(None))
            else:
                at = (slice(None),) * g.ndim
            delta, m_new, v_new = _adamw(w_ref[at], g, m_ref[at], v_ref[at])
            g_out[at] = g
            d_out[at] = delta
            m_out[at] = m_new
            v_out[at] = v_new

    vm = pl.BlockSpec(memory_space=pltpu.VMEM)
    flat_w = [a for wmv in weights for a in wmv]
    out_shape = [jax.ShapeDtypeStruct((1, 128), F32)]
    for w, _, _ in weights:
        out_shape += [jax.ShapeDtypeStruct(w.shape, F32)] * 4
    return pl.pallas_call(
        body, name="small_update", in_specs=[vm] * (n_p + 1 + 3 * n_w), out_specs=[vm] * len(out_shape),
        out_shape=out_shape, compiler_params=pltpu.CompilerParams(vmem_limit_bytes=VMEM_LIMIT),
    )(*partials, loss_parts, *flat_w)


def _pack_small_shard(cw, lg, lb):
    return (jnp.pad(cw, ((0, 13), (0, 64))) + jnp.pad(lg, ((8, 7), (0, 0))) + jnp.pad(lb, ((9, 6), (0, 0))))


def kernel(x, pre_norm, post_norm, even_w_in, even_conv_w, even_pool_w, even_pool_scale, even_w_out, odd_w_in, odd_ln_g, odd_ln_b, odd_w_s, odd_b_s, odd_w_out, loss_target, m_pre_norm, m_post_norm, m_even_w_in, m_even_conv_w, m_even_pool_w, m_even_pool_scale, m_even_w_out, m_odd_w_in, m_odd_ln_g, m_odd_ln_b, m_odd_w_s, m_odd_b_s, m_odd_w_out, v_pre_norm, v_post_norm, v_even_w_in, v_even_conv_w, v_even_pool_w, v_even_pool_scale, v_even_w_out, v_odd_w_in, v_odd_ln_g, v_odd_ln_b, v_odd_w_s, v_odd_b_s, v_odd_w_out):
    S = x.shape[1]
    xs = x.reshape(S, D)
    tgt = loss_target.reshape(S, D)

    gpre0, gpre1 = pre_norm[0:1], pre_norm[1:2]
    gpost0, gpost1 = post_norm[0:1], post_norm[1:2]
    small_shard = _pack_small_shard(even_conv_w[0], odd_ln_g, odd_ln_b)
    h0, wie_g, small_g, woe_g, wio_b, woo_b = _prenorm_gather(xs, gpre0, even_w_in[0], small_shard, even_w_out[0],
                                                              odd_w_in[0], odd_w_out[0], TS_PRENORM)
    wie_t = jnp.transpose(wie_g, (0, 2, 1)).reshape(PROJ, D)
    woe = woe_g.reshape(D, D)
    woe_t = woe.T
    conv_w = jnp.transpose(small_g[:, 0:3, 0:64], (1, 0, 2)).reshape(3, A_W)
    ln_g = small_g[:, 8, :].reshape(1, D)
    ln_b = small_g[:, 9, :].reshape(1, D)

    pool_w_b = even_pool_w[0].astype(BF16)
    pool_wt_b = jnp.swapaxes(even_pool_w[0], 1, 2).astype(BF16)
    ws_tril = jnp.tril(odd_w_s[0])
    ws_b = ws_tril.astype(BF16)
    wst_b = jnp.swapaxes(ws_tril, 1, 2).astype(BF16)
    b_full = jnp.broadcast_to(odd_b_s[0][:, :, None], (N_HEADS, GROUP, GROUP))

    p0, m0, x1, h1, wio_g, woo_g = _even_forward_fused(h0, xs, wie_g, conv_w, pool_w_b, even_pool_scale, woe, gpost0,
                                                       gpre1, TS_FWD, sides=[(_Gather, [wio_b, woo_b])])
    wio_t = jnp.transpose(wio_g, (0, 2, 1)).reshape(PROJ, D)
    woo = woo_g.reshape(D, D)
    woo_t = woo.T
    p1, m1, g2, loss_vec = _odd_forward(h1, x1, tgt, wio_g, ln_g, ln_b, ws_b, b_full, woo, gpost1, TS_FWD)

    dp1, g1, dwoo, dws, dbs, dlg, dlb, dgpost1, dgpre1 = _odd_backward(
        g2, m1, p1, x1, ln_g, ln_b, ws_b, wst_b, b_full, woo_t, wio_t, gpost1, gpre1, TS_FWD)
    by_owner = lambda dwo: dwo.reshape(N_DEV, SHARD_OUT, D)
    (dwio_parts, dwoo_parts, dlg_parts, dlb_parts,
     dws_g, dbs_g, gpost1_g, gpre1_g, loss_g) = _weight_grad_scatter(
        h1, dp1, TS_WGRAD, "odd_weight_grad",
        sides=[(_Scatter, [by_owner(dwoo), dlg.reshape(N_DEV, 1, 128), dlb.reshape(N_DEV, 1, 128)]),
               (_Gather, [dws.astype(BF16), dbs, dgpost1, dgpre1, loss_vec])])
    dp0, gx, dwoe, dcw, dpw, dps, dgpost0, dgpre0 = _even_backward_fused(
        g1, m0, p0, xs, conv_w, pool_w_b, pool_wt_b, even_pool_scale, woe_t, wie_t, gpost0, gpre0, TS_FWD)
    dcw_by_owner = jnp.transpose(dcw[0:3].reshape(3, N_DEV, 64), (1, 0, 2))
    dwie_parts, dwoe_parts, dcw_parts, dpw_g, dps_g, gpost0_g, gpre0_g = _weight_grad_scatter(
        h0, dp0, TS_WGRAD, "even_weight_grad",
        sides=[(_Scatter, [by_owner(dwoe), dcw_by_owner]), (_Gather, [dpw.astype(BF16), dps, dgpost0, dgpre0])])

    ((g_wie, d_wie, nm_wie, nv_wie), (g_wio, d_wio, nm_wio, nv_wio), (g_woe, d_woe, nm_woe, nv_woe),
     (g_woo, d_woo, nm_woo, nv_woo)) = _sum_update(
        [dwie_parts, dwio_parts, dwoe_parts, dwoo_parts],
        [(even_w_in[0], m_even_w_in[0], v_even_w_in[0]), (odd_w_in[0], m_odd_w_in[0], v_odd_w_in[0]),
         (even_w_out[0], m_even_w_out[0], v_even_w_out[0]), (odd_w_out[0], m_odd_w_out[0], v_odd_w_out[0])])

    partials = [gpre0_g, gpre1_g, gpost0_g, gpost1_g, dpw_g, dps_g, dws_g, dbs_g, dcw_parts, dlg_parts, dlb_parts]
    row_of = [(0, 0), (0, 1), (1, 0), (1, 1), (2, 0), (3, 0), (4, 0), (5, 0), (6, 0), (7, 0), (8, 0)]
    weights = [(pre_norm, m_pre_norm, v_pre_norm), (post_norm, m_post_norm, v_post_norm),
               (even_pool_w[0], m_even_pool_w[0], v_even_pool_w[0]),
               (even_pool_scale, m_even_pool_scale, v_even_pool_scale),
               (odd_w_s[0], m_odd_w_s[0], v_odd_w_s[0]), (odd_b_s[0], m_odd_b_s[0], v_odd_b_s[0]),
               (even_conv_w[0], m_even_conv_w[0], v_even_conv_w[0]),
               (odd_ln_g, m_odd_ln_g, v_odd_ln_g), (odd_ln_b, m_odd_ln_b, v_odd_ln_b)]
    small = _small_sum_update(partials, row_of, weights, loss_g)
    loss = small[0][0, 0]

    def leaves(k, big):
        pre, post, pw, psc, ws, bs, cw, lg, lb = [small[1 + 4 * i + k] for i in range(len(weights))]
        wie_k, woe_k, wio_k, woo_k = big
        return [pre, post, wie_k[None], cw[None], pw[None], psc, woe_k[None], wio_k[None], lg, lb, ws[None], bs[None],
                woo_k[None]]

    outs = [loss, gx.reshape(1, S, D)]
    outs += leaves(0, (g_wie, g_woe, g_wio, g_woo))
    outs += leaves(1, (d_wie, d_woe, d_wio, d_woo))
    outs += leaves(2, (nm_wie, nm_woe, nm_wio, nm_woo))
    outs += leaves(3, (nv_wie, nv_woe, nv_wio, nv_woo))
    return tuple(outs)
```

```python
import jax
import jax.numpy as jnp
from jax import lax
from jax.experimental import pallas as pl
from jax.experimental.pallas import tpu as pltpu

F32 = jnp.float32
BF16 = jnp.bfloat16
MESH = pl.DeviceIdType.MESH

D = 1024
EPS = 1e-6
A_W = 512
B_W = 512
POOL_WINDOWS = (2, 4, 8, 16)
GROUP = 128
N_GROUPS = 4
N_HEADS = 8
HALO = 16
PROJ = 3072
N_DEV = 8
N_CHIPS = 4
SHARD_IN = PROJ // N_DEV
SHARD_OUT = D // N_DEV

ADAM_LR = 0.001
ADAM_B1 = 0.9
ADAM_B2 = 0.999
ADAM_EPS = 1e-08
ADAM_WD = 0.01
ADAM_STEP = 10

VMEM_LIMIT = 54 * 1024 * 1024

TS_PRENORM = 1024
TS_FWD = 512
TS_BWD = 256
TS_WGRAD = 2048
SUB_ROWS = 256


def _params(n_grid=1, vmem=VMEM_LIMIT):
    return pltpu.CompilerParams(dimension_semantics=("arbitrary",) * n_grid, vmem_limit_bytes=vmem)


def _full(shape):
    return pl.BlockSpec(shape, lambda *_: (0,) * len(shape))


def _resident(shape):
    return pl.BlockSpec(shape, lambda *_: (0,) * len(shape), pipeline_mode=pl.Buffered(1))


def _load_columns(blocks_ref, whole_ref, sems):
    copies = [pltpu.make_async_copy(blocks_ref.at[k], whole_ref.at[:, k * SHARD_IN:(k + 1) * SHARD_IN], sems.at[k])
              for k in range(N_DEV)]
    for cp in copies:
        cp.start()
    for cp in copies:
        cp.wait()


def _sigmoid(z):
    return jax.nn.sigmoid(z)


def _dsilu(z, s):
    return s * (1.0 + z * (1.0 - s))


def _dot(a, b):
    return jnp.dot(a, b, preferred_element_type=F32)


def _dot_t0(a, b):
    return lax.dot_general(a, b, (((0,), (0,)), ((), ())), preferred_element_type=F32)


def _dot_t1(a, b):
    return lax.dot_general(a, b, (((1,), (1,)), ((), ())), preferred_element_type=F32)


def _rms(x):
    return lax.rsqrt(jnp.mean(x * x, axis=-1, keepdims=True) + EPS)


def _shift_down(a, k):
    return pltpu.roll(a, k, 0)


def _shift_up(a, k):
    return pltpu.roll(a, a.shape[0] - k, 0)


def _mesh_position():
    return lax.axis_index("x"), lax.axis_index("y"), lax.axis_index("c")


def _index(pos):
    return 4 * pos[0] + 2 * pos[1] + pos[2]


def _peer(pos, d):
    x, y, c = pos
    return (1 - x if d & 4 else x, 1 - y if d & 2 else y, 1 - c if d & 1 else c)


def _remote(src, dst, send_sems, recv_sems, k, to):
    return pltpu.make_async_remote_copy(src_ref=src, dst_ref=dst, send_sem=send_sems.at[k], recv_sem=recv_sems.at[k],
                                        device_id=to, device_id_type=MESH)


class _Gather:
    def __init__(self, srcs, dsts, send_sems, recv_sems, relays=False):
        x, y, c = _mesh_position()
        n = len(srcs)
        me, sibling = (x, y, c), (x, y, 1 - c)
        near = [(1 - x, y), (x, 1 - y)]
        far = (1 - x, 1 - y)
        x_first = c == 0
        relay_from = (jnp.where(x_first, 1 - x, x), jnp.where(x_first, y, 1 - y))
        relay_to = (jnp.where(x_first, x, 1 - x), jnp.where(x_first, 1 - y, y))

        def copy(k, t, block, to, src=None):
            slot = dsts[t].at[_index(block)]
            return _remote(slot if src is None else src, slot, send_sems, recv_sems, k * n + t, to)

        tensors = range(n)
        self.local = [pltpu.make_async_copy(srcs[t], dsts[t].at[_index(me)], recv_sems.at[7 * n + t]) for t in tensors]
        self.first = [copy(1 + j, t, me, chip + (c,), srcs[t]) for t in tensors for j, chip in enumerate(near)]
        self.first += [copy(0, t, me, sibling, srcs[t]) for t in tensors]
        self.near_in = [copy(1 + j, t, chip + (c,), me) for t in tensors for j, chip in enumerate(near)]
        self.near_passed = [copy(4 + j, t, chip + (c,), sibling) for t in tensors for j, chip in enumerate(near)]
        if relays:
            self.relayed = [copy(3, t, relay_from + (c,), relay_to + (c,)) for t in tensors]
        else:
            self.relayed = []
            self.first += [copy(3, t, me, far + (c,), srcs[t]) for t in tensors]
        self.relays = relays
        self.far_in = [copy(3, t, far + (c,), me) for t in tensors]
        self.far_passed = [copy(6, t, far + (c,), sibling) for t in tensors]
        self.d2d_in = [copy(0, t, sibling, me) for t in tensors]
        self.d2d_in += [copy(4 + j, t, chip + (1 - c,), me) for t in tensors for j, chip in enumerate(near + [far])]

    def start(self):
        for cp in self.local + self.first:
            cp.start()

    def _pass_on(self, landed_blocks, onward_copies):
        for landed, onward in zip(landed_blocks, onward_copies):
            landed.wait_recv()
            onward.start()

    def relay(self):
        if self.relays:
            for t, cp in enumerate(self.relayed):
                self._pass_on(self.near_in[2 * t:2 * t + 2], self.near_passed[2 * t:2 * t + 2])
                cp.start()

    def middle(self):
        if not self.relays:
            self._pass_on(self.near_in, self.near_passed)
        self._pass_on(self.far_in, self.far_passed)

    def finish(self):
        for cp in self.d2d_in:
            cp.wait_recv()
        for cp in self.first + self.near_passed + self.relayed + self.far_passed:
            cp.wait_send()
        for cp in self.local:
            cp.wait()


class _Scatter:
    def __init__(self, srcs, dsts, send_sems, recv_sems):
        pos = _mesh_position()
        n = len(srcs)
        self.local = [pltpu.make_async_copy(srcs[t].at[_index(pos)], dsts[t].at[0], recv_sems.at[t]) for t in range(n)]
        self.remote = []
        for d in range(1, N_DEV):
            to = _peer(pos, d)
            self.remote += [_remote(srcs[t].at[_index(to)], dsts[t].at[d], send_sems, recv_sems, d * n + t, to)
                            for t in range(n)]

    def start(self):
        for cp in self.local + self.remote:
            cp.start()

    def relay(self):
        pass

    def middle(self):
        pass

    def finish(self):
        for cp in self.local:
            cp.wait()
        for cp in self.remote:
            cp.wait_recv()
        for cp in self.remote:
            cp.wait_send()


HBM_SPEC = pl.BlockSpec(memory_space=pltpu.HBM)
MIDDLE_STEPS_BEFORE_END = 4


def _landing_shape(exchange, src):
    return (N_DEV,) + src.shape if exchange is _Gather else src.shape


def _call(body, *, name, grid, in_specs, out_specs, out_shape, scratch_shapes, args, sides=()):
    params = _params(len(grid))
    if not sides:
        return pl.pallas_call(body, name=name, grid=grid, in_specs=in_specs, out_specs=out_specs, out_shape=out_shape,
                              scratch_shapes=scratch_shapes, compiler_params=params)(*args)
    n_in, n_out, n_scratch = len(in_specs), len(out_specs), len(scratch_shapes)
    counts = [len(srcs) for _, srcs in sides]
    ns = sum(counts)
    total = 1
    for g in grid:
        total *= g
    assert total // 2 < total - MIDDLE_STEPS_BEFORE_END
    side_args, side_shapes, side_sems = [], [], []
    for exchange, srcs in sides:
        side_args += list(srcs)
        side_shapes += [jax.ShapeDtypeStruct(_landing_shape(exchange, s), s.dtype) for s in srcs]
        side_sems += [pltpu.SemaphoreType.DMA((N_DEV * len(srcs),))] * 2

    def wrapped(*refs):
        ins, side_in = refs[:n_in], refs[n_in:n_in + ns]
        outs = refs[n_in + ns:n_in + ns + n_out]
        side_out = refs[n_in + ns + n_out:n_in + 2 * ns + n_out]
        rest = refs[n_in + 2 * ns + n_out:]
        scratch, sems = rest[:n_scratch], rest[n_scratch:]
        step = pl.program_id(0)
        for axis in range(1, len(grid)):
            step = step * grid[axis] + pl.program_id(axis)

        def exchanges():
            built, at = [], 0
            for k, (exchange, _) in enumerate(sides):
                built.append(exchange(side_in[at:at + counts[k]], side_out[at:at + counts[k]],
                                      sems[2 * k], sems[2 * k + 1]))
                at += counts[k]
            return built

        @pl.when(step == 0)
        def _():
            for ex in exchanges():
                ex.start()

        @pl.when(step == total // 2)
        def _():
            for ex in exchanges():
                ex.relay()

        @pl.when(step == total - MIDDLE_STEPS_BEFORE_END)
        def _():
            for ex in exchanges():
                ex.middle()

        body(*ins, *outs, *scratch)

        @pl.when(step == total - 1)
        def _():
            for ex in exchanges():
                ex.finish()

    return pl.pallas_call(
        wrapped, name=name, grid=grid, in_specs=list(in_specs) + [HBM_SPEC] * ns,
        out_specs=list(out_specs) + [HBM_SPEC] * ns, out_shape=list(out_shape) + side_shapes,
        scratch_shapes=list(scratch_shapes) + side_sems, compiler_params=params)(*args, *side_args)


def _prenorm_gather(x, g, wie, small, woe, wio, woo, ts):
    S = x.shape[0]
    nt = S // ts

    def body(x_ref, g_ref, wie_ref, sm_ref, woe_ref, wio_ref, woo_ref,
             h_ref, wie_out, sm_out, woe_out, wio_b, woo_b, cast_ref, small_ref, cast2_ref, send_sems, recv_sems):
        step = pl.program_id(0)
        gather = lambda: _Gather([cast_ref, small_ref, cast2_ref], [wie_out, sm_out, woe_out], send_sems, recv_sems,
                                 relays=True)

        @pl.when(step == 0)
        def _():
            cast_ref[...] = wie_ref[...].astype(BF16)
            small_ref[...] = sm_ref[...]
            cast2_ref[...] = woe_ref[...].astype(BF16)
            gather().start()
            wio_b[...] = wio_ref[...].astype(BF16)
            woo_b[...] = woo_ref[...].astype(BF16)

        xv = x_ref[...]
        h_ref[...] = ((xv * _rms(xv)) * g_ref[...]).astype(BF16)

        @pl.when(step == 1)
        def _():
            gather().relay()

        @pl.when(step == nt - 1)
        def _():
            exchange = gather()
            exchange.middle()
            exchange.finish()

    tile = pl.BlockSpec((ts, D), lambda i: (i, 0))
    sems = pltpu.SemaphoreType.DMA((N_DEV * 3,))
    shards = [wie, small, woe, wio, woo]
    return pl.pallas_call(
        body, name="prenorm_gather", grid=(nt,),
        in_specs=[tile, _full((1, D))] + [_full(a.shape) for a in shards],
        out_specs=[tile, HBM_SPEC, HBM_SPEC, HBM_SPEC] + [_full(a.shape) for a in shards[3:]],
        out_shape=[jax.ShapeDtypeStruct((S, D), BF16), jax.ShapeDtypeStruct((N_DEV,) + wie.shape, BF16),
                   jax.ShapeDtypeStruct((N_DEV,) + small.shape, F32), jax.ShapeDtypeStruct((N_DEV,) + woe.shape, BF16)]
        + [jax.ShapeDtypeStruct(a.shape, BF16) for a in shards[3:]],
        scratch_shapes=[pltpu.VMEM(wie.shape, BF16), pltpu.VMEM(small.shape, F32), pltpu.VMEM(woe.shape, BF16),
                        sems, sems],
        compiler_params=_params(),
    )(x, g, wie, small, woe, wio, woo)


def _even_mixer(proj, halo, row0, cw_ref, pw_ref):
    xa, gb, gc, za, xp, zp = proj
    nr = xa.shape[0]
    ha = gc * xa
    ha_ext = jnp.concatenate([halo[0], ha], axis=0)
    ha_m1 = _shift_down(ha_ext, 1)[HALO:]
    ha_m2 = _shift_down(ha_ext, 2)[HALO:]
    conv = cw_ref[2:3, :] * ha + cw_ref[1:2, :] * ha_m1 + cw_ref[0:1, :] * ha_m2
    sig_a = _sigmoid(za)
    silu_a = za * sig_a

    xp_ext = jnp.concatenate([halo[1], xp], axis=0)
    pos = row0 + lax.broadcasted_iota(jnp.int32, (nr, 1), 0)
    pooled, inv_cnt, mixed = [], [], []
    for g, w in enumerate(POOL_WINDOWS):
        cols = slice(g * GROUP, (g + 1) * GROUP)
        s = xp_ext[:, cols]
        k = 1
        while k < w:
            s = s + _shift_down(s, k)
            k *= 2
        inv = 1.0 / jnp.minimum(pos + 1, w).astype(F32)
        pg = s[HALO:] * inv - xp[:, cols]
        pooled.append(pg)
        inv_cnt.append(inv)
        mixed.append(_dot(pg.astype(BF16), pw_ref[g]))
    mixed = jnp.concatenate(mixed, axis=1)
    sig_b = _sigmoid(zp)
    silu_b = zp * sig_b
    return dict(xa=xa, gb=gb, gc=gc, za=za, xp=xp, zp=zp, ha=ha, ha_m1=ha_m1, ha_m2=ha_m2, conv=conv,
                sig_a=sig_a, silu_a=silu_a, pooled=pooled, inv_cnt=inv_cnt, mixed=mixed, sig_b=sig_b,
                silu_b=silu_b)


def _columns(a):
    return [a[:, k * A_W:(k + 1) * A_W].astype(F32) for k in range(6)]


def _even_forward_fused(h, x, win_b, cw, pw_b, ps, wout_b, gpost, gpre_next, ts, sides=()):
    S = x.shape[0]

    def body(h_ref, x_ref, wi_blocks, cw_ref, pw_ref, ps_ref, wo_ref, gp_ref, gn_ref,
             p_ref, m_ref, x1_ref, h1_ref, mix_ref, carry_ref, wi_ref, wi_sems):
        i = pl.program_id(0)

        @pl.when(i == 0)
        def _():
            carry_ref[...] = jnp.zeros_like(carry_ref)
            _load_columns(wi_blocks, wi_ref, wi_sems)

        for r0 in range(0, ts, SUB_ROWS):
            rows = slice(r0, r0 + SUB_ROWS)
            proj = _dot(h_ref[rows, :], wi_ref[...])
            p_ref[rows, :] = proj.astype(BF16)
            fw = _even_mixer(_columns(proj), (carry_ref[:, 0:A_W], carry_ref[:, A_W:D]), i * ts + r0, cw_ref, pw_ref)
            carry_ref[:, 0:A_W] = fw["ha"][SUB_ROWS - HALO:]
            carry_ref[:, A_W:D] = fw["xp"][SUB_ROWS - HALO:]
            mix_ref[rows, 0:A_W] = (fw["gb"] * fw["conv"] * fw["silu_a"]).astype(BF16)
            mix_ref[rows, A_W:D] = (fw["mixed"] * ps_ref[...] * fw["silu_b"]).astype(BF16)
            m = _dot(mix_ref[rows, :], wo_ref[...])
            m_ref[rows, :] = m.astype(BF16)
            x1 = x_ref[rows, :] + (m * _rms(m)) * gp_ref[...]
            x1_ref[rows, :] = x1
            h1_ref[rows, :] = ((x1 * _rms(x1)) * gn_ref[...]).astype(BF16)

    tile = pl.BlockSpec((ts, D), lambda i: (i, 0))
    return _call(
        body, name="even_forward", grid=(S // ts,),
        in_specs=[tile, tile, HBM_SPEC, _full((3, A_W)), _full((N_GROUPS, GROUP, GROUP)), _full((1, B_W)),
                  _resident((D, D)), _full((1, D)), _full((1, D))],
        out_specs=[pl.BlockSpec((ts, PROJ), lambda i: (i, 0)), tile, tile, tile],
        out_shape=[jax.ShapeDtypeStruct((S, PROJ), BF16), jax.ShapeDtypeStruct((S, D), BF16),
                   jax.ShapeDtypeStruct((S, D), F32), jax.ShapeDtypeStruct((S, D), BF16)],
        scratch_shapes=[pltpu.VMEM((ts, D), BF16), pltpu.VMEM((HALO, D), F32), pltpu.VMEM((D, PROJ), BF16),
                        pltpu.SemaphoreType.DMA((N_DEV,))],
        args=(h, x, win_b, cw, pw_b, ps, wout_b, gpost, gpre_next), sides=sides)


def _even_backward_fused(g1, m0, p, x, cw, pw_b, pwt_b, ps, woutt_b, wint_b, gpost, gpre, ts):
    S = g1.shape[0]
    nt = S // ts

    def body(g_ref, m_ref, p_ref, halo_ref, x_ref, cw_ref, pw_ref, pwt_ref, ps_ref, wot_ref, wit_ref, gp_ref, gn_ref,
             dp_ref, dx_ref, dwob_ref, dcw_ref, dpw_ref, dps_ref, dgp_ref, dgn_ref,
             carry_ref, mix_ref, dm_ref, dwo_ref):
        step = pl.program_id(0)
        i = nt - 1 - step

        @pl.when(step == 0)
        def _():
            carry_ref[...] = jnp.zeros_like(carry_ref)
            dwo_ref[...] = jnp.zeros_like(dwo_ref)
            dcw_ref[...] = jnp.zeros_like(dcw_ref)
            dpw_ref[...] = jnp.zeros_like(dpw_ref)
            dps_ref[...] = jnp.zeros_like(dps_ref)
            dgp_ref[...] = jnp.zeros_like(dgp_ref)
            dgn_ref[...] = jnp.zeros_like(dgn_ref)

        ps_v = ps_ref[...]
        sums = [None] * 6
        add = lambda k, part: sums.__setitem__(k, part if sums[k] is None else sums[k] + part)
        for r0 in range(ts - SUB_ROWS, -1, -SUB_ROWS):
            rows = slice(r0, r0 + SUB_ROWS)
            g = g_ref[rows, :]
            m = m_ref[rows, :].astype(F32)
            q = _rms(m)
            n = m * q
            add(0, jnp.sum(g * n, axis=0, keepdims=True))
            dn = g * gp_ref[...]
            dm = q * (dn - n * jnp.mean(dn * n, axis=-1, keepdims=True))
            dm_ref[rows, :] = dm.astype(BF16)
            dmix = _dot(dm_ref[rows, :], wot_ref[...])
            dya = dmix[:, 0:A_W]
            dyb = dmix[:, A_W:D]

            if r0 == 0:
                before = _columns(halo_ref[...])
                keep = jnp.where(i == 0, 0.0, 1.0).astype(F32)
                halo = (before[2] * before[0] * keep, before[4] * keep)
            else:
                before = _columns(p_ref[r0 - HALO:r0, :])
                halo = (before[2] * before[0], before[4])
            fw = _even_mixer(_columns(p_ref[rows, :]), halo, i * ts + r0, cw_ref, pw_ref)
            mix_ref[rows, 0:A_W] = (fw["gb"] * fw["conv"] * fw["silu_a"]).astype(BF16)
            mix_ref[rows, A_W:D] = (fw["mixed"] * ps_v * fw["silu_b"]).astype(BF16)

            t = dya * fw["gb"]
            dconv = t * fw["silu_a"]
            dgb = dya * fw["conv"] * fw["silu_a"]
            dza = t * fw["conv"] * _dsilu(fw["za"], fw["sig_a"])
            add(2, jnp.sum(dconv * fw["ha"], axis=0, keepdims=True))
            add(3, jnp.sum(dconv * fw["ha_m1"], axis=0, keepdims=True))
            add(4, jnp.sum(dconv * fw["ha_m2"], axis=0, keepdims=True))
            dconv_ext = jnp.concatenate([dconv, carry_ref[:, 0:A_W]], axis=0)
            dha = (cw_ref[2:3, :] * dconv + cw_ref[1:2, :] * _shift_up(dconv_ext, 1)[:SUB_ROWS]
                   + cw_ref[0:1, :] * _shift_up(dconv_ext, 2)[:SUB_ROWS])
            dgc = dha * fw["xa"]
            dxa = dha * fw["gc"]

            u = dyb * fw["mixed"]
            add(1, jnp.sum(u * fw["silu_b"], axis=0, keepdims=True))
            dzp = u * ps_v * _dsilu(fw["zp"], fw["sig_b"])
            dmixed = (dyb * ps_v * fw["silu_b"]).astype(BF16)
            dxp, e_first = [], []
            for gi, w in enumerate(POOL_WINDOWS):
                cols = slice(gi * GROUP, (gi + 1) * GROUP)
                dmg = dmixed[:, cols]
                dpooled = _dot(dmg, pwt_ref[gi])
                dpw_ref[gi] += _dot_t0(fw["pooled"][gi].astype(BF16), dmg)
                e = dpooled * fw["inv_cnt"][gi]
                e_first.append(e[0:HALO])
                s = jnp.concatenate([e, carry_ref[:, A_W + gi * GROUP:A_W + (gi + 1) * GROUP]], axis=0)
                k = 1
                while k < w:
                    s = s + _shift_up(s, k)
                    k *= 2
                dxp.append(s[:SUB_ROWS] - dpooled)
            carry_ref[:, 0:A_W] = dconv[0:HALO]
            carry_ref[:, A_W:D] = jnp.concatenate(e_first, axis=1)

            dp_ref[rows, 0:512] = dxa.astype(BF16)
            dp_ref[rows, 512:1024] = dgb.astype(BF16)
            dp_ref[rows, 1024:1536] = dgc.astype(BF16)
            dp_ref[rows, 1536:2048] = dza.astype(BF16)
            dp_ref[rows, 2048:2560] = jnp.concatenate(dxp, axis=1).astype(BF16)
            dp_ref[rows, 2560:3072] = dzp.astype(BF16)

            dh = _dot(dp_ref[rows, :], wit_ref[...])
            xv = x_ref[rows, :]
            r = _rms(xv)
            xn = xv * r
            add(5, jnp.sum(dh * xn, axis=0, keepdims=True))
            dxn = dh * gn_ref[...]
            dx_ref[rows, :] = g + r * (dxn - xn * jnp.mean(dxn * xn, axis=-1, keepdims=True))

        dwo_ref[...] += _dot_t0(mix_ref[...], dm_ref[...])
        dgp_ref[...] += sums[0]
        dps_ref[...] += sums[1]
        dcw_ref[2:3, :] += sums[2]
        dcw_ref[1:2, :] += sums[3]
        dcw_ref[0:1, :] += sums[4]
        dgn_ref[...] += sums[5]

        @pl.when(step == nt - 1)
        def _():
            dwob_ref[...] = dwo_ref[...].astype(BF16)

    rev = lambda s: (nt - 1 - s, 0)
    tile = pl.BlockSpec((ts, D), rev)
    wide = pl.BlockSpec((ts, PROJ), rev)
    bpt = ts // HALO
    halo_map = lambda s: (jnp.maximum((nt - 1 - s) * bpt - 1, 0), 0)
    groups = _full((N_GROUPS, GROUP, GROUP))
    vec = _full((1, D))
    return pl.pallas_call(
        body, name="even_backward", grid=(nt,),
        in_specs=[tile, tile, wide, pl.BlockSpec((HALO, PROJ), halo_map), tile, _full((3, A_W)), groups, groups,
                  _full((1, B_W)), _resident((D, D)), _resident((PROJ, D)), vec, vec],
        out_specs=[wide, tile, _full((D, D)), _full((8, A_W)), groups, _full((1, B_W)), vec, vec],
        out_shape=[jax.ShapeDtypeStruct((S, PROJ), BF16), jax.ShapeDtypeStruct((S, D), F32),
                   jax.ShapeDtypeStruct((D, D), BF16), jax.ShapeDtypeStruct((8, A_W), F32),
                   jax.ShapeDtypeStruct((N_GROUPS, GROUP, GROUP), F32), jax.ShapeDtypeStruct((1, B_W), F32),
                   jax.ShapeDtypeStruct((1, D), F32), jax.ShapeDtypeStruct((1, D), F32)],
        scratch_shapes=[pltpu.VMEM((HALO, D), F32), pltpu.VMEM((ts, D), BF16), pltpu.VMEM((ts, D), BF16),
                        pltpu.VMEM((D, D), F32)],
        compiler_params=_params(),
    )(g1, m0, p, p, x, cw, pw_b, pwt_b, ps, woutt_b, wint_b, gpost, gpre)


def _gate_matmuls(w_ref, in_ref, out_ref, chunks, bias_ref=None):
    for h in range(N_HEADS):
        cols = slice(h * GROUP, (h + 1) * GROUP)
        wide = jnp.concatenate([in_ref[chunk, cols] for chunk in chunks], axis=1)
        res = _dot(w_ref[h], wide)
        for k, chunk in enumerate(chunks):
            part = res[:, k * GROUP:(k + 1) * GROUP]
            out_ref[chunk, cols] = part if bias_ref is None else part + bias_ref[h]


def _odd_forward_parts(v, lg_ref, lb_ref, wt_ref, bfull_ref, vln_ref, sv_ref, r0, nr):
    rows = slice(r0, r0 + nr)
    mu = jnp.mean(v, axis=-1, keepdims=True)
    vc = v - mu
    rstd = lax.rsqrt(jnp.mean(vc * vc, axis=-1, keepdims=True) + EPS)
    vh = vc * rstd
    vln_ref[rows, :] = (vh * lg_ref[...] + lb_ref[...]).astype(BF16)
    _gate_matmuls(wt_ref, vln_ref, sv_ref, [slice(r0 + k * GROUP, r0 + (k + 1) * GROUP) for k in range(nr // GROUP)],
                  bfull_ref)
    return vh, rstd


def _odd_forward(h, x1, tgt, win_b, lg, lb, wt_b, bfull, wout_b, gpost, ts):
    S = x1.shape[0]

    def body(h_ref, x_ref, t_ref, wi_blocks, lg_ref, lb_ref, wt_ref, bfull_ref, wo_ref, gp_ref,
             p_ref, m_ref, g_ref, loss_ref, vln_ref, sv_ref, y_ref, wi_ref, wi_sems):
        @pl.when(pl.program_id(0) == 0)
        def _():
            loss_ref[...] = jnp.zeros_like(loss_ref)
            _load_columns(wi_blocks, wi_ref, wi_sems)

        loss = None
        for r0 in range(0, ts, SUB_ROWS):
            rows = slice(r0, r0 + SUB_ROWS)
            proj = _dot(h_ref[rows, :], wi_ref[...])
            p_ref[rows, :] = proj.astype(BF16)
            u, v, z = proj[:, 0:D], proj[:, D:2 * D], proj[:, 2 * D:3 * D]
            _odd_forward_parts(v, lg_ref, lb_ref, wt_ref, bfull_ref, vln_ref, sv_ref, r0, SUB_ROWS)
            y_ref[rows, :] = (u * sv_ref[rows, :] * (z * _sigmoid(z))).astype(BF16)
            m = _dot(y_ref[rows, :], wo_ref[...])
            m_ref[rows, :] = m.astype(BF16)
            x2 = x_ref[rows, :] + (m * _rms(m)) * gp_ref[...]
            err = x2 - t_ref[rows, :]
            g_ref[rows, :] = (err * (1.0 / D)).astype(BF16)
            part = jnp.sum(err * err, axis=0, keepdims=True)
            loss = part if loss is None else loss + part
        loss_ref[...] += loss

    tile = pl.BlockSpec((ts, D), lambda i: (i, 0))
    small = _full((N_HEADS, GROUP, GROUP))
    return pl.pallas_call(
        body, name="odd_forward", grid=(S // ts,),
        in_specs=[pl.BlockSpec((ts, D), lambda i: (i, 0)), tile, tile, HBM_SPEC, _full((1, D)), _full((1, D)),
                  small, small, _full((D, D)), _full((1, D))],
        out_specs=[pl.BlockSpec((ts, PROJ), lambda i: (i, 0)), tile, tile, _full((1, D))],
        out_shape=[jax.ShapeDtypeStruct((S, PROJ), BF16), jax.ShapeDtypeStruct((S, D), BF16),
                   jax.ShapeDtypeStruct((S, D), BF16), jax.ShapeDtypeStruct((1, D), F32)],
        scratch_shapes=[pltpu.VMEM((ts, D), BF16), pltpu.VMEM((ts, D), F32), pltpu.VMEM((ts, D), BF16),
                        pltpu.VMEM((D, PROJ), BF16), pltpu.SemaphoreType.DMA((N_DEV,))],
        compiler_params=_params(),
    )(h, x1, tgt, win_b, lg, lb, wt_b, bfull, wout_b, gpost)


def _odd_backward(g2, m1, p, x1, lg, lb, wt_b, wtt_b, bfull, woutt_b, wint_b, gpost, gpre, ts):
    S = g2.shape[0]
    nt = S // ts
    sub = ts // 2

    def body(g_ref, m_ref, p_ref, x_ref, lg_ref, lb_ref, wt_ref, wtt_ref, bfull_ref, wot_ref, wit_ref, gp_ref, gn_ref,
             dp_ref, dx_ref, dwob_ref, dws_ref, dbs_ref, dlg_ref, dlb_ref, dgp_ref, dgn_ref,
             vln_ref, sv_ref, dsvb_ref, dvln_ref, dsum_ref, y_ref, dm_ref, dwo_ref):
        step = pl.program_id(0)

        @pl.when(step == 0)
        def _():
            dwo_ref[...] = jnp.zeros_like(dwo_ref)
            dws_ref[...] = jnp.zeros_like(dws_ref)
            dsum_ref[...] = jnp.zeros_like(dsum_ref)
            dlg_ref[...] = jnp.zeros_like(dlg_ref)
            dlb_ref[...] = jnp.zeros_like(dlb_ref)
            dgp_ref[...] = jnp.zeros_like(dgp_ref)
            dgn_ref[...] = jnp.zeros_like(dgn_ref)

        tril = (lax.broadcasted_iota(jnp.int32, (GROUP, GROUP), 0)
                >= lax.broadcasted_iota(jnp.int32, (GROUP, GROUP), 1))
        sums = [None] * 4
        add = lambda k, part: sums.__setitem__(k, part if sums[k] is None else sums[k] + part)

        def post_norm(c):
            g = g_ref[c["rows"], :].astype(F32)
            m = m_ref[c["rows"], :].astype(F32)
            q = _rms(m)
            n = m * q
            add(0, jnp.sum(g * n, axis=0, keepdims=True))
            dn = g * gp_ref[...]
            dm = q * (dn - n * jnp.mean(dn * n, axis=-1, keepdims=True))
            dm_ref[c["rows"], :] = dm.astype(BF16)

        def out_projection(c):
            c["dy"] = _dot(dm_ref[c["rows"], :], wot_ref[...])

        def layernorm(c):
            v = p_ref[c["rows"], D:2 * D].astype(F32)
            mu = jnp.mean(v, axis=-1, keepdims=True)
            vc = v - mu
            c["rstd"] = lax.rsqrt(jnp.mean(vc * vc, axis=-1, keepdims=True) + EPS)
            c["vh"] = vc * c["rstd"]
            vln_ref[c["rows"], :] = (c["vh"] * lg_ref[...] + lb_ref[...]).astype(BF16)

        def gate_matmuls(c):
            _gate_matmuls(wt_ref, vln_ref, sv_ref, c["chunks"], bfull_ref)

        def gating(c):
            rows = c["rows"]
            u = p_ref[rows, 0:D].astype(F32)
            z = p_ref[rows, 2 * D:3 * D].astype(F32)
            sv = sv_ref[rows, :]
            sig = _sigmoid(z)
            sz = z * sig
            y_ref[rows, :] = (u * sv * sz).astype(BF16)
            dy = c.pop("dy")
            t = dy * sz
            dsv = t * u
            dsvb_ref[rows, :] = dsv.astype(BF16)
            for k in range(sub // GROUP):
                dsum_ref[...] += dsv[k * GROUP:(k + 1) * GROUP]
            dp_ref[rows, 0:D] = (t * sv).astype(BF16)
            dp_ref[rows, 2 * D:3 * D] = (dy * u * sv * _dsilu(z, sig)).astype(BF16)

        def gate_backward_matmuls(c):
            for chunk in c["chunks"]:
                for h in range(N_HEADS):
                    cols = slice(h * GROUP, (h + 1) * GROUP)
                    dvln_ref[chunk, cols] = _dot(wtt_ref[h], dsvb_ref[chunk, cols])
                    dws_ref[h] += jnp.where(tril, _dot_t1(dsvb_ref[chunk, cols], vln_ref[chunk, cols]), 0.0)

        def layernorm_backward(c):
            vh, rstd = c.pop("vh"), c.pop("rstd")
            dvln = dvln_ref[c["rows"], :]
            add(1, jnp.sum(dvln * vh, axis=0, keepdims=True))
            add(2, jnp.sum(dvln, axis=0, keepdims=True))
            dvh = dvln * lg_ref[...]
            dv = rstd * (dvh - jnp.mean(dvh, axis=-1, keepdims=True)
                         - vh * jnp.mean(dvh * vh, axis=-1, keepdims=True))
            dp_ref[c["rows"], D:2 * D] = dv.astype(BF16)

        def in_projection(c):
            c["dh"] = _dot(dp_ref[c["rows"], :], wit_ref[...])

        def pre_norm(c):
            dh = c.pop("dh")
            xv = x_ref[c["rows"], :]
            r = _rms(xv)
            xn = xv * r
            add(3, jnp.sum(dh * xn, axis=0, keepdims=True))
            dxn = dh * gn_ref[...]
            dx_ref[c["rows"], :] = (g_ref[c["rows"], :].astype(F32)
                                    + r * (dxn - xn * jnp.mean(dxn * xn, axis=-1, keepdims=True)))

        phases = [post_norm, out_projection, layernorm, gate_matmuls, gating, gate_backward_matmuls,
                  layernorm_backward, in_projection, pre_norm]
        groups = [dict(rows=slice(r0, r0 + sub),
                       chunks=[slice(r0 + k * GROUP, r0 + (k + 1) * GROUP) for k in range(sub // GROUP)])
                  for r0 in range(0, ts, sub)]
        for group in groups:
            for phase in phases:
                phase(group)

        dwo_ref[...] += _dot_t0(y_ref[...], dm_ref[...])
        dgp_ref[...] += sums[0]
        dlg_ref[...] += sums[1]
        dlb_ref[...] += sums[2]
        dgn_ref[...] += sums[3]

        @pl.when(step == nt - 1)
        def _():
            dwob_ref[...] = dwo_ref[...].astype(BF16)
            ones = jnp.ones((8, GROUP), F32)
            for h in range(N_HEADS):
                cols = slice(h * GROUP, (h + 1) * GROUP)
                total = lax.dot_general(ones, dsum_ref[:, cols], (((1,), (1,)), ((), ())),
                                        precision=lax.Precision.HIGHEST, preferred_element_type=F32)
                dbs_ref[h:h + 1, :] = total[0:1, :]

    tile = pl.BlockSpec((ts, D), lambda i: (i, 0))
    wide = pl.BlockSpec((ts, PROJ), lambda i: (i, 0))
    small = _full((N_HEADS, GROUP, GROUP))
    heads = _resident((N_HEADS, GROUP, GROUP))
    vec = _full((1, D))
    return pl.pallas_call(
        body, name="odd_backward", grid=(nt,),
        in_specs=[tile, tile, wide, tile, vec, vec, heads, heads, heads, _resident((D, D)), _resident((PROJ, D)),
                  vec, vec],
        out_specs=[wide, tile, _full((D, D)), small, _full((N_HEADS, GROUP)), vec, vec, vec, vec],
        out_shape=[jax.ShapeDtypeStruct((S, PROJ), BF16), jax.ShapeDtypeStruct((S, D), F32),
                   jax.ShapeDtypeStruct((D, D), BF16),
                   jax.ShapeDtypeStruct((N_HEADS, GROUP, GROUP), F32), jax.ShapeDtypeStruct((N_HEADS, GROUP), F32),
                   jax.ShapeDtypeStruct((1, D), F32), jax.ShapeDtypeStruct((1, D), F32),
                   jax.ShapeDtypeStruct((1, D), F32), jax.ShapeDtypeStruct((1, D), F32)],
        scratch_shapes=[pltpu.VMEM((ts, D), BF16), pltpu.VMEM((ts, D), F32), pltpu.VMEM((ts, D), BF16),
                        pltpu.VMEM((ts, D), F32), pltpu.VMEM((GROUP, D), F32), pltpu.VMEM((ts, D), BF16),
                        pltpu.VMEM((ts, D), BF16), pltpu.VMEM((D, D), F32)],
        compiler_params=_params(),
    )(g2, m1, p, x1, lg, lb, wt_b, wtt_b, bfull, woutt_b, wint_b, gpost, gpre)


def _weight_grad_scatter(h, dp, ts, name, sides=()):
    S = h.shape[0]
    ts = min(ts, S)
    nt = S // ts
    panel = 2 * SHARD_IN
    last = N_CHIPS - 1

    def body(h_ref, dp_ref, parts_ref, acc_ref, half_ref, swap_ref, sum_ref, send_sems, recv_sems):
        s, t = pl.program_id(0), pl.program_id(1)
        x, y, c = _mesh_position()
        sibling = (x, y, 1 - c)

        def swap(k):
            return _remote(half_ref.at[k, 1], swap_ref.at[k], send_sems, recv_sems, k, sibling)

        def chip_sum(k):
            flip = last - k
            to = (x ^ (flip >> 1), y ^ (flip & 1), c)
            return _remote(sum_ref.at[k], parts_ref.at[1 + flip], send_sems, recv_sems, 3 + flip, to)

        own = pltpu.make_async_copy(half_ref.at[last, 0], parts_ref.at[0], recv_sems.at[7])
        to_sibling = _remote(half_ref.at[last, 1], parts_ref.at[1], send_sems, recv_sems, 3, sibling)

        @pl.when(t == 0)
        def _():
            acc_ref[...] = jnp.zeros_like(acc_ref)

        for k in range(last):
            @pl.when((s == k + 1) & (t == 0))
            def _():
                swap(k).wait_recv()
                sum_ref[k] = (half_ref[k, 0].astype(F32) + swap_ref[k].astype(F32)).astype(BF16)
                chip_sum(k).start()

        acc_ref[...] += _dot_t0(h_ref[...], dp_ref[...])

        @pl.when(t == nt - 1)
        def _():
            for core in range(2):
                @pl.when(c == core)
                def _():
                    half_ref[s, core] = acc_ref[:, 0:SHARD_IN].astype(BF16)
                    half_ref[s, 1 - core] = acc_ref[:, SHARD_IN:panel].astype(BF16)
            for k in range(last):
                @pl.when(s == k)
                def _():
                    swap(k).start()

        @pl.when((s == last) & (t == nt - 1))
        def _():
            own.start()
            to_sibling.start()
            own.wait()
            _remote(half_ref.at[last, 1], parts_ref.at[1], send_sems, recv_sems, 3, sibling).wait_recv()
            for k in range(last):
                chip_sum(k).wait_recv()
            to_sibling.wait_send()
            for k in range(last):
                swap(k).wait_send()
                chip_sum(k).wait_send()

    chip_panel = lambda s, t: (t, (2 * lax.axis_index("x") + lax.axis_index("y")) ^ (last - s))
    sems = pltpu.SemaphoreType.DMA((8,))
    return _call(
        body, name=name, grid=(N_CHIPS, nt),
        in_specs=[pl.BlockSpec((ts, D), lambda s, t: (t, 0)), pl.BlockSpec((ts, panel), chip_panel)],
        out_specs=[HBM_SPEC], out_shape=[jax.ShapeDtypeStruct((N_CHIPS + 1, D, SHARD_IN), BF16)],
        scratch_shapes=[pltpu.VMEM((D, panel), F32), pltpu.VMEM((N_CHIPS, 2, D, SHARD_IN), BF16),
                        pltpu.VMEM((last, D, SHARD_IN), BF16), pltpu.VMEM((last, D, SHARD_IN), BF16), sems, sems],
        args=(h, dp), sides=sides)


def _adamw(w, g, m, v):
    m = ADAM_B1 * m + (1.0 - ADAM_B1) * g
    v = ADAM_B2 * v + (1.0 - ADAM_B2) * (g * g)
    m_hat = m / (1.0 - ADAM_B1 ** ADAM_STEP)
    v_hat = v / (1.0 - ADAM_B2 ** ADAM_STEP)
    delta = -ADAM_LR * (m_hat / (jnp.sqrt(v_hat) + ADAM_EPS) + ADAM_WD * w)
    return delta, m, v


UPDATE_STEPS = 4


def _sum_update(parts, weights):
    n_w = len(weights)

    def body(*refs):
        p_refs, wmv_refs, outs = refs[:n_w], refs[n_w:4 * n_w], refs[4 * n_w:]
        for i in range(n_w):
            acc = p_refs[i][0].astype(F32)
            for d in range(1, parts[i].shape[0]):
                acc = acc + p_refs[i][d].astype(F32)
            w_ref, m_ref, v_ref = wmv_refs[3 * i:3 * i + 3]
            g_out, d_out, m_out, v_out = outs[4 * i:4 * i + 4]
            g_out[...] = acc
            delta, m_new, v_new = _adamw(w_ref[...], acc, m_ref[...], v_ref[...])
            d_out[...] = delta
            m_out[...] = m_new
            v_out[...] = v_new

    part_specs, wmv_specs, out_specs, out_shape = [], [], [], []
    for p, (w, _, _) in zip(parts, weights):
        R, C = w.shape
        rb = R // UPDATE_STEPS
        blk = pl.BlockSpec((rb, C), lambda i: (i, 0))
        part_specs.append(pl.BlockSpec((p.shape[0], rb, C), lambda i: (0, i, 0)))
        wmv_specs += [blk] * 3
        out_specs += [blk] * 4
        out_shape += [jax.ShapeDtypeStruct((R, C), F32)] * 4
    flat = pl.pallas_call(
        body, name="update_large_weights", grid=(UPDATE_STEPS,), in_specs=part_specs + wmv_specs,
        out_specs=out_specs, out_shape=out_shape, compiler_params=_params(),
    )(*parts, *[a for wmv in weights for a in wmv])
    return [flat[4 * i:4 * i + 4] for i in range(n_w)]


def _small_sum_update(partials, row_of, weights, loss_parts):
    n_p, n_w = len(partials), len(weights)

    def body(*refs):
        p_refs = refs[:n_p]
        loss_ref = refs[n_p]
        w_refs = refs[n_p + 1:n_p + 1 + 3 * n_w]
        loss_out = refs[n_p + 1 + 3 * n_w]
        outs = refs[n_p + 2 + 3 * n_w:]

        def total(ref):
            acc = ref[0].astype(F32)
            for d in range(1, N_DEV):
                acc = acc + ref[d].astype(F32)
            return acc

        lsum = jnp.sum(total(loss_ref), axis=-1, keepdims=True) * (0.5 / D)
        loss_out[...] = jnp.broadcast_to(lsum, loss_out.shape)
        for k in range(n_p):
            i, r0 = row_of[k]
            g = total(p_refs[k])
            w_ref, m_ref, v_ref = w_refs[3 * i:3 * i + 3]
            g_out, d_out, m_out, v_out = outs[4 * i:4 * i + 4]
            if g.ndim == 2:
                at = (slice(r0, r0 + g.shape[0]), slice(None))
            else:
                at = (slice(None),) * g.ndim
            delta, m_new, v_new = _adamw(w_ref[at], g, m_ref[at], v_ref[at])
            g_out[at] = g
            d_out[at] = delta
            m_out[at] = m_new
            v_out[at] = v_new

    vm = pl.BlockSpec(memory_space=pltpu.VMEM)
    flat_w = [a for wmv in weights for a in wmv]
    out_shape = [jax.ShapeDtypeStruct((1, 128), F32)]
    for w, _, _ in weights:
        out_shape += [jax.ShapeDtypeStruct(w.shape, F32)] * 4
    return pl.pallas_call(
        body, name="small_update", in_specs=[vm] * (n_p + 1 + 3 * n_w), out_specs=[vm] * len(out_shape),
        out_shape=out_shape, compiler_params=pltpu.CompilerParams(vmem_limit_bytes=VMEM_LIMIT),
    )(*partials, loss_parts, *flat_w)


def _pack_small_shard(cw, lg, lb):
    return (jnp.pad(cw, ((0, 13), (0, 64))) + jnp.pad(lg, ((8, 7), (0, 0))) + jnp.pad(lb, ((9, 6), (0, 0))))


def kernel(x, pre_norm, post_norm, even_w_in, even_conv_w, even_pool_w, even_pool_scale, even_w_out, odd_w_in, odd_ln_g, odd_ln_b, odd_w_s, odd_b_s, odd_w_out, loss_target, m_pre_norm, m_post_norm, m_even_w_in, m_even_conv_w, m_even_pool_w, m_even_pool_scale, m_even_w_out, m_odd_w_in, m_odd_ln_g, m_odd_ln_b, m_odd_w_s, m_odd_b_s, m_odd_w_out, v_pre_norm, v_post_norm, v_even_w_in, v_even_conv_w, v_even_pool_w, v_even_pool_scale, v_even_w_out, v_odd_w_in, v_odd_ln_g, v_odd_ln_b, v_odd_w_s, v_odd_b_s, v_odd_w_out):
    S = x.shape[1]
    xs = x.reshape(S, D)
    tgt = loss_target.reshape(S, D)

    gpre0, gpre1 = pre_norm[0:1], pre_norm[1:2]
    gpost0, gpost1 = post_norm[0:1], post_norm[1:2]
    small_shard = _pack_small_shard(even_conv_w[0], odd_ln_g, odd_ln_b)
    h0, wie_g, small_g, woe_g, wio_b, woo_b = _prenorm_gather(xs, gpre0, even_w_in[0], small_shard, even_w_out[0],
                                                              odd_w_in[0], odd_w_out[0], TS_PRENORM)
    wie_t = jnp.transpose(wie_g, (0, 2, 1)).reshape(PROJ, D)
    woe = woe_g.reshape(D, D)
    woe_t = woe.T
    conv_w = jnp.transpose(small_g[:, 0:3, 0:64], (1, 0, 2)).reshape(3, A_W)
    ln_g = small_g[:, 8, :].reshape(1, D)
    ln_b = small_g[:, 9, :].reshape(1, D)

    pool_w_b = even_pool_w[0].astype(BF16)
    pool_wt_b = jnp.swapaxes(even_pool_w[0], 1, 2).astype(BF16)
    ws_tril = jnp.tril(odd_w_s[0])
    ws_b = ws_tril.astype(BF16)
    wst_b = jnp.swapaxes(ws_tril, 1, 2).astype(BF16)
    b_full = jnp.broadcast_to(odd_b_s[0][:, :, None], (N_HEADS, GROUP, GROUP))

    p0, m0, x1, h1, wio_g, woo_g = _even_forward_fused(h0, xs, wie_g, conv_w, pool_w_b, even_pool_scale, woe, gpost0,
                                                       gpre1, TS_FWD, sides=[(_Gather, [wio_b, woo_b])])
    wio_t = jnp.transpose(wio_g, (0, 2, 1)).reshape(PROJ, D)
    woo = woo_g.reshape(D, D)
    woo_t = woo.T
    p1, m1, g2, loss_vec = _odd_forward(h1, x1, tgt, wio_g, ln_g, ln_b, ws_b, b_full, woo, gpost1, TS_FWD)

    dp1, g1, dwoo, dws, dbs, dlg, dlb, dgpost1, dgpre1 = _odd_backward(
        g2, m1, p1, x1, ln_g, ln_b, ws_b, wst_b, b_full, woo_t, wio_t, gpost1, gpre1, TS_FWD)
    by_owner = lambda dwo: dwo.reshape(N_DEV, SHARD_OUT, D)
    (dwio_parts, dwoo_parts, dlg_parts, dlb_parts,
     dws_g, dbs_g, gpost1_g, gpre1_g, loss_g) = _weight_grad_scatter(
        h1, dp1, TS_WGRAD, "odd_weight_grad",
        sides=[(_Scatter, [by_owner(dwoo), dlg.reshape(N_DEV, 1, 128), dlb.reshape(N_DEV, 1, 128)]),
               (_Gather, [dws.astype(BF16), dbs, dgpost1, dgpre1, loss_vec])])
    dp0, gx, dwoe, dcw, dpw, dps, dgpost0, dgpre0 = _even_backward_fused(
        g1, m0, p0, xs, conv_w, pool_w_b, pool_wt_b, even_pool_scale, woe_t, wie_t, gpost0, gpre0, TS_FWD)
    dcw_by_owner = jnp.transpose(dcw[0:3].reshape(3, N_DEV, 64), (1, 0, 2))
    dwie_parts, dwoe_parts, dcw_parts, dpw_g, dps_g, gpost0_g, gpre0_g = _weight_grad_scatter(
        h0, dp0, TS_WGRAD, "even_weight_grad",
        sides=[(_Scatter, [by_owner(dwoe), dcw_by_owner]), (_Gather, [dpw.astype(BF16), dps, dgpost0, dgpre0])])

    ((g_wie, d_wie, nm_wie, nv_wie), (g_wio, d_wio, nm_wio, nv_wio), (g_woe, d_woe, nm_woe, nv_woe),
     (g_woo, d_woo, nm_woo, nv_woo)) = _sum_update(
        [dwie_parts, dwio_parts, dwoe_parts, dwoo_parts],
        [(even_w_in[0], m_even_w_in[0], v_even_w_in[0]), (odd_w_in[0], m_odd_w_in[0], v_odd_w_in[0]),
         (even_w_out[0], m_even_w_out[0], v_even_w_out[0]), (odd_w_out[0], m_odd_w_out[0], v_odd_w_out[0])])

    partials = [gpre0_g, gpre1_g, gpost0_g, gpost1_g, dpw_g, dps_g, dws_g, dbs_g, dcw_parts, dlg_parts, dlb_parts]
    row_of = [(0, 0), (0, 1), (1, 0), (1, 1), (2, 0), (3, 0), (4, 0), (5, 0), (6, 0), (7, 0), (8, 0)]
    weights = [(pre_norm, m_pre_norm, v_pre_norm), (post_norm, m_post_norm, v_post_norm),
               (even_pool_w[0], m_even_pool_w[0], v_even_pool_w[0]),
               (even_pool_scale, m_even_pool_scale, v_even_pool_scale),
               (odd_w_s[0], m_odd_w_s[0], v_odd_w_s[0]), (odd_b_s[0], m_odd_b_s[0], v_odd_b_s[0]),
               (even_conv_w[0], m_even_conv_w[0], v_even_conv_w[0]),
               (odd_ln_g, m_odd_ln_g, v_odd_ln_g), (odd_ln_b, m_odd_ln_b, v_odd_ln_b)]
    small = _small_sum_update(partials, row_of, weights, loss_g)
    loss = small[0][0, 0]

    def leaves(k, big):
        pre, post, pw, psc, ws, bs, cw, lg, lb = [small[1 + 4 * i + k] for i in range(len(weights))]
        wie_k, woe_k, wio_k, woo_k = big
        return [pre, post, wie_k[None], cw[None], pw[None], psc, woe_k[None], wio_k[None], lg, lb, ws[None], bs[None],
                woo_k[None]]

    outs = [loss, gx.reshape(1, S, D)]
    outs += leaves(0, (g_wie, g_woe, g_wio, g_woo))
    outs += leaves(1, (d_wie, d_woe, d_wio, d_woo))
    outs += leaves(2, (nm_wie, nm_woe, nm_wio, nm_woo))
    outs += leaves(3, (nv_wie, nv_woe, nv_wio, nv_woo))
    return tuple(outs)
```
